```python
import math
import jax, jax.numpy as jnp
from jax import lax
import numpy as np

D_MODEL = 2048
BATCH = 16
SEQ = 2048
DEPTH = 2

MIX_WIDTH = D_MODEL
HEAD_DIM = 128
ATT_WIDTH = MIX_WIDTH // 2
ATT_HEADS = ATT_WIDTH // HEAD_DIM
DILATED_PATTERN = ((128, 1), (512, 4), (2048, 16))
CONV_CHANNELS = MIX_WIDTH - ATT_WIDTH
CONV_K = 3
IN_WIDTH = 3 * ATT_WIDTH + 3 * CONV_CHANNELS
D_FF = 256 * ((8 * D_MODEL // 3 + 255) // 256)
S5_GROUP = 16
S5_STATE = 64
S5_GROUPS = D_MODEL // S5_GROUP
ROPE_THETA = 10000.0
RMS_EPS = 1e-6
NEG_INF = -1e30
N_AB = (DEPTH + 1) // 2
N_C = DEPTH // 2

kernel_name = 'hybrid_dilated_attn_shortconv_s5_macaron'


def rmsnorm(x, g):
    xf = x.astype(jnp.float32)
    y = xf * lax.rsqrt(jnp.mean(xf * xf, axis=-1, keepdims=True) + RMS_EPS)
    return (y * g.astype(jnp.float32)).astype(x.dtype)


def swiglu(x, w1, w3, w2):
    return (jax.nn.silu(x @ w1) * (x @ w3)) @ w2


def rope_tables(seq):
    half = HEAD_DIM // 2
    inv = ROPE_THETA ** (-jnp.arange(0, half, dtype=jnp.float32) * 2.0 / HEAD_DIM)
    ang = jnp.arange(seq, dtype=jnp.float32)[:, None] * inv[None, :]
    return jnp.cos(ang), jnp.sin(ang)


def apply_rope(t, cos, sin):
    half = HEAD_DIM // 2
    tf = t.astype(jnp.float32)
    t1, t2 = tf[..., :half], tf[..., half:]
    c, s = cos[:, None, :], sin[:, None, :]
    return jnp.concatenate([t1 * c - t2 * s, t2 * c + t1 * s], axis=-1)


def dilated_branch(q, k, v, window, dilation):
    bsz, seq, nh, dh = q.shape
    L = window // dilation
    n = seq // dilation
    nb = -(-n // L)
    pad = nb * L - n

    def to_blocks(t):
        t = t.reshape(bsz, n, dilation, nh, dh).transpose(0, 2, 1, 3, 4)
        t = jnp.pad(t, ((0, 0), (0, 0), (0, pad), (0, 0), (0, 0)))
        return t.reshape(bsz, dilation, nb, L, nh, dh)

    def with_prev(t):
        prev = jnp.pad(t, ((0, 0), (0, 0), (1, 0), (0, 0), (0, 0), (0, 0)))[:, :, :-1]
        return jnp.concatenate([prev, t], axis=3)

    qb = to_blocks(q)
    kw = with_prev(to_blocks(k))
    vw = with_prev(to_blocks(v))
    s = jnp.einsum('bgnqhd,bgnkhd->bghnqk', qb, kw)
    qi = jnp.arange(L)[:, None]
    kj = jnp.arange(2 * L)[None, :]
    dist = qi + L - kj
    band = (dist >= 0) & (dist <= L)
    kpos = jnp.arange(nb)[:, None, None] * L + kj[None] - L
    valid = band[None] & (kpos >= 0)
    s = jnp.where(valid, s, NEG_INF)
    m = jnp.max(s, axis=-1, keepdims=True)
    p = jnp.exp(s - m)
    denom = jnp.sum(p, axis=-1)
    o = jnp.einsum('bghnqk,bgnkhd->bgnqhd', p, vw)
    denom_t = denom.transpose(0, 1, 3, 4, 2)
    o = o / denom_t[..., None]
    lse = m[..., 0].transpose(0, 1, 3, 4, 2) + jnp.log(denom_t)

    def from_blocks(t):
        t = t.reshape((bsz, dilation, nb * L) + t.shape[4:])[:, :, :n]
        t = jnp.moveaxis(t, 1, 2)
        return t.reshape((bsz, seq) + t.shape[3:])

    return from_blocks(o), from_blocks(lse)


def mixer_attn_conv(h, w_in, conv_w, w_out, cos, sin):
    bsz, seq, _ = h.shape
    proj = h @ w_in
    a = ATT_WIDTH
    c = CONV_CHANNELS
    q, k, v, gate_b, gate_c, x_in = jnp.split(
        proj, [a, 2 * a, 3 * a, 3 * a + c, 3 * a + 2 * c], axis=-1)
    q = apply_rope(q.reshape(bsz, seq, ATT_HEADS, HEAD_DIM), cos, sin) * (HEAD_DIM ** -0.5)
    k = apply_rope(k.reshape(bsz, seq, ATT_HEADS, HEAD_DIM), cos, sin)
    v = v.reshape(bsz, seq, ATT_HEADS, HEAD_DIM).astype(jnp.float32)
    outs, lses = [], []
    for window, dilation in DILATED_PATTERN:
        o_i, l_i = dilated_branch(q, k, v, window, dilation)
        outs.append(o_i)
        lses.append(l_i)
    wts = jax.nn.softmax(jnp.stack(lses, axis=0), axis=0)
    att = jnp.einsum('nbsh,nbshd->bshd', wts, jnp.stack(outs, axis=0))
    att = att.reshape(bsz, seq, ATT_WIDTH).astype(h.dtype)
    u = gate_c * x_in
    conv = lax.conv_general_dilated(
        u, conv_w[:, None, :], window_strides=(1,), padding=[(CONV_K - 1, 0)],
        dimension_numbers=('NWC', 'WIO', 'NWC'), feature_group_count=CONV_CHANNELS)
    sc = gate_b * conv
    return jnp.concatenate([att, sc], axis=-1) @ w_out


def mixer_s5(u, lam_re, lam_im, log_dt, b_re, b_im, c_re, c_im, d_skip, w_a, w_b):
    bsz, seq, dm = u.shape
    uf = u.astype(jnp.float32).reshape(bsz, seq, S5_GROUPS, S5_GROUP)
    lr = lam_re.astype(jnp.float32)
    li = lam_im.astype(jnp.float32)
    dt = jnp.exp(log_dt.astype(jnp.float32))[:, None]
    mag = jnp.exp(lr * dt)
    ar = mag * jnp.cos(li * dt)
    ai = mag * jnp.sin(li * dt)
    den = lr * lr + li * li
    fr = ((ar - 1.0) * lr + ai * li) / den
    fi = (ai * lr - (ar - 1.0) * li) / den
    br = b_re.astype(jnp.float32)
    bi = b_im.astype(jnp.float32)
    bbar_re = fr[..., None] * br - fi[..., None] * bi
    bbar_im = fr[..., None] * bi + fi[..., None] * br
    bu_re = jnp.einsum('bsgc,gpc->bsgp', uf, bbar_re)
    bu_im = jnp.einsum('bsgc,gpc->bsgp', uf, bbar_im)
    a_re = jnp.broadcast_to(ar[None, None], (1, seq, S5_GROUPS, S5_STATE))
    a_im = jnp.broadcast_to(ai[None, None], (1, seq, S5_GROUPS, S5_STATE))

    def combine(e1, e2):
        a1r, a1i, b1r, b1i = e1
        a2r, a2i, b2r, b2i = e2
        return (a2r * a1r - a2i * a1i,
                a2r * a1i + a2i * a1r,
                a2r * b1r - a2i * b1i + b2r,
                a2r * b1i + a2i * b1r + b2i)

    _, _, s_re, s_im = lax.associative_scan(combine, (a_re, a_im, bu_re, bu_im), axis=1)
    y = (jnp.einsum('bsgp,gcp->bsgc', s_re, c_re.astype(jnp.float32))
         - jnp.einsum('bsgp,gcp->bsgc', s_im, c_im.astype(jnp.float32)))
    y = y + d_skip.astype(jnp.float32).reshape(S5_GROUPS, S5_GROUP) * uf
    y = jax.nn.gelu(y.reshape(bsz, seq, dm)).astype(u.dtype)
    return (y @ w_a) * jax.nn.sigmoid(y @ w_b)


def _fwd_setup_inputs(seed: int = 0) -> dict:
    key = jax.random.key(seed)
    ks = jax.random.split(key, 24)
    f32 = jnp.float32
    nrm = lambda k, shape, scale: jax.random.normal(k, shape, f32) * scale
    x = jax.random.normal(ks[0], (BATCH, SEQ, D_MODEL), f32)
    ln_ffn_pre = 1.0 + nrm(ks[1], (DEPTH, D_MODEL), 0.01)
    ln_mix = 1.0 + nrm(ks[2], (DEPTH, D_MODEL), 0.01)
    ln_ffn_post = 1.0 + nrm(ks[3], (DEPTH, D_MODEL), 0.01)
    ln_final = 1.0 + nrm(ks[4], (D_MODEL,), 0.01)
    ffn_w1 = nrm(ks[5], (DEPTH, 2, D_MODEL, D_FF), D_MODEL ** -0.5)
    ffn_w3 = nrm(ks[6], (DEPTH, 2, D_MODEL, D_FF), D_MODEL ** -0.5)
    ffn_w2 = nrm(ks[7], (DEPTH, 2, D_FF, D_MODEL), D_FF ** -0.5)
    ab_w_in = nrm(ks[8], (N_AB, D_MODEL, IN_WIDTH), D_MODEL ** -0.5)
    ab_conv_w = nrm(ks[9], (N_AB, CONV_K, CONV_CHANNELS), CONV_K ** -0.5)
    ab_w_out = nrm(ks[10], (N_AB, MIX_WIDTH, D_MODEL), MIX_WIDTH ** -0.5)
    s5_lambda_re = -0.5 + nrm(ks[11], (N_C, S5_GROUPS, S5_STATE), 0.01)
    s5_lambda_im = (math.pi * jnp.arange(S5_STATE, dtype=f32))[None, None, :] + nrm(
        ks[12], (N_C, S5_GROUPS, S5_STATE), 0.01)
    s5_log_dt = jax.random.uniform(ks[13], (N_C, S5_GROUPS), f32,
                                   minval=math.log(1e-3), maxval=math.log(1e-1))
    s5_b_re = nrm(ks[14], (N_C, S5_GROUPS, S5_STATE, S5_GROUP), (2 * S5_GROUP) ** -0.5)
    s5_b_im = nrm(ks[15], (N_C, S5_GROUPS, S5_STATE, S5_GROUP), (2 * S5_GROUP) ** -0.5)
    s5_c_re = nrm(ks[16], (N_C, S5_GROUPS, S5_GROUP, S5_STATE), (2 * S5_STATE) ** -0.5)
    s5_c_im = nrm(ks[17], (N_C, S5_GROUPS, S5_GROUP, S5_STATE), (2 * S5_STATE) ** -0.5)
    s5_d = nrm(ks[18], (N_C, D_MODEL), 1.0)
    s5_glu_wa = nrm(ks[19], (N_C, D_MODEL, D_MODEL), D_MODEL ** -0.5)
    s5_glu_wb = nrm(ks[20], (N_C, D_MODEL, D_MODEL), D_MODEL ** -0.5)
    return {'x': x, 'ln_ffn_pre': ln_ffn_pre, 'ln_mix': ln_mix, 'ln_ffn_post': ln_ffn_post,
            'ln_final': ln_final, 'ffn_w1': ffn_w1, 'ffn_w3': ffn_w3, 'ffn_w2': ffn_w2,
            'ab_w_in': ab_w_in, 'ab_conv_w': ab_conv_w, 'ab_w_out': ab_w_out,
            's5_lambda_re': s5_lambda_re, 's5_lambda_im': s5_lambda_im, 's5_log_dt': s5_log_dt,
            's5_b_re': s5_b_re, 's5_b_im': s5_b_im, 's5_c_re': s5_c_re, 's5_c_im': s5_c_im,
            's5_d': s5_d, 's5_glu_wa': s5_glu_wa, 's5_glu_wb': s5_glu_wb}


def _fwd_reference(x, ln_ffn_pre, ln_mix, ln_ffn_post, ln_final, ffn_w1, ffn_w3, ffn_w2,
              ab_w_in, ab_conv_w, ab_w_out, s5_lambda_re, s5_lambda_im, s5_log_dt,
              s5_b_re, s5_b_im, s5_c_re, s5_c_im, s5_d, s5_glu_wa, s5_glu_wb):
    cos, sin = rope_tables(x.shape[1])
    h = x
    for i in range(DEPTH):
        h = h + 0.5 * swiglu(rmsnorm(h, ln_ffn_pre[i]), ffn_w1[i, 0], ffn_w3[i, 0], ffn_w2[i, 0])
        u = rmsnorm(h, ln_mix[i])
        j = i // 2
        if i % 2 == 0:
            h = h + mixer_attn_conv(u, ab_w_in[j], ab_conv_w[j], ab_w_out[j], cos, sin)
        else:
            h = h + mixer_s5(u, s5_lambda_re[j], s5_lambda_im[j], s5_log_dt[j],
                             s5_b_re[j], s5_b_im[j], s5_c_re[j], s5_c_im[j],
                             s5_d[j], s5_glu_wa[j], s5_glu_wb[j])
        h = h + 0.5 * swiglu(rmsnorm(h, ln_ffn_post[i]), ffn_w1[i, 1], ffn_w3[i, 1], ffn_w2[i, 1])
    return rmsnorm(h, ln_final)


import jax as _jax
import jax.numpy as _jnp

TWIN_FORMAT = 'train_step'
FWD_PARAMS = ['x', 'ln_ffn_pre', 'ln_mix', 'ln_ffn_post', 'ln_final', 'ffn_w1', 'ffn_w3', 'ffn_w2', 'ab_w_in', 'ab_conv_w', 'ab_w_out', 's5_lambda_re', 's5_lambda_im', 's5_log_dt', 's5_b_re', 's5_b_im', 's5_c_re', 's5_c_im', 's5_d', 's5_glu_wa', 's5_glu_wb']
TWIN_WEIGHTS = ['ln_ffn_pre', 'ln_mix', 'ln_ffn_post', 'ln_final', 'ffn_w1', 'ffn_w3', 'ffn_w2', 'ab_w_in', 'ab_conv_w', 'ab_w_out', 's5_lambda_re', 's5_lambda_im', 's5_log_dt', 's5_b_re', 's5_b_im', 's5_c_re', 's5_c_im', 's5_d', 's5_glu_wa', 's5_glu_wb']
TWIN_DIFF_INPUT = 'x'
TWIN_INPUTS = ['x', 'ln_ffn_pre', 'ln_mix', 'ln_ffn_post', 'ln_final', 'ffn_w1', 'ffn_w3', 'ffn_w2', 'ab_w_in', 'ab_conv_w', 'ab_w_out', 's5_lambda_re', 's5_lambda_im', 's5_log_dt', 's5_b_re', 's5_b_im', 's5_c_re', 's5_c_im', 's5_d', 's5_glu_wa', 's5_glu_wb', 'loss_target', 'm_ln_ffn_pre', 'm_ln_mix', 'm_ln_ffn_post', 'm_ln_final', 'm_ffn_w1', 'm_ffn_w3', 'm_ffn_w2', 'm_ab_w_in', 'm_ab_conv_w', 'm_ab_w_out', 'm_s5_lambda_re', 'm_s5_lambda_im', 'm_s5_log_dt', 'm_s5_b_re', 'm_s5_b_im', 'm_s5_c_re', 'm_s5_c_im', 'm_s5_d', 'm_s5_glu_wa', 'm_s5_glu_wb', 'v_ln_ffn_pre', 'v_ln_mix', 'v_ln_ffn_post', 'v_ln_final', 'v_ffn_w1', 'v_ffn_w3', 'v_ffn_w2', 'v_ab_w_in', 'v_ab_conv_w', 'v_ab_w_out', 'v_s5_lambda_re', 'v_s5_lambda_im', 'v_s5_log_dt', 'v_s5_b_re', 'v_s5_b_im', 'v_s5_c_re', 'v_s5_c_im', 'v_s5_d', 'v_s5_glu_wa', 'v_s5_glu_wb']
TWIN_OUTPUTS = ['loss', 'grad_x', 'grad_ln_ffn_pre', 'grad_ln_mix', 'grad_ln_ffn_post', 'grad_ln_final', 'grad_ffn_w1', 'grad_ffn_w3', 'grad_ffn_w2', 'grad_ab_w_in', 'grad_ab_conv_w', 'grad_ab_w_out', 'grad_s5_lambda_re', 'grad_s5_lambda_im', 'grad_s5_log_dt', 'grad_s5_b_re', 'grad_s5_b_im', 'grad_s5_c_re', 'grad_s5_c_im', 'grad_s5_d', 'grad_s5_glu_wa', 'grad_s5_glu_wb', 'delta_ln_ffn_pre', 'delta_ln_mix', 'delta_ln_ffn_post', 'delta_ln_final', 'delta_ffn_w1', 'delta_ffn_w3', 'delta_ffn_w2', 'delta_ab_w_in', 'delta_ab_conv_w', 'delta_ab_w_out', 'delta_s5_lambda_re', 'delta_s5_lambda_im', 'delta_s5_log_dt', 'delta_s5_b_re', 'delta_s5_b_im', 'delta_s5_c_re', 'delta_s5_c_im', 'delta_s5_d', 'delta_s5_glu_wa', 'delta_s5_glu_wb', 'new_m_ln_ffn_pre', 'new_m_ln_mix', 'new_m_ln_ffn_post', 'new_m_ln_final', 'new_m_ffn_w1', 'new_m_ffn_w3', 'new_m_ffn_w2', 'new_m_ab_w_in', 'new_m_ab_conv_w', 'new_m_ab_w_out', 'new_m_s5_lambda_re', 'new_m_s5_lambda_im', 'new_m_s5_log_dt', 'new_m_s5_b_re', 'new_m_s5_b_im', 'new_m_s5_c_re', 'new_m_s5_c_im', 'new_m_s5_d', 'new_m_s5_glu_wa', 'new_m_s5_glu_wb', 'new_v_ln_ffn_pre', 'new_v_ln_mix', 'new_v_ln_ffn_post', 'new_v_ln_final', 'new_v_ffn_w1', 'new_v_ffn_w3', 'new_v_ffn_w2', 'new_v_ab_w_in', 'new_v_ab_conv_w', 'new_v_ab_w_out', 'new_v_s5_lambda_re', 'new_v_s5_lambda_im', 'new_v_s5_log_dt', 'new_v_s5_b_re', 'new_v_s5_b_im', 'new_v_s5_c_re', 'new_v_s5_c_im', 'new_v_s5_d', 'new_v_s5_glu_wa', 'new_v_s5_glu_wb']
TWIN_LEAF_KINDS = {'loss': 'loss', 'grad_x': 'grad_x', 'grad_ln_ffn_pre': 'grad_w', 'grad_ln_mix': 'grad_w', 'grad_ln_ffn_post': 'grad_w', 'grad_ln_final': 'grad_w', 'grad_ffn_w1': 'grad_w', 'grad_ffn_w3': 'grad_w', 'grad_ffn_w2': 'grad_w', 'grad_ab_w_in': 'grad_w', 'grad_ab_conv_w': 'grad_w', 'grad_ab_w_out': 'grad_w', 'grad_s5_lambda_re': 'grad_w', 'grad_s5_lambda_im': 'grad_w', 'grad_s5_log_dt': 'grad_w', 'grad_s5_b_re': 'grad_w', 'grad_s5_b_im': 'grad_w', 'grad_s5_c_re': 'grad_w', 'grad_s5_c_im': 'grad_w', 'grad_s5_d': 'grad_w', 'grad_s5_glu_wa': 'grad_w', 'grad_s5_glu_wb': 'grad_w', 'delta_ln_ffn_pre': 'delta_w', 'delta_ln_mix': 'delta_w', 'delta_ln_ffn_post': 'delta_w', 'delta_ln_final': 'delta_w', 'delta_ffn_w1': 'delta_w', 'delta_ffn_w3': 'delta_w', 'delta_ffn_w2': 'delta_w', 'delta_ab_w_in': 'delta_w', 'delta_ab_conv_w': 'delta_w', 'delta_ab_w_out': 'delta_w', 'delta_s5_lambda_re': 'delta_w', 'delta_s5_lambda_im': 'delta_w', 'delta_s5_log_dt': 'delta_w', 'delta_s5_b_re': 'delta_w', 'delta_s5_b_im': 'delta_w', 'delta_s5_c_re': 'delta_w', 'delta_s5_c_im': 'delta_w', 'delta_s5_d': 'delta_w', 'delta_s5_glu_wa': 'delta_w', 'delta_s5_glu_wb': 'delta_w', 'new_m_ln_ffn_pre': 'new_m', 'new_m_ln_mix': 'new_m', 'new_m_ln_ffn_post': 'new_m', 'new_m_ln_final': 'new_m', 'new_m_ffn_w1': 'new_m', 'new_m_ffn_w3': 'new_m', 'new_m_ffn_w2': 'new_m', 'new_m_ab_w_in': 'new_m', 'new_m_ab_conv_w': 'new_m', 'new_m_ab_w_out': 'new_m', 'new_m_s5_lambda_re': 'new_m', 'new_m_s5_lambda_im': 'new_m', 'new_m_s5_log_dt': 'new_m', 'new_m_s5_b_re': 'new_m', 'new_m_s5_b_im': 'new_m', 'new_m_s5_c_re': 'new_m', 'new_m_s5_c_im': 'new_m', 'new_m_s5_d': 'new_m', 'new_m_s5_glu_wa': 'new_m', 'new_m_s5_glu_wb': 'new_m', 'new_v_ln_ffn_pre': 'new_v', 'new_v_ln_mix': 'new_v', 'new_v_ln_ffn_post': 'new_v', 'new_v_ln_final': 'new_v', 'new_v_ffn_w1': 'new_v', 'new_v_ffn_w3': 'new_v', 'new_v_ffn_w2': 'new_v', 'new_v_ab_w_in': 'new_v', 'new_v_ab_conv_w': 'new_v', 'new_v_ab_w_out': 'new_v', 'new_v_s5_lambda_re': 'new_v', 'new_v_s5_lambda_im': 'new_v', 'new_v_s5_log_dt': 'new_v', 'new_v_s5_b_re': 'new_v', 'new_v_s5_b_im': 'new_v', 'new_v_s5_c_re': 'new_v', 'new_v_s5_c_im': 'new_v', 'new_v_s5_d': 'new_v', 'new_v_s5_glu_wa': 'new_v', 'new_v_s5_glu_wb': 'new_v'}


def _forward(args):
    return _fwd_reference(*[args[k] for k in FWD_PARAMS])


def _output_shape():
    out = _jax.eval_shape(lambda: _forward(_fwd_setup_inputs(0)))
    return out.shape, out.dtype

N_MICROBATCH = 1
ADAM_LR = 0.001
ADAM_B1 = 0.9
ADAM_B2 = 0.999
ADAM_EPS = 1e-08
ADAM_WD = 0.01
ADAM_STEP = 10
PER_EXAMPLE_BATCH_AXIS = {'x': 0, 'loss_target': 0}
SHARED_INPUTS = []
_WEIGHT_DTYPES = {'ln_ffn_pre': _jnp.float32, 'ln_mix': _jnp.float32, 'ln_ffn_post': _jnp.float32, 'ln_final': _jnp.float32, 'ffn_w1': _jnp.float32, 'ffn_w3': _jnp.float32, 'ffn_w2': _jnp.float32, 'ab_w_in': _jnp.float32, 'ab_conv_w': _jnp.float32, 'ab_w_out': _jnp.float32, 's5_lambda_re': _jnp.float32, 's5_lambda_im': _jnp.float32, 's5_log_dt': _jnp.float32, 's5_b_re': _jnp.float32, 's5_b_im': _jnp.float32, 's5_c_re': _jnp.float32, 's5_c_im': _jnp.float32, 's5_d': _jnp.float32, 's5_glu_wa': _jnp.float32, 's5_glu_wb': _jnp.float32}
MOMENT_SCALE = {'ln_ffn_pre': 4.134459e-02, 'ln_mix': 7.401631e-02, 'ln_ffn_post': 2.932214e-02, 'ln_final': 1.598353e+01, 'ffn_w1': 1.545100e-02, 'ffn_w3': 1.499794e-02, 'ffn_w2': 2.483181e-02, 'ab_w_in': 5.447171e-02, 'ab_conv_w': 7.685367e-02, 'ab_w_out': 5.456247e-02, 's5_lambda_re': 1.318761e-03, 's5_lambda_im': 1.279046e-03, 's5_log_dt': 8.396292e-01, 's5_b_re': 8.360263e-04, 's5_b_im': 8.289380e-04, 's5_c_re': 1.640925e-03, 's5_c_im': 1.664434e-03, 's5_d': 2.784926e-02, 's5_glu_wa': 2.380991e-02, 's5_glu_wb': 7.019518e-03}


def _to_microbatches(a, axis):
    t = _jnp.moveaxis(a, axis, 0)
    t = t.reshape((N_MICROBATCH, t.shape[0] // N_MICROBATCH) + t.shape[1:])
    return _jnp.moveaxis(t, 1, axis + 1)


def setup_inputs(seed: int = 0) -> dict:
    inp = _fwd_setup_inputs(seed)
    key = _jax.random.fold_in(_jax.random.key(seed), 7919)
    shape, _ = _output_shape()
    out = dict(inp)
    out["loss_target"] = _jax.random.normal(_jax.random.fold_in(key, 0), shape, _jnp.float32)
    for i, name in enumerate(TWIN_WEIGHTS):
        w = inp[name].astype(_jnp.float32)
        if MOMENT_SCALE is None:
            s = _jnp.sqrt(_jnp.mean(_jnp.square(w)) + 1e-30)
        else:
            s = MOMENT_SCALE[name]
        km, kv = _jax.random.split(_jax.random.fold_in(key, i + 1))
        out[name] = w
        out["m_" + name] = s * _jax.random.normal(km, w.shape, _jnp.float32)
        out["v_" + name] = (s * s) * _jax.random.uniform(kv, w.shape, _jnp.float32, 0.5, 1.5)
    if N_MICROBATCH > 1:
        for name, axis in PER_EXAMPLE_BATCH_AXIS.items():
            out[name] = _to_microbatches(out[name], axis)
    return {'x': out['x'], 'ln_ffn_pre': out['ln_ffn_pre'], 'ln_mix': out['ln_mix'], 'ln_ffn_post': out['ln_ffn_post'], 'ln_final': out['ln_final'], 'ffn_w1': out['ffn_w1'], 'ffn_w3': out['ffn_w3'], 'ffn_w2': out['ffn_w2'], 'ab_w_in': out['ab_w_in'], 'ab_conv_w': out['ab_conv_w'], 'ab_w_out': out['ab_w_out'], 's5_lambda_re': out['s5_lambda_re'], 's5_lambda_im': out['s5_lambda_im'], 's5_log_dt': out['s5_log_dt'], 's5_b_re': out['s5_b_re'], 's5_b_im': out['s5_b_im'], 's5_c_re': out['s5_c_re'], 's5_c_im': out['s5_c_im'], 's5_d': out['s5_d'], 's5_glu_wa': out['s5_glu_wa'], 's5_glu_wb': out['s5_glu_wb'], 'loss_target': out['loss_target'], 'm_ln_ffn_pre': out['m_ln_ffn_pre'], 'm_ln_mix': out['m_ln_mix'], 'm_ln_ffn_post': out['m_ln_ffn_post'], 'm_ln_final': out['m_ln_final'], 'm_ffn_w1': out['m_ffn_w1'], 'm_ffn_w3': out['m_ffn_w3'], 'm_ffn_w2': out['m_ffn_w2'], 'm_ab_w_in': out['m_ab_w_in'], 'm_ab_conv_w': out['m_ab_conv_w'], 'm_ab_w_out': out['m_ab_w_out'], 'm_s5_lambda_re': out['m_s5_lambda_re'], 'm_s5_lambda_im': out['m_s5_lambda_im'], 'm_s5_log_dt': out['m_s5_log_dt'], 'm_s5_b_re': out['m_s5_b_re'], 'm_s5_b_im': out['m_s5_b_im'], 'm_s5_c_re': out['m_s5_c_re'], 'm_s5_c_im': out['m_s5_c_im'], 'm_s5_d': out['m_s5_d'], 'm_s5_glu_wa': out['m_s5_glu_wa'], 'm_s5_glu_wb': out['m_s5_glu_wb'], 'v_ln_ffn_pre': out['v_ln_ffn_pre'], 'v_ln_mix': out['v_ln_mix'], 'v_ln_ffn_post': out['v_ln_ffn_post'], 'v_ln_final': out['v_ln_final'], 'v_ffn_w1': out['v_ffn_w1'], 'v_ffn_w3': out['v_ffn_w3'], 'v_ffn_w2': out['v_ffn_w2'], 'v_ab_w_in': out['v_ab_w_in'], 'v_ab_conv_w': out['v_ab_conv_w'], 'v_ab_w_out': out['v_ab_w_out'], 'v_s5_lambda_re': out['v_s5_lambda_re'], 'v_s5_lambda_im': out['v_s5_lambda_im'], 'v_s5_log_dt': out['v_s5_log_dt'], 'v_s5_b_re': out['v_s5_b_re'], 'v_s5_b_im': out['v_s5_b_im'], 'v_s5_c_re': out['v_s5_c_re'], 'v_s5_c_im': out['v_s5_c_im'], 'v_s5_d': out['v_s5_d'], 'v_s5_glu_wa': out['v_s5_glu_wa'], 'v_s5_glu_wb': out['v_s5_glu_wb']}


def _loss(weights, diff, rest, loss_target):
    with _jax.named_scope("forward"):
        args = {**rest, TWIN_DIFF_INPUT: diff, **{k: w.astype(_WEIGHT_DTYPES[k]) for k, w in weights.items()}}
        y = _forward(args)
    with _jax.named_scope("loss_head"):
        err = _jnp.square(y.astype(_jnp.float32) - loss_target)
        return 0.5 * _jnp.sum(_jnp.mean(err, axis=-1)) if err.ndim else 0.5 * err


def _adamw(w, g, m, v):
    m = ADAM_B1 * m + (1.0 - ADAM_B1) * g
    v = ADAM_B2 * v + (1.0 - ADAM_B2) * _jnp.square(g)
    m_hat = m / (1.0 - ADAM_B1 ** ADAM_STEP)
    v_hat = v / (1.0 - ADAM_B2 ** ADAM_STEP)
    delta = -ADAM_LR * (m_hat / (_jnp.sqrt(v_hat) + ADAM_EPS) + ADAM_WD * w)
    return delta, m, v


def reference(x, ln_ffn_pre, ln_mix, ln_ffn_post, ln_final, ffn_w1, ffn_w3, ffn_w2, ab_w_in, ab_conv_w, ab_w_out, s5_lambda_re, s5_lambda_im, s5_log_dt, s5_b_re, s5_b_im, s5_c_re, s5_c_im, s5_d, s5_glu_wa, s5_glu_wb, loss_target, m_ln_ffn_pre, m_ln_mix, m_ln_ffn_post, m_ln_final, m_ffn_w1, m_ffn_w3, m_ffn_w2, m_ab_w_in, m_ab_conv_w, m_ab_w_out, m_s5_lambda_re, m_s5_lambda_im, m_s5_log_dt, m_s5_b_re, m_s5_b_im, m_s5_c_re, m_s5_c_im, m_s5_d, m_s5_glu_wa, m_s5_glu_wb, v_ln_ffn_pre, v_ln_mix, v_ln_ffn_post, v_ln_final, v_ffn_w1, v_ffn_w3, v_ffn_w2, v_ab_w_in, v_ab_conv_w, v_ab_w_out, v_s5_lambda_re, v_s5_lambda_im, v_s5_log_dt, v_s5_b_re, v_s5_b_im, v_s5_c_re, v_s5_c_im, v_s5_d, v_s5_glu_wa, v_s5_glu_wb):
    given = dict(x=x, ln_ffn_pre=ln_ffn_pre, ln_mix=ln_mix, ln_ffn_post=ln_ffn_post, ln_final=ln_final, ffn_w1=ffn_w1, ffn_w3=ffn_w3, ffn_w2=ffn_w2, ab_w_in=ab_w_in, ab_conv_w=ab_conv_w, ab_w_out=ab_w_out, s5_lambda_re=s5_lambda_re, s5_lambda_im=s5_lambda_im, s5_log_dt=s5_log_dt, s5_b_re=s5_b_re, s5_b_im=s5_b_im, s5_c_re=s5_c_re, s5_c_im=s5_c_im, s5_d=s5_d, s5_glu_wa=s5_glu_wa, s5_glu_wb=s5_glu_wb, loss_target=loss_target, m_ln_ffn_pre=m_ln_ffn_pre, m_ln_mix=m_ln_mix, m_ln_ffn_post=m_ln_ffn_post, m_ln_final=m_ln_final, m_ffn_w1=m_ffn_w1, m_ffn_w3=m_ffn_w3, m_ffn_w2=m_ffn_w2, m_ab_w_in=m_ab_w_in, m_ab_conv_w=m_ab_conv_w, m_ab_w_out=m_ab_w_out, m_s5_lambda_re=m_s5_lambda_re, m_s5_lambda_im=m_s5_lambda_im, m_s5_log_dt=m_s5_log_dt, m_s5_b_re=m_s5_b_re, m_s5_b_im=m_s5_b_im, m_s5_c_re=m_s5_c_re, m_s5_c_im=m_s5_c_im, m_s5_d=m_s5_d, m_s5_glu_wa=m_s5_glu_wa, m_s5_glu_wb=m_s5_glu_wb, v_ln_ffn_pre=v_ln_ffn_pre, v_ln_mix=v_ln_mix, v_ln_ffn_post=v_ln_ffn_post, v_ln_final=v_ln_final, v_ffn_w1=v_ffn_w1, v_ffn_w3=v_ffn_w3, v_ffn_w2=v_ffn_w2, v_ab_w_in=v_ab_w_in, v_ab_conv_w=v_ab_conv_w, v_ab_w_out=v_ab_w_out, v_s5_lambda_re=v_s5_lambda_re, v_s5_lambda_im=v_s5_lambda_im, v_s5_log_dt=v_s5_log_dt, v_s5_b_re=v_s5_b_re, v_s5_b_im=v_s5_b_im, v_s5_c_re=v_s5_c_re, v_s5_c_im=v_s5_c_im, v_s5_d=v_s5_d, v_s5_glu_wa=v_s5_glu_wa, v_s5_glu_wb=v_s5_glu_wb)
    weights = {n: given[n] for n in TWIN_WEIGHTS}
    shared = {n: given[n] for n in SHARED_INPUTS}
    per_example = {n: given[n] for n in ['x']}
    grad_fn = _jax.value_and_grad(_loss, argnums=(0, 1))

    def one_microbatch(ex, loss_target):
        ex = dict(ex)
        diff = ex.pop(TWIN_DIFF_INPUT)
        return grad_fn(weights, diff, {**shared, **ex}, loss_target)

    if N_MICROBATCH == 1:
        loss, (grad_w, grad_x) = one_microbatch(per_example, given["loss_target"])
    else:
        def body(carry, xs):
            loss_sum, grad_sum = carry
            l_k, (gw_k, gx_k) = one_microbatch(xs[0], xs[1])
            with _jax.named_scope("update"):
                return (loss_sum + l_k, _jax.tree.map(_jnp.add, grad_sum, gw_k)), gx_k

        init = (_jnp.zeros((), _jnp.float32), _jax.tree.map(_jnp.zeros_like, weights))
        (loss, grad_w), grad_x = _jax.lax.scan(body, init, (per_example, given["loss_target"]))
    with _jax.named_scope("update"):
        delta_w, new_m, new_v = {}, {}, {}
        for n in TWIN_WEIGHTS:
            delta_w[n], new_m[n], new_v[n] = _adamw(weights[n], grad_w[n], given["m_" + n], given["v_" + n])
    return (loss, grad_x, *[grad_w[n] for n in TWIN_WEIGHTS], *[delta_w[n] for n in TWIN_WEIGHTS],
            *[new_m[n] for n in TWIN_WEIGHTS], *[new_v[n] for n in TWIN_WEIGHTS])
```

```python
import jax
import jax.numpy as jnp
from jax import lax
from jax.experimental import pallas as pl
from jax.experimental.pallas import tpu as pltpu

F32, BF = jnp.float32, jnp.bfloat16
N_DEV = 8
MESH = pl.DeviceIdType.MESH
LANES = 128
SUBLANES = 8
VMEM_LIMIT = 56 * 2 ** 20
ROW_TILE = 512
ELEMS_PER_BLOCK = 256 * 1024
RMS_EPS = 1e-6
ROPE_THETA = 10000.0
NEG_INF = -1e30
S5_STATE = 64
S5_GROUP = 16
GROUPS_PER_BLOCK = LANES // S5_GROUP
STATE_COLS = GROUPS_PER_BLOCK * S5_STATE
DILATED_PATTERN = ((128, 1), (512, 4), (2048, 16))
ADAM_LR, ADAM_B1, ADAM_B2, ADAM_EPS, ADAM_WD, ADAM_STEP = 0.001, 0.9, 0.999, 1e-08, 0.01, 10
GELU_C = 0.7978845608028654
GELU_A = 0.044715


def _cparams(n_grid, vmem=VMEM_LIMIT):
    sem = ("arbitrary",) * n_grid if n_grid else None
    return pltpu.CompilerParams(dimension_semantics=sem, vmem_limit_bytes=vmem)


def _sig(x):
    return 1.0 / (1.0 + jnp.exp(-x))


def _gelu(x):
    return 0.5 * x * (1.0 + jnp.tanh(GELU_C * (x + GELU_A * x * x * x)))


def _gelu_grad(x):
    t = jnp.tanh(GELU_C * (x + GELU_A * x * x * x))
    return 0.5 * (1.0 + t) + 0.5 * x * (1.0 - t * t) * GELU_C * (1.0 + 3.0 * GELU_A * x * x)


def _dot(a, b, dims):
    a = a if a.dtype == BF else a.astype(BF)
    b = b if b.dtype == BF else b.astype(BF)
    return lax.dot_general(a, b, (dims, ((), ())), preferred_element_type=F32)


NN = ((1,), (0,))
NT = ((1,), (1,))
TN = ((0,), (0,))


def _row_block(rows, cols, mult=16):
    cap = max(mult, ELEMS_PER_BLOCK // cols)
    best = None
    for b in range(mult, min(rows, cap) + 1, mult):
        if rows % b == 0:
            best = b
    return rows if best is None else best


def _mm(name, grid, operands, pairs, n_acc, acc_shape, extras, outs, epilogue, aliases=()):
    nk = grid[2]
    n_op, n_ex, n_al, n_out = len(operands), len(extras), len(aliases), len(outs)

    def body(*refs):
        op = refs[:n_op]
        ex = refs[n_op:n_op + n_ex]
        out = refs[n_op + n_ex + n_al:n_op + n_ex + n_al + n_out]
        acc = refs[n_op + n_ex + n_al + n_out:]
        parts = [None] * n_acc
        for ai, bi, dims, ci in pairs:
            d = _dot(op[ai][...], op[bi][...], dims)
            parts[ci] = d if parts[ci] is None else parts[ci] + d
        if nk == 1:
            epilogue(parts, ex, out)
        else:
            k = pl.program_id(2)

            @pl.when(k == 0)
            def _():
                for ci in range(n_acc):
                    acc[ci][...] = parts[ci]

            @pl.when(k > 0)
            def _():
                for ci in range(n_acc):
                    acc[ci][...] += parts[ci]

            @pl.when(k == nk - 1)
            def _():
                epilogue([r[...] for r in acc], ex, out)

    in_specs = [s for _, s in operands] + [s for _, s in extras]
    in_specs += [pl.BlockSpec(memory_space=pl.ANY)] * n_al
    args = [a for a, _ in operands] + [a for a, _ in extras] + [a for a, _ in aliases]
    io_alias = {n_op + n_ex + q: oi for q, (_, oi) in enumerate(aliases)}
    return pl.pallas_call(
        body, grid=grid, in_specs=in_specs, out_specs=[s for _, s in outs],
        out_shape=[sh for sh, _ in outs],
        scratch_shapes=[pltpu.VMEM(acc_shape, F32) for _ in range(n_acc if nk > 1 else 0)],
        input_output_aliases=io_alias, name=name, compiler_params=_cparams(3))(*args)


def _to_seg(a, seg_len):
    T, D = a.shape
    return a.reshape(SUBLANES, seg_len, D).transpose(1, 0, 2).reshape(T, D)


def _to_tok(a, seg_len):
    T, D = a.shape
    return a.reshape(seg_len, SUBLANES, D).transpose(1, 0, 2).reshape(T, D)


def _rms_fwd(h, gain, out_dtype):
    T, D = h.shape
    bm = min(ROW_TILE, T)

    def body(h_ref, g_ref, o_ref):
        x = h_ref[...]
        r = lax.rsqrt(jnp.mean(x * x, axis=-1, keepdims=True) + RMS_EPS)
        o_ref[...] = (x * r * g_ref[...]).astype(out_dtype)

    row = pl.BlockSpec((bm, D), lambda i: (i, 0))
    return pl.pallas_call(
        body, grid=(T // bm,), in_specs=[row, pl.BlockSpec((1, D), lambda i: (0, 0))],
        out_specs=row, out_shape=jax.ShapeDtypeStruct((T, D), out_dtype), name="rms_fwd",
        compiler_params=_cparams(1))(h, gain)


def _rms_bwd_rows(dn, x, g):
    r = lax.rsqrt(jnp.mean(x * x, axis=-1, keepdims=True) + RMS_EPS)
    xh = x * r
    dng = dn * g
    dx = r * (dng - xh * jnp.mean(dng * xh, axis=-1, keepdims=True))
    return dx, jnp.sum(dn * xh, axis=0, keepdims=True)


def _rms_bwd(dn, h, gain, dh_up):
    T, D = h.shape
    bm = min(ROW_TILE, T)

    def body(dn_ref, h_ref, g_ref, up_ref, dh_ref, dhb_ref, dg_ref):
        dx, dg = _rms_bwd_rows(dn_ref[...], h_ref[...], g_ref[...])
        dh = up_ref[...] + dx
        dh_ref[...] = dh
        dhb_ref[...] = dh.astype(BF)

        @pl.when(pl.program_id(0) == 0)
        def _():
            dg_ref[...] = jnp.zeros_like(dg_ref)

        dg_ref[...] += dg

    row = pl.BlockSpec((bm, D), lambda i: (i, 0))
    vec = pl.BlockSpec((1, D), lambda i: (0, 0))
    return pl.pallas_call(
        body, grid=(T // bm,), in_specs=[row, row, vec, row], out_specs=[row, row, vec],
        out_shape=[jax.ShapeDtypeStruct((T, D), F32), jax.ShapeDtypeStruct((T, D), BF),
                   jax.ShapeDtypeStruct((1, D), F32)],
        name="rms_bwd", compiler_params=_cparams(1))(dn, h, gain, dh_up)


def _loss_head(h, gain, target):
    T, D = h.shape
    bm = min(ROW_TILE, T)

    def body(h_ref, g_ref, t_ref, dh_ref, dhb_ref, dg_ref, loss_ref):
        x = h_ref[...]
        g = g_ref[...]
        r = lax.rsqrt(jnp.mean(x * x, axis=-1, keepdims=True) + RMS_EPS)
        err = x * r * g - t_ref[...]
        part = 0.5 * jnp.sum(jnp.sum(err * err, axis=-1, keepdims=True), axis=0, keepdims=True) / D
        dx, dg = _rms_bwd_rows(err / D, x, g)
        dh_ref[...] = dx
        dhb_ref[...] = dx.astype(BF)

        @pl.when(pl.program_id(0) == 0)
        def _():
            dg_ref[...] = jnp.zeros_like(dg_ref)
            loss_ref[...] = jnp.zeros_like(loss_ref)

        dg_ref[...] += dg
        loss_ref[...] += jnp.broadcast_to(part, loss_ref.shape)

    row = pl.BlockSpec((bm, D), lambda i: (i, 0))
    vec = pl.BlockSpec((1, D), lambda i: (0, 0))
    return pl.pallas_call(
        body, grid=(T // bm,), in_specs=[row, vec, row],
        out_specs=[row, row, vec, pl.BlockSpec((SUBLANES, LANES), lambda i: (0, 0))],
        out_shape=[jax.ShapeDtypeStruct((T, D), F32), jax.ShapeDtypeStruct((T, D), BF),
                   jax.ShapeDtypeStruct((1, D), F32), jax.ShapeDtypeStruct((SUBLANES, LANES), F32)],
        name="loss_head", compiler_params=_cparams(1))(h, gain, target)


def _ffn_up(n, w1g, w3g, li, fj):
    T, D = n.shape
    fs = w1g.shape[-1]
    bm = min(ROW_TILE, T)
    wspec = pl.BlockSpec((None, None, None, D, fs), lambda s, i, k: (s, li, fj, 0, 0))
    ospec = pl.BlockSpec((None, bm, fs), lambda s, i, k: (s, i, 0))

    def epi(accs, ex, outs):
        a1, a3 = accs
        outs[0][...] = a1.astype(BF)
        outs[1][...] = a3.astype(BF)
        outs[2][...] = (a1 * _sig(a1) * a3).astype(BF)

    sh = jax.ShapeDtypeStruct((N_DEV, T, fs), BF)
    return _mm("ffn_up", (N_DEV, T // bm, 1),
               [(n, pl.BlockSpec((bm, D), lambda s, i, k: (i, 0))), (w1g, wspec), (w3g, wspec)],
               [(0, 1, NN, 0), (0, 2, NN, 1)], 2, None, [], [(sh, ospec)] * 3, epi)


def _ffn_down(g, w2g, h, li, fj):
    _, T, fs = g.shape
    D = h.shape[1]
    bm = min(ROW_TILE, T)
    row = pl.BlockSpec((bm, D), lambda i, j, s: (i, 0))

    def epi(accs, ex, outs):
        outs[0][...] = ex[0][...] + 0.5 * accs[0]

    return _mm("ffn_down", (T // bm, 1, N_DEV),
               [(g, pl.BlockSpec((None, bm, fs), lambda i, j, s: (s, i, 0))),
                (w2g, pl.BlockSpec((None, None, None, fs, D), lambda i, j, s: (s, li, fj, 0, 0)))],
               [(0, 1, NN, 0)], 1, (bm, D), [(h, row)],
               [(jax.ShapeDtypeStruct((T, D), F32), row)], epi)[0]


def _ffn_bwd_hidden(dhb, w2g, a1, a3, li, fj):
    T, D = dhb.shape
    fs = a1.shape[-1]
    bm = min(ROW_TILE, T)
    aspec = pl.BlockSpec((None, bm, fs), lambda s, i, k: (s, i, 0))

    def epi(accs, ex, outs):
        dg = 0.5 * accs[0]
        a1v = ex[0][...].astype(F32)
        a3v = ex[1][...].astype(F32)
        sg = _sig(a1v)
        outs[0][...] = (dg * a3v * sg * (1.0 + a1v * (1.0 - sg))).astype(BF)
        outs[1][...] = (dg * a1v * sg).astype(BF)

    sh = jax.ShapeDtypeStruct((N_DEV, T, fs), BF)
    return _mm("ffn_bwd_hidden", (N_DEV, T // bm, 1),
               [(dhb, pl.BlockSpec((bm, D), lambda s, i, k: (i, 0))),
                (w2g, pl.BlockSpec((None, None, None, fs, D), lambda s, i, k: (s, li, fj, 0, 0)))],
               [(0, 1, NT, 0)], 1, None, [(a1, aspec), (a3, aspec)], [(sh, aspec)] * 2, epi)


def _ffn_dw2(g, dhb, buf, li, fj):
    _, T, fs = g.shape
    D = dhb.shape[1]
    bk = min(ROW_TILE, T)

    def epi(accs, ex, outs):
        outs[0][...] = (0.5 * accs[0]).astype(BF)

    return _mm("ffn_dw2", (N_DEV, 1, T // bk),
               [(g, pl.BlockSpec((None, bk, fs), lambda s, j, t: (s, t, 0))),
                (dhb, pl.BlockSpec((bk, D), lambda s, j, t: (t, 0)))],
               [(0, 1, TN, 0)], 1, (fs, D), [],
               [(jax.ShapeDtypeStruct(buf.shape, BF),
                 pl.BlockSpec((None, None, None, fs, D), lambda s, j, t: (s, li, fj, 0, 0)))],
               epi, aliases=[(buf, 0)])[0]


def _ffn_dw13(n, da1, da3, buf1, buf3, li, fj):
    T, D = n.shape
    fs = da1.shape[-1]
    bk = min(ROW_TILE, T)
    dspec = pl.BlockSpec((None, bk, fs), lambda s, j, t: (s, t, 0))
    ospec = pl.BlockSpec((None, None, None, D, fs), lambda s, j, t: (s, li, fj, 0, 0))

    def epi(accs, ex, outs):
        outs[0][...] = accs[0].astype(BF)
        outs[1][...] = accs[1].astype(BF)

    sh = jax.ShapeDtypeStruct(buf1.shape, BF)
    return _mm("ffn_dw13", (N_DEV, 1, T // bk),
               [(n, pl.BlockSpec((bk, D), lambda s, j, t: (t, 0))), (da1, dspec), (da3, dspec)],
               [(0, 1, TN, 0), (0, 2, TN, 1)], 2, (D, fs), [], [(sh, ospec)] * 2, epi,
               aliases=[(buf1, 0), (buf3, 1)])


def _ffn_dn(da1, da3, w1g, w3g, li, fj):
    _, T, fs = da1.shape
    D = w1g.shape[-2]
    bm = min(ROW_TILE, T)
    dspec = pl.BlockSpec((None, bm, fs), lambda i, j, s: (s, i, 0))
    wspec = pl.BlockSpec((None, None, None, D, fs), lambda i, j, s: (s, li, fj, 0, 0))
    row = pl.BlockSpec((bm, D), lambda i, j, s: (i, 0))

    def epi(accs, ex, outs):
        outs[0][...] = accs[0]

    return _mm("ffn_dn", (T // bm, 1, N_DEV),
               [(da1, dspec), (w1g, wspec), (da3, dspec), (w3g, wspec)],
               [(0, 1, NT, 0), (2, 3, NT, 0)], 1, (bm, D), [],
               [(jax.ShapeDtypeStruct((T, D), F32), row)], epi)[0]


def _ffn_fwd(h, gain, wg, li, fj):
    n = _rms_fwd(h, gain, BF)
    a1, a3, g = _ffn_up(n, wg["ffn_w1"], wg["ffn_w3"], li, fj)
    return _ffn_down(g, wg["ffn_w2"], h, li, fj), (h, n, a1, a3, g)


def _ffn_bwd(dh, dhb, saved, gain, wg, bufs, li, fj):
    h, n, a1, a3, g = saved
    da1, da3 = _ffn_bwd_hidden(dhb, wg["ffn_w2"], a1, a3, li, fj)
    bufs["ffn_w2"] = _ffn_dw2(g, dhb, bufs["ffn_w2"], li, fj)
    bufs["ffn_w1"], bufs["ffn_w3"] = _ffn_dw13(n, da1, da3, bufs["ffn_w1"], bufs["ffn_w3"], li, fj)
    dn = _ffn_dn(da1, da3, wg["ffn_w1"], wg["ffn_w3"], li, fj)
    return _rms_bwd(dn, h, gain, dh)


def _rope_tables(seq):
    half = LANES // 2
    inv = ROPE_THETA ** (-jnp.arange(0, half, dtype=F32) * 2.0 / LANES)
    ang = jnp.arange(seq, dtype=F32)[:, None] * inv[None, :]
    cos, sin = jnp.cos(ang), jnp.sin(ang)
    return jnp.concatenate([cos, cos], axis=1), jnp.concatenate([-sin, sin], axis=1)


def _branch_bias(nq, bq):
    d = (jnp.arange(nq)[:, None, None] * bq + jnp.arange(bq)[None, :, None]
         - jnp.arange(bq)[None, None, :])
    mult = jnp.zeros(d.shape, F32)
    for window, dil in DILATED_PATTERN:
        mult = mult + ((d >= 0) & (d % dil == 0) & (d <= window)).astype(F32)
    return jnp.where(mult > 0, jnp.log(jnp.maximum(mult, 1.0)), NEG_INF)


def _proj_fwd(u, wing):
    T, D = u.shape
    ws = wing.shape[-1]
    bm = min(ROW_TILE, T)

    def epi(accs, ex, outs):
        outs[0][...] = accs[0]

    return _mm("proj_fwd", (N_DEV, T // bm, 1),
               [(u, pl.BlockSpec((bm, D), lambda s, i, k: (i, 0))),
                (wing, pl.BlockSpec((None, D, ws), lambda s, i, k: (s, 0, 0)))],
               [(0, 1, NN, 0)], 1, None, [],
               [(jax.ShapeDtypeStruct((T, N_DEV * ws), F32),
                 pl.BlockSpec((bm, ws), lambda s, i, k: (i, s)))], epi)[0]


def _rope_fwd(proj, cosf, sinf, seq, nh):
    T = proj.shape[0]
    bs = min(ROW_TILE, seq)
    nst = seq // bs
    scale = LANES ** -0.5

    def body(x_ref, c_ref, s_ref, o_ref):
        j = pl.program_id(1)
        t = x_ref[...]
        rot = t * c_ref[...] + pltpu.roll(t, LANES // 2, 1) * s_ref[...]
        rot = rot * jnp.where(j < nh, scale, 1.0)
        o_ref[...] = jnp.where(j < 2 * nh, rot, t).astype(BF)

    blk = pl.BlockSpec((bs, LANES), lambda r, j: (r, j))
    tab = pl.BlockSpec((bs, LANES), lambda r, j: (r % nst, 0))
    return pl.pallas_call(
        body, grid=(T // bs, 3 * nh), in_specs=[blk, tab, tab], out_specs=blk,
        out_shape=jax.ShapeDtypeStruct((T, 3 * nh * LANES), BF), name="rope_fwd",
        compiler_params=_cparams(2))(proj, cosf, sinf)


def _attn_fwd(qkv, bias, nb, seq, nh):
    T = nb * seq
    bq = bias.shape[1]
    nq = seq // bq

    def body(q_ref, k_ref, v_ref, b_ref, o_ref, lse_ref):
        qi = pl.program_id(2)
        q = q_ref[...]

        def step(kj, carry):
            m, l, acc = carry
            rows = pl.ds(pl.multiple_of(kj * bq, bq), bq)
            s = _dot(q, k_ref[rows, :], NT) + b_ref[qi - kj]
            m_new = jnp.maximum(m, jnp.max(s, axis=1, keepdims=True))
            p = jnp.exp(s - m_new)
            alpha = jnp.exp(m - m_new)
            l = alpha * l + jnp.sum(p, axis=1, keepdims=True)
            acc = alpha * acc + _dot(p, v_ref[rows, :], NN)
            return m_new, l, acc

        init = (jnp.full((bq, 1), NEG_INF, F32), jnp.zeros((bq, 1), F32), jnp.zeros((bq, LANES), F32))
        m, l, acc = lax.fori_loop(0, qi + 1, step, init)
        o_ref[...] = (acc / l).astype(BF)
        lse_ref[...] = m + jnp.log(l)

    return pl.pallas_call(
        body, grid=(nb, nh, nq),
        in_specs=[pl.BlockSpec((bq, LANES), lambda b, h, i: (b * nq + i, h)),
                  pl.BlockSpec((seq, LANES), lambda b, h, i: (b, nh + h)),
                  pl.BlockSpec((seq, LANES), lambda b, h, i: (b, 2 * nh + h)),
                  pl.BlockSpec((nq, bq, bq), lambda b, h, i: (0, 0, 0))],
        out_specs=[pl.BlockSpec((bq, LANES), lambda b, h, i: (b * nq + i, h)),
                   pl.BlockSpec((None, bq, 1), lambda b, h, i: (h, b * nq + i, 0))],
        out_shape=[jax.ShapeDtypeStruct((T, 2 * nh * LANES), BF), jax.ShapeDtypeStruct((nh, T, 1), F32)],
        name="attn_fwd", compiler_params=_cparams(3))(qkv, qkv, qkv, bias)


def _attn_bwd_dq(qkv, cat, dcat, lse, bias, nb, seq, nh):
    T = nb * seq
    bq = bias.shape[1]
    nq = seq // bq

    def body(q_ref, k_ref, v_ref, o_ref, do_ref, lse_ref, b_ref, dq_ref, delta_ref):
        qi = pl.program_id(2)
        q = q_ref[...]
        do = do_ref[...]
        dob = do.astype(BF)
        lse_t = lse_ref[...]
        delta = jnp.sum(do * o_ref[...].astype(F32), axis=1, keepdims=True)
        delta_ref[...] = delta

        def step(kj, dq):
            rows = pl.ds(pl.multiple_of(kj * bq, bq), bq)
            k = k_ref[rows, :]
            p = jnp.exp(_dot(q, k, NT) + b_ref[qi - kj] - lse_t)
            ds = p * (_dot(dob, v_ref[rows, :], NT) - delta)
            return dq + _dot(ds, k, NN)

        dq_ref[...] = lax.fori_loop(0, qi + 1, step, jnp.zeros((bq, LANES), F32))

    tile = pl.BlockSpec((bq, LANES), lambda b, h, i: (b * nq + i, h))
    stat = pl.BlockSpec((None, bq, 1), lambda b, h, i: (h, b * nq + i, 0))
    return pl.pallas_call(
        body, grid=(nb, nh, nq),
        in_specs=[tile,
                  pl.BlockSpec((seq, LANES), lambda b, h, i: (b, nh + h)),
                  pl.BlockSpec((seq, LANES), lambda b, h, i: (b, 2 * nh + h)),
                  tile, tile, stat,
                  pl.BlockSpec((nq, bq, bq), lambda b, h, i: (0, 0, 0))],
        out_specs=[tile, stat],
        out_shape=[jax.ShapeDtypeStruct((T, nh * LANES), F32), jax.ShapeDtypeStruct((nh, T, 1), F32)],
        name="attn_bwd_dq", compiler_params=_cparams(3))(qkv, qkv, qkv, cat, dcat, lse, bias)


def _attn_bwd_dkv(qkv, dcat, lse, delta, bias, nb, seq, nh):
    T = nb * seq
    bq = bias.shape[1]
    nq = seq // bq

    def body(k_ref, v_ref, q_ref, do_ref, lse_ref, delta_ref, b_ref, dk_ref, dv_ref):
        kj = pl.program_id(2)
        k = k_ref[...]
        v = v_ref[...]

        def step(qi, carry):
            dk, dv = carry
            rows = pl.ds(pl.multiple_of(qi * bq, bq), bq)
            q = q_ref[rows, :]
            dob = do_ref[rows, :].astype(BF)
            p = jnp.exp(_dot(q, k, NT) + b_ref[qi - kj] - lse_ref[rows, :])
            dv = dv + _dot(p, dob, TN)
            ds = p * (_dot(dob, v, NT) - delta_ref[rows, :])
            return dk + _dot(ds, q, TN), dv

        z = jnp.zeros((bq, LANES), F32)
        dk, dv = lax.fori_loop(kj, nq, step, (z, z))
        dk_ref[...] = dk
        dv_ref[...] = dv

    stat = pl.BlockSpec((None, seq, 1), lambda b, h, i: (h, b, 0))
    out = pl.BlockSpec((bq, LANES), lambda b, h, i: (b * nq + i, h))
    sh = jax.ShapeDtypeStruct((T, nh * LANES), F32)
    return pl.pallas_call(
        body, grid=(nb, nh, nq),
        in_specs=[pl.BlockSpec((bq, LANES), lambda b, h, i: (b * nq + i, nh + h)),
                  pl.BlockSpec((bq, LANES), lambda b, h, i: (b * nq + i, 2 * nh + h)),
                  pl.BlockSpec((seq, LANES), lambda b, h, i: (b, h)),
                  pl.BlockSpec((seq, LANES), lambda b, h, i: (b, h)),
                  stat, stat,
                  pl.BlockSpec((nq, bq, bq), lambda b, h, i: (0, 0, 0))],
        out_specs=[out, out], out_shape=[sh, sh],
        name="attn_bwd_dkv", compiler_params=_cparams(3))(qkv, qkv, qkv, dcat, lse, delta, bias)


def _conv_parts(gc, xin, w_ref):
    w = [w_ref[k:k + 1, :] for k in range(3)]
    u = gc * xin
    row = lax.broadcasted_iota(jnp.int32, u.shape, 0)
    u1 = jnp.where(row >= 1, pltpu.roll(u, 1, 0), 0.0)
    u2 = jnp.where(row >= 2, pltpu.roll(u, 2, 0), 0.0)
    return u, u1, u2, w[0] * u2 + w[1] * u1 + w[2] * u, w, row


def _conv_fwd(proj, conv_w, cat, nb, seq, width):
    cw = min(2 * LANES, width)
    nc = width // cw

    def body(gb_ref, gc_ref, x_ref, w_ref, cat_ref, o_ref):
        _, _, _, conv, _, _ = _conv_parts(gc_ref[...], x_ref[...], w_ref)
        o_ref[...] = (gb_ref[...] * conv).astype(BF)

    def sec(k):
        return pl.BlockSpec((seq, cw), lambda b, c: (b, k * nc + c))

    return pl.pallas_call(
        body, grid=(nb, nc),
        in_specs=[sec(3), sec(4), sec(5), pl.BlockSpec((3, cw), lambda b, c: (0, c)),
                  pl.BlockSpec(memory_space=pl.ANY)],
        out_specs=pl.BlockSpec((seq, cw), lambda b, c: (b, nc + c)),
        out_shape=jax.ShapeDtypeStruct(cat.shape, BF), input_output_aliases={4: 0},
        name="conv_fwd", compiler_params=_cparams(2))(proj, proj, proj, conv_w, cat)


def _conv_bwd(proj, conv_w, dcat, nb, seq, width):
    cw = min(2 * LANES, width)
    nc = width // cw
    T = nb * seq

    def body(gb_ref, gc_ref, x_ref, w_ref, d_ref, dgb_ref, dgc_ref, dx_ref, dw_ref):
        gc = gc_ref[...]
        xin = x_ref[...]
        u, u1, u2, conv, w, row = _conv_parts(gc, xin, w_ref)
        dsc = d_ref[...]
        dgb_ref[...] = dsc * conv
        dconv = dsc * gb_ref[...]
        d1 = jnp.where(row < seq - 1, pltpu.roll(dconv, seq - 1, 0), 0.0)
        d2 = jnp.where(row < seq - 2, pltpu.roll(dconv, seq - 2, 0), 0.0)
        du = w[2] * dconv + w[1] * d1 + w[0] * d2
        dgc_ref[...] = du * xin
        dx_ref[...] = du * gc

        @pl.when(pl.program_id(1) == 0)
        def _():
            dw_ref[...] = jnp.zeros_like(dw_ref)

        dw_ref[0:1, :] += jnp.sum(dconv * u2, axis=0, keepdims=True)
        dw_ref[1:2, :] += jnp.sum(dconv * u1, axis=0, keepdims=True)
        dw_ref[2:3, :] += jnp.sum(dconv * u, axis=0, keepdims=True)

    def sec(k):
        return pl.BlockSpec((seq, cw), lambda c, b: (b, k * nc + c))

    out = pl.BlockSpec((seq, cw), lambda c, b: (b, c))
    wsp = pl.BlockSpec((3, cw), lambda c, b: (0, c))
    sh = jax.ShapeDtypeStruct((T, width), F32)
    return pl.pallas_call(
        body, grid=(nc, nb), in_specs=[sec(3), sec(4), sec(5), wsp, sec(1)],
        out_specs=[out, out, out, wsp], out_shape=[sh, sh, sh, jax.ShapeDtypeStruct((3, width), F32)],
        name="conv_bwd", compiler_params=_cparams(2))(proj, proj, proj, conv_w, dcat)


def _assemble_dproj(dq, dk, dv, dgb, dgc, dxin, cosf, sinf, seq):
    T, width = dq.shape
    nh = width // LANES
    bs = min(256, seq)
    nst = seq // bs
    scale = LANES ** -0.5

    def body(dq_ref, dk_ref, dv_ref, dgb_ref, dgc_ref, dx_ref, c_ref, s_ref, o_ref):
        sec = pl.program_id(1)
        c = c_ref[...]
        s = s_ref[...]

        def unrope(ref, mul):
            for h in range(nh):
                cols = slice(h * LANES, (h + 1) * LANES)
                t = ref[:, cols]
                o_ref[:, cols] = ((t * c + pltpu.roll(t * s, LANES // 2, 1)) * mul).astype(BF)

        @pl.when(sec == 0)
        def _():
            unrope(dq_ref, scale)

        @pl.when(sec == 1)
        def _():
            unrope(dk_ref, 1.0)

        for k, ref in ((2, dv_ref), (3, dgb_ref), (4, dgc_ref), (5, dx_ref)):
            @pl.when(sec == k)
            def _(ref=ref):
                o_ref[...] = ref[...].astype(BF)

    blk = pl.BlockSpec((bs, width), lambda r, k: (r, 0))
    tab = pl.BlockSpec((bs, LANES), lambda r, k: (r % nst, 0))
    return pl.pallas_call(
        body, grid=(T // bs, 6), in_specs=[blk] * 6 + [tab, tab],
        out_specs=pl.BlockSpec((bs, width), lambda r, k: (r, k)),
        out_shape=jax.ShapeDtypeStruct((T, 6 * width), BF), name="assemble_dproj",
        compiler_params=_cparams(2))(dq, dk, dv, dgb, dgc, dxin, cosf, sinf)


def _res_mm(name, a, w, h):
    T, K = a.shape
    N = w.shape[1]
    bm = min(ROW_TILE, T)
    bk = min(ROW_TILE, K)
    row = pl.BlockSpec((bm, N), lambda i, j, k: (i, 0))

    def epi(accs, ex, outs):
        outs[0][...] = ex[0][...] + accs[0]

    return _mm(name, (T // bm, 1, K // bk),
               [(a, pl.BlockSpec((bm, bk), lambda i, j, k: (i, k))),
                (w, pl.BlockSpec((bk, N), lambda i, j, k: (k, 0)))],
               [(0, 1, NN, 0)], 1, (bm, N), [(h, row)],
               [(jax.ShapeDtypeStruct((T, N), F32), row)], epi)[0]


def _mm_nt(name, a, w, out_dtype):
    T, K = a.shape
    N = w.shape[0]
    bm = min(ROW_TILE, T)
    bn = min(ROW_TILE, N)

    def epi(accs, ex, outs):
        outs[0][...] = accs[0].astype(out_dtype)

    return _mm(name, (T // bm, N // bn, 1),
               [(a, pl.BlockSpec((bm, K), lambda i, j, k: (i, 0))),
                (w, pl.BlockSpec((bn, K), lambda i, j, k: (j, 0)))],
               [(0, 1, NT, 0)], 1, None, [],
               [(jax.ShapeDtypeStruct((T, N), out_dtype), pl.BlockSpec((bm, bn), lambda i, j, k: (i, j)))],
               epi)[0]


def _mm_tn(name, a, bs_list):
    T, M = a.shape
    N = bs_list[0].shape[1]
    bk = min(ROW_TILE, T)
    bmr = min(ROW_TILE, M)
    n = len(bs_list)

    def epi(accs, ex, outs):
        for q in range(n):
            outs[q][...] = accs[q].astype(BF)

    ops = [(a, pl.BlockSpec((bk, bmr), lambda r, j, t: (t, r)))]
    ops += [(b, pl.BlockSpec((bk, N), lambda r, j, t: (t, 0))) for b in bs_list]
    return _mm(name, (M // bmr, 1, T // bk), ops, [(0, 1 + q, TN, q) for q in range(n)], n, (bmr, N), [],
               [(jax.ShapeDtypeStruct((M, N), BF), pl.BlockSpec((bmr, N), lambda r, j, t: (r, 0)))] * n, epi)


def _proj_bwd_x(dproj, wing):
    T = dproj.shape[0]
    _, D, ws = wing.shape
    bm = min(ROW_TILE, T)
    row = pl.BlockSpec((bm, D), lambda i, j, s: (i, 0))

    def epi(accs, ex, outs):
        outs[0][...] = accs[0]

    return _mm("proj_bwd_x", (T // bm, 1, N_DEV),
               [(dproj, pl.BlockSpec((bm, ws), lambda i, j, s: (i, s))),
                (wing, pl.BlockSpec((None, D, ws), lambda i, j, s: (s, 0, 0)))],
               [(0, 1, NT, 0)], 1, (bm, D), [], [(jax.ShapeDtypeStruct((T, D), F32), row)], epi)[0]


def _proj_dw(u, dproj, ws):
    T, D = u.shape
    bk = min(ROW_TILE, T)

    def epi(accs, ex, outs):
        outs[0][...] = accs[0].astype(BF)

    return _mm("proj_dw", (N_DEV, 1, T // bk),
               [(u, pl.BlockSpec((bk, D), lambda s, j, t: (t, 0))),
                (dproj, pl.BlockSpec((bk, ws), lambda s, j, t: (t, s)))],
               [(0, 1, TN, 0)], 1, (D, ws), [],
               [(jax.ShapeDtypeStruct((N_DEV, D, ws), BF),
                 pl.BlockSpec((None, D, ws), lambda s, j, t: (s, 0, 0)))], epi)[0]


def _mixer_ab_fwd(h, gain, wg, tabs, nb, seq):
    cosf, sinf, bias = tabs
    wing = wg["ab_w_in"]
    width = wing.shape[-1] * N_DEV // 6
    nh = width // LANES
    u = _rms_fwd(h, gain, BF)
    proj = _proj_fwd(u, wing)
    qkv = _rope_fwd(proj, cosf, sinf, seq, nh)
    cat, lse = _attn_fwd(qkv, bias, nb, seq, nh)
    cat = _conv_fwd(proj, wg["ab_conv_w"], cat, nb, seq, width)
    return _res_mm("outproj_fwd", cat, wg["ab_w_out"], h), (h, u, proj, qkv, cat, lse)


def _mixer_ab_bwd(dh, dhb, saved, gain, wg, tabs, nb, seq):
    cosf, sinf, bias = tabs
    h, u, proj, qkv, cat, lse = saved
    wing = wg["ab_w_in"]
    ws = wing.shape[-1]
    width = ws * N_DEV // 6
    nh = width // LANES
    dcat = _mm_nt("outproj_bwd_x", dhb, wg["ab_w_out"], F32)
    dwout = _mm_tn("outproj_dw", cat, [dhb])[0]
    dq, delta = _attn_bwd_dq(qkv, cat, dcat, lse, bias, nb, seq, nh)
    dk, dv = _attn_bwd_dkv(qkv, dcat, lse, delta, bias, nb, seq, nh)
    dgb, dgc, dxin, dconvw = _conv_bwd(proj, wg["ab_conv_w"], dcat, nb, seq, width)
    dproj = _assemble_dproj(dq, dk, dv, dgb, dgc, dxin, cosf, sinf, seq)
    du = _proj_bwd_x(dproj, wing)
    dwin = _proj_dw(u, dproj, ws)
    dh_in, dhb_in, dgain = _rms_bwd(du, h, gain, dh)
    return dh_in, dhb_in, dgain, dwin, dconvw, dwout


def _s5_zoh(lr, li, log_dt):
    dt = jnp.exp(log_dt)
    mag = jnp.exp(lr * dt)
    ar = mag * jnp.cos(li * dt)
    ai = mag * jnp.sin(li * dt)
    den = lr * lr + li * li
    return dt, ar, ai, den, ((ar - 1.0) * lr + ai * li) / den, (ai * lr - (ar - 1.0) * li) / den


def _s5_discretize(lam_re, lam_im, log_dt, bt_re, bt_im):
    def body(lr_ref, li_ref, ld_ref, br_ref, bi_ref, ar_ref, ai_ref, bbr_ref, bbi_ref):
        _, ar, ai, _, fr, fi = _s5_zoh(lr_ref[...], li_ref[...], ld_ref[...])
        ar_ref[...] = ar
        ai_ref[...] = ai
        bbr_ref[...] = fr * br_ref[...] - fi * bi_ref[...]
        bbi_ref[...] = fr * bi_ref[...] + fi * br_ref[...]

    small = jax.ShapeDtypeStruct(lam_re.shape, F32)
    big = jax.ShapeDtypeStruct(bt_re.shape, F32)
    return pl.pallas_call(body, out_shape=[small, small, big, big], name="s5_discretize",
                          compiler_params=_cparams(0))(lam_re, lam_im, log_dt, bt_re, bt_im)


def _s5_discretize_bwd(lam_re, lam_im, log_dt, bt_re, bt_im, d_ar, d_ai, d_bbr, d_bbi):

    def body(lr_ref, li_ref, ld_ref, br_ref, bi_ref, dar_ref, dai_ref, dbbr_ref, dbbi_ref,
             dlr_ref, dli_ref, dld_ref, dbr_ref, dbi_ref):
        lr, li = lr_ref[...], li_ref[...]
        dt, ar, ai, den, fr, fi = _s5_zoh(lr, li, ld_ref[...])
        br, bi = br_ref[...], bi_ref[...]
        dbbr, dbbi = dbbr_ref[...], dbbi_ref[...]
        dbr_ref[...] = dbbr * fr + dbbi * fi
        dbi_ref[...] = dbbi * fr - dbbr * fi
        dfr = jnp.sum(dbbr * br + dbbi * bi, axis=1, keepdims=True)
        dfi = jnp.sum(dbbi * br - dbbr * bi, axis=1, keepdims=True)
        dnr = dfr / den
        dni = dfi / den
        dden = -(dfr * fr + dfi * fi) / den
        dar = dar_ref[...] + dnr * lr - dni * li
        dai = dai_ref[...] + dnr * li + dni * lr
        dlr_ref[...] = dnr * (ar - 1.0) + dni * ai + 2.0 * dden * lr + dt * (dar * ar + dai * ai)
        dli_ref[...] = dnr * ai - dni * (ar - 1.0) + 2.0 * dden * li + dt * (dai * ar - dar * ai)
        ddt = jnp.sum(dar * (lr * ar - li * ai) + dai * (lr * ai + li * ar), axis=2, keepdims=True)
        dld_ref[...] = ddt * dt

    small = jax.ShapeDtypeStruct(lam_re.shape, F32)
    big = jax.ShapeDtypeStruct(bt_re.shape, F32)
    return pl.pallas_call(
        body, out_shape=[small, small, jax.ShapeDtypeStruct(log_dt.shape, F32), big, big],
        name="s5_discretize_bwd", compiler_params=_cparams(0))(
            lam_re, lam_im, log_dt, bt_re, bt_im, d_ar, d_ai, d_bbr, d_bbi)


def _rows8(t):
    return pl.ds(pl.multiple_of(t * SUBLANES, SUBLANES), SUBLANES)


def _cmul_add(ar, ai, sr, si, br, bi):
    return ar * sr - ai * si + br, ar * si + ai * sr + bi


def _s5_specs(R, nj):
    sh = STATE_COLS // 2
    return dict(
        rows=pl.BlockSpec((R, LANES), lambda j, hh: (0, j)),
        bd=pl.BlockSpec((None, LANES, sh), lambda j, hh: (j, 0, hh)),
        cd=pl.BlockSpec((None, sh, LANES), lambda j, hh: (j, hh, 0)),
        a=pl.BlockSpec((None, 1, sh), lambda j, hh: (j, 0, hh)),
        vec=pl.BlockSpec((1, LANES), lambda j, hh: (0, j)),
        init=pl.BlockSpec((None, SUBLANES, sh), lambda j, hh: (j, 0, hh)))


def _s5_fwd(u, mats, seg_len, nseg):
    bdr, bdi, cdr, cdi, are, aim, dsk = mats
    R, D = u.shape
    nj = D // LANES
    sh = STATE_COLS // 2
    rc = min(R, 1024)
    sp = _s5_specs(R, nj)

    def body(u_ref, bdr_ref, bdi_ref, cdr_ref, cdi_ref, ar_ref, ai_ref, d_ref,
             y_ref, yg_ref, ir_ref, ii_ref, sre, sim):
        hh = pl.program_id(1)
        ar = jnp.broadcast_to(ar_ref[...], (SUBLANES, sh))
        ai = jnp.broadcast_to(ai_ref[...], (SUBLANES, sh))

        def bu_chunk(c, _):
            rows = pl.ds(pl.multiple_of(c * rc, rc), rc)
            ub = u_ref[rows, :].astype(BF)
            sre[rows, :] = _dot(ub, bdr_ref[...], NN)
            sim[rows, :] = _dot(ub, bdi_ref[...], NN)
            return 0

        lax.fori_loop(0, R // rc, bu_chunk, 0)
        z = jnp.zeros((SUBLANES, sh), F32)

        def local_scan(t, c):
            return _cmul_add(ar, ai, c[0], c[1], sre[_rows8(t), :], sim[_rows8(t), :])

        er, ei = lax.fori_loop(0, seg_len, local_scan, (z, z))
        pr, pi = lax.fori_loop(0, seg_len - 1, lambda _, c: _cmul_add(ar, ai, c[0], c[1], 0.0, 0.0), (ar, ai))
        first = (lax.broadcasted_iota(jnp.int32, (SUBLANES, sh), 0) & (nseg - 1)) == 0

        def prev(x):
            return jnp.where(first, 0.0, pltpu.roll(x, 1, 0))

        xr, xi = er, ei
        for _ in range(nseg - 1):
            xr, xi = _cmul_add(pr, pi, prev(xr), prev(xi), er, ei)
        i_r, i_i = prev(xr), prev(xi)
        ir_ref[...] = i_r
        ii_ref[...] = i_i

        def scan(t, c):
            nr, ni = _cmul_add(ar, ai, c[0], c[1], sre[_rows8(t), :], sim[_rows8(t), :])
            sre[_rows8(t), :] = nr
            sim[_rows8(t), :] = ni
            return nr, ni

        lax.fori_loop(0, seg_len, scan, (i_r, i_i))

        def y_chunk(c, _):
            rows = pl.ds(pl.multiple_of(c * rc, rc), rc)
            y = _dot(sre[rows, :], cdr_ref[...], NN) + _dot(sim[rows, :], cdi_ref[...], NN)

            @pl.when(hh == 0)
            def _():
                y_ref[rows, :] = y + d_ref[...] * u_ref[rows, :]

            @pl.when(hh == 1)
            def _():
                yt = y_ref[rows, :] + y
                y_ref[rows, :] = yt
                yg_ref[rows, :] = _gelu(yt).astype(BF)

            return 0

        lax.fori_loop(0, R // rc, y_chunk, 0)

    init_sh = jax.ShapeDtypeStruct((nj, SUBLANES, STATE_COLS), F32)
    return pl.pallas_call(
        body, grid=(nj, 2),
        in_specs=[sp["rows"], sp["bd"], sp["bd"], sp["cd"], sp["cd"], sp["a"], sp["a"], sp["vec"]],
        out_specs=[sp["rows"], sp["rows"], sp["init"], sp["init"]],
        out_shape=[jax.ShapeDtypeStruct((R, D), F32), jax.ShapeDtypeStruct((R, D), BF), init_sh, init_sh],
        scratch_shapes=[pltpu.VMEM((R, sh), F32)] * 2, name="s5_fwd",
        compiler_params=_cparams(2))(u, bdr, bdi, cdr, cdi, are, aim, dsk)


def _s5_bwd(u, dy, mats, init_re, init_im, seg_len, nseg):
    bdr, bdi, cdr, cdi, are, aim, dsk = mats
    R, D = u.shape
    nj = D // LANES
    sh = STATE_COLS // 2
    rc = min(R, 1024)
    sp = _s5_specs(R, nj)

    def body(u_ref, dy_ref, bdr_ref, bdi_ref, cdr_ref, cdi_ref, ar_ref, ai_ref, d_ref, ir_ref, ii_ref,
             du_ref, dbdr_ref, dbdi_ref, dcdr_ref, dcdi_ref, dar_ref, dai_ref, dd_ref,
             sre, sim, gre, gim):
        hh = pl.program_id(1)
        ar = jnp.broadcast_to(ar_ref[...], (SUBLANES, sh))
        ai = jnp.broadcast_to(ai_ref[...], (SUBLANES, sh))
        i_r, i_i = ir_ref[...], ii_ref[...]

        def chunk(c):
            return pl.ds(pl.multiple_of(c * rc, rc), rc)

        def bu_chunk(c, _):
            ub = u_ref[chunk(c), :].astype(BF)
            sre[chunk(c), :] = _dot(ub, bdr_ref[...], NN)
            sim[chunk(c), :] = _dot(ub, bdi_ref[...], NN)
            return 0

        lax.fori_loop(0, R // rc, bu_chunk, 0)

        def scan(t, c):
            nr, ni = _cmul_add(ar, ai, c[0], c[1], sre[_rows8(t), :], sim[_rows8(t), :])
            sre[_rows8(t), :] = nr
            sim[_rows8(t), :] = ni
            return nr, ni

        lax.fori_loop(0, seg_len, scan, (i_r, i_i))

        def c_chunk(c, carry):
            dyb = dy_ref[chunk(c), :].astype(BF)
            gre[chunk(c), :] = _dot(dyb, cdr_ref[...], NT)
            gim[chunk(c), :] = _dot(dyb, cdi_ref[...], NT)
            return (carry[0] + _dot(sre[chunk(c), :], dyb, TN), carry[1] + _dot(sim[chunk(c), :], dyb, TN))

        zc = jnp.zeros((sh, LANES), F32)
        dcr, dci = lax.fori_loop(0, R // rc, c_chunk, (zc, zc))
        dcdr_ref[...] = dcr
        dcdi_ref[...] = dci

        def adj(t, gr_next, gi_next):
            return _cmul_add(ar, -ai, gr_next, gi_next, gre[_rows8(t), :], gim[_rows8(t), :])

        z = jnp.zeros((SUBLANES, sh), F32)
        fr, fi = lax.fori_loop(0, seg_len, lambda i, c: adj(seg_len - 1 - i, c[0], c[1]), (z, z))
        pr, pi = lax.fori_loop(0, seg_len - 1, lambda _, c: _cmul_add(ar, ai, c[0], c[1], 0.0, 0.0), (ar, ai))
        last = (lax.broadcasted_iota(jnp.int32, (SUBLANES, sh), 0) & (nseg - 1)) == nseg - 1

        def nxt(x):
            return jnp.where(last, 0.0, pltpu.roll(x, SUBLANES - 1, 0))

        xr, xi = fr, fi
        for _ in range(nseg - 1):
            xr, xi = _cmul_add(pr, -pi, nxt(xr), nxt(xi), fr, fi)
        g0r, g0i = nxt(xr), nxt(xi)

        def adj_scan(i, c):
            t = seg_len - 1 - i
            gr, gi = adj(t, c[0], c[1])
            gre[_rows8(t), :] = gr
            gim[_rows8(t), :] = gi
            spr, spi = sre[_rows8(t - 1), :], sim[_rows8(t - 1), :]
            return gr, gi, c[2] + spr * gr + spi * gi, c[3] + spr * gi - spi * gr

        gr, gi, dar, dai = lax.fori_loop(0, seg_len - 1, adj_scan, (g0r, g0i, z, z))
        gr, gi = adj(0, gr, gi)
        gre[_rows8(0), :] = gr
        gim[_rows8(0), :] = gi
        dar_ref[...] = jnp.sum(dar + i_r * gr + i_i * gi, axis=0, keepdims=True)
        dai_ref[...] = jnp.sum(dai + i_r * gi - i_i * gr, axis=0, keepdims=True)

        def d_chunk(c, carry):
            ub = u_ref[chunk(c), :].astype(BF)
            grb = gre[chunk(c), :].astype(BF)
            gib = gim[chunk(c), :].astype(BF)
            du = _dot(grb, bdr_ref[...], NT) + _dot(gib, bdi_ref[...], NT)

            @pl.when(hh == 0)
            def _():
                du_ref[chunk(c), :] = du + d_ref[...] * dy_ref[chunk(c), :]

            @pl.when(hh == 1)
            def _():
                du_ref[chunk(c), :] += du

            dd = carry[2] + jnp.sum(dy_ref[chunk(c), :] * u_ref[chunk(c), :], axis=0, keepdims=True)
            return carry[0] + _dot(ub, grb, TN), carry[1] + _dot(ub, gib, TN), dd

        zb = jnp.zeros((LANES, sh), F32)
        dbr, dbi, dd = lax.fori_loop(0, R // rc, d_chunk, (zb, zb, jnp.zeros((1, LANES), F32)))
        dbdr_ref[...] = dbr
        dbdi_ref[...] = dbi
        dd_ref[...] = dd

    bd_sh = jax.ShapeDtypeStruct((nj, LANES, STATE_COLS), F32)
    cd_sh = jax.ShapeDtypeStruct((nj, STATE_COLS, LANES), F32)
    a_sh = jax.ShapeDtypeStruct((nj, 1, STATE_COLS), F32)
    return pl.pallas_call(
        body, grid=(nj, 2),
        in_specs=[sp["rows"], sp["rows"], sp["bd"], sp["bd"], sp["cd"], sp["cd"], sp["a"], sp["a"],
                  sp["vec"], sp["init"], sp["init"]],
        out_specs=[sp["rows"], sp["bd"], sp["bd"], sp["cd"], sp["cd"], sp["a"], sp["a"], sp["vec"]],
        out_shape=[jax.ShapeDtypeStruct((R, D), F32), bd_sh, bd_sh, cd_sh, cd_sh, a_sh, a_sh,
                   jax.ShapeDtypeStruct((1, D), F32)],
        scratch_shapes=[pltpu.VMEM((R, sh), F32)] * 4, name="s5_bwd",
        compiler_params=_cparams(2))(u, dy, bdr, bdi, cdr, cdi, are, aim, dsk, init_re, init_im)


def _glu_fwd(yg, wa, wb, h):
    T, D = yg.shape
    N = wa.shape[1]
    bm = min(ROW_TILE, T)
    bn = min(ROW_TILE, N)
    wspec = pl.BlockSpec((D, bn), lambda i, j, k: (0, j))
    ospec = pl.BlockSpec((bm, bn), lambda i, j, k: (i, j))

    def epi(accs, ex, outs):
        pa, pb = accs
        outs[0][...] = ex[0][...] + pa * _sig(pb)
        outs[1][...] = pa.astype(BF)
        outs[2][...] = pb.astype(BF)

    return _mm("glu_fwd", (T // bm, N // bn, 1),
               [(yg, pl.BlockSpec((bm, D), lambda i, j, k: (i, 0))), (wa, wspec), (wb, wspec)],
               [(0, 1, NN, 0), (0, 2, NN, 1)], 2, None, [(h, ospec)],
               [(jax.ShapeDtypeStruct((T, N), F32), ospec), (jax.ShapeDtypeStruct((T, N), BF), ospec),
                (jax.ShapeDtypeStruct((T, N), BF), ospec)], epi)


def _glu_bwd_gates(dz, pa, pb):
    T, D = dz.shape
    bm = min(ROW_TILE, T)

    def body(dz_ref, pa_ref, pb_ref, dpa_ref, dpb_ref):
        dz = dz_ref[...]
        sg = _sig(pb_ref[...].astype(F32))
        dpa_ref[...] = (dz * sg).astype(BF)
        dpb_ref[...] = (dz * pa_ref[...].astype(F32) * sg * (1.0 - sg)).astype(BF)

    row = pl.BlockSpec((bm, D), lambda i: (i, 0))
    return pl.pallas_call(
        body, grid=(T // bm,), in_specs=[row] * 3, out_specs=[row] * 2,
        out_shape=[jax.ShapeDtypeStruct((T, D), BF)] * 2, name="glu_bwd_gates",
        compiler_params=_cparams(1))(dz, pa, pb)


def _glu_bwd_y(dpa, dpb, wa, wb, y_pre):
    T, N = dpa.shape
    D = wa.shape[0]
    bm = min(ROW_TILE, T)
    bn = min(ROW_TILE, D)
    aspec = pl.BlockSpec((bm, N), lambda i, j, k: (i, 0))
    wspec = pl.BlockSpec((bn, N), lambda i, j, k: (j, 0))
    ospec = pl.BlockSpec((bm, bn), lambda i, j, k: (i, j))

    def epi(accs, ex, outs):
        outs[0][...] = accs[0] * _gelu_grad(ex[0][...])

    return _mm("glu_bwd_y", (T // bm, D // bn, 1), [(dpa, aspec), (wa, wspec), (dpb, aspec), (wb, wspec)],
               [(0, 1, NT, 0), (2, 3, NT, 0)], 1, None, [(y_pre, ospec)],
               [(jax.ShapeDtypeStruct((T, D), F32), ospec)], epi)[0]


def _block_diag_in(x, nj):
    g = GROUPS_PER_BLOCK
    x = x.reshape(nj, g, 1, S5_GROUP, S5_STATE)
    eye = jnp.eye(g, dtype=bool)[None, :, :, None, None]
    full = jnp.where(eye, x, 0.0)
    return full.transpose(0, 1, 3, 2, 4).reshape(nj, g * S5_GROUP, g * S5_STATE)


def _block_diag_out(x, nj):
    return _block_diag_in(x, nj).transpose(0, 2, 1)


def _diag_of_in(m, nj):
    g = GROUPS_PER_BLOCK
    m5 = m.reshape(nj, g, S5_GROUP, g, S5_STATE)
    d = jnp.diagonal(m5, axis1=1, axis2=3)
    return d.transpose(0, 3, 1, 2).reshape(nj * g, S5_GROUP, S5_STATE)


def _mixer_s5_fwd(h, gain, wg, p, nb, seq):
    T, D = h.shape
    nj = D // LANES
    nseg = SUBLANES // nb
    seg_len = seq // nseg
    G = p["s5_lambda_re"].shape[1]
    lam_re = p["s5_lambda_re"].reshape(G, 1, S5_STATE)
    lam_im = p["s5_lambda_im"].reshape(G, 1, S5_STATE)
    log_dt = p["s5_log_dt"].reshape(G, 1, 1)
    bt_re = p["s5_b_re"][0].transpose(0, 2, 1)
    bt_im = p["s5_b_im"][0].transpose(0, 2, 1)
    ar, ai, bbr, bbi = _s5_discretize(lam_re, lam_im, log_dt, bt_re, bt_im)
    mats = (_block_diag_in(bbr, nj).astype(BF), _block_diag_in(bbi, nj).astype(BF),
            _block_diag_out(p["s5_c_re"][0], nj).astype(BF),
            _block_diag_out(-p["s5_c_im"][0], nj).astype(BF),
            ar.reshape(nj, 1, STATE_COLS), ai.reshape(nj, 1, STATE_COLS), wg["s5_d"])
    h_seg = _to_seg(h, seg_len)
    u = _rms_fwd(h_seg, gain, F32)
    y_pre, yg, init_re, init_im = _s5_fwd(u, mats, seg_len, nseg)
    h_out, pa, pb = _glu_fwd(yg, wg["s5_glu_wa"], wg["s5_glu_wb"], h_seg)
    disc_in = (lam_re, lam_im, log_dt, bt_re, bt_im)
    return _to_tok(h_out, seg_len), (h_seg, u, mats, y_pre, yg, init_re, init_im, pa, pb, disc_in, seg_len, nseg)


def _mixer_s5_bwd(dh, saved, gain, wg):
    h_seg, u, mats, y_pre, yg, init_re, init_im, pa, pb, disc_in, seg_len, nseg = saved
    T, D = h_seg.shape
    nj = D // LANES
    G = nj * GROUPS_PER_BLOCK
    dh_seg = _to_seg(dh, seg_len)
    dpa, dpb = _glu_bwd_gates(dh_seg, pa, pb)
    dy = _glu_bwd_y(dpa, dpb, wg["s5_glu_wa"], wg["s5_glu_wb"], y_pre)
    dwa, dwb = _mm_tn("glu_dw", yg, [dpa, dpb])
    du, dbdr, dbdi, dcdr, dcdi, dar, dai, dd = _s5_bwd(u, dy, mats, init_re, init_im, seg_len, nseg)
    d_bbr = _diag_of_in(dbdr, nj)
    d_bbi = _diag_of_in(dbdi, nj)
    d_c_re = _diag_of_in(dcdr.transpose(0, 2, 1), nj)
    d_c_im = -_diag_of_in(dcdi.transpose(0, 2, 1), nj)
    dlr, dli, dld, dbr, dbi = _s5_discretize_bwd(
        *disc_in, dar.reshape(G, 1, S5_STATE), dai.reshape(G, 1, S5_STATE), d_bbr, d_bbi)
    small = {"s5_lambda_re": dlr.reshape(1, G, S5_STATE), "s5_lambda_im": dli.reshape(1, G, S5_STATE),
             "s5_log_dt": dld.reshape(1, G),
             "s5_b_re": dbr.transpose(0, 2, 1)[None], "s5_b_im": dbi.transpose(0, 2, 1)[None],
             "s5_c_re": d_c_re[None], "s5_c_im": d_c_im[None], "s5_d": dd}
    dh_in, _, dgain = _rms_bwd(du, h_seg, gain, dh_seg)
    dh_in = _to_tok(dh_in, seg_len)
    return dh_in, dh_in.astype(BF), dgain, dwa, dwb, small


def _mesh_pos():
    return lax.axis_index("x"), lax.axis_index("y"), lax.axis_index("c")


def _two_level_gather(srcs, slots, send_sems, recv_sems):
    x, y, c = _mesh_pos()
    me, sib = (x, y, c), (x, y, 1 - c)
    chips = [(1 - x, y), (x, 1 - y), (1 - x, 1 - y)]

    def idx(p):
        return 4 * p[0] + 2 * p[1] + p[2]

    def copy(a, k, block, to, src=None):
        dst = slots[a].at[idx(block)]
        return pltpu.make_async_remote_copy(
            src_ref=dst if src is None else src, dst_ref=dst, send_sem=send_sems.at[7 * a + k],
            recv_sem=recv_sems.at[7 * a + k], device_id=to, device_id_type=MESH)

    n = len(srcs)
    sends = []
    for a in range(n):
        first = [copy(a, 0, me, sib, src=srcs[a])]
        first += [copy(a, 1 + j, me, (*chip, c), src=srcs[a]) for j, chip in enumerate(chips)]
        for cp in first:
            cp.start()
        sends += first
    for a in range(n):
        for j, chip in enumerate(chips):
            copy(a, 1 + j, (*chip, c), me).wait_recv()
            fwd = copy(a, 4 + j, (*chip, c), sib)
            fwd.start()
            sends.append(fwd)
    for a in range(n):
        copy(a, 0, sib, me).wait_recv()
        for j, chip in enumerate(chips):
            copy(a, 4 + j, (*chip, 1 - c), me).wait_recv()
    for cp in sends:
        cp.wait_send()
    return idx(me)


def _all_gather(arrs, name):
    n = len(arrs)

    def body(*refs):
        xs, outs = refs[:n], refs[n:2 * n]
        send_sems, recv_sems, local_sems = refs[2 * n:]
        x, y, c = _mesh_pos()
        mine = [pltpu.make_async_copy(xs[a], outs[a].at[4 * x + 2 * y + c], local_sems.at[a]) for a in range(n)]
        for cp in mine:
            cp.start()
        _two_level_gather(xs, outs, send_sems, recv_sems)
        for cp in mine:
            cp.wait()

    any_spec = pl.BlockSpec(memory_space=pl.ANY)
    return pl.pallas_call(
        body, in_specs=[any_spec] * n, out_specs=[any_spec] * n,
        out_shape=[jax.ShapeDtypeStruct((N_DEV,) + a.shape, a.dtype) for a in arrs],
        scratch_shapes=[pltpu.SemaphoreType.DMA((7 * n,)), pltpu.SemaphoreType.DMA((7 * n,)),
                        pltpu.SemaphoreType.DMA((n,))],
        name=name, compiler_params=_cparams(0))(*arrs)


def _pair_exchange(grads, name):
    n = len(grads)

    def body(*refs):
        gs, outs = refs[:n], refs[n:2 * n]
        send_sems, recv_sems = refs[2 * n:]
        x, y, c = _mesh_pos()
        copies = []
        for a in range(n):
            for k in range(4):
                copies.append(pltpu.make_async_remote_copy(
                    src_ref=gs[a].at[2 * k + 1 - c], dst_ref=outs[a].at[k], send_sem=send_sems.at[4 * a + k],
                    recv_sem=recv_sems.at[4 * a + k], device_id=(x, y, 1 - c), device_id_type=MESH))
        for cp in copies:
            cp.start()
        for cp in copies:
            cp.wait()

    any_spec = pl.BlockSpec(memory_space=pl.ANY)
    return pl.pallas_call(
        body, in_specs=[any_spec] * n, out_specs=[any_spec] * n,
        out_shape=[jax.ShapeDtypeStruct((4,) + g.shape[1:], g.dtype) for g in grads],
        scratch_shapes=[pltpu.SemaphoreType.DMA((4 * n,)), pltpu.SemaphoreType.DMA((4 * n,))],
        name=name, compiler_params=_cparams(0))(*grads)


def _chip_exchange(parts, name):
    n = len(parts)

    def body(*refs):
        ps, outs = refs[:n], refs[n:2 * n]
        send_sems, recv_sems = refs[2 * n:]
        x, y, c = _mesh_pos()
        chips = [(1 - x, y), (x, 1 - y), (1 - x, 1 - y)]
        copies = []
        for a in range(n):
            for j, chip in enumerate(chips):
                copies.append(pltpu.make_async_remote_copy(
                    src_ref=ps[a].at[2 * chip[0] + chip[1]], dst_ref=outs[a].at[j],
                    send_sem=send_sems.at[3 * a + j], recv_sem=recv_sems.at[3 * a + j],
                    device_id=(*chip, c), device_id_type=MESH))
        for cp in copies:
            cp.start()
        for cp in copies:
            cp.wait()

    any_spec = pl.BlockSpec(memory_space=pl.ANY)
    return pl.pallas_call(
        body, in_specs=[any_spec] * n, out_specs=[any_spec] * n,
        out_shape=[jax.ShapeDtypeStruct((3,) + p.shape[1:], p.dtype) for p in parts],
        scratch_shapes=[pltpu.SemaphoreType.DMA((3 * n,)), pltpu.SemaphoreType.DMA((3 * n,))],
        name=name, compiler_params=_cparams(0))(*parts)


def _pair_sum(grad, recv, core):
    _, R, C = grad.shape
    br = _row_block(R, C)

    def body(core_ref, g_ref, r_ref, o_ref):
        o_ref[...] = (g_ref[...].astype(F32) + r_ref[...].astype(F32)).astype(BF)

    spec = pl.BlockSpec((None, br, C), lambda k, i, core_ref: (k, i, 0))
    return pl.pallas_call(
        body, grid_spec=pltpu.PrefetchScalarGridSpec(
            num_scalar_prefetch=1, grid=(4, R // br),
            in_specs=[pl.BlockSpec((None, br, C), lambda k, i, core_ref: (2 * k + core_ref[0], i, 0)), spec],
            out_specs=spec),
        out_shape=jax.ShapeDtypeStruct((4, R, C), BF), name="pair_sum", compiler_params=_cparams(2))(core, grad, recv)


def _adamw(w, g, m, v):
    m = ADAM_B1 * m + (1.0 - ADAM_B1) * g
    v = ADAM_B2 * v + (1.0 - ADAM_B2) * (g * g)
    m_hat = m / (1.0 - ADAM_B1 ** ADAM_STEP)
    v_hat = v / (1.0 - ADAM_B2 ** ADAM_STEP)
    return -ADAM_LR * (m_hat / (jnp.sqrt(v_hat) + ADAM_EPS) + ADAM_WD * w), m, v


def _adamw_sharded(w, m, v, part, recv, chip):
    R, C = w.shape
    br = _row_block(R, C)

    def body(chip_ref, w_ref, m_ref, v_ref, p_ref, r_ref, g_ref, d_ref, nm_ref, nv_ref):
        g = p_ref[...].astype(F32)
        for j in range(3):
            g = g + r_ref[j].astype(F32)
        d, nm, nv = _adamw(w_ref[...], g, m_ref[...], v_ref[...])
        g_ref[...] = g
        d_ref[...] = d
        nm_ref[...] = nm
        nv_ref[...] = nv

    row = pl.BlockSpec((br, C), lambda i, chip_ref: (i, 0))
    sh = jax.ShapeDtypeStruct((R, C), F32)
    return pl.pallas_call(
        body, grid_spec=pltpu.PrefetchScalarGridSpec(
            num_scalar_prefetch=1, grid=(R // br,),
            in_specs=[row, row, row,
                      pl.BlockSpec((None, br, C), lambda i, chip_ref: (chip_ref[0], i, 0)),
                      pl.BlockSpec((3, br, C), lambda i, chip_ref: (0, i, 0))],
            out_specs=[row] * 4),
        out_shape=[sh] * 4, name="adamw_sharded", compiler_params=_cparams(1))(chip, w, m, v, part, recv)


def _all_reduce_small(x):
    rows = x.shape[0]

    def body(x_ref, o_ref, buf, send_sems, recv_sems):
        xp, yp, cp = _mesh_pos()
        buf[4 * xp + 2 * yp + cp] = x_ref[...]
        _two_level_gather([x_ref], [buf], send_sems, recv_sems)
        acc = buf[0]
        for d in range(1, N_DEV):
            acc = acc + buf[d]
        o_ref[...] = acc

    vm = pl.BlockSpec(memory_space=pltpu.VMEM)
    return pl.pallas_call(
        body, in_specs=[vm], out_specs=vm, out_shape=jax.ShapeDtypeStruct(x.shape, F32),
        scratch_shapes=[pltpu.VMEM((N_DEV, rows, LANES), F32), pltpu.SemaphoreType.DMA((7,)),
                        pltpu.SemaphoreType.DMA((7,))],
        name="all_reduce_small", compiler_params=_cparams(0))(x)


def _adamw_small(w, g, m, v):
    def body(w_ref, g_ref, m_ref, v_ref, d_ref, nm_ref, nv_ref):
        d, nm, nv = _adamw(w_ref[...], g_ref[...], m_ref[...], v_ref[...])
        d_ref[...] = d
        nm_ref[...] = nm
        nv_ref[...] = nv

    sh = jax.ShapeDtypeStruct(w.shape, F32)
    return pl.pallas_call(body, out_shape=[sh] * 3, name="adamw_small", compiler_params=_cparams(0))(w, g, m, v)


def _pack(arrs):
    flat = jnp.concatenate([a.reshape(-1).astype(F32) for a in arrs])
    rows = -(-flat.shape[0] // (SUBLANES * LANES)) * SUBLANES
    return jnp.pad(flat, (0, rows * LANES - flat.shape[0])).reshape(rows, LANES)


def _unpack(buf, shapes):
    flat = buf.reshape(-1)
    out, off = [], 0
    for s in shapes:
        n = 1
        for d in s:
            n *= d
        out.append(flat[off:off + n].reshape(s))
        off += n
    return out


BIG = ("ffn_w1", "ffn_w3", "ffn_w2", "ab_w_in", "ab_w_out", "s5_glu_wa", "s5_glu_wb")
SMALL_SHARDED = ("ab_conv_w", "s5_d")
NAMES = ("ln_ffn_pre", "ln_mix", "ln_ffn_post", "ln_final", "ffn_w1", "ffn_w3", "ffn_w2", "ab_w_in",
         "ab_conv_w", "ab_w_out", "s5_lambda_re", "s5_lambda_im", "s5_log_dt", "s5_b_re", "s5_b_im",
         "s5_c_re", "s5_c_im", "s5_d", "s5_glu_wa", "s5_glu_wb")


def kernel(x, ln_ffn_pre, ln_mix, ln_ffn_post, ln_final, ffn_w1, ffn_w3, ffn_w2, ab_w_in, ab_conv_w, ab_w_out, s5_lambda_re, s5_lambda_im, s5_log_dt, s5_b_re, s5_b_im, s5_c_re, s5_c_im, s5_d, s5_glu_wa, s5_glu_wb, loss_target, m_ln_ffn_pre, m_ln_mix, m_ln_ffn_post, m_ln_final, m_ffn_w1, m_ffn_w3, m_ffn_w2, m_ab_w_in, m_ab_conv_w, m_ab_w_out, m_s5_lambda_re, m_s5_lambda_im, m_s5_log_dt, m_s5_b_re, m_s5_b_im, m_s5_c_re, m_s5_c_im, m_s5_d, m_s5_glu_wa, m_s5_glu_wb, v_ln_ffn_pre, v_ln_mix, v_ln_ffn_post, v_ln_final, v_ffn_w1, v_ffn_w3, v_ffn_w2, v_ab_w_in, v_ab_conv_w, v_ab_w_out, v_s5_lambda_re, v_s5_lambda_im, v_s5_log_dt, v_s5_b_re, v_s5_b_im, v_s5_c_re, v_s5_c_im, v_s5_d, v_s5_glu_wa, v_s5_glu_wb):
    w = dict(zip(NAMES, (ln_ffn_pre, ln_mix, ln_ffn_post, ln_final, ffn_w1, ffn_w3, ffn_w2, ab_w_in, ab_conv_w,
                         ab_w_out, s5_lambda_re, s5_lambda_im, s5_log_dt, s5_b_re, s5_b_im, s5_c_re, s5_c_im,
                         s5_d, s5_glu_wa, s5_glu_wb)))
    mom = dict(zip(NAMES, (m_ln_ffn_pre, m_ln_mix, m_ln_ffn_post, m_ln_final, m_ffn_w1, m_ffn_w3, m_ffn_w2,
                           m_ab_w_in, m_ab_conv_w, m_ab_w_out, m_s5_lambda_re, m_s5_lambda_im, m_s5_log_dt,
                           m_s5_b_re, m_s5_b_im, m_s5_c_re, m_s5_c_im, m_s5_d, m_s5_glu_wa, m_s5_glu_wb)))
    var = dict(zip(NAMES, (v_ln_ffn_pre, v_ln_mix, v_ln_ffn_post, v_ln_final, v_ffn_w1, v_ffn_w3, v_ffn_w2,
                           v_ab_w_in, v_ab_conv_w, v_ab_w_out, v_s5_lambda_re, v_s5_lambda_im, v_s5_log_dt,
                           v_s5_b_re, v_s5_b_im, v_s5_c_re, v_s5_c_im, v_s5_d, v_s5_glu_wa, v_s5_glu_wb)))
    nb, seq, D = x.shape
    T = nb * seq
    depth = ln_mix.shape[0]
    xc, yc, cc = _mesh_pos()
    dev = 4 * xc + 2 * yc + cc

    gathered = _all_gather([w[k].astype(BF) for k in BIG] + [w[k] for k in SMALL_SHARDED], "all_gather_weights")
    wg = dict(zip(BIG + SMALL_SHARDED, gathered))
    wg["ab_w_in"] = wg["ab_w_in"].reshape(N_DEV, D, -1)
    wg["ab_w_out"] = wg["ab_w_out"].reshape(-1, D)
    wg["s5_glu_wa"] = wg["s5_glu_wa"].reshape(-1, D)
    wg["s5_glu_wb"] = wg["s5_glu_wb"].reshape(-1, D)
    wg["ab_conv_w"] = wg["ab_conv_w"].reshape(N_DEV, 3, -1).transpose(1, 0, 2).reshape(3, -1)
    wg["s5_d"] = wg["s5_d"].reshape(1, D)
    bq = min(256, seq)
    tabs = _rope_tables(seq) + (_branch_bias(seq // bq, bq),)

    h = x.reshape(T, D)
    saved = []
    for i in range(depth):
        h, s_pre = _ffn_fwd(h, ln_ffn_pre[i:i + 1], wg, i, 0)
        if i % 2 == 0:
            h, s_mix = _mixer_ab_fwd(h, ln_mix[i:i + 1], wg, tabs, nb, seq)
        else:
            h, s_mix = _mixer_s5_fwd(h, ln_mix[i:i + 1], wg, w, nb, seq)
        h, s_post = _ffn_fwd(h, ln_ffn_post[i:i + 1], wg, i, 1)
        saved.append((s_pre, s_mix, s_post))
    dh, dhb, d_ln_final, loss_part = _loss_head(h, ln_final.reshape(1, D), loss_target.reshape(T, D))
    loss = lax.psum(loss_part[0, 0], ("x", "y", "c"))

    bufs = {k: lax.empty((N_DEV,) + w[k].shape, BF) for k in ("ffn_w1", "ffn_w3", "ffn_w2")}
    g_small = {"ln_final": d_ln_final.reshape(D)}
    g_ln = {k: [None] * depth for k in ("ln_ffn_pre", "ln_mix", "ln_ffn_post")}
    g_big = {}
    for i in reversed(range(depth)):
        s_pre, s_mix, s_post = saved[i]
        dh, dhb, g_ln["ln_ffn_post"][i] = _ffn_bwd(dh, dhb, s_post, ln_ffn_post[i:i + 1], wg, bufs, i, 1)
        if i % 2 == 0:
            dh, dhb, g_ln["ln_mix"][i], dwin, dconvw, dwout = _mixer_ab_bwd(
                dh, dhb, s_mix, ln_mix[i:i + 1], wg, tabs, nb, seq)
            g_big["ab_w_in"] = dwin
            g_big["ab_w_out"] = dwout
            g_small["ab_conv_w"] = dconvw
        else:
            dh, dhb, g_ln["ln_mix"][i], dwa, dwb, s5_small = _mixer_s5_bwd(dh, s_mix, ln_mix[i:i + 1], wg)
            g_big["s5_glu_wa"] = dwa
            g_big["s5_glu_wb"] = dwb
            g_small.update(s5_small)
        dh, dhb, g_ln["ln_ffn_pre"][i] = _ffn_bwd(dh, dhb, s_pre, ln_ffn_pre[i:i + 1], wg, bufs, i, 0)
    grad_x = dh.reshape(nb, seq, D)
    for k in g_ln:
        g_small[k] = jnp.concatenate(g_ln[k], axis=0)
    g_big.update(bufs)

    def slabs(k):
        g = g_big[k]
        cols = w[k].shape[-1]
        return g.reshape(N_DEV, -1, cols)

    grads3 = [slabs(k) for k in BIG]
    recv1 = _pair_exchange(grads3, "pair_exchange")
    core = jnp.reshape(cc, (1,)).astype(jnp.int32)
    chip = jnp.reshape(2 * xc + yc, (1,)).astype(jnp.int32)
    parts = [_pair_sum(g, r, core) for g, r in zip(grads3, recv1)]
    recv2 = _chip_exchange(parts, "chip_exchange")
    out = {}
    for k, p, r in zip(BIG, parts, recv2):
        cols = w[k].shape[-1]
        res = _adamw_sharded(w[k].reshape(-1, cols), mom[k].reshape(-1, cols), var[k].reshape(-1, cols), p, r, chip)
        out[k] = [t.reshape(w[k].shape) for t in res]

    small_names = [k for k in NAMES if k not in BIG]
    red = _unpack(_all_reduce_small(_pack([g_small[k] for k in small_names])),
                  [g_small[k].shape for k in small_names])
    g_red = dict(zip(small_names, red))
    cw = w["ab_conv_w"].shape[-1]
    g_red["ab_conv_w"] = lax.dynamic_slice_in_dim(g_red["ab_conv_w"], dev * cw, cw, axis=1)[None]
    dsz = w["s5_d"].shape[-1]
    g_red["s5_d"] = lax.dynamic_slice_in_dim(g_red["s5_d"].reshape(1, -1), dev * dsz, dsz, axis=1)
    shapes = [w[k].shape for k in small_names]
    g_red = {k: g_red[k].reshape(w[k].shape) for k in small_names}
    d_s, m_s, v_s = _adamw_small(_pack([w[k] for k in small_names]), _pack([g_red[k] for k in small_names]),
                                 _pack([mom[k] for k in small_names]), _pack([var[k] for k in small_names]))
    for k, d, nm, nv in zip(small_names, _unpack(d_s, shapes), _unpack(m_s, shapes), _unpack(v_s, shapes)):
        out[k] = [g_red[k], d, nm, nv]

    return (loss, grad_x, *[out[k][0] for k in NAMES], *[out[k][1] for k in NAMES],
            *[out[k][2] for k in NAMES], *[out[k][3] for k in NAMES])
```

```python
import jax
import jax.numpy as jnp
from jax import lax
from jax.experimental import pallas as pl
from jax.experimental.pallas import tpu as pltpu

F32, BF = jnp.float32, jnp.bfloat16
N_DEV = 8
MESH = pl.DeviceIdType.MESH
LANES = 128
SUBLANES = 8
VMEM_LIMIT = 56 * 2 ** 20
ROW_TILE = 512
ELEMS_PER_BLOCK = 256 * 1024
RMS_EPS = 1e-6
ROPE_THETA = 10000.0
NEG_INF = -1e30
S5_STATE = 64
S5_GROUP = 16
GROUPS_PER_BLOCK = LANES // S5_GROUP
STATE_COLS = GROUPS_PER_BLOCK * S5_STATE
DILATED_PATTERN = ((128, 1), (512, 4), (2048, 16))
ADAM_LR, ADAM_B1, ADAM_B2, ADAM_EPS, ADAM_WD, ADAM_STEP = 0.001, 0.9, 0.999, 1e-08, 0.01, 10
GELU_C = 0.7978845608028654
GELU_A = 0.044715


def _cparams(n_grid, vmem=VMEM_LIMIT):
    sem = ("arbitrary",) * n_grid if n_grid else None
    return pltpu.CompilerParams(dimension_semantics=sem, vmem_limit_bytes=vmem)


def _sig(x):
    return 1.0 / (1.0 + jnp.exp(-x))


def _gelu(x):
    return 0.5 * x * (1.0 + jnp.tanh(GELU_C * (x + GELU_A * x * x * x)))


def _gelu_grad(x):
    t = jnp.tanh(GELU_C * (x + GELU_A * x * x * x))
    return 0.5 * (1.0 + t) + 0.5 * x * (1.0 - t * t) * GELU_C * (1.0 + 3.0 * GELU_A * x * x)


def _dot(a, b, dims):
    a = a if a.dtype == BF else a.astype(BF)
    b = b if b.dtype == BF else b.astype(BF)
    return lax.dot_general(a, b, (dims, ((), ())), preferred_element_type=F32)


NN = ((1,), (0,))
NT = ((1,), (1,))
TN = ((0,), (0,))


def _row_block(rows, cols, mult=16):
    cap = max(mult, ELEMS_PER_BLOCK // cols)
    best = None
    for b in range(mult, min(rows, cap) + 1, mult):
        if rows % b == 0:
            best = b
    return rows if best is None else best


class _Comm:
    def __init__(self, ins, outs, sems, start, finish):
        self.ins, self.outs, self.sems, self.start, self.finish = ins, outs, sems, start, finish
        self.results = None


def _call(body, name, grid, in_specs, out_specs, out_shape, args, scratch=(), comm=None):
    in_specs, out_specs, out_shape, scratch = list(in_specs), list(out_specs), list(out_shape), list(scratch)
    if comm is None:
        return pl.pallas_call(body, grid=grid, in_specs=in_specs, out_specs=out_specs, out_shape=out_shape,
                              scratch_shapes=scratch, name=name, compiler_params=_cparams(len(grid)))(*args)
    n_in, n_out, n_sc = len(in_specs), len(out_specs), len(scratch)
    ci, co = len(comm.ins), len(comm.outs)

    def hosted(*refs):
        ins, refs = refs[:n_in], refs[n_in:]
        cins, refs = refs[:ci], refs[ci:]
        outs, refs = refs[:n_out], refs[n_out:]
        couts, refs = refs[:co], refs[co:]
        sc, csems = refs[:n_sc], refs[n_sc:]
        first = last = None
        for d, n in enumerate(grid):
            p = pl.program_id(d)
            first = (p == 0) if first is None else first & (p == 0)
            last = (p == n - 1) if last is None else last & (p == n - 1)

        @pl.when(first)
        def _():
            comm.start(cins, couts, csems)

        body(*ins, *outs, *sc)

        @pl.when(last)
        def _():
            comm.finish(cins, couts, csems)

    any_spec = pl.BlockSpec(memory_space=pl.ANY)
    res = pl.pallas_call(
        hosted, grid=grid, in_specs=in_specs + [any_spec] * ci, out_specs=out_specs + [any_spec] * co,
        out_shape=out_shape + list(comm.outs), scratch_shapes=scratch + list(comm.sems), name=name,
        compiler_params=_cparams(len(grid)))(*args, *comm.ins)
    comm.results = list(res[n_out:])
    return list(res[:n_out])


def _mm(name, grid, operands, pairs, n_acc, acc_shape, extras, outs, epilogue, comm=None):
    nk = grid[2]
    n_op, n_ex, n_out = len(operands), len(extras), len(outs)

    def body(*refs):
        op = refs[:n_op]
        ex = refs[n_op:n_op + n_ex]
        out = refs[n_op + n_ex:n_op + n_ex + n_out]
        acc = refs[n_op + n_ex + n_out:]
        parts = [None] * n_acc
        for ai, bi, dims, ci in pairs:
            d = _dot(op[ai][...], op[bi][...], dims)
            parts[ci] = d if parts[ci] is None else parts[ci] + d
        if nk == 1:
            epilogue(parts, ex, out)
        else:
            k = pl.program_id(2)

            @pl.when(k == 0)
            def _():
                for ci in range(n_acc):
                    acc[ci][...] = parts[ci]

            @pl.when(k > 0)
            def _():
                for ci in range(n_acc):
                    acc[ci][...] += parts[ci]

            @pl.when(k == nk - 1)
            def _():
                epilogue([r[...] for r in acc], ex, out)

    return _call(body, name, grid, [s for _, s in operands] + [s for _, s in extras], [s for _, s in outs],
                 [sh for sh, _ in outs], [a for a, _ in operands] + [a for a, _ in extras],
                 scratch=[pltpu.VMEM(acc_shape, F32) for _ in range(n_acc if nk > 1 else 0)], comm=comm)


def _to_seg(a, seg_len):
    T, D = a.shape
    return a.reshape(SUBLANES, seg_len, D).transpose(1, 0, 2).reshape(T, D)


def _to_tok(a, seg_len):
    T, D = a.shape
    return a.reshape(seg_len, SUBLANES, D).transpose(1, 0, 2).reshape(T, D)


def _rms_fwd(h, gain, out_dtype):
    T, D = h.shape
    bm = min(ROW_TILE, T)

    def body(h_ref, g_ref, o_ref):
        x = h_ref[...]
        r = lax.rsqrt(jnp.mean(x * x, axis=-1, keepdims=True) + RMS_EPS)
        o_ref[...] = (x * r * g_ref[...]).astype(out_dtype)

    row = pl.BlockSpec((bm, D), lambda i: (i, 0))
    return pl.pallas_call(
        body, grid=(T // bm,), in_specs=[row, pl.BlockSpec((1, D), lambda i: (0, 0))],
        out_specs=row, out_shape=jax.ShapeDtypeStruct((T, D), out_dtype), name="rms_fwd",
        compiler_params=_cparams(1))(h, gain)


def _rms_bwd_rows(dn, x, g):
    r = lax.rsqrt(jnp.mean(x * x, axis=-1, keepdims=True) + RMS_EPS)
    xh = x * r
    dng = dn * g
    dx = r * (dng - xh * jnp.mean(dng * xh, axis=-1, keepdims=True))
    return dx, jnp.sum(dn * xh, axis=0, keepdims=True)


def _rms_bwd(dn, h, gain, dh_up):
    T, D = h.shape
    bm = min(ROW_TILE, T)

    def body(dn_ref, h_ref, g_ref, up_ref, dh_ref, dhb_ref, dg_ref):
        dx, dg = _rms_bwd_rows(dn_ref[...], h_ref[...], g_ref[...])
        dh = up_ref[...] + dx
        dh_ref[...] = dh
        dhb_ref[...] = dh.astype(BF)

        @pl.when(pl.program_id(0) == 0)
        def _():
            dg_ref[...] = jnp.zeros_like(dg_ref)

        dg_ref[...] += dg

    row = pl.BlockSpec((bm, D), lambda i: (i, 0))
    vec = pl.BlockSpec((1, D), lambda i: (0, 0))
    return pl.pallas_call(
        body, grid=(T // bm,), in_specs=[row, row, vec, row], out_specs=[row, row, vec],
        out_shape=[jax.ShapeDtypeStruct((T, D), F32), jax.ShapeDtypeStruct((T, D), BF),
                   jax.ShapeDtypeStruct((1, D), F32)],
        name="rms_bwd", compiler_params=_cparams(1))(dn, h, gain, dh_up)


def _loss_head(h, gain, target):
    T, D = h.shape
    bm = min(ROW_TILE, T)

    def body(h_ref, g_ref, t_ref, dh_ref, dhb_ref, dg_ref, loss_ref):
        x = h_ref[...]
        g = g_ref[...]
        r = lax.rsqrt(jnp.mean(x * x, axis=-1, keepdims=True) + RMS_EPS)
        err = x * r * g - t_ref[...]
        part = 0.5 * jnp.sum(jnp.sum(err * err, axis=-1, keepdims=True), axis=0, keepdims=True) / D
        dx, dg = _rms_bwd_rows(err / D, x, g)
        dh_ref[...] = dx
        dhb_ref[...] = dx.astype(BF)

        @pl.when(pl.program_id(0) == 0)
        def _():
            dg_ref[...] = jnp.zeros_like(dg_ref)
            loss_ref[...] = jnp.zeros_like(loss_ref)

        dg_ref[...] += dg
        loss_ref[...] += jnp.broadcast_to(part, loss_ref.shape)

    row = pl.BlockSpec((bm, D), lambda i: (i, 0))
    vec = pl.BlockSpec((1, D), lambda i: (0, 0))
    return pl.pallas_call(
        body, grid=(T // bm,), in_specs=[row, vec, row],
        out_specs=[row, row, vec, pl.BlockSpec((SUBLANES, LANES), lambda i: (0, 0))],
        out_shape=[jax.ShapeDtypeStruct((T, D), F32), jax.ShapeDtypeStruct((T, D), BF),
                   jax.ShapeDtypeStruct((1, D), F32), jax.ShapeDtypeStruct((SUBLANES, LANES), F32)],
        name="loss_head", compiler_params=_cparams(1))(h, gain, target)


def _ffn_up(n, w1g, w3g, comm=None):
    T, D = n.shape
    fs = w1g.shape[-1]
    bm = min(ROW_TILE, T)
    wspec = pl.BlockSpec((None, D, fs), lambda s, i, k: (s, 0, 0))
    ospec = pl.BlockSpec((None, bm, fs), lambda s, i, k: (s, i, 0))

    def epi(accs, ex, outs):
        a1, a3 = accs
        outs[0][...] = a1.astype(BF)
        outs[1][...] = a3.astype(BF)
        outs[2][...] = (a1 * _sig(a1) * a3).astype(BF)

    sh = jax.ShapeDtypeStruct((N_DEV, T, fs), BF)
    return _mm("ffn_up", (N_DEV, T // bm, 1),
               [(n, pl.BlockSpec((bm, D), lambda s, i, k: (i, 0))), (w1g, wspec), (w3g, wspec)],
               [(0, 1, NN, 0), (0, 2, NN, 1)], 2, None, [], [(sh, ospec)] * 3, epi, comm=comm)


def _ffn_down(g, w2g, h, comm=None):
    _, T, fs = g.shape
    D = h.shape[1]
    bm = min(ROW_TILE, T)
    row = pl.BlockSpec((bm, D), lambda i, j, s: (i, 0))

    def epi(accs, ex, outs):
        outs[0][...] = ex[0][...] + 0.5 * accs[0]

    return _mm("ffn_down", (T // bm, 1, N_DEV),
               [(g, pl.BlockSpec((None, bm, fs), lambda i, j, s: (s, i, 0))),
                (w2g, pl.BlockSpec((None, fs, D), lambda i, j, s: (s, 0, 0)))],
               [(0, 1, NN, 0)], 1, (bm, D), [(h, row)],
               [(jax.ShapeDtypeStruct((T, D), F32), row)], epi, comm=comm)[0]


def _ffn_bwd_hidden(dhb, w2g, a1, a3, comm=None):
    T, D = dhb.shape
    fs = a1.shape[-1]
    bm = min(ROW_TILE, T)
    aspec = pl.BlockSpec((None, bm, fs), lambda s, i, k: (s, i, 0))

    def epi(accs, ex, outs):
        dg = 0.5 * accs[0]
        a1v = ex[0][...].astype(F32)
        a3v = ex[1][...].astype(F32)
        sg = _sig(a1v)
        outs[0][...] = (dg * a3v * sg * (1.0 + a1v * (1.0 - sg))).astype(BF)
        outs[1][...] = (dg * a1v * sg).astype(BF)

    sh = jax.ShapeDtypeStruct((N_DEV, T, fs), BF)
    return _mm("ffn_bwd_hidden", (N_DEV, T // bm, 1),
               [(dhb, pl.BlockSpec((bm, D), lambda s, i, k: (i, 0))),
                (w2g, pl.BlockSpec((None, fs, D), lambda s, i, k: (s, 0, 0)))],
               [(0, 1, NT, 0)], 1, None, [(a1, aspec), (a3, aspec)], [(sh, aspec)] * 2, epi, comm=comm)


def _ffn_dw2(g, dhb):
    _, T, fs = g.shape
    D = dhb.shape[1]
    bk = min(ROW_TILE, T)

    def epi(accs, ex, outs):
        outs[0][...] = (0.5 * accs[0]).astype(BF)

    return _mm("ffn_dw2", (N_DEV, 1, T // bk),
               [(g, pl.BlockSpec((None, bk, fs), lambda s, j, t: (s, t, 0))),
                (dhb, pl.BlockSpec((bk, D), lambda s, j, t: (t, 0)))],
               [(0, 1, TN, 0)], 1, (fs, D), [],
               [(jax.ShapeDtypeStruct((N_DEV, fs, D), BF), pl.BlockSpec((None, fs, D), lambda s, j, t: (s, 0, 0)))],
               epi)[0]


def _ffn_dw13(n, da1, da3, comm=None):
    T, D = n.shape
    fs = da1.shape[-1]
    bk = min(ROW_TILE, T)
    dspec = pl.BlockSpec((None, bk, fs), lambda s, j, t: (s, t, 0))
    ospec = pl.BlockSpec((None, D, fs), lambda s, j, t: (s, 0, 0))

    def epi(accs, ex, outs):
        outs[0][...] = accs[0].astype(BF)
        outs[1][...] = accs[1].astype(BF)

    sh = jax.ShapeDtypeStruct((N_DEV, D, fs), BF)
    return _mm("ffn_dw13", (N_DEV, 1, T // bk),
               [(n, pl.BlockSpec((bk, D), lambda s, j, t: (t, 0))), (da1, dspec), (da3, dspec)],
               [(0, 1, TN, 0), (0, 2, TN, 1)], 2, (D, fs), [], [(sh, ospec)] * 2, epi, comm=comm)


def _ffn_dn(da1, da3, w1g, w3g, comm=None):
    _, T, fs = da1.shape
    D = w1g.shape[-2]
    bm = min(ROW_TILE, T)
    dspec = pl.BlockSpec((None, bm, fs), lambda i, j, s: (s, i, 0))
    wspec = pl.BlockSpec((None, D, fs), lambda i, j, s: (s, 0, 0))
    row = pl.BlockSpec((bm, D), lambda i, j, s: (i, 0))

    def epi(accs, ex, outs):
        outs[0][...] = accs[0]

    return _mm("ffn_dn", (T // bm, 1, N_DEV),
               [(da1, dspec), (w1g, wspec), (da3, dspec), (w3g, wspec)],
               [(0, 1, NT, 0), (2, 3, NT, 0)], 1, (bm, D), [],
               [(jax.ShapeDtypeStruct((T, D), F32), row)], epi, comm=comm)[0]


def _rope_tables(seq):
    half = LANES // 2
    inv = ROPE_THETA ** (-jnp.arange(0, half, dtype=F32) * 2.0 / LANES)
    ang = jnp.arange(seq, dtype=F32)[:, None] * inv[None, :]
    cos, sin = jnp.cos(ang), jnp.sin(ang)
    return jnp.concatenate([cos, cos], axis=1), jnp.concatenate([-sin, sin], axis=1)


def _branch_bias(nq, bq):
    d = (jnp.arange(nq)[:, None, None] * bq + jnp.arange(bq)[None, :, None]
         - jnp.arange(bq)[None, None, :])
    mult = jnp.zeros(d.shape, F32)
    for window, dil in DILATED_PATTERN:
        mult = mult + ((d >= 0) & (d % dil == 0) & (d <= window)).astype(F32)
    return jnp.where(mult > 0, jnp.log(jnp.maximum(mult, 1.0)), NEG_INF)


def _proj_fwd(u, wing, comm=None):
    T, D = u.shape
    ws = wing.shape[-1]
    bm = min(ROW_TILE, T)

    def epi(accs, ex, outs):
        outs[0][...] = accs[0]

    return _mm("proj_fwd", (N_DEV, T // bm, 1),
               [(u, pl.BlockSpec((bm, D), lambda s, i, k: (i, 0))),
                (wing, pl.BlockSpec((None, D, ws), lambda s, i, k: (s, 0, 0)))],
               [(0, 1, NN, 0)], 1, None, [],
               [(jax.ShapeDtypeStruct((T, N_DEV * ws), F32),
                 pl.BlockSpec((bm, ws), lambda s, i, k: (i, s)))], epi, comm=comm)[0]


def _rope_fwd(proj, cosf, sinf, seq, nh):
    T = proj.shape[0]
    bs = min(ROW_TILE, seq)
    nst = seq // bs
    scale = LANES ** -0.5

    def body(x_ref, c_ref, s_ref, o_ref):
        j = pl.program_id(1)
        t = x_ref[...]
        rot = t * c_ref[...] + pltpu.roll(t, LANES // 2, 1) * s_ref[...]
        rot = rot * jnp.where(j < nh, scale, 1.0)
        o_ref[...] = jnp.where(j < 2 * nh, rot, t).astype(BF)

    blk = pl.BlockSpec((bs, LANES), lambda r, j: (r, j))
    tab = pl.BlockSpec((bs, LANES), lambda r, j: (r % nst, 0))
    return pl.pallas_call(
        body, grid=(T // bs, 3 * nh), in_specs=[blk, tab, tab], out_specs=blk,
        out_shape=jax.ShapeDtypeStruct((T, 3 * nh * LANES), BF), name="rope_fwd",
        compiler_params=_cparams(2))(proj, cosf, sinf)


def _attn_fwd(qkv, bias, nb, seq, nh, comm=None):
    T = nb * seq
    bq = bias.shape[1]
    nq = seq // bq

    def body(q_ref, k_ref, v_ref, b_ref, o_ref, lse_ref):
        qi = pl.program_id(2)
        q = q_ref[...]

        def step(kj, carry):
            m, l, acc = carry
            rows = pl.ds(pl.multiple_of(kj * bq, bq), bq)
            s = _dot(q, k_ref[rows, :], NT) + b_ref[qi - kj]
            m_new = jnp.maximum(m, jnp.max(s, axis=1, keepdims=True))
            p = jnp.exp(s - m_new)
            alpha = jnp.exp(m - m_new)
            l = alpha * l + jnp.sum(p, axis=1, keepdims=True)
            acc = alpha * acc + _dot(p, v_ref[rows, :], NN)
            return m_new, l, acc

        init = (jnp.full((bq, 1), NEG_INF, F32), jnp.zeros((bq, 1), F32), jnp.zeros((bq, LANES), F32))
        m, l, acc = lax.fori_loop(0, qi + 1, step, init)
        o_ref[...] = (acc / l).astype(BF)
        lse_ref[...] = m + jnp.log(l)

    return _call(
        body, "attn_fwd", (nb, nh, nq),
        [pl.BlockSpec((bq, LANES), lambda b, h, i: (b * nq + i, h)),
         pl.BlockSpec((seq, LANES), lambda b, h, i: (b, nh + h)),
         pl.BlockSpec((seq, LANES), lambda b, h, i: (b, 2 * nh + h)),
         pl.BlockSpec((nq, bq, bq), lambda b, h, i: (0, 0, 0))],
        [pl.BlockSpec((bq, LANES), lambda b, h, i: (b * nq + i, h)),
         pl.BlockSpec((None, bq, 1), lambda b, h, i: (h, b * nq + i, 0))],
        [jax.ShapeDtypeStruct((T, 2 * nh * LANES), BF), jax.ShapeDtypeStruct((nh, T, 1), F32)],
        (qkv, qkv, qkv, bias), comm=comm)


def _attn_bwd_dq(qkv, cat, dcat, lse, bias, nb, seq, nh, comm=None):
    T = nb * seq
    bq = bias.shape[1]
    nq = seq // bq

    def body(q_ref, k_ref, v_ref, o_ref, do_ref, lse_ref, b_ref, dq_ref, delta_ref):
        qi = pl.program_id(2)
        q = q_ref[...]
        do = do_ref[...]
        dob = do.astype(BF)
        lse_t = lse_ref[...]
        delta = jnp.sum(do * o_ref[...].astype(F32), axis=1, keepdims=True)
        delta_ref[...] = delta

        def step(kj, dq):
            rows = pl.ds(pl.multiple_of(kj * bq, bq), bq)
            k = k_ref[rows, :]
            p = jnp.exp(_dot(q, k, NT) + b_ref[qi - kj] - lse_t)
            ds = p * (_dot(dob, v_ref[rows, :], NT) - delta)
            return dq + _dot(ds, k, NN)

        dq_ref[...] = lax.fori_loop(0, qi + 1, step, jnp.zeros((bq, LANES), F32))

    tile = pl.BlockSpec((bq, LANES), lambda b, h, i: (b * nq + i, h))
    stat = pl.BlockSpec((None, bq, 1), lambda b, h, i: (h, b * nq + i, 0))
    return _call(
        body, "attn_bwd_dq", (nb, nh, nq),
        [tile, pl.BlockSpec((seq, LANES), lambda b, h, i: (b, nh + h)),
         pl.BlockSpec((seq, LANES), lambda b, h, i: (b, 2 * nh + h)), tile, tile, stat,
         pl.BlockSpec((nq, bq, bq), lambda b, h, i: (0, 0, 0))],
        [tile, stat],
        [jax.ShapeDtypeStruct((T, nh * LANES), F32), jax.ShapeDtypeStruct((nh, T, 1), F32)],
        (qkv, qkv, qkv, cat, dcat, lse, bias), comm=comm)


def _attn_bwd_dkv(qkv, dcat, lse, delta, bias, nb, seq, nh):
    T = nb * seq
    bq = bias.shape[1]
    nq = seq // bq

    def body(k_ref, v_ref, q_ref, do_ref, lse_ref, delta_ref, b_ref, dk_ref, dv_ref):
        kj = pl.program_id(2)
        k = k_ref[...]
        v = v_ref[...]

        def step(qi, carry):
            dk, dv = carry
            rows = pl.ds(pl.multiple_of(qi * bq, bq), bq)
            q = q_ref[rows, :]
            dob = do_ref[rows, :].astype(BF)
            p = jnp.exp(_dot(q, k, NT) + b_ref[qi - kj] - lse_ref[rows, :])
            dv = dv + _dot(p, dob, TN)
            ds = p * (_dot(dob, v, NT) - delta_ref[rows, :])
            return dk + _dot(ds, q, TN), dv

        z = jnp.zeros((bq, LANES), F32)
        dk, dv = lax.fori_loop(kj, nq, step, (z, z))
        dk_ref[...] = dk
        dv_ref[...] = dv

    stat = pl.BlockSpec((None, seq, 1), lambda b, h, i: (h, b, 0))
    out = pl.BlockSpec((bq, LANES), lambda b, h, i: (b * nq + i, h))
    sh = jax.ShapeDtypeStruct((T, nh * LANES), F32)
    return pl.pallas_call(
        body, grid=(nb, nh, nq),
        in_specs=[pl.BlockSpec((bq, LANES), lambda b, h, i: (b * nq + i, nh + h)),
                  pl.BlockSpec((bq, LANES), lambda b, h, i: (b * nq + i, 2 * nh + h)),
                  pl.BlockSpec((seq, LANES), lambda b, h, i: (b, h)),
                  pl.BlockSpec((seq, LANES), lambda b, h, i: (b, h)),
                  stat, stat,
                  pl.BlockSpec((nq, bq, bq), lambda b, h, i: (0, 0, 0))],
        out_specs=[out, out], out_shape=[sh, sh],
        name="attn_bwd_dkv", compiler_params=_cparams(3))(qkv, qkv, qkv, dcat, lse, delta, bias)


def _conv_parts(gc, xin, w_ref):
    w = [w_ref[k:k + 1, :] for k in range(3)]
    u = gc * xin
    row = lax.broadcasted_iota(jnp.int32, u.shape, 0)
    u1 = jnp.where(row >= 1, pltpu.roll(u, 1, 0), 0.0)
    u2 = jnp.where(row >= 2, pltpu.roll(u, 2, 0), 0.0)
    return u, u1, u2, w[0] * u2 + w[1] * u1 + w[2] * u, w, row


def _conv_fwd(proj, conv_w, cat, nb, seq, width):
    cw = min(2 * LANES, width)
    nc = width // cw

    def body(gb_ref, gc_ref, x_ref, w_ref, cat_ref, o_ref):
        _, _, _, conv, _, _ = _conv_parts(gc_ref[...], x_ref[...], w_ref)
        o_ref[...] = (gb_ref[...] * conv).astype(BF)

    def sec(k):
        return pl.BlockSpec((seq, cw), lambda b, c: (b, k * nc + c))

    return pl.pallas_call(
        body, grid=(nb, nc),
        in_specs=[sec(3), sec(4), sec(5), pl.BlockSpec((3, cw), lambda b, c: (0, c)),
                  pl.BlockSpec(memory_space=pl.ANY)],
        out_specs=pl.BlockSpec((seq, cw), lambda b, c: (b, nc + c)),
        out_shape=jax.ShapeDtypeStruct(cat.shape, BF), input_output_aliases={4: 0},
        name="conv_fwd", compiler_params=_cparams(2))(proj, proj, proj, conv_w, cat)


def _conv_bwd(proj, conv_w, dcat, nb, seq, width):
    cw = min(2 * LANES, width)
    nc = width // cw
    T = nb * seq

    def body(gb_ref, gc_ref, x_ref, w_ref, d_ref, dgb_ref, dgc_ref, dx_ref, dw_ref):
        gc = gc_ref[...]
        xin = x_ref[...]
        u, u1, u2, conv, w, row = _conv_parts(gc, xin, w_ref)
        dsc = d_ref[...]
        dgb_ref[...] = dsc * conv
        dconv = dsc * gb_ref[...]
        d1 = jnp.where(row < seq - 1, pltpu.roll(dconv, seq - 1, 0), 0.0)
        d2 = jnp.where(row < seq - 2, pltpu.roll(dconv, seq - 2, 0), 0.0)
        du = w[2] * dconv + w[1] * d1 + w[0] * d2
        dgc_ref[...] = du * xin
        dx_ref[...] = du * gc

        @pl.when(pl.program_id(1) == 0)
        def _():
            dw_ref[...] = jnp.zeros_like(dw_ref)

        dw_ref[0:1, :] += jnp.sum(dconv * u2, axis=0, keepdims=True)
        dw_ref[1:2, :] += jnp.sum(dconv * u1, axis=0, keepdims=True)
        dw_ref[2:3, :] += jnp.sum(dconv * u, axis=0, keepdims=True)

    def sec(k):
        return pl.BlockSpec((seq, cw), lambda c, b: (b, k * nc + c))

    out = pl.BlockSpec((seq, cw), lambda c, b: (b, c))
    wsp = pl.BlockSpec((3, cw), lambda c, b: (0, c))
    sh = jax.ShapeDtypeStruct((T, width), F32)
    return pl.pallas_call(
        body, grid=(nc, nb), in_specs=[sec(3), sec(4), sec(5), wsp, sec(1)],
        out_specs=[out, out, out, wsp], out_shape=[sh, sh, sh, jax.ShapeDtypeStruct((3, width), F32)],
        name="conv_bwd", compiler_params=_cparams(2))(proj, proj, proj, conv_w, dcat)


def _assemble_dproj(dq, dk, dv, dgb, dgc, dxin, cosf, sinf, seq):
    T, width = dq.shape
    nh = width // LANES
    bs = min(256, seq)
    nst = seq // bs
    scale = LANES ** -0.5

    def body(dq_ref, dk_ref, dv_ref, dgb_ref, dgc_ref, dx_ref, c_ref, s_ref, o_ref):
        sec = pl.program_id(1)
        c = c_ref[...]
        s = s_ref[...]

        def unrope(ref, mul):
            for h in range(nh):
                cols = slice(h * LANES, (h + 1) * LANES)
                t = ref[:, cols]
                o_ref[:, cols] = ((t * c + pltpu.roll(t * s, LANES // 2, 1)) * mul).astype(BF)

        @pl.when(sec == 0)
        def _():
            unrope(dq_ref, scale)

        @pl.when(sec == 1)
        def _():
            unrope(dk_ref, 1.0)

        for k, ref in ((2, dv_ref), (3, dgb_ref), (4, dgc_ref), (5, dx_ref)):
            @pl.when(sec == k)
            def _(ref=ref):
                o_ref[...] = ref[...].astype(BF)

    blk = pl.BlockSpec((bs, width), lambda r, k: (r, 0))
    tab = pl.BlockSpec((bs, LANES), lambda r, k: (r % nst, 0))
    return pl.pallas_call(
        body, grid=(T // bs, 6), in_specs=[blk] * 6 + [tab, tab],
        out_specs=pl.BlockSpec((bs, width), lambda r, k: (r, k)),
        out_shape=jax.ShapeDtypeStruct((T, 6 * width), BF), name="assemble_dproj",
        compiler_params=_cparams(2))(dq, dk, dv, dgb, dgc, dxin, cosf, sinf)


def _res_mm(name, a, w, h):
    T, K = a.shape
    N = w.shape[1]
    bm = min(ROW_TILE, T)
    bk = min(ROW_TILE, K)
    row = pl.BlockSpec((bm, N), lambda i, j, k: (i, 0))

    def epi(accs, ex, outs):
        outs[0][...] = ex[0][...] + accs[0]

    return _mm(name, (T // bm, 1, K // bk),
               [(a, pl.BlockSpec((bm, bk), lambda i, j, k: (i, k))),
                (w, pl.BlockSpec((bk, N), lambda i, j, k: (k, 0)))],
               [(0, 1, NN, 0)], 1, (bm, N), [(h, row)],
               [(jax.ShapeDtypeStruct((T, N), F32), row)], epi)[0]


def _mm_nt(name, a, w, out_dtype):
    T, K = a.shape
    N = w.shape[0]
    bm = min(ROW_TILE, T)
    bn = min(ROW_TILE, N)

    def epi(accs, ex, outs):
        outs[0][...] = accs[0].astype(out_dtype)

    return _mm(name, (T // bm, N // bn, 1),
               [(a, pl.BlockSpec((bm, K), lambda i, j, k: (i, 0))),
                (w, pl.BlockSpec((bn, K), lambda i, j, k: (j, 0)))],
               [(0, 1, NT, 0)], 1, None, [],
               [(jax.ShapeDtypeStruct((T, N), out_dtype), pl.BlockSpec((bm, bn), lambda i, j, k: (i, j)))],
               epi)[0]


def _mm_tn(name, a, bs_list):
    T, M = a.shape
    N = bs_list[0].shape[1]
    bk = min(ROW_TILE, T)
    bmr = min(ROW_TILE, M)
    n = len(bs_list)

    def epi(accs, ex, outs):
        for q in range(n):
            outs[q][...] = accs[q].astype(BF)

    ops = [(a, pl.BlockSpec((bk, bmr), lambda r, j, t: (t, r)))]
    ops += [(b, pl.BlockSpec((bk, N), lambda r, j, t: (t, 0))) for b in bs_list]
    return _mm(name, (M // bmr, 1, T // bk), ops, [(0, 1 + q, TN, q) for q in range(n)], n, (bmr, N), [],
               [(jax.ShapeDtypeStruct((M, N), BF), pl.BlockSpec((bmr, N), lambda r, j, t: (r, 0)))] * n, epi)


def _proj_bwd_x(dproj, wing):
    T = dproj.shape[0]
    _, D, ws = wing.shape
    bm = min(ROW_TILE, T)
    row = pl.BlockSpec((bm, D), lambda i, j, s: (i, 0))

    def epi(accs, ex, outs):
        outs[0][...] = accs[0]

    return _mm("proj_bwd_x", (T // bm, 1, N_DEV),
               [(dproj, pl.BlockSpec((bm, ws), lambda i, j, s: (i, s))),
                (wing, pl.BlockSpec((None, D, ws), lambda i, j, s: (s, 0, 0)))],
               [(0, 1, NT, 0)], 1, (bm, D), [], [(jax.ShapeDtypeStruct((T, D), F32), row)], epi)[0]


def _proj_dw(u, dproj, ws):
    T, D = u.shape
    bk = min(ROW_TILE, T)

    def epi(accs, ex, outs):
        outs[0][...] = accs[0].astype(BF)

    return _mm("proj_dw", (N_DEV, 1, T // bk),
               [(u, pl.BlockSpec((bk, D), lambda s, j, t: (t, 0))),
                (dproj, pl.BlockSpec((bk, ws), lambda s, j, t: (t, s)))],
               [(0, 1, TN, 0)], 1, (D, ws), [],
               [(jax.ShapeDtypeStruct((N_DEV, D, ws), BF),
                 pl.BlockSpec((None, D, ws), lambda s, j, t: (s, 0, 0)))], epi)[0]


def _mixer_ab_fwd(h, gain, wing, conv_w, wout, tabs, nb, seq, comm_proj=None, comm_attn=None):
    cosf, sinf, bias = tabs
    width = wing.shape[-1] * N_DEV // 6
    nh = width // LANES
    u = _rms_fwd(h, gain, BF)
    proj = _proj_fwd(u, wing, comm=comm_proj)
    qkv = _rope_fwd(proj, cosf, sinf, seq, nh)
    cat, lse = _attn_fwd(qkv, bias, nb, seq, nh, comm=comm_attn)
    cat = _conv_fwd(proj, conv_w, cat, nb, seq, width)
    return _res_mm("outproj_fwd", cat, wout, h), (h, u, proj, qkv, cat, lse)


def _mixer_ab_bwd(dh, dhb, saved, gain, wing, conv_w, wout, tabs, nb, seq, reduce_start):
    cosf, sinf, bias = tabs
    h, u, proj, qkv, cat, lse = saved
    D = h.shape[1]
    ws = wing.shape[-1]
    width = ws * N_DEV // 6
    nh = width // LANES
    dcat = _mm_nt("outproj_bwd_x", dhb, wout, F32)
    dwout = _mm_tn("outproj_dw", cat, [dhb])[0]
    comm = reduce_start(["ab_w_out"], [dwout.reshape(N_DEV, -1, D)])
    dq, delta = _attn_bwd_dq(qkv, cat, dcat, lse, bias, nb, seq, nh, comm=comm)
    dk, dv = _attn_bwd_dkv(qkv, dcat, lse, delta, bias, nb, seq, nh)
    dgb, dgc, dxin, dconvw = _conv_bwd(proj, conv_w, dcat, nb, seq, width)
    dproj = _assemble_dproj(dq, dk, dv, dgb, dgc, dxin, cosf, sinf, seq)
    du = _proj_bwd_x(dproj, wing)
    comm = reduce_start(["ab_w_in"], [_proj_dw(u, dproj, ws)])
    dh_in, dhb_in, dgain = _rms_bwd(du, h, gain, dh)
    return dh_in, dhb_in, dgain, dconvw, comm


def _s5_zoh(lr, li, log_dt):
    dt = jnp.exp(log_dt)
    mag = jnp.exp(lr * dt)
    ar = mag * jnp.cos(li * dt)
    ai = mag * jnp.sin(li * dt)
    den = lr * lr + li * li
    return dt, ar, ai, den, ((ar - 1.0) * lr + ai * li) / den, (ai * lr - (ar - 1.0) * li) / den


def _s5_discretize(lam_re, lam_im, log_dt, bt_re, bt_im):
    def body(lr_ref, li_ref, ld_ref, br_ref, bi_ref, ar_ref, ai_ref, bbr_ref, bbi_ref):
        _, ar, ai, _, fr, fi = _s5_zoh(lr_ref[...], li_ref[...], ld_ref[...])
        ar_ref[...] = ar
        ai_ref[...] = ai
        bbr_ref[...] = fr * br_ref[...] - fi * bi_ref[...]
        bbi_ref[...] = fr * bi_ref[...] + fi * br_ref[...]

    small = jax.ShapeDtypeStruct(lam_re.shape, F32)
    big = jax.ShapeDtypeStruct(bt_re.shape, F32)
    return pl.pallas_call(body, out_shape=[small, small, big, big], name="s5_discretize",
                          compiler_params=_cparams(0))(lam_re, lam_im, log_dt, bt_re, bt_im)


def _s5_discretize_bwd(lam_re, lam_im, log_dt, bt_re, bt_im, d_ar, d_ai, d_bbr, d_bbi):

    def body(lr_ref, li_ref, ld_ref, br_ref, bi_ref, dar_ref, dai_ref, dbbr_ref, dbbi_ref,
             dlr_ref, dli_ref, dld_ref, dbr_ref, dbi_ref):
        lr, li = lr_ref[...], li_ref[...]
        dt, ar, ai, den, fr, fi = _s5_zoh(lr, li, ld_ref[...])
        br, bi = br_ref[...], bi_ref[...]
        dbbr, dbbi = dbbr_ref[...], dbbi_ref[...]
        dbr_ref[...] = dbbr * fr + dbbi * fi
        dbi_ref[...] = dbbi * fr - dbbr * fi
        dfr = jnp.sum(dbbr * br + dbbi * bi, axis=1, keepdims=True)
        dfi = jnp.sum(dbbi * br - dbbr * bi, axis=1, keepdims=True)
        dnr = dfr / den
        dni = dfi / den
        dden = -(dfr * fr + dfi * fi) / den
        dar = dar_ref[...] + dnr * lr - dni * li
        dai = dai_ref[...] + dnr * li + dni * lr
        dlr_ref[...] = dnr * (ar - 1.0) + dni * ai + 2.0 * dden * lr + dt * (dar * ar + dai * ai)
        dli_ref[...] = dnr * ai - dni * (ar - 1.0) + 2.0 * dden * li + dt * (dai * ar - dar * ai)
        ddt = jnp.sum(dar * (lr * ar - li * ai) + dai * (lr * ai + li * ar), axis=2, keepdims=True)
        dld_ref[...] = ddt * dt

    small = jax.ShapeDtypeStruct(lam_re.shape, F32)
    big = jax.ShapeDtypeStruct(bt_re.shape, F32)
    return pl.pallas_call(
        body, out_shape=[small, small, jax.ShapeDtypeStruct(log_dt.shape, F32), big, big],
        name="s5_discretize_bwd", compiler_params=_cparams(0))(
            lam_re, lam_im, log_dt, bt_re, bt_im, d_ar, d_ai, d_bbr, d_bbi)


def _rows8(t):
    return pl.ds(pl.multiple_of(t * SUBLANES, SUBLANES), SUBLANES)


def _cmul_add(ar, ai, sr, si, br, bi):
    return ar * sr - ai * si + br, ar * si + ai * sr + bi


def _s5_specs(R, nj):
    sh = STATE_COLS // 2
    return dict(
        rows=pl.BlockSpec((R, LANES), lambda j, hh: (0, j)),
        bd=pl.BlockSpec((None, LANES, sh), lambda j, hh: (j, 0, hh)),
        cd=pl.BlockSpec((None, sh, LANES), lambda j, hh: (j, hh, 0)),
        a=pl.BlockSpec((None, 1, sh), lambda j, hh: (j, 0, hh)),
        vec=pl.BlockSpec((1, LANES), lambda j, hh: (0, j)),
        init=pl.BlockSpec((None, SUBLANES, sh), lambda j, hh: (j, 0, hh)))


def _s5_fwd(u, mats, seg_len, nseg, comm=None):
    bdr, bdi, cdr, cdi, are, aim, dsk = mats
    R, D = u.shape
    nj = D // LANES
    sh = STATE_COLS // 2
    rc = min(R, 1024)
    sp = _s5_specs(R, nj)

    def body(u_ref, bdr_ref, bdi_ref, cdr_ref, cdi_ref, ar_ref, ai_ref, d_ref,
             y_ref, yg_ref, ir_ref, ii_ref, sre, sim):
        hh = pl.program_id(1)
        ar = jnp.broadcast_to(ar_ref[...], (SUBLANES, sh))
        ai = jnp.broadcast_to(ai_ref[...], (SUBLANES, sh))

        def bu_chunk(c, _):
            rows = pl.ds(pl.multiple_of(c * rc, rc), rc)
            ub = u_ref[rows, :].astype(BF)
            sre[rows, :] = _dot(ub, bdr_ref[...], NN)
            sim[rows, :] = _dot(ub, bdi_ref[...], NN)
            return 0

        lax.fori_loop(0, R // rc, bu_chunk, 0)
        z = jnp.zeros((SUBLANES, sh), F32)

        def local_scan(t, c):
            return _cmul_add(ar, ai, c[0], c[1], sre[_rows8(t), :], sim[_rows8(t), :])

        er, ei = lax.fori_loop(0, seg_len, local_scan, (z, z))
        pr, pi = lax.fori_loop(0, seg_len - 1, lambda _, c: _cmul_add(ar, ai, c[0], c[1], 0.0, 0.0), (ar, ai))
        first = (lax.broadcasted_iota(jnp.int32, (SUBLANES, sh), 0) & (nseg - 1)) == 0

        def prev(x):
            return jnp.where(first, 0.0, pltpu.roll(x, 1, 0))

        xr, xi = er, ei
        for _ in range(nseg - 1):
            xr, xi = _cmul_add(pr, pi, prev(xr), prev(xi), er, ei)
        i_r, i_i = prev(xr), prev(xi)
        ir_ref[...] = i_r
        ii_ref[...] = i_i

        def scan(t, c):
            nr, ni = _cmul_add(ar, ai, c[0], c[1], sre[_rows8(t), :], sim[_rows8(t), :])
            sre[_rows8(t), :] = nr
            sim[_rows8(t), :] = ni
            return nr, ni

        lax.fori_loop(0, seg_len, scan, (i_r, i_i))

        def y_chunk(c, _):
            rows = pl.ds(pl.multiple_of(c * rc, rc), rc)
            y = _dot(sre[rows, :], cdr_ref[...], NN) + _dot(sim[rows, :], cdi_ref[...], NN)

            @pl.when(hh == 0)
            def _():
                y_ref[rows, :] = y + d_ref[...] * u_ref[rows, :]

            @pl.when(hh == 1)
            def _():
                yt = y_ref[rows, :] + y
                y_ref[rows, :] = yt
                yg_ref[rows, :] = _gelu(yt).astype(BF)

            return 0

        lax.fori_loop(0, R // rc, y_chunk, 0)

    init_sh = jax.ShapeDtypeStruct((nj, SUBLANES, STATE_COLS), F32)
    return _call(
        body, "s5_fwd", (nj, 2),
        [sp["rows"], sp["bd"], sp["bd"], sp["cd"], sp["cd"], sp["a"], sp["a"], sp["vec"]],
        [sp["rows"], sp["rows"], sp["init"], sp["init"]],
        [jax.ShapeDtypeStruct((R, D), F32), jax.ShapeDtypeStruct((R, D), BF), init_sh, init_sh],
        (u, bdr, bdi, cdr, cdi, are, aim, dsk),
        scratch=[pltpu.VMEM((R, sh), F32) for _ in range(2)], comm=comm)


def _s5_bwd(u, dy, mats, init_re, init_im, seg_len, nseg, comm=None):
    bdr, bdi, cdr, cdi, are, aim, dsk = mats
    R, D = u.shape
    nj = D // LANES
    sh = STATE_COLS // 2
    rc = min(R, 1024)
    sp = _s5_specs(R, nj)

    def body(u_ref, dy_ref, bdr_ref, bdi_ref, cdr_ref, cdi_ref, ar_ref, ai_ref, d_ref, ir_ref, ii_ref,
             du_ref, dbdr_ref, dbdi_ref, dcdr_ref, dcdi_ref, dar_ref, dai_ref, dd_ref,
             sre, sim, gre, gim):
        hh = pl.program_id(1)
        ar = jnp.broadcast_to(ar_ref[...], (SUBLANES, sh))
        ai = jnp.broadcast_to(ai_ref[...], (SUBLANES, sh))
        i_r, i_i = ir_ref[...], ii_ref[...]

        def chunk(c):
            return pl.ds(pl.multiple_of(c * rc, rc), rc)

        def bu_chunk(c, _):
            ub = u_ref[chunk(c), :].astype(BF)
            sre[chunk(c), :] = _dot(ub, bdr_ref[...], NN)
            sim[chunk(c), :] = _dot(ub, bdi_ref[...], NN)
            return 0

        lax.fori_loop(0, R // rc, bu_chunk, 0)

        def scan(t, c):
            nr, ni = _cmul_add(ar, ai, c[0], c[1], sre[_rows8(t), :], sim[_rows8(t), :])
            sre[_rows8(t), :] = nr
            sim[_rows8(t), :] = ni
            return nr, ni

        lax.fori_loop(0, seg_len, scan, (i_r, i_i))

        def c_chunk(c, carry):
            dyb = dy_ref[chunk(c), :].astype(BF)
            gre[chunk(c), :] = _dot(dyb, cdr_ref[...], NT)
            gim[chunk(c), :] = _dot(dyb, cdi_ref[...], NT)
            return (carry[0] + _dot(sre[chunk(c), :], dyb, TN), carry[1] + _dot(sim[chunk(c), :], dyb, TN))

        zc = jnp.zeros((sh, LANES), F32)
        dcr, dci = lax.fori_loop(0, R // rc, c_chunk, (zc, zc))
        dcdr_ref[...] = dcr
        dcdi_ref[...] = dci

        def adj(t, gr_next, gi_next):
            return _cmul_add(ar, -ai, gr_next, gi_next, gre[_rows8(t), :], gim[_rows8(t), :])

        z = jnp.zeros((SUBLANES, sh), F32)
        fr, fi = lax.fori_loop(0, seg_len, lambda i, c: adj(seg_len - 1 - i, c[0], c[1]), (z, z))
        pr, pi = lax.fori_loop(0, seg_len - 1, lambda _, c: _cmul_add(ar, ai, c[0], c[1], 0.0, 0.0), (ar, ai))
        last = (lax.broadcasted_iota(jnp.int32, (SUBLANES, sh), 0) & (nseg - 1)) == nseg - 1

        def nxt(x):
            return jnp.where(last, 0.0, pltpu.roll(x, SUBLANES - 1, 0))

        xr, xi = fr, fi
        for _ in range(nseg - 1):
            xr, xi = _cmul_add(pr, -pi, nxt(xr), nxt(xi), fr, fi)
        g0r, g0i = nxt(xr), nxt(xi)

        def adj_scan(i, c):
            t = seg_len - 1 - i
            gr, gi = adj(t, c[0], c[1])
            gre[_rows8(t), :] = gr
            gim[_rows8(t), :] = gi
            spr, spi = sre[_rows8(t - 1), :], sim[_rows8(t - 1), :]
            return gr, gi, c[2] + spr * gr + spi * gi, c[3] + spr * gi - spi * gr

        gr, gi, dar, dai = lax.fori_loop(0, seg_len - 1, adj_scan, (g0r, g0i, z, z))
        gr, gi = adj(0, gr, gi)
        gre[_rows8(0), :] = gr
        gim[_rows8(0), :] = gi
        dar_ref[...] = jnp.sum(dar + i_r * gr + i_i * gi, axis=0, keepdims=True)
        dai_ref[...] = jnp.sum(dai + i_r * gi - i_i * gr, axis=0, keepdims=True)

        def d_chunk(c, carry):
            ub = u_ref[chunk(c), :].astype(BF)
            grb = gre[chunk(c), :].astype(BF)
            gib = gim[chunk(c), :].astype(BF)
            du = _dot(grb, bdr_ref[...], NT) + _dot(gib, bdi_ref[...], NT)

            @pl.when(hh == 0)
            def _():
                du_ref[chunk(c), :] = du + d_ref[...] * dy_ref[chunk(c), :]

            @pl.when(hh == 1)
            def _():
                du_ref[chunk(c), :] += du

            dd = carry[2] + jnp.sum(dy_ref[chunk(c), :] * u_ref[chunk(c), :], axis=0, keepdims=True)
            return carry[0] + _dot(ub, grb, TN), carry[1] + _dot(ub, gib, TN), dd

        zb = jnp.zeros((LANES, sh), F32)
        dbr, dbi, dd = lax.fori_loop(0, R // rc, d_chunk, (zb, zb, jnp.zeros((1, LANES), F32)))
        dbdr_ref[...] = dbr
        dbdi_ref[...] = dbi
        dd_ref[...] = dd

    bd_sh = jax.ShapeDtypeStruct((nj, LANES, STATE_COLS), F32)
    cd_sh = jax.ShapeDtypeStruct((nj, STATE_COLS, LANES), F32)
    a_sh = jax.ShapeDtypeStruct((nj, 1, STATE_COLS), F32)
    return _call(
        body, "s5_bwd", (nj, 2),
        [sp["rows"], sp["rows"], sp["bd"], sp["bd"], sp["cd"], sp["cd"], sp["a"], sp["a"],
         sp["vec"], sp["init"], sp["init"]],
        [sp["rows"], sp["bd"], sp["bd"], sp["cd"], sp["cd"], sp["a"], sp["a"], sp["vec"]],
        [jax.ShapeDtypeStruct((R, D), F32), bd_sh, bd_sh, cd_sh, cd_sh, a_sh, a_sh,
         jax.ShapeDtypeStruct((1, D), F32)],
        (u, dy, bdr, bdi, cdr, cdi, are, aim, dsk, init_re, init_im),
        scratch=[pltpu.VMEM((R, sh), F32) for _ in range(4)], comm=comm)


def _glu_fwd(yg, wa, wb, h):
    T, D = yg.shape
    N = wa.shape[1]
    bm = min(ROW_TILE, T)
    bn = min(ROW_TILE, N)
    wspec = pl.BlockSpec((D, bn), lambda i, j, k: (0, j))
    ospec = pl.BlockSpec((bm, bn), lambda i, j, k: (i, j))

    def epi(accs, ex, outs):
        pa, pb = accs
        outs[0][...] = ex[0][...] + pa * _sig(pb)
        outs[1][...] = pa.astype(BF)
        outs[2][...] = pb.astype(BF)

    return _mm("glu_fwd", (T // bm, N // bn, 1),
               [(yg, pl.BlockSpec((bm, D), lambda i, j, k: (i, 0))), (wa, wspec), (wb, wspec)],
               [(0, 1, NN, 0), (0, 2, NN, 1)], 2, None, [(h, ospec)],
               [(jax.ShapeDtypeStruct((T, N), F32), ospec), (jax.ShapeDtypeStruct((T, N), BF), ospec),
                (jax.ShapeDtypeStruct((T, N), BF), ospec)], epi)


def _glu_bwd_gates(dz, pa, pb):
    T, D = dz.shape
    bm = min(ROW_TILE, T)

    def body(dz_ref, pa_ref, pb_ref, dpa_ref, dpb_ref):
        dz = dz_ref[...]
        sg = _sig(pb_ref[...].astype(F32))
        dpa_ref[...] = (dz * sg).astype(BF)
        dpb_ref[...] = (dz * pa_ref[...].astype(F32) * sg * (1.0 - sg)).astype(BF)

    row = pl.BlockSpec((bm, D), lambda i: (i, 0))
    return pl.pallas_call(
        body, grid=(T // bm,), in_specs=[row] * 3, out_specs=[row] * 2,
        out_shape=[jax.ShapeDtypeStruct((T, D), BF)] * 2, name="glu_bwd_gates",
        compiler_params=_cparams(1))(dz, pa, pb)


def _glu_bwd_y(dpa, dpb, wa, wb, y_pre):
    T, N = dpa.shape
    D = wa.shape[0]
    bm = min(ROW_TILE, T)
    bn = min(ROW_TILE, D)
    aspec = pl.BlockSpec((bm, N), lambda i, j, k: (i, 0))
    wspec = pl.BlockSpec((bn, N), lambda i, j, k: (j, 0))
    ospec = pl.BlockSpec((bm, bn), lambda i, j, k: (i, j))

    def epi(accs, ex, outs):
        outs[0][...] = accs[0] * _gelu_grad(ex[0][...])

    return _mm("glu_bwd_y", (T // bm, D // bn, 1), [(dpa, aspec), (wa, wspec), (dpb, aspec), (wb, wspec)],
               [(0, 1, NT, 0), (2, 3, NT, 0)], 1, None, [(y_pre, ospec)],
               [(jax.ShapeDtypeStruct((T, D), F32), ospec)], epi)[0]


def _block_diag_in(x, nj):
    g = GROUPS_PER_BLOCK
    x = x.reshape(nj, g, 1, S5_GROUP, S5_STATE)
    eye = jnp.eye(g, dtype=bool)[None, :, :, None, None]
    full = jnp.where(eye, x, 0.0)
    return full.transpose(0, 1, 3, 2, 4).reshape(nj, g * S5_GROUP, g * S5_STATE)


def _block_diag_out(x, nj):
    return _block_diag_in(x, nj).transpose(0, 2, 1)


def _diag_of_in(m, nj):
    g = GROUPS_PER_BLOCK
    m5 = m.reshape(nj, g, S5_GROUP, g, S5_STATE)
    d = jnp.diagonal(m5, axis1=1, axis2=3)
    return d.transpose(0, 3, 1, 2).reshape(nj * g, S5_GROUP, S5_STATE)


def _mixer_s5_fwd(h, gain, p, dsk, wa, wb, nb, seq, comm_s5=None):
    T, D = h.shape
    nj = D // LANES
    nseg = SUBLANES // nb
    seg_len = seq // nseg
    G = p["s5_lambda_re"].shape[1]
    lam_re = p["s5_lambda_re"].reshape(G, 1, S5_STATE)
    lam_im = p["s5_lambda_im"].reshape(G, 1, S5_STATE)
    log_dt = p["s5_log_dt"].reshape(G, 1, 1)
    bt_re = p["s5_b_re"][0].transpose(0, 2, 1)
    bt_im = p["s5_b_im"][0].transpose(0, 2, 1)
    ar, ai, bbr, bbi = _s5_discretize(lam_re, lam_im, log_dt, bt_re, bt_im)
    mats = (_block_diag_in(bbr, nj).astype(BF), _block_diag_in(bbi, nj).astype(BF),
            _block_diag_out(p["s5_c_re"][0], nj).astype(BF),
            _block_diag_out(-p["s5_c_im"][0], nj).astype(BF),
            ar.reshape(nj, 1, STATE_COLS), ai.reshape(nj, 1, STATE_COLS), dsk)
    h_seg = _to_seg(h, seg_len)
    u = _rms_fwd(h_seg, gain, F32)
    y_pre, yg, init_re, init_im = _s5_fwd(u, mats, seg_len, nseg, comm=comm_s5)
    h_out, pa, pb = _glu_fwd(yg, wa, wb, h_seg)
    disc_in = (lam_re, lam_im, log_dt, bt_re, bt_im)
    return _to_tok(h_out, seg_len), (h_seg, u, mats, y_pre, yg, init_re, init_im, pa, pb, disc_in, seg_len, nseg)


def _mixer_s5_bwd(dh, saved, gain, wa, wb, reduce_start):
    h_seg, u, mats, y_pre, yg, init_re, init_im, pa, pb, disc_in, seg_len, nseg = saved
    T, D = h_seg.shape
    nj = D // LANES
    G = nj * GROUPS_PER_BLOCK
    dh_seg = _to_seg(dh, seg_len)
    dpa, dpb = _glu_bwd_gates(dh_seg, pa, pb)
    dy = _glu_bwd_y(dpa, dpb, wa, wb, y_pre)
    dwa, dwb = _mm_tn("glu_dw", yg, [dpa, dpb])
    comm = reduce_start(["s5_glu_wa", "s5_glu_wb"], [dwa.reshape(N_DEV, -1, D), dwb.reshape(N_DEV, -1, D)])
    du, dbdr, dbdi, dcdr, dcdi, dar, dai, dd = _s5_bwd(u, dy, mats, init_re, init_im, seg_len, nseg, comm=comm)
    d_bbr = _diag_of_in(dbdr, nj)
    d_bbi = _diag_of_in(dbdi, nj)
    d_c_re = _diag_of_in(dcdr.transpose(0, 2, 1), nj)
    d_c_im = -_diag_of_in(dcdi.transpose(0, 2, 1), nj)
    dlr, dli, dld, dbr, dbi = _s5_discretize_bwd(
        *disc_in, dar.reshape(G, 1, S5_STATE), dai.reshape(G, 1, S5_STATE), d_bbr, d_bbi)
    small = {"s5_lambda_re": dlr.reshape(1, G, S5_STATE), "s5_lambda_im": dli.reshape(1, G, S5_STATE),
             "s5_log_dt": dld.reshape(1, G),
             "s5_b_re": dbr.transpose(0, 2, 1)[None], "s5_b_im": dbi.transpose(0, 2, 1)[None],
             "s5_c_re": d_c_re[None], "s5_c_im": d_c_im[None], "s5_d": dd}
    dh_in, _, dgain = _rms_bwd(du, h_seg, gain, dh_seg)
    dh_in = _to_tok(dh_in, seg_len)
    return dh_in, dh_in.astype(BF), dgain, small


def _mesh_pos():
    return lax.axis_index("x"), lax.axis_index("y"), lax.axis_index("c")


class _Gather:
    def __init__(self, srcs, slots, send_sems, recv_sems):
        self.srcs, self.slots, self.send_sems, self.recv_sems = srcs, slots, send_sems, recv_sems
        x, y, c = _mesh_pos()
        self.c = c
        self.me, self.sib = (x, y, c), (x, y, 1 - c)
        self.chips = [(1 - x, y), (x, 1 - y), (1 - x, 1 - y)]

    def copy(self, a, k, block, to, own=False):
        dst = self.slots[a].at[4 * block[0] + 2 * block[1] + block[2]]
        return pltpu.make_async_remote_copy(
            src_ref=self.srcs[a] if own else dst, dst_ref=dst, send_sem=self.send_sems.at[7 * a + k],
            recv_sem=self.recv_sems.at[7 * a + k], device_id=to, device_id_type=MESH)

    def own_copies(self, a):
        cps = [self.copy(a, 0, self.me, self.sib, own=True)]
        return cps + [self.copy(a, 1 + j, self.me, (*chip, self.c), own=True) for j, chip in enumerate(self.chips)]

    def start(self):
        for a in range(len(self.srcs)):
            for cp in self.own_copies(a):
                cp.start()

    def finish(self):
        n = len(self.srcs)
        for a in range(n):
            for j, chip in enumerate(self.chips):
                self.copy(a, 1 + j, (*chip, self.c), self.me).wait_recv()
                self.copy(a, 4 + j, (*chip, self.c), self.sib).start()
        for a in range(n):
            self.copy(a, 0, self.sib, self.me).wait_recv()
            for j, chip in enumerate(self.chips):
                self.copy(a, 4 + j, (*chip, 1 - self.c), self.me).wait_recv()
        for a in range(n):
            for cp in self.own_copies(a):
                cp.wait_send()
            for j, chip in enumerate(self.chips):
                self.copy(a, 4 + j, (*chip, self.c), self.sib).wait_send()


def _gather_comm(arrs):
    n = len(arrs)

    def local(xs, outs, sems, a):
        x, y, c = _mesh_pos()
        return pltpu.make_async_copy(xs[a], outs[a].at[4 * x + 2 * y + c], sems[2].at[a])

    def start(xs, outs, sems):
        for a in range(n):
            local(xs, outs, sems, a).start()
        _Gather(xs, outs, sems[0], sems[1]).start()

    def finish(xs, outs, sems):
        _Gather(xs, outs, sems[0], sems[1]).finish()
        for a in range(n):
            local(xs, outs, sems, a).wait()

    return _Comm(list(arrs), [jax.ShapeDtypeStruct((N_DEV,) + a.shape, a.dtype) for a in arrs],
                 [pltpu.SemaphoreType.DMA((7 * n,)), pltpu.SemaphoreType.DMA((7 * n,)),
                  pltpu.SemaphoreType.DMA((n,))], start, finish)


def _exchange_comm(parts):
    n = len(parts)

    def copies(ps, outs, sems):
        x, y, c = _mesh_pos()
        cps = []
        for a in range(n):
            for j in range(1, 4):
                to = (jnp.bitwise_xor(x, j // 2), jnp.bitwise_xor(y, j % 2), c)
                cps.append(pltpu.make_async_remote_copy(
                    src_ref=ps[a].at[j], dst_ref=outs[a].at[j - 1], send_sem=sems[0].at[3 * a + j - 1],
                    recv_sem=sems[1].at[3 * a + j - 1], device_id=to, device_id_type=MESH))
        return cps

    def start(ps, outs, sems):
        for cp in copies(ps, outs, sems):
            cp.start()

    def finish(ps, outs, sems):
        for cp in copies(ps, outs, sems):
            cp.wait()

    return _Comm(list(parts), [jax.ShapeDtypeStruct((3,) + p.shape[1:], p.dtype) for p in parts],
                 [pltpu.SemaphoreType.DMA((3 * n,)), pltpu.SemaphoreType.DMA((3 * n,))], start, finish)


def _run_comm(comm, name):
    ci, co = len(comm.ins), len(comm.outs)

    def body(*refs):
        comm.start(refs[:ci], refs[ci:ci + co], refs[ci + co:])
        comm.finish(refs[:ci], refs[ci:ci + co], refs[ci + co:])

    any_spec = pl.BlockSpec(memory_space=pl.ANY)
    comm.results = list(pl.pallas_call(
        body, in_specs=[any_spec] * ci, out_specs=[any_spec] * co, out_shape=list(comm.outs),
        scratch_shapes=list(comm.sems), name=name, compiler_params=_cparams(0))(*comm.ins))


def _pair_exchange(grads, name):
    n = len(grads)

    def body(*refs):
        gs, outs = refs[:n], refs[n:2 * n]
        send_sems, recv_sems = refs[2 * n:]
        x, y, c = _mesh_pos()
        copies = []
        for a in range(n):
            for k in range(4):
                copies.append(pltpu.make_async_remote_copy(
                    src_ref=gs[a].at[2 * k + 1 - c], dst_ref=outs[a].at[k], send_sem=send_sems.at[4 * a + k],
                    recv_sem=recv_sems.at[4 * a + k], device_id=(x, y, 1 - c), device_id_type=MESH))
        for cp in copies:
            cp.start()
        for cp in copies:
            cp.wait()

    any_spec = pl.BlockSpec(memory_space=pl.ANY)
    return pl.pallas_call(
        body, in_specs=[any_spec] * n, out_specs=[any_spec] * n,
        out_shape=[jax.ShapeDtypeStruct((4,) + g.shape[1:], g.dtype) for g in grads],
        scratch_shapes=[pltpu.SemaphoreType.DMA((4 * n,)), pltpu.SemaphoreType.DMA((4 * n,))],
        name=name, compiler_params=_cparams(0))(*grads)


def _pair_sum(grad, recv, pos):
    _, R, C = grad.shape
    br = _row_block(R, C)

    def body(pos_ref, g_ref, r_ref, o_ref):
        o_ref[...] = (g_ref[...].astype(F32) + r_ref[...].astype(F32)).astype(BF)

    def chip(j, p):
        return jnp.bitwise_xor(p[1], j)

    return pl.pallas_call(
        body, grid_spec=pltpu.PrefetchScalarGridSpec(
            num_scalar_prefetch=1, grid=(4, R // br),
            in_specs=[pl.BlockSpec((None, br, C), lambda j, i, p: (2 * chip(j, p) + p[0], i, 0)),
                      pl.BlockSpec((None, br, C), lambda j, i, p: (chip(j, p), i, 0))],
            out_specs=pl.BlockSpec((None, br, C), lambda j, i, p: (j, i, 0))),
        out_shape=jax.ShapeDtypeStruct((4, R, C), BF), name="pair_sum", compiler_params=_cparams(2))(pos, grad, recv)


def _adamw(w, g, m, v):
    m = ADAM_B1 * m + (1.0 - ADAM_B1) * g
    v = ADAM_B2 * v + (1.0 - ADAM_B2) * (g * g)
    m_hat = m / (1.0 - ADAM_B1 ** ADAM_STEP)
    v_hat = v / (1.0 - ADAM_B2 ** ADAM_STEP)
    return -ADAM_LR * (m_hat / (jnp.sqrt(v_hat) + ADAM_EPS) + ADAM_WD * w), m, v


def _adamw_piece(w, m, v, piece, part, recv, bufs):
    _, R, C = w.shape
    br = _row_block(R, C)

    def body(w_ref, m_ref, v_ref, p_ref, r_ref, b0, b1, b2, b3, g_ref, d_ref, nm_ref, nv_ref):
        g = p_ref[...].astype(F32)
        for j in range(3):
            g = g + r_ref[j].astype(F32)
        d, nm, nv = _adamw(w_ref[...], g, m_ref[...], v_ref[...])
        g_ref[...] = g
        d_ref[...] = d
        nm_ref[...] = nm
        nv_ref[...] = nv

    row = pl.BlockSpec((None, br, C), lambda i: (piece, i, 0))
    any_spec = pl.BlockSpec(memory_space=pl.ANY)
    return pl.pallas_call(
        body, grid=(R // br,),
        in_specs=[row, row, row, pl.BlockSpec((None, br, C), lambda i: (0, i, 0)),
                  pl.BlockSpec((3, br, C), lambda i: (0, i, 0))] + [any_spec] * 4,
        out_specs=[row] * 4, out_shape=[jax.ShapeDtypeStruct(w.shape, F32)] * 4,
        input_output_aliases={5: 0, 6: 1, 7: 2, 8: 3}, name="adamw_piece",
        compiler_params=_cparams(1))(w, m, v, part, recv, *bufs)


def _all_reduce_small(x):
    rows = x.shape[0]

    def body(x_ref, o_ref, buf, send_sems, recv_sems):
        xp, yp, cp = _mesh_pos()
        buf[4 * xp + 2 * yp + cp] = x_ref[...]
        gather = _Gather([x_ref], [buf], send_sems, recv_sems)
        gather.start()
        gather.finish()
        acc = buf[0]
        for d in range(1, N_DEV):
            acc = acc + buf[d]
        o_ref[...] = acc

    vm = pl.BlockSpec(memory_space=pltpu.VMEM)
    return pl.pallas_call(
        body, in_specs=[vm], out_specs=vm, out_shape=jax.ShapeDtypeStruct(x.shape, F32),
        scratch_shapes=[pltpu.VMEM((N_DEV, rows, LANES), F32), pltpu.SemaphoreType.DMA((7,)),
                        pltpu.SemaphoreType.DMA((7,))],
        name="all_reduce_small", compiler_params=_cparams(0))(x)


def _adamw_small(w, g, m, v):
    def body(w_ref, g_ref, m_ref, v_ref, d_ref, nm_ref, nv_ref):
        d, nm, nv = _adamw(w_ref[...], g_ref[...], m_ref[...], v_ref[...])
        d_ref[...] = d
        nm_ref[...] = nm
        nv_ref[...] = nv

    sh = jax.ShapeDtypeStruct(w.shape, F32)
    return pl.pallas_call(body, out_shape=[sh] * 3, name="adamw_small", compiler_params=_cparams(0))(w, g, m, v)


def _pack(arrs):
    flat = jnp.concatenate([a.reshape(-1).astype(F32) for a in arrs])
    rows = -(-flat.shape[0] // (SUBLANES * LANES)) * SUBLANES
    return jnp.pad(flat, (0, rows * LANES - flat.shape[0])).reshape(rows, LANES)


def _unpack(buf, shapes):
    flat = buf.reshape(-1)
    out, off = [], 0
    for s in shapes:
        n = 1
        for d in s:
            n *= d
        out.append(flat[off:off + n].reshape(s))
        off += n
    return out


BIG = ("ffn_w1", "ffn_w3", "ffn_w2", "ab_w_in", "ab_w_out", "s5_glu_wa", "s5_glu_wb")
NAMES = ("ln_ffn_pre", "ln_mix", "ln_ffn_post", "ln_final", "ffn_w1", "ffn_w3", "ffn_w2", "ab_w_in",
         "ab_conv_w", "ab_w_out", "s5_lambda_re", "s5_lambda_im", "s5_log_dt", "s5_b_re", "s5_b_im",
         "s5_c_re", "s5_c_im", "s5_d", "s5_glu_wa", "s5_glu_wb")


def kernel(x, ln_ffn_pre, ln_mix, ln_ffn_post, ln_final, ffn_w1, ffn_w3, ffn_w2, ab_w_in, ab_conv_w, ab_w_out, s5_lambda_re, s5_lambda_im, s5_log_dt, s5_b_re, s5_b_im, s5_c_re, s5_c_im, s5_d, s5_glu_wa, s5_glu_wb, loss_target, m_ln_ffn_pre, m_ln_mix, m_ln_ffn_post, m_ln_final, m_ffn_w1, m_ffn_w3, m_ffn_w2, m_ab_w_in, m_ab_conv_w, m_ab_w_out, m_s5_lambda_re, m_s5_lambda_im, m_s5_log_dt, m_s5_b_re, m_s5_b_im, m_s5_c_re, m_s5_c_im, m_s5_d, m_s5_glu_wa, m_s5_glu_wb, v_ln_ffn_pre, v_ln_mix, v_ln_ffn_post, v_ln_final, v_ffn_w1, v_ffn_w3, v_ffn_w2, v_ab_w_in, v_ab_conv_w, v_ab_w_out, v_s5_lambda_re, v_s5_lambda_im, v_s5_log_dt, v_s5_b_re, v_s5_b_im, v_s5_c_re, v_s5_c_im, v_s5_d, v_s5_glu_wa, v_s5_glu_wb):
    w = dict(zip(NAMES, (ln_ffn_pre, ln_mix, ln_ffn_post, ln_final, ffn_w1, ffn_w3, ffn_w2, ab_w_in, ab_conv_w,
                         ab_w_out, s5_lambda_re, s5_lambda_im, s5_log_dt, s5_b_re, s5_b_im, s5_c_re, s5_c_im,
                         s5_d, s5_glu_wa, s5_glu_wb)))
    mom = dict(zip(NAMES, (m_ln_ffn_pre, m_ln_mix, m_ln_ffn_post, m_ln_final, m_ffn_w1, m_ffn_w3, m_ffn_w2,
                           m_ab_w_in, m_ab_conv_w, m_ab_w_out, m_s5_lambda_re, m_s5_lambda_im, m_s5_log_dt,
                           m_s5_b_re, m_s5_b_im, m_s5_c_re, m_s5_c_im, m_s5_d, m_s5_glu_wa, m_s5_glu_wb)))
    var = dict(zip(NAMES, (v_ln_ffn_pre, v_ln_mix, v_ln_ffn_post, v_ln_final, v_ffn_w1, v_ffn_w3, v_ffn_w2,
                           v_ab_w_in, v_ab_conv_w, v_ab_w_out, v_s5_lambda_re, v_s5_lambda_im, v_s5_log_dt,
                           v_s5_b_re, v_s5_b_im, v_s5_c_re, v_s5_c_im, v_s5_d, v_s5_glu_wa, v_s5_glu_wb)))
    nb, seq, D = x.shape
    T = nb * seq
    assert ln_mix.shape[0] == 2 and ab_w_in.shape[0] == 1 and s5_glu_wa.shape[0] == 1
    xc, yc, cc = _mesh_pos()
    dev = 4 * xc + 2 * yc + cc
    pos = jnp.stack([cc, 2 * xc + yc]).astype(jnp.int32)
    bq = min(256, seq)
    tabs = _rope_tables(seq) + (_branch_bias(seq // bq, bq),)

    def ffn_piece(k, li, fj):
        return w[k][li, fj].astype(BF)

    g0 = _gather_comm([ffn_piece("ffn_w1", 0, 0), ffn_piece("ffn_w3", 0, 0), ab_conv_w[0], s5_d])
    _run_comm(g0, "gather_first")
    w1, w3 = {(0, 0): g0.results[0]}, {(0, 0): g0.results[1]}
    w2 = {}
    conv_w = g0.results[2].transpose(1, 0, 2).reshape(3, -1)
    dsk = g0.results[3].reshape(1, D)
    gains = {k: [w[k][i:i + 1] for i in range(2)] for k in ("ln_ffn_pre", "ln_mix", "ln_ffn_post")}

    h = x.reshape(T, D)
    saved = {}

    def ffn_fwd(h, gain, key, comm_up):
        n = _rms_fwd(h, gain, BF)
        a1, a3, g = _ffn_up(n, w1[key], w3[key], comm=comm_up)
        return n, a1, a3, g

    c_up = _gather_comm([ffn_piece("ffn_w2", 0, 0), ab_w_in[0].astype(BF)])
    n, a1, a3, g = ffn_fwd(h, gains["ln_ffn_pre"][0], (0, 0), c_up)
    w2[(0, 0)], wing = c_up.results
    c_dn = _gather_comm([ab_w_out[0].astype(BF), ffn_piece("ffn_w1", 0, 1)])
    saved["pre0"] = (h, n, a1, a3, g)
    h = _ffn_down(g, w2[(0, 0)], h, comm=c_dn)
    wout = c_dn.results[0].reshape(-1, D)
    w1[(0, 1)] = c_dn.results[1]
    c_proj = _gather_comm([ffn_piece("ffn_w3", 0, 1)])
    c_attn = _gather_comm([ffn_piece("ffn_w2", 0, 1), ffn_piece("ffn_w1", 1, 0)])
    h, saved["mix0"] = _mixer_ab_fwd(h, gains["ln_mix"][0], wing, conv_w, wout, tabs, nb, seq, c_proj, c_attn)
    w3[(0, 1)] = c_proj.results[0]
    w2[(0, 1)], w1[(1, 0)] = c_attn.results
    c_up = _gather_comm([ffn_piece("ffn_w3", 1, 0), s5_glu_wa[0].astype(BF)])
    n, a1, a3, g = ffn_fwd(h, gains["ln_ffn_post"][0], (0, 1), c_up)
    w3[(1, 0)] = c_up.results[0]
    wa = c_up.results[1].reshape(-1, D)
    c_dn = _gather_comm([ffn_piece("ffn_w2", 1, 0)])
    saved["post0"] = (h, n, a1, a3, g)
    h = _ffn_down(g, w2[(0, 1)], h, comm=c_dn)
    w2[(1, 0)] = c_dn.results[0]
    c_up = _gather_comm([s5_glu_wb[0].astype(BF), ffn_piece("ffn_w1", 1, 1)])
    n, a1, a3, g = ffn_fwd(h, gains["ln_ffn_pre"][1], (1, 0), c_up)
    wb = c_up.results[0].reshape(-1, D)
    w1[(1, 1)] = c_up.results[1]
    c_dn = _gather_comm([ffn_piece("ffn_w3", 1, 1)])
    saved["pre1"] = (h, n, a1, a3, g)
    h = _ffn_down(g, w2[(1, 0)], h, comm=c_dn)
    w3[(1, 1)] = c_dn.results[0]
    c_s5 = _gather_comm([ffn_piece("ffn_w2", 1, 1)])
    h, saved["mix1"] = _mixer_s5_fwd(h, gains["ln_mix"][1], w, dsk, wa, wb, nb, seq, c_s5)
    w2[(1, 1)] = c_s5.results[0]
    n, a1, a3, g = ffn_fwd(h, gains["ln_ffn_post"][1], (1, 1), None)
    saved["post1"] = (h, n, a1, a3, g)
    h = _ffn_down(g, w2[(1, 1)], h)
    dh, dhb, d_ln_final, loss_part = _loss_head(h, ln_final.reshape(1, D), loss_target.reshape(T, D))
    loss = lax.psum(loss_part[0, 0], ("x", "y", "c"))

    reduced = {}

    def reduce_start(names, grads):
        recv = _pair_exchange(grads, "pair_exchange")
        parts = [_pair_sum(g, r, pos) for g, r in zip(grads, recv)]
        comm = _exchange_comm(parts)
        for k, (nm, p) in enumerate(zip(names, parts)):
            reduced[nm] = (p, comm, k)
        return comm

    def ffn_bwd(dh, dhb, key, tag, gain, comm_first=None):
        h_in, n, a1, a3, g = saved[tag]
        da1, da3 = _ffn_bwd_hidden(dhb, w2[key], a1, a3, comm=comm_first)
        c2 = reduce_start([("ffn_w2",) + key], [_ffn_dw2(g, dhb)])
        dw1, dw3 = _ffn_dw13(n, da1, da3, comm=c2)
        c13 = reduce_start([("ffn_w1",) + key, ("ffn_w3",) + key], [dw1, dw3])
        dn = _ffn_dn(da1, da3, w1[key], w3[key], comm=c13)
        return _rms_bwd(dn, h_in, gain, dh)

    g_small = {"ln_final": d_ln_final.reshape(D)}
    g_ln = {k: [None, None] for k in gains}
    dh, dhb, g_ln["ln_ffn_post"][1] = ffn_bwd(dh, dhb, (1, 1), "post1", gains["ln_ffn_post"][1])
    dh, dhb, g_ln["ln_mix"][1], s5_small = _mixer_s5_bwd(dh, saved["mix1"], gains["ln_mix"][1], wa, wb, reduce_start)
    g_small.update(s5_small)
    dh, dhb, g_ln["ln_ffn_pre"][1] = ffn_bwd(dh, dhb, (1, 0), "pre1", gains["ln_ffn_pre"][1])
    dh, dhb, g_ln["ln_ffn_post"][0] = ffn_bwd(dh, dhb, (0, 1), "post0", gains["ln_ffn_post"][0])
    dh, dhb, g_ln["ln_mix"][0], g_small["ab_conv_w"], c_win = _mixer_ab_bwd(
        dh, dhb, saved["mix0"], gains["ln_mix"][0], wing, conv_w, wout, tabs, nb, seq, reduce_start)
    dh, dhb, g_ln["ln_ffn_pre"][0] = ffn_bwd(dh, dhb, (0, 0), "pre0", gains["ln_ffn_pre"][0], comm_first=c_win)
    grad_x = dh.reshape(nb, seq, D)
    for k in g_ln:
        g_small[k] = jnp.concatenate(g_ln[k], axis=0)

    out = {}
    for k in BIG:
        cols = w[k].shape[-1]
        pieces = [(li, fj) for li in range(2) for fj in range(2)] if w[k].ndim == 4 else [None]
        view = (len(pieces), -1, cols)
        w3d, m3d, v3d = w[k].reshape(view), mom[k].reshape(view), var[k].reshape(view)
        bufs = [lax.empty(w3d.shape, F32) for _ in range(4)]
        for q, key in enumerate(pieces):
            part, comm, slot = reduced[k if key is None else (k,) + key]
            bufs = _adamw_piece(w3d, m3d, v3d, q, part, comm.results[slot], bufs)
        out[k] = [t.reshape(w[k].shape) for t in bufs]

    small_names = [k for k in NAMES if k not in BIG]
    red = _unpack(_all_reduce_small(_pack([g_small[k] for k in small_names])),
                  [g_small[k].shape for k in small_names])
    g_red = dict(zip(small_names, red))
    cw = w["ab_conv_w"].shape[-1]
    g_red["ab_conv_w"] = lax.dynamic_slice_in_dim(g_red["ab_conv_w"], dev * cw, cw, axis=1)[None]
    dsz = w["s5_d"].shape[-1]
    g_red["s5_d"] = lax.dynamic_slice_in_dim(g_red["s5_d"].reshape(1, -1), dev * dsz, dsz, axis=1)
    shapes = [w[k].shape for k in small_names]
    g_red = {k: g_red[k].reshape(w[k].shape) for k in small_names}
    d_s, m_s, v_s = _adamw_small(_pack([w[k] for k in small_names]), _pack([g_red[k] for k in small_names]),
                                 _pack([mom[k] for k in small_names]), _pack([var[k] for k in small_names]))
    for k, d, nm, nv in zip(small_names, _unpack(d_s, shapes), _unpack(m_s, shapes), _unpack(v_s, shapes)):
        out[k] = [g_red[k], d, nm, nv]

    return (loss, grad_x, *[out[k][0] for k in NAMES], *[out[k][1] for k in NAMES],
            *[out[k][2] for k in NAMES], *[out[k][3] for k in NAMES])
```

```python
import jax
import jax.numpy as jnp
from jax import lax
from jax.experimental import pallas as pl
from jax.experimental.pallas import tpu as pltpu

F32, BF = jnp.float32, jnp.bfloat16
N_DEV = 8
MESH = pl.DeviceIdType.MESH
LANES = 128
SUBLANES = 8
VMEM_LIMIT = 56 * 2 ** 20
ROW_TILE = 512
COL_TILE = 512
ELEMS_PER_BLOCK = 256 * 1024
RMS_EPS = 1e-6
ROPE_THETA = 10000.0
NEG_INF = -1e30
S5_STATE = 64
S5_GROUP = 16
GROUPS_PER_BLOCK = LANES // S5_GROUP
STATE_COLS = GROUPS_PER_BLOCK * S5_STATE
DILATED_PATTERN = ((128, 1), (512, 4), (2048, 16))
ADAM_LR, ADAM_B1, ADAM_B2, ADAM_EPS, ADAM_WD, ADAM_STEP = 0.001, 0.9, 0.999, 1e-08, 0.01, 10
GELU_C = 0.7978845608028654
GELU_A = 0.044715


def _cparams(n_grid, vmem=VMEM_LIMIT):
    sem = ("arbitrary",) * n_grid if n_grid else None
    return pltpu.CompilerParams(dimension_semantics=sem, vmem_limit_bytes=vmem)


def _sig(x):
    return 1.0 / (1.0 + jnp.exp(-x))


def _gelu(x):
    return 0.5 * x * (1.0 + jnp.tanh(GELU_C * (x + GELU_A * x * x * x)))


def _gelu_grad(x):
    t = jnp.tanh(GELU_C * (x + GELU_A * x * x * x))
    return 0.5 * (1.0 + t) + 0.5 * x * (1.0 - t * t) * GELU_C * (1.0 + 3.0 * GELU_A * x * x)


def _dot(a, b, dims):
    a = a if a.dtype == BF else a.astype(BF)
    b = b if b.dtype == BF else b.astype(BF)
    return lax.dot_general(a, b, (dims, ((), ())), preferred_element_type=F32)


NN = ((1,), (0,))
NT = ((1,), (1,))
TN = ((0,), (0,))


def _row_block(rows, cols, mult=16):
    cap = max(mult, ELEMS_PER_BLOCK // cols)
    best = None
    for b in range(mult, min(rows, cap) + 1, mult):
        if rows % b == 0:
            best = b
    return rows if best is None else best


class _Comm:
    def __init__(self, ins, outs, sems, start, finish):
        self.ins, self.outs, self.sems, self.start, self.finish = ins, outs, sems, start, finish
        self.results = None


def _call(body, name, grid, in_specs, out_specs, out_shape, args, scratch=(), comm=None):
    in_specs, out_specs, out_shape, scratch = list(in_specs), list(out_specs), list(out_shape), list(scratch)
    if comm is None:
        return pl.pallas_call(body, grid=grid, in_specs=in_specs, out_specs=out_specs, out_shape=out_shape,
                              scratch_shapes=scratch, name=name, compiler_params=_cparams(len(grid)))(*args)
    n_in, n_out, n_sc = len(in_specs), len(out_specs), len(scratch)
    ci, co = len(comm.ins), len(comm.outs)

    def hosted(*refs):
        ins, refs = refs[:n_in], refs[n_in:]
        cins, refs = refs[:ci], refs[ci:]
        outs, refs = refs[:n_out], refs[n_out:]
        couts, refs = refs[:co], refs[co:]
        sc, csems = refs[:n_sc], refs[n_sc:]
        first = last = None
        for d, n in enumerate(grid):
            p = pl.program_id(d)
            first = (p == 0) if first is None else first & (p == 0)
            last = (p == n - 1) if last is None else last & (p == n - 1)

        @pl.when(first)
        def _():
            comm.start(cins, couts, csems)

        body(*ins, *outs, *sc)

        @pl.when(last)
        def _():
            comm.finish(cins, couts, csems)

    any_spec = pl.BlockSpec(memory_space=pl.ANY)
    res = pl.pallas_call(
        hosted, grid=grid, in_specs=in_specs + [any_spec] * ci, out_specs=out_specs + [any_spec] * co,
        out_shape=out_shape + list(comm.outs), scratch_shapes=scratch + list(comm.sems), name=name,
        compiler_params=_cparams(len(grid)))(*args, *comm.ins)
    comm.results = list(res[n_out:])
    return list(res[:n_out])


def _mm(name, grid, operands, pairs, n_acc, acc_shape, extras, outs, epilogue, comm=None, nrow=1, ncol=1):
    nk = grid[2]
    n_op, n_ex, n_out = len(operands), len(extras), len(outs)

    def part_of(ref, dim, t, n):
        if n == 1:
            return ref
        size = ref.shape[dim] // n
        idx = [slice(None)] * len(ref.shape)
        idx[dim] = pl.ds(t * size, size)
        return ref.at[tuple(idx)]

    def tile_of(ref, r, c):
        return part_of(part_of(ref, 0, r, nrow), 1, c, ncol)

    def products(op, r, c):
        parts = [None] * n_acc
        for ai, bi, dims, ci in pairs:
            a = part_of(op[ai], 1 - dims[0][0], r, nrow)
            b = part_of(op[bi], 1 - dims[1][0], c, ncol)
            d = _dot(a[...], b[...], dims)
            parts[ci] = d if parts[ci] is None else parts[ci] + d
        return parts

    def body(*refs):
        op = refs[:n_op]
        ex = refs[n_op:n_op + n_ex]
        out = refs[n_op + n_ex:n_op + n_ex + n_out]
        acc = refs[n_op + n_ex + n_out:]
        tiles = [(r, c) for r in range(nrow) for c in range(ncol)]

        def views(refs_, t):
            return [tile_of(q, *t) for q in refs_]

        if nk == 1:
            parts = products(op, *tiles[0])
            for q, t in enumerate(tiles):
                nxt = products(op, *tiles[q + 1]) if q + 1 < len(tiles) else None
                epilogue(parts, views(ex, t), views(out, t))
                parts = nxt
            return
        k = pl.program_id(2)

        @pl.when(k == 0)
        def _():
            for q in acc:
                q[...] = jnp.zeros_like(q)

        for t in tiles:
            parts = products(op, *t)
            for q, p in zip(views(acc, t), parts):
                q[...] += p

        @pl.when(k == nk - 1)
        def _():
            for t in tiles:
                epilogue([q[...] for q in views(acc, t)], views(ex, t), views(out, t))

    return _call(body, name, grid, [s for _, s in operands] + [s for _, s in extras], [s for _, s in outs],
                 [sh for sh, _ in outs], [a for a, _ in operands] + [a for a, _ in extras],
                 scratch=[pltpu.VMEM(acc_shape, F32) for _ in range(n_acc if nk > 1 else 0)], comm=comm)


def _to_seg(a, seg_len):
    T, D = a.shape
    return a.reshape(SUBLANES, seg_len, D).transpose(1, 0, 2).reshape(T, D)


def _to_tok(a, seg_len):
    T, D = a.shape
    return a.reshape(seg_len, SUBLANES, D).transpose(1, 0, 2).reshape(T, D)


def _rms_fwd(h, gain, out_dtype):
    T, D = h.shape
    bm = min(ROW_TILE, T)

    def body(h_ref, g_ref, o_ref):
        x = h_ref[...]
        r = lax.rsqrt(jnp.mean(x * x, axis=-1, keepdims=True) + RMS_EPS)
        o_ref[...] = (x * r * g_ref[...]).astype(out_dtype)

    row = pl.BlockSpec((bm, D), lambda i: (i, 0))
    return pl.pallas_call(
        body, grid=(T // bm,), in_specs=[row, pl.BlockSpec((1, D), lambda i: (0, 0))],
        out_specs=row, out_shape=jax.ShapeDtypeStruct((T, D), out_dtype), name="rms_fwd",
        compiler_params=_cparams(1))(h, gain)


def _rms_bwd_rows(dn, x, g):
    r = lax.rsqrt(jnp.mean(x * x, axis=-1, keepdims=True) + RMS_EPS)
    xh = x * r
    dng = dn * g
    dx = r * (dng - xh * jnp.mean(dng * xh, axis=-1, keepdims=True))
    return dx, jnp.sum(dn * xh, axis=0, keepdims=True)


def _rms_bwd(dn, h, gain, dh_up):
    T, D = h.shape
    bm = min(ROW_TILE, T)

    def body(dn_ref, h_ref, g_ref, up_ref, dh_ref, dhb_ref, dg_ref):
        dx, dg = _rms_bwd_rows(dn_ref[...], h_ref[...], g_ref[...])
        dh = up_ref[...] + dx
        dh_ref[...] = dh
        dhb_ref[...] = dh.astype(BF)

        @pl.when(pl.program_id(0) == 0)
        def _():
            dg_ref[...] = jnp.zeros_like(dg_ref)

        dg_ref[...] += dg

    row = pl.BlockSpec((bm, D), lambda i: (i, 0))
    vec = pl.BlockSpec((1, D), lambda i: (0, 0))
    return pl.pallas_call(
        body, grid=(T // bm,), in_specs=[row, row, vec, row], out_specs=[row, row, vec],
        out_shape=[jax.ShapeDtypeStruct((T, D), F32), jax.ShapeDtypeStruct((T, D), BF),
                   jax.ShapeDtypeStruct((1, D), F32)],
        name="rms_bwd", compiler_params=_cparams(1))(dn, h, gain, dh_up)


def _loss_head(h, gain, target):
    T, D = h.shape
    bm = min(ROW_TILE, T)

    def body(h_ref, g_ref, t_ref, dh_ref, dhb_ref, dg_ref, loss_ref):
        x = h_ref[...]
        g = g_ref[...]
        r = lax.rsqrt(jnp.mean(x * x, axis=-1, keepdims=True) + RMS_EPS)
        err = x * r * g - t_ref[...]
        part = 0.5 * jnp.sum(jnp.sum(err * err, axis=-1, keepdims=True), axis=0, keepdims=True) / D
        dx, dg = _rms_bwd_rows(err / D, x, g)
        dh_ref[...] = dx
        dhb_ref[...] = dx.astype(BF)

        @pl.when(pl.program_id(0) == 0)
        def _():
            dg_ref[...] = jnp.zeros_like(dg_ref)
            loss_ref[...] = jnp.zeros_like(loss_ref)

        dg_ref[...] += dg
        loss_ref[...] += jnp.broadcast_to(part, loss_ref.shape)

    row = pl.BlockSpec((bm, D), lambda i: (i, 0))
    vec = pl.BlockSpec((1, D), lambda i: (0, 0))
    return pl.pallas_call(
        body, grid=(T // bm,), in_specs=[row, vec, row],
        out_specs=[row, row, vec, pl.BlockSpec((SUBLANES, LANES), lambda i: (0, 0))],
        out_shape=[jax.ShapeDtypeStruct((T, D), F32), jax.ShapeDtypeStruct((T, D), BF),
                   jax.ShapeDtypeStruct((1, D), F32), jax.ShapeDtypeStruct((SUBLANES, LANES), F32)],
        name="loss_head", compiler_params=_cparams(1))(h, gain, target)


def _ffn_up(n, w1g, w3g, comm=None):
    T, D = n.shape
    fs = w1g.shape[-1]
    bm = min(ROW_TILE, T)
    wspec = pl.BlockSpec((None, D, fs), lambda s, i, k: (s, 0, 0))
    ospec = pl.BlockSpec((None, bm, fs), lambda s, i, k: (s, i, 0))

    def epi(accs, ex, outs):
        a1, a3 = accs
        outs[0][...] = a1.astype(BF)
        outs[1][...] = a3.astype(BF)
        outs[2][...] = (a1 * _sig(a1) * a3).astype(BF)

    sh = jax.ShapeDtypeStruct((N_DEV, T, fs), BF)
    return _mm("ffn_up", (N_DEV, T // bm, 1),
               [(n, pl.BlockSpec((bm, D), lambda s, i, k: (i, 0))), (w1g, wspec), (w3g, wspec)],
               [(0, 1, NN, 0), (0, 2, NN, 1)], 2, None, [], [(sh, ospec)] * 3, epi, comm=comm, nrow=2)


def _ffn_down(g, w2g, h, comm=None):
    _, T, fs = g.shape
    D = h.shape[1]
    bm = min(ROW_TILE, T)
    row = pl.BlockSpec((bm, D), lambda i, j, s: (i, 0))

    def epi(accs, ex, outs):
        outs[0][...] = ex[0][...] + 0.5 * accs[0]

    return _mm("ffn_down", (T // bm, 1, N_DEV),
               [(g, pl.BlockSpec((None, bm, fs), lambda i, j, s: (s, i, 0))),
                (w2g, pl.BlockSpec((None, fs, D), lambda i, j, s: (s, 0, 0)))],
               [(0, 1, NN, 0)], 1, (bm, D), [(h, row)],
               [(jax.ShapeDtypeStruct((T, D), F32), row)], epi, comm=comm, ncol=max(1, D // COL_TILE))[0]


def _ffn_bwd_hidden(dhb, w2g, a1, a3, comm=None):
    T, D = dhb.shape
    fs = a1.shape[-1]
    bm = min(ROW_TILE, T)
    aspec = pl.BlockSpec((None, bm, fs), lambda s, i, k: (s, i, 0))

    def epi(accs, ex, outs):
        dg = 0.5 * accs[0]
        a1v = ex[0][...].astype(F32)
        a3v = ex[1][...].astype(F32)
        sg = _sig(a1v)
        outs[0][...] = (dg * a3v * sg * (1.0 + a1v * (1.0 - sg))).astype(BF)
        outs[1][...] = (dg * a1v * sg).astype(BF)

    sh = jax.ShapeDtypeStruct((N_DEV, T, fs), BF)
    return _mm("ffn_bwd_hidden", (N_DEV, T // bm, 1),
               [(dhb, pl.BlockSpec((bm, D), lambda s, i, k: (i, 0))),
                (w2g, pl.BlockSpec((None, fs, D), lambda s, i, k: (s, 0, 0)))],
               [(0, 1, NT, 0)], 1, None, [(a1, aspec), (a3, aspec)], [(sh, aspec)] * 2, epi, comm=comm, nrow=2)


def _ffn_dw2(g, dhb):
    _, T, fs = g.shape
    D = dhb.shape[1]
    bn = min(COL_TILE, D)

    def epi(accs, ex, outs):
        outs[0][...] = (0.5 * accs[0]).astype(BF)

    return _mm("ffn_dw2", (N_DEV, D // bn, 1),
               [(g, pl.BlockSpec((None, T, fs), lambda s, j, k: (s, 0, 0))),
                (dhb, pl.BlockSpec((T, bn), lambda s, j, k: (0, j)))],
               [(0, 1, TN, 0)], 1, None, [],
               [(jax.ShapeDtypeStruct((N_DEV, fs, D), BF), pl.BlockSpec((None, fs, bn), lambda s, j, k: (s, 0, j)))],
               epi)[0]


def _ffn_dw13(n, da1, da3, comm=None):
    T, D = n.shape
    fs = da1.shape[-1]
    bmr = min(COL_TILE, D)
    dspec = pl.BlockSpec((None, T, fs), lambda s, r, k: (s, 0, 0))
    ospec = pl.BlockSpec((None, bmr, fs), lambda s, r, k: (s, r, 0))

    def epi(accs, ex, outs):
        outs[0][...] = accs[0].astype(BF)
        outs[1][...] = accs[1].astype(BF)

    sh = jax.ShapeDtypeStruct((N_DEV, D, fs), BF)
    return _mm("ffn_dw13", (N_DEV, D // bmr, 1),
               [(n, pl.BlockSpec((T, bmr), lambda s, r, k: (0, r))), (da1, dspec), (da3, dspec)],
               [(0, 1, TN, 0), (0, 2, TN, 1)], 2, None, [], [(sh, ospec)] * 2, epi, comm=comm)


def _ffn_dn(da1, da3, w1g, w3g, comm=None):
    _, T, fs = da1.shape
    D = w1g.shape[-2]
    bm = min(ROW_TILE, T)
    dspec = pl.BlockSpec((None, bm, fs), lambda i, j, s: (s, i, 0))
    wspec = pl.BlockSpec((None, D, fs), lambda i, j, s: (s, 0, 0))
    row = pl.BlockSpec((bm, D), lambda i, j, s: (i, 0))

    def epi(accs, ex, outs):
        outs[0][...] = accs[0]

    return _mm("ffn_dn", (T // bm, 1, N_DEV),
               [(da1, dspec), (w1g, wspec), (da3, dspec), (w3g, wspec)],
               [(0, 1, NT, 0), (2, 3, NT, 0)], 1, (bm, D), [],
               [(jax.ShapeDtypeStruct((T, D), F32), row)], epi, comm=comm, ncol=max(1, D // COL_TILE))[0]


def _rope_tables(seq):
    half = LANES // 2
    inv = ROPE_THETA ** (-jnp.arange(0, half, dtype=F32) * 2.0 / LANES)
    ang = jnp.arange(seq, dtype=F32)[:, None] * inv[None, :]
    cos, sin = jnp.cos(ang), jnp.sin(ang)
    return jnp.concatenate([cos, cos], axis=1), jnp.concatenate([-sin, sin], axis=1)


def _branch_bias(nq, bq):
    d = (jnp.arange(nq)[:, None, None] * bq + jnp.arange(bq)[None, :, None]
         - jnp.arange(bq)[None, None, :])
    mult = jnp.zeros(d.shape, F32)
    for window, dil in DILATED_PATTERN:
        mult = mult + ((d >= 0) & (d % dil == 0) & (d <= window)).astype(F32)
    return jnp.where(mult > 0, jnp.log(jnp.maximum(mult, 1.0)), NEG_INF)


def _proj_fwd(u, wing, comm=None):
    T, D = u.shape
    ws = wing.shape[-1]
    bm = min(ROW_TILE, T)

    def epi(accs, ex, outs):
        outs[0][...] = accs[0]

    return _mm("proj_fwd", (N_DEV, T // bm, 1),
               [(u, pl.BlockSpec((bm, D), lambda s, i, k: (i, 0))),
                (wing, pl.BlockSpec((None, D, ws), lambda s, i, k: (s, 0, 0)))],
               [(0, 1, NN, 0)], 1, None, [],
               [(jax.ShapeDtypeStruct((T, N_DEV * ws), F32),
                 pl.BlockSpec((bm, ws), lambda s, i, k: (i, s)))], epi, comm=comm)[0]


def _rope_fwd(proj, cosf, sinf, seq, nh):
    T = proj.shape[0]
    bs = min(ROW_TILE, seq)
    nst = seq // bs
    scale = LANES ** -0.5

    def body(x_ref, c_ref, s_ref, o_ref):
        j = pl.program_id(1)
        t = x_ref[...]
        rot = t * c_ref[...] + pltpu.roll(t, LANES // 2, 1) * s_ref[...]
        rot = rot * jnp.where(j < nh, scale, 1.0)
        o_ref[...] = jnp.where(j < 2 * nh, rot, t).astype(BF)

    blk = pl.BlockSpec((bs, LANES), lambda r, j: (r, j))
    tab = pl.BlockSpec((bs, LANES), lambda r, j: (r % nst, 0))
    return pl.pallas_call(
        body, grid=(T // bs, 3 * nh), in_specs=[blk, tab, tab], out_specs=blk,
        out_shape=jax.ShapeDtypeStruct((T, 3 * nh * LANES), BF), name="rope_fwd",
        compiler_params=_cparams(2))(proj, cosf, sinf)


def _attn_fwd(qkv, bias, nb, seq, nh, comm=None):
    T = nb * seq
    bq = bias.shape[1]
    nq = seq // bq

    def body(q_ref, k_ref, v_ref, b_ref, o_ref, lse_ref):
        qi = pl.program_id(2)
        q = q_ref[...]

        def step(kj, carry):
            m, l, acc = carry
            rows = pl.ds(pl.multiple_of(kj * bq, bq), bq)
            s = _dot(q, k_ref[rows, :], NT) + b_ref[qi - kj]
            m_new = jnp.maximum(m, jnp.max(s, axis=1, keepdims=True))
            p = jnp.exp(s - m_new)
            alpha = jnp.exp(m - m_new)
            l = alpha * l + jnp.sum(p, axis=1, keepdims=True)
            acc = alpha * acc + _dot(p, v_ref[rows, :], NN)
            return m_new, l, acc

        init = (jnp.full((bq, 1), NEG_INF, F32), jnp.zeros((bq, 1), F32), jnp.zeros((bq, LANES), F32))
        m, l, acc = lax.fori_loop(0, qi + 1, step, init)
        o_ref[...] = (acc / l).astype(BF)
        lse_ref[...] = m + jnp.log(l)

    return _call(
        body, "attn_fwd", (nb, nh, nq),
        [pl.BlockSpec((bq, LANES), lambda b, h, i: (b * nq + i, h)),
         pl.BlockSpec((seq, LANES), lambda b, h, i: (b, nh + h)),
         pl.BlockSpec((seq, LANES), lambda b, h, i: (b, 2 * nh + h)),
         pl.BlockSpec((nq, bq, bq), lambda b, h, i: (0, 0, 0))],
        [pl.BlockSpec((bq, LANES), lambda b, h, i: (b * nq + i, h)),
         pl.BlockSpec((None, bq, 1), lambda b, h, i: (h, b * nq + i, 0))],
        [jax.ShapeDtypeStruct((T, 2 * nh * LANES), BF), jax.ShapeDtypeStruct((nh, T, 1), F32)],
        (qkv, qkv, qkv, bias), comm=comm)


def _attn_bwd_dq(qkv, cat, dcat, lse, bias, nb, seq, nh, comm=None):
    T = nb * seq
    bq = bias.shape[1]
    nq = seq // bq

    def body(q_ref, k_ref, v_ref, o_ref, do_ref, lse_ref, b_ref, dq_ref, delta_ref):
        qi = pl.program_id(2)
        q = q_ref[...]
        do = do_ref[...]
        dob = do.astype(BF)
        lse_t = lse_ref[...]
        delta = jnp.sum(do * o_ref[...].astype(F32), axis=1, keepdims=True)
        delta_ref[...] = delta

        def step(kj, dq):
            rows = pl.ds(pl.multiple_of(kj * bq, bq), bq)
            k = k_ref[rows, :]
            p = jnp.exp(_dot(q, k, NT) + b_ref[qi - kj] - lse_t)
            ds = p * (_dot(dob, v_ref[rows, :], NT) - delta)
            return dq + _dot(ds, k, NN)

        dq_ref[...] = lax.fori_loop(0, qi + 1, step, jnp.zeros((bq, LANES), F32))

    tile = pl.BlockSpec((bq, LANES), lambda b, h, i: (b * nq + i, h))
    stat = pl.BlockSpec((None, bq, 1), lambda b, h, i: (h, b * nq + i, 0))
    return _call(
        body, "attn_bwd_dq", (nb, nh, nq),
        [tile, pl.BlockSpec((seq, LANES), lambda b, h, i: (b, nh + h)),
         pl.BlockSpec((seq, LANES), lambda b, h, i: (b, 2 * nh + h)), tile, tile, stat,
         pl.BlockSpec((nq, bq, bq), lambda b, h, i: (0, 0, 0))],
        [tile, stat],
        [jax.ShapeDtypeStruct((T, nh * LANES), F32), jax.ShapeDtypeStruct((nh, T, 1), F32)],
        (qkv, qkv, qkv, cat, dcat, lse, bias), comm=comm)


def _attn_bwd_dkv(qkv, dcat, lse, delta, bias, nb, seq, nh):
    T = nb * seq
    bq = bias.shape[1]
    nq = seq // bq

    def body(k_ref, v_ref, q_ref, do_ref, lse_ref, delta_ref, b_ref, dk_ref, dv_ref):
        kj = pl.program_id(2)
        k = k_ref[...]
        v = v_ref[...]

        def step(qi, carry):
            dk, dv = carry
            rows = pl.ds(pl.multiple_of(qi * bq, bq), bq)
            q = q_ref[rows, :]
            dob = do_ref[rows, :].astype(BF)
            p = jnp.exp(_dot(q, k, NT) + b_ref[qi - kj] - lse_ref[rows, :])
            dv = dv + _dot(p, dob, TN)
            ds = p * (_dot(dob, v, NT) - delta_ref[rows, :])
            return dk + _dot(ds, q, TN), dv

        z = jnp.zeros((bq, LANES), F32)
        dk, dv = lax.fori_loop(kj, nq, step, (z, z))
        dk_ref[...] = dk
        dv_ref[...] = dv

    stat = pl.BlockSpec((None, seq, 1), lambda b, h, i: (h, b, 0))
    out = pl.BlockSpec((bq, LANES), lambda b, h, i: (b * nq + i, h))
    sh = jax.ShapeDtypeStruct((T, nh * LANES), F32)
    return pl.pallas_call(
        body, grid=(nb, nh, nq),
        in_specs=[pl.BlockSpec((bq, LANES), lambda b, h, i: (b * nq + i, nh + h)),
                  pl.BlockSpec((bq, LANES), lambda b, h, i: (b * nq + i, 2 * nh + h)),
                  pl.BlockSpec((seq, LANES), lambda b, h, i: (b, h)),
                  pl.BlockSpec((seq, LANES), lambda b, h, i: (b, h)),
                  stat, stat,
                  pl.BlockSpec((nq, bq, bq), lambda b, h, i: (0, 0, 0))],
        out_specs=[out, out], out_shape=[sh, sh],
        name="attn_bwd_dkv", compiler_params=_cparams(3))(qkv, qkv, qkv, dcat, lse, delta, bias)


def _conv_parts(gc, xin, w_ref):
    w = [w_ref[k:k + 1, :] for k in range(3)]
    u = gc * xin
    row = lax.broadcasted_iota(jnp.int32, u.shape, 0)
    u1 = jnp.where(row >= 1, pltpu.roll(u, 1, 0), 0.0)
    u2 = jnp.where(row >= 2, pltpu.roll(u, 2, 0), 0.0)
    return u, u1, u2, w[0] * u2 + w[1] * u1 + w[2] * u, w, row


def _conv_fwd(proj, conv_w, cat, nb, seq, width):
    cw = min(2 * LANES, width)
    nc = width // cw

    def body(gb_ref, gc_ref, x_ref, w_ref, cat_ref, o_ref):
        _, _, _, conv, _, _ = _conv_parts(gc_ref[...], x_ref[...], w_ref)
        o_ref[...] = (gb_ref[...] * conv).astype(BF)

    def sec(k):
        return pl.BlockSpec((seq, cw), lambda b, c: (b, k * nc + c))

    return pl.pallas_call(
        body, grid=(nb, nc),
        in_specs=[sec(3), sec(4), sec(5), pl.BlockSpec((3, cw), lambda b, c: (0, c)),
                  pl.BlockSpec(memory_space=pl.ANY)],
        out_specs=pl.BlockSpec((seq, cw), lambda b, c: (b, nc + c)),
        out_shape=jax.ShapeDtypeStruct(cat.shape, BF), input_output_aliases={4: 0},
        name="conv_fwd", compiler_params=_cparams(2))(proj, proj, proj, conv_w, cat)


def _conv_bwd(proj, conv_w, dcat, nb, seq, width):
    cw = min(2 * LANES, width)
    nc = width // cw
    T = nb * seq

    def body(gb_ref, gc_ref, x_ref, w_ref, d_ref, dgb_ref, dgc_ref, dx_ref, dw_ref):
        gc = gc_ref[...]
        xin = x_ref[...]
        u, u1, u2, conv, w, row = _conv_parts(gc, xin, w_ref)
        dsc = d_ref[...]
        dgb_ref[...] = dsc * conv
        dconv = dsc * gb_ref[...]
        d1 = jnp.where(row < seq - 1, pltpu.roll(dconv, seq - 1, 0), 0.0)
        d2 = jnp.where(row < seq - 2, pltpu.roll(dconv, seq - 2, 0), 0.0)
        du = w[2] * dconv + w[1] * d1 + w[0] * d2
        dgc_ref[...] = du * xin
        dx_ref[...] = du * gc

        @pl.when(pl.program_id(1) == 0)
        def _():
            dw_ref[...] = jnp.zeros_like(dw_ref)

        dw_ref[0:1, :] += jnp.sum(dconv * u2, axis=0, keepdims=True)
        dw_ref[1:2, :] += jnp.sum(dconv * u1, axis=0, keepdims=True)
        dw_ref[2:3, :] += jnp.sum(dconv * u, axis=0, keepdims=True)

    def sec(k):
        return pl.BlockSpec((seq, cw), lambda c, b: (b, k * nc + c))

    out = pl.BlockSpec((seq, cw), lambda c, b: (b, c))
    wsp = pl.BlockSpec((3, cw), lambda c, b: (0, c))
    sh = jax.ShapeDtypeStruct((T, width), F32)
    return pl.pallas_call(
        body, grid=(nc, nb), in_specs=[sec(3), sec(4), sec(5), wsp, sec(1)],
        out_specs=[out, out, out, wsp], out_shape=[sh, sh, sh, jax.ShapeDtypeStruct((3, width), F32)],
        name="conv_bwd", compiler_params=_cparams(2))(proj, proj, proj, conv_w, dcat)


def _assemble_dproj(dq, dk, dv, dgb, dgc, dxin, cosf, sinf, seq):
    T, width = dq.shape
    nh = width // LANES
    bs = min(256, seq)
    nst = seq // bs
    scale = LANES ** -0.5

    def body(dq_ref, dk_ref, dv_ref, dgb_ref, dgc_ref, dx_ref, c_ref, s_ref, o_ref):
        sec = pl.program_id(1)
        c = c_ref[...]
        s = s_ref[...]

        def unrope(ref, mul):
            for h in range(nh):
                cols = slice(h * LANES, (h + 1) * LANES)
                t = ref[:, cols]
                o_ref[:, cols] = ((t * c + pltpu.roll(t * s, LANES // 2, 1)) * mul).astype(BF)

        @pl.when(sec == 0)
        def _():
            unrope(dq_ref, scale)

        @pl.when(sec == 1)
        def _():
            unrope(dk_ref, 1.0)

        for k, ref in ((2, dv_ref), (3, dgb_ref), (4, dgc_ref), (5, dx_ref)):
            @pl.when(sec == k)
            def _(ref=ref):
                o_ref[...] = ref[...].astype(BF)

    blk = pl.BlockSpec((bs, width), lambda r, k: (r, 0))
    tab = pl.BlockSpec((bs, LANES), lambda r, k: (r % nst, 0))
    return pl.pallas_call(
        body, grid=(T // bs, 6), in_specs=[blk] * 6 + [tab, tab],
        out_specs=pl.BlockSpec((bs, width), lambda r, k: (r, k)),
        out_shape=jax.ShapeDtypeStruct((T, 6 * width), BF), name="assemble_dproj",
        compiler_params=_cparams(2))(dq, dk, dv, dgb, dgc, dxin, cosf, sinf)


def _res_mm(name, a, w, h):
    T, K = a.shape
    N = w.shape[1]
    bm = min(ROW_TILE, T)
    bk = min(ROW_TILE, K)
    row = pl.BlockSpec((bm, N), lambda i, j, k: (i, 0))

    def epi(accs, ex, outs):
        outs[0][...] = ex[0][...] + accs[0]

    return _mm(name, (T // bm, 1, K // bk),
               [(a, pl.BlockSpec((bm, bk), lambda i, j, k: (i, k))),
                (w, pl.BlockSpec((bk, N), lambda i, j, k: (k, 0)))],
               [(0, 1, NN, 0)], 1, (bm, N), [(h, row)],
               [(jax.ShapeDtypeStruct((T, N), F32), row)], epi, ncol=max(1, N // COL_TILE))[0]


def _mm_nt(name, a, w, out_dtype):
    T, K = a.shape
    N = w.shape[0]
    bm = min(ROW_TILE, T)
    bn = min(ROW_TILE, N)

    def epi(accs, ex, outs):
        outs[0][...] = accs[0].astype(out_dtype)

    return _mm(name, (T // bm, N // bn, 1),
               [(a, pl.BlockSpec((bm, K), lambda i, j, k: (i, 0))),
                (w, pl.BlockSpec((bn, K), lambda i, j, k: (j, 0)))],
               [(0, 1, NT, 0)], 1, None, [],
               [(jax.ShapeDtypeStruct((T, N), out_dtype), pl.BlockSpec((bm, bn), lambda i, j, k: (i, j)))],
               epi)[0]


def _mm_tn(name, a, bs_list):
    T, M = a.shape
    N = bs_list[0].shape[1]
    bk = min(ROW_TILE, T)
    bmr = min(ROW_TILE, M)
    n = len(bs_list)

    def epi(accs, ex, outs):
        for q in range(n):
            outs[q][...] = accs[q].astype(BF)

    ops = [(a, pl.BlockSpec((bk, bmr), lambda r, j, t: (t, r)))]
    ops += [(b, pl.BlockSpec((bk, N), lambda r, j, t: (t, 0))) for b in bs_list]
    return _mm(name, (M // bmr, 1, T // bk), ops, [(0, 1 + q, TN, q) for q in range(n)], n, (bmr, N), [],
               [(jax.ShapeDtypeStruct((M, N), BF), pl.BlockSpec((bmr, N), lambda r, j, t: (r, 0)))] * n, epi)


def _proj_bwd_x(dproj, wing):
    T = dproj.shape[0]
    _, D, ws = wing.shape
    bm = min(ROW_TILE, T)
    row = pl.BlockSpec((bm, D), lambda i, j, s: (i, 0))

    def epi(accs, ex, outs):
        outs[0][...] = accs[0]

    return _mm("proj_bwd_x", (T // bm, 1, N_DEV),
               [(dproj, pl.BlockSpec((bm, ws), lambda i, j, s: (i, s))),
                (wing, pl.BlockSpec((None, D, ws), lambda i, j, s: (s, 0, 0)))],
               [(0, 1, NT, 0)], 1, (bm, D), [], [(jax.ShapeDtypeStruct((T, D), F32), row)], epi,
               ncol=max(1, D // COL_TILE))[0]


def _proj_dw(u, dproj, ws):
    T, D = u.shape
    bk = min(ROW_TILE, T)

    def epi(accs, ex, outs):
        outs[0][...] = accs[0].astype(BF)

    return _mm("proj_dw", (N_DEV, 1, T // bk),
               [(u, pl.BlockSpec((bk, D), lambda s, j, t: (t, 0))),
                (dproj, pl.BlockSpec((bk, ws), lambda s, j, t: (t, s)))],
               [(0, 1, TN, 0)], 1, (D, ws), [],
               [(jax.ShapeDtypeStruct((N_DEV, D, ws), BF),
                 pl.BlockSpec((None, D, ws), lambda s, j, t: (s, 0, 0)))], epi)[0]


def _mixer_ab_fwd(h, gain, wing, conv_w, wout, tabs, nb, seq, comm_proj=None, comm_attn=None):
    cosf, sinf, bias = tabs
    width = wing.shape[-1] * N_DEV // 6
    nh = width // LANES
    u = _rms_fwd(h, gain, BF)
    proj = _proj_fwd(u, wing, comm=comm_proj)
    qkv = _rope_fwd(proj, cosf, sinf, seq, nh)
    cat, lse = _attn_fwd(qkv, bias, nb, seq, nh, comm=comm_attn)
    cat = _conv_fwd(proj, conv_w, cat, nb, seq, width)
    return _res_mm("outproj_fwd", cat, wout, h), (h, u, proj, qkv, cat, lse)


def _mixer_ab_bwd(dh, dhb, saved, gain, wing, conv_w, wout, tabs, nb, seq, reduce_start):
    cosf, sinf, bias = tabs
    h, u, proj, qkv, cat, lse = saved
    D = h.shape[1]
    ws = wing.shape[-1]
    width = ws * N_DEV // 6
    nh = width // LANES
    dcat = _mm_nt("outproj_bwd_x", dhb, wout, F32)
    dwout = _mm_tn("outproj_dw", cat, [dhb])[0]
    comm = reduce_start(["ab_w_out"], [dwout.reshape(N_DEV, -1, D)])
    dq, delta = _attn_bwd_dq(qkv, cat, dcat, lse, bias, nb, seq, nh, comm=comm)
    dk, dv = _attn_bwd_dkv(qkv, dcat, lse, delta, bias, nb, seq, nh)
    dgb, dgc, dxin, dconvw = _conv_bwd(proj, conv_w, dcat, nb, seq, width)
    dproj = _assemble_dproj(dq, dk, dv, dgb, dgc, dxin, cosf, sinf, seq)
    du = _proj_bwd_x(dproj, wing)
    comm = reduce_start(["ab_w_in"], [_proj_dw(u, dproj, ws)])
    dh_in, dhb_in, dgain = _rms_bwd(du, h, gain, dh)
    return dh_in, dhb_in, dgain, dconvw, comm


def _s5_zoh(lr, li, log_dt):
    dt = jnp.exp(log_dt)
    mag = jnp.exp(lr * dt)
    ar = mag * jnp.cos(li * dt)
    ai = mag * jnp.sin(li * dt)
    den = lr * lr + li * li
    return dt, ar, ai, den, ((ar - 1.0) * lr + ai * li) / den, (ai * lr - (ar - 1.0) * li) / den


def _s5_discretize(lam_re, lam_im, log_dt, bt_re, bt_im):
    def body(lr_ref, li_ref, ld_ref, br_ref, bi_ref, ar_ref, ai_ref, bbr_ref, bbi_ref):
        _, ar, ai, _, fr, fi = _s5_zoh(lr_ref[...], li_ref[...], ld_ref[...])
        ar_ref[...] = ar
        ai_ref[...] = ai
        bbr_ref[...] = fr * br_ref[...] - fi * bi_ref[...]
        bbi_ref[...] = fr * bi_ref[...] + fi * br_ref[...]

    small = jax.ShapeDtypeStruct(lam_re.shape, F32)
    big = jax.ShapeDtypeStruct(bt_re.shape, F32)
    return pl.pallas_call(body, out_shape=[small, small, big, big], name="s5_discretize",
                          compiler_params=_cparams(0))(lam_re, lam_im, log_dt, bt_re, bt_im)


def _s5_discretize_bwd(lam_re, lam_im, log_dt, bt_re, bt_im, d_ar, d_ai, d_bbr, d_bbi):

    def body(lr_ref, li_ref, ld_ref, br_ref, bi_ref, dar_ref, dai_ref, dbbr_ref, dbbi_ref,
             dlr_ref, dli_ref, dld_ref, dbr_ref, dbi_ref):
        lr, li = lr_ref[...], li_ref[...]
        dt, ar, ai, den, fr, fi = _s5_zoh(lr, li, ld_ref[...])
        br, bi = br_ref[...], bi_ref[...]
        dbbr, dbbi = dbbr_ref[...], dbbi_ref[...]
        dbr_ref[...] = dbbr * fr + dbbi * fi
        dbi_ref[...] = dbbi * fr - dbbr * fi
        dfr = jnp.sum(dbbr * br + dbbi * bi, axis=1, keepdims=True)
        dfi = jnp.sum(dbbi * br - dbbr * bi, axis=1, keepdims=True)
        dnr = dfr / den
        dni = dfi / den
        dden = -(dfr * fr + dfi * fi) / den
        dar = dar_ref[...] + dnr * lr - dni * li
        dai = dai_ref[...] + dnr * li + dni * lr
        dlr_ref[...] = dnr * (ar - 1.0) + dni * ai + 2.0 * dden * lr + dt * (dar * ar + dai * ai)
        dli_ref[...] = dnr * ai - dni * (ar - 1.0) + 2.0 * dden * li + dt * (dai * ar - dar * ai)
        ddt = jnp.sum(dar * (lr * ar - li * ai) + dai * (lr * ai + li * ar), axis=2, keepdims=True)
        dld_ref[...] = ddt * dt

    small = jax.ShapeDtypeStruct(lam_re.shape, F32)
    big = jax.ShapeDtypeStruct(bt_re.shape, F32)
    return pl.pallas_call(
        body, out_shape=[small, small, jax.ShapeDtypeStruct(log_dt.shape, F32), big, big],
        name="s5_discretize_bwd", compiler_params=_cparams(0))(
            lam_re, lam_im, log_dt, bt_re, bt_im, d_ar, d_ai, d_bbr, d_bbi)


def _rows8(t):
    return pl.ds(pl.multiple_of(t * SUBLANES, SUBLANES), SUBLANES)


def _cmul_add(ar, ai, sr, si, br, bi):
    return ar * sr - ai * si + br, ar * si + ai * sr + bi


def _s5_specs(R, nj):
    sh = STATE_COLS // 2
    return dict(
        rows=pl.BlockSpec((R, LANES), lambda j, hh: (0, j)),
        bd=pl.BlockSpec((None, LANES, sh), lambda j, hh: (j, 0, hh)),
        cd=pl.BlockSpec((None, sh, LANES), lambda j, hh: (j, hh, 0)),
        a=pl.BlockSpec((None, 1, sh), lambda j, hh: (j, 0, hh)),
        vec=pl.BlockSpec((1, LANES), lambda j, hh: (0, j)),
        init=pl.BlockSpec((None, SUBLANES, sh), lambda j, hh: (j, 0, hh)))


def _s5_fwd(u, mats, seg_len, nseg, comm=None):
    bdr, bdi, cdr, cdi, are, aim, dsk = mats
    R, D = u.shape
    nj = D // LANES
    sh = STATE_COLS // 2
    rc = min(R, 1024)
    sp = _s5_specs(R, nj)

    def body(u_ref, bdr_ref, bdi_ref, cdr_ref, cdi_ref, ar_ref, ai_ref, d_ref,
             y_ref, yg_ref, ir_ref, ii_ref, sre, sim):
        hh = pl.program_id(1)
        ar = jnp.broadcast_to(ar_ref[...], (SUBLANES, sh))
        ai = jnp.broadcast_to(ai_ref[...], (SUBLANES, sh))

        def bu_chunk(c, _):
            rows = pl.ds(pl.multiple_of(c * rc, rc), rc)
            ub = u_ref[rows, :].astype(BF)
            sre[rows, :] = _dot(ub, bdr_ref[...], NN)
            sim[rows, :] = _dot(ub, bdi_ref[...], NN)
            return 0

        lax.fori_loop(0, R // rc, bu_chunk, 0)
        z = jnp.zeros((SUBLANES, sh), F32)

        def local_scan(t, c):
            return _cmul_add(ar, ai, c[0], c[1], sre[_rows8(t), :], sim[_rows8(t), :])

        er, ei = lax.fori_loop(0, seg_len, local_scan, (z, z))
        pr, pi = lax.fori_loop(0, seg_len - 1, lambda _, c: _cmul_add(ar, ai, c[0], c[1], 0.0, 0.0), (ar, ai))
        first = (lax.broadcasted_iota(jnp.int32, (SUBLANES, sh), 0) & (nseg - 1)) == 0

        def prev(x):
            return jnp.where(first, 0.0, pltpu.roll(x, 1, 0))

        xr, xi = er, ei
        for _ in range(nseg - 1):
            xr, xi = _cmul_add(pr, pi, prev(xr), prev(xi), er, ei)
        i_r, i_i = prev(xr), prev(xi)
        ir_ref[...] = i_r
        ii_ref[...] = i_i

        def scan(t, c):
            nr, ni = _cmul_add(ar, ai, c[0], c[1], sre[_rows8(t), :], sim[_rows8(t), :])
            sre[_rows8(t), :] = nr
            sim[_rows8(t), :] = ni
            return nr, ni

        lax.fori_loop(0, seg_len, scan, (i_r, i_i))

        def y_chunk(c, _):
            rows = pl.ds(pl.multiple_of(c * rc, rc), rc)
            y = _dot(sre[rows, :], cdr_ref[...], NN) + _dot(sim[rows, :], cdi_ref[...], NN)

            @pl.when(hh == 0)
            def _():
                y_ref[rows, :] = y + d_ref[...] * u_ref[rows, :]

            @pl.when(hh == 1)
            def _():
                yt = y_ref[rows, :] + y
                y_ref[rows, :] = yt
                yg_ref[rows, :] = _gelu(yt).astype(BF)

            return 0

        lax.fori_loop(0, R // rc, y_chunk, 0)

    init_sh = jax.ShapeDtypeStruct((nj, SUBLANES, STATE_COLS), F32)
    return _call(
        body, "s5_fwd", (nj, 2),
        [sp["rows"], sp["bd"], sp["bd"], sp["cd"], sp["cd"], sp["a"], sp["a"], sp["vec"]],
        [sp["rows"], sp["rows"], sp["init"], sp["init"]],
        [jax.ShapeDtypeStruct((R, D), F32), jax.ShapeDtypeStruct((R, D), BF), init_sh, init_sh],
        (u, bdr, bdi, cdr, cdi, are, aim, dsk),
        scratch=[pltpu.VMEM((R, sh), F32) for _ in range(2)], comm=comm)


def _s5_bwd(u, dy, mats, init_re, init_im, seg_len, nseg, comm=None):
    bdr, bdi, cdr, cdi, are, aim, dsk = mats
    R, D = u.shape
    nj = D // LANES
    sh = STATE_COLS // 2
    rc = min(R, 1024)
    sp = _s5_specs(R, nj)

    def body(u_ref, dy_ref, bdr_ref, bdi_ref, cdr_ref, cdi_ref, ar_ref, ai_ref, d_ref, ir_ref, ii_ref,
             du_ref, dbdr_ref, dbdi_ref, dcdr_ref, dcdi_ref, dar_ref, dai_ref, dd_ref,
             sre, sim, gre, gim):
        hh = pl.program_id(1)
        ar = jnp.broadcast_to(ar_ref[...], (SUBLANES, sh))
        ai = jnp.broadcast_to(ai_ref[...], (SUBLANES, sh))
        i_r, i_i = ir_ref[...], ii_ref[...]

        def chunk(c):
            return pl.ds(pl.multiple_of(c * rc, rc), rc)

        def bu_chunk(c, _):
            ub = u_ref[chunk(c), :].astype(BF)
            sre[chunk(c), :] = _dot(ub, bdr_ref[...], NN)
            sim[chunk(c), :] = _dot(ub, bdi_ref[...], NN)
            return 0

        lax.fori_loop(0, R // rc, bu_chunk, 0)

        def scan(t, c):
            nr, ni = _cmul_add(ar, ai, c[0], c[1], sre[_rows8(t), :], sim[_rows8(t), :])
            sre[_rows8(t), :] = nr
            sim[_rows8(t), :] = ni
            return nr, ni

        lax.fori_loop(0, seg_len, scan, (i_r, i_i))

        def c_chunk(c, carry):
            dyb = dy_ref[chunk(c), :].astype(BF)
            gre[chunk(c), :] = _dot(dyb, cdr_ref[...], NT)
            gim[chunk(c), :] = _dot(dyb, cdi_ref[...], NT)
            return (carry[0] + _dot(sre[chunk(c), :], dyb, TN), carry[1] + _dot(sim[chunk(c), :], dyb, TN))

        zc = jnp.zeros((sh, LANES), F32)
        dcr, dci = lax.fori_loop(0, R // rc, c_chunk, (zc, zc))
        dcdr_ref[...] = dcr
        dcdi_ref[...] = dci

        def adj(t, gr_next, gi_next):
            return _cmul_add(ar, -ai, gr_next, gi_next, gre[_rows8(t), :], gim[_rows8(t), :])

        z = jnp.zeros((SUBLANES, sh), F32)
        fr, fi = lax.fori_loop(0, seg_len, lambda i, c: adj(seg_len - 1 - i, c[0], c[1]), (z, z))
        pr, pi = lax.fori_loop(0, seg_len - 1, lambda _, c: _cmul_add(ar, ai, c[0], c[1], 0.0, 0.0), (ar, ai))
        last = (lax.broadcasted_iota(jnp.int32, (SUBLANES, sh), 0) & (nseg - 1)) == nseg - 1

        def nxt(x):
            return jnp.where(last, 0.0, pltpu.roll(x, SUBLANES - 1, 0))

        xr, xi = fr, fi
        for _ in range(nseg - 1):
            xr, xi = _cmul_add(pr, -pi, nxt(xr), nxt(xi), fr, fi)
        g0r, g0i = nxt(xr), nxt(xi)

        def adj_scan(i, c):
            t = seg_len - 1 - i
            gr, gi = adj(t, c[0], c[1])
            gre[_rows8(t), :] = gr
            gim[_rows8(t), :] = gi
            spr, spi = sre[_rows8(t - 1), :], sim[_rows8(t - 1), :]
            return gr, gi, c[2] + spr * gr + spi * gi, c[3] + spr * gi - spi * gr

        gr, gi, dar, dai = lax.fori_loop(0, seg_len - 1, adj_scan, (g0r, g0i, z, z))
        gr, gi = adj(0, gr, gi)
        gre[_rows8(0), :] = gr
        gim[_rows8(0), :] = gi
        dar_ref[...] = jnp.sum(dar + i_r * gr + i_i * gi, axis=0, keepdims=True)
        dai_ref[...] = jnp.sum(dai + i_r * gi - i_i * gr, axis=0, keepdims=True)

        def d_chunk(c, carry):
            ub = u_ref[chunk(c), :].astype(BF)
            grb = gre[chunk(c), :].astype(BF)
            gib = gim[chunk(c), :].astype(BF)
            du = _dot(grb, bdr_ref[...], NT) + _dot(gib, bdi_ref[...], NT)

            @pl.when(hh == 0)
            def _():
                du_ref[chunk(c), :] = du + d_ref[...] * dy_ref[chunk(c), :]

            @pl.when(hh == 1)
            def _():
                du_ref[chunk(c), :] += du

            dd = carry[2] + jnp.sum(dy_ref[chunk(c), :] * u_ref[chunk(c), :], axis=0, keepdims=True)
            return carry[0] + _dot(ub, grb, TN), carry[1] + _dot(ub, gib, TN), dd

        zb = jnp.zeros((LANES, sh), F32)
        dbr, dbi, dd = lax.fori_loop(0, R // rc, d_chunk, (zb, zb, jnp.zeros((1, LANES), F32)))
        dbdr_ref[...] = dbr
        dbdi_ref[...] = dbi
        dd_ref[...] = dd

    bd_sh = jax.ShapeDtypeStruct((nj, LANES, STATE_COLS), F32)
    cd_sh = jax.ShapeDtypeStruct((nj, STATE_COLS, LANES), F32)
    a_sh = jax.ShapeDtypeStruct((nj, 1, STATE_COLS), F32)
    return _call(
        body, "s5_bwd", (nj, 2),
        [sp["rows"], sp["rows"], sp["bd"], sp["bd"], sp["cd"], sp["cd"], sp["a"], sp["a"],
         sp["vec"], sp["init"], sp["init"]],
        [sp["rows"], sp["bd"], sp["bd"], sp["cd"], sp["cd"], sp["a"], sp["a"], sp["vec"]],
        [jax.ShapeDtypeStruct((R, D), F32), bd_sh, bd_sh, cd_sh, cd_sh, a_sh, a_sh,
         jax.ShapeDtypeStruct((1, D), F32)],
        (u, dy, bdr, bdi, cdr, cdi, are, aim, dsk, init_re, init_im),
        scratch=[pltpu.VMEM((R, sh), F32) for _ in range(4)], comm=comm)


def _glu_fwd(yg, wa, wb, h):
    T, D = yg.shape
    N = wa.shape[1]
    bm = min(ROW_TILE, T)
    bn = min(ROW_TILE, N)
    wspec = pl.BlockSpec((D, bn), lambda i, j, k: (0, j))
    ospec = pl.BlockSpec((bm, bn), lambda i, j, k: (i, j))

    def epi(accs, ex, outs):
        pa, pb = accs
        outs[0][...] = ex[0][...] + pa * _sig(pb)
        outs[1][...] = pa.astype(BF)
        outs[2][...] = pb.astype(BF)

    return _mm("glu_fwd", (T // bm, N // bn, 1),
               [(yg, pl.BlockSpec((bm, D), lambda i, j, k: (i, 0))), (wa, wspec), (wb, wspec)],
               [(0, 1, NN, 0), (0, 2, NN, 1)], 2, None, [(h, ospec)],
               [(jax.ShapeDtypeStruct((T, N), F32), ospec), (jax.ShapeDtypeStruct((T, N), BF), ospec),
                (jax.ShapeDtypeStruct((T, N), BF), ospec)], epi)


def _glu_bwd_gates(dz, pa, pb):
    T, D = dz.shape
    bm = min(ROW_TILE, T)

    def body(dz_ref, pa_ref, pb_ref, dpa_ref, dpb_ref):
        dz = dz_ref[...]
        sg = _sig(pb_ref[...].astype(F32))
        dpa_ref[...] = (dz * sg).astype(BF)
        dpb_ref[...] = (dz * pa_ref[...].astype(F32) * sg * (1.0 - sg)).astype(BF)

    row = pl.BlockSpec((bm, D), lambda i: (i, 0))
    return pl.pallas_call(
        body, grid=(T // bm,), in_specs=[row] * 3, out_specs=[row] * 2,
        out_shape=[jax.ShapeDtypeStruct((T, D), BF)] * 2, name="glu_bwd_gates",
        compiler_params=_cparams(1))(dz, pa, pb)


def _glu_bwd_y(dpa, dpb, wa, wb, y_pre):
    T, N = dpa.shape
    D = wa.shape[0]
    bm = min(ROW_TILE, T)
    bn = min(ROW_TILE, D)
    aspec = pl.BlockSpec((bm, N), lambda i, j, k: (i, 0))
    wspec = pl.BlockSpec((bn, N), lambda i, j, k: (j, 0))
    ospec = pl.BlockSpec((bm, bn), lambda i, j, k: (i, j))

    def epi(accs, ex, outs):
        outs[0][...] = accs[0] * _gelu_grad(ex[0][...])

    return _mm("glu_bwd_y", (T // bm, D // bn, 1), [(dpa, aspec), (wa, wspec), (dpb, aspec), (wb, wspec)],
               [(0, 1, NT, 0), (2, 3, NT, 0)], 1, None, [(y_pre, ospec)],
               [(jax.ShapeDtypeStruct((T, D), F32), ospec)], epi)[0]


def _block_diag_in(x, nj):
    g = GROUPS_PER_BLOCK
    x = x.reshape(nj, g, 1, S5_GROUP, S5_STATE)
    eye = jnp.eye(g, dtype=bool)[None, :, :, None, None]
    full = jnp.where(eye, x, 0.0)
    return full.transpose(0, 1, 3, 2, 4).reshape(nj, g * S5_GROUP, g * S5_STATE)


def _block_diag_out(x, nj):
    return _block_diag_in(x, nj).transpose(0, 2, 1)


def _diag_of_in(m, nj):
    g = GROUPS_PER_BLOCK
    m5 = m.reshape(nj, g, S5_GROUP, g, S5_STATE)
    d = jnp.diagonal(m5, axis1=1, axis2=3)
    return d.transpose(0, 3, 1, 2).reshape(nj * g, S5_GROUP, S5_STATE)


def _mixer_s5_fwd(h, gain, p, dsk, wa, wb, nb, seq, comm_s5=None):
    T, D = h.shape
    nj = D // LANES
    nseg = SUBLANES // nb
    seg_len = seq // nseg
    G = p["s5_lambda_re"].shape[1]
    lam_re = p["s5_lambda_re"].reshape(G, 1, S5_STATE)
    lam_im = p["s5_lambda_im"].reshape(G, 1, S5_STATE)
    log_dt = p["s5_log_dt"].reshape(G, 1, 1)
    bt_re = p["s5_b_re"][0].transpose(0, 2, 1)
    bt_im = p["s5_b_im"][0].transpose(0, 2, 1)
    ar, ai, bbr, bbi = _s5_discretize(lam_re, lam_im, log_dt, bt_re, bt_im)
    mats = (_block_diag_in(bbr, nj).astype(BF), _block_diag_in(bbi, nj).astype(BF),
            _block_diag_out(p["s5_c_re"][0], nj).astype(BF),
            _block_diag_out(-p["s5_c_im"][0], nj).astype(BF),
            ar.reshape(nj, 1, STATE_COLS), ai.reshape(nj, 1, STATE_COLS), dsk)
    h_seg = _to_seg(h, seg_len)
    u = _rms_fwd(h_seg, gain, F32)
    y_pre, yg, init_re, init_im = _s5_fwd(u, mats, seg_len, nseg, comm=comm_s5)
    h_out, pa, pb = _glu_fwd(yg, wa, wb, h_seg)
    disc_in = (lam_re, lam_im, log_dt, bt_re, bt_im)
    return _to_tok(h_out, seg_len), (h_seg, u, mats, y_pre, yg, init_re, init_im, pa, pb, disc_in, seg_len, nseg)


def _mixer_s5_bwd(dh, saved, gain, wa, wb, reduce_start):
    h_seg, u, mats, y_pre, yg, init_re, init_im, pa, pb, disc_in, seg_len, nseg = saved
    T, D = h_seg.shape
    nj = D // LANES
    G = nj * GROUPS_PER_BLOCK
    dh_seg = _to_seg(dh, seg_len)
    dpa, dpb = _glu_bwd_gates(dh_seg, pa, pb)
    dy = _glu_bwd_y(dpa, dpb, wa, wb, y_pre)
    dwa, dwb = _mm_tn("glu_dw", yg, [dpa, dpb])
    comm = reduce_start(["s5_glu_wa", "s5_glu_wb"], [dwa.reshape(N_DEV, -1, D), dwb.reshape(N_DEV, -1, D)])
    du, dbdr, dbdi, dcdr, dcdi, dar, dai, dd = _s5_bwd(u, dy, mats, init_re, init_im, seg_len, nseg, comm=comm)
    d_bbr = _diag_of_in(dbdr, nj)
    d_bbi = _diag_of_in(dbdi, nj)
    d_c_re = _diag_of_in(dcdr.transpose(0, 2, 1), nj)
    d_c_im = -_diag_of_in(dcdi.transpose(0, 2, 1), nj)
    dlr, dli, dld, dbr, dbi = _s5_discretize_bwd(
        *disc_in, dar.reshape(G, 1, S5_STATE), dai.reshape(G, 1, S5_STATE), d_bbr, d_bbi)
    small = {"s5_lambda_re": dlr.reshape(1, G, S5_STATE), "s5_lambda_im": dli.reshape(1, G, S5_STATE),
             "s5_log_dt": dld.reshape(1, G),
             "s5_b_re": dbr.transpose(0, 2, 1)[None], "s5_b_im": dbi.transpose(0, 2, 1)[None],
             "s5_c_re": d_c_re[None], "s5_c_im": d_c_im[None], "s5_d": dd}
    dh_in, _, dgain = _rms_bwd(du, h_seg, gain, dh_seg)
    dh_in = _to_tok(dh_in, seg_len)
    return dh_in, dh_in.astype(BF), dgain, small


def _mesh_pos():
    return lax.axis_index("x"), lax.axis_index("y"), lax.axis_index("c")


class _Gather:
    def __init__(self, srcs, slots, send_sems, recv_sems):
        self.srcs, self.slots, self.send_sems, self.recv_sems = srcs, slots, send_sems, recv_sems
        x, y, c = _mesh_pos()
        self.c = c
        self.me, self.sib = (x, y, c), (x, y, 1 - c)
        self.chips = [(1 - x, y), (x, 1 - y), (1 - x, 1 - y)]

    def copy(self, a, k, block, to, own=False):
        dst = self.slots[a].at[4 * block[0] + 2 * block[1] + block[2]]
        return pltpu.make_async_remote_copy(
            src_ref=self.srcs[a] if own else dst, dst_ref=dst, send_sem=self.send_sems.at[7 * a + k],
            recv_sem=self.recv_sems.at[7 * a + k], device_id=to, device_id_type=MESH)

    def own_copies(self, a):
        cps = [self.copy(a, 0, self.me, self.sib, own=True)]
        return cps + [self.copy(a, 1 + j, self.me, (*chip, self.c), own=True) for j, chip in enumerate(self.chips)]

    def start(self):
        for a in range(len(self.srcs)):
            for cp in self.own_copies(a):
                cp.start()

    def finish(self):
        n = len(self.srcs)
        for a in range(n):
            for j, chip in enumerate(self.chips):
                self.copy(a, 1 + j, (*chip, self.c), self.me).wait_recv()
                self.copy(a, 4 + j, (*chip, self.c), self.sib).start()
        for a in range(n):
            self.copy(a, 0, self.sib, self.me).wait_recv()
            for j, chip in enumerate(self.chips):
                self.copy(a, 4 + j, (*chip, 1 - self.c), self.me).wait_recv()
        for a in range(n):
            for cp in self.own_copies(a):
                cp.wait_send()
            for j, chip in enumerate(self.chips):
                self.copy(a, 4 + j, (*chip, self.c), self.sib).wait_send()


def _gather_comm(arrs):
    n = len(arrs)

    def local(xs, outs, sems, a):
        x, y, c = _mesh_pos()
        return pltpu.make_async_copy(xs[a], outs[a].at[4 * x + 2 * y + c], sems[2].at[a])

    def start(xs, outs, sems):
        for a in range(n):
            local(xs, outs, sems, a).start()
        _Gather(xs, outs, sems[0], sems[1]).start()

    def finish(xs, outs, sems):
        _Gather(xs, outs, sems[0], sems[1]).finish()
        for a in range(n):
            local(xs, outs, sems, a).wait()

    return _Comm(list(arrs), [jax.ShapeDtypeStruct((N_DEV,) + a.shape, a.dtype) for a in arrs],
                 [pltpu.SemaphoreType.DMA((7 * n,)), pltpu.SemaphoreType.DMA((7 * n,)),
                  pltpu.SemaphoreType.DMA((n,))], start, finish)


def _exchange_comm(parts):
    n = len(parts)

    def copies(ps, outs, sems):
        x, y, c = _mesh_pos()
        cps = []
        for a in range(n):
            for j in range(1, 4):
                to = (jnp.bitwise_xor(x, j // 2), jnp.bitwise_xor(y, j % 2), c)
                cps.append(pltpu.make_async_remote_copy(
                    src_ref=ps[a].at[j], dst_ref=outs[a].at[j - 1], send_sem=sems[0].at[3 * a + j - 1],
                    recv_sem=sems[1].at[3 * a + j - 1], device_id=to, device_id_type=MESH))
        return cps

    def start(ps, outs, sems):
        for cp in copies(ps, outs, sems):
            cp.start()

    def finish(ps, outs, sems):
        for cp in copies(ps, outs, sems):
            cp.wait()

    return _Comm(list(parts), [jax.ShapeDtypeStruct((3,) + p.shape[1:], p.dtype) for p in parts],
                 [pltpu.SemaphoreType.DMA((3 * n,)), pltpu.SemaphoreType.DMA((3 * n,))], start, finish)


def _run_comm(comm, name):
    ci, co = len(comm.ins), len(comm.outs)

    def body(*refs):
        comm.start(refs[:ci], refs[ci:ci + co], refs[ci + co:])
        comm.finish(refs[:ci], refs[ci:ci + co], refs[ci + co:])

    any_spec = pl.BlockSpec(memory_space=pl.ANY)
    comm.results = list(pl.pallas_call(
        body, in_specs=[any_spec] * ci, out_specs=[any_spec] * co, out_shape=list(comm.outs),
        scratch_shapes=list(comm.sems), name=name, compiler_params=_cparams(0))(*comm.ins))


def _pair_exchange(grads, name):
    n = len(grads)

    def body(*refs):
        gs, outs = refs[:n], refs[n:2 * n]
        send_sems, recv_sems = refs[2 * n:]
        x, y, c = _mesh_pos()
        copies = []
        for a in range(n):
            for k in range(4):
                copies.append(pltpu.make_async_remote_copy(
                    src_ref=gs[a].at[2 * k + 1 - c], dst_ref=outs[a].at[k], send_sem=send_sems.at[4 * a + k],
                    recv_sem=recv_sems.at[4 * a + k], device_id=(x, y, 1 - c), device_id_type=MESH))
        for cp in copies:
            cp.start()
        for cp in copies:
            cp.wait()

    any_spec = pl.BlockSpec(memory_space=pl.ANY)
    return pl.pallas_call(
        body, in_specs=[any_spec] * n, out_specs=[any_spec] * n,
        out_shape=[jax.ShapeDtypeStruct((4,) + g.shape[1:], g.dtype) for g in grads],
        scratch_shapes=[pltpu.SemaphoreType.DMA((4 * n,)), pltpu.SemaphoreType.DMA((4 * n,))],
        name=name, compiler_params=_cparams(0))(*grads)


def _pair_sum(grad, recv, pos):
    _, R, C = grad.shape
    br = _row_block(R, C)

    def body(pos_ref, g_ref, r_ref, o_ref):
        o_ref[...] = (g_ref[...].astype(F32) + r_ref[...].astype(F32)).astype(BF)

    def chip(j, p):
        return jnp.bitwise_xor(p[1], j)

    return pl.pallas_call(
        body, grid_spec=pltpu.PrefetchScalarGridSpec(
            num_scalar_prefetch=1, grid=(4, R // br),
            in_specs=[pl.BlockSpec((None, br, C), lambda j, i, p: (2 * chip(j, p) + p[0], i, 0)),
                      pl.BlockSpec((None, br, C), lambda j, i, p: (chip(j, p), i, 0))],
            out_specs=pl.BlockSpec((None, br, C), lambda j, i, p: (j, i, 0))),
        out_shape=jax.ShapeDtypeStruct((4, R, C), BF), name="pair_sum", compiler_params=_cparams(2))(pos, grad, recv)


def _adamw(w, g, m, v):
    m = ADAM_B1 * m + (1.0 - ADAM_B1) * g
    v = ADAM_B2 * v + (1.0 - ADAM_B2) * (g * g)
    m_hat = m / (1.0 - ADAM_B1 ** ADAM_STEP)
    v_hat = v / (1.0 - ADAM_B2 ** ADAM_STEP)
    return -ADAM_LR * (m_hat / (jnp.sqrt(v_hat) + ADAM_EPS) + ADAM_WD * w), m, v


def _adamw_piece(w, m, v, piece, part, recv, bufs):
    _, R, C = w.shape
    br = _row_block(R, C)

    def body(w_ref, m_ref, v_ref, p_ref, r_ref, b0, b1, b2, b3, g_ref, d_ref, nm_ref, nv_ref):
        g = p_ref[...].astype(F32)
        for j in range(3):
            g = g + r_ref[j].astype(F32)
        d, nm, nv = _adamw(w_ref[...], g, m_ref[...], v_ref[...])
        g_ref[...] = g
        d_ref[...] = d
        nm_ref[...] = nm
        nv_ref[...] = nv

    row = pl.BlockSpec((None, br, C), lambda i: (piece, i, 0))
    any_spec = pl.BlockSpec(memory_space=pl.ANY)
    return pl.pallas_call(
        body, grid=(R // br,),
        in_specs=[row, row, row, pl.BlockSpec((None, br, C), lambda i: (0, i, 0)),
                  pl.BlockSpec((3, br, C), lambda i: (0, i, 0))] + [any_spec] * 4,
        out_specs=[row] * 4, out_shape=[jax.ShapeDtypeStruct(w.shape, F32)] * 4,
        input_output_aliases={5: 0, 6: 1, 7: 2, 8: 3}, name="adamw_piece",
        compiler_params=_cparams(1))(w, m, v, part, recv, *bufs)


def _all_reduce_small(x):
    rows = x.shape[0]

    def body(x_ref, o_ref, buf, send_sems, recv_sems):
        xp, yp, cp = _mesh_pos()
        buf[4 * xp + 2 * yp + cp] = x_ref[...]
        gather = _Gather([x_ref], [buf], send_sems, recv_sems)
        gather.start()
        gather.finish()
        acc = buf[0]
        for d in range(1, N_DEV):
            acc = acc + buf[d]
        o_ref[...] = acc

    vm = pl.BlockSpec(memory_space=pltpu.VMEM)
    return pl.pallas_call(
        body, in_specs=[vm], out_specs=vm, out_shape=jax.ShapeDtypeStruct(x.shape, F32),
        scratch_shapes=[pltpu.VMEM((N_DEV, rows, LANES), F32), pltpu.SemaphoreType.DMA((7,)),
                        pltpu.SemaphoreType.DMA((7,))],
        name="all_reduce_small", compiler_params=_cparams(0))(x)


def _adamw_small(w, g, m, v):
    def body(w_ref, g_ref, m_ref, v_ref, d_ref, nm_ref, nv_ref):
        d, nm, nv = _adamw(w_ref[...], g_ref[...], m_ref[...], v_ref[...])
        d_ref[...] = d
        nm_ref[...] = nm
        nv_ref[...] = nv

    sh = jax.ShapeDtypeStruct(w.shape, F32)
    return pl.pallas_call(body, out_shape=[sh] * 3, name="adamw_small", compiler_params=_cparams(0))(w, g, m, v)


def _pack(arrs):
    flat = jnp.concatenate([a.reshape(-1).astype(F32) for a in arrs])
    rows = -(-flat.shape[0] // (SUBLANES * LANES)) * SUBLANES
    return jnp.pad(flat, (0, rows * LANES - flat.shape[0])).reshape(rows, LANES)


def _unpack(buf, shapes):
    flat = buf.reshape(-1)
    out, off = [], 0
    for s in shapes:
        n = 1
        for d in s:
            n *= d
        out.append(flat[off:off + n].reshape(s))
        off += n
    return out


BIG = ("ffn_w1", "ffn_w3", "ffn_w2", "ab_w_in", "ab_w_out", "s5_glu_wa", "s5_glu_wb")
NAMES = ("ln_ffn_pre", "ln_mix", "ln_ffn_post", "ln_final", "ffn_w1", "ffn_w3", "ffn_w2", "ab_w_in",
         "ab_conv_w", "ab_w_out", "s5_lambda_re", "s5_lambda_im", "s5_log_dt", "s5_b_re", "s5_b_im",
         "s5_c_re", "s5_c_im", "s5_d", "s5_glu_wa", "s5_glu_wb")


def kernel(x, ln_ffn_pre, ln_mix, ln_ffn_post, ln_final, ffn_w1, ffn_w3, ffn_w2, ab_w_in, ab_conv_w, ab_w_out, s5_lambda_re, s5_lambda_im, s5_log_dt, s5_b_re, s5_b_im, s5_c_re, s5_c_im, s5_d, s5_glu_wa, s5_glu_wb, loss_target, m_ln_ffn_pre, m_ln_mix, m_ln_ffn_post, m_ln_final, m_ffn_w1, m_ffn_w3, m_ffn_w2, m_ab_w_in, m_ab_conv_w, m_ab_w_out, m_s5_lambda_re, m_s5_lambda_im, m_s5_log_dt, m_s5_b_re, m_s5_b_im, m_s5_c_re, m_s5_c_im, m_s5_d, m_s5_glu_wa, m_s5_glu_wb, v_ln_ffn_pre, v_ln_mix, v_ln_ffn_post, v_ln_final, v_ffn_w1, v_ffn_w3, v_ffn_w2, v_ab_w_in, v_ab_conv_w, v_ab_w_out, v_s5_lambda_re, v_s5_lambda_im, v_s5_log_dt, v_s5_b_re, v_s5_b_im, v_s5_c_re, v_s5_c_im, v_s5_d, v_s5_glu_wa, v_s5_glu_wb):
    w = dict(zip(NAMES, (ln_ffn_pre, ln_mix, ln_ffn_post, ln_final, ffn_w1, ffn_w3, ffn_w2, ab_w_in, ab_conv_w,
                         ab_w_out, s5_lambda_re, s5_lambda_im, s5_log_dt, s5_b_re, s5_b_im, s5_c_re, s5_c_im,
                         s5_d, s5_glu_wa, s5_glu_wb)))
    mom = dict(zip(NAMES, (m_ln_ffn_pre, m_ln_mix, m_ln_ffn_post, m_ln_final, m_ffn_w1, m_ffn_w3, m_ffn_w2,
                           m_ab_w_in, m_ab_conv_w, m_ab_w_out, m_s5_lambda_re, m_s5_lambda_im, m_s5_log_dt,
                           m_s5_b_re, m_s5_b_im, m_s5_c_re, m_s5_c_im, m_s5_d, m_s5_glu_wa, m_s5_glu_wb)))
    var = dict(zip(NAMES, (v_ln_ffn_pre, v_ln_mix, v_ln_ffn_post, v_ln_final, v_ffn_w1, v_ffn_w3, v_ffn_w2,
                           v_ab_w_in, v_ab_conv_w, v_ab_w_out, v_s5_lambda_re, v_s5_lambda_im, v_s5_log_dt,
                           v_s5_b_re, v_s5_b_im, v_s5_c_re, v_s5_c_im, v_s5_d, v_s5_glu_wa, v_s5_glu_wb)))
    nb, seq, D = x.shape
    T = nb * seq
    assert ln_mix.shape[0] == 2 and ab_w_in.shape[0] == 1 and s5_glu_wa.shape[0] == 1
    xc, yc, cc = _mesh_pos()
    dev = 4 * xc + 2 * yc + cc
    pos = jnp.stack([cc, 2 * xc + yc]).astype(jnp.int32)
    bq = min(256, seq)
    tabs = _rope_tables(seq) + (_branch_bias(seq // bq, bq),)

    def ffn_piece(k, li, fj):
        return w[k][li, fj].astype(BF)

    g0 = _gather_comm([ffn_piece("ffn_w1", 0, 0), ffn_piece("ffn_w3", 0, 0), ab_conv_w[0], s5_d])
    _run_comm(g0, "gather_first")
    w1, w3 = {(0, 0): g0.results[0]}, {(0, 0): g0.results[1]}
    w2 = {}
    conv_w = g0.results[2].transpose(1, 0, 2).reshape(3, -1)
    dsk = g0.results[3].reshape(1, D)
    gains = {k: [w[k][i:i + 1] for i in range(2)] for k in ("ln_ffn_pre", "ln_mix", "ln_ffn_post")}

    h = x.reshape(T, D)
    saved = {}

    def ffn_fwd(h, gain, key, comm_up):
        n = _rms_fwd(h, gain, BF)
        a1, a3, g = _ffn_up(n, w1[key], w3[key], comm=comm_up)
        return n, a1, a3, g

    c_up = _gather_comm([ffn_piece("ffn_w2", 0, 0), ab_w_in[0].astype(BF)])
    n, a1, a3, g = ffn_fwd(h, gains["ln_ffn_pre"][0], (0, 0), c_up)
    w2[(0, 0)], wing = c_up.results
    c_dn = _gather_comm([ab_w_out[0].astype(BF), ffn_piece("ffn_w1", 0, 1)])
    saved["pre0"] = (h, n, a1, a3, g)
    h = _ffn_down(g, w2[(0, 0)], h, comm=c_dn)
    wout = c_dn.results[0].reshape(-1, D)
    w1[(0, 1)] = c_dn.results[1]
    c_proj = _gather_comm([ffn_piece("ffn_w3", 0, 1)])
    c_attn = _gather_comm([ffn_piece("ffn_w2", 0, 1), ffn_piece("ffn_w1", 1, 0)])
    h, saved["mix0"] = _mixer_ab_fwd(h, gains["ln_mix"][0], wing, conv_w, wout, tabs, nb, seq, c_proj, c_attn)
    w3[(0, 1)] = c_proj.results[0]
    w2[(0, 1)], w1[(1, 0)] = c_attn.results
    c_up = _gather_comm([ffn_piece("ffn_w3", 1, 0), s5_glu_wa[0].astype(BF)])
    n, a1, a3, g = ffn_fwd(h, gains["ln_ffn_post"][0], (0, 1), c_up)
    w3[(1, 0)] = c_up.results[0]
    wa = c_up.results[1].reshape(-1, D)
    c_dn = _gather_comm([ffn_piece("ffn_w2", 1, 0)])
    saved["post0"] = (h, n, a1, a3, g)
    h = _ffn_down(g, w2[(0, 1)], h, comm=c_dn)
    w2[(1, 0)] = c_dn.results[0]
    c_up = _gather_comm([s5_glu_wb[0].astype(BF), ffn_piece("ffn_w1", 1, 1)])
    n, a1, a3, g = ffn_fwd(h, gains["ln_ffn_pre"][1], (1, 0), c_up)
    wb = c_up.results[0].reshape(-1, D)
    w1[(1, 1)] = c_up.results[1]
    c_dn = _gather_comm([ffn_piece("ffn_w3", 1, 1)])
    saved["pre1"] = (h, n, a1, a3, g)
    h = _ffn_down(g, w2[(1, 0)], h, comm=c_dn)
    w3[(1, 1)] = c_dn.results[0]
    c_s5 = _gather_comm([ffn_piece("ffn_w2", 1, 1)])
    h, saved["mix1"] = _mixer_s5_fwd(h, gains["ln_mix"][1], w, dsk, wa, wb, nb, seq, c_s5)
    w2[(1, 1)] = c_s5.results[0]
    n, a1, a3, g = ffn_fwd(h, gains["ln_ffn_post"][1], (1, 1), None)
    saved["post1"] = (h, n, a1, a3, g)
    h = _ffn_down(g, w2[(1, 1)], h)
    dh, dhb, d_ln_final, loss_part = _loss_head(h, ln_final.reshape(1, D), loss_target.reshape(T, D))
    loss = lax.psum(loss_part[0, 0], ("x", "y", "c"))

    reduced = {}

    def reduce_start(names, grads):
        recv = _pair_exchange(grads, "pair_exchange")
        parts = [_pair_sum(g, r, pos) for g, r in zip(grads, recv)]
        comm = _exchange_comm(parts)
        for k, (nm, p) in enumerate(zip(names, parts)):
            reduced[nm] = (p, comm, k)
        return comm

    def ffn_bwd(dh, dhb, key, tag, gain, comm_first=None):
        h_in, n, a1, a3, g = saved[tag]
        da1, da3 = _ffn_bwd_hidden(dhb, w2[key], a1, a3, comm=comm_first)
        c2 = reduce_start([("ffn_w2",) + key], [_ffn_dw2(g, dhb)])
        dw1, dw3 = _ffn_dw13(n, da1, da3, comm=c2)
        c13 = reduce_start([("ffn_w1",) + key, ("ffn_w3",) + key], [dw1, dw3])
        dn = _ffn_dn(da1, da3, w1[key], w3[key], comm=c13)
        return _rms_bwd(dn, h_in, gain, dh)

    g_small = {"ln_final": d_ln_final.reshape(D)}
    g_ln = {k: [None, None] for k in gains}
    dh, dhb, g_ln["ln_ffn_post"][1] = ffn_bwd(dh, dhb, (1, 1), "post1", gains["ln_ffn_post"][1])
    dh, dhb, g_ln["ln_mix"][1], s5_small = _mixer_s5_bwd(dh, saved["mix1"], gains["ln_mix"][1], wa, wb, reduce_start)
    g_small.update(s5_small)
    dh, dhb, g_ln["ln_ffn_pre"][1] = ffn_bwd(dh, dhb, (1, 0), "pre1", gains["ln_ffn_pre"][1])
    dh, dhb, g_ln["ln_ffn_post"][0] = ffn_bwd(dh, dhb, (0, 1), "post0", gains["ln_ffn_post"][0])
    dh, dhb, g_ln["ln_mix"][0], g_small["ab_conv_w"], c_win = _mixer_ab_bwd(
        dh, dhb, saved["mix0"], gains["ln_mix"][0], wing, conv_w, wout, tabs, nb, seq, reduce_start)
    dh, dhb, g_ln["ln_ffn_pre"][0] = ffn_bwd(dh, dhb, (0, 0), "pre0", gains["ln_ffn_pre"][0], comm_first=c_win)
    grad_x = dh.reshape(nb, seq, D)
    for k in g_ln:
        g_small[k] = jnp.concatenate(g_ln[k], axis=0)

    out = {}
    for k in BIG:
        cols = w[k].shape[-1]
        pieces = [(li, fj) for li in range(2) for fj in range(2)] if w[k].ndim == 4 else [None]
        view = (len(pieces), -1, cols)
        w3d, m3d, v3d = w[k].reshape(view), mom[k].reshape(view), var[k].reshape(view)
        bufs = [lax.empty(w3d.shape, F32) for _ in range(4)]
        for q, key in enumerate(pieces):
            part, comm, slot = reduced[k if key is None else (k,) + key]
            bufs = _adamw_piece(w3d, m3d, v3d, q, part, comm.results[slot], bufs)
        out[k] = [t.reshape(w[k].shape) for t in bufs]

    small_names = [k for k in NAMES if k not in BIG]
    red = _unpack(_all_reduce_small(_pack([g_small[k] for k in small_names])),
                  [g_small[k].shape for k in small_names])
    g_red = dict(zip(small_names, red))
    cw = w["ab_conv_w"].shape[-1]
    g_red["ab_conv_w"] = lax.dynamic_slice_in_dim(g_red["ab_conv_w"], dev * cw, cw, axis=1)[None]
    dsz = w["s5_d"].shape[-1]
    g_red["s5_d"] = lax.dynamic_slice_in_dim(g_red["s5_d"].reshape(1, -1), dev * dsz, dsz, axis=1)
    shapes = [w[k].shape for k in small_names]
    g_red = {k: g_red[k].reshape(w[k].shape) for k in small_names}
    d_s, m_s, v_s = _adamw_small(_pack([w[k] for k in small_names]), _pack([g_red[k] for k in small_names]),
                                 _pack([mom[k] for k in small_names]), _pack([var[k] for k in small_names]))
    for k, d, nm, nv in zip(small_names, _unpack(d_s, shapes), _unpack(m_s, shapes), _unpack(v_s, shapes)):
        out[k] = [g_red[k], d, nm, nv]

    return (loss, grad_x, *[out[k][0] for k in NAMES], *[out[k][1] for k in NAMES],
            *[out[k][2] for k in NAMES], *[out[k][3] for k in NAMES])
```

```python
import jax
import jax.numpy as jnp
from jax import lax
from jax.experimental import pallas as pl
from jax.experimental.pallas import tpu as pltpu

F32, BF = jnp.float32, jnp.bfloat16
N_DEV = 8
MESH = pl.DeviceIdType.MESH
LANES = 128
SUBLANES = 8
VMEM_LIMIT = 56 * 2 ** 20
ROW_TILE = 512
COL_TILE = 512
ATTN_TILE = 512
SCAN_UNROLL = 4
ELEMS_PER_BLOCK = 256 * 1024
RMS_EPS = 1e-6
ROPE_THETA = 10000.0
NEG_INF = -1e30
S5_STATE = 64
S5_GROUP = 16
GROUPS_PER_BLOCK = LANES // S5_GROUP
STATE_COLS = GROUPS_PER_BLOCK * S5_STATE
DILATED_PATTERN = ((128, 1), (512, 4), (2048, 16))
ADAM_LR, ADAM_B1, ADAM_B2, ADAM_EPS, ADAM_WD, ADAM_STEP = 0.001, 0.9, 0.999, 1e-08, 0.01, 10
GELU_C = 0.7978845608028654
GELU_A = 0.044715


def _cparams(n_grid, vmem=VMEM_LIMIT):
    sem = ("arbitrary",) * n_grid if n_grid else None
    return pltpu.CompilerParams(dimension_semantics=sem, vmem_limit_bytes=vmem)


def _sig(x):
    return 1.0 / (1.0 + jnp.exp(-x))


def _gelu(x):
    return 0.5 * x * (1.0 + jnp.tanh(GELU_C * (x + GELU_A * x * x * x)))


def _gelu_grad(x):
    t = jnp.tanh(GELU_C * (x + GELU_A * x * x * x))
    return 0.5 * (1.0 + t) + 0.5 * x * (1.0 - t * t) * GELU_C * (1.0 + 3.0 * GELU_A * x * x)


def _dot(a, b, dims):
    a = a if a.dtype == BF else a.astype(BF)
    b = b if b.dtype == BF else b.astype(BF)
    return lax.dot_general(a, b, (dims, ((), ())), preferred_element_type=F32)


NN = ((1,), (0,))
NT = ((1,), (1,))
TN = ((0,), (0,))


def _row_block(rows, cols, mult=16):
    cap = max(mult, ELEMS_PER_BLOCK // cols)
    best = None
    for b in range(mult, min(rows, cap) + 1, mult):
        if rows % b == 0:
            best = b
    return rows if best is None else best


class _Comm:
    def __init__(self, ins, outs, sems, start, finish, members=()):
        self.ins, self.outs, self.sems, self.start, self.finish = ins, outs, sems, start, finish
        self.members = members
        self.results = None

    def set_results(self, res):
        self.results = list(res)
        off = 0
        for m in self.members:
            m.set_results(res[off:off + len(m.outs)])
            off += len(m.outs)


def _merge_comms(comms):
    comms = [c for c in comms if c is not None]
    if len(comms) < 2:
        return comms[0] if comms else None

    def each(fn_name, ins, outs, sems):
        i = o = s = 0
        for c in comms:
            ni, no, ns = len(c.ins), len(c.outs), len(c.sems)
            getattr(c, fn_name)(ins[i:i + ni], outs[o:o + no], sems[s:s + ns])
            i, o, s = i + ni, o + no, s + ns

    return _Comm([a for c in comms for a in c.ins], [a for c in comms for a in c.outs],
                 [a for c in comms for a in c.sems],
                 lambda ins, outs, sems: each("start", ins, outs, sems),
                 lambda ins, outs, sems: each("finish", ins, outs, sems), members=tuple(comms))


def _call(body, name, grid, in_specs, out_specs, out_shape, args, scratch=(), comm=None):
    in_specs, out_specs, out_shape, scratch = list(in_specs), list(out_specs), list(out_shape), list(scratch)
    if comm is None:
        return pl.pallas_call(body, grid=grid, in_specs=in_specs, out_specs=out_specs, out_shape=out_shape,
                              scratch_shapes=scratch, name=name, compiler_params=_cparams(len(grid)))(*args)
    n_in, n_out, n_sc = len(in_specs), len(out_specs), len(scratch)
    ci, co = len(comm.ins), len(comm.outs)

    def hosted(*refs):
        ins, refs = refs[:n_in], refs[n_in:]
        cins, refs = refs[:ci], refs[ci:]
        outs, refs = refs[:n_out], refs[n_out:]
        couts, refs = refs[:co], refs[co:]
        sc, csems = refs[:n_sc], refs[n_sc:]
        first = last = None
        for d, n in enumerate(grid):
            p = pl.program_id(d)
            first = (p == 0) if first is None else first & (p == 0)
            last = (p == n - 1) if last is None else last & (p == n - 1)

        @pl.when(first)
        def _():
            comm.start(cins, couts, csems)

        body(*ins, *outs, *sc)

        @pl.when(last)
        def _():
            comm.finish(cins, couts, csems)

    any_spec = pl.BlockSpec(memory_space=pl.ANY)
    res = pl.pallas_call(
        hosted, grid=grid, in_specs=in_specs + [any_spec] * ci, out_specs=out_specs + [any_spec] * co,
        out_shape=out_shape + list(comm.outs), scratch_shapes=scratch + list(comm.sems), name=name,
        compiler_params=_cparams(len(grid)))(*args, *comm.ins)
    comm.set_results(res[n_out:])
    return list(res[:n_out])


def _mm(name, grid, operands, pairs, n_acc, acc_shape, extras, outs, epilogue, comm=None, nrow=1, ncol=1):
    nk = grid[2]
    n_op, n_ex, n_out = len(operands), len(extras), len(outs)

    def part_of(ref, dim, t, n):
        if n == 1:
            return ref
        size = ref.shape[dim] // n
        idx = [slice(None)] * len(ref.shape)
        idx[dim] = pl.ds(t * size, size)
        return ref.at[tuple(idx)]

    def tile_of(ref, r, c):
        return part_of(part_of(ref, 0, r, nrow), 1, c, ncol)

    def products(op, r, c):
        parts = [None] * n_acc
        for ai, bi, dims, ci in pairs:
            a = part_of(op[ai], 1 - dims[0][0], r, nrow)
            b = part_of(op[bi], 1 - dims[1][0], c, ncol)
            d = _dot(a[...], b[...], dims)
            parts[ci] = d if parts[ci] is None else parts[ci] + d
        return parts

    def body(*refs):
        op = refs[:n_op]
        ex = refs[n_op:n_op + n_ex]
        out = refs[n_op + n_ex:n_op + n_ex + n_out]
        acc = refs[n_op + n_ex + n_out:]
        tiles = [(r, c) for r in range(nrow) for c in range(ncol)]

        def views(refs_, t):
            return [tile_of(q, *t) for q in refs_]

        if nk == 1:
            parts = products(op, *tiles[0])
            for q, t in enumerate(tiles):
                nxt = products(op, *tiles[q + 1]) if q + 1 < len(tiles) else None
                epilogue(parts, views(ex, t), views(out, t))
                parts = nxt
            return
        k = pl.program_id(2)

        @pl.when(k == 0)
        def _():
            for q in acc:
                q[...] = jnp.zeros_like(q)

        for t in tiles:
            parts = products(op, *t)
            for q, p in zip(views(acc, t), parts):
                q[...] += p

        @pl.when(k == nk - 1)
        def _():
            for t in tiles:
                epilogue([q[...] for q in views(acc, t)], views(ex, t), views(out, t))

    return _call(body, name, grid, [s for _, s in operands] + [s for _, s in extras], [s for _, s in outs],
                 [sh for sh, _ in outs], [a for a, _ in operands] + [a for a, _ in extras],
                 scratch=[pltpu.VMEM(acc_shape, F32) for _ in range(n_acc if nk > 1 else 0)], comm=comm)


def _to_seg(a, seg_len):
    T, D = a.shape
    return a.reshape(SUBLANES, seg_len, D).transpose(1, 0, 2).reshape(T, D)


def _to_tok(a, seg_len):
    T, D = a.shape
    return a.reshape(seg_len, SUBLANES, D).transpose(1, 0, 2).reshape(T, D)


def _rms_fwd(h, gain, out_dtype):
    T, D = h.shape
    bm = min(ROW_TILE, T)

    def body(h_ref, g_ref, o_ref):
        x = h_ref[...]
        r = lax.rsqrt(jnp.mean(x * x, axis=-1, keepdims=True) + RMS_EPS)
        o_ref[...] = (x * r * g_ref[...]).astype(out_dtype)

    row = pl.BlockSpec((bm, D), lambda i: (i, 0))
    return pl.pallas_call(
        body, grid=(T // bm,), in_specs=[row, pl.BlockSpec((1, D), lambda i: (0, 0))],
        out_specs=row, out_shape=jax.ShapeDtypeStruct((T, D), out_dtype), name="rms_fwd",
        compiler_params=_cparams(1))(h, gain)


def _rms_bwd_rows(dn, x, g):
    r = lax.rsqrt(jnp.mean(x * x, axis=-1, keepdims=True) + RMS_EPS)
    xh = x * r
    dng = dn * g
    dx = r * (dng - xh * jnp.mean(dng * xh, axis=-1, keepdims=True))
    return dx, jnp.sum(dn * xh, axis=0, keepdims=True)


def _rms_bwd(dn, h, gain, dh_up):
    T, D = h.shape
    bm = min(ROW_TILE, T)

    def body(dn_ref, h_ref, g_ref, up_ref, dh_ref, dhb_ref, dg_ref):
        dx, dg = _rms_bwd_rows(dn_ref[...], h_ref[...], g_ref[...])
        dh = up_ref[...] + dx
        dh_ref[...] = dh
        dhb_ref[...] = dh.astype(BF)

        @pl.when(pl.program_id(0) == 0)
        def _():
            dg_ref[...] = jnp.zeros_like(dg_ref)

        dg_ref[...] += dg

    row = pl.BlockSpec((bm, D), lambda i: (i, 0))
    vec = pl.BlockSpec((1, D), lambda i: (0, 0))
    return pl.pallas_call(
        body, grid=(T // bm,), in_specs=[row, row, vec, row], out_specs=[row, row, vec],
        out_shape=[jax.ShapeDtypeStruct((T, D), F32), jax.ShapeDtypeStruct((T, D), BF),
                   jax.ShapeDtypeStruct((1, D), F32)],
        name="rms_bwd", compiler_params=_cparams(1))(dn, h, gain, dh_up)


def _loss_head(h, gain, target):
    T, D = h.shape
    bm = min(ROW_TILE, T)

    def body(h_ref, g_ref, t_ref, dh_ref, dhb_ref, dg_ref, loss_ref):
        x = h_ref[...]
        g = g_ref[...]
        r = lax.rsqrt(jnp.mean(x * x, axis=-1, keepdims=True) + RMS_EPS)
        err = x * r * g - t_ref[...]
        part = 0.5 * jnp.sum(jnp.sum(err * err, axis=-1, keepdims=True), axis=0, keepdims=True) / D
        dx, dg = _rms_bwd_rows(err / D, x, g)
        dh_ref[...] = dx
        dhb_ref[...] = dx.astype(BF)

        @pl.when(pl.program_id(0) == 0)
        def _():
            dg_ref[...] = jnp.zeros_like(dg_ref)
            loss_ref[...] = jnp.zeros_like(loss_ref)

        dg_ref[...] += dg
        loss_ref[...] += jnp.broadcast_to(part, loss_ref.shape)

    row = pl.BlockSpec((bm, D), lambda i: (i, 0))
    vec = pl.BlockSpec((1, D), lambda i: (0, 0))
    return pl.pallas_call(
        body, grid=(T // bm,), in_specs=[row, vec, row],
        out_specs=[row, row, vec, pl.BlockSpec((SUBLANES, LANES), lambda i: (0, 0))],
        out_shape=[jax.ShapeDtypeStruct((T, D), F32), jax.ShapeDtypeStruct((T, D), BF),
                   jax.ShapeDtypeStruct((1, D), F32), jax.ShapeDtypeStruct((SUBLANES, LANES), F32)],
        name="loss_head", compiler_params=_cparams(1))(h, gain, target)


def _ffn_up(n, w1g, w3g, comm=None):
    T, D = n.shape
    fs = w1g.shape[-1]
    bm = min(ROW_TILE, T)
    wspec = pl.BlockSpec((None, D, fs), lambda s, i, k: (s, 0, 0))
    ospec = pl.BlockSpec((None, bm, fs), lambda s, i, k: (s, i, 0))

    def epi(accs, ex, outs):
        a1, a3 = accs
        outs[0][...] = a1.astype(BF)
        outs[1][...] = a3.astype(BF)
        outs[2][...] = (a1 * _sig(a1) * a3).astype(BF)

    sh = jax.ShapeDtypeStruct((N_DEV, T, fs), BF)
    return _mm("ffn_up", (N_DEV, T // bm, 1),
               [(n, pl.BlockSpec((bm, D), lambda s, i, k: (i, 0))), (w1g, wspec), (w3g, wspec)],
               [(0, 1, NN, 0), (0, 2, NN, 1)], 2, None, [], [(sh, ospec)] * 3, epi, comm=comm, nrow=2)


def _ffn_down(g, w2g, h, comm=None):
    _, T, fs = g.shape
    D = h.shape[1]
    bm = min(ROW_TILE, T)
    row = pl.BlockSpec((bm, D), lambda i, j, s: (i, 0))

    def epi(accs, ex, outs):
        outs[0][...] = ex[0][...] + 0.5 * accs[0]

    return _mm("ffn_down", (T // bm, 1, N_DEV),
               [(g, pl.BlockSpec((None, bm, fs), lambda i, j, s: (s, i, 0))),
                (w2g, pl.BlockSpec((None, fs, D), lambda i, j, s: (s, 0, 0)))],
               [(0, 1, NN, 0)], 1, (bm, D), [(h, row)],
               [(jax.ShapeDtypeStruct((T, D), F32), row)], epi, comm=comm, ncol=max(1, D // COL_TILE))[0]


def _ffn_bwd_hidden(dhb, w2g, a1, a3, comm=None):
    T, D = dhb.shape
    fs = a1.shape[-1]
    bm = min(ROW_TILE, T)
    aspec = pl.BlockSpec((None, bm, fs), lambda s, i, k: (s, i, 0))

    def epi(accs, ex, outs):
        dg = 0.5 * accs[0]
        a1v = ex[0][...].astype(F32)
        a3v = ex[1][...].astype(F32)
        sg = _sig(a1v)
        outs[0][...] = (dg * a3v * sg * (1.0 + a1v * (1.0 - sg))).astype(BF)
        outs[1][...] = (dg * a1v * sg).astype(BF)

    sh = jax.ShapeDtypeStruct((N_DEV, T, fs), BF)
    return _mm("ffn_bwd_hidden", (N_DEV, T // bm, 1),
               [(dhb, pl.BlockSpec((bm, D), lambda s, i, k: (i, 0))),
                (w2g, pl.BlockSpec((None, fs, D), lambda s, i, k: (s, 0, 0)))],
               [(0, 1, NT, 0)], 1, None, [(a1, aspec), (a3, aspec)], [(sh, aspec)] * 2, epi, comm=comm, nrow=2)


def _ffn_dw2(g, dhb):
    _, T, fs = g.shape
    D = dhb.shape[1]
    bn = min(COL_TILE, D)

    def epi(accs, ex, outs):
        outs[0][...] = (0.5 * accs[0]).astype(BF)

    return _mm("ffn_dw2", (N_DEV, D // bn, 1),
               [(g, pl.BlockSpec((None, T, fs), lambda s, j, k: (s, 0, 0))),
                (dhb, pl.BlockSpec((T, bn), lambda s, j, k: (0, j)))],
               [(0, 1, TN, 0)], 1, None, [],
               [(jax.ShapeDtypeStruct((N_DEV, fs, D), BF), pl.BlockSpec((None, fs, bn), lambda s, j, k: (s, 0, j)))],
               epi)[0]


def _ffn_dw13(n, da1, da3, comm=None):
    T, D = n.shape
    fs = da1.shape[-1]
    bmr = min(COL_TILE, D)
    dspec = pl.BlockSpec((None, T, fs), lambda s, r, k: (s, 0, 0))
    ospec = pl.BlockSpec((None, bmr, fs), lambda s, r, k: (s, r, 0))

    def epi(accs, ex, outs):
        outs[0][...] = accs[0].astype(BF)
        outs[1][...] = accs[1].astype(BF)

    sh = jax.ShapeDtypeStruct((N_DEV, D, fs), BF)
    return _mm("ffn_dw13", (N_DEV, D // bmr, 1),
               [(n, pl.BlockSpec((T, bmr), lambda s, r, k: (0, r))), (da1, dspec), (da3, dspec)],
               [(0, 1, TN, 0), (0, 2, TN, 1)], 2, None, [], [(sh, ospec)] * 2, epi, comm=comm)


def _ffn_dn(da1, da3, w1g, w3g, comm=None):
    _, T, fs = da1.shape
    D = w1g.shape[-2]
    bm = min(ROW_TILE, T)
    dspec = pl.BlockSpec((None, bm, fs), lambda i, j, s: (s, i, 0))
    wspec = pl.BlockSpec((None, D, fs), lambda i, j, s: (s, 0, 0))
    row = pl.BlockSpec((bm, D), lambda i, j, s: (i, 0))

    def epi(accs, ex, outs):
        outs[0][...] = accs[0]

    return _mm("ffn_dn", (T // bm, 1, N_DEV),
               [(da1, dspec), (w1g, wspec), (da3, dspec), (w3g, wspec)],
               [(0, 1, NT, 0), (2, 3, NT, 0)], 1, (bm, D), [],
               [(jax.ShapeDtypeStruct((T, D), F32), row)], epi, comm=comm, ncol=max(1, D // COL_TILE))[0]


def _rope_tables(seq):
    half = LANES // 2
    inv = ROPE_THETA ** (-jnp.arange(0, half, dtype=F32) * 2.0 / LANES)
    ang = jnp.arange(seq, dtype=F32)[:, None] * inv[None, :]
    cos, sin = jnp.cos(ang), jnp.sin(ang)
    return jnp.concatenate([cos, cos], axis=1), jnp.concatenate([-sin, sin], axis=1)


def _branch_bias(nq, bq):
    d = (jnp.arange(nq)[:, None, None] * bq + jnp.arange(bq)[None, :, None]
         - jnp.arange(bq)[None, None, :])
    mult = jnp.zeros(d.shape, F32)
    for window, dil in DILATED_PATTERN:
        mult = mult + ((d >= 0) & (d % dil == 0) & (d <= window)).astype(F32)
    return jnp.where(mult > 0, jnp.log(jnp.maximum(mult, 1.0)), NEG_INF)


def _proj_fwd(u, wing, comm=None):
    T, D = u.shape
    ws = wing.shape[-1]
    bm = min(ROW_TILE, T)

    def epi(accs, ex, outs):
        outs[0][...] = accs[0]

    return _mm("proj_fwd", (N_DEV, T // bm, 1),
               [(u, pl.BlockSpec((bm, D), lambda s, i, k: (i, 0))),
                (wing, pl.BlockSpec((None, D, ws), lambda s, i, k: (s, 0, 0)))],
               [(0, 1, NN, 0)], 1, None, [],
               [(jax.ShapeDtypeStruct((T, N_DEV * ws), F32),
                 pl.BlockSpec((bm, ws), lambda s, i, k: (i, s)))], epi, comm=comm)[0]


def _rope_fwd(proj, cosf, sinf, seq, nh):
    T = proj.shape[0]
    bs = min(ROW_TILE, seq)
    nst = seq // bs
    scale = LANES ** -0.5

    def body(x_ref, c_ref, s_ref, o_ref):
        j = pl.program_id(1)
        t = x_ref[...]
        rot = t * c_ref[...] + pltpu.roll(t, LANES // 2, 1) * s_ref[...]
        rot = rot * jnp.where(j < nh, scale, 1.0)
        o_ref[...] = jnp.where(j < 2 * nh, rot, t).astype(BF)

    blk = pl.BlockSpec((bs, LANES), lambda r, j: (r, j))
    tab = pl.BlockSpec((bs, LANES), lambda r, j: (r % nst, 0))
    return pl.pallas_call(
        body, grid=(T // bs, 3 * nh), in_specs=[blk, tab, tab], out_specs=blk,
        out_shape=jax.ShapeDtypeStruct((T, 3 * nh * LANES), BF), name="rope_fwd",
        compiler_params=_cparams(2))(proj, cosf, sinf)


def _attn_fwd(qkv, bias, nb, seq, nh, comm=None):
    T = nb * seq
    bq = bias.shape[1]
    nq = seq // bq

    def body(q_ref, k_ref, v_ref, b_ref, o_ref, lse_ref):
        qi = pl.program_id(2)
        q = q_ref[...]

        def step(kj, carry):
            m, l, acc = carry
            rows = pl.ds(pl.multiple_of(kj * bq, bq), bq)
            s = _dot(q, k_ref[rows, :], NT) + b_ref[qi - kj]
            m_new = jnp.maximum(m, jnp.max(s, axis=1, keepdims=True))
            p = jnp.exp(s - m_new)
            alpha = jnp.exp(m - m_new)
            l = alpha * l + jnp.sum(p, axis=1, keepdims=True)
            acc = alpha * acc + _dot(p, v_ref[rows, :], NN)
            return m_new, l, acc

        init = (jnp.full((bq, 1), NEG_INF, F32), jnp.zeros((bq, 1), F32), jnp.zeros((bq, LANES), F32))
        m, l, acc = lax.fori_loop(0, qi + 1, step, init)
        o_ref[...] = (acc / l).astype(BF)
        lse_ref[...] = m + jnp.log(l)

    return _call(
        body, "attn_fwd", (nb, nh, nq),
        [pl.BlockSpec((bq, LANES), lambda b, h, i: (b * nq + i, h)),
         pl.BlockSpec((seq, LANES), lambda b, h, i: (b, nh + h)),
         pl.BlockSpec((seq, LANES), lambda b, h, i: (b, 2 * nh + h)),
         pl.BlockSpec((nq, bq, bq), lambda b, h, i: (0, 0, 0))],
        [pl.BlockSpec((bq, LANES), lambda b, h, i: (b * nq + i, h)),
         pl.BlockSpec((None, bq, 1), lambda b, h, i: (h, b * nq + i, 0))],
        [jax.ShapeDtypeStruct((T, 2 * nh * LANES), BF), jax.ShapeDtypeStruct((nh, T, 1), F32)],
        (qkv, qkv, qkv, bias), comm=comm)


def _attn_bwd_dq(qkv, cat, dcat, lse, bias, nb, seq, nh, comm=None):
    T = nb * seq
    bq = bias.shape[1]
    nq = seq // bq

    def body(q_ref, k_ref, v_ref, o_ref, do_ref, lse_ref, b_ref, dq_ref, delta_ref):
        qi = pl.program_id(2)
        q = q_ref[...]
        do = do_ref[...]
        dob = do.astype(BF)
        lse_t = lse_ref[...]
        delta = jnp.sum(do * o_ref[...].astype(F32), axis=1, keepdims=True)
        delta_ref[...] = delta

        def step(kj, dq):
            rows = pl.ds(pl.multiple_of(kj * bq, bq), bq)
            k = k_ref[rows, :]
            p = jnp.exp(_dot(q, k, NT) + b_ref[qi - kj] - lse_t)
            ds = p * (_dot(dob, v_ref[rows, :], NT) - delta)
            return dq + _dot(ds, k, NN)

        dq_ref[...] = lax.fori_loop(0, qi + 1, step, jnp.zeros((bq, LANES), F32))

    tile = pl.BlockSpec((bq, LANES), lambda b, h, i: (b * nq + i, h))
    stat = pl.BlockSpec((None, bq, 1), lambda b, h, i: (h, b * nq + i, 0))
    return _call(
        body, "attn_bwd_dq", (nb, nh, nq),
        [tile, pl.BlockSpec((seq, LANES), lambda b, h, i: (b, nh + h)),
         pl.BlockSpec((seq, LANES), lambda b, h, i: (b, 2 * nh + h)), tile, tile, stat,
         pl.BlockSpec((nq, bq, bq), lambda b, h, i: (0, 0, 0))],
        [tile, stat],
        [jax.ShapeDtypeStruct((T, nh * LANES), F32), jax.ShapeDtypeStruct((nh, T, 1), F32)],
        (qkv, qkv, qkv, cat, dcat, lse, bias), comm=comm)


def _attn_bwd_dkv(qkv, dcat, lse, delta, bias, nb, seq, nh):
    T = nb * seq
    bq = bias.shape[1]
    nq = seq // bq

    def body(k_ref, v_ref, q_ref, do_ref, lse_ref, delta_ref, b_ref, dk_ref, dv_ref):
        kj = pl.program_id(2)
        k = k_ref[...]
        v = v_ref[...]

        def step(qi, carry):
            dk, dv = carry
            rows = pl.ds(pl.multiple_of(qi * bq, bq), bq)
            q = q_ref[rows, :]
            dob = do_ref[rows, :].astype(BF)
            p = jnp.exp(_dot(q, k, NT) + b_ref[qi - kj] - lse_ref[rows, :])
            dv = dv + _dot(p, dob, TN)
            ds = p * (_dot(dob, v, NT) - delta_ref[rows, :])
            return dk + _dot(ds, q, TN), dv

        z = jnp.zeros((bq, LANES), F32)
        dk, dv = lax.fori_loop(kj, nq, step, (z, z))
        dk_ref[...] = dk
        dv_ref[...] = dv

    stat = pl.BlockSpec((None, seq, 1), lambda b, h, i: (h, b, 0))
    out = pl.BlockSpec((bq, LANES), lambda b, h, i: (b * nq + i, h))
    sh = jax.ShapeDtypeStruct((T, nh * LANES), F32)
    return pl.pallas_call(
        body, grid=(nb, nh, nq),
        in_specs=[pl.BlockSpec((bq, LANES), lambda b, h, i: (b * nq + i, nh + h)),
                  pl.BlockSpec((bq, LANES), lambda b, h, i: (b * nq + i, 2 * nh + h)),
                  pl.BlockSpec((seq, LANES), lambda b, h, i: (b, h)),
                  pl.BlockSpec((seq, LANES), lambda b, h, i: (b, h)),
                  stat, stat,
                  pl.BlockSpec((nq, bq, bq), lambda b, h, i: (0, 0, 0))],
        out_specs=[out, out], out_shape=[sh, sh],
        name="attn_bwd_dkv", compiler_params=_cparams(3))(qkv, qkv, qkv, dcat, lse, delta, bias)


def _conv_parts(gc, xin, w_ref):
    w = [w_ref[k:k + 1, :] for k in range(3)]
    u = gc * xin
    row = lax.broadcasted_iota(jnp.int32, u.shape, 0)
    u1 = jnp.where(row >= 1, pltpu.roll(u, 1, 0), 0.0)
    u2 = jnp.where(row >= 2, pltpu.roll(u, 2, 0), 0.0)
    return u, u1, u2, w[0] * u2 + w[1] * u1 + w[2] * u, w, row


def _conv_fwd(proj, conv_w, cat, nb, seq, width):
    cw = min(2 * LANES, width)
    nc = width // cw

    def body(gb_ref, gc_ref, x_ref, w_ref, cat_ref, o_ref):
        _, _, _, conv, _, _ = _conv_parts(gc_ref[...], x_ref[...], w_ref)
        o_ref[...] = (gb_ref[...] * conv).astype(BF)

    def sec(k):
        return pl.BlockSpec((seq, cw), lambda b, c: (b, k * nc + c))

    return pl.pallas_call(
        body, grid=(nb, nc),
        in_specs=[sec(3), sec(4), sec(5), pl.BlockSpec((3, cw), lambda b, c: (0, c)),
                  pl.BlockSpec(memory_space=pl.ANY)],
        out_specs=pl.BlockSpec((seq, cw), lambda b, c: (b, nc + c)),
        out_shape=jax.ShapeDtypeStruct(cat.shape, BF), input_output_aliases={4: 0},
        name="conv_fwd", compiler_params=_cparams(2))(proj, proj, proj, conv_w, cat)


def _conv_bwd(proj, conv_w, dcat, nb, seq, width):
    cw = min(2 * LANES, width)
    nc = width // cw
    T = nb * seq

    def body(gb_ref, gc_ref, x_ref, w_ref, d_ref, dgb_ref, dgc_ref, dx_ref, dw_ref):
        gc = gc_ref[...]
        xin = x_ref[...]
        u, u1, u2, conv, w, row = _conv_parts(gc, xin, w_ref)
        dsc = d_ref[...]
        dgb_ref[...] = dsc * conv
        dconv = dsc * gb_ref[...]
        d1 = jnp.where(row < seq - 1, pltpu.roll(dconv, seq - 1, 0), 0.0)
        d2 = jnp.where(row < seq - 2, pltpu.roll(dconv, seq - 2, 0), 0.0)
        du = w[2] * dconv + w[1] * d1 + w[0] * d2
        dgc_ref[...] = du * xin
        dx_ref[...] = du * gc

        @pl.when(pl.program_id(1) == 0)
        def _():
            dw_ref[...] = jnp.zeros_like(dw_ref)

        dw_ref[0:1, :] += jnp.sum(dconv * u2, axis=0, keepdims=True)
        dw_ref[1:2, :] += jnp.sum(dconv * u1, axis=0, keepdims=True)
        dw_ref[2:3, :] += jnp.sum(dconv * u, axis=0, keepdims=True)

    def sec(k):
        return pl.BlockSpec((seq, cw), lambda c, b: (b, k * nc + c))

    out = pl.BlockSpec((seq, cw), lambda c, b: (b, c))
    wsp = pl.BlockSpec((3, cw), lambda c, b: (0, c))
    sh = jax.ShapeDtypeStruct((T, width), F32)
    return pl.pallas_call(
        body, grid=(nc, nb), in_specs=[sec(3), sec(4), sec(5), wsp, sec(1)],
        out_specs=[out, out, out, wsp], out_shape=[sh, sh, sh, jax.ShapeDtypeStruct((3, width), F32)],
        name="conv_bwd", compiler_params=_cparams(2))(proj, proj, proj, conv_w, dcat)


def _assemble_dproj(dq, dk, dv, dgb, dgc, dxin, cosf, sinf, seq):
    T, width = dq.shape
    nh = width // LANES
    bs = min(256, seq)
    nst = seq // bs
    scale = LANES ** -0.5

    def body(dq_ref, dk_ref, dv_ref, dgb_ref, dgc_ref, dx_ref, c_ref, s_ref, o_ref):
        sec = pl.program_id(1)
        c = c_ref[...]
        s = s_ref[...]

        def unrope(ref, mul):
            for h in range(nh):
                cols = slice(h * LANES, (h + 1) * LANES)
                t = ref[:, cols]
                o_ref[:, cols] = ((t * c + pltpu.roll(t * s, LANES // 2, 1)) * mul).astype(BF)

        @pl.when(sec == 0)
        def _():
            unrope(dq_ref, scale)

        @pl.when(sec == 1)
        def _():
            unrope(dk_ref, 1.0)

        for k, ref in ((2, dv_ref), (3, dgb_ref), (4, dgc_ref), (5, dx_ref)):
            @pl.when(sec == k)
            def _(ref=ref):
                o_ref[...] = ref[...].astype(BF)

    blk = pl.BlockSpec((bs, width), lambda r, k: (r, 0))
    tab = pl.BlockSpec((bs, LANES), lambda r, k: (r % nst, 0))
    return pl.pallas_call(
        body, grid=(T // bs, 6), in_specs=[blk] * 6 + [tab, tab],
        out_specs=pl.BlockSpec((bs, width), lambda r, k: (r, k)),
        out_shape=jax.ShapeDtypeStruct((T, 6 * width), BF), name="assemble_dproj",
        compiler_params=_cparams(2))(dq, dk, dv, dgb, dgc, dxin, cosf, sinf)


def _res_mm(name, a, w, h):
    T, K = a.shape
    N = w.shape[1]
    bm = min(ROW_TILE, T)
    bk = min(ROW_TILE, K)
    row = pl.BlockSpec((bm, N), lambda i, j, k: (i, 0))

    def epi(accs, ex, outs):
        outs[0][...] = ex[0][...] + accs[0]

    return _mm(name, (T // bm, 1, K // bk),
               [(a, pl.BlockSpec((bm, bk), lambda i, j, k: (i, k))),
                (w, pl.BlockSpec((bk, N), lambda i, j, k: (k, 0)))],
               [(0, 1, NN, 0)], 1, (bm, N), [(h, row)],
               [(jax.ShapeDtypeStruct((T, N), F32), row)], epi, ncol=max(1, N // COL_TILE))[0]


def _mm_nt(name, a, w, out_dtype):
    T, K = a.shape
    N = w.shape[0]
    bm = min(ROW_TILE, T)
    bn = min(ROW_TILE, N)

    def epi(accs, ex, outs):
        outs[0][...] = accs[0].astype(out_dtype)

    return _mm(name, (T // bm, N // bn, 1),
               [(a, pl.BlockSpec((bm, K), lambda i, j, k: (i, 0))),
                (w, pl.BlockSpec((bn, K), lambda i, j, k: (j, 0)))],
               [(0, 1, NT, 0)], 1, None, [],
               [(jax.ShapeDtypeStruct((T, N), out_dtype), pl.BlockSpec((bm, bn), lambda i, j, k: (i, j)))],
               epi)[0]


def _mm_tn(name, a, bs_list):
    T, M = a.shape
    N = bs_list[0].shape[1]
    bk = min(ROW_TILE, T)
    bmr = min(ROW_TILE, M)
    n = len(bs_list)

    def epi(accs, ex, outs):
        for q in range(n):
            outs[q][...] = accs[q].astype(BF)

    ops = [(a, pl.BlockSpec((bk, bmr), lambda r, j, t: (t, r)))]
    ops += [(b, pl.BlockSpec((bk, N), lambda r, j, t: (t, 0))) for b in bs_list]
    return _mm(name, (M // bmr, 1, T // bk), ops, [(0, 1 + q, TN, q) for q in range(n)], n, (bmr, N), [],
               [(jax.ShapeDtypeStruct((M, N), BF), pl.BlockSpec((bmr, N), lambda r, j, t: (r, 0)))] * n, epi)


def _proj_bwd_x(dproj, wing):
    T = dproj.shape[0]
    _, D, ws = wing.shape
    bm = min(ROW_TILE, T)
    row = pl.BlockSpec((bm, D), lambda i, j, s: (i, 0))

    def epi(accs, ex, outs):
        outs[0][...] = accs[0]

    return _mm("proj_bwd_x", (T // bm, 1, N_DEV),
               [(dproj, pl.BlockSpec((bm, ws), lambda i, j, s: (i, s))),
                (wing, pl.BlockSpec((None, D, ws), lambda i, j, s: (s, 0, 0)))],
               [(0, 1, NT, 0)], 1, (bm, D), [], [(jax.ShapeDtypeStruct((T, D), F32), row)], epi,
               ncol=max(1, D // COL_TILE))[0]


def _proj_dw(u, dproj, ws):
    T, D = u.shape
    bk = min(ROW_TILE, T)

    def epi(accs, ex, outs):
        outs[0][...] = accs[0].astype(BF)

    return _mm("proj_dw", (N_DEV, 1, T // bk),
               [(u, pl.BlockSpec((bk, D), lambda s, j, t: (t, 0))),
                (dproj, pl.BlockSpec((bk, ws), lambda s, j, t: (t, s)))],
               [(0, 1, TN, 0)], 1, (D, ws), [],
               [(jax.ShapeDtypeStruct((N_DEV, D, ws), BF),
                 pl.BlockSpec((None, D, ws), lambda s, j, t: (s, 0, 0)))], epi)[0]


def _mixer_ab_fwd(h, gain, wing, conv_w, wout, tabs, nb, seq, comm_proj=None, comm_attn=None):
    cosf, sinf, bias = tabs
    width = wing.shape[-1] * N_DEV // 6
    nh = width // LANES
    u = _rms_fwd(h, gain, BF)
    proj = _proj_fwd(u, wing, comm=comm_proj)
    qkv = _rope_fwd(proj, cosf, sinf, seq, nh)
    cat, lse = _attn_fwd(qkv, bias, nb, seq, nh, comm=comm_attn)
    cat = _conv_fwd(proj, conv_w, cat, nb, seq, width)
    return _res_mm("outproj_fwd", cat, wout, h), (h, u, proj, qkv, cat, lse)


def _mixer_ab_bwd(dh, dhb, saved, gain, wing, conv_w, wout, tabs, nb, seq, reduce_start, carry):
    cosf, sinf, bias = tabs
    h, u, proj, qkv, cat, lse = saved
    D = h.shape[1]
    ws = wing.shape[-1]
    width = ws * N_DEV // 6
    nh = width // LANES
    dcat = _mm_nt("outproj_bwd_x", dhb, wout, F32)
    dwout = _mm_tn("outproj_dw", cat, [dhb])[0]
    comm = _merge_comms(reduce_start(["ab_w_out"], [dwout.reshape(N_DEV, -1, D)]) + [carry])
    dq, delta = _attn_bwd_dq(qkv, cat, dcat, lse, bias, nb, seq, nh, comm=comm)
    dk, dv = _attn_bwd_dkv(qkv, dcat, lse, delta, bias, nb, seq, nh)
    dgb, dgc, dxin, dconvw = _conv_bwd(proj, conv_w, dcat, nb, seq, width)
    dproj = _assemble_dproj(dq, dk, dv, dgb, dgc, dxin, cosf, sinf, seq)
    du = _proj_bwd_x(dproj, wing)
    comm, = reduce_start(["ab_w_in"], [_proj_dw(u, dproj, ws)])
    dh_in, dhb_in, dgain = _rms_bwd(du, h, gain, dh)
    return dh_in, dhb_in, dgain, dconvw, comm


def _s5_zoh(lr, li, log_dt):
    dt = jnp.exp(log_dt)
    mag = jnp.exp(lr * dt)
    ar = mag * jnp.cos(li * dt)
    ai = mag * jnp.sin(li * dt)
    den = lr * lr + li * li
    return dt, ar, ai, den, ((ar - 1.0) * lr + ai * li) / den, (ai * lr - (ar - 1.0) * li) / den


def _s5_discretize(lam_re, lam_im, log_dt, bt_re, bt_im):
    def body(lr_ref, li_ref, ld_ref, br_ref, bi_ref, ar_ref, ai_ref, bbr_ref, bbi_ref):
        _, ar, ai, _, fr, fi = _s5_zoh(lr_ref[...], li_ref[...], ld_ref[...])
        ar_ref[...] = ar
        ai_ref[...] = ai
        bbr_ref[...] = fr * br_ref[...] - fi * bi_ref[...]
        bbi_ref[...] = fr * bi_ref[...] + fi * br_ref[...]

    small = jax.ShapeDtypeStruct(lam_re.shape, F32)
    big = jax.ShapeDtypeStruct(bt_re.shape, F32)
    return pl.pallas_call(body, out_shape=[small, small, big, big], name="s5_discretize",
                          compiler_params=_cparams(0))(lam_re, lam_im, log_dt, bt_re, bt_im)


def _s5_discretize_bwd(lam_re, lam_im, log_dt, bt_re, bt_im, d_ar, d_ai, d_bbr, d_bbi):

    def body(lr_ref, li_ref, ld_ref, br_ref, bi_ref, dar_ref, dai_ref, dbbr_ref, dbbi_ref,
             dlr_ref, dli_ref, dld_ref, dbr_ref, dbi_ref):
        lr, li = lr_ref[...], li_ref[...]
        dt, ar, ai, den, fr, fi = _s5_zoh(lr, li, ld_ref[...])
        br, bi = br_ref[...], bi_ref[...]
        dbbr, dbbi = dbbr_ref[...], dbbi_ref[...]
        dbr_ref[...] = dbbr * fr + dbbi * fi
        dbi_ref[...] = dbbi * fr - dbbr * fi
        dfr = jnp.sum(dbbr * br + dbbi * bi, axis=1, keepdims=True)
        dfi = jnp.sum(dbbi * br - dbbr * bi, axis=1, keepdims=True)
        dnr = dfr / den
        dni = dfi / den
        dden = -(dfr * fr + dfi * fi) / den
        dar = dar_ref[...] + dnr * lr - dni * li
        dai = dai_ref[...] + dnr * li + dni * lr
        dlr_ref[...] = dnr * (ar - 1.0) + dni * ai + 2.0 * dden * lr + dt * (dar * ar + dai * ai)
        dli_ref[...] = dnr * ai - dni * (ar - 1.0) + 2.0 * dden * li + dt * (dai * ar - dar * ai)
        ddt = jnp.sum(dar * (lr * ar - li * ai) + dai * (lr * ai + li * ar), axis=2, keepdims=True)
        dld_ref[...] = ddt * dt

    small = jax.ShapeDtypeStruct(lam_re.shape, F32)
    big = jax.ShapeDtypeStruct(bt_re.shape, F32)
    return pl.pallas_call(
        body, out_shape=[small, small, jax.ShapeDtypeStruct(log_dt.shape, F32), big, big],
        name="s5_discretize_bwd", compiler_params=_cparams(0))(
            lam_re, lam_im, log_dt, bt_re, bt_im, d_ar, d_ai, d_bbr, d_bbi)


def _rows8(t):
    return pl.ds(pl.multiple_of(t * SUBLANES, SUBLANES), SUBLANES)


def _cmul_add(ar, ai, sr, si, br, bi):
    return ar * sr - ai * si + br, ar * si + ai * sr + bi


def _cpow(ar, ai, n):
    rr = ri = None
    while n:
        if n & 1:
            rr, ri = (ar, ai) if rr is None else (rr * ar - ri * ai, rr * ai + ri * ar)
        ar, ai = ar * ar - ai * ai, 2.0 * ar * ai
        n >>= 1
    return rr, ri


def _s5_specs(R, nj):
    sh = STATE_COLS
    return dict(
        rows=pl.BlockSpec((R, LANES), lambda j: (0, j)),
        bd=pl.BlockSpec((None, LANES, sh), lambda j: (j, 0, 0)),
        cd=pl.BlockSpec((None, sh, LANES), lambda j: (j, 0, 0)),
        a=pl.BlockSpec((None, 1, sh), lambda j: (j, 0, 0)),
        vec=pl.BlockSpec((1, LANES), lambda j: (0, j)),
        init=pl.BlockSpec((None, SUBLANES, sh), lambda j: (j, 0, 0)))


def _s5_fwd(u, mats, seg_len, nseg, comm=None):
    bdr, bdi, cdr, cdi, are, aim, dsk = mats
    R, D = u.shape
    nj = D // LANES
    sh = STATE_COLS
    rc = min(R, 512)
    sp = _s5_specs(R, nj)

    def body(u_ref, bdr_ref, bdi_ref, cdr_ref, cdi_ref, ar_ref, ai_ref, d_ref,
             y_ref, yg_ref, ir_ref, ii_ref, sre, sim):
        ar = jnp.broadcast_to(ar_ref[...], (SUBLANES, sh))
        ai = jnp.broadcast_to(ai_ref[...], (SUBLANES, sh))

        def bu_chunk(c, _):
            rows = pl.ds(pl.multiple_of(c * rc, rc), rc)
            ub = u_ref[rows, :].astype(BF)
            sre[rows, :] = _dot(ub, bdr_ref[...], NN)
            sim[rows, :] = _dot(ub, bdi_ref[...], NN)
            return 0

        lax.fori_loop(0, R // rc, bu_chunk, 0)
        z = jnp.zeros((SUBLANES, sh), F32)

        def local_scan(t, c):
            return _cmul_add(ar, ai, c[0], c[1], sre[_rows8(t), :], sim[_rows8(t), :])

        er, ei = lax.fori_loop(0, seg_len, local_scan, (z, z), unroll=SCAN_UNROLL)
        pr, pi = _cpow(ar, ai, seg_len)
        first = (lax.broadcasted_iota(jnp.int32, (SUBLANES, sh), 0) & (nseg - 1)) == 0

        def prev(x):
            return jnp.where(first, 0.0, pltpu.roll(x, 1, 0))

        xr, xi = er, ei
        for _ in range(nseg - 1):
            xr, xi = _cmul_add(pr, pi, prev(xr), prev(xi), er, ei)
        i_r, i_i = prev(xr), prev(xi)
        ir_ref[...] = i_r
        ii_ref[...] = i_i

        def scan(t, c):
            nr, ni = _cmul_add(ar, ai, c[0], c[1], sre[_rows8(t), :], sim[_rows8(t), :])
            sre[_rows8(t), :] = nr
            sim[_rows8(t), :] = ni
            return nr, ni

        lax.fori_loop(0, seg_len, scan, (i_r, i_i), unroll=SCAN_UNROLL)

        def y_chunk(c, _):
            rows = pl.ds(pl.multiple_of(c * rc, rc), rc)
            y = _dot(sre[rows, :], cdr_ref[...], NN) + _dot(sim[rows, :], cdi_ref[...], NN)
            y = y + d_ref[...] * u_ref[rows, :]
            y_ref[rows, :] = y
            yg_ref[rows, :] = _gelu(y).astype(BF)
            return 0

        lax.fori_loop(0, R // rc, y_chunk, 0)

    init_sh = jax.ShapeDtypeStruct((nj, SUBLANES, STATE_COLS), F32)
    return _call(
        body, "s5_fwd", (nj,),
        [sp["rows"], sp["bd"], sp["bd"], sp["cd"], sp["cd"], sp["a"], sp["a"], sp["vec"]],
        [sp["rows"], sp["rows"], sp["init"], sp["init"]],
        [jax.ShapeDtypeStruct((R, D), F32), jax.ShapeDtypeStruct((R, D), BF), init_sh, init_sh],
        (u, bdr, bdi, cdr, cdi, are, aim, dsk),
        scratch=[pltpu.VMEM((R, sh), F32) for _ in range(2)], comm=comm)


def _s5_bwd(u, dy, mats, init_re, init_im, seg_len, nseg, comm=None):
    bdr, bdi, cdr, cdi, are, aim, dsk = mats
    R, D = u.shape
    nj = D // LANES
    sh = STATE_COLS
    rc = min(R, 512)
    sp = _s5_specs(R, nj)

    def body(u_ref, dy_ref, bdr_ref, bdi_ref, cdr_ref, cdi_ref, ar_ref, ai_ref, d_ref, ir_ref, ii_ref,
             du_ref, dbdr_ref, dbdi_ref, dcdr_ref, dcdi_ref, dar_ref, dai_ref, dd_ref,
             sre, sim, gre, gim):
        ar = jnp.broadcast_to(ar_ref[...], (SUBLANES, sh))
        ai = jnp.broadcast_to(ai_ref[...], (SUBLANES, sh))
        i_r, i_i = ir_ref[...], ii_ref[...]

        def chunk(c):
            return pl.ds(pl.multiple_of(c * rc, rc), rc)

        def bu_chunk(c, _):
            ub = u_ref[chunk(c), :].astype(BF)
            sre[chunk(c), :] = _dot(ub, bdr_ref[...], NN)
            sim[chunk(c), :] = _dot(ub, bdi_ref[...], NN)
            return 0

        lax.fori_loop(0, R // rc, bu_chunk, 0)

        def scan(t, c):
            nr, ni = _cmul_add(ar, ai, c[0], c[1], sre[_rows8(t), :], sim[_rows8(t), :])
            sre[_rows8(t), :] = nr
            sim[_rows8(t), :] = ni
            return nr, ni

        lax.fori_loop(0, seg_len, scan, (i_r, i_i), unroll=SCAN_UNROLL)

        def c_chunk(c, carry):
            dyb = dy_ref[chunk(c), :].astype(BF)
            gre[chunk(c), :] = _dot(dyb, cdr_ref[...], NT)
            gim[chunk(c), :] = _dot(dyb, cdi_ref[...], NT)
            return (carry[0] + _dot(sre[chunk(c), :], dyb, TN), carry[1] + _dot(sim[chunk(c), :], dyb, TN))

        zc = jnp.zeros((sh, LANES), F32)
        dcr, dci = lax.fori_loop(0, R // rc, c_chunk, (zc, zc))
        dcdr_ref[...] = dcr
        dcdi_ref[...] = dci

        def adj(t, gr_next, gi_next):
            return _cmul_add(ar, -ai, gr_next, gi_next, gre[_rows8(t), :], gim[_rows8(t), :])

        z = jnp.zeros((SUBLANES, sh), F32)
        fr, fi = lax.fori_loop(0, seg_len, lambda i, c: adj(seg_len - 1 - i, c[0], c[1]), (z, z),
                               unroll=SCAN_UNROLL)
        pr, pi = _cpow(ar, ai, seg_len)
        last =(lax.broadcasted_iota(jnp.int32, (SUBLANES, sh), 0) & (nseg - 1)) == nseg - 1

        def nxt(x):
            return jnp.where(last, 0.0, pltpu.roll(x, SUBLANES - 1, 0))

        xr, xi = fr, fi
        for _ in range(nseg - 1):
            xr, xi = _cmul_add(pr, -pi, nxt(xr), nxt(xi), fr, fi)
        g0r, g0i = nxt(xr), nxt(xi)

        def adj_scan(i, c):
            t = seg_len - 1 - i
            gr, gi = adj(t, c[0], c[1])
            gre[_rows8(t), :] = gr
            gim[_rows8(t), :] = gi
            spr, spi = sre[_rows8(t - 1), :], sim[_rows8(t - 1), :]
            return gr, gi, c[2] + spr * gr + spi * gi, c[3] + spr * gi - spi * gr

        gr, gi, dar, dai = lax.fori_loop(0, seg_len - 1, adj_scan, (g0r, g0i, z, z))
        gr, gi = adj(0, gr, gi)
        gre[_rows8(0), :] = gr
        gim[_rows8(0), :] = gi
        dar_ref[...] = jnp.sum(dar + i_r * gr + i_i * gi, axis=0, keepdims=True)
        dai_ref[...] = jnp.sum(dai + i_r * gi - i_i * gr, axis=0, keepdims=True)

        def d_chunk(c, carry):
            ub = u_ref[chunk(c), :].astype(BF)
            grb = gre[chunk(c), :].astype(BF)
            gib = gim[chunk(c), :].astype(BF)
            du = _dot(grb, bdr_ref[...], NT) + _dot(gib, bdi_ref[...], NT)
            du_ref[chunk(c), :] = du + d_ref[...] * dy_ref[chunk(c), :]
            dd = carry[2] + jnp.sum(dy_ref[chunk(c), :] * u_ref[chunk(c), :], axis=0, keepdims=True)
            return carry[0] + _dot(ub, grb, TN), carry[1] + _dot(ub, gib, TN), dd

        zb = jnp.zeros((LANES, sh), F32)
        dbr, dbi, dd = lax.fori_loop(0, R // rc, d_chunk, (zb, zb, jnp.zeros((1, LANES), F32)))
        dbdr_ref[...] = dbr
        dbdi_ref[...] = dbi
        dd_ref[...] = dd

    bd_sh = jax.ShapeDtypeStruct((nj, LANES, STATE_COLS), F32)
    cd_sh = jax.ShapeDtypeStruct((nj, STATE_COLS, LANES), F32)
    a_sh = jax.ShapeDtypeStruct((nj, 1, STATE_COLS), F32)
    return _call(
        body, "s5_bwd", (nj,),
        [sp["rows"], sp["rows"], sp["bd"], sp["bd"], sp["cd"], sp["cd"], sp["a"], sp["a"],
         sp["vec"], sp["init"], sp["init"]],
        [sp["rows"], sp["bd"], sp["bd"], sp["cd"], sp["cd"], sp["a"], sp["a"], sp["vec"]],
        [jax.ShapeDtypeStruct((R, D), F32), bd_sh, bd_sh, cd_sh, cd_sh, a_sh, a_sh,
         jax.ShapeDtypeStruct((1, D), F32)],
        (u, dy, bdr, bdi, cdr, cdi, are, aim, dsk, init_re, init_im),
        scratch=[pltpu.VMEM((R, sh), F32) for _ in range(4)], comm=comm)


def _glu_fwd(yg, wa, wb, h):
    T, D = yg.shape
    N = wa.shape[1]
    bm = min(ROW_TILE, T)
    bn = min(ROW_TILE, N)
    wspec = pl.BlockSpec((D, bn), lambda i, j, k: (0, j))
    ospec = pl.BlockSpec((bm, bn), lambda i, j, k: (i, j))

    def epi(accs, ex, outs):
        pa, pb = accs
        outs[0][...] = ex[0][...] + pa * _sig(pb)
        outs[1][...] = pa.astype(BF)
        outs[2][...] = pb.astype(BF)

    return _mm("glu_fwd", (T // bm, N // bn, 1),
               [(yg, pl.BlockSpec((bm, D), lambda i, j, k: (i, 0))), (wa, wspec), (wb, wspec)],
               [(0, 1, NN, 0), (0, 2, NN, 1)], 2, None, [(h, ospec)],
               [(jax.ShapeDtypeStruct((T, N), F32), ospec), (jax.ShapeDtypeStruct((T, N), BF), ospec),
                (jax.ShapeDtypeStruct((T, N), BF), ospec)], epi)


def _glu_bwd_gates(dz, pa, pb):
    T, D = dz.shape
    bm = min(ROW_TILE, T)

    def body(dz_ref, pa_ref, pb_ref, dpa_ref, dpb_ref):
        dz = dz_ref[...]
        sg = _sig(pb_ref[...].astype(F32))
        dpa_ref[...] = (dz * sg).astype(BF)
        dpb_ref[...] = (dz * pa_ref[...].astype(F32) * sg * (1.0 - sg)).astype(BF)

    row = pl.BlockSpec((bm, D), lambda i: (i, 0))
    return pl.pallas_call(
        body, grid=(T // bm,), in_specs=[row] * 3, out_specs=[row] * 2,
        out_shape=[jax.ShapeDtypeStruct((T, D), BF)] * 2, name="glu_bwd_gates",
        compiler_params=_cparams(1))(dz, pa, pb)


def _glu_bwd_y(dpa, dpb, wa, wb, y_pre, comm=None):
    T, N = dpa.shape
    D = wa.shape[0]
    bm = min(ROW_TILE, T)
    bn = min(ROW_TILE, D)
    aspec = pl.BlockSpec((bm, N), lambda i, j, k: (i, 0))
    wspec = pl.BlockSpec((bn, N), lambda i, j, k: (j, 0))
    ospec = pl.BlockSpec((bm, bn), lambda i, j, k: (i, j))

    def epi(accs, ex, outs):
        outs[0][...] = accs[0] * _gelu_grad(ex[0][...])

    return _mm("glu_bwd_y", (T // bm, D // bn, 1), [(dpa, aspec), (wa, wspec), (dpb, aspec), (wb, wspec)],
               [(0, 1, NT, 0), (2, 3, NT, 0)], 1, None, [(y_pre, ospec)],
               [(jax.ShapeDtypeStruct((T, D), F32), ospec)], epi, comm=comm)[0]


def _block_diag_in(x, nj):
    g = GROUPS_PER_BLOCK
    x = x.reshape(nj, g, 1, S5_GROUP, S5_STATE)
    eye = jnp.eye(g, dtype=bool)[None, :, :, None, None]
    full = jnp.where(eye, x, 0.0)
    return full.transpose(0, 1, 3, 2, 4).reshape(nj, g * S5_GROUP, g * S5_STATE)


def _block_diag_out(x, nj):
    return _block_diag_in(x, nj).transpose(0, 2, 1)


def _diag_of_in(m, nj):
    g = GROUPS_PER_BLOCK
    m5 = m.reshape(nj, g, S5_GROUP, g, S5_STATE)
    d = jnp.diagonal(m5, axis1=1, axis2=3)
    return d.transpose(0, 3, 1, 2).reshape(nj * g, S5_GROUP, S5_STATE)


def _mixer_s5_fwd(h, gain, p, dsk, wa, wb, nb, seq, comm_s5=None):
    T, D = h.shape
    nj = D // LANES
    nseg = SUBLANES // nb
    seg_len = seq // nseg
    G = p["s5_lambda_re"].shape[1]
    lam_re = p["s5_lambda_re"].reshape(G, 1, S5_STATE)
    lam_im = p["s5_lambda_im"].reshape(G, 1, S5_STATE)
    log_dt = p["s5_log_dt"].reshape(G, 1, 1)
    bt_re = p["s5_b_re"][0].transpose(0, 2, 1)
    bt_im = p["s5_b_im"][0].transpose(0, 2, 1)
    ar, ai, bbr, bbi = _s5_discretize(lam_re, lam_im, log_dt, bt_re, bt_im)
    mats = (_block_diag_in(bbr, nj).astype(BF), _block_diag_in(bbi, nj).astype(BF),
            _block_diag_out(p["s5_c_re"][0], nj).astype(BF),
            _block_diag_out(-p["s5_c_im"][0], nj).astype(BF),
            ar.reshape(nj, 1, STATE_COLS), ai.reshape(nj, 1, STATE_COLS), dsk)
    h_seg = _to_seg(h, seg_len)
    u = _rms_fwd(h_seg, gain, F32)
    y_pre, yg, init_re, init_im = _s5_fwd(u, mats, seg_len, nseg, comm=comm_s5)
    h_out, pa, pb = _glu_fwd(yg, wa, wb, h_seg)
    disc_in = (lam_re, lam_im, log_dt, bt_re, bt_im)
    return _to_tok(h_out, seg_len), (h_seg, u, mats, y_pre, yg, init_re, init_im, pa, pb, disc_in, seg_len, nseg)


def _mixer_s5_bwd(dh, saved, gain, wa, wb, reduce_start, carry):
    h_seg, u, mats, y_pre, yg, init_re, init_im, pa, pb, disc_in, seg_len, nseg = saved
    T, D = h_seg.shape
    nj = D // LANES
    G = nj * GROUPS_PER_BLOCK
    dh_seg = _to_seg(dh, seg_len)
    dpa, dpb = _glu_bwd_gates(dh_seg, pa, pb)
    dy = _glu_bwd_y(dpa, dpb, wa, wb, y_pre, comm=carry)
    dwa, dwb = _mm_tn("glu_dw", yg, [dpa, dpb])
    comm = _merge_comms(reduce_start(["s5_glu_wa", "s5_glu_wb"],
                                     [dwa.reshape(N_DEV, -1, D), dwb.reshape(N_DEV, -1, D)]))
    du, dbdr, dbdi, dcdr, dcdi, dar, dai, dd = _s5_bwd(u, dy, mats, init_re, init_im, seg_len, nseg, comm=comm)
    d_bbr = _diag_of_in(dbdr, nj)
    d_bbi = _diag_of_in(dbdi, nj)
    d_c_re = _diag_of_in(dcdr.transpose(0, 2, 1), nj)
    d_c_im = -_diag_of_in(dcdi.transpose(0, 2, 1), nj)
    dlr, dli, dld, dbr, dbi = _s5_discretize_bwd(
        *disc_in, dar.reshape(G, 1, S5_STATE), dai.reshape(G, 1, S5_STATE), d_bbr, d_bbi)
    small = {"s5_lambda_re": dlr.reshape(1, G, S5_STATE), "s5_lambda_im": dli.reshape(1, G, S5_STATE),
             "s5_log_dt": dld.reshape(1, G),
             "s5_b_re": dbr.transpose(0, 2, 1)[None], "s5_b_im": dbi.transpose(0, 2, 1)[None],
             "s5_c_re": d_c_re[None], "s5_c_im": d_c_im[None], "s5_d": dd}
    dh_in, _, dgain = _rms_bwd(du, h_seg, gain, dh_seg)
    dh_in = _to_tok(dh_in, seg_len)
    return dh_in, dh_in.astype(BF), dgain, small


def _mesh_pos():
    return lax.axis_index("x"), lax.axis_index("y"), lax.axis_index("c")


class _Gather:
    def __init__(self, srcs, slots, send_sems, recv_sems):
        self.srcs, self.slots, self.send_sems, self.recv_sems = srcs, slots, send_sems, recv_sems
        x, y, c = _mesh_pos()
        self.c = c
        self.me, self.sib = (x, y, c), (x, y, 1 - c)
        self.chips = [(1 - x, y), (x, 1 - y), (1 - x, 1 - y)]

    def copy(self, a, k, block, to, own=False):
        dst = self.slots[a].at[4 * block[0] + 2 * block[1] + block[2]]
        return pltpu.make_async_remote_copy(
            src_ref=self.srcs[a] if own else dst, dst_ref=dst, send_sem=self.send_sems.at[7 * a + k],
            recv_sem=self.recv_sems.at[7 * a + k], device_id=to, device_id_type=MESH)

    def own_copies(self, a):
        cps = [self.copy(a, 0, self.me, self.sib, own=True)]
        return cps + [self.copy(a, 1 + j, self.me, (*chip, self.c), own=True) for j, chip in enumerate(self.chips)]

    def start(self):
        for a in range(len(self.srcs)):
            for cp in self.own_copies(a):
                cp.start()

    def finish(self):
        n = len(self.srcs)
        for a in range(n):
            for j, chip in enumerate(self.chips):
                self.copy(a, 1 + j, (*chip, self.c), self.me).wait_recv()
                self.copy(a, 4 + j, (*chip, self.c), self.sib).start()
        for a in range(n):
            self.copy(a, 0, self.sib, self.me).wait_recv()
            for j, chip in enumerate(self.chips):
                self.copy(a, 4 + j, (*chip, 1 - self.c), self.me).wait_recv()
        for a in range(n):
            for cp in self.own_copies(a):
                cp.wait_send()
            for j, chip in enumerate(self.chips):
                self.copy(a, 4 + j, (*chip, self.c), self.sib).wait_send()


def _gather_comm(arrs):
    n = len(arrs)

    def local(xs, outs, sems, a):
        x, y, c = _mesh_pos()
        return pltpu.make_async_copy(xs[a], outs[a].at[4 * x + 2 * y + c], sems[2].at[a])

    def start(xs, outs, sems):
        for a in range(n):
            local(xs, outs, sems, a).start()
        _Gather(xs, outs, sems[0], sems[1]).start()

    def finish(xs, outs, sems):
        _Gather(xs, outs, sems[0], sems[1]).finish()
        for a in range(n):
            local(xs, outs, sems, a).wait()

    return _Comm(list(arrs), [jax.ShapeDtypeStruct((N_DEV,) + a.shape, a.dtype) for a in arrs],
                 [pltpu.SemaphoreType.DMA((7 * n,)), pltpu.SemaphoreType.DMA((7 * n,)),
                  pltpu.SemaphoreType.DMA((n,))], start, finish)


def _exchange_comm(parts):
    n = len(parts)

    def copies(ps, outs, sems):
        x, y, c = _mesh_pos()
        cps = []
        for a in range(n):
            for j in range(1, 4):
                to = (jnp.bitwise_xor(x, j // 2), jnp.bitwise_xor(y, j % 2), c)
                cps.append(pltpu.make_async_remote_copy(
                    src_ref=ps[a].at[j], dst_ref=outs[a].at[j - 1], send_sem=sems[0].at[3 * a + j - 1],
                    recv_sem=sems[1].at[3 * a + j - 1], device_id=to, device_id_type=MESH))
        return cps

    def start(ps, outs, sems):
        for cp in copies(ps, outs, sems):
            cp.start()

    def finish(ps, outs, sems):
        for cp in copies(ps, outs, sems):
            cp.wait()

    return _Comm(list(parts), [jax.ShapeDtypeStruct((3,) + p.shape[1:], p.dtype) for p in parts],
                 [pltpu.SemaphoreType.DMA((3 * n,)), pltpu.SemaphoreType.DMA((3 * n,))], start, finish)


def _run_comm(comm, name):
    ci, co = len(comm.ins), len(comm.outs)

    def body(*refs):
        comm.start(refs[:ci], refs[ci:ci + co], refs[ci + co:])
        comm.finish(refs[:ci], refs[ci:ci + co], refs[ci + co:])

    any_spec = pl.BlockSpec(memory_space=pl.ANY)
    comm.set_results(pl.pallas_call(
        body, in_specs=[any_spec] * ci, out_specs=[any_spec] * co, out_shape=list(comm.outs),
        scratch_shapes=list(comm.sems), name=name, compiler_params=_cparams(0))(*comm.ins))


def _pair_exchange(grads, name):
    n = len(grads)

    def body(*refs):
        gs, outs = refs[:n], refs[n:2 * n]
        send_sems, recv_sems = refs[2 * n:]
        x, y, c = _mesh_pos()
        copies = []
        for a in range(n):
            for k in range(4):
                copies.append(pltpu.make_async_remote_copy(
                    src_ref=gs[a].at[2 * k + 1 - c], dst_ref=outs[a].at[k], send_sem=send_sems.at[4 * a + k],
                    recv_sem=recv_sems.at[4 * a + k], device_id=(x, y, 1 - c), device_id_type=MESH))
        for cp in copies:
            cp.start()
        for cp in copies:
            cp.wait()

    any_spec = pl.BlockSpec(memory_space=pl.ANY)
    return pl.pallas_call(
        body, in_specs=[any_spec] * n, out_specs=[any_spec] * n,
        out_shape=[jax.ShapeDtypeStruct((4,) + g.shape[1:], g.dtype) for g in grads],
        scratch_shapes=[pltpu.SemaphoreType.DMA((4 * n,)), pltpu.SemaphoreType.DMA((4 * n,))],
        name=name, compiler_params=_cparams(0))(*grads)


def _pair_sum(grad, recv, pos):
    _, R, C = grad.shape
    br = _row_block(R, C)

    def body(pos_ref, g_ref, r_ref, o_ref):
        o_ref[...] = (g_ref[...].astype(F32) + r_ref[...].astype(F32)).astype(BF)

    def chip(j, p):
        return jnp.bitwise_xor(p[1], j)

    return pl.pallas_call(
        body, grid_spec=pltpu.PrefetchScalarGridSpec(
            num_scalar_prefetch=1, grid=(4, R // br),
            in_specs=[pl.BlockSpec((None, br, C), lambda j, i, p: (2 * chip(j, p) + p[0], i, 0)),
                      pl.BlockSpec((None, br, C), lambda j, i, p: (chip(j, p), i, 0))],
            out_specs=pl.BlockSpec((None, br, C), lambda j, i, p: (j, i, 0))),
        out_shape=jax.ShapeDtypeStruct((4, R, C), BF), name="pair_sum", compiler_params=_cparams(2))(pos, grad, recv)


def _adamw(w, g, m, v):
    m = ADAM_B1 * m + (1.0 - ADAM_B1) * g
    v = ADAM_B2 * v + (1.0 - ADAM_B2) * (g * g)
    m_hat = m / (1.0 - ADAM_B1 ** ADAM_STEP)
    v_hat = v / (1.0 - ADAM_B2 ** ADAM_STEP)
    return -ADAM_LR * (m_hat / (jnp.sqrt(v_hat) + ADAM_EPS) + ADAM_WD * w), m, v


def _adamw_piece(w, m, v, piece, part, recv, bufs):
    _, R, C = w.shape
    br = _row_block(R, C)

    def body(w_ref, m_ref, v_ref, p_ref, r_ref, b0, b1, b2, b3, g_ref, d_ref, nm_ref, nv_ref):
        g = p_ref[...].astype(F32)
        for j in range(3):
            g = g + r_ref[j].astype(F32)
        d, nm, nv = _adamw(w_ref[...], g, m_ref[...], v_ref[...])
        g_ref[...] = g
        d_ref[...] = d
        nm_ref[...] = nm
        nv_ref[...] = nv

    row = pl.BlockSpec((None, br, C), lambda i: (piece, i, 0))
    any_spec = pl.BlockSpec(memory_space=pl.ANY)
    return pl.pallas_call(
        body, grid=(R // br,),
        in_specs=[row, row, row, pl.BlockSpec((None, br, C), lambda i: (0, i, 0)),
                  pl.BlockSpec((3, br, C), lambda i: (0, i, 0))] + [any_spec] * 4,
        out_specs=[row] * 4, out_shape=[jax.ShapeDtypeStruct(w.shape, F32)] * 4,
        input_output_aliases={5: 0, 6: 1, 7: 2, 8: 3}, name="adamw_piece",
        compiler_params=_cparams(1))(w, m, v, part, recv, *bufs)


def _all_reduce_small(x):
    rows = x.shape[0]

    def body(x_ref, o_ref, buf, send_sems, recv_sems):
        xp, yp, cp = _mesh_pos()
        buf[4 * xp + 2 * yp + cp] = x_ref[...]
        gather = _Gather([x_ref], [buf], send_sems, recv_sems)
        gather.start()
        gather.finish()
        acc = buf[0]
        for d in range(1, N_DEV):
            acc = acc + buf[d]
        o_ref[...] = acc

    vm = pl.BlockSpec(memory_space=pltpu.VMEM)
    return pl.pallas_call(
        body, in_specs=[vm], out_specs=vm, out_shape=jax.ShapeDtypeStruct(x.shape, F32),
        scratch_shapes=[pltpu.VMEM((N_DEV, rows, LANES), F32), pltpu.SemaphoreType.DMA((7,)),
                        pltpu.SemaphoreType.DMA((7,))],
        name="all_reduce_small", compiler_params=_cparams(0))(x)


def _adamw_small(w, g, m, v):
    def body(w_ref, g_ref, m_ref, v_ref, d_ref, nm_ref, nv_ref):
        d, nm, nv = _adamw(w_ref[...], g_ref[...], m_ref[...], v_ref[...])
        d_ref[...] = d
        nm_ref[...] = nm
        nv_ref[...] = nv

    sh = jax.ShapeDtypeStruct(w.shape, F32)
    return pl.pallas_call(body, out_shape=[sh] * 3, name="adamw_small", compiler_params=_cparams(0))(w, g, m, v)


def _pack(arrs):
    flat = jnp.concatenate([a.reshape(-1).astype(F32) for a in arrs])
    rows = -(-flat.shape[0] // (SUBLANES * LANES)) * SUBLANES
    return jnp.pad(flat, (0, rows * LANES - flat.shape[0])).reshape(rows, LANES)


def _unpack(buf, shapes):
    flat = buf.reshape(-1)
    out, off = [], 0
    for s in shapes:
        n = 1
        for d in s:
            n *= d
        out.append(flat[off:off + n].reshape(s))
        off += n
    return out


BIG = ("ffn_w1", "ffn_w3", "ffn_w2", "ab_w_in", "ab_w_out", "s5_glu_wa", "s5_glu_wb")
NAMES = ("ln_ffn_pre", "ln_mix", "ln_ffn_post", "ln_final", "ffn_w1", "ffn_w3", "ffn_w2", "ab_w_in",
         "ab_conv_w", "ab_w_out", "s5_lambda_re", "s5_lambda_im", "s5_log_dt", "s5_b_re", "s5_b_im",
         "s5_c_re", "s5_c_im", "s5_d", "s5_glu_wa", "s5_glu_wb")


def kernel(x, ln_ffn_pre, ln_mix, ln_ffn_post, ln_final, ffn_w1, ffn_w3, ffn_w2, ab_w_in, ab_conv_w, ab_w_out, s5_lambda_re, s5_lambda_im, s5_log_dt, s5_b_re, s5_b_im, s5_c_re, s5_c_im, s5_d, s5_glu_wa, s5_glu_wb, loss_target, m_ln_ffn_pre, m_ln_mix, m_ln_ffn_post, m_ln_final, m_ffn_w1, m_ffn_w3, m_ffn_w2, m_ab_w_in, m_ab_conv_w, m_ab_w_out, m_s5_lambda_re, m_s5_lambda_im, m_s5_log_dt, m_s5_b_re, m_s5_b_im, m_s5_c_re, m_s5_c_im, m_s5_d, m_s5_glu_wa, m_s5_glu_wb, v_ln_ffn_pre, v_ln_mix, v_ln_ffn_post, v_ln_final, v_ffn_w1, v_ffn_w3, v_ffn_w2, v_ab_w_in, v_ab_conv_w, v_ab_w_out, v_s5_lambda_re, v_s5_lambda_im, v_s5_log_dt, v_s5_b_re, v_s5_b_im, v_s5_c_re, v_s5_c_im, v_s5_d, v_s5_glu_wa, v_s5_glu_wb):
    w = dict(zip(NAMES, (ln_ffn_pre, ln_mix, ln_ffn_post, ln_final, ffn_w1, ffn_w3, ffn_w2, ab_w_in, ab_conv_w,
                         ab_w_out, s5_lambda_re, s5_lambda_im, s5_log_dt, s5_b_re, s5_b_im, s5_c_re, s5_c_im,
                         s5_d, s5_glu_wa, s5_glu_wb)))
    mom = dict(zip(NAMES, (m_ln_ffn_pre, m_ln_mix, m_ln_ffn_post, m_ln_final, m_ffn_w1, m_ffn_w3, m_ffn_w2,
                           m_ab_w_in, m_ab_conv_w, m_ab_w_out, m_s5_lambda_re, m_s5_lambda_im, m_s5_log_dt,
                           m_s5_b_re, m_s5_b_im, m_s5_c_re, m_s5_c_im, m_s5_d, m_s5_glu_wa, m_s5_glu_wb)))
    var = dict(zip(NAMES, (v_ln_ffn_pre, v_ln_mix, v_ln_ffn_post, v_ln_final, v_ffn_w1, v_ffn_w3, v_ffn_w2,
                           v_ab_w_in, v_ab_conv_w, v_ab_w_out, v_s5_lambda_re, v_s5_lambda_im, v_s5_log_dt,
                           v_s5_b_re, v_s5_b_im, v_s5_c_re, v_s5_c_im, v_s5_d, v_s5_glu_wa, v_s5_glu_wb)))
    nb, seq, D = x.shape
    T = nb * seq
    assert ln_mix.shape[0] == 2 and ab_w_in.shape[0] == 1 and s5_glu_wa.shape[0] == 1
    xc, yc, cc = _mesh_pos()
    dev = 4 * xc + 2 * yc + cc
    pos = jnp.stack([cc, 2 * xc + yc]).astype(jnp.int32)
    bq = min(ATTN_TILE, seq)
    tabs =_rope_tables(seq) + (_branch_bias(seq // bq, bq),)

    def ffn_piece(k, li, fj):
        return w[k][li, fj].astype(BF)

    g0 = _gather_comm([ffn_piece("ffn_w1", 0, 0), ffn_piece("ffn_w3", 0, 0), ab_conv_w[0], s5_d])
    _run_comm(g0, "gather_first")
    w1, w3 = {(0, 0): g0.results[0]}, {(0, 0): g0.results[1]}
    w2 = {}
    conv_w = g0.results[2].transpose(1, 0, 2).reshape(3, -1)
    dsk = g0.results[3].reshape(1, D)
    gains = {k: [w[k][i:i + 1] for i in range(2)] for k in ("ln_ffn_pre", "ln_mix", "ln_ffn_post")}

    h = x.reshape(T, D)
    saved = {}

    def ffn_fwd(h, gain, key, comm_up):
        n = _rms_fwd(h, gain, BF)
        a1, a3, g = _ffn_up(n, w1[key], w3[key], comm=comm_up)
        return n, a1, a3, g

    c_up = _gather_comm([ffn_piece("ffn_w2", 0, 0), ab_w_in[0].astype(BF)])
    n, a1, a3, g = ffn_fwd(h, gains["ln_ffn_pre"][0], (0, 0), c_up)
    w2[(0, 0)], wing = c_up.results
    c_dn = _gather_comm([ab_w_out[0].astype(BF), ffn_piece("ffn_w1", 0, 1)])
    saved["pre0"] = (h, n, a1, a3, g)
    h = _ffn_down(g, w2[(0, 0)], h, comm=c_dn)
    wout = c_dn.results[0].reshape(-1, D)
    w1[(0, 1)] = c_dn.results[1]
    c_proj = _gather_comm([ffn_piece("ffn_w3", 0, 1)])
    c_attn = _gather_comm([ffn_piece("ffn_w2", 0, 1), ffn_piece("ffn_w1", 1, 0)])
    h, saved["mix0"] = _mixer_ab_fwd(h, gains["ln_mix"][0], wing, conv_w, wout, tabs, nb, seq, c_proj, c_attn)
    w3[(0, 1)] = c_proj.results[0]
    w2[(0, 1)], w1[(1, 0)] = c_attn.results
    c_up = _gather_comm([ffn_piece("ffn_w3", 1, 0), s5_glu_wa[0].astype(BF)])
    n, a1, a3, g = ffn_fwd(h, gains["ln_ffn_post"][0], (0, 1), c_up)
    w3[(1, 0)] = c_up.results[0]
    wa = c_up.results[1].reshape(-1, D)
    c_dn = _gather_comm([ffn_piece("ffn_w2", 1, 0)])
    saved["post0"] = (h, n, a1, a3, g)
    h = _ffn_down(g, w2[(0, 1)], h, comm=c_dn)
    w2[(1, 0)] = c_dn.results[0]
    c_up = _gather_comm([s5_glu_wb[0].astype(BF), ffn_piece("ffn_w1", 1, 1)])
    n, a1, a3, g = ffn_fwd(h, gains["ln_ffn_pre"][1], (1, 0), c_up)
    wb = c_up.results[0].reshape(-1, D)
    w1[(1, 1)] = c_up.results[1]
    c_dn = _gather_comm([ffn_piece("ffn_w3", 1, 1)])
    saved["pre1"] = (h, n, a1, a3, g)
    h = _ffn_down(g, w2[(1, 0)], h, comm=c_dn)
    w3[(1, 1)] = c_dn.results[0]
    c_s5 = _gather_comm([ffn_piece("ffn_w2", 1, 1)])
    h, saved["mix1"] = _mixer_s5_fwd(h, gains["ln_mix"][1], w, dsk, wa, wb, nb, seq, c_s5)
    w2[(1, 1)] = c_s5.results[0]
    n, a1, a3, g = ffn_fwd(h, gains["ln_ffn_post"][1], (1, 1), None)
    saved["post1"] = (h, n, a1, a3, g)
    h = _ffn_down(g, w2[(1, 1)], h)
    dh, dhb, d_ln_final, loss_part = _loss_head(h, ln_final.reshape(1, D), loss_target.reshape(T, D))
    loss = lax.psum(loss_part[0, 0], ("x", "y", "c"))

    reduced = {}

    def reduce_start(names, grads):
        recv = _pair_exchange(grads, "pair_exchange")
        comms = []
        for nm, g, r in zip(names, grads, recv):
            part = _pair_sum(g, r, pos)
            comms.append(_exchange_comm([part]))
            reduced[nm] = (part, comms[-1])
        return comms

    def ffn_bwd(dh, dhb, key, tag, gain, carry, is_last=False):
        h_in, n, a1, a3, g = saved[tag]
        da1, da3 = _ffn_bwd_hidden(dhb, w2[key], a1, a3, comm=carry)
        c2, = reduce_start([("ffn_w2",) + key], [_ffn_dw2(g, dhb)])
        dw1, dw3 = _ffn_dw13(n, da1, da3, comm=c2)
        c1, c3 = reduce_start([("ffn_w1",) + key, ("ffn_w3",) + key], [dw1, dw3])
        dn = _ffn_dn(da1, da3, w1[key], w3[key], comm=_merge_comms([c1, c3]) if is_last else c1)
        return list(_rms_bwd(dn, h_in, gain, dh)) + [None if is_last else c3]

    g_small = {"ln_final": d_ln_final.reshape(D)}
    g_ln = {k: [None, None] for k in gains}
    dh, dhb, g_ln["ln_ffn_post"][1], carry = ffn_bwd(dh, dhb, (1, 1), "post1", gains["ln_ffn_post"][1], None)
    dh, dhb, g_ln["ln_mix"][1], s5_small = _mixer_s5_bwd(
        dh, saved["mix1"], gains["ln_mix"][1], wa, wb, reduce_start, carry)
    g_small.update(s5_small)
    dh, dhb, g_ln["ln_ffn_pre"][1], carry = ffn_bwd(dh, dhb, (1, 0), "pre1", gains["ln_ffn_pre"][1], None)
    dh, dhb, g_ln["ln_ffn_post"][0], carry = ffn_bwd(dh, dhb, (0, 1), "post0", gains["ln_ffn_post"][0], carry)
    dh, dhb, g_ln["ln_mix"][0], g_small["ab_conv_w"], carry = _mixer_ab_bwd(
        dh, dhb, saved["mix0"], gains["ln_mix"][0], wing, conv_w, wout, tabs, nb, seq, reduce_start, carry)
    dh, dhb, g_ln["ln_ffn_pre"][0], _ = ffn_bwd(dh, dhb, (0, 0), "pre0", gains["ln_ffn_pre"][0], carry, is_last=True)
    grad_x = dh.reshape(nb, seq, D)
    for k in g_ln:
        g_small[k] = jnp.concatenate(g_ln[k], axis=0)

    out = {}
    for k in BIG:
        cols = w[k].shape[-1]
        pieces = [(li, fj) for li in range(2) for fj in range(2)] if w[k].ndim == 4 else [None]
        view = (len(pieces), -1, cols)
        w3d, m3d, v3d = w[k].reshape(view), mom[k].reshape(view), var[k].reshape(view)
        bufs = [lax.empty(w3d.shape, F32) for _ in range(4)]
        for q, key in enumerate(pieces):
            part, comm = reduced[k if key is None else (k,) + key]
            bufs = _adamw_piece(w3d, m3d, v3d, q, part, comm.results[0], bufs)
        out[k] = [t.reshape(w[k].shape) for t in bufs]

    small_names = [k for k in NAMES if k not in BIG]
    red = _unpack(_all_reduce_small(_pack([g_small[k] for k in small_names])),
                  [g_small[k].shape for k in small_names])
    g_red = dict(zip(small_names, red))
    cw = w["ab_conv_w"].shape[-1]
    g_red["ab_conv_w"] = lax.dynamic_slice_in_dim(g_red["ab_conv_w"], dev * cw, cw, axis=1)[None]
    dsz = w["s5_d"].shape[-1]
    g_red["s5_d"] = lax.dynamic_slice_in_dim(g_red["s5_d"].reshape(1, -1), dev * dsz, dsz, axis=1)
    shapes = [w[k].shape for k in small_names]
    g_red = {k: g_red[k].reshape(w[k].shape) for k in small_names}
    d_s, m_s, v_s = _adamw_small(_pack([w[k] for k in small_names]), _pack([g_red[k] for k in small_names]),
                                 _pack([mom[k] for k in small_names]), _pack([var[k] for k in small_names]))
    for k, d, nm, nv in zip(small_names, _unpack(d_s, shapes), _unpack(m_s, shapes), _unpack(v_s, shapes)):
        out[k] = [g_red[k], d, nm, nv]

    return (loss, grad_x, *[out[k][0] for k in NAMES], *[out[k][1] for k in NAMES],
            *[out[k][2] for k in NAMES], *[out[k][3] for k in NAMES])
```

```python
import jax
import jax.numpy as jnp
from jax import lax
from jax.experimental import pallas as pl
from jax.experimental.pallas import tpu as pltpu

F32, BF = jnp.float32, jnp.bfloat16
N_DEV = 8
MESH = pl.DeviceIdType.MESH
LANES = 128
SUBLANES = 8
VMEM_LIMIT = 56 * 2 ** 20
ROW_TILE = 512
COL_TILE = 512
ATTN_TILE = 512
SCAN_UNROLL = 4
ELEMS_PER_BLOCK = 256 * 1024
PAIR_SUM_ELEMS = 2048 * 1024
RMS_EPS = 1e-6
ROPE_THETA = 10000.0
NEG_INF = -1e30
S5_STATE = 64
S5_GROUP = 16
GROUPS_PER_BLOCK = LANES // S5_GROUP
STATE_COLS = GROUPS_PER_BLOCK * S5_STATE
DILATED_PATTERN = ((128, 1), (512, 4), (2048, 16))
ADAM_LR, ADAM_B1, ADAM_B2, ADAM_EPS, ADAM_WD, ADAM_STEP = 0.001, 0.9, 0.999, 1e-08, 0.01, 10
GELU_C = 0.7978845608028654
GELU_A = 0.044715


def _cparams(n_grid, vmem=VMEM_LIMIT):
    sem = ("arbitrary",) * n_grid if n_grid else None
    return pltpu.CompilerParams(dimension_semantics=sem, vmem_limit_bytes=vmem)


def _sig(x):
    return 1.0 / (1.0 + jnp.exp(-x))


def _gelu(x):
    return 0.5 * x * (1.0 + jnp.tanh(GELU_C * (x + GELU_A * x * x * x)))


def _gelu_grad(x):
    t = jnp.tanh(GELU_C * (x + GELU_A * x * x * x))
    return 0.5 * (1.0 + t) + 0.5 * x * (1.0 - t * t) * GELU_C * (1.0 + 3.0 * GELU_A * x * x)


def _dot(a, b, dims):
    a = a if a.dtype == BF else a.astype(BF)
    b = b if b.dtype == BF else b.astype(BF)
    return lax.dot_general(a, b, (dims, ((), ())), preferred_element_type=F32)


NN = ((1,), (0,))
NT = ((1,), (1,))
TN = ((0,), (0,))


def _row_block(rows, cols, elems=ELEMS_PER_BLOCK, mult=16):
    cap = max(mult, elems // cols)
    best = None
    for b in range(mult, min(rows, cap) + 1, mult):
        if rows % b == 0:
            best = b
    return rows if best is None else best


class _Comm:
    def __init__(self, ins, outs, sems, start, finish, members=()):
        self.ins, self.outs, self.sems, self.start, self.finish = ins, outs, sems, start, finish
        self.members = members
        self.results = None

    def set_results(self, res):
        self.results = list(res)
        off = 0
        for m in self.members:
            m.set_results(res[off:off + len(m.outs)])
            off += len(m.outs)


def _merge_comms(comms):
    comms = [c for c in comms if c is not None]
    if len(comms) < 2:
        return comms[0] if comms else None

    def each(fn_name, ins, outs, sems):
        i = o = s = 0
        for c in comms:
            ni, no, ns = len(c.ins), len(c.outs), len(c.sems)
            getattr(c, fn_name)(ins[i:i + ni], outs[o:o + no], sems[s:s + ns])
            i, o, s = i + ni, o + no, s + ns

    return _Comm([a for c in comms for a in c.ins], [a for c in comms for a in c.outs],
                 [a for c in comms for a in c.sems],
                 lambda ins, outs, sems: each("start", ins, outs, sems),
                 lambda ins, outs, sems: each("finish", ins, outs, sems), members=tuple(comms))


def _call(body, name, grid, in_specs, out_specs, out_shape, args, scratch=(), comm=None):
    in_specs, out_specs, out_shape, scratch = list(in_specs), list(out_specs), list(out_shape), list(scratch)
    if comm is None:
        return pl.pallas_call(body, grid=grid, in_specs=in_specs, out_specs=out_specs, out_shape=out_shape,
                              scratch_shapes=scratch, name=name, compiler_params=_cparams(len(grid)))(*args)
    n_in, n_out, n_sc = len(in_specs), len(out_specs), len(scratch)
    ci, co = len(comm.ins), len(comm.outs)

    def hosted(*refs):
        ins, refs = refs[:n_in], refs[n_in:]
        cins, refs = refs[:ci], refs[ci:]
        outs, refs = refs[:n_out], refs[n_out:]
        couts, refs = refs[:co], refs[co:]
        sc, csems = refs[:n_sc], refs[n_sc:]
        first = last = None
        for d, n in enumerate(grid):
            p = pl.program_id(d)
            first = (p == 0) if first is None else first & (p == 0)
            last = (p == n - 1) if last is None else last & (p == n - 1)

        @pl.when(first)
        def _():
            comm.start(cins, couts, csems)

        body(*ins, *outs, *sc)

        @pl.when(last)
        def _():
            comm.finish(cins, couts, csems)

    any_spec = pl.BlockSpec(memory_space=pl.ANY)
    res = pl.pallas_call(
        hosted, grid=grid, in_specs=in_specs + [any_spec] * ci, out_specs=out_specs + [any_spec] * co,
        out_shape=out_shape + list(comm.outs), scratch_shapes=scratch + list(comm.sems), name=name,
        compiler_params=_cparams(len(grid)))(*args, *comm.ins)
    comm.set_results(res[n_out:])
    return list(res[:n_out])


def _mm(name, grid, operands, pairs, n_acc, acc_shape, extras, outs, epilogue, comm=None, nrow=1, ncol=1):
    nk = grid[2]
    n_op, n_ex, n_out = len(operands), len(extras), len(outs)

    def part_of(ref, dim, t, n):
        if n == 1:
            return ref
        size = ref.shape[dim] // n
        idx = [slice(None)] * len(ref.shape)
        idx[dim] = pl.ds(t * size, size)
        return ref.at[tuple(idx)]

    def tile_of(ref, r, c):
        return part_of(part_of(ref, 0, r, nrow), 1, c, ncol)

    def products(op, r, c):
        parts = [None] * n_acc
        for ai, bi, dims, ci in pairs:
            a = part_of(op[ai], 1 - dims[0][0], r, nrow)
            b = part_of(op[bi], 1 - dims[1][0], c, ncol)
            d = _dot(a[...], b[...], dims)
            parts[ci] = d if parts[ci] is None else parts[ci] + d
        return parts

    def body(*refs):
        op = refs[:n_op]
        ex = refs[n_op:n_op + n_ex]
        out = refs[n_op + n_ex:n_op + n_ex + n_out]
        acc = refs[n_op + n_ex + n_out:]
        tiles = [(r, c) for r in range(nrow) for c in range(ncol)]

        def views(refs_, t):
            return [tile_of(q, *t) for q in refs_]

        if nk == 1:
            parts = products(op, *tiles[0])
            for q, t in enumerate(tiles):
                nxt = products(op, *tiles[q + 1]) if q + 1 < len(tiles) else None
                epilogue(parts, views(ex, t), views(out, t))
                parts = nxt
            return
        k = pl.program_id(2)

        @pl.when(k == 0)
        def _():
            for q in acc:
                q[...] = jnp.zeros_like(q)

        for t in tiles:
            parts = products(op, *t)
            for q, p in zip(views(acc, t), parts):
                q[...] += p

        @pl.when(k == nk - 1)
        def _():
            for t in tiles:
                epilogue([q[...] for q in views(acc, t)], views(ex, t), views(out, t))

    return _call(body, name, grid, [s for _, s in operands] + [s for _, s in extras], [s for _, s in outs],
                 [sh for sh, _ in outs], [a for a, _ in operands] + [a for a, _ in extras],
                 scratch=[pltpu.VMEM(acc_shape, F32) for _ in range(n_acc if nk > 1 else 0)], comm=comm)


def _to_seg(a, seg_len):
    T, D = a.shape
    return a.reshape(SUBLANES, seg_len, D).transpose(1, 0, 2).reshape(T, D)


def _to_tok(a, seg_len):
    T, D = a.shape
    return a.reshape(seg_len, SUBLANES, D).transpose(1, 0, 2).reshape(T, D)


def _rms_fwd(h, gain, out_dtype):
    T, D = h.shape
    bm = min(ROW_TILE, T)

    def body(h_ref, g_ref, o_ref):
        x = h_ref[...]
        r = lax.rsqrt(jnp.mean(x * x, axis=-1, keepdims=True) + RMS_EPS)
        o_ref[...] = (x * r * g_ref[...]).astype(out_dtype)

    row = pl.BlockSpec((bm, D), lambda i: (i, 0))
    return pl.pallas_call(
        body, grid=(T // bm,), in_specs=[row, pl.BlockSpec((1, D), lambda i: (0, 0))],
        out_specs=row, out_shape=jax.ShapeDtypeStruct((T, D), out_dtype), name="rms_fwd",
        compiler_params=_cparams(1))(h, gain)


def _rms_bwd_rows(dn, x, g):
    r = lax.rsqrt(jnp.mean(x * x, axis=-1, keepdims=True) + RMS_EPS)
    xh = x * r
    dng = dn * g
    dx = r * (dng - xh * jnp.mean(dng * xh, axis=-1, keepdims=True))
    return dx, jnp.sum(dn * xh, axis=0, keepdims=True)


def _rms_bwd(dn, h, gain, dh_up):
    T, D = h.shape
    bm = min(ROW_TILE, T)

    def body(dn_ref, h_ref, g_ref, up_ref, dh_ref, dhb_ref, dg_ref):
        dx, dg = _rms_bwd_rows(dn_ref[...], h_ref[...], g_ref[...])
        dh = up_ref[...] + dx
        dh_ref[...] = dh
        dhb_ref[...] = dh.astype(BF)

        @pl.when(pl.program_id(0) == 0)
        def _():
            dg_ref[...] = jnp.zeros_like(dg_ref)

        dg_ref[...] += dg

    row = pl.BlockSpec((bm, D), lambda i: (i, 0))
    vec = pl.BlockSpec((1, D), lambda i: (0, 0))
    return pl.pallas_call(
        body, grid=(T // bm,), in_specs=[row, row, vec, row], out_specs=[row, row, vec],
        out_shape=[jax.ShapeDtypeStruct((T, D), F32), jax.ShapeDtypeStruct((T, D), BF),
                   jax.ShapeDtypeStruct((1, D), F32)],
        name="rms_bwd", compiler_params=_cparams(1))(dn, h, gain, dh_up)


def _loss_head(h, gain, target):
    T, D = h.shape
    bm = min(ROW_TILE, T)

    def body(h_ref, g_ref, t_ref, dh_ref, dhb_ref, dg_ref, loss_ref):
        x = h_ref[...]
        g = g_ref[...]
        r = lax.rsqrt(jnp.mean(x * x, axis=-1, keepdims=True) + RMS_EPS)
        err = x * r * g - t_ref[...]
        part = 0.5 * jnp.sum(jnp.sum(err * err, axis=-1, keepdims=True), axis=0, keepdims=True) / D
        dx, dg = _rms_bwd_rows(err / D, x, g)
        dh_ref[...] = dx
        dhb_ref[...] = dx.astype(BF)

        @pl.when(pl.program_id(0) == 0)
        def _():
            dg_ref[...] = jnp.zeros_like(dg_ref)
            loss_ref[...] = jnp.zeros_like(loss_ref)

        dg_ref[...] += dg
        loss_ref[...] += jnp.broadcast_to(part, loss_ref.shape)

    row = pl.BlockSpec((bm, D), lambda i: (i, 0))
    vec = pl.BlockSpec((1, D), lambda i: (0, 0))
    return pl.pallas_call(
        body, grid=(T // bm,), in_specs=[row, vec, row],
        out_specs=[row, row, vec, pl.BlockSpec((SUBLANES, LANES), lambda i: (0, 0))],
        out_shape=[jax.ShapeDtypeStruct((T, D), F32), jax.ShapeDtypeStruct((T, D), BF),
                   jax.ShapeDtypeStruct((1, D), F32), jax.ShapeDtypeStruct((SUBLANES, LANES), F32)],
        name="loss_head", compiler_params=_cparams(1))(h, gain, target)


def _ffn_up(n, w1g, w3g, comm=None):
    T, D = n.shape
    fs = w1g.shape[-1]
    bm = min(ROW_TILE, T)
    wspec = pl.BlockSpec((None, D, fs), lambda s, i, k: (s, 0, 0))
    ospec = pl.BlockSpec((None, bm, fs), lambda s, i, k: (s, i, 0))

    def epi(accs, ex, outs):
        a1, a3 = accs
        sg = _sig(a1)
        silu = a1 * sg
        outs[0][...] = (a3 * sg * (1.0 + a1 * (1.0 - sg))).astype(BF)
        outs[1][...] = silu.astype(BF)
        outs[2][...] = (silu * a3).astype(BF)

    sh = jax.ShapeDtypeStruct((N_DEV, T, fs), BF)
    return _mm("ffn_up", (N_DEV, T // bm, 1),
               [(n, pl.BlockSpec((bm, D), lambda s, i, k: (i, 0))), (w1g, wspec), (w3g, wspec)],
               [(0, 1, NN, 0), (0, 2, NN, 1)], 2, None, [], [(sh, ospec)] * 3, epi, comm=comm, nrow=2)


def _ffn_down(g, w2g, h, comm=None):
    _, T, fs = g.shape
    D = h.shape[1]
    bm = min(ROW_TILE, T)
    row = pl.BlockSpec((bm, D), lambda i, j, s: (i, 0))

    def epi(accs, ex, outs):
        outs[0][...] = ex[0][...] + 0.5 * accs[0]

    return _mm("ffn_down", (T // bm, 1, N_DEV),
               [(g, pl.BlockSpec((None, bm, fs), lambda i, j, s: (s, i, 0))),
                (w2g, pl.BlockSpec((None, fs, D), lambda i, j, s: (s, 0, 0)))],
               [(0, 1, NN, 0)], 1, (bm, D), [(h, row)],
               [(jax.ShapeDtypeStruct((T, D), F32), row)], epi, comm=comm, ncol=max(1, D // COL_TILE))[0]


def _ffn_bwd_hidden(dhb, w2g, t1, t3, comm=None):
    T, D = dhb.shape
    fs = t1.shape[-1]
    bm = min(ROW_TILE, T)
    aspec = pl.BlockSpec((None, bm, fs), lambda s, i, k: (s, i, 0))

    def epi(accs, ex, outs):
        dg = 0.5 * accs[0]
        outs[0][...] = (dg * ex[0][...].astype(F32)).astype(BF)
        outs[1][...] = (dg * ex[1][...].astype(F32)).astype(BF)

    sh = jax.ShapeDtypeStruct((N_DEV, T, fs), BF)
    return _mm("ffn_bwd_hidden", (N_DEV, T // bm, 1),
               [(dhb, pl.BlockSpec((bm, D), lambda s, i, k: (i, 0))),
                (w2g, pl.BlockSpec((None, fs, D), lambda s, i, k: (s, 0, 0)))],
               [(0, 1, NT, 0)], 1, None, [(t1, aspec), (t3, aspec)], [(sh, aspec)] * 2, epi, comm=comm, nrow=2)


def _ffn_dw2(g, dhb):
    _, T, fs = g.shape
    D = dhb.shape[1]
    bn = min(COL_TILE, D)

    def epi(accs, ex, outs):
        outs[0][...] = (0.5 * accs[0]).astype(BF)

    return _mm("ffn_dw2", (N_DEV, D // bn, 1),
               [(g, pl.BlockSpec((None, T, fs), lambda s, j, k: (s, 0, 0))),
                (dhb, pl.BlockSpec((T, bn), lambda s, j, k: (0, j)))],
               [(0, 1, TN, 0)], 1, None, [],
               [(jax.ShapeDtypeStruct((N_DEV, fs, D), BF), pl.BlockSpec((None, fs, bn), lambda s, j, k: (s, 0, j)))],
               epi)[0]


def _ffn_dw13(n, da1, da3, comm=None):
    T, D = n.shape
    fs = da1.shape[-1]
    bmr = min(COL_TILE, D)
    dspec = pl.BlockSpec((None, T, fs), lambda s, r, k: (s, 0, 0))
    ospec = pl.BlockSpec((None, bmr, fs), lambda s, r, k: (s, r, 0))

    def epi(accs, ex, outs):
        outs[0][...] = accs[0].astype(BF)
        outs[1][...] = accs[1].astype(BF)

    sh = jax.ShapeDtypeStruct((N_DEV, D, fs), BF)
    return _mm("ffn_dw13", (N_DEV, D // bmr, 1),
               [(n, pl.BlockSpec((T, bmr), lambda s, r, k: (0, r))), (da1, dspec), (da3, dspec)],
               [(0, 1, TN, 0), (0, 2, TN, 1)], 2, None, [], [(sh, ospec)] * 2, epi, comm=comm)


def _ffn_dn(da1, da3, w1g, w3g, comm=None):
    _, T, fs = da1.shape
    D = w1g.shape[-2]
    bm = min(ROW_TILE, T)
    dspec = pl.BlockSpec((None, bm, fs), lambda i, j, s: (s, i, 0))
    wspec = pl.BlockSpec((None, D, fs), lambda i, j, s: (s, 0, 0))
    row = pl.BlockSpec((bm, D), lambda i, j, s: (i, 0))

    def epi(accs, ex, outs):
        outs[0][...] = accs[0]

    return _mm("ffn_dn", (T // bm, 1, N_DEV),
               [(da1, dspec), (w1g, wspec), (da3, dspec), (w3g, wspec)],
               [(0, 1, NT, 0), (2, 3, NT, 0)], 1, (bm, D), [],
               [(jax.ShapeDtypeStruct((T, D), F32), row)], epi, comm=comm, ncol=max(1, D // COL_TILE))[0]


def _rope_tables(seq):
    half = LANES // 2
    inv = ROPE_THETA ** (-jnp.arange(0, half, dtype=F32) * 2.0 / LANES)
    ang = jnp.arange(seq, dtype=F32)[:, None] * inv[None, :]
    cos, sin = jnp.cos(ang), jnp.sin(ang)
    return jnp.concatenate([cos, cos], axis=1), jnp.concatenate([-sin, sin], axis=1)


def _branch_bias(nq, bq):
    d = (jnp.arange(nq)[:, None, None] * bq + jnp.arange(bq)[None, :, None]
         - jnp.arange(bq)[None, None, :])
    mult = jnp.zeros(d.shape, F32)
    for window, dil in DILATED_PATTERN:
        mult = mult + ((d >= 0) & (d % dil == 0) & (d <= window)).astype(F32)
    return jnp.where(mult > 0, jnp.log(jnp.maximum(mult, 1.0)), NEG_INF)


def _proj_fwd(u, wing, comm=None):
    T, D = u.shape
    ws = wing.shape[-1]
    bm = min(ROW_TILE, T)

    def epi(accs, ex, outs):
        outs[0][...] = accs[0]

    return _mm("proj_fwd", (N_DEV, T // bm, 1),
               [(u, pl.BlockSpec((bm, D), lambda s, i, k: (i, 0))),
                (wing, pl.BlockSpec((None, D, ws), lambda s, i, k: (s, 0, 0)))],
               [(0, 1, NN, 0)], 1, None, [],
               [(jax.ShapeDtypeStruct((T, N_DEV * ws), F32),
                 pl.BlockSpec((bm, ws), lambda s, i, k: (i, s)))], epi, comm=comm)[0]


def _rope_fwd(proj, cosf, sinf, seq, nh):
    T = proj.shape[0]
    bs = min(ROW_TILE, seq)
    nst = seq // bs
    scale = LANES ** -0.5

    def body(x_ref, c_ref, s_ref, o_ref):
        j = pl.program_id(1)
        t = x_ref[...]
        rot = t * c_ref[...] + pltpu.roll(t, LANES // 2, 1) * s_ref[...]
        rot = rot * jnp.where(j < nh, scale, 1.0)
        o_ref[...] = jnp.where(j < 2 * nh, rot, t).astype(BF)

    blk = pl.BlockSpec((bs, LANES), lambda r, j: (r, j))
    tab = pl.BlockSpec((bs, LANES), lambda r, j: (r % nst, 0))
    return pl.pallas_call(
        body, grid=(T // bs, 3 * nh), in_specs=[blk, tab, tab], out_specs=blk,
        out_shape=jax.ShapeDtypeStruct((T, 3 * nh * LANES), BF), name="rope_fwd",
        compiler_params=_cparams(2))(proj, cosf, sinf)


def _attn_fwd(qkv, bias, nb, seq, nh, comm=None):
    T = nb * seq
    bq = bias.shape[1]
    nq = seq // bq

    def body(q_ref, k_ref, v_ref, b_ref, o_ref, lse_ref):
        qi = pl.program_id(2)
        q = q_ref[...]

        def step(kj, carry):
            m, l, acc = carry
            rows = pl.ds(pl.multiple_of(kj * bq, bq), bq)
            s = _dot(q, k_ref[rows, :], NT) + b_ref[qi - kj]
            m_new = jnp.maximum(m, jnp.max(s, axis=1, keepdims=True))
            p = jnp.exp(s - m_new)
            alpha = jnp.exp(m - m_new)
            l = alpha * l + jnp.sum(p, axis=1, keepdims=True)
            acc = alpha * acc + _dot(p, v_ref[rows, :], NN)
            return m_new, l, acc

        init = (jnp.full((bq, 1), NEG_INF, F32), jnp.zeros((bq, 1), F32), jnp.zeros((bq, LANES), F32))
        m, l, acc = lax.fori_loop(0, qi + 1, step, init)
        o_ref[...] = (acc / l).astype(BF)
        lse_ref[...] = m + jnp.log(l)

    return _call(
        body, "attn_fwd", (nb, nh, nq),
        [pl.BlockSpec((bq, LANES), lambda b, h, i: (b * nq + i, h)),
         pl.BlockSpec((seq, LANES), lambda b, h, i: (b, nh + h)),
         pl.BlockSpec((seq, LANES), lambda b, h, i: (b, 2 * nh + h)),
         pl.BlockSpec((nq, bq, bq), lambda b, h, i: (0, 0, 0))],
        [pl.BlockSpec((bq, LANES), lambda b, h, i: (b * nq + i, h)),
         pl.BlockSpec((None, bq, 1), lambda b, h, i: (h, b * nq + i, 0))],
        [jax.ShapeDtypeStruct((T, 2 * nh * LANES), BF), jax.ShapeDtypeStruct((nh, T, 1), F32)],
        (qkv, qkv, qkv, bias), comm=comm)


def _attn_bwd_dq(qkv, cat, dcat, lse, bias, nb, seq, nh, comm=None):
    T = nb * seq
    bq = bias.shape[1]
    nq = seq // bq

    def body(q_ref, k_ref, v_ref, o_ref, do_ref, lse_ref, b_ref, dq_ref, delta_ref):
        qi = pl.program_id(2)
        q = q_ref[...]
        do = do_ref[...]
        dob = do.astype(BF)
        lse_t = lse_ref[...]
        delta = jnp.sum(do * o_ref[...].astype(F32), axis=1, keepdims=True)
        delta_ref[...] = delta

        def step(kj, dq):
            rows = pl.ds(pl.multiple_of(kj * bq, bq), bq)
            k = k_ref[rows, :]
            p = jnp.exp(_dot(q, k, NT) + b_ref[qi - kj] - lse_t)
            ds = p * (_dot(dob, v_ref[rows, :], NT) - delta)
            return dq + _dot(ds, k, NN)

        dq_ref[...] = lax.fori_loop(0, qi + 1, step, jnp.zeros((bq, LANES), F32))

    tile = pl.BlockSpec((bq, LANES), lambda b, h, i: (b * nq + i, h))
    stat = pl.BlockSpec((None, bq, 1), lambda b, h, i: (h, b * nq + i, 0))
    return _call(
        body, "attn_bwd_dq", (nb, nh, nq),
        [tile, pl.BlockSpec((seq, LANES), lambda b, h, i: (b, nh + h)),
         pl.BlockSpec((seq, LANES), lambda b, h, i: (b, 2 * nh + h)), tile, tile, stat,
         pl.BlockSpec((nq, bq, bq), lambda b, h, i: (0, 0, 0))],
        [tile, stat],
        [jax.ShapeDtypeStruct((T, nh * LANES), F32), jax.ShapeDtypeStruct((nh, T, 1), F32)],
        (qkv, qkv, qkv, cat, dcat, lse, bias), comm=comm)


def _attn_bwd_dkv(qkv, dcat, lse, delta, bias, nb, seq, nh):
    T = nb * seq
    bq = bias.shape[1]
    nq = seq // bq

    def body(k_ref, v_ref, q_ref, do_ref, lse_ref, delta_ref, b_ref, dk_ref, dv_ref):
        kj = pl.program_id(2)
        k = k_ref[...]
        v = v_ref[...]

        def step(qi, carry):
            dk, dv = carry
            rows = pl.ds(pl.multiple_of(qi * bq, bq), bq)
            q = q_ref[rows, :]
            dob = do_ref[rows, :].astype(BF)
            p = jnp.exp(_dot(q, k, NT) + b_ref[qi - kj] - lse_ref[rows, :])
            dv = dv + _dot(p, dob, TN)
            ds = p * (_dot(dob, v, NT) - delta_ref[rows, :])
            return dk + _dot(ds, q, TN), dv

        z = jnp.zeros((bq, LANES), F32)
        dk, dv = lax.fori_loop(kj, nq, step, (z, z))
        dk_ref[...] = dk
        dv_ref[...] = dv

    stat = pl.BlockSpec((None, seq, 1), lambda b, h, i: (h, b, 0))
    out = pl.BlockSpec((bq, LANES), lambda b, h, i: (b * nq + i, h))
    sh = jax.ShapeDtypeStruct((T, nh * LANES), F32)
    return pl.pallas_call(
        body, grid=(nb, nh, nq),
        in_specs=[pl.BlockSpec((bq, LANES), lambda b, h, i: (b * nq + i, nh + h)),
                  pl.BlockSpec((bq, LANES), lambda b, h, i: (b * nq + i, 2 * nh + h)),
                  pl.BlockSpec((seq, LANES), lambda b, h, i: (b, h)),
                  pl.BlockSpec((seq, LANES), lambda b, h, i: (b, h)),
                  stat, stat,
                  pl.BlockSpec((nq, bq, bq), lambda b, h, i: (0, 0, 0))],
        out_specs=[out, out], out_shape=[sh, sh],
        name="attn_bwd_dkv", compiler_params=_cparams(3))(qkv, qkv, qkv, dcat, lse, delta, bias)


def _conv_parts(gc, xin, w_ref):
    w = [w_ref[k:k + 1, :] for k in range(3)]
    u = gc * xin
    row = lax.broadcasted_iota(jnp.int32, u.shape, 0)
    u1 = jnp.where(row >= 1, pltpu.roll(u, 1, 0), 0.0)
    u2 = jnp.where(row >= 2, pltpu.roll(u, 2, 0), 0.0)
    return u, u1, u2, w[0] * u2 + w[1] * u1 + w[2] * u, w, row


def _conv_fwd(proj, conv_w, cat, nb, seq, width):
    cw = min(2 * LANES, width)
    nc = width // cw

    def body(gb_ref, gc_ref, x_ref, w_ref, cat_ref, o_ref):
        _, _, _, conv, _, _ = _conv_parts(gc_ref[...], x_ref[...], w_ref)
        o_ref[...] = (gb_ref[...] * conv).astype(BF)

    def sec(k):
        return pl.BlockSpec((seq, cw), lambda b, c: (b, k * nc + c))

    return pl.pallas_call(
        body, grid=(nb, nc),
        in_specs=[sec(3), sec(4), sec(5), pl.BlockSpec((3, cw), lambda b, c: (0, c)),
                  pl.BlockSpec(memory_space=pl.ANY)],
        out_specs=pl.BlockSpec((seq, cw), lambda b, c: (b, nc + c)),
        out_shape=jax.ShapeDtypeStruct(cat.shape, BF), input_output_aliases={4: 0},
        name="conv_fwd", compiler_params=_cparams(2))(proj, proj, proj, conv_w, cat)


def _conv_bwd(proj, conv_w, dcat, nb, seq, width):
    cw = min(2 * LANES, width)
    nc = width // cw
    T = nb * seq

    def body(gb_ref, gc_ref, x_ref, w_ref, d_ref, dgb_ref, dgc_ref, dx_ref, dw_ref):
        gc = gc_ref[...]
        xin = x_ref[...]
        u, u1, u2, conv, w, row = _conv_parts(gc, xin, w_ref)
        dsc = d_ref[...]
        dgb_ref[...] = dsc * conv
        dconv = dsc * gb_ref[...]
        d1 = jnp.where(row < seq - 1, pltpu.roll(dconv, seq - 1, 0), 0.0)
        d2 = jnp.where(row < seq - 2, pltpu.roll(dconv, seq - 2, 0), 0.0)
        du = w[2] * dconv + w[1] * d1 + w[0] * d2
        dgc_ref[...] = du * xin
        dx_ref[...] = du * gc

        @pl.when(pl.program_id(1) == 0)
        def _():
            dw_ref[...] = jnp.zeros_like(dw_ref)

        dw_ref[0:1, :] += jnp.sum(dconv * u2, axis=0, keepdims=True)
        dw_ref[1:2, :] += jnp.sum(dconv * u1, axis=0, keepdims=True)
        dw_ref[2:3, :] += jnp.sum(dconv * u, axis=0, keepdims=True)

    def sec(k):
        return pl.BlockSpec((seq, cw), lambda c, b: (b, k * nc + c))

    out = pl.BlockSpec((seq, cw), lambda c, b: (b, c))
    wsp = pl.BlockSpec((3, cw), lambda c, b: (0, c))
    sh = jax.ShapeDtypeStruct((T, width), F32)
    return pl.pallas_call(
        body, grid=(nc, nb), in_specs=[sec(3), sec(4), sec(5), wsp, sec(1)],
        out_specs=[out, out, out, wsp], out_shape=[sh, sh, sh, jax.ShapeDtypeStruct((3, width), F32)],
        name="conv_bwd", compiler_params=_cparams(2))(proj, proj, proj, conv_w, dcat)


def _assemble_dproj(dq, dk, dv, dgb, dgc, dxin, cosf, sinf, seq):
    T, width = dq.shape
    nh = width // LANES
    bs = min(256, seq)
    nst = seq // bs
    scale = LANES ** -0.5

    def body(dq_ref, dk_ref, dv_ref, dgb_ref, dgc_ref, dx_ref, c_ref, s_ref, o_ref):
        sec = pl.program_id(1)
        c = c_ref[...]
        s = s_ref[...]

        def unrope(ref, mul):
            for h in range(nh):
                cols = slice(h * LANES, (h + 1) * LANES)
                t = ref[:, cols]
                o_ref[:, cols] = ((t * c + pltpu.roll(t * s, LANES // 2, 1)) * mul).astype(BF)

        @pl.when(sec == 0)
        def _():
            unrope(dq_ref, scale)

        @pl.when(sec == 1)
        def _():
            unrope(dk_ref, 1.0)

        for k, ref in ((2, dv_ref), (3, dgb_ref), (4, dgc_ref), (5, dx_ref)):
            @pl.when(sec == k)
            def _(ref=ref):
                o_ref[...] = ref[...].astype(BF)

    blk = pl.BlockSpec((bs, width), lambda r, k: (r, 0))
    tab = pl.BlockSpec((bs, LANES), lambda r, k: (r % nst, 0))
    return pl.pallas_call(
        body, grid=(T // bs, 6), in_specs=[blk] * 6 + [tab, tab],
        out_specs=pl.BlockSpec((bs, width), lambda r, k: (r, k)),
        out_shape=jax.ShapeDtypeStruct((T, 6 * width), BF), name="assemble_dproj",
        compiler_params=_cparams(2))(dq, dk, dv, dgb, dgc, dxin, cosf, sinf)


def _res_mm(name, a, w, h, comm=None):
    T, K = a.shape
    N = w.shape[1]
    bm = min(ROW_TILE, T)
    bk = min(ROW_TILE, K)
    row = pl.BlockSpec((bm, N), lambda i, j, k: (i, 0))

    def epi(accs, ex, outs):
        outs[0][...] = ex[0][...] + accs[0]

    return _mm(name, (T // bm, 1, K // bk),
               [(a, pl.BlockSpec((bm, bk), lambda i, j, k: (i, k))),
                (w, pl.BlockSpec((bk, N), lambda i, j, k: (k, 0)))],
               [(0, 1, NN, 0)], 1, (bm, N), [(h, row)],
               [(jax.ShapeDtypeStruct((T, N), F32), row)], epi, comm=comm, ncol=max(1, N // COL_TILE))[0]


def _mm_nt(name, a, w, out_dtype):
    T, K = a.shape
    N = w.shape[0]
    bm = min(ROW_TILE, T)
    bn = min(ROW_TILE, N)

    def epi(accs, ex, outs):
        outs[0][...] = accs[0].astype(out_dtype)

    return _mm(name, (T // bm, N // bn, 1),
               [(a, pl.BlockSpec((bm, K), lambda i, j, k: (i, 0))),
                (w, pl.BlockSpec((bn, K), lambda i, j, k: (j, 0)))],
               [(0, 1, NT, 0)], 1, None, [],
               [(jax.ShapeDtypeStruct((T, N), out_dtype), pl.BlockSpec((bm, bn), lambda i, j, k: (i, j)))],
               epi)[0]


def _mm_tn(name, a, bs_list):
    T, M = a.shape
    N = bs_list[0].shape[1]
    bk = min(ROW_TILE, T)
    bmr = min(ROW_TILE, M)
    n = len(bs_list)

    def epi(accs, ex, outs):
        for q in range(n):
            outs[q][...] = accs[q].astype(BF)

    ops = [(a, pl.BlockSpec((bk, bmr), lambda r, j, t: (t, r)))]
    ops += [(b, pl.BlockSpec((bk, N), lambda r, j, t: (t, 0))) for b in bs_list]
    return _mm(name, (M // bmr, 1, T // bk), ops, [(0, 1 + q, TN, q) for q in range(n)], n, (bmr, N), [],
               [(jax.ShapeDtypeStruct((M, N), BF), pl.BlockSpec((bmr, N), lambda r, j, t: (r, 0)))] * n, epi)


def _proj_bwd_x(dproj, wing):
    T = dproj.shape[0]
    _, D, ws = wing.shape
    bm = min(ROW_TILE, T)
    row = pl.BlockSpec((bm, D), lambda i, j, s: (i, 0))

    def epi(accs, ex, outs):
        outs[0][...] = accs[0]

    return _mm("proj_bwd_x", (T // bm, 1, N_DEV),
               [(dproj, pl.BlockSpec((bm, ws), lambda i, j, s: (i, s))),
                (wing, pl.BlockSpec((None, D, ws), lambda i, j, s: (s, 0, 0)))],
               [(0, 1, NT, 0)], 1, (bm, D), [], [(jax.ShapeDtypeStruct((T, D), F32), row)], epi,
               ncol=max(1, D // COL_TILE))[0]


def _proj_dw(u, dproj, ws):
    T, D = u.shape
    bk = min(ROW_TILE, T)

    def epi(accs, ex, outs):
        outs[0][...] = accs[0].astype(BF)

    return _mm("proj_dw", (N_DEV, 1, T // bk),
               [(u, pl.BlockSpec((bk, D), lambda s, j, t: (t, 0))),
                (dproj, pl.BlockSpec((bk, ws), lambda s, j, t: (t, s)))],
               [(0, 1, TN, 0)], 1, (D, ws), [],
               [(jax.ShapeDtypeStruct((N_DEV, D, ws), BF),
                 pl.BlockSpec((None, D, ws), lambda s, j, t: (s, 0, 0)))], epi)[0]


def _mixer_ab_fwd(h, gain, wing, conv_w, wout, tabs, nb, seq, comm_proj=None, comm_attn=None, comm_out=None):
    cosf, sinf, bias = tabs
    width = wing.shape[-1] * N_DEV // 6
    nh = width // LANES
    u = _rms_fwd(h, gain, BF)
    proj = _proj_fwd(u, wing, comm=comm_proj)
    qkv = _rope_fwd(proj, cosf, sinf, seq, nh)
    cat, lse = _attn_fwd(qkv, bias, nb, seq, nh, comm=comm_attn)
    cat = _conv_fwd(proj, conv_w, cat, nb, seq, width)
    return _res_mm("outproj_fwd", cat, wout, h, comm=comm_out), (h, u, proj, qkv, cat, lse)


def _mixer_ab_bwd(dh, dhb, saved, gain, wing, conv_w, wout, tabs, nb, seq, reduce_start, carry):
    cosf, sinf, bias = tabs
    h, u, proj, qkv, cat, lse = saved
    D = h.shape[1]
    ws = wing.shape[-1]
    width = ws * N_DEV // 6
    nh = width // LANES
    dcat = _mm_nt("outproj_bwd_x", dhb, wout, F32)
    dwout = _mm_tn("outproj_dw", cat, [dhb])[0]
    comm = _merge_comms(reduce_start(["ab_w_out"], [dwout.reshape(N_DEV, -1, D)]) + [carry])
    dq, delta = _attn_bwd_dq(qkv, cat, dcat, lse, bias, nb, seq, nh, comm=comm)
    dk, dv = _attn_bwd_dkv(qkv, dcat, lse, delta, bias, nb, seq, nh)
    dgb, dgc, dxin, dconvw = _conv_bwd(proj, conv_w, dcat, nb, seq, width)
    dproj = _assemble_dproj(dq, dk, dv, dgb, dgc, dxin, cosf, sinf, seq)
    du = _proj_bwd_x(dproj, wing)
    comm, = reduce_start(["ab_w_in"], [_proj_dw(u, dproj, ws)])
    dh_in, dhb_in, dgain = _rms_bwd(du, h, gain, dh)
    return dh_in, dhb_in, dgain, dconvw, comm


def _s5_zoh(lr, li, log_dt):
    dt = jnp.exp(log_dt)
    mag = jnp.exp(lr * dt)
    ar = mag * jnp.cos(li * dt)
    ai = mag * jnp.sin(li * dt)
    den = lr * lr + li * li
    return dt, ar, ai, den, ((ar - 1.0) * lr + ai * li) / den, (ai * lr - (ar - 1.0) * li) / den


def _s5_discretize(lam_re, lam_im, log_dt, bt_re, bt_im):
    def body(lr_ref, li_ref, ld_ref, br_ref, bi_ref, ar_ref, ai_ref, bbr_ref, bbi_ref):
        _, ar, ai, _, fr, fi = _s5_zoh(lr_ref[...], li_ref[...], ld_ref[...])
        ar_ref[...] = ar
        ai_ref[...] = ai
        bbr_ref[...] = fr * br_ref[...] - fi * bi_ref[...]
        bbi_ref[...] = fr * bi_ref[...] + fi * br_ref[...]

    small = jax.ShapeDtypeStruct(lam_re.shape, F32)
    big = jax.ShapeDtypeStruct(bt_re.shape, F32)
    return pl.pallas_call(body, out_shape=[small, small, big, big], name="s5_discretize",
                          compiler_params=_cparams(0))(lam_re, lam_im, log_dt, bt_re, bt_im)


def _s5_discretize_bwd(lam_re, lam_im, log_dt, bt_re, bt_im, d_ar, d_ai, d_bbr, d_bbi):

    def body(lr_ref, li_ref, ld_ref, br_ref, bi_ref, dar_ref, dai_ref, dbbr_ref, dbbi_ref,
             dlr_ref, dli_ref, dld_ref, dbr_ref, dbi_ref):
        lr, li = lr_ref[...], li_ref[...]
        dt, ar, ai, den, fr, fi = _s5_zoh(lr, li, ld_ref[...])
        br, bi = br_ref[...], bi_ref[...]
        dbbr, dbbi = dbbr_ref[...], dbbi_ref[...]
        dbr_ref[...] = dbbr * fr + dbbi * fi
        dbi_ref[...] = dbbi * fr - dbbr * fi
        dfr = jnp.sum(dbbr * br + dbbi * bi, axis=1, keepdims=True)
        dfi = jnp.sum(dbbi * br - dbbr * bi, axis=1, keepdims=True)
        dnr = dfr / den
        dni = dfi / den
        dden = -(dfr * fr + dfi * fi) / den
        dar = dar_ref[...] + dnr * lr - dni * li
        dai = dai_ref[...] + dnr * li + dni * lr
        dlr_ref[...] = dnr * (ar - 1.0) + dni * ai + 2.0 * dden * lr + dt * (dar * ar + dai * ai)
        dli_ref[...] = dnr * ai - dni * (ar - 1.0) + 2.0 * dden * li + dt * (dai * ar - dar * ai)
        ddt = jnp.sum(dar * (lr * ar - li * ai) + dai * (lr * ai + li * ar), axis=2, keepdims=True)
        dld_ref[...] = ddt * dt

    small = jax.ShapeDtypeStruct(lam_re.shape, F32)
    big = jax.ShapeDtypeStruct(bt_re.shape, F32)
    return pl.pallas_call(
        body, out_shape=[small, small, jax.ShapeDtypeStruct(log_dt.shape, F32), big, big],
        name="s5_discretize_bwd", compiler_params=_cparams(0))(
            lam_re, lam_im, log_dt, bt_re, bt_im, d_ar, d_ai, d_bbr, d_bbi)


def _rows8(t):
    return pl.ds(pl.multiple_of(t * SUBLANES, SUBLANES), SUBLANES)


def _cmul_add(ar, ai, sr, si, br, bi):
    return ar * sr - ai * si + br, ar * si + ai * sr + bi


def _cpow(ar, ai, n):
    rr = ri = None
    while n:
        if n & 1:
            rr, ri = (ar, ai) if rr is None else (rr * ar - ri * ai, rr * ai + ri * ar)
        ar, ai = ar * ar - ai * ai, 2.0 * ar * ai
        n >>= 1
    return rr, ri


def _s5_specs(R, nj):
    sh = STATE_COLS
    return dict(
        rows=pl.BlockSpec((R, LANES), lambda j: (0, j)),
        bd=pl.BlockSpec((None, LANES, sh), lambda j: (j, 0, 0)),
        cd=pl.BlockSpec((None, sh, LANES), lambda j: (j, 0, 0)),
        a=pl.BlockSpec((None, 1, sh), lambda j: (j, 0, 0)),
        vec=pl.BlockSpec((1, LANES), lambda j: (0, j)),
        init=pl.BlockSpec((None, SUBLANES, sh), lambda j: (j, 0, 0)))


def _s5_fwd(u, mats, seg_len, nseg, comm=None):
    bdr, bdi, cdr, cdi, are, aim, dsk = mats
    R, D = u.shape
    nj = D // LANES
    sh = STATE_COLS
    rc = min(R, 512)
    sp = _s5_specs(R, nj)

    def body(u_ref, bdr_ref, bdi_ref, cdr_ref, cdi_ref, ar_ref, ai_ref, d_ref,
             y_ref, yg_ref, ir_ref, ii_ref, sre, sim):
        ar = jnp.broadcast_to(ar_ref[...], (SUBLANES, sh))
        ai = jnp.broadcast_to(ai_ref[...], (SUBLANES, sh))

        def bu_chunk(c, _):
            rows = pl.ds(pl.multiple_of(c * rc, rc), rc)
            ub = u_ref[rows, :].astype(BF)
            sre[rows, :] = _dot(ub, bdr_ref[...], NN)
            sim[rows, :] = _dot(ub, bdi_ref[...], NN)
            return 0

        lax.fori_loop(0, R // rc, bu_chunk, 0)
        z = jnp.zeros((SUBLANES, sh), F32)

        def local_scan(t, c):
            return _cmul_add(ar, ai, c[0], c[1], sre[_rows8(t), :], sim[_rows8(t), :])

        er, ei = lax.fori_loop(0, seg_len, local_scan, (z, z), unroll=SCAN_UNROLL)
        pr, pi = _cpow(ar, ai, seg_len)
        first = (lax.broadcasted_iota(jnp.int32, (SUBLANES, sh), 0) & (nseg - 1)) == 0

        def prev(x):
            return jnp.where(first, 0.0, pltpu.roll(x, 1, 0))

        xr, xi = er, ei
        for _ in range(nseg - 1):
            xr, xi = _cmul_add(pr, pi, prev(xr), prev(xi), er, ei)
        i_r, i_i = prev(xr), prev(xi)
        ir_ref[...] = i_r
        ii_ref[...] = i_i

        def scan(t, c):
            nr, ni = _cmul_add(ar, ai, c[0], c[1], sre[_rows8(t), :], sim[_rows8(t), :])
            sre[_rows8(t), :] = nr
            sim[_rows8(t), :] = ni
            return nr, ni

        lax.fori_loop(0, seg_len, scan, (i_r, i_i), unroll=SCAN_UNROLL)

        def y_chunk(c, _):
            rows = pl.ds(pl.multiple_of(c * rc, rc), rc)
            y = _dot(sre[rows, :], cdr_ref[...], NN) + _dot(sim[rows, :], cdi_ref[...], NN)
            y = y + d_ref[...] * u_ref[rows, :]
            y_ref[rows, :] = y
            yg_ref[rows, :] = _gelu(y).astype(BF)
            return 0

        lax.fori_loop(0, R // rc, y_chunk, 0)

    init_sh = jax.ShapeDtypeStruct((nj, SUBLANES, STATE_COLS), F32)
    return _call(
        body, "s5_fwd", (nj,),
        [sp["rows"], sp["bd"], sp["bd"], sp["cd"], sp["cd"], sp["a"], sp["a"], sp["vec"]],
        [sp["rows"], sp["rows"], sp["init"], sp["init"]],
        [jax.ShapeDtypeStruct((R, D), F32), jax.ShapeDtypeStruct((R, D), BF), init_sh, init_sh],
        (u, bdr, bdi, cdr, cdi, are, aim, dsk),
        scratch=[pltpu.VMEM((R, sh), F32) for _ in range(2)], comm=comm)


def _s5_bwd(u, dy, mats, init_re, init_im, seg_len, nseg, comm=None):
    bdr, bdi, cdr, cdi, are, aim, dsk = mats
    R, D = u.shape
    nj = D // LANES
    sh = STATE_COLS
    rc = min(R, 512)
    sp = _s5_specs(R, nj)

    def body(u_ref, dy_ref, bdr_ref, bdi_ref, cdr_ref, cdi_ref, ar_ref, ai_ref, d_ref, ir_ref, ii_ref,
             du_ref, dbdr_ref, dbdi_ref, dcdr_ref, dcdi_ref, dar_ref, dai_ref, dd_ref,
             sre, sim, gre, gim):
        ar = jnp.broadcast_to(ar_ref[...], (SUBLANES, sh))
        ai = jnp.broadcast_to(ai_ref[...], (SUBLANES, sh))
        i_r, i_i = ir_ref[...], ii_ref[...]

        def chunk(c):
            return pl.ds(pl.multiple_of(c * rc, rc), rc)

        def bu_chunk(c, _):
            ub = u_ref[chunk(c), :].astype(BF)
            sre[chunk(c), :] = _dot(ub, bdr_ref[...], NN)
            sim[chunk(c), :] = _dot(ub, bdi_ref[...], NN)
            return 0

        lax.fori_loop(0, R // rc, bu_chunk, 0)

        def scan(t, c):
            nr, ni = _cmul_add(ar, ai, c[0], c[1], sre[_rows8(t), :], sim[_rows8(t), :])
            sre[_rows8(t), :] = nr
            sim[_rows8(t), :] = ni
            return nr, ni

        lax.fori_loop(0, seg_len, scan, (i_r, i_i), unroll=SCAN_UNROLL)

        def c_chunk(c, carry):
            dyb = dy_ref[chunk(c), :].astype(BF)
            gre[chunk(c), :] = _dot(dyb, cdr_ref[...], NT)
            gim[chunk(c), :] = _dot(dyb, cdi_ref[...], NT)
            return (carry[0] + _dot(sre[chunk(c), :], dyb, TN), carry[1] + _dot(sim[chunk(c), :], dyb, TN))

        zc = jnp.zeros((sh, LANES), F32)
        dcr, dci = lax.fori_loop(0, R // rc, c_chunk, (zc, zc))
        dcdr_ref[...] = dcr
        dcdi_ref[...] = dci

        def adj(t, gr_next, gi_next):
            return _cmul_add(ar, -ai, gr_next, gi_next, gre[_rows8(t), :], gim[_rows8(t), :])

        z = jnp.zeros((SUBLANES, sh), F32)
        fr, fi = lax.fori_loop(0, seg_len, lambda i, c: adj(seg_len - 1 - i, c[0], c[1]), (z, z),
                               unroll=SCAN_UNROLL)
        pr, pi = _cpow(ar, ai, seg_len)
        last =(lax.broadcasted_iota(jnp.int32, (SUBLANES, sh), 0) & (nseg - 1)) == nseg - 1

        def nxt(x):
            return jnp.where(last, 0.0, pltpu.roll(x, SUBLANES - 1, 0))

        xr, xi = fr, fi
        for _ in range(nseg - 1):
            xr, xi = _cmul_add(pr, -pi, nxt(xr), nxt(xi), fr, fi)
        g0r, g0i = nxt(xr), nxt(xi)

        def adj_scan(i, c):
            t = seg_len - 1 - i
            gr, gi = adj(t, c[0], c[1])
            gre[_rows8(t), :] = gr
            gim[_rows8(t), :] = gi
            spr, spi = sre[_rows8(t - 1), :], sim[_rows8(t - 1), :]
            return gr, gi, c[2] + spr * gr + spi * gi, c[3] + spr * gi - spi * gr

        gr, gi, dar, dai = lax.fori_loop(0, seg_len - 1, adj_scan, (g0r, g0i, z, z))
        gr, gi = adj(0, gr, gi)
        gre[_rows8(0), :] = gr
        gim[_rows8(0), :] = gi
        dar_ref[...] = jnp.sum(dar + i_r * gr + i_i * gi, axis=0, keepdims=True)
        dai_ref[...] = jnp.sum(dai + i_r * gi - i_i * gr, axis=0, keepdims=True)

        def d_chunk(c, carry):
            ub = u_ref[chunk(c), :].astype(BF)
            grb = gre[chunk(c), :].astype(BF)
            gib = gim[chunk(c), :].astype(BF)
            du = _dot(grb, bdr_ref[...], NT) + _dot(gib, bdi_ref[...], NT)
            du_ref[chunk(c), :] = du + d_ref[...] * dy_ref[chunk(c), :]
            dd = carry[2] + jnp.sum(dy_ref[chunk(c), :] * u_ref[chunk(c), :], axis=0, keepdims=True)
            return carry[0] + _dot(ub, grb, TN), carry[1] + _dot(ub, gib, TN), dd

        zb = jnp.zeros((LANES, sh), F32)
        dbr, dbi, dd = lax.fori_loop(0, R // rc, d_chunk, (zb, zb, jnp.zeros((1, LANES), F32)))
        dbdr_ref[...] = dbr
        dbdi_ref[...] = dbi
        dd_ref[...] = dd

    bd_sh = jax.ShapeDtypeStruct((nj, LANES, STATE_COLS), F32)
    cd_sh = jax.ShapeDtypeStruct((nj, STATE_COLS, LANES), F32)
    a_sh = jax.ShapeDtypeStruct((nj, 1, STATE_COLS), F32)
    return _call(
        body, "s5_bwd", (nj,),
        [sp["rows"], sp["rows"], sp["bd"], sp["bd"], sp["cd"], sp["cd"], sp["a"], sp["a"],
         sp["vec"], sp["init"], sp["init"]],
        [sp["rows"], sp["bd"], sp["bd"], sp["cd"], sp["cd"], sp["a"], sp["a"], sp["vec"]],
        [jax.ShapeDtypeStruct((R, D), F32), bd_sh, bd_sh, cd_sh, cd_sh, a_sh, a_sh,
         jax.ShapeDtypeStruct((1, D), F32)],
        (u, dy, bdr, bdi, cdr, cdi, are, aim, dsk, init_re, init_im),
        scratch=[pltpu.VMEM((R, sh), F32) for _ in range(4)], comm=comm)


def _glu_fwd(yg, wa, wb, h):
    T, D = yg.shape
    N = wa.shape[1]
    bm = min(ROW_TILE, T)
    bn = min(ROW_TILE, N)
    wspec = pl.BlockSpec((D, bn), lambda i, j, k: (0, j))
    ospec = pl.BlockSpec((bm, bn), lambda i, j, k: (i, j))

    def epi(accs, ex, outs):
        pa, pb = accs
        outs[0][...] = ex[0][...] + pa * _sig(pb)
        outs[1][...] = pa.astype(BF)
        outs[2][...] = pb.astype(BF)

    return _mm("glu_fwd", (T // bm, N // bn, 1),
               [(yg, pl.BlockSpec((bm, D), lambda i, j, k: (i, 0))), (wa, wspec), (wb, wspec)],
               [(0, 1, NN, 0), (0, 2, NN, 1)], 2, None, [(h, ospec)],
               [(jax.ShapeDtypeStruct((T, N), F32), ospec), (jax.ShapeDtypeStruct((T, N), BF), ospec),
                (jax.ShapeDtypeStruct((T, N), BF), ospec)], epi)


def _glu_bwd_gates(dz, pa, pb):
    T, D = dz.shape
    bm = min(ROW_TILE, T)

    def body(dz_ref, pa_ref, pb_ref, dpa_ref, dpb_ref):
        dz = dz_ref[...]
        sg = _sig(pb_ref[...].astype(F32))
        dpa_ref[...] = (dz * sg).astype(BF)
        dpb_ref[...] = (dz * pa_ref[...].astype(F32) * sg * (1.0 - sg)).astype(BF)

    row = pl.BlockSpec((bm, D), lambda i: (i, 0))
    return pl.pallas_call(
        body, grid=(T // bm,), in_specs=[row] * 3, out_specs=[row] * 2,
        out_shape=[jax.ShapeDtypeStruct((T, D), BF)] * 2, name="glu_bwd_gates",
        compiler_params=_cparams(1))(dz, pa, pb)


def _glu_bwd_y(dpa, dpb, wa, wb, y_pre, comm=None):
    T, N = dpa.shape
    D = wa.shape[0]
    bm = min(ROW_TILE, T)
    bn = min(ROW_TILE, D)
    aspec = pl.BlockSpec((bm, N), lambda i, j, k: (i, 0))
    wspec = pl.BlockSpec((bn, N), lambda i, j, k: (j, 0))
    ospec = pl.BlockSpec((bm, bn), lambda i, j, k: (i, j))

    def epi(accs, ex, outs):
        outs[0][...] = accs[0] * _gelu_grad(ex[0][...])

    return _mm("glu_bwd_y", (T // bm, D // bn, 1), [(dpa, aspec), (wa, wspec), (dpb, aspec), (wb, wspec)],
               [(0, 1, NT, 0), (2, 3, NT, 0)], 1, None, [(y_pre, ospec)],
               [(jax.ShapeDtypeStruct((T, D), F32), ospec)], epi, comm=comm)[0]


def _block_diag_in(x, nj):
    g = GROUPS_PER_BLOCK
    x = x.reshape(nj, g, 1, S5_GROUP, S5_STATE)
    eye = jnp.eye(g, dtype=bool)[None, :, :, None, None]
    full = jnp.where(eye, x, 0.0)
    return full.transpose(0, 1, 3, 2, 4).reshape(nj, g * S5_GROUP, g * S5_STATE)


def _block_diag_out(x, nj):
    return _block_diag_in(x, nj).transpose(0, 2, 1)


def _diag_of_in(m, nj):
    g = GROUPS_PER_BLOCK
    m5 = m.reshape(nj, g, S5_GROUP, g, S5_STATE)
    d = jnp.diagonal(m5, axis1=1, axis2=3)
    return d.transpose(0, 3, 1, 2).reshape(nj * g, S5_GROUP, S5_STATE)


def _mixer_s5_fwd(h, gain, p, dsk, wa, wb, nb, seq, comm_s5=None):
    T, D = h.shape
    nj = D // LANES
    nseg = SUBLANES // nb
    seg_len = seq // nseg
    G = p["s5_lambda_re"].shape[1]
    lam_re = p["s5_lambda_re"].reshape(G, 1, S5_STATE)
    lam_im = p["s5_lambda_im"].reshape(G, 1, S5_STATE)
    log_dt = p["s5_log_dt"].reshape(G, 1, 1)
    bt_re = p["s5_b_re"][0].transpose(0, 2, 1)
    bt_im = p["s5_b_im"][0].transpose(0, 2, 1)
    ar, ai, bbr, bbi = _s5_discretize(lam_re, lam_im, log_dt, bt_re, bt_im)
    mats = (_block_diag_in(bbr, nj).astype(BF), _block_diag_in(bbi, nj).astype(BF),
            _block_diag_out(p["s5_c_re"][0], nj).astype(BF),
            _block_diag_out(-p["s5_c_im"][0], nj).astype(BF),
            ar.reshape(nj, 1, STATE_COLS), ai.reshape(nj, 1, STATE_COLS), dsk)
    h_seg = _to_seg(h, seg_len)
    u = _rms_fwd(h_seg, gain, F32)
    y_pre, yg, init_re, init_im = _s5_fwd(u, mats, seg_len, nseg, comm=comm_s5)
    h_out, pa, pb = _glu_fwd(yg, wa, wb, h_seg)
    disc_in = (lam_re, lam_im, log_dt, bt_re, bt_im)
    return _to_tok(h_out, seg_len), (h_seg, u, mats, y_pre, yg, init_re, init_im, pa, pb, disc_in, seg_len, nseg)


def _mixer_s5_bwd(dh, saved, gain, wa, wb, reduce_start, carry):
    h_seg, u, mats, y_pre, yg, init_re, init_im, pa, pb, disc_in, seg_len, nseg = saved
    T, D = h_seg.shape
    nj = D // LANES
    G = nj * GROUPS_PER_BLOCK
    dh_seg = _to_seg(dh, seg_len)
    dpa, dpb = _glu_bwd_gates(dh_seg, pa, pb)
    dy = _glu_bwd_y(dpa, dpb, wa, wb, y_pre, comm=carry)
    dwa, dwb = _mm_tn("glu_dw", yg, [dpa, dpb])
    comm = _merge_comms(reduce_start(["s5_glu_wa", "s5_glu_wb"],
                                     [dwa.reshape(N_DEV, -1, D), dwb.reshape(N_DEV, -1, D)]))
    du, dbdr, dbdi, dcdr, dcdi, dar, dai, dd = _s5_bwd(u, dy, mats, init_re, init_im, seg_len, nseg, comm=comm)
    d_bbr = _diag_of_in(dbdr, nj)
    d_bbi = _diag_of_in(dbdi, nj)
    d_c_re = _diag_of_in(dcdr.transpose(0, 2, 1), nj)
    d_c_im = -_diag_of_in(dcdi.transpose(0, 2, 1), nj)
    dlr, dli, dld, dbr, dbi = _s5_discretize_bwd(
        *disc_in, dar.reshape(G, 1, S5_STATE), dai.reshape(G, 1, S5_STATE), d_bbr, d_bbi)
    small = {"s5_lambda_re": dlr.reshape(1, G, S5_STATE), "s5_lambda_im": dli.reshape(1, G, S5_STATE),
             "s5_log_dt": dld.reshape(1, G),
             "s5_b_re": dbr.transpose(0, 2, 1)[None], "s5_b_im": dbi.transpose(0, 2, 1)[None],
             "s5_c_re": d_c_re[None], "s5_c_im": d_c_im[None], "s5_d": dd}
    dh_in, _, dgain = _rms_bwd(du, h_seg, gain, dh_seg)
    dh_in = _to_tok(dh_in, seg_len)
    return dh_in, dh_in.astype(BF), dgain, small


def _mesh_pos():
    return lax.axis_index("x"), lax.axis_index("y"), lax.axis_index("c")


class _Gather:
    def __init__(self, srcs, slots, send_sems, recv_sems):
        self.srcs, self.slots, self.send_sems, self.recv_sems = srcs, slots, send_sems, recv_sems
        x, y, c = _mesh_pos()
        self.c = c
        self.me, self.sib = (x, y, c), (x, y, 1 - c)
        self.chips = [(1 - x, y), (x, 1 - y), (1 - x, 1 - y)]

    def copy(self, a, k, block, to, own=False):
        dst = self.slots[a].at[4 * block[0] + 2 * block[1] + block[2]]
        return pltpu.make_async_remote_copy(
            src_ref=self.srcs[a] if own else dst, dst_ref=dst, send_sem=self.send_sems.at[7 * a + k],
            recv_sem=self.recv_sems.at[7 * a + k], device_id=to, device_id_type=MESH)

    def own_copies(self, a):
        cps = [self.copy(a, 0, self.me, self.sib, own=True)]
        return cps + [self.copy(a, 1 + j, self.me, (*chip, self.c), own=True) for j, chip in enumerate(self.chips)]

    def start(self):
        for a in range(len(self.srcs)):
            for cp in self.own_copies(a):
                cp.start()

    def finish(self):
        n = len(self.srcs)
        for a in range(n):
            for j, chip in enumerate(self.chips):
                self.copy(a, 1 + j, (*chip, self.c), self.me).wait_recv()
                self.copy(a, 4 + j, (*chip, self.c), self.sib).start()
        for a in range(n):
            self.copy(a, 0, self.sib, self.me).wait_recv()
            for j, chip in enumerate(self.chips):
                self.copy(a, 4 + j, (*chip, 1 - self.c), self.me).wait_recv()
        for a in range(n):
            for cp in self.own_copies(a):
                cp.wait_send()
            for j, chip in enumerate(self.chips):
                self.copy(a, 4 + j, (*chip, self.c), self.sib).wait_send()


def _gather_comm(arrs):
    n = len(arrs)

    def local(xs, outs, sems, a):
        x, y, c = _mesh_pos()
        return pltpu.make_async_copy(xs[a], outs[a].at[4 * x + 2 * y + c], sems[2].at[a])

    def start(xs, outs, sems):
        for a in range(n):
            local(xs, outs, sems, a).start()
        _Gather(xs, outs, sems[0], sems[1]).start()

    def finish(xs, outs, sems):
        _Gather(xs, outs, sems[0], sems[1]).finish()
        for a in range(n):
            local(xs, outs, sems, a).wait()

    return _Comm(list(arrs), [jax.ShapeDtypeStruct((N_DEV,) + a.shape, a.dtype) for a in arrs],
                 [pltpu.SemaphoreType.DMA((7 * n,)), pltpu.SemaphoreType.DMA((7 * n,)),
                  pltpu.SemaphoreType.DMA((n,))], start, finish)


def _exchange_comm(parts):
    n = len(parts)

    def copies(ps, outs, sems):
        x, y, c = _mesh_pos()
        cps = []
        for a in range(n):
            for j in range(1, 4):
                to = (jnp.bitwise_xor(x, j // 2), jnp.bitwise_xor(y, j % 2), c)
                cps.append(pltpu.make_async_remote_copy(
                    src_ref=ps[a].at[j], dst_ref=outs[a].at[j - 1], send_sem=sems[0].at[3 * a + j - 1],
                    recv_sem=sems[1].at[3 * a + j - 1], device_id=to, device_id_type=MESH))
        return cps

    def start(ps, outs, sems):
        for cp in copies(ps, outs, sems):
            cp.start()

    def finish(ps, outs, sems):
        for cp in copies(ps, outs, sems):
            cp.wait()

    return _Comm(list(parts), [jax.ShapeDtypeStruct((3,) + p.shape[1:], p.dtype) for p in parts],
                 [pltpu.SemaphoreType.DMA((3 * n,)), pltpu.SemaphoreType.DMA((3 * n,))], start, finish)


def _run_comm(comm, name):
    ci, co = len(comm.ins), len(comm.outs)

    def body(*refs):
        comm.start(refs[:ci], refs[ci:ci + co], refs[ci + co:])
        comm.finish(refs[:ci], refs[ci:ci + co], refs[ci + co:])

    any_spec = pl.BlockSpec(memory_space=pl.ANY)
    comm.set_results(pl.pallas_call(
        body, in_specs=[any_spec] * ci, out_specs=[any_spec] * co, out_shape=list(comm.outs),
        scratch_shapes=list(comm.sems), name=name, compiler_params=_cparams(0))(*comm.ins))


def _pair_exchange(grads, name):
    n = len(grads)

    def body(*refs):
        gs, outs = refs[:n], refs[n:2 * n]
        send_sems, recv_sems = refs[2 * n:]
        x, y, c = _mesh_pos()
        copies = []
        for a in range(n):
            for k in range(4):
                copies.append(pltpu.make_async_remote_copy(
                    src_ref=gs[a].at[2 * k + 1 - c], dst_ref=outs[a].at[k], send_sem=send_sems.at[4 * a + k],
                    recv_sem=recv_sems.at[4 * a + k], device_id=(x, y, 1 - c), device_id_type=MESH))
        for cp in copies:
            cp.start()
        for cp in copies:
            cp.wait()

    any_spec = pl.BlockSpec(memory_space=pl.ANY)
    return pl.pallas_call(
        body, in_specs=[any_spec] * n, out_specs=[any_spec] * n,
        out_shape=[jax.ShapeDtypeStruct((4,) + g.shape[1:], g.dtype) for g in grads],
        scratch_shapes=[pltpu.SemaphoreType.DMA((4 * n,)), pltpu.SemaphoreType.DMA((4 * n,))],
        name=name, compiler_params=_cparams(0))(*grads)


def _pair_sum(grad, recv, pos):
    _, R, C = grad.shape
    br = _row_block(R, C, PAIR_SUM_ELEMS)

    def body(pos_ref, g_ref, r_ref, o_ref):
        o_ref[...] = (g_ref[...].astype(F32) + r_ref[...].astype(F32)).astype(BF)

    def chip(j, p):
        return jnp.bitwise_xor(p[1], j)

    return pl.pallas_call(
        body, grid_spec=pltpu.PrefetchScalarGridSpec(
            num_scalar_prefetch=1, grid=(4, R // br),
            in_specs=[pl.BlockSpec((None, br, C), lambda j, i, p: (2 * chip(j, p) + p[0], i, 0)),
                      pl.BlockSpec((None, br, C), lambda j, i, p: (chip(j, p), i, 0))],
            out_specs=pl.BlockSpec((None, br, C), lambda j, i, p: (j, i, 0))),
        out_shape=jax.ShapeDtypeStruct((4, R, C), BF), name="pair_sum", compiler_params=_cparams(2))(pos, grad, recv)


def _adamw(w, g, m, v):
    m = ADAM_B1 * m + (1.0 - ADAM_B1) * g
    v = ADAM_B2 * v + (1.0 - ADAM_B2) * (g * g)
    m_hat = m / (1.0 - ADAM_B1 ** ADAM_STEP)
    v_hat = v / (1.0 - ADAM_B2 ** ADAM_STEP)
    return -ADAM_LR * (m_hat / (jnp.sqrt(v_hat) + ADAM_EPS) + ADAM_WD * w), m, v


def _adamw_piece(w, m, v, piece, part, recv, bufs):
    _, R, C = w.shape
    br = _row_block(R, C)

    def body(w_ref, m_ref, v_ref, p_ref, r_ref, b0, b1, b2, b3, g_ref, d_ref, nm_ref, nv_ref):
        g = p_ref[...].astype(F32)
        for j in range(3):
            g = g + r_ref[j].astype(F32)
        d, nm, nv = _adamw(w_ref[...], g, m_ref[...], v_ref[...])
        g_ref[...] = g
        d_ref[...] = d
        nm_ref[...] = nm
        nv_ref[...] = nv

    row = pl.BlockSpec((None, br, C), lambda i: (piece, i, 0))
    any_spec = pl.BlockSpec(memory_space=pl.ANY)
    return pl.pallas_call(
        body, grid=(R // br,),
        in_specs=[row, row, row, pl.BlockSpec((None, br, C), lambda i: (0, i, 0)),
                  pl.BlockSpec((3, br, C), lambda i: (0, i, 0))] + [any_spec] * 4,
        out_specs=[row] * 4, out_shape=[jax.ShapeDtypeStruct(w.shape, F32)] * 4,
        input_output_aliases={5: 0, 6: 1, 7: 2, 8: 3}, name="adamw_piece",
        compiler_params=_cparams(1))(w, m, v, part, recv, *bufs)


def _all_reduce_small(x):
    rows = x.shape[0]

    def body(x_ref, o_ref, buf, send_sems, recv_sems):
        xp, yp, cp = _mesh_pos()
        buf[4 * xp + 2 * yp + cp] = x_ref[...]
        gather = _Gather([x_ref], [buf], send_sems, recv_sems)
        gather.start()
        gather.finish()
        acc = buf[0]
        for d in range(1, N_DEV):
            acc = acc + buf[d]
        o_ref[...] = acc

    vm = pl.BlockSpec(memory_space=pltpu.VMEM)
    return pl.pallas_call(
        body, in_specs=[vm], out_specs=vm, out_shape=jax.ShapeDtypeStruct(x.shape, F32),
        scratch_shapes=[pltpu.VMEM((N_DEV, rows, LANES), F32), pltpu.SemaphoreType.DMA((7,)),
                        pltpu.SemaphoreType.DMA((7,))],
        name="all_reduce_small", compiler_params=_cparams(0))(x)


def _adamw_small(w, g, m, v):
    def body(w_ref, g_ref, m_ref, v_ref, d_ref, nm_ref, nv_ref):
        d, nm, nv = _adamw(w_ref[...], g_ref[...], m_ref[...], v_ref[...])
        d_ref[...] = d
        nm_ref[...] = nm
        nv_ref[...] = nv

    sh = jax.ShapeDtypeStruct(w.shape, F32)
    return pl.pallas_call(body, out_shape=[sh] * 3, name="adamw_small", compiler_params=_cparams(0))(w, g, m, v)


def _pack(arrs):
    flat = jnp.concatenate([a.reshape(-1).astype(F32) for a in arrs])
    rows = -(-flat.shape[0] // (SUBLANES * LANES)) * SUBLANES
    return jnp.pad(flat, (0, rows * LANES - flat.shape[0])).reshape(rows, LANES)


def _unpack(buf, shapes):
    flat = buf.reshape(-1)
    out, off = [], 0
    for s in shapes:
        n = 1
        for d in s:
            n *= d
        out.append(flat[off:off + n].reshape(s))
        off += n
    return out


BIG = ("ffn_w1", "ffn_w3", "ffn_w2", "ab_w_in", "ab_w_out", "s5_glu_wa", "s5_glu_wb")
NAMES = ("ln_ffn_pre", "ln_mix", "ln_ffn_post", "ln_final", "ffn_w1", "ffn_w3", "ffn_w2", "ab_w_in",
         "ab_conv_w", "ab_w_out", "s5_lambda_re", "s5_lambda_im", "s5_log_dt", "s5_b_re", "s5_b_im",
         "s5_c_re", "s5_c_im", "s5_d", "s5_glu_wa", "s5_glu_wb")


def kernel(x, ln_ffn_pre, ln_mix, ln_ffn_post, ln_final, ffn_w1, ffn_w3, ffn_w2, ab_w_in, ab_conv_w, ab_w_out, s5_lambda_re, s5_lambda_im, s5_log_dt, s5_b_re, s5_b_im, s5_c_re, s5_c_im, s5_d, s5_glu_wa, s5_glu_wb, loss_target, m_ln_ffn_pre, m_ln_mix, m_ln_ffn_post, m_ln_final, m_ffn_w1, m_ffn_w3, m_ffn_w2, m_ab_w_in, m_ab_conv_w, m_ab_w_out, m_s5_lambda_re, m_s5_lambda_im, m_s5_log_dt, m_s5_b_re, m_s5_b_im, m_s5_c_re, m_s5_c_im, m_s5_d, m_s5_glu_wa, m_s5_glu_wb, v_ln_ffn_pre, v_ln_mix, v_ln_ffn_post, v_ln_final, v_ffn_w1, v_ffn_w3, v_ffn_w2, v_ab_w_in, v_ab_conv_w, v_ab_w_out, v_s5_lambda_re, v_s5_lambda_im, v_s5_log_dt, v_s5_b_re, v_s5_b_im, v_s5_c_re, v_s5_c_im, v_s5_d, v_s5_glu_wa, v_s5_glu_wb):
    w = dict(zip(NAMES, (ln_ffn_pre, ln_mix, ln_ffn_post, ln_final, ffn_w1, ffn_w3, ffn_w2, ab_w_in, ab_conv_w,
                         ab_w_out, s5_lambda_re, s5_lambda_im, s5_log_dt, s5_b_re, s5_b_im, s5_c_re, s5_c_im,
                         s5_d, s5_glu_wa, s5_glu_wb)))
    mom = dict(zip(NAMES, (m_ln_ffn_pre, m_ln_mix, m_ln_ffn_post, m_ln_final, m_ffn_w1, m_ffn_w3, m_ffn_w2,
                           m_ab_w_in, m_ab_conv_w, m_ab_w_out, m_s5_lambda_re, m_s5_lambda_im, m_s5_log_dt,
                           m_s5_b_re, m_s5_b_im, m_s5_c_re, m_s5_c_im, m_s5_d, m_s5_glu_wa, m_s5_glu_wb)))
    var = dict(zip(NAMES, (v_ln_ffn_pre, v_ln_mix, v_ln_ffn_post, v_ln_final, v_ffn_w1, v_ffn_w3, v_ffn_w2,
                           v_ab_w_in, v_ab_conv_w, v_ab_w_out, v_s5_lambda_re, v_s5_lambda_im, v_s5_log_dt,
                           v_s5_b_re, v_s5_b_im, v_s5_c_re, v_s5_c_im, v_s5_d, v_s5_glu_wa, v_s5_glu_wb)))
    nb, seq, D = x.shape
    T = nb * seq
    assert ln_mix.shape[0] == 2 and ab_w_in.shape[0] == 1 and s5_glu_wa.shape[0] == 1
    xc, yc, cc = _mesh_pos()
    dev = 4 * xc + 2 * yc + cc
    pos = jnp.stack([cc, 2 * xc + yc]).astype(jnp.int32)
    bq = min(ATTN_TILE, seq)
    tabs =_rope_tables(seq) + (_branch_bias(seq // bq, bq),)

    def ffn_piece(k, li, fj):
        return w[k][li, fj].astype(BF)

    g0 = _gather_comm([ffn_piece("ffn_w1", 0, 0), ffn_piece("ffn_w3", 0, 0), ab_conv_w[0], s5_d])
    _run_comm(g0, "gather_first")
    w1, w3 = {(0, 0): g0.results[0]}, {(0, 0): g0.results[1]}
    w2 = {}
    conv_w = g0.results[2].transpose(1, 0, 2).reshape(3, -1)
    dsk = g0.results[3].reshape(1, D)
    gains = {k: [w[k][i:i + 1] for i in range(2)] for k in ("ln_ffn_pre", "ln_mix", "ln_ffn_post")}

    h = x.reshape(T, D)
    saved = {}

    def ffn_fwd(h, gain, key, tag, comm_up, comm_down, after_up):
        n = _rms_fwd(h, gain, BF)
        t1, t3, g = _ffn_up(n, w1[key], w3[key], comm=comm_up)
        after_up()
        saved[tag] = (h, n, t1, t3, g)
        return _ffn_down(g, w2[key], h, comm=comm_down)

    c_up = _gather_comm([ffn_piece("ffn_w2", 0, 0), ab_w_out[0].astype(BF)])
    c_dn = _gather_comm([ab_w_in[0].astype(BF)])
    h = ffn_fwd(h, gains["ln_ffn_pre"][0], (0, 0), "pre0", c_up, c_dn,
                lambda: w2.update({(0, 0): c_up.results[0]}))
    wout = c_up.results[1].reshape(-1, D)
    wing = c_dn.results[0]
    c_proj = _gather_comm([ffn_piece("ffn_w1", 0, 1)])
    c_attn = _gather_comm([ffn_piece("ffn_w3", 0, 1), s5_glu_wa[0].astype(BF)])
    c_out = _gather_comm([s5_glu_wb[0].astype(BF)])
    h, saved["mix0"] = _mixer_ab_fwd(h, gains["ln_mix"][0], wing, conv_w, wout, tabs, nb, seq, c_proj, c_attn, c_out)
    w1[(0, 1)] = c_proj.results[0]
    w3[(0, 1)] = c_attn.results[0]
    wa = c_attn.results[1].reshape(-1, D)
    wb = c_out.results[0].reshape(-1, D)
    c_up2 = _gather_comm([ffn_piece("ffn_w2", 0, 1), ffn_piece("ffn_w1", 1, 0)])
    c_dn = _gather_comm([ffn_piece("ffn_w3", 1, 0)])
    h = ffn_fwd(h, gains["ln_ffn_post"][0], (0, 1), "post0", c_up2, c_dn,
                lambda: w2.update({(0, 1): c_up2.results[0]}))
    w1[(1, 0)] = c_up2.results[1]
    w3[(1, 0)] = c_dn.results[0]
    c_up3 = _gather_comm([ffn_piece("ffn_w2", 1, 0), ffn_piece("ffn_w1", 1, 1)])
    c_dn = _gather_comm([ffn_piece("ffn_w3", 1, 1)])
    h = ffn_fwd(h, gains["ln_ffn_pre"][1], (1, 0), "pre1", c_up3, c_dn,
                lambda: w2.update({(1, 0): c_up3.results[0]}))
    w1[(1, 1)] = c_up3.results[1]
    w3[(1, 1)] = c_dn.results[0]
    c_s5 = _gather_comm([ffn_piece("ffn_w2", 1, 1)])
    h, saved["mix1"] = _mixer_s5_fwd(h, gains["ln_mix"][1], w, dsk, wa, wb, nb, seq, c_s5)
    w2[(1, 1)] = c_s5.results[0]
    h = ffn_fwd(h, gains["ln_ffn_post"][1], (1, 1), "post1", None, None, lambda: None)
    dh, dhb, d_ln_final, loss_part = _loss_head(h, ln_final.reshape(1, D), loss_target.reshape(T, D))
    loss = lax.psum(loss_part[0, 0], ("x", "y", "c"))

    reduced = {}

    def reduce_start(names, grads):
        recv = _pair_exchange(grads, "pair_exchange")
        comms = []
        for nm, g, r in zip(names, grads, recv):
            part = _pair_sum(g, r, pos)
            comms.append(_exchange_comm([part]))
            reduced[nm] = (part, comms[-1])
        return comms

    def ffn_bwd(dh, dhb, key, tag, gain, carry, is_last=False):
        h_in, n, t1, t3, g = saved[tag]
        da1, da3 = _ffn_bwd_hidden(dhb, w2[key], t1, t3, comm=carry)
        c2, = reduce_start([("ffn_w2",) + key], [_ffn_dw2(g, dhb)])
        dw1, dw3 = _ffn_dw13(n, da1, da3, comm=c2)
        c1, c3 = reduce_start([("ffn_w1",) + key, ("ffn_w3",) + key], [dw1, dw3])
        dn = _ffn_dn(da1, da3, w1[key], w3[key], comm=_merge_comms([c1, c3]) if is_last else c1)
        return list(_rms_bwd(dn, h_in, gain, dh)) + [None if is_last else c3]

    g_small = {"ln_final": d_ln_final.reshape(D)}
    g_ln = {k: [None, None] for k in gains}
    dh, dhb, g_ln["ln_ffn_post"][1], carry = ffn_bwd(dh, dhb, (1, 1), "post1", gains["ln_ffn_post"][1], None)
    dh, dhb, g_ln["ln_mix"][1], s5_small = _mixer_s5_bwd(
        dh, saved["mix1"], gains["ln_mix"][1], wa, wb, reduce_start, carry)
    g_small.update(s5_small)
    dh, dhb, g_ln["ln_ffn_pre"][1], carry = ffn_bwd(dh, dhb, (1, 0), "pre1", gains["ln_ffn_pre"][1], None)
    dh, dhb, g_ln["ln_ffn_post"][0], carry = ffn_bwd(dh, dhb, (0, 1), "post0", gains["ln_ffn_post"][0], carry)
    dh, dhb, g_ln["ln_mix"][0], g_small["ab_conv_w"], carry = _mixer_ab_bwd(
        dh, dhb, saved["mix0"], gains["ln_mix"][0], wing, conv_w, wout, tabs, nb, seq, reduce_start, carry)
    dh, dhb, g_ln["ln_ffn_pre"][0], _ = ffn_bwd(dh, dhb, (0, 0), "pre0", gains["ln_ffn_pre"][0], carry, is_last=True)
    grad_x = dh.reshape(nb, seq, D)
    for k in g_ln:
        g_small[k] = jnp.concatenate(g_ln[k], axis=0)

    out = {}
    for k in BIG:
        cols = w[k].shape[-1]
        pieces = [(li, fj) for li in range(2) for fj in range(2)] if w[k].ndim == 4 else [None]
        view = (len(pieces), -1, cols)
        w3d, m3d, v3d = w[k].reshape(view), mom[k].reshape(view), var[k].reshape(view)
        bufs = [lax.empty(w3d.shape, F32) for _ in range(4)]
        for q, key in enumerate(pieces):
            part, comm = reduced[k if key is None else (k,) + key]
            bufs = _adamw_piece(w3d, m3d, v3d, q, part, comm.results[0], bufs)
        out[k] = [t.reshape(w[k].shape) for t in bufs]

    small_names = [k for k in NAMES if k not in BIG]
    red = _unpack(_all_reduce_small(_pack([g_small[k] for k in small_names])),
                  [g_small[k].shape for k in small_names])
    g_red = dict(zip(small_names, red))
    cw = w["ab_conv_w"].shape[-1]
    g_red["ab_conv_w"] = lax.dynamic_slice_in_dim(g_red["ab_conv_w"], dev * cw, cw, axis=1)[None]
    dsz = w["s5_d"].shape[-1]
    g_red["s5_d"] = lax.dynamic_slice_in_dim(g_red["s5_d"].reshape(1, -1), dev * dsz, dsz, axis=1)
    shapes = [w[k].shape for k in small_names]
    g_red = {k: g_red[k].reshape(w[k].shape) for k in small_names}
    d_s, m_s, v_s = _adamw_small(_pack([w[k] for k in small_names]), _pack([g_red[k] for k in small_names]),
                                 _pack([mom[k] for k in small_names]), _pack([var[k] for k in small_names]))
    for k, d, nm, nv in zip(small_names, _unpack(d_s, shapes), _unpack(m_s, shapes), _unpack(v_s, shapes)):
        out[k] = [g_red[k], d, nm, nv]

    return (loss, grad_x, *[out[k][0] for k in NAMES], *[out[k][1] for k in NAMES],
            *[out[k][2] for k in NAMES], *[out[k][3] for k in NAMES])
```

```python
import jax
import jax.numpy as jnp
from jax import lax
from jax.experimental import pallas as pl
from jax.experimental.pallas import tpu as pltpu

F32, BF = jnp.float32, jnp.bfloat16
N_DEV = 8
MESH = pl.DeviceIdType.MESH
LANES = 128
SUBLANES = 8
VMEM_LIMIT = 56 * 2 ** 20
ROW_TILE = 512
FFN_ROW_TILE = 1024
COL_TILE = 512
ATTN_TILE = 512
SCAN_UNROLL = 4
ELEMS_PER_BLOCK = 256 * 1024
PAIR_SUM_ELEMS = 2048 * 1024
RMS_EPS = 1e-6
ROPE_THETA = 10000.0
NEG_INF = -1e30
S5_STATE = 64
S5_GROUP = 16
GROUPS_PER_BLOCK = LANES // S5_GROUP
STATE_COLS = GROUPS_PER_BLOCK * S5_STATE
DILATED_PATTERN = ((128, 1), (512, 4), (2048, 16))
ADAM_LR, ADAM_B1, ADAM_B2, ADAM_EPS, ADAM_WD, ADAM_STEP = 0.001, 0.9, 0.999, 1e-08, 0.01, 10
GELU_C = 0.7978845608028654
GELU_A = 0.044715


def _cparams(n_grid, vmem=VMEM_LIMIT):
    sem = ("arbitrary",) * n_grid if n_grid else None
    return pltpu.CompilerParams(dimension_semantics=sem, vmem_limit_bytes=vmem)


def _sig(x):
    return 1.0 / (1.0 + jnp.exp(-x))


def _gelu(x):
    return 0.5 * x * (1.0 + jnp.tanh(GELU_C * (x + GELU_A * x * x * x)))


def _gelu_grad(x):
    t = jnp.tanh(GELU_C * (x + GELU_A * x * x * x))
    return 0.5 * (1.0 + t) + 0.5 * x * (1.0 - t * t) * GELU_C * (1.0 + 3.0 * GELU_A * x * x)


def _dot(a, b, dims):
    a = a if a.dtype == BF else a.astype(BF)
    b = b if b.dtype == BF else b.astype(BF)
    return lax.dot_general(a, b, (dims, ((), ())), preferred_element_type=F32)


NN = ((1,), (0,))
NT = ((1,), (1,))
TN = ((0,), (0,))


def _row_block(rows, cols, elems=ELEMS_PER_BLOCK, mult=16):
    cap = max(mult, elems // cols)
    best = None
    for b in range(mult, min(rows, cap) + 1, mult):
        if rows % b == 0:
            best = b
    return rows if best is None else best


class _Comm:
    def __init__(self, ins, outs, sems, start, finish, members=()):
        self.ins, self.outs, self.sems, self.start, self.finish = ins, outs, sems, start, finish
        self.members = members
        self.results = None

    def set_results(self, res):
        self.results = list(res)
        off = 0
        for m in self.members:
            m.set_results(res[off:off + len(m.outs)])
            off += len(m.outs)


def _merge_comms(comms):
    comms = [c for c in comms if c is not None]
    if len(comms) < 2:
        return comms[0] if comms else None

    def each(fn_name, ins, outs, sems):
        i = o = s = 0
        for c in comms:
            ni, no, ns = len(c.ins), len(c.outs), len(c.sems)
            getattr(c, fn_name)(ins[i:i + ni], outs[o:o + no], sems[s:s + ns])
            i, o, s = i + ni, o + no, s + ns

    return _Comm([a for c in comms for a in c.ins], [a for c in comms for a in c.outs],
                 [a for c in comms for a in c.sems],
                 lambda ins, outs, sems: each("start", ins, outs, sems),
                 lambda ins, outs, sems: each("finish", ins, outs, sems), members=tuple(comms))


def _call(body, name, grid, in_specs, out_specs, out_shape, args, scratch=(), comm=None):
    in_specs, out_specs, out_shape, scratch = list(in_specs), list(out_specs), list(out_shape), list(scratch)
    if comm is None:
        return pl.pallas_call(body, grid=grid, in_specs=in_specs, out_specs=out_specs, out_shape=out_shape,
                              scratch_shapes=scratch, name=name, compiler_params=_cparams(len(grid)))(*args)
    n_in, n_out, n_sc = len(in_specs), len(out_specs), len(scratch)
    ci, co = len(comm.ins), len(comm.outs)

    def hosted(*refs):
        ins, refs = refs[:n_in], refs[n_in:]
        cins, refs = refs[:ci], refs[ci:]
        outs, refs = refs[:n_out], refs[n_out:]
        couts, refs = refs[:co], refs[co:]
        sc, csems = refs[:n_sc], refs[n_sc:]
        first = last = None
        for d, n in enumerate(grid):
            p = pl.program_id(d)
            first = (p == 0) if first is None else first & (p == 0)
            last = (p == n - 1) if last is None else last & (p == n - 1)

        @pl.when(first)
        def _():
            comm.start(cins, couts, csems)

        body(*ins, *outs, *sc)

        @pl.when(last)
        def _():
            comm.finish(cins, couts, csems)

    any_spec = pl.BlockSpec(memory_space=pl.ANY)
    res = pl.pallas_call(
        hosted, grid=grid, in_specs=in_specs + [any_spec] * ci, out_specs=out_specs + [any_spec] * co,
        out_shape=out_shape + list(comm.outs), scratch_shapes=scratch + list(comm.sems), name=name,
        compiler_params=_cparams(len(grid)))(*args, *comm.ins)
    comm.set_results(res[n_out:])
    return list(res[:n_out])


def _mm(name, grid, operands, pairs, n_acc, acc_shape, extras, outs, epilogue, comm=None, nrow=1, ncol=1):
    nk = grid[2]
    n_op, n_ex, n_out = len(operands), len(extras), len(outs)

    def part_of(ref, dim, t, n):
        if n == 1:
            return ref
        size = ref.shape[dim] // n
        idx = [slice(None)] * len(ref.shape)
        idx[dim] = pl.ds(t * size, size)
        return ref.at[tuple(idx)]

    def tile_of(ref, r, c):
        return part_of(part_of(ref, 0, r, nrow), 1, c, ncol)

    def products(op, r, c):
        parts = [None] * n_acc
        for ai, bi, dims, ci in pairs:
            a = part_of(op[ai], 1 - dims[0][0], r, nrow)
            b = part_of(op[bi], 1 - dims[1][0], c, ncol)
            d = _dot(a[...], b[...], dims)
            parts[ci] = d if parts[ci] is None else parts[ci] + d
        return parts

    def body(*refs):
        op = refs[:n_op]
        ex = refs[n_op:n_op + n_ex]
        out = refs[n_op + n_ex:n_op + n_ex + n_out]
        acc = refs[n_op + n_ex + n_out:]
        tiles = [(r, c) for r in range(nrow) for c in range(ncol)]

        def views(refs_, t):
            return [tile_of(q, *t) for q in refs_]

        if nk == 1:
            parts = products(op, *tiles[0])
            for q, t in enumerate(tiles):
                nxt = products(op, *tiles[q + 1]) if q + 1 < len(tiles) else None
                epilogue(parts, views(ex, t), views(out, t))
                parts = nxt
            return
        k = pl.program_id(2)

        @pl.when(k == 0)
        def _():
            for q in acc:
                q[...] = jnp.zeros_like(q)

        for t in tiles:
            parts = products(op, *t)
            for q, p in zip(views(acc, t), parts):
                q[...] += p

        @pl.when(k == nk - 1)
        def _():
            for t in tiles:
                epilogue([q[...] for q in views(acc, t)], views(ex, t), views(out, t))

    return _call(body, name, grid, [s for _, s in operands] + [s for _, s in extras], [s for _, s in outs],
                 [sh for sh, _ in outs], [a for a, _ in operands] + [a for a, _ in extras],
                 scratch=[pltpu.VMEM(acc_shape, F32) for _ in range(n_acc if nk > 1 else 0)], comm=comm)


def _to_seg(a, seg_len):
    T, D = a.shape
    return a.reshape(SUBLANES, seg_len, D).transpose(1, 0, 2).reshape(T, D)


def _to_tok(a, seg_len):
    T, D = a.shape
    return a.reshape(seg_len, SUBLANES, D).transpose(1, 0, 2).reshape(T, D)


def _rms_fwd(h, gain, out_dtype):
    T, D = h.shape
    bm = min(ROW_TILE, T)

    def body(h_ref, g_ref, o_ref):
        x = h_ref[...]
        r = lax.rsqrt(jnp.mean(x * x, axis=-1, keepdims=True) + RMS_EPS)
        o_ref[...] = (x * r * g_ref[...]).astype(out_dtype)

    row = pl.BlockSpec((bm, D), lambda i: (i, 0))
    return pl.pallas_call(
        body, grid=(T // bm,), in_specs=[row, pl.BlockSpec((1, D), lambda i: (0, 0))],
        out_specs=row, out_shape=jax.ShapeDtypeStruct((T, D), out_dtype), name="rms_fwd",
        compiler_params=_cparams(1))(h, gain)


def _rms_bwd_rows(dn, x, g):
    r = lax.rsqrt(jnp.mean(x * x, axis=-1, keepdims=True) + RMS_EPS)
    xh = x * r
    dng = dn * g
    dx = r * (dng - xh * jnp.mean(dng * xh, axis=-1, keepdims=True))
    return dx, jnp.sum(dn * xh, axis=0, keepdims=True)


def _rms_bwd(dn, h, gain, dh_up):
    T, D = h.shape
    bm = min(ROW_TILE, T)

    def body(dn_ref, h_ref, g_ref, up_ref, dh_ref, dhb_ref, dg_ref):
        dx, dg = _rms_bwd_rows(dn_ref[...], h_ref[...], g_ref[...])
        dh = up_ref[...] + dx
        dh_ref[...] = dh
        dhb_ref[...] = dh.astype(BF)

        @pl.when(pl.program_id(0) == 0)
        def _():
            dg_ref[...] = jnp.zeros_like(dg_ref)

        dg_ref[...] += dg

    row = pl.BlockSpec((bm, D), lambda i: (i, 0))
    vec = pl.BlockSpec((1, D), lambda i: (0, 0))
    return pl.pallas_call(
        body, grid=(T // bm,), in_specs=[row, row, vec, row], out_specs=[row, row, vec],
        out_shape=[jax.ShapeDtypeStruct((T, D), F32), jax.ShapeDtypeStruct((T, D), BF),
                   jax.ShapeDtypeStruct((1, D), F32)],
        name="rms_bwd", compiler_params=_cparams(1))(dn, h, gain, dh_up)


def _loss_head(h, gain, target):
    T, D = h.shape
    bm = min(ROW_TILE, T)

    def body(h_ref, g_ref, t_ref, dh_ref, dhb_ref, dg_ref, loss_ref):
        x = h_ref[...]
        g = g_ref[...]
        r = lax.rsqrt(jnp.mean(x * x, axis=-1, keepdims=True) + RMS_EPS)
        err = x * r * g - t_ref[...]
        part = 0.5 * jnp.sum(jnp.sum(err * err, axis=-1, keepdims=True), axis=0, keepdims=True) / D
        dx, dg = _rms_bwd_rows(err / D, x, g)
        dh_ref[...] = dx
        dhb_ref[...] = dx.astype(BF)

        @pl.when(pl.program_id(0) == 0)
        def _():
            dg_ref[...] = jnp.zeros_like(dg_ref)
            loss_ref[...] = jnp.zeros_like(loss_ref)

        dg_ref[...] += dg
        loss_ref[...] += jnp.broadcast_to(part, loss_ref.shape)

    row = pl.BlockSpec((bm, D), lambda i: (i, 0))
    vec = pl.BlockSpec((1, D), lambda i: (0, 0))
    return pl.pallas_call(
        body, grid=(T // bm,), in_specs=[row, vec, row],
        out_specs=[row, row, vec, pl.BlockSpec((SUBLANES, LANES), lambda i: (0, 0))],
        out_shape=[jax.ShapeDtypeStruct((T, D), F32), jax.ShapeDtypeStruct((T, D), BF),
                   jax.ShapeDtypeStruct((1, D), F32), jax.ShapeDtypeStruct((SUBLANES, LANES), F32)],
        name="loss_head", compiler_params=_cparams(1))(h, gain, target)


def _ffn_up(n, w1g, w3g, comm=None):
    T, D = n.shape
    fs = w1g.shape[-1]
    bm = min(FFN_ROW_TILE, T)
    wspec = pl.BlockSpec((None, D, fs), lambda s, i, k: (s, 0, 0))
    ospec = pl.BlockSpec((None, bm, fs), lambda s, i, k: (s, i, 0))

    def epi(accs, ex, outs):
        a1, a3 = accs
        sg = _sig(a1)
        silu = a1 * sg
        outs[0][...] = (a3 * sg * (1.0 + a1 * (1.0 - sg))).astype(BF)
        outs[1][...] = silu.astype(BF)
        outs[2][...] = (silu * a3).astype(BF)

    sh = jax.ShapeDtypeStruct((N_DEV, T, fs), BF)
    return _mm("ffn_up", (N_DEV, T // bm, 1),
               [(n, pl.BlockSpec((bm, D), lambda s, i, k: (i, 0))), (w1g, wspec), (w3g, wspec)],
               [(0, 1, NN, 0), (0, 2, NN, 1)], 2, None, [], [(sh, ospec)] * 3, epi, comm=comm,
               nrow=max(1, bm // ROW_TILE))


def _ffn_down(g, w2g, h, comm=None):
    _, T, fs = g.shape
    D = h.shape[1]
    bm = min(ROW_TILE, T)
    row = pl.BlockSpec((bm, D), lambda i, j, s: (i, 0))

    def epi(accs, ex, outs):
        outs[0][...] = ex[0][...] + 0.5 * accs[0]

    return _mm("ffn_down", (T // bm, 1, N_DEV),
               [(g, pl.BlockSpec((None, bm, fs), lambda i, j, s: (s, i, 0))),
                (w2g, pl.BlockSpec((None, fs, D), lambda i, j, s: (s, 0, 0)))],
               [(0, 1, NN, 0)], 1, (bm, D), [(h, row)],
               [(jax.ShapeDtypeStruct((T, D), F32), row)], epi, comm=comm, ncol=max(1, D // COL_TILE))[0]


def _ffn_bwd_hidden(dhb, w2g, t1, t3, comm=None):
    T, D = dhb.shape
    fs = t1.shape[-1]
    bm = min(FFN_ROW_TILE, T)
    aspec = pl.BlockSpec((None, bm, fs), lambda s, i, k: (s, i, 0))

    def epi(accs, ex, outs):
        dg = 0.5 * accs[0]
        outs[0][...] = (dg * ex[0][...].astype(F32)).astype(BF)
        outs[1][...] = (dg * ex[1][...].astype(F32)).astype(BF)

    sh = jax.ShapeDtypeStruct((N_DEV, T, fs), BF)
    return _mm("ffn_bwd_hidden", (N_DEV, T // bm, 1),
               [(dhb, pl.BlockSpec((bm, D), lambda s, i, k: (i, 0))),
                (w2g, pl.BlockSpec((None, fs, D), lambda s, i, k: (s, 0, 0)))],
               [(0, 1, NT, 0)], 1, None, [(t1, aspec), (t3, aspec)], [(sh, aspec)] * 2, epi, comm=comm,
               nrow=max(1, bm // ROW_TILE))


def _ffn_dw2(g, dhb):
    _, T, fs = g.shape
    D = dhb.shape[1]
    bn = min(COL_TILE, D)

    def epi(accs, ex, outs):
        outs[0][...] = (0.5 * accs[0]).astype(BF)

    return _mm("ffn_dw2", (N_DEV, D // bn, 1),
               [(g, pl.BlockSpec((None, T, fs), lambda s, j, k: (s, 0, 0))),
                (dhb, pl.BlockSpec((T, bn), lambda s, j, k: (0, j)))],
               [(0, 1, TN, 0)], 1, None, [],
               [(jax.ShapeDtypeStruct((N_DEV, fs, D), BF), pl.BlockSpec((None, fs, bn), lambda s, j, k: (s, 0, j)))],
               epi)[0]


def _ffn_dw13(n, da1, da3, comm=None):
    T, D = n.shape
    fs = da1.shape[-1]
    bmr = min(COL_TILE, D)
    dspec = pl.BlockSpec((None, T, fs), lambda s, r, k: (s, 0, 0))
    ospec = pl.BlockSpec((None, bmr, fs), lambda s, r, k: (s, r, 0))

    def epi(accs, ex, outs):
        outs[0][...] = accs[0].astype(BF)
        outs[1][...] = accs[1].astype(BF)

    sh = jax.ShapeDtypeStruct((N_DEV, D, fs), BF)
    return _mm("ffn_dw13", (N_DEV, D // bmr, 1),
               [(n, pl.BlockSpec((T, bmr), lambda s, r, k: (0, r))), (da1, dspec), (da3, dspec)],
               [(0, 1, TN, 0), (0, 2, TN, 1)], 2, None, [], [(sh, ospec)] * 2, epi, comm=comm)


def _ffn_dn(da1, da3, w1g, w3g, comm=None):
    _, T, fs = da1.shape
    D = w1g.shape[-2]
    bm = min(FFN_ROW_TILE, T)
    dspec = pl.BlockSpec((None, bm, fs), lambda i, j, s: (s, i, 0))
    wspec = pl.BlockSpec((None, D, fs), lambda i, j, s: (s, 0, 0))
    row = pl.BlockSpec((bm, D), lambda i, j, s: (i, 0))

    def epi(accs, ex, outs):
        outs[0][...] = accs[0]

    return _mm("ffn_dn", (T // bm, 1, N_DEV),
               [(da1, dspec), (w1g, wspec), (da3, dspec), (w3g, wspec)],
               [(0, 1, NT, 0), (2, 3, NT, 0)], 1, (bm, D), [],
               [(jax.ShapeDtypeStruct((T, D), F32), row)], epi, comm=comm,
               nrow=max(1, bm // ROW_TILE), ncol=max(1, D // COL_TILE))[0]


def _rope_tables(seq):
    half = LANES // 2
    inv = ROPE_THETA ** (-jnp.arange(0, half, dtype=F32) * 2.0 / LANES)
    ang = jnp.arange(seq, dtype=F32)[:, None] * inv[None, :]
    cos, sin = jnp.cos(ang), jnp.sin(ang)
    return jnp.concatenate([cos, cos], axis=1), jnp.concatenate([-sin, sin], axis=1)


def _branch_bias(nq, bq):
    d = (jnp.arange(nq)[:, None, None] * bq + jnp.arange(bq)[None, :, None]
         - jnp.arange(bq)[None, None, :])
    mult = jnp.zeros(d.shape, F32)
    for window, dil in DILATED_PATTERN:
        mult = mult + ((d >= 0) & (d % dil == 0) & (d <= window)).astype(F32)
    return jnp.where(mult > 0, jnp.log(jnp.maximum(mult, 1.0)), NEG_INF)


def _proj_fwd(u, wing, comm=None):
    T, D = u.shape
    ws = wing.shape[-1]
    bm = min(ROW_TILE, T)

    def epi(accs, ex, outs):
        outs[0][...] = accs[0]

    return _mm("proj_fwd", (N_DEV, T // bm, 1),
               [(u, pl.BlockSpec((bm, D), lambda s, i, k: (i, 0))),
                (wing, pl.BlockSpec((None, D, ws), lambda s, i, k: (s, 0, 0)))],
               [(0, 1, NN, 0)], 1, None, [],
               [(jax.ShapeDtypeStruct((T, N_DEV * ws), F32),
                 pl.BlockSpec((bm, ws), lambda s, i, k: (i, s)))], epi, comm=comm)[0]


def _rope_fwd(proj, cosf, sinf, seq, nh):
    T = proj.shape[0]
    bs = min(ROW_TILE, seq)
    nst = seq // bs
    scale = LANES ** -0.5

    def body(x_ref, c_ref, s_ref, o_ref):
        j = pl.program_id(1)
        t = x_ref[...]
        rot = t * c_ref[...] + pltpu.roll(t, LANES // 2, 1) * s_ref[...]
        rot = rot * jnp.where(j < nh, scale, 1.0)
        o_ref[...] = jnp.where(j < 2 * nh, rot, t).astype(BF)

    blk = pl.BlockSpec((bs, LANES), lambda r, j: (r, j))
    tab = pl.BlockSpec((bs, LANES), lambda r, j: (r % nst, 0))
    return pl.pallas_call(
        body, grid=(T // bs, 3 * nh), in_specs=[blk, tab, tab], out_specs=blk,
        out_shape=jax.ShapeDtypeStruct((T, 3 * nh * LANES), BF), name="rope_fwd",
        compiler_params=_cparams(2))(proj, cosf, sinf)


def _attn_fwd(qkv, bias, nb, seq, nh, comm=None):
    T = nb * seq
    bq = bias.shape[1]
    nq = seq // bq

    def body(q_ref, k_ref, v_ref, b_ref, o_ref, lse_ref):
        qi = pl.program_id(2)
        q = q_ref[...]

        def step(kj, carry):
            m, l, acc = carry
            rows = pl.ds(pl.multiple_of(kj * bq, bq), bq)
            s = _dot(q, k_ref[rows, :], NT) + b_ref[qi - kj]
            m_new = jnp.maximum(m, jnp.max(s, axis=1, keepdims=True))
            p = jnp.exp(s - m_new)
            alpha = jnp.exp(m - m_new)
            l = alpha * l + jnp.sum(p, axis=1, keepdims=True)
            acc = alpha * acc + _dot(p, v_ref[rows, :], NN)
            return m_new, l, acc

        init = (jnp.full((bq, 1), NEG_INF, F32), jnp.zeros((bq, 1), F32), jnp.zeros((bq, LANES), F32))
        m, l, acc = lax.fori_loop(0, qi + 1, step, init)
        o_ref[...] = (acc / l).astype(BF)
        lse_ref[...] = m + jnp.log(l)

    return _call(
        body, "attn_fwd", (nb, nh, nq),
        [pl.BlockSpec((bq, LANES), lambda b, h, i: (b * nq + i, h)),
         pl.BlockSpec((seq, LANES), lambda b, h, i: (b, nh + h)),
         pl.BlockSpec((seq, LANES), lambda b, h, i: (b, 2 * nh + h)),
         pl.BlockSpec((nq, bq, bq), lambda b, h, i: (0, 0, 0))],
        [pl.BlockSpec((bq, LANES), lambda b, h, i: (b * nq + i, h)),
         pl.BlockSpec((None, bq, 1), lambda b, h, i: (h, b * nq + i, 0))],
        [jax.ShapeDtypeStruct((T, 2 * nh * LANES), BF), jax.ShapeDtypeStruct((nh, T, 1), F32)],
        (qkv, qkv, qkv, bias), comm=comm)


def _attn_bwd_dq(qkv, cat, dcat, lse, bias, nb, seq, nh, comm=None):
    T = nb * seq
    bq = bias.shape[1]
    nq = seq // bq

    def body(q_ref, k_ref, v_ref, o_ref, do_ref, lse_ref, b_ref, dq_ref, delta_ref):
        qi = pl.program_id(2)
        q = q_ref[...]
        do = do_ref[...]
        dob = do.astype(BF)
        lse_t = lse_ref[...]
        delta = jnp.sum(do * o_ref[...].astype(F32), axis=1, keepdims=True)
        delta_ref[...] = delta

        def step(kj, dq):
            rows = pl.ds(pl.multiple_of(kj * bq, bq), bq)
            k = k_ref[rows, :]
            p = jnp.exp(_dot(q, k, NT) + b_ref[qi - kj] - lse_t)
            ds = p * (_dot(dob, v_ref[rows, :], NT) - delta)
            return dq + _dot(ds, k, NN)

        dq_ref[...] = lax.fori_loop(0, qi + 1, step, jnp.zeros((bq, LANES), F32))

    tile = pl.BlockSpec((bq, LANES), lambda b, h, i: (b * nq + i, h))
    stat = pl.BlockSpec((None, bq, 1), lambda b, h, i: (h, b * nq + i, 0))
    return _call(
        body, "attn_bwd_dq", (nb, nh, nq),
        [tile, pl.BlockSpec((seq, LANES), lambda b, h, i: (b, nh + h)),
         pl.BlockSpec((seq, LANES), lambda b, h, i: (b, 2 * nh + h)), tile, tile, stat,
         pl.BlockSpec((nq, bq, bq), lambda b, h, i: (0, 0, 0))],
        [tile, stat],
        [jax.ShapeDtypeStruct((T, nh * LANES), F32), jax.ShapeDtypeStruct((nh, T, 1), F32)],
        (qkv, qkv, qkv, cat, dcat, lse, bias), comm=comm)


def _attn_bwd_dkv(qkv, dcat, lse, delta, bias, nb, seq, nh):
    T = nb * seq
    bq = bias.shape[1]
    nq = seq // bq

    def body(k_ref, v_ref, q_ref, do_ref, lse_ref, delta_ref, b_ref, dk_ref, dv_ref):
        kj = pl.program_id(2)
        k = k_ref[...]
        v = v_ref[...]

        def step(qi, carry):
            dk, dv = carry
            rows = pl.ds(pl.multiple_of(qi * bq, bq), bq)
            q = q_ref[rows, :]
            dob = do_ref[rows, :].astype(BF)
            p = jnp.exp(_dot(q, k, NT) + b_ref[qi - kj] - lse_ref[rows, :])
            dv = dv + _dot(p, dob, TN)
            ds = p * (_dot(dob, v, NT) - delta_ref[rows, :])
            return dk + _dot(ds, q, TN), dv

        z = jnp.zeros((bq, LANES), F32)
        dk, dv = lax.fori_loop(kj, nq, step, (z, z))
        dk_ref[...] = dk
        dv_ref[...] = dv

    stat = pl.BlockSpec((None, seq, 1), lambda b, h, i: (h, b, 0))
    out = pl.BlockSpec((bq, LANES), lambda b, h, i: (b * nq + i, h))
    sh = jax.ShapeDtypeStruct((T, nh * LANES), F32)
    return pl.pallas_call(
        body, grid=(nb, nh, nq),
        in_specs=[pl.BlockSpec((bq, LANES), lambda b, h, i: (b * nq + i, nh + h)),
                  pl.BlockSpec((bq, LANES), lambda b, h, i: (b * nq + i, 2 * nh + h)),
                  pl.BlockSpec((seq, LANES), lambda b, h, i: (b, h)),
                  pl.BlockSpec((seq, LANES), lambda b, h, i: (b, h)),
                  stat, stat,
                  pl.BlockSpec((nq, bq, bq), lambda b, h, i: (0, 0, 0))],
        out_specs=[out, out], out_shape=[sh, sh],
        name="attn_bwd_dkv", compiler_params=_cparams(3))(qkv, qkv, qkv, dcat, lse, delta, bias)


def _conv_parts(gc, xin, w_ref):
    w = [w_ref[k:k + 1, :] for k in range(3)]
    u = gc * xin
    row = lax.broadcasted_iota(jnp.int32, u.shape, 0)
    u1 = jnp.where(row >= 1, pltpu.roll(u, 1, 0), 0.0)
    u2 = jnp.where(row >= 2, pltpu.roll(u, 2, 0), 0.0)
    return u, u1, u2, w[0] * u2 + w[1] * u1 + w[2] * u, w, row


def _conv_fwd(proj, conv_w, cat, nb, seq, width):
    cw = min(2 * LANES, width)
    nc = width // cw

    def body(gb_ref, gc_ref, x_ref, w_ref, cat_ref, o_ref):
        _, _, _, conv, _, _ = _conv_parts(gc_ref[...], x_ref[...], w_ref)
        o_ref[...] = (gb_ref[...] * conv).astype(BF)

    def sec(k):
        return pl.BlockSpec((seq, cw), lambda b, c: (b, k * nc + c))

    return pl.pallas_call(
        body, grid=(nb, nc),
        in_specs=[sec(3), sec(4), sec(5), pl.BlockSpec((3, cw), lambda b, c: (0, c)),
                  pl.BlockSpec(memory_space=pl.ANY)],
        out_specs=pl.BlockSpec((seq, cw), lambda b, c: (b, nc + c)),
        out_shape=jax.ShapeDtypeStruct(cat.shape, BF), input_output_aliases={4: 0},
        name="conv_fwd", compiler_params=_cparams(2))(proj, proj, proj, conv_w, cat)


def _conv_bwd(proj, conv_w, dcat, nb, seq, width):
    cw = min(2 * LANES, width)
    nc = width // cw
    T = nb * seq

    def body(gb_ref, gc_ref, x_ref, w_ref, d_ref, dgb_ref, dgc_ref, dx_ref, dw_ref):
        gc = gc_ref[...]
        xin = x_ref[...]
        u, u1, u2, conv, w, row = _conv_parts(gc, xin, w_ref)
        dsc = d_ref[...]
        dgb_ref[...] = dsc * conv
        dconv = dsc * gb_ref[...]
        d1 = jnp.where(row < seq - 1, pltpu.roll(dconv, seq - 1, 0), 0.0)
        d2 = jnp.where(row < seq - 2, pltpu.roll(dconv, seq - 2, 0), 0.0)
        du = w[2] * dconv + w[1] * d1 + w[0] * d2
        dgc_ref[...] = du * xin
        dx_ref[...] = du * gc

        @pl.when(pl.program_id(1) == 0)
        def _():
            dw_ref[...] = jnp.zeros_like(dw_ref)

        dw_ref[0:1, :] += jnp.sum(dconv * u2, axis=0, keepdims=True)
        dw_ref[1:2, :] += jnp.sum(dconv * u1, axis=0, keepdims=True)
        dw_ref[2:3, :] += jnp.sum(dconv * u, axis=0, keepdims=True)

    def sec(k):
        return pl.BlockSpec((seq, cw), lambda c, b: (b, k * nc + c))

    out = pl.BlockSpec((seq, cw), lambda c, b: (b, c))
    wsp = pl.BlockSpec((3, cw), lambda c, b: (0, c))
    sh = jax.ShapeDtypeStruct((T, width), F32)
    return pl.pallas_call(
        body, grid=(nc, nb), in_specs=[sec(3), sec(4), sec(5), wsp, sec(1)],
        out_specs=[out, out, out, wsp], out_shape=[sh, sh, sh, jax.ShapeDtypeStruct((3, width), F32)],
        name="conv_bwd", compiler_params=_cparams(2))(proj, proj, proj, conv_w, dcat)


def _assemble_dproj(dq, dk, dv, dgb, dgc, dxin, cosf, sinf, seq):
    T, width = dq.shape
    nh = width // LANES
    bs = min(256, seq)
    nst = seq // bs
    scale = LANES ** -0.5

    def body(dq_ref, dk_ref, dv_ref, dgb_ref, dgc_ref, dx_ref, c_ref, s_ref, o_ref):
        sec = pl.program_id(1)
        c = c_ref[...]
        s = s_ref[...]

        def unrope(ref, mul):
            for h in range(nh):
                cols = slice(h * LANES, (h + 1) * LANES)
                t = ref[:, cols]
                o_ref[:, cols] = ((t * c + pltpu.roll(t * s, LANES // 2, 1)) * mul).astype(BF)

        @pl.when(sec == 0)
        def _():
            unrope(dq_ref, scale)

        @pl.when(sec == 1)
        def _():
            unrope(dk_ref, 1.0)

        for k, ref in ((2, dv_ref), (3, dgb_ref), (4, dgc_ref), (5, dx_ref)):
            @pl.when(sec == k)
            def _(ref=ref):
                o_ref[...] = ref[...].astype(BF)

    blk = pl.BlockSpec((bs, width), lambda r, k: (r, 0))
    tab = pl.BlockSpec((bs, LANES), lambda r, k: (r % nst, 0))
    return pl.pallas_call(
        body, grid=(T // bs, 6), in_specs=[blk] * 6 + [tab, tab],
        out_specs=pl.BlockSpec((bs, width), lambda r, k: (r, k)),
        out_shape=jax.ShapeDtypeStruct((T, 6 * width), BF), name="assemble_dproj",
        compiler_params=_cparams(2))(dq, dk, dv, dgb, dgc, dxin, cosf, sinf)


def _res_mm(name, a, w, h, comm=None):
    T, K = a.shape
    N = w.shape[1]
    bm = min(ROW_TILE, T)
    bk = min(ROW_TILE, K)
    row = pl.BlockSpec((bm, N), lambda i, j, k: (i, 0))

    def epi(accs, ex, outs):
        outs[0][...] = ex[0][...] + accs[0]

    return _mm(name, (T // bm, 1, K // bk),
               [(a, pl.BlockSpec((bm, bk), lambda i, j, k: (i, k))),
                (w, pl.BlockSpec((bk, N), lambda i, j, k: (k, 0)))],
               [(0, 1, NN, 0)], 1, (bm, N), [(h, row)],
               [(jax.ShapeDtypeStruct((T, N), F32), row)], epi, comm=comm, ncol=max(1, N // COL_TILE))[0]


def _mm_nt(name, a, w, out_dtype):
    T, K = a.shape
    N = w.shape[0]
    bm = min(ROW_TILE, T)
    bn = min(ROW_TILE, N)

    def epi(accs, ex, outs):
        outs[0][...] = accs[0].astype(out_dtype)

    return _mm(name, (T // bm, N // bn, 1),
               [(a, pl.BlockSpec((bm, K), lambda i, j, k: (i, 0))),
                (w, pl.BlockSpec((bn, K), lambda i, j, k: (j, 0)))],
               [(0, 1, NT, 0)], 1, None, [],
               [(jax.ShapeDtypeStruct((T, N), out_dtype), pl.BlockSpec((bm, bn), lambda i, j, k: (i, j)))],
               epi)[0]


def _mm_tn(name, a, bs_list):
    T, M = a.shape
    N = bs_list[0].shape[1]
    bmr = min(COL_TILE, M)
    bn = min(COL_TILE, N)
    n = len(bs_list)

    def epi(accs, ex, outs):
        for q in range(n):
            outs[q][...] = accs[q].astype(BF)

    ops = [(a, pl.BlockSpec((T, bmr), lambda r, j, k: (0, r)))]
    ops += [(b, pl.BlockSpec((T, bn), lambda r, j, k: (0, j))) for b in bs_list]
    return _mm(name, (M // bmr, N // bn, 1), ops, [(0, 1 + q, TN, q) for q in range(n)], n, None, [],
               [(jax.ShapeDtypeStruct((M, N), BF), pl.BlockSpec((bmr, bn), lambda r, j, k: (r, j)))] * n, epi)


def _proj_bwd_x(dproj, wing):
    T = dproj.shape[0]
    _, D, ws = wing.shape
    bm = min(ROW_TILE, T)
    row = pl.BlockSpec((bm, D), lambda i, j, s: (i, 0))

    def epi(accs, ex, outs):
        outs[0][...] = accs[0]

    return _mm("proj_bwd_x", (T // bm, 1, N_DEV),
               [(dproj, pl.BlockSpec((bm, ws), lambda i, j, s: (i, s))),
                (wing, pl.BlockSpec((None, D, ws), lambda i, j, s: (s, 0, 0)))],
               [(0, 1, NT, 0)], 1, (bm, D), [], [(jax.ShapeDtypeStruct((T, D), F32), row)], epi,
               ncol=max(1, D // COL_TILE))[0]


def _proj_dw(u, dproj, ws):
    T, D = u.shape
    bmr = min(COL_TILE, D)

    def epi(accs, ex, outs):
        outs[0][...] = accs[0].astype(BF)

    return _mm("proj_dw", (N_DEV, D // bmr, 1),
               [(u, pl.BlockSpec((T, bmr), lambda s, r, k: (0, r))),
                (dproj, pl.BlockSpec((T, ws), lambda s, r, k: (0, s)))],
               [(0, 1, TN, 0)], 1, None, [],
               [(jax.ShapeDtypeStruct((N_DEV, D, ws), BF),
                 pl.BlockSpec((None, bmr, ws), lambda s, r, k: (s, r, 0)))], epi)[0]


def _mixer_ab_fwd(h, gain, wing, conv_w, wout, tabs, nb, seq, comm_proj=None, comm_attn=None, comm_out=None):
    cosf, sinf, bias = tabs
    width = wing.shape[-1] * N_DEV // 6
    nh = width // LANES
    u = _rms_fwd(h, gain, BF)
    proj = _proj_fwd(u, wing, comm=comm_proj)
    qkv = _rope_fwd(proj, cosf, sinf, seq, nh)
    cat, lse = _attn_fwd(qkv, bias, nb, seq, nh, comm=comm_attn)
    cat = _conv_fwd(proj, conv_w, cat, nb, seq, width)
    return _res_mm("outproj_fwd", cat, wout, h, comm=comm_out), (h, u, proj, qkv, cat, lse)


def _mixer_ab_bwd(dh, dhb, saved, gain, wing, conv_w, wout, tabs, nb, seq, reduce_start, carry):
    cosf, sinf, bias = tabs
    h, u, proj, qkv, cat, lse = saved
    D = h.shape[1]
    ws = wing.shape[-1]
    width = ws * N_DEV // 6
    nh = width // LANES
    dcat = _mm_nt("outproj_bwd_x", dhb, wout, F32)
    dwout = _mm_tn("outproj_dw", cat, [dhb])[0]
    comm = _merge_comms(reduce_start(["ab_w_out"], [dwout.reshape(N_DEV, -1, D)]) + [carry])
    dq, delta = _attn_bwd_dq(qkv, cat, dcat, lse, bias, nb, seq, nh, comm=comm)
    dk, dv = _attn_bwd_dkv(qkv, dcat, lse, delta, bias, nb, seq, nh)
    dgb, dgc, dxin, dconvw = _conv_bwd(proj, conv_w, dcat, nb, seq, width)
    dproj = _assemble_dproj(dq, dk, dv, dgb, dgc, dxin, cosf, sinf, seq)
    du = _proj_bwd_x(dproj, wing)
    comm, = reduce_start(["ab_w_in"], [_proj_dw(u, dproj, ws)])
    dh_in, dhb_in, dgain = _rms_bwd(du, h, gain, dh)
    return dh_in, dhb_in, dgain, dconvw, comm


def _s5_zoh(lr, li, log_dt):
    dt = jnp.exp(log_dt)
    mag = jnp.exp(lr * dt)
    ar = mag * jnp.cos(li * dt)
    ai = mag * jnp.sin(li * dt)
    den = lr * lr + li * li
    return dt, ar, ai, den, ((ar - 1.0) * lr + ai * li) / den, (ai * lr - (ar - 1.0) * li) / den


def _s5_discretize(lam_re, lam_im, log_dt, bt_re, bt_im):
    def body(lr_ref, li_ref, ld_ref, br_ref, bi_ref, ar_ref, ai_ref, bbr_ref, bbi_ref):
        _, ar, ai, _, fr, fi = _s5_zoh(lr_ref[...], li_ref[...], ld_ref[...])
        ar_ref[...] = ar
        ai_ref[...] = ai
        bbr_ref[...] = fr * br_ref[...] - fi * bi_ref[...]
        bbi_ref[...] = fr * bi_ref[...] + fi * br_ref[...]

    small = jax.ShapeDtypeStruct(lam_re.shape, F32)
    big = jax.ShapeDtypeStruct(bt_re.shape, F32)
    return pl.pallas_call(body, out_shape=[small, small, big, big], name="s5_discretize",
                          compiler_params=_cparams(0))(lam_re, lam_im, log_dt, bt_re, bt_im)


def _s5_discretize_bwd(lam_re, lam_im, log_dt, bt_re, bt_im, d_ar, d_ai, d_bbr, d_bbi):

    def body(lr_ref, li_ref, ld_ref, br_ref, bi_ref, dar_ref, dai_ref, dbbr_ref, dbbi_ref,
             dlr_ref, dli_ref, dld_ref, dbr_ref, dbi_ref):
        lr, li = lr_ref[...], li_ref[...]
        dt, ar, ai, den, fr, fi = _s5_zoh(lr, li, ld_ref[...])
        br, bi = br_ref[...], bi_ref[...]
        dbbr, dbbi = dbbr_ref[...], dbbi_ref[...]
        dbr_ref[...] = dbbr * fr + dbbi * fi
        dbi_ref[...] = dbbi * fr - dbbr * fi
        dfr = jnp.sum(dbbr * br + dbbi * bi, axis=1, keepdims=True)
        dfi = jnp.sum(dbbi * br - dbbr * bi, axis=1, keepdims=True)
        dnr = dfr / den
        dni = dfi / den
        dden = -(dfr * fr + dfi * fi) / den
        dar = dar_ref[...] + dnr * lr - dni * li
        dai = dai_ref[...] + dnr * li + dni * lr
        dlr_ref[...] = dnr * (ar - 1.0) + dni * ai + 2.0 * dden * lr + dt * (dar * ar + dai * ai)
        dli_ref[...] = dnr * ai - dni * (ar - 1.0) + 2.0 * dden * li + dt * (dai * ar - dar * ai)
        ddt = jnp.sum(dar * (lr * ar - li * ai) + dai * (lr * ai + li * ar), axis=2, keepdims=True)
        dld_ref[...] = ddt * dt

    small = jax.ShapeDtypeStruct(lam_re.shape, F32)
    big = jax.ShapeDtypeStruct(bt_re.shape, F32)
    return pl.pallas_call(
        body, out_shape=[small, small, jax.ShapeDtypeStruct(log_dt.shape, F32), big, big],
        name="s5_discretize_bwd", compiler_params=_cparams(0))(
            lam_re, lam_im, log_dt, bt_re, bt_im, d_ar, d_ai, d_bbr, d_bbi)


def _rows8(t):
    return pl.ds(pl.multiple_of(t * SUBLANES, SUBLANES), SUBLANES)


def _cmul_add(ar, ai, sr, si, br, bi):
    return ar * sr - ai * si + br, ar * si + ai * sr + bi


def _cpow(ar, ai, n):
    rr = ri = None
    while n:
        if n & 1:
            rr, ri = (ar, ai) if rr is None else (rr * ar - ri * ai, rr * ai + ri * ar)
        ar, ai = ar * ar - ai * ai, 2.0 * ar * ai
        n >>= 1
    return rr, ri


def _s5_specs(R, nj):
    sh = STATE_COLS
    return dict(
        rows=pl.BlockSpec((R, LANES), lambda j: (0, j)),
        bd=pl.BlockSpec((None, LANES, sh), lambda j: (j, 0, 0)),
        cd=pl.BlockSpec((None, sh, LANES), lambda j: (j, 0, 0)),
        a=pl.BlockSpec((None, 1, sh), lambda j: (j, 0, 0)),
        vec=pl.BlockSpec((1, LANES), lambda j: (0, j)),
        init=pl.BlockSpec((None, SUBLANES, sh), lambda j: (j, 0, 0)))


def _s5_fwd(u, mats, seg_len, nseg, comm=None):
    bdr, bdi, cdr, cdi, are, aim, dsk = mats
    R, D = u.shape
    nj = D // LANES
    sh = STATE_COLS
    rc = min(R, 512)
    sp = _s5_specs(R, nj)

    def body(u_ref, bdr_ref, bdi_ref, cdr_ref, cdi_ref, ar_ref, ai_ref, d_ref,
             y_ref, yg_ref, ir_ref, ii_ref, sre, sim):
        ar = jnp.broadcast_to(ar_ref[...], (SUBLANES, sh))
        ai = jnp.broadcast_to(ai_ref[...], (SUBLANES, sh))

        def bu_chunk(c, _):
            rows = pl.ds(pl.multiple_of(c * rc, rc), rc)
            ub = u_ref[rows, :].astype(BF)
            sre[rows, :] = _dot(ub, bdr_ref[...], NN)
            sim[rows, :] = _dot(ub, bdi_ref[...], NN)
            return 0

        lax.fori_loop(0, R // rc, bu_chunk, 0)
        z = jnp.zeros((SUBLANES, sh), F32)

        def local_scan(t, c):
            return _cmul_add(ar, ai, c[0], c[1], sre[_rows8(t), :], sim[_rows8(t), :])

        er, ei = lax.fori_loop(0, seg_len, local_scan, (z, z), unroll=SCAN_UNROLL)
        pr, pi = _cpow(ar, ai, seg_len)
        first = (lax.broadcasted_iota(jnp.int32, (SUBLANES, sh), 0) & (nseg - 1)) == 0

        def prev(x):
            return jnp.where(first, 0.0, pltpu.roll(x, 1, 0))

        xr, xi = er, ei
        for _ in range(nseg - 1):
            xr, xi = _cmul_add(pr, pi, prev(xr), prev(xi), er, ei)
        i_r, i_i = prev(xr), prev(xi)
        ir_ref[...] = i_r
        ii_ref[...] = i_i

        def scan(t, c):
            nr, ni = _cmul_add(ar, ai, c[0], c[1], sre[_rows8(t), :], sim[_rows8(t), :])
            sre[_rows8(t), :] = nr
            sim[_rows8(t), :] = ni
            return nr, ni

        lax.fori_loop(0, seg_len, scan, (i_r, i_i), unroll=SCAN_UNROLL)

        def y_chunk(c, _):
            rows = pl.ds(pl.multiple_of(c * rc, rc), rc)
            y = _dot(sre[rows, :], cdr_ref[...], NN) + _dot(sim[rows, :], cdi_ref[...], NN)
            y = y + d_ref[...] * u_ref[rows, :]
            y_ref[rows, :] = y
            yg_ref[rows, :] = _gelu(y).astype(BF)
            return 0

        lax.fori_loop(0, R // rc, y_chunk, 0)

    init_sh = jax.ShapeDtypeStruct((nj, SUBLANES, STATE_COLS), F32)
    return _call(
        body, "s5_fwd", (nj,),
        [sp["rows"], sp["bd"], sp["bd"], sp["cd"], sp["cd"], sp["a"], sp["a"], sp["vec"]],
        [sp["rows"], sp["rows"], sp["init"], sp["init"]],
        [jax.ShapeDtypeStruct((R, D), F32), jax.ShapeDtypeStruct((R, D), BF), init_sh, init_sh],
        (u, bdr, bdi, cdr, cdi, are, aim, dsk),
        scratch=[pltpu.VMEM((R, sh), F32) for _ in range(2)], comm=comm)


def _s5_bwd(u, dy, mats, init_re, init_im, seg_len, nseg, comm=None):
    bdr, bdi, cdr, cdi, are, aim, dsk = mats
    R, D = u.shape
    nj = D // LANES
    sh = STATE_COLS
    rc = min(R, 512)
    sp = _s5_specs(R, nj)

    def body(u_ref, dy_ref, bdr_ref, bdi_ref, cdr_ref, cdi_ref, ar_ref, ai_ref, d_ref, ir_ref, ii_ref,
             du_ref, dbdr_ref, dbdi_ref, dcdr_ref, dcdi_ref, dar_ref, dai_ref, dd_ref,
             sre, sim, gre, gim):
        ar = jnp.broadcast_to(ar_ref[...], (SUBLANES, sh))
        ai = jnp.broadcast_to(ai_ref[...], (SUBLANES, sh))
        i_r, i_i = ir_ref[...], ii_ref[...]

        def chunk(c):
            return pl.ds(pl.multiple_of(c * rc, rc), rc)

        def bu_chunk(c, _):
            ub = u_ref[chunk(c), :].astype(BF)
            sre[chunk(c), :] = _dot(ub, bdr_ref[...], NN)
            sim[chunk(c), :] = _dot(ub, bdi_ref[...], NN)
            return 0

        lax.fori_loop(0, R // rc, bu_chunk, 0)

        def scan(t, c):
            nr, ni = _cmul_add(ar, ai, c[0], c[1], sre[_rows8(t), :], sim[_rows8(t), :])
            sre[_rows8(t), :] = nr
            sim[_rows8(t), :] = ni
            return nr, ni

        lax.fori_loop(0, seg_len, scan, (i_r, i_i), unroll=SCAN_UNROLL)

        def c_chunk(c, carry):
            dyb = dy_ref[chunk(c), :].astype(BF)
            gre[chunk(c), :] = _dot(dyb, cdr_ref[...], NT)
            gim[chunk(c), :] = _dot(dyb, cdi_ref[...], NT)
            return (carry[0] + _dot(sre[chunk(c), :], dyb, TN), carry[1] + _dot(sim[chunk(c), :], dyb, TN))

        zc = jnp.zeros((sh, LANES), F32)
        dcr, dci = lax.fori_loop(0, R // rc, c_chunk, (zc, zc))
        dcdr_ref[...] = dcr
        dcdi_ref[...] = dci

        def adj(t, gr_next, gi_next):
            return _cmul_add(ar, -ai, gr_next, gi_next, gre[_rows8(t), :], gim[_rows8(t), :])

        z = jnp.zeros((SUBLANES, sh), F32)
        fr, fi = lax.fori_loop(0, seg_len, lambda i, c: adj(seg_len - 1 - i, c[0], c[1]), (z, z),
                               unroll=SCAN_UNROLL)
        pr, pi = _cpow(ar, ai, seg_len)
        last =(lax.broadcasted_iota(jnp.int32, (SUBLANES, sh), 0) & (nseg - 1)) == nseg - 1

        def nxt(x):
            return jnp.where(last, 0.0, pltpu.roll(x, SUBLANES - 1, 0))

        xr, xi = fr, fi
        for _ in range(nseg - 1):
            xr, xi = _cmul_add(pr, -pi, nxt(xr), nxt(xi), fr, fi)
        g0r, g0i = nxt(xr), nxt(xi)

        def adj_scan(i, c):
            t = seg_len - 1 - i
            gr, gi = adj(t, c[0], c[1])
            gre[_rows8(t), :] = gr
            gim[_rows8(t), :] = gi
            spr, spi = sre[_rows8(t - 1), :], sim[_rows8(t - 1), :]
            return gr, gi, c[2] + spr * gr + spi * gi, c[3] + spr * gi - spi * gr

        gr, gi, dar, dai = lax.fori_loop(0, seg_len - 1, adj_scan, (g0r, g0i, z, z))
        gr, gi = adj(0, gr, gi)
        gre[_rows8(0), :] = gr
        gim[_rows8(0), :] = gi
        dar_ref[...] = jnp.sum(dar + i_r * gr + i_i * gi, axis=0, keepdims=True)
        dai_ref[...] = jnp.sum(dai + i_r * gi - i_i * gr, axis=0, keepdims=True)

        def d_chunk(c, carry):
            ub = u_ref[chunk(c), :].astype(BF)
            grb = gre[chunk(c), :].astype(BF)
            gib = gim[chunk(c), :].astype(BF)
            du = _dot(grb, bdr_ref[...], NT) + _dot(gib, bdi_ref[...], NT)
            du_ref[chunk(c), :] = du + d_ref[...] * dy_ref[chunk(c), :]
            dd = carry[2] + jnp.sum(dy_ref[chunk(c), :] * u_ref[chunk(c), :], axis=0, keepdims=True)
            return carry[0] + _dot(ub, grb, TN), carry[1] + _dot(ub, gib, TN), dd

        zb = jnp.zeros((LANES, sh), F32)
        dbr, dbi, dd = lax.fori_loop(0, R // rc, d_chunk, (zb, zb, jnp.zeros((1, LANES), F32)))
        dbdr_ref[...] = dbr
        dbdi_ref[...] = dbi
        dd_ref[...] = dd

    bd_sh = jax.ShapeDtypeStruct((nj, LANES, STATE_COLS), F32)
    cd_sh = jax.ShapeDtypeStruct((nj, STATE_COLS, LANES), F32)
    a_sh = jax.ShapeDtypeStruct((nj, 1, STATE_COLS), F32)
    return _call(
        body, "s5_bwd", (nj,),
        [sp["rows"], sp["rows"], sp["bd"], sp["bd"], sp["cd"], sp["cd"], sp["a"], sp["a"],
         sp["vec"], sp["init"], sp["init"]],
        [sp["rows"], sp["bd"], sp["bd"], sp["cd"], sp["cd"], sp["a"], sp["a"], sp["vec"]],
        [jax.ShapeDtypeStruct((R, D), F32), bd_sh, bd_sh, cd_sh, cd_sh, a_sh, a_sh,
         jax.ShapeDtypeStruct((1, D), F32)],
        (u, dy, bdr, bdi, cdr, cdi, are, aim, dsk, init_re, init_im),
        scratch=[pltpu.VMEM((R, sh), F32) for _ in range(4)], comm=comm)


def _glu_fwd(yg, wa, wb, h):
    T, D = yg.shape
    N = wa.shape[1]
    bm = min(ROW_TILE, T)
    bn = min(ROW_TILE, N)
    wspec = pl.BlockSpec((D, bn), lambda i, j, k: (0, j))
    ospec = pl.BlockSpec((bm, bn), lambda i, j, k: (i, j))

    def epi(accs, ex, outs):
        pa, pb = accs
        outs[0][...] = ex[0][...] + pa * _sig(pb)
        outs[1][...] = pa.astype(BF)
        outs[2][...] = pb.astype(BF)

    return _mm("glu_fwd", (T // bm, N // bn, 1),
               [(yg, pl.BlockSpec((bm, D), lambda i, j, k: (i, 0))), (wa, wspec), (wb, wspec)],
               [(0, 1, NN, 0), (0, 2, NN, 1)], 2, None, [(h, ospec)],
               [(jax.ShapeDtypeStruct((T, N), F32), ospec), (jax.ShapeDtypeStruct((T, N), BF), ospec),
                (jax.ShapeDtypeStruct((T, N), BF), ospec)], epi)


def _glu_bwd_gates(dz, pa, pb):
    T, D = dz.shape
    bm = min(ROW_TILE, T)

    def body(dz_ref, pa_ref, pb_ref, dpa_ref, dpb_ref):
        dz = dz_ref[...]
        sg = _sig(pb_ref[...].astype(F32))
        dpa_ref[...] = (dz * sg).astype(BF)
        dpb_ref[...] = (dz * pa_ref[...].astype(F32) * sg * (1.0 - sg)).astype(BF)

    row = pl.BlockSpec((bm, D), lambda i: (i, 0))
    return pl.pallas_call(
        body, grid=(T // bm,), in_specs=[row] * 3, out_specs=[row] * 2,
        out_shape=[jax.ShapeDtypeStruct((T, D), BF)] * 2, name="glu_bwd_gates",
        compiler_params=_cparams(1))(dz, pa, pb)


def _glu_bwd_y(dpa, dpb, wa, wb, y_pre, comm=None):
    T, N = dpa.shape
    D = wa.shape[0]
    bm = min(ROW_TILE, T)
    bn = min(ROW_TILE, D)
    aspec = pl.BlockSpec((bm, N), lambda i, j, k: (i, 0))
    wspec = pl.BlockSpec((bn, N), lambda i, j, k: (j, 0))
    ospec = pl.BlockSpec((bm, bn), lambda i, j, k: (i, j))

    def epi(accs, ex, outs):
        outs[0][...] = accs[0] * _gelu_grad(ex[0][...])

    return _mm("glu_bwd_y", (T // bm, D // bn, 1), [(dpa, aspec), (wa, wspec), (dpb, aspec), (wb, wspec)],
               [(0, 1, NT, 0), (2, 3, NT, 0)], 1, None, [(y_pre, ospec)],
               [(jax.ShapeDtypeStruct((T, D), F32), ospec)], epi, comm=comm)[0]


def _block_diag_in(x, nj):
    g = GROUPS_PER_BLOCK
    x = x.reshape(nj, g, 1, S5_GROUP, S5_STATE)
    eye = jnp.eye(g, dtype=bool)[None, :, :, None, None]
    full = jnp.where(eye, x, 0.0)
    return full.transpose(0, 1, 3, 2, 4).reshape(nj, g * S5_GROUP, g * S5_STATE)


def _block_diag_out(x, nj):
    return _block_diag_in(x, nj).transpose(0, 2, 1)


def _diag_of_in(m, nj):
    g = GROUPS_PER_BLOCK
    m5 = m.reshape(nj, g, S5_GROUP, g, S5_STATE)
    d = jnp.diagonal(m5, axis1=1, axis2=3)
    return d.transpose(0, 3, 1, 2).reshape(nj * g, S5_GROUP, S5_STATE)


def _mixer_s5_fwd(h, gain, p, dsk, wa, wb, nb, seq, comm_s5=None):
    T, D = h.shape
    nj = D // LANES
    nseg = SUBLANES // nb
    seg_len = seq // nseg
    G = p["s5_lambda_re"].shape[1]
    lam_re = p["s5_lambda_re"].reshape(G, 1, S5_STATE)
    lam_im = p["s5_lambda_im"].reshape(G, 1, S5_STATE)
    log_dt = p["s5_log_dt"].reshape(G, 1, 1)
    bt_re = p["s5_b_re"][0].transpose(0, 2, 1)
    bt_im = p["s5_b_im"][0].transpose(0, 2, 1)
    ar, ai, bbr, bbi = _s5_discretize(lam_re, lam_im, log_dt, bt_re, bt_im)
    mats = (_block_diag_in(bbr, nj).astype(BF), _block_diag_in(bbi, nj).astype(BF),
            _block_diag_out(p["s5_c_re"][0], nj).astype(BF),
            _block_diag_out(-p["s5_c_im"][0], nj).astype(BF),
            ar.reshape(nj, 1, STATE_COLS), ai.reshape(nj, 1, STATE_COLS), dsk)
    h_seg = _to_seg(h, seg_len)
    u = _rms_fwd(h_seg, gain, F32)
    y_pre, yg, init_re, init_im = _s5_fwd(u, mats, seg_len, nseg, comm=comm_s5)
    h_out, pa, pb = _glu_fwd(yg, wa, wb, h_seg)
    disc_in = (lam_re, lam_im, log_dt, bt_re, bt_im)
    return _to_tok(h_out, seg_len), (h_seg, u, mats, y_pre, yg, init_re, init_im, pa, pb, disc_in, seg_len, nseg)


def _mixer_s5_bwd(dh, saved, gain, wa, wb, reduce_start, carry):
    h_seg, u, mats, y_pre, yg, init_re, init_im, pa, pb, disc_in, seg_len, nseg = saved
    T, D = h_seg.shape
    nj = D // LANES
    G = nj * GROUPS_PER_BLOCK
    dh_seg = _to_seg(dh, seg_len)
    dpa, dpb = _glu_bwd_gates(dh_seg, pa, pb)
    dy = _glu_bwd_y(dpa, dpb, wa, wb, y_pre)
    dwa, dwb = _mm_tn("glu_dw", yg, [dpa, dpb])
    comm = _merge_comms(reduce_start(["s5_glu_wa", "s5_glu_wb"],
                                     [dwa.reshape(N_DEV, -1, D), dwb.reshape(N_DEV, -1, D)]) + [carry])
    du, dbdr, dbdi, dcdr, dcdi, dar, dai, dd = _s5_bwd(u, dy, mats, init_re, init_im, seg_len, nseg, comm=comm)
    d_bbr = _diag_of_in(dbdr, nj)
    d_bbi = _diag_of_in(dbdi, nj)
    d_c_re = _diag_of_in(dcdr.transpose(0, 2, 1), nj)
    d_c_im = -_diag_of_in(dcdi.transpose(0, 2, 1), nj)
    dlr, dli, dld, dbr, dbi = _s5_discretize_bwd(
        *disc_in, dar.reshape(G, 1, S5_STATE), dai.reshape(G, 1, S5_STATE), d_bbr, d_bbi)
    small = {"s5_lambda_re": dlr.reshape(1, G, S5_STATE), "s5_lambda_im": dli.reshape(1, G, S5_STATE),
             "s5_log_dt": dld.reshape(1, G),
             "s5_b_re": dbr.transpose(0, 2, 1)[None], "s5_b_im": dbi.transpose(0, 2, 1)[None],
             "s5_c_re": d_c_re[None], "s5_c_im": d_c_im[None], "s5_d": dd}
    dh_in, _, dgain = _rms_bwd(du, h_seg, gain, dh_seg)
    dh_in = _to_tok(dh_in, seg_len)
    return dh_in, dh_in.astype(BF), dgain, small


def _mesh_pos():
    return lax.axis_index("x"), lax.axis_index("y"), lax.axis_index("c")


class _Gather:
    def __init__(self, srcs, slots, send_sems, recv_sems):
        self.srcs, self.slots, self.send_sems, self.recv_sems = srcs, slots, send_sems, recv_sems
        x, y, c = _mesh_pos()
        self.c = c
        self.me, self.sib = (x, y, c), (x, y, 1 - c)
        self.chips = [(1 - x, y), (x, 1 - y), (1 - x, 1 - y)]

    def copy(self, a, k, block, to, own=False):
        dst = self.slots[a].at[4 * block[0] + 2 * block[1] + block[2]]
        return pltpu.make_async_remote_copy(
            src_ref=self.srcs[a] if own else dst, dst_ref=dst, send_sem=self.send_sems.at[7 * a + k],
            recv_sem=self.recv_sems.at[7 * a + k], device_id=to, device_id_type=MESH)

    def own_copies(self, a):
        cps = [self.copy(a, 0, self.me, self.sib, own=True)]
        return cps + [self.copy(a, 1 + j, self.me, (*chip, self.c), own=True) for j, chip in enumerate(self.chips)]

    def start(self):
        for a in range(len(self.srcs)):
            for cp in self.own_copies(a):
                cp.start()

    def finish(self):
        n = len(self.srcs)
        for a in range(n):
            for j, chip in enumerate(self.chips):
                self.copy(a, 1 + j, (*chip, self.c), self.me).wait_recv()
                self.copy(a, 4 + j, (*chip, self.c), self.sib).start()
        for a in range(n):
            self.copy(a, 0, self.sib, self.me).wait_recv()
            for j, chip in enumerate(self.chips):
                self.copy(a, 4 + j, (*chip, 1 - self.c), self.me).wait_recv()
        for a in range(n):
            for cp in self.own_copies(a):
                cp.wait_send()
            for j, chip in enumerate(self.chips):
                self.copy(a, 4 + j, (*chip, self.c), self.sib).wait_send()


def _gather_comm(arrs):
    n = len(arrs)

    def local(xs, outs, sems, a):
        x, y, c = _mesh_pos()
        return pltpu.make_async_copy(xs[a], outs[a].at[4 * x + 2 * y + c], sems[2].at[a])

    def start(xs, outs, sems):
        for a in range(n):
            local(xs, outs, sems, a).start()
        _Gather(xs, outs, sems[0], sems[1]).start()

    def finish(xs, outs, sems):
        _Gather(xs, outs, sems[0], sems[1]).finish()
        for a in range(n):
            local(xs, outs, sems, a).wait()

    return _Comm(list(arrs), [jax.ShapeDtypeStruct((N_DEV,) + a.shape, a.dtype) for a in arrs],
                 [pltpu.SemaphoreType.DMA((7 * n,)), pltpu.SemaphoreType.DMA((7 * n,)),
                  pltpu.SemaphoreType.DMA((n,))], start, finish)


def _exchange_comm(parts):
    n = len(parts)

    def copies(ps, outs, sems):
        x, y, c = _mesh_pos()
        cps = []
        for a in range(n):
            for j in range(1, 4):
                to = (jnp.bitwise_xor(x, j // 2), jnp.bitwise_xor(y, j % 2), c)
                cps.append(pltpu.make_async_remote_copy(
                    src_ref=ps[a].at[j], dst_ref=outs[a].at[j - 1], send_sem=sems[0].at[3 * a + j - 1],
                    recv_sem=sems[1].at[3 * a + j - 1], device_id=to, device_id_type=MESH))
        return cps

    def start(ps, outs, sems):
        for cp in copies(ps, outs, sems):
            cp.start()

    def finish(ps, outs, sems):
        for cp in copies(ps, outs, sems):
            cp.wait()

    return _Comm(list(parts), [jax.ShapeDtypeStruct((3,) + p.shape[1:], p.dtype) for p in parts],
                 [pltpu.SemaphoreType.DMA((3 * n,)), pltpu.SemaphoreType.DMA((3 * n,))], start, finish)


def _run_comm(comm, name):
    ci, co = len(comm.ins), len(comm.outs)

    def body(*refs):
        comm.start(refs[:ci], refs[ci:ci + co], refs[ci + co:])
        comm.finish(refs[:ci], refs[ci:ci + co], refs[ci + co:])

    any_spec = pl.BlockSpec(memory_space=pl.ANY)
    comm.set_results(pl.pallas_call(
        body, in_specs=[any_spec] * ci, out_specs=[any_spec] * co, out_shape=list(comm.outs),
        scratch_shapes=list(comm.sems), name=name, compiler_params=_cparams(0))(*comm.ins))


def _pair_exchange(grads, name):
    n = len(grads)

    def body(*refs):
        gs, outs = refs[:n], refs[n:2 * n]
        send_sems, recv_sems = refs[2 * n:]
        x, y, c = _mesh_pos()
        copies = []
        for a in range(n):
            for k in range(4):
                copies.append(pltpu.make_async_remote_copy(
                    src_ref=gs[a].at[2 * k + 1 - c], dst_ref=outs[a].at[k], send_sem=send_sems.at[4 * a + k],
                    recv_sem=recv_sems.at[4 * a + k], device_id=(x, y, 1 - c), device_id_type=MESH))
        for cp in copies:
            cp.start()
        for cp in copies:
            cp.wait()

    any_spec = pl.BlockSpec(memory_space=pl.ANY)
    return pl.pallas_call(
        body, in_specs=[any_spec] * n, out_specs=[any_spec] * n,
        out_shape=[jax.ShapeDtypeStruct((4,) + g.shape[1:], g.dtype) for g in grads],
        scratch_shapes=[pltpu.SemaphoreType.DMA((4 * n,)), pltpu.SemaphoreType.DMA((4 * n,))],
        name=name, compiler_params=_cparams(0))(*grads)


def _pair_sum(grad, recv, pos):
    _, R, C = grad.shape
    br = _row_block(R, C, PAIR_SUM_ELEMS)

    def body(pos_ref, g_ref, r_ref, o_ref):
        o_ref[...] = (g_ref[...].astype(F32) + r_ref[...].astype(F32)).astype(BF)

    def chip(j, p):
        return jnp.bitwise_xor(p[1], j)

    return pl.pallas_call(
        body, grid_spec=pltpu.PrefetchScalarGridSpec(
            num_scalar_prefetch=1, grid=(4, R // br),
            in_specs=[pl.BlockSpec((None, br, C), lambda j, i, p: (2 * chip(j, p) + p[0], i, 0)),
                      pl.BlockSpec((None, br, C), lambda j, i, p: (chip(j, p), i, 0))],
            out_specs=pl.BlockSpec((None, br, C), lambda j, i, p: (j, i, 0))),
        out_shape=jax.ShapeDtypeStruct((4, R, C), BF), name="pair_sum", compiler_params=_cparams(2))(pos, grad, recv)


def _adamw(w, g, m, v):
    m = ADAM_B1 * m + (1.0 - ADAM_B1) * g
    v = ADAM_B2 * v + (1.0 - ADAM_B2) * (g * g)
    m_hat = m / (1.0 - ADAM_B1 ** ADAM_STEP)
    v_hat = v / (1.0 - ADAM_B2 ** ADAM_STEP)
    return -ADAM_LR * (m_hat / (jnp.sqrt(v_hat) + ADAM_EPS) + ADAM_WD * w), m, v


def _adamw_piece(w, m, v, piece, part, recv, bufs):
    _, R, C = w.shape
    br = _row_block(R, C)

    def body(w_ref, m_ref, v_ref, p_ref, r_ref, b0, b1, b2, b3, g_ref, d_ref, nm_ref, nv_ref):
        g = p_ref[...].astype(F32)
        for j in range(3):
            g = g + r_ref[j].astype(F32)
        d, nm, nv = _adamw(w_ref[...], g, m_ref[...], v_ref[...])
        g_ref[...] = g
        d_ref[...] = d
        nm_ref[...] = nm
        nv_ref[...] = nv

    row = pl.BlockSpec((None, br, C), lambda i: (piece, i, 0))
    any_spec = pl.BlockSpec(memory_space=pl.ANY)
    return pl.pallas_call(
        body, grid=(R // br,),
        in_specs=[row, row, row, pl.BlockSpec((None, br, C), lambda i: (0, i, 0)),
                  pl.BlockSpec((3, br, C), lambda i: (0, i, 0))] + [any_spec] * 4,
        out_specs=[row] * 4, out_shape=[jax.ShapeDtypeStruct(w.shape, F32)] * 4,
        input_output_aliases={5: 0, 6: 1, 7: 2, 8: 3}, name="adamw_piece",
        compiler_params=_cparams(1))(w, m, v, part, recv, *bufs)


def _all_reduce_small(x):
    rows = x.shape[0]

    def body(x_ref, o_ref, buf, send_sems, recv_sems):
        xp, yp, cp = _mesh_pos()
        buf[4 * xp + 2 * yp + cp] = x_ref[...]
        gather = _Gather([x_ref], [buf], send_sems, recv_sems)
        gather.start()
        gather.finish()
        acc = buf[0]
        for d in range(1, N_DEV):
            acc = acc + buf[d]
        o_ref[...] = acc

    vm = pl.BlockSpec(memory_space=pltpu.VMEM)
    return pl.pallas_call(
        body, in_specs=[vm], out_specs=vm, out_shape=jax.ShapeDtypeStruct(x.shape, F32),
        scratch_shapes=[pltpu.VMEM((N_DEV, rows, LANES), F32), pltpu.SemaphoreType.DMA((7,)),
                        pltpu.SemaphoreType.DMA((7,))],
        name="all_reduce_small", compiler_params=_cparams(0))(x)


def _adamw_small(w, g, m, v):
    def body(w_ref, g_ref, m_ref, v_ref, d_ref, nm_ref, nv_ref):
        d, nm, nv = _adamw(w_ref[...], g_ref[...], m_ref[...], v_ref[...])
        d_ref[...] = d
        nm_ref[...] = nm
        nv_ref[...] = nv

    sh = jax.ShapeDtypeStruct(w.shape, F32)
    return pl.pallas_call(body, out_shape=[sh] * 3, name="adamw_small", compiler_params=_cparams(0))(w, g, m, v)


def _pack(arrs):
    flat = jnp.concatenate([a.reshape(-1).astype(F32) for a in arrs])
    rows = -(-flat.shape[0] // (SUBLANES * LANES)) * SUBLANES
    return jnp.pad(flat, (0, rows * LANES - flat.shape[0])).reshape(rows, LANES)


def _unpack(buf, shapes):
    flat = buf.reshape(-1)
    out, off = [], 0
    for s in shapes:
        n = 1
        for d in s:
            n *= d
        out.append(flat[off:off + n].reshape(s))
        off += n
    return out


BIG = ("ffn_w1", "ffn_w3", "ffn_w2", "ab_w_in", "ab_w_out", "s5_glu_wa", "s5_glu_wb")
NAMES = ("ln_ffn_pre", "ln_mix", "ln_ffn_post", "ln_final", "ffn_w1", "ffn_w3", "ffn_w2", "ab_w_in",
         "ab_conv_w", "ab_w_out", "s5_lambda_re", "s5_lambda_im", "s5_log_dt", "s5_b_re", "s5_b_im",
         "s5_c_re", "s5_c_im", "s5_d", "s5_glu_wa", "s5_glu_wb")


def kernel(x, ln_ffn_pre, ln_mix, ln_ffn_post, ln_final, ffn_w1, ffn_w3, ffn_w2, ab_w_in, ab_conv_w, ab_w_out, s5_lambda_re, s5_lambda_im, s5_log_dt, s5_b_re, s5_b_im, s5_c_re, s5_c_im, s5_d, s5_glu_wa, s5_glu_wb, loss_target, m_ln_ffn_pre, m_ln_mix, m_ln_ffn_post, m_ln_final, m_ffn_w1, m_ffn_w3, m_ffn_w2, m_ab_w_in, m_ab_conv_w, m_ab_w_out, m_s5_lambda_re, m_s5_lambda_im, m_s5_log_dt, m_s5_b_re, m_s5_b_im, m_s5_c_re, m_s5_c_im, m_s5_d, m_s5_glu_wa, m_s5_glu_wb, v_ln_ffn_pre, v_ln_mix, v_ln_ffn_post, v_ln_final, v_ffn_w1, v_ffn_w3, v_ffn_w2, v_ab_w_in, v_ab_conv_w, v_ab_w_out, v_s5_lambda_re, v_s5_lambda_im, v_s5_log_dt, v_s5_b_re, v_s5_b_im, v_s5_c_re, v_s5_c_im, v_s5_d, v_s5_glu_wa, v_s5_glu_wb):
    w = dict(zip(NAMES, (ln_ffn_pre, ln_mix, ln_ffn_post, ln_final, ffn_w1, ffn_w3, ffn_w2, ab_w_in, ab_conv_w,
                         ab_w_out, s5_lambda_re, s5_lambda_im, s5_log_dt, s5_b_re, s5_b_im, s5_c_re, s5_c_im,
                         s5_d, s5_glu_wa, s5_glu_wb)))
    mom = dict(zip(NAMES, (m_ln_ffn_pre, m_ln_mix, m_ln_ffn_post, m_ln_final, m_ffn_w1, m_ffn_w3, m_ffn_w2,
                           m_ab_w_in, m_ab_conv_w, m_ab_w_out, m_s5_lambda_re, m_s5_lambda_im, m_s5_log_dt,
                           m_s5_b_re, m_s5_b_im, m_s5_c_re, m_s5_c_im, m_s5_d, m_s5_glu_wa, m_s5_glu_wb)))
    var = dict(zip(NAMES, (v_ln_ffn_pre, v_ln_mix, v_ln_ffn_post, v_ln_final, v_ffn_w1, v_ffn_w3, v_ffn_w2,
                           v_ab_w_in, v_ab_conv_w, v_ab_w_out, v_s5_lambda_re, v_s5_lambda_im, v_s5_log_dt,
                           v_s5_b_re, v_s5_b_im, v_s5_c_re, v_s5_c_im, v_s5_d, v_s5_glu_wa, v_s5_glu_wb)))
    nb, seq, D = x.shape
    T = nb * seq
    assert ln_mix.shape[0] == 2 and ab_w_in.shape[0] == 1 and s5_glu_wa.shape[0] == 1
    xc, yc, cc = _mesh_pos()
    dev = 4 * xc + 2 * yc + cc
    pos = jnp.stack([cc, 2 * xc + yc]).astype(jnp.int32)
    bq = min(ATTN_TILE, seq)
    tabs =_rope_tables(seq) + (_branch_bias(seq // bq, bq),)

    def ffn_piece(k, li, fj):
        return w[k][li, fj].astype(BF)

    g0 = _gather_comm([ffn_piece("ffn_w1", 0, 0), ffn_piece("ffn_w3", 0, 0), ab_conv_w[0], s5_d])
    _run_comm(g0, "gather_first")
    w1, w3 = {(0, 0): g0.results[0]}, {(0, 0): g0.results[1]}
    w2 = {}
    conv_w = g0.results[2].transpose(1, 0, 2).reshape(3, -1)
    dsk = g0.results[3].reshape(1, D)
    gains = {k: [w[k][i:i + 1] for i in range(2)] for k in ("ln_ffn_pre", "ln_mix", "ln_ffn_post")}

    h = x.reshape(T, D)
    saved = {}

    def ffn_fwd(h, gain, key, tag, comm_up, comm_down, after_up):
        n = _rms_fwd(h, gain, BF)
        t1, t3, g = _ffn_up(n, w1[key], w3[key], comm=comm_up)
        after_up()
        saved[tag] = (h, n, t1, t3, g)
        return _ffn_down(g, w2[key], h, comm=comm_down)

    c_up = _gather_comm([ffn_piece("ffn_w2", 0, 0), ab_w_out[0].astype(BF)])
    c_dn = _gather_comm([ab_w_in[0].astype(BF)])
    h = ffn_fwd(h, gains["ln_ffn_pre"][0], (0, 0), "pre0", c_up, c_dn,
                lambda: w2.update({(0, 0): c_up.results[0]}))
    wout = c_up.results[1].reshape(-1, D)
    wing = c_dn.results[0]
    c_proj = _gather_comm([ffn_piece("ffn_w1", 0, 1)])
    c_attn = _gather_comm([ffn_piece("ffn_w3", 0, 1), s5_glu_wa[0].astype(BF)])
    c_out = _gather_comm([s5_glu_wb[0].astype(BF)])
    h, saved["mix0"] = _mixer_ab_fwd(h, gains["ln_mix"][0], wing, conv_w, wout, tabs, nb, seq, c_proj, c_attn, c_out)
    w1[(0, 1)] = c_proj.results[0]
    w3[(0, 1)] = c_attn.results[0]
    wa = c_attn.results[1].reshape(-1, D)
    wb = c_out.results[0].reshape(-1, D)
    c_up2 = _gather_comm([ffn_piece("ffn_w2", 0, 1), ffn_piece("ffn_w1", 1, 0)])
    c_dn = _gather_comm([ffn_piece("ffn_w3", 1, 0)])
    h = ffn_fwd(h, gains["ln_ffn_post"][0], (0, 1), "post0", c_up2, c_dn,
                lambda: w2.update({(0, 1): c_up2.results[0]}))
    w1[(1, 0)] = c_up2.results[1]
    w3[(1, 0)] = c_dn.results[0]
    c_up3 = _gather_comm([ffn_piece("ffn_w2", 1, 0), ffn_piece("ffn_w1", 1, 1)])
    c_dn = _gather_comm([ffn_piece("ffn_w3", 1, 1)])
    h = ffn_fwd(h, gains["ln_ffn_pre"][1], (1, 0), "pre1", c_up3, c_dn,
                lambda: w2.update({(1, 0): c_up3.results[0]}))
    w1[(1, 1)] = c_up3.results[1]
    w3[(1, 1)] = c_dn.results[0]
    c_s5 = _gather_comm([ffn_piece("ffn_w2", 1, 1)])
    h, saved["mix1"] = _mixer_s5_fwd(h, gains["ln_mix"][1], w, dsk, wa, wb, nb, seq, c_s5)
    w2[(1, 1)] = c_s5.results[0]
    h = ffn_fwd(h, gains["ln_ffn_post"][1], (1, 1), "post1", None, None, lambda: None)
    dh, dhb, d_ln_final, loss_part = _loss_head(h, ln_final.reshape(1, D), loss_target.reshape(T, D))
    loss = lax.psum(loss_part[0, 0], ("x", "y", "c"))

    reduced = {}

    def reduce_start(names, grads):
        recv = _pair_exchange(grads, "pair_exchange")
        comms = []
        for nm, g, r in zip(names, grads, recv):
            part = _pair_sum(g, r, pos)
            comms.append(_exchange_comm([part]))
            reduced[nm] = (part, comms[-1])
        return comms

    def ffn_bwd(dh, dhb, key, tag, gain, carry, is_last=False):
        h_in, n, t1, t3, g = saved[tag]
        da1, da3 = _ffn_bwd_hidden(dhb, w2[key], t1, t3, comm=carry)
        c2, = reduce_start([("ffn_w2",) + key], [_ffn_dw2(g, dhb)])
        dw1, dw3 = _ffn_dw13(n, da1, da3, comm=c2)
        c1, c3 = reduce_start([("ffn_w1",) + key, ("ffn_w3",) + key], [dw1, dw3])
        dn = _ffn_dn(da1, da3, w1[key], w3[key], comm=_merge_comms([c1, c3]) if is_last else c1)
        return list(_rms_bwd(dn, h_in, gain, dh)) + [None if is_last else c3]

    g_small = {"ln_final": d_ln_final.reshape(D)}
    g_ln = {k: [None, None] for k in gains}
    dh, dhb, g_ln["ln_ffn_post"][1], carry = ffn_bwd(dh, dhb, (1, 1), "post1", gains["ln_ffn_post"][1], None)
    dh, dhb, g_ln["ln_mix"][1], s5_small = _mixer_s5_bwd(
        dh, saved["mix1"], gains["ln_mix"][1], wa, wb, reduce_start, carry)
    g_small.update(s5_small)
    dh, dhb, g_ln["ln_ffn_pre"][1], carry = ffn_bwd(dh, dhb, (1, 0), "pre1", gains["ln_ffn_pre"][1], None)
    dh, dhb, g_ln["ln_ffn_post"][0], carry = ffn_bwd(dh, dhb, (0, 1), "post0", gains["ln_ffn_post"][0], carry)
    dh, dhb, g_ln["ln_mix"][0], g_small["ab_conv_w"], carry = _mixer_ab_bwd(
        dh, dhb, saved["mix0"], gains["ln_mix"][0], wing, conv_w, wout, tabs, nb, seq, reduce_start, carry)
    dh, dhb, g_ln["ln_ffn_pre"][0], _ = ffn_bwd(dh, dhb, (0, 0), "pre0", gains["ln_ffn_pre"][0], carry, is_last=True)
    grad_x = dh.reshape(nb, seq, D)
    for k in g_ln:
        g_small[k] = jnp.concatenate(g_ln[k], axis=0)

    out = {}
    for k in BIG:
        cols = w[k].shape[-1]
        pieces = [(li, fj) for li in range(2) for fj in range(2)] if w[k].ndim == 4 else [None]
        view = (len(pieces), -1, cols)
        w3d, m3d, v3d = w[k].reshape(view), mom[k].reshape(view), var[k].reshape(view)
        bufs = [lax.empty(w3d.shape, F32) for _ in range(4)]
        for q, key in enumerate(pieces):
            part, comm = reduced[k if key is None else (k,) + key]
            bufs = _adamw_piece(w3d, m3d, v3d, q, part, comm.results[0], bufs)
        out[k] = [t.reshape(w[k].shape) for t in bufs]

    small_names = [k for k in NAMES if k not in BIG]
    red = _unpack(_all_reduce_small(_pack([g_small[k] for k in small_names])),
                  [g_small[k].shape for k in small_names])
    g_red = dict(zip(small_names, red))
    cw = w["ab_conv_w"].shape[-1]
    g_red["ab_conv_w"] = lax.dynamic_slice_in_dim(g_red["ab_conv_w"], dev * cw, cw, axis=1)[None]
    dsz = w["s5_d"].shape[-1]
    g_red["s5_d"] = lax.dynamic_slice_in_dim(g_red["s5_d"].reshape(1, -1), dev * dsz, dsz, axis=1)
    shapes = [w[k].shape for k in small_names]
    g_red = {k: g_red[k].reshape(w[k].shape) for k in small_names}
    d_s, m_s, v_s = _adamw_small(_pack([w[k] for k in small_names]), _pack([g_red[k] for k in small_names]),
                                 _pack([mom[k] for k in small_names]), _pack([var[k] for k in small_names]))
    for k, d, nm, nv in zip(small_names, _unpack(d_s, shapes), _unpack(m_s, shapes), _unpack(v_s, shapes)):
        out[k] = [g_red[k], d, nm, nv]

    return (loss, grad_x, *[out[k][0] for k in NAMES], *[out[k][1] for k in NAMES],
            *[out[k][2] for k in NAMES], *[out[k][3] for k in NAMES])
```

```python
import jax
import jax.numpy as jnp
from jax import lax
from jax.experimental import pallas as pl
from jax.experimental.pallas import tpu as pltpu

F32, BF = jnp.float32, jnp.bfloat16
N_DEV = 8
MESH = pl.DeviceIdType.MESH
LANES = 128
SUBLANES = 8
VMEM_LIMIT = 56 * 2 ** 20
ROW_TILE = 512
FFN_ROW_TILE = 1024
COL_TILE = 512
ATTN_TILE = 512
SCAN_UNROLL = 4
ELEMS_PER_BLOCK = 256 * 1024
PAIR_SUM_ELEMS = 2048 * 1024
RMS_EPS = 1e-6
ROPE_THETA = 10000.0
NEG_INF = -1e30
S5_STATE = 64
S5_GROUP = 16
GROUPS_PER_BLOCK = LANES // S5_GROUP
STATE_COLS = GROUPS_PER_BLOCK * S5_STATE
DILATED_PATTERN = ((128, 1), (512, 4), (2048, 16))
ADAM_LR, ADAM_B1, ADAM_B2, ADAM_EPS, ADAM_WD, ADAM_STEP = 0.001, 0.9, 0.999, 1e-08, 0.01, 10
GELU_C = 0.7978845608028654
GELU_A = 0.044715


def _cparams(n_grid, vmem=VMEM_LIMIT):
    sem = ("arbitrary",) * n_grid if n_grid else None
    return pltpu.CompilerParams(dimension_semantics=sem, vmem_limit_bytes=vmem)


def _sig(x):
    return 1.0 / (1.0 + jnp.exp(-x))


def _gelu(x):
    return 0.5 * x * (1.0 + jnp.tanh(GELU_C * (x + GELU_A * x * x * x)))


def _gelu_grad(x):
    t = jnp.tanh(GELU_C * (x + GELU_A * x * x * x))
    return 0.5 * (1.0 + t) + 0.5 * x * (1.0 - t * t) * GELU_C * (1.0 + 3.0 * GELU_A * x * x)


def _dot(a, b, dims):
    a = a if a.dtype == BF else a.astype(BF)
    b = b if b.dtype == BF else b.astype(BF)
    return lax.dot_general(a, b, (dims, ((), ())), preferred_element_type=F32)


NN = ((1,), (0,))
NT = ((1,), (1,))
TN = ((0,), (0,))


def _row_block(rows, cols, elems=ELEMS_PER_BLOCK, mult=16):
    cap = max(mult, elems // cols)
    best = None
    for b in range(mult, min(rows, cap) + 1, mult):
        if rows % b == 0:
            best = b
    return rows if best is None else best


class _Comm:
    def __init__(self, ins, outs, sems, start, finish, members=()):
        self.ins, self.outs, self.sems, self.start, self.finish = ins, outs, sems, start, finish
        self.members = members
        self.results = None

    def set_results(self, res):
        self.results = list(res)
        off = 0
        for m in self.members:
            m.set_results(res[off:off + len(m.outs)])
            off += len(m.outs)


def _merge_comms(comms):
    comms = [c for c in comms if c is not None]
    if len(comms) < 2:
        return comms[0] if comms else None

    def each(fn_name, ins, outs, sems):
        i = o = s = 0
        for c in comms:
            ni, no, ns = len(c.ins), len(c.outs), len(c.sems)
            getattr(c, fn_name)(ins[i:i + ni], outs[o:o + no], sems[s:s + ns])
            i, o, s = i + ni, o + no, s + ns

    return _Comm([a for c in comms for a in c.ins], [a for c in comms for a in c.outs],
                 [a for c in comms for a in c.sems],
                 lambda ins, outs, sems: each("start", ins, outs, sems),
                 lambda ins, outs, sems: each("finish", ins, outs, sems), members=tuple(comms))


def _call(body, name, grid, in_specs, out_specs, out_shape, args, scratch=(), comm=None):
    in_specs, out_specs, out_shape, scratch = list(in_specs), list(out_specs), list(out_shape), list(scratch)
    if comm is None:
        return pl.pallas_call(body, grid=grid, in_specs=in_specs, out_specs=out_specs, out_shape=out_shape,
                              scratch_shapes=scratch, name=name, compiler_params=_cparams(len(grid)))(*args)
    n_in, n_out, n_sc = len(in_specs), len(out_specs), len(scratch)
    ci, co = len(comm.ins), len(comm.outs)

    def hosted(*refs):
        ins, refs = refs[:n_in], refs[n_in:]
        cins, refs = refs[:ci], refs[ci:]
        outs, refs = refs[:n_out], refs[n_out:]
        couts, refs = refs[:co], refs[co:]
        sc, csems = refs[:n_sc], refs[n_sc:]
        first = last = None
        for d, n in enumerate(grid):
            p = pl.program_id(d)
            first = (p == 0) if first is None else first & (p == 0)
            last = (p == n - 1) if last is None else last & (p == n - 1)

        @pl.when(first)
        def _():
            comm.start(cins, couts, csems)

        body(*ins, *outs, *sc)

        @pl.when(last)
        def _():
            comm.finish(cins, couts, csems)

    any_spec = pl.BlockSpec(memory_space=pl.ANY)
    res = pl.pallas_call(
        hosted, grid=grid, in_specs=in_specs + [any_spec] * ci, out_specs=out_specs + [any_spec] * co,
        out_shape=out_shape + list(comm.outs), scratch_shapes=scratch + list(comm.sems), name=name,
        compiler_params=_cparams(len(grid)))(*args, *comm.ins)
    comm.set_results(res[n_out:])
    return list(res[:n_out])


def _mm(name, grid, operands, pairs, n_acc, acc_shape, extras, outs, epilogue, comm=None, nrow=1, ncol=1):
    nk = grid[2]
    n_op, n_ex, n_out = len(operands), len(extras), len(outs)

    def part_of(ref, dim, t, n):
        if n == 1:
            return ref
        size = ref.shape[dim] // n
        idx = [slice(None)] * len(ref.shape)
        idx[dim] = pl.ds(t * size, size)
        return ref.at[tuple(idx)]

    def tile_of(ref, r, c):
        return part_of(part_of(ref, 0, r, nrow), 1, c, ncol)

    def products(op, r, c):
        parts = [None] * n_acc
        for ai, bi, dims, ci in pairs:
            a = part_of(op[ai], 1 - dims[0][0], r, nrow)
            b = part_of(op[bi], 1 - dims[1][0], c, ncol)
            d = _dot(a[...], b[...], dims)
            parts[ci] = d if parts[ci] is None else parts[ci] + d
        return parts

    def body(*refs):
        op = refs[:n_op]
        ex = refs[n_op:n_op + n_ex]
        out = refs[n_op + n_ex:n_op + n_ex + n_out]
        acc = refs[n_op + n_ex + n_out:]
        tiles = [(r, c) for r in range(nrow) for c in range(ncol)]

        def views(refs_, t):
            return [tile_of(q, *t) for q in refs_]

        if nk == 1:
            parts = products(op, *tiles[0])
            for q, t in enumerate(tiles):
                nxt = products(op, *tiles[q + 1]) if q + 1 < len(tiles) else None
                epilogue(parts, views(ex, t), views(out, t))
                parts = nxt
            return
        k = pl.program_id(2)

        @pl.when(k == 0)
        def _():
            for q in acc:
                q[...] = jnp.zeros_like(q)

        for t in tiles:
            parts = products(op, *t)
            for q, p in zip(views(acc, t), parts):
                q[...] += p

        @pl.when(k == nk - 1)
        def _():
            for t in tiles:
                epilogue([q[...] for q in views(acc, t)], views(ex, t), views(out, t))

    return _call(body, name, grid, [s for _, s in operands] + [s for _, s in extras], [s for _, s in outs],
                 [sh for sh, _ in outs], [a for a, _ in operands] + [a for a, _ in extras],
                 scratch=[pltpu.VMEM(acc_shape, F32) for _ in range(n_acc if nk > 1 else 0)], comm=comm)


def _to_seg(a, seg_len):
    T, D = a.shape
    return a.reshape(SUBLANES, seg_len, D).transpose(1, 0, 2).reshape(T, D)


def _to_tok(a, seg_len):
    T, D = a.shape
    return a.reshape(seg_len, SUBLANES, D).transpose(1, 0, 2).reshape(T, D)


def _rms_fwd(h, gain, out_dtype):
    T, D = h.shape
    bm = min(ROW_TILE, T)

    def body(h_ref, g_ref, o_ref):
        x = h_ref[...]
        r = lax.rsqrt(jnp.mean(x * x, axis=-1, keepdims=True) + RMS_EPS)
        o_ref[...] = (x * r * g_ref[...]).astype(out_dtype)

    row = pl.BlockSpec((bm, D), lambda i: (i, 0))
    return pl.pallas_call(
        body, grid=(T // bm,), in_specs=[row, pl.BlockSpec((1, D), lambda i: (0, 0))],
        out_specs=row, out_shape=jax.ShapeDtypeStruct((T, D), out_dtype), name="rms_fwd",
        compiler_params=_cparams(1))(h, gain)


def _rms_bwd_rows(dn, x, g):
    r = lax.rsqrt(jnp.mean(x * x, axis=-1, keepdims=True) + RMS_EPS)
    xh = x * r
    dng = dn * g
    dx = r * (dng - xh * jnp.mean(dng * xh, axis=-1, keepdims=True))
    return dx, jnp.sum(dn * xh, axis=0, keepdims=True)


def _rms_bwd(dn, h, gain, dh_up):
    T, D = h.shape
    bm = min(ROW_TILE, T)

    def body(dn_ref, h_ref, g_ref, up_ref, dh_ref, dhb_ref, dg_ref):
        dx, dg = _rms_bwd_rows(dn_ref[...], h_ref[...], g_ref[...])
        dh = up_ref[...] + dx
        dh_ref[...] = dh
        dhb_ref[...] = dh.astype(BF)

        @pl.when(pl.program_id(0) == 0)
        def _():
            dg_ref[...] = jnp.zeros_like(dg_ref)

        dg_ref[...] += dg

    row = pl.BlockSpec((bm, D), lambda i: (i, 0))
    vec = pl.BlockSpec((1, D), lambda i: (0, 0))
    return pl.pallas_call(
        body, grid=(T // bm,), in_specs=[row, row, vec, row], out_specs=[row, row, vec],
        out_shape=[jax.ShapeDtypeStruct((T, D), F32), jax.ShapeDtypeStruct((T, D), BF),
                   jax.ShapeDtypeStruct((1, D), F32)],
        name="rms_bwd", compiler_params=_cparams(1))(dn, h, gain, dh_up)


def _loss_head(h, gain, target):
    T, D = h.shape
    bm = min(ROW_TILE, T)

    def body(h_ref, g_ref, t_ref, dh_ref, dhb_ref, dg_ref, loss_ref):
        x = h_ref[...]
        g = g_ref[...]
        r = lax.rsqrt(jnp.mean(x * x, axis=-1, keepdims=True) + RMS_EPS)
        err = x * r * g - t_ref[...]
        part = 0.5 * jnp.sum(jnp.sum(err * err, axis=-1, keepdims=True), axis=0, keepdims=True) / D
        dx, dg = _rms_bwd_rows(err / D, x, g)
        dh_ref[...] = dx
        dhb_ref[...] = dx.astype(BF)

        @pl.when(pl.program_id(0) == 0)
        def _():
            dg_ref[...] = jnp.zeros_like(dg_ref)
            loss_ref[...] = jnp.zeros_like(loss_ref)

        dg_ref[...] += dg
        loss_ref[...] += jnp.broadcast_to(part, loss_ref.shape)

    row = pl.BlockSpec((bm, D), lambda i: (i, 0))
    vec = pl.BlockSpec((1, D), lambda i: (0, 0))
    return pl.pallas_call(
        body, grid=(T // bm,), in_specs=[row, vec, row],
        out_specs=[row, row, vec, pl.BlockSpec((SUBLANES, LANES), lambda i: (0, 0))],
        out_shape=[jax.ShapeDtypeStruct((T, D), F32), jax.ShapeDtypeStruct((T, D), BF),
                   jax.ShapeDtypeStruct((1, D), F32), jax.ShapeDtypeStruct((SUBLANES, LANES), F32)],
        name="loss_head", compiler_params=_cparams(1))(h, gain, target)


def _ffn_up(n, w1g, w3g, comm=None):
    T, D = n.shape
    fs = w1g.shape[-1]
    bm = min(FFN_ROW_TILE, T)
    wspec = pl.BlockSpec((None, D, fs), lambda s, i, k: (s, 0, 0))
    ospec = pl.BlockSpec((None, bm, fs), lambda s, i, k: (s, i, 0))

    def epi(accs, ex, outs):
        a1, a3 = accs
        sg = _sig(a1)
        silu = a1 * sg
        outs[0][...] = (a3 * sg * (1.0 + a1 * (1.0 - sg))).astype(BF)
        outs[1][...] = silu.astype(BF)
        outs[2][...] = (silu * a3).astype(BF)

    sh = jax.ShapeDtypeStruct((N_DEV, T, fs), BF)
    return _mm("ffn_up", (N_DEV, T // bm, 1),
               [(n, pl.BlockSpec((bm, D), lambda s, i, k: (i, 0))), (w1g, wspec), (w3g, wspec)],
               [(0, 1, NN, 0), (0, 2, NN, 1)], 2, None, [], [(sh, ospec)] * 3, epi, comm=comm,
               nrow=max(1, bm // ROW_TILE))


def _ffn_down(g, w2g, h, comm=None):
    _, T, fs = g.shape
    D = h.shape[1]
    bm = min(ROW_TILE, T)
    row = pl.BlockSpec((bm, D), lambda i, j, s: (i, 0))

    def epi(accs, ex, outs):
        outs[0][...] = ex[0][...] + 0.5 * accs[0]

    return _mm("ffn_down", (T // bm, 1, N_DEV),
               [(g, pl.BlockSpec((None, bm, fs), lambda i, j, s: (s, i, 0))),
                (w2g, pl.BlockSpec((None, fs, D), lambda i, j, s: (s, 0, 0)))],
               [(0, 1, NN, 0)], 1, (bm, D), [(h, row)],
               [(jax.ShapeDtypeStruct((T, D), F32), row)], epi, comm=comm, ncol=max(1, D // COL_TILE))[0]


def _ffn_bwd_hidden(dhb, w2g, t1, t3, comm=None):
    T, D = dhb.shape
    fs = t1.shape[-1]
    bm = min(FFN_ROW_TILE, T)
    aspec = pl.BlockSpec((None, bm, fs), lambda s, i, k: (s, i, 0))

    def epi(accs, ex, outs):
        dg = 0.5 * accs[0]
        outs[0][...] = (dg * ex[0][...].astype(F32)).astype(BF)
        outs[1][...] = (dg * ex[1][...].astype(F32)).astype(BF)

    sh = jax.ShapeDtypeStruct((N_DEV, T, fs), BF)
    return _mm("ffn_bwd_hidden", (N_DEV, T // bm, 1),
               [(dhb, pl.BlockSpec((bm, D), lambda s, i, k: (i, 0))),
                (w2g, pl.BlockSpec((None, fs, D), lambda s, i, k: (s, 0, 0)))],
               [(0, 1, NT, 0)], 1, None, [(t1, aspec), (t3, aspec)], [(sh, aspec)] * 2, epi, comm=comm,
               nrow=max(1, bm // ROW_TILE))


def _ffn_dw2(g, dhb):
    _, T, fs = g.shape
    D = dhb.shape[1]
    bn = min(COL_TILE, D)

    def epi(accs, ex, outs):
        outs[0][...] = (0.5 * accs[0]).astype(BF)

    return _mm("ffn_dw2", (N_DEV, D // bn, 1),
               [(g, pl.BlockSpec((None, T, fs), lambda s, j, k: (s, 0, 0))),
                (dhb, pl.BlockSpec((T, bn), lambda s, j, k: (0, j)))],
               [(0, 1, TN, 0)], 1, None, [],
               [(jax.ShapeDtypeStruct((N_DEV, fs, D), BF), pl.BlockSpec((None, fs, bn), lambda s, j, k: (s, 0, j)))],
               epi)[0]


def _ffn_dw13(n, da1, da3, comm=None):
    T, D = n.shape
    fs = da1.shape[-1]
    bn = min(COL_TILE, D)
    dspec = pl.BlockSpec((None, T, fs), lambda s, j, k: (s, 0, 0))
    ospec = pl.BlockSpec((None, fs, bn), lambda s, j, k: (s, 0, j))

    def epi(accs, ex, outs):
        outs[0][...] = accs[0].astype(BF)
        outs[1][...] = accs[1].astype(BF)

    sh = jax.ShapeDtypeStruct((N_DEV, fs, D), BF)
    return _mm("ffn_dw13", (N_DEV, D // bn, 1),
               [(da1, dspec), (da3, dspec), (n, pl.BlockSpec((T, bn), lambda s, j, k: (0, j)))],
               [(0, 2, TN, 0), (1, 2, TN, 1)], 2, None, [], [(sh, ospec)] * 2, epi, comm=comm)


def _ffn_dn(da1, da3, w1g, w3g, comm=None):
    _, T, fs = da1.shape
    D = w1g.shape[-2]
    bm = min(FFN_ROW_TILE, T)
    dspec = pl.BlockSpec((None, bm, fs), lambda i, j, s: (s, i, 0))
    wspec = pl.BlockSpec((None, D, fs), lambda i, j, s: (s, 0, 0))
    row = pl.BlockSpec((bm, D), lambda i, j, s: (i, 0))

    def epi(accs, ex, outs):
        outs[0][...] = accs[0]

    return _mm("ffn_dn", (T // bm, 1, N_DEV),
               [(da1, dspec), (w1g, wspec), (da3, dspec), (w3g, wspec)],
               [(0, 1, NT, 0), (2, 3, NT, 0)], 1, (bm, D), [],
               [(jax.ShapeDtypeStruct((T, D), F32), row)], epi, comm=comm,
               nrow=max(1, bm // ROW_TILE), ncol=max(1, D // COL_TILE))[0]


def _rope_tables(seq):
    half = LANES // 2
    inv = ROPE_THETA ** (-jnp.arange(0, half, dtype=F32) * 2.0 / LANES)
    ang = jnp.arange(seq, dtype=F32)[:, None] * inv[None, :]
    cos, sin = jnp.cos(ang), jnp.sin(ang)
    return jnp.concatenate([cos, cos], axis=1), jnp.concatenate([-sin, sin], axis=1)


def _branch_bias(nq, bq):
    d = (jnp.arange(nq)[:, None, None] * bq + jnp.arange(bq)[None, :, None]
         - jnp.arange(bq)[None, None, :])
    mult = jnp.zeros(d.shape, F32)
    for window, dil in DILATED_PATTERN:
        mult = mult + ((d >= 0) & (d % dil == 0) & (d <= window)).astype(F32)
    return jnp.where(mult > 0, jnp.log(jnp.maximum(mult, 1.0)), NEG_INF)


def _proj_fwd(u, wing, comm=None):
    T, D = u.shape
    ws = wing.shape[-1]
    bm = min(ROW_TILE, T)

    def epi(accs, ex, outs):
        outs[0][...] = accs[0]

    return _mm("proj_fwd", (N_DEV, T // bm, 1),
               [(u, pl.BlockSpec((bm, D), lambda s, i, k: (i, 0))),
                (wing, pl.BlockSpec((None, D, ws), lambda s, i, k: (s, 0, 0)))],
               [(0, 1, NN, 0)], 1, None, [],
               [(jax.ShapeDtypeStruct((T, N_DEV * ws), F32),
                 pl.BlockSpec((bm, ws), lambda s, i, k: (i, s)))], epi, comm=comm)[0]


def _rope_fwd(proj, cosf, sinf, seq, nh):
    T = proj.shape[0]
    bs = min(ROW_TILE, seq)
    nst = seq // bs
    scale = LANES ** -0.5

    def body(x_ref, c_ref, s_ref, o_ref):
        j = pl.program_id(1)
        t = x_ref[...]
        rot = t * c_ref[...] + pltpu.roll(t, LANES // 2, 1) * s_ref[...]
        rot = rot * jnp.where(j < nh, scale, 1.0)
        o_ref[...] = jnp.where(j < 2 * nh, rot, t).astype(BF)

    blk = pl.BlockSpec((bs, LANES), lambda r, j: (r, j))
    tab = pl.BlockSpec((bs, LANES), lambda r, j: (r % nst, 0))
    return pl.pallas_call(
        body, grid=(T // bs, 3 * nh), in_specs=[blk, tab, tab], out_specs=blk,
        out_shape=jax.ShapeDtypeStruct((T, 3 * nh * LANES), BF), name="rope_fwd",
        compiler_params=_cparams(2))(proj, cosf, sinf)


def _attn_fwd(qkv, bias, nb, seq, nh, comm=None):
    T = nb * seq
    bq = bias.shape[1]
    nq = seq // bq

    def body(q_ref, k_ref, v_ref, b_ref, o_ref, lse_ref):
        qi = pl.program_id(2)
        q = q_ref[...]

        def step(kj, carry):
            m, l, acc = carry
            rows = pl.ds(pl.multiple_of(kj * bq, bq), bq)
            s = _dot(q, k_ref[rows, :], NT) + b_ref[qi - kj]
            m_new = jnp.maximum(m, jnp.max(s, axis=1, keepdims=True))
            p = jnp.exp(s - m_new)
            alpha = jnp.exp(m - m_new)
            l = alpha * l + jnp.sum(p, axis=1, keepdims=True)
            acc = alpha * acc + _dot(p, v_ref[rows, :], NN)
            return m_new, l, acc

        init = (jnp.full((bq, 1), NEG_INF, F32), jnp.zeros((bq, 1), F32), jnp.zeros((bq, LANES), F32))
        m, l, acc = lax.fori_loop(0, qi + 1, step, init)
        o_ref[...] = (acc / l).astype(BF)
        lse_ref[...] = m + jnp.log(l)

    return _call(
        body, "attn_fwd", (nb, nh, nq),
        [pl.BlockSpec((bq, LANES), lambda b, h, i: (b * nq + i, h)),
         pl.BlockSpec((seq, LANES), lambda b, h, i: (b, nh + h)),
         pl.BlockSpec((seq, LANES), lambda b, h, i: (b, 2 * nh + h)),
         pl.BlockSpec((nq, bq, bq), lambda b, h, i: (0, 0, 0))],
        [pl.BlockSpec((bq, LANES), lambda b, h, i: (b * nq + i, h)),
         pl.BlockSpec((None, bq, 1), lambda b, h, i: (h, b * nq + i, 0))],
        [jax.ShapeDtypeStruct((T, 2 * nh * LANES), BF), jax.ShapeDtypeStruct((nh, T, 1), F32)],
        (qkv, qkv, qkv, bias), comm=comm)


def _attn_bwd_dq(qkv, cat, dcat, lse, bias, nb, seq, nh, comm=None):
    T = nb * seq
    bq = bias.shape[1]
    nq = seq // bq

    def body(q_ref, k_ref, v_ref, o_ref, do_ref, lse_ref, b_ref, dq_ref, delta_ref):
        qi = pl.program_id(2)
        q = q_ref[...]
        do = do_ref[...]
        dob = do.astype(BF)
        lse_t = lse_ref[...]
        delta = jnp.sum(do * o_ref[...].astype(F32), axis=1, keepdims=True)
        delta_ref[...] = delta

        def step(kj, dq):
            rows = pl.ds(pl.multiple_of(kj * bq, bq), bq)
            k = k_ref[rows, :]
            p = jnp.exp(_dot(q, k, NT) + b_ref[qi - kj] - lse_t)
            ds = p * (_dot(dob, v_ref[rows, :], NT) - delta)
            return dq + _dot(ds, k, NN)

        dq_ref[...] = lax.fori_loop(0, qi + 1, step, jnp.zeros((bq, LANES), F32))

    tile = pl.BlockSpec((bq, LANES), lambda b, h, i: (b * nq + i, h))
    stat = pl.BlockSpec((None, bq, 1), lambda b, h, i: (h, b * nq + i, 0))
    return _call(
        body, "attn_bwd_dq", (nb, nh, nq),
        [tile, pl.BlockSpec((seq, LANES), lambda b, h, i: (b, nh + h)),
         pl.BlockSpec((seq, LANES), lambda b, h, i: (b, 2 * nh + h)), tile, tile, stat,
         pl.BlockSpec((nq, bq, bq), lambda b, h, i: (0, 0, 0))],
        [tile, stat],
        [jax.ShapeDtypeStruct((T, nh * LANES), F32), jax.ShapeDtypeStruct((nh, T, 1), F32)],
        (qkv, qkv, qkv, cat, dcat, lse, bias), comm=comm)


def _attn_bwd_dkv(qkv, dcat, lse, delta, bias, nb, seq, nh):
    T = nb * seq
    bq = bias.shape[1]
    nq = seq // bq

    def body(k_ref, v_ref, q_ref, do_ref, lse_ref, delta_ref, b_ref, dk_ref, dv_ref):
        kj = pl.program_id(2)
        k = k_ref[...]
        v = v_ref[...]

        def step(qi, carry):
            dk, dv = carry
            rows = pl.ds(pl.multiple_of(qi * bq, bq), bq)
            q = q_ref[rows, :]
            dob = do_ref[rows, :].astype(BF)
            p = jnp.exp(_dot(q, k, NT) + b_ref[qi - kj] - lse_ref[rows, :])
            dv = dv + _dot(p, dob, TN)
            ds = p * (_dot(dob, v, NT) - delta_ref[rows, :])
            return dk + _dot(ds, q, TN), dv

        z = jnp.zeros((bq, LANES), F32)
        dk, dv = lax.fori_loop(kj, nq, step, (z, z))
        dk_ref[...] = dk
        dv_ref[...] = dv

    stat = pl.BlockSpec((None, seq, 1), lambda b, h, i: (h, b, 0))
    out = pl.BlockSpec((bq, LANES), lambda b, h, i: (b * nq + i, h))
    sh = jax.ShapeDtypeStruct((T, nh * LANES), F32)
    return pl.pallas_call(
        body, grid=(nb, nh, nq),
        in_specs=[pl.BlockSpec((bq, LANES), lambda b, h, i: (b * nq + i, nh + h)),
                  pl.BlockSpec((bq, LANES), lambda b, h, i: (b * nq + i, 2 * nh + h)),
                  pl.BlockSpec((seq, LANES), lambda b, h, i: (b, h)),
                  pl.BlockSpec((seq, LANES), lambda b, h, i: (b, h)),
                  stat, stat,
                  pl.BlockSpec((nq, bq, bq), lambda b, h, i: (0, 0, 0))],
        out_specs=[out, out], out_shape=[sh, sh],
        name="attn_bwd_dkv", compiler_params=_cparams(3))(qkv, qkv, qkv, dcat, lse, delta, bias)


def _conv_parts(gc, xin, w_ref):
    w = [w_ref[k:k + 1, :] for k in range(3)]
    u = gc * xin
    row = lax.broadcasted_iota(jnp.int32, u.shape, 0)
    u1 = jnp.where(row >= 1, pltpu.roll(u, 1, 0), 0.0)
    u2 = jnp.where(row >= 2, pltpu.roll(u, 2, 0), 0.0)
    return u, u1, u2, w[0] * u2 + w[1] * u1 + w[2] * u, w, row


def _conv_fwd(proj, conv_w, cat, nb, seq, width):
    cw = min(2 * LANES, width)
    nc = width // cw

    def body(gb_ref, gc_ref, x_ref, w_ref, cat_ref, o_ref):
        _, _, _, conv, _, _ = _conv_parts(gc_ref[...], x_ref[...], w_ref)
        o_ref[...] = (gb_ref[...] * conv).astype(BF)

    def sec(k):
        return pl.BlockSpec((seq, cw), lambda b, c: (b, k * nc + c))

    return pl.pallas_call(
        body, grid=(nb, nc),
        in_specs=[sec(3), sec(4), sec(5), pl.BlockSpec((3, cw), lambda b, c: (0, c)),
                  pl.BlockSpec(memory_space=pl.ANY)],
        out_specs=pl.BlockSpec((seq, cw), lambda b, c: (b, nc + c)),
        out_shape=jax.ShapeDtypeStruct(cat.shape, BF), input_output_aliases={4: 0},
        name="conv_fwd", compiler_params=_cparams(2))(proj, proj, proj, conv_w, cat)


def _conv_bwd(proj, conv_w, dcat, nb, seq, width):
    cw = min(2 * LANES, width)
    nc = width // cw
    T = nb * seq

    def body(gb_ref, gc_ref, x_ref, w_ref, d_ref, dgb_ref, dgc_ref, dx_ref, dw_ref):
        gc = gc_ref[...]
        xin = x_ref[...]
        u, u1, u2, conv, w, row = _conv_parts(gc, xin, w_ref)
        dsc = d_ref[...]
        dgb_ref[...] = dsc * conv
        dconv = dsc * gb_ref[...]
        d1 = jnp.where(row < seq - 1, pltpu.roll(dconv, seq - 1, 0), 0.0)
        d2 = jnp.where(row < seq - 2, pltpu.roll(dconv, seq - 2, 0), 0.0)
        du = w[2] * dconv + w[1] * d1 + w[0] * d2
        dgc_ref[...] = du * xin
        dx_ref[...] = du * gc

        @pl.when(pl.program_id(1) == 0)
        def _():
            dw_ref[...] = jnp.zeros_like(dw_ref)

        dw_ref[0:1, :] += jnp.sum(dconv * u2, axis=0, keepdims=True)
        dw_ref[1:2, :] += jnp.sum(dconv * u1, axis=0, keepdims=True)
        dw_ref[2:3, :] += jnp.sum(dconv * u, axis=0, keepdims=True)

    def sec(k):
        return pl.BlockSpec((seq, cw), lambda c, b: (b, k * nc + c))

    out = pl.BlockSpec((seq, cw), lambda c, b: (b, c))
    wsp = pl.BlockSpec((3, cw), lambda c, b: (0, c))
    sh = jax.ShapeDtypeStruct((T, width), F32)
    return pl.pallas_call(
        body, grid=(nc, nb), in_specs=[sec(3), sec(4), sec(5), wsp, sec(1)],
        out_specs=[out, out, out, wsp], out_shape=[sh, sh, sh, jax.ShapeDtypeStruct((3, width), F32)],
        name="conv_bwd", compiler_params=_cparams(2))(proj, proj, proj, conv_w, dcat)


def _assemble_dproj(dq, dk, dv, dgb, dgc, dxin, cosf, sinf, seq):
    T, width = dq.shape
    nh = width // LANES
    bs = min(256, seq)
    nst = seq // bs
    scale = LANES ** -0.5

    def body(dq_ref, dk_ref, dv_ref, dgb_ref, dgc_ref, dx_ref, c_ref, s_ref, o_ref):
        sec = pl.program_id(1)
        c = c_ref[...]
        s = s_ref[...]

        def unrope(ref, mul):
            for h in range(nh):
                cols = slice(h * LANES, (h + 1) * LANES)
                t = ref[:, cols]
                o_ref[:, cols] = ((t * c + pltpu.roll(t * s, LANES // 2, 1)) * mul).astype(BF)

        @pl.when(sec == 0)
        def _():
            unrope(dq_ref, scale)

        @pl.when(sec == 1)
        def _():
            unrope(dk_ref, 1.0)

        for k, ref in ((2, dv_ref), (3, dgb_ref), (4, dgc_ref), (5, dx_ref)):
            @pl.when(sec == k)
            def _(ref=ref):
                o_ref[...] = ref[...].astype(BF)

    blk = pl.BlockSpec((bs, width), lambda r, k: (r, 0))
    tab = pl.BlockSpec((bs, LANES), lambda r, k: (r % nst, 0))
    return pl.pallas_call(
        body, grid=(T // bs, 6), in_specs=[blk] * 6 + [tab, tab],
        out_specs=pl.BlockSpec((bs, width), lambda r, k: (r, k)),
        out_shape=jax.ShapeDtypeStruct((T, 6 * width), BF), name="assemble_dproj",
        compiler_params=_cparams(2))(dq, dk, dv, dgb, dgc, dxin, cosf, sinf)


def _res_mm(name, a, w, h, comm=None):
    T, K = a.shape
    N = w.shape[1]
    bm = min(ROW_TILE, T)
    bk = min(ROW_TILE, K)
    row = pl.BlockSpec((bm, N), lambda i, j, k: (i, 0))

    def epi(accs, ex, outs):
        outs[0][...] = ex[0][...] + accs[0]

    return _mm(name, (T // bm, 1, K // bk),
               [(a, pl.BlockSpec((bm, bk), lambda i, j, k: (i, k))),
                (w, pl.BlockSpec((bk, N), lambda i, j, k: (k, 0)))],
               [(0, 1, NN, 0)], 1, (bm, N), [(h, row)],
               [(jax.ShapeDtypeStruct((T, N), F32), row)], epi, comm=comm, ncol=max(1, N // COL_TILE))[0]


def _mm_nt(name, a, w, out_dtype):
    T, K = a.shape
    N = w.shape[0]
    bm = min(ROW_TILE, T)
    bn = min(ROW_TILE, N)

    def epi(accs, ex, outs):
        outs[0][...] = accs[0].astype(out_dtype)

    return _mm(name, (T // bm, N // bn, 1),
               [(a, pl.BlockSpec((bm, K), lambda i, j, k: (i, 0))),
                (w, pl.BlockSpec((bn, K), lambda i, j, k: (j, 0)))],
               [(0, 1, NT, 0)], 1, None, [],
               [(jax.ShapeDtypeStruct((T, N), out_dtype), pl.BlockSpec((bm, bn), lambda i, j, k: (i, j)))],
               epi)[0]


def _mm_tn(name, a, bs_list):
    T, M = a.shape
    N = bs_list[0].shape[1]
    bmr = min(COL_TILE, M)
    bn = min(COL_TILE, N)
    n = len(bs_list)

    def epi(accs, ex, outs):
        for q in range(n):
            outs[q][...] = accs[q].astype(BF)

    ops = [(a, pl.BlockSpec((T, bmr), lambda r, j, k: (0, r)))]
    ops += [(b, pl.BlockSpec((T, bn), lambda r, j, k: (0, j))) for b in bs_list]
    return _mm(name, (M // bmr, N // bn, 1), ops, [(0, 1 + q, TN, q) for q in range(n)], n, None, [],
               [(jax.ShapeDtypeStruct((M, N), BF), pl.BlockSpec((bmr, bn), lambda r, j, k: (r, j)))] * n, epi)


def _proj_bwd_x(dproj, wing):
    T = dproj.shape[0]
    _, D, ws = wing.shape
    bm = min(ROW_TILE, T)
    row = pl.BlockSpec((bm, D), lambda i, j, s: (i, 0))

    def epi(accs, ex, outs):
        outs[0][...] = accs[0]

    return _mm("proj_bwd_x", (T // bm, 1, N_DEV),
               [(dproj, pl.BlockSpec((bm, ws), lambda i, j, s: (i, s))),
                (wing, pl.BlockSpec((None, D, ws), lambda i, j, s: (s, 0, 0)))],
               [(0, 1, NT, 0)], 1, (bm, D), [], [(jax.ShapeDtypeStruct((T, D), F32), row)], epi,
               ncol=max(1, D // COL_TILE))[0]


def _proj_dw(u, dproj, ws):
    T, D = u.shape
    bmr = min(COL_TILE, D)

    def epi(accs, ex, outs):
        outs[0][...] = accs[0].astype(BF)

    return _mm("proj_dw", (N_DEV, D // bmr, 1),
               [(u, pl.BlockSpec((T, bmr), lambda s, r, k: (0, r))),
                (dproj, pl.BlockSpec((T, ws), lambda s, r, k: (0, s)))],
               [(0, 1, TN, 0)], 1, None, [],
               [(jax.ShapeDtypeStruct((N_DEV, D, ws), BF),
                 pl.BlockSpec((None, bmr, ws), lambda s, r, k: (s, r, 0)))], epi)[0]


def _mixer_ab_fwd(h, gain, wing, conv_w, wout, tabs, nb, seq, comm_proj=None, comm_attn=None, comm_out=None):
    cosf, sinf, bias = tabs
    width = wing.shape[-1] * N_DEV // 6
    nh = width // LANES
    u = _rms_fwd(h, gain, BF)
    proj = _proj_fwd(u, wing, comm=comm_proj)
    qkv = _rope_fwd(proj, cosf, sinf, seq, nh)
    cat, lse = _attn_fwd(qkv, bias, nb, seq, nh, comm=comm_attn)
    cat = _conv_fwd(proj, conv_w, cat, nb, seq, width)
    return _res_mm("outproj_fwd", cat, wout, h, comm=comm_out), (h, u, proj, qkv, cat, lse)


def _mixer_ab_bwd(dh, dhb, saved, gain, wing, conv_w, wout, tabs, nb, seq, reduce_start, carry):
    cosf, sinf, bias = tabs
    h, u, proj, qkv, cat, lse = saved
    D = h.shape[1]
    ws = wing.shape[-1]
    width = ws * N_DEV // 6
    nh = width // LANES
    dcat = _mm_nt("outproj_bwd_x", dhb, wout, F32)
    dwout = _mm_tn("outproj_dw", cat, [dhb])[0]
    comm = _merge_comms(reduce_start(["ab_w_out"], [dwout.reshape(N_DEV, -1, D)]) + [carry])
    dq, delta = _attn_bwd_dq(qkv, cat, dcat, lse, bias, nb, seq, nh, comm=comm)
    dk, dv = _attn_bwd_dkv(qkv, dcat, lse, delta, bias, nb, seq, nh)
    dgb, dgc, dxin, dconvw = _conv_bwd(proj, conv_w, dcat, nb, seq, width)
    dproj = _assemble_dproj(dq, dk, dv, dgb, dgc, dxin, cosf, sinf, seq)
    du = _proj_bwd_x(dproj, wing)
    comm, = reduce_start(["ab_w_in"], [_proj_dw(u, dproj, ws)])
    dh_in, dhb_in, dgain = _rms_bwd(du, h, gain, dh)
    return dh_in, dhb_in, dgain, dconvw, comm


def _s5_zoh(lr, li, log_dt):
    dt = jnp.exp(log_dt)
    mag = jnp.exp(lr * dt)
    ar = mag * jnp.cos(li * dt)
    ai = mag * jnp.sin(li * dt)
    den = lr * lr + li * li
    return dt, ar, ai, den, ((ar - 1.0) * lr + ai * li) / den, (ai * lr - (ar - 1.0) * li) / den


def _s5_discretize(lam_re, lam_im, log_dt, bt_re, bt_im):
    def body(lr_ref, li_ref, ld_ref, br_ref, bi_ref, ar_ref, ai_ref, bbr_ref, bbi_ref):
        _, ar, ai, _, fr, fi = _s5_zoh(lr_ref[...], li_ref[...], ld_ref[...])
        ar_ref[...] = ar
        ai_ref[...] = ai
        bbr_ref[...] = fr * br_ref[...] - fi * bi_ref[...]
        bbi_ref[...] = fr * bi_ref[...] + fi * br_ref[...]

    small = jax.ShapeDtypeStruct(lam_re.shape, F32)
    big = jax.ShapeDtypeStruct(bt_re.shape, F32)
    return pl.pallas_call(body, out_shape=[small, small, big, big], name="s5_discretize",
                          compiler_params=_cparams(0))(lam_re, lam_im, log_dt, bt_re, bt_im)


def _s5_discretize_bwd(lam_re, lam_im, log_dt, bt_re, bt_im, d_ar, d_ai, d_bbr, d_bbi):

    def body(lr_ref, li_ref, ld_ref, br_ref, bi_ref, dar_ref, dai_ref, dbbr_ref, dbbi_ref,
             dlr_ref, dli_ref, dld_ref, dbr_ref, dbi_ref):
        lr, li = lr_ref[...], li_ref[...]
        dt, ar, ai, den, fr, fi = _s5_zoh(lr, li, ld_ref[...])
        br, bi = br_ref[...], bi_ref[...]
        dbbr, dbbi = dbbr_ref[...], dbbi_ref[...]
        dbr_ref[...] = dbbr * fr + dbbi * fi
        dbi_ref[...] = dbbi * fr - dbbr * fi
        dfr = jnp.sum(dbbr * br + dbbi * bi, axis=1, keepdims=True)
        dfi = jnp.sum(dbbi * br - dbbr * bi, axis=1, keepdims=True)
        dnr = dfr / den
        dni = dfi / den
        dden = -(dfr * fr + dfi * fi) / den
        dar = dar_ref[...] + dnr * lr - dni * li
        dai = dai_ref[...] + dnr * li + dni * lr
        dlr_ref[...] = dnr * (ar - 1.0) + dni * ai + 2.0 * dden * lr + dt * (dar * ar + dai * ai)
        dli_ref[...] = dnr * ai - dni * (ar - 1.0) + 2.0 * dden * li + dt * (dai * ar - dar * ai)
        ddt = jnp.sum(dar * (lr * ar - li * ai) + dai * (lr * ai + li * ar), axis=2, keepdims=True)
        dld_ref[...] = ddt * dt

    small = jax.ShapeDtypeStruct(lam_re.shape, F32)
    big = jax.ShapeDtypeStruct(bt_re.shape, F32)
    return pl.pallas_call(
        body, out_shape=[small, small, jax.ShapeDtypeStruct(log_dt.shape, F32), big, big],
        name="s5_discretize_bwd", compiler_params=_cparams(0))(
            lam_re, lam_im, log_dt, bt_re, bt_im, d_ar, d_ai, d_bbr, d_bbi)


def _rows8(t):
    return pl.ds(pl.multiple_of(t * SUBLANES, SUBLANES), SUBLANES)


def _cmul_add(ar, ai, sr, si, br, bi):
    return ar * sr - ai * si + br, ar * si + ai * sr + bi


def _cpow(ar, ai, n):
    rr = ri = None
    while n:
        if n & 1:
            rr, ri = (ar, ai) if rr is None else (rr * ar - ri * ai, rr * ai + ri * ar)
        ar, ai = ar * ar - ai * ai, 2.0 * ar * ai
        n >>= 1
    return rr, ri


def _s5_specs(R, nj):
    sh = STATE_COLS
    return dict(
        rows=pl.BlockSpec((R, LANES), lambda j: (0, j)),
        bd=pl.BlockSpec((None, LANES, sh), lambda j: (j, 0, 0)),
        cd=pl.BlockSpec((None, sh, LANES), lambda j: (j, 0, 0)),
        a=pl.BlockSpec((None, 1, sh), lambda j: (j, 0, 0)),
        vec=pl.BlockSpec((1, LANES), lambda j: (0, j)),
        init=pl.BlockSpec((None, SUBLANES, sh), lambda j: (j, 0, 0)))


def _s5_fwd(u, mats, seg_len, nseg, comm=None):
    bdr, bdi, cdr, cdi, are, aim, dsk = mats
    R, D = u.shape
    nj = D // LANES
    sh = STATE_COLS
    rc = min(R, 512)
    sp = _s5_specs(R, nj)

    def body(u_ref, bdr_ref, bdi_ref, cdr_ref, cdi_ref, ar_ref, ai_ref, d_ref,
             y_ref, yg_ref, ir_ref, ii_ref, sre, sim):
        ar = jnp.broadcast_to(ar_ref[...], (SUBLANES, sh))
        ai = jnp.broadcast_to(ai_ref[...], (SUBLANES, sh))

        def bu_chunk(c, _):
            rows = pl.ds(pl.multiple_of(c * rc, rc), rc)
            ub = u_ref[rows, :].astype(BF)
            sre[rows, :] = _dot(ub, bdr_ref[...], NN)
            sim[rows, :] = _dot(ub, bdi_ref[...], NN)
            return 0

        lax.fori_loop(0, R // rc, bu_chunk, 0)
        z = jnp.zeros((SUBLANES, sh), F32)

        def local_scan(t, c):
            return _cmul_add(ar, ai, c[0], c[1], sre[_rows8(t), :], sim[_rows8(t), :])

        er, ei = lax.fori_loop(0, seg_len, local_scan, (z, z), unroll=SCAN_UNROLL)
        pr, pi = _cpow(ar, ai, seg_len)
        first = (lax.broadcasted_iota(jnp.int32, (SUBLANES, sh), 0) & (nseg - 1)) == 0

        def prev(x):
            return jnp.where(first, 0.0, pltpu.roll(x, 1, 0))

        xr, xi = er, ei
        for _ in range(nseg - 1):
            xr, xi = _cmul_add(pr, pi, prev(xr), prev(xi), er, ei)
        i_r, i_i = prev(xr), prev(xi)
        ir_ref[...] = i_r
        ii_ref[...] = i_i

        def scan(t, c):
            nr, ni = _cmul_add(ar, ai, c[0], c[1], sre[_rows8(t), :], sim[_rows8(t), :])
            sre[_rows8(t), :] = nr
            sim[_rows8(t), :] = ni
            return nr, ni

        lax.fori_loop(0, seg_len, scan, (i_r, i_i), unroll=SCAN_UNROLL)

        def y_chunk(c, _):
            rows = pl.ds(pl.multiple_of(c * rc, rc), rc)
            y = _dot(sre[rows, :], cdr_ref[...], NN) + _dot(sim[rows, :], cdi_ref[...], NN)
            y = y + d_ref[...] * u_ref[rows, :]
            y_ref[rows, :] = y
            yg_ref[rows, :] = _gelu(y).astype(BF)
            return 0

        lax.fori_loop(0, R // rc, y_chunk, 0)

    init_sh = jax.ShapeDtypeStruct((nj, SUBLANES, STATE_COLS), F32)
    return _call(
        body, "s5_fwd", (nj,),
        [sp["rows"], sp["bd"], sp["bd"], sp["cd"], sp["cd"], sp["a"], sp["a"], sp["vec"]],
        [sp["rows"], sp["rows"], sp["init"], sp["init"]],
        [jax.ShapeDtypeStruct((R, D), F32), jax.ShapeDtypeStruct((R, D), BF), init_sh, init_sh],
        (u, bdr, bdi, cdr, cdi, are, aim, dsk),
        scratch=[pltpu.VMEM((R, sh), F32) for _ in range(2)], comm=comm)


def _s5_bwd(u, dy, mats, init_re, init_im, seg_len, nseg, comm=None):
    bdr, bdi, cdr, cdi, are, aim, dsk = mats
    R, D = u.shape
    nj = D // LANES
    sh = STATE_COLS
    rc = min(R, 512)
    sp = _s5_specs(R, nj)

    def body(u_ref, dy_ref, bdr_ref, bdi_ref, cdr_ref, cdi_ref, ar_ref, ai_ref, d_ref, ir_ref, ii_ref,
             du_ref, dbdr_ref, dbdi_ref, dcdr_ref, dcdi_ref, dar_ref, dai_ref, dd_ref,
             sre, sim, gre, gim):
        ar = jnp.broadcast_to(ar_ref[...], (SUBLANES, sh))
        ai = jnp.broadcast_to(ai_ref[...], (SUBLANES, sh))
        i_r, i_i = ir_ref[...], ii_ref[...]

        def chunk(c):
            return pl.ds(pl.multiple_of(c * rc, rc), rc)

        def bu_chunk(c, _):
            ub = u_ref[chunk(c), :].astype(BF)
            sre[chunk(c), :] = _dot(ub, bdr_ref[...], NN)
            sim[chunk(c), :] = _dot(ub, bdi_ref[...], NN)
            return 0

        lax.fori_loop(0, R // rc, bu_chunk, 0)

        def scan(t, c):
            nr, ni = _cmul_add(ar, ai, c[0], c[1], sre[_rows8(t), :], sim[_rows8(t), :])
            sre[_rows8(t), :] = nr
            sim[_rows8(t), :] = ni
            return nr, ni

        lax.fori_loop(0, seg_len, scan, (i_r, i_i), unroll=SCAN_UNROLL)

        def c_chunk(c, carry):
            dyb = dy_ref[chunk(c), :].astype(BF)
            gre[chunk(c), :] = _dot(dyb, cdr_ref[...], NT)
            gim[chunk(c), :] = _dot(dyb, cdi_ref[...], NT)
            return (carry[0] + _dot(sre[chunk(c), :], dyb, TN), carry[1] + _dot(sim[chunk(c), :], dyb, TN))

        zc = jnp.zeros((sh, LANES), F32)
        dcr, dci = lax.fori_loop(0, R // rc, c_chunk, (zc, zc))
        dcdr_ref[...] = dcr
        dcdi_ref[...] = dci

        def adj(t, gr_next, gi_next):
            return _cmul_add(ar, -ai, gr_next, gi_next, gre[_rows8(t), :], gim[_rows8(t), :])

        z = jnp.zeros((SUBLANES, sh), F32)
        fr, fi = lax.fori_loop(0, seg_len, lambda i, c: adj(seg_len - 1 - i, c[0], c[1]), (z, z),
                               unroll=SCAN_UNROLL)
        pr, pi = _cpow(ar, ai, seg_len)
        last =(lax.broadcasted_iota(jnp.int32, (SUBLANES, sh), 0) & (nseg - 1)) == nseg - 1

        def nxt(x):
            return jnp.where(last, 0.0, pltpu.roll(x, SUBLANES - 1, 0))

        xr, xi = fr, fi
        for _ in range(nseg - 1):
            xr, xi = _cmul_add(pr, -pi, nxt(xr), nxt(xi), fr, fi)
        g0r, g0i = nxt(xr), nxt(xi)

        def adj_scan(i, c):
            t = seg_len - 1 - i
            gr, gi = adj(t, c[0], c[1])
            gre[_rows8(t), :] = gr
            gim[_rows8(t), :] = gi
            spr, spi = sre[_rows8(t - 1), :], sim[_rows8(t - 1), :]
            return gr, gi, c[2] + spr * gr + spi * gi, c[3] + spr * gi - spi * gr

        gr, gi, dar, dai = lax.fori_loop(0, seg_len - 1, adj_scan, (g0r, g0i, z, z))
        gr, gi = adj(0, gr, gi)
        gre[_rows8(0), :] = gr
        gim[_rows8(0), :] = gi
        dar_ref[...] = jnp.sum(dar + i_r * gr + i_i * gi, axis=0, keepdims=True)
        dai_ref[...] = jnp.sum(dai + i_r * gi - i_i * gr, axis=0, keepdims=True)

        def d_chunk(c, carry):
            ub = u_ref[chunk(c), :].astype(BF)
            grb = gre[chunk(c), :].astype(BF)
            gib = gim[chunk(c), :].astype(BF)
            du = _dot(grb, bdr_ref[...], NT) + _dot(gib, bdi_ref[...], NT)
            du_ref[chunk(c), :] = du + d_ref[...] * dy_ref[chunk(c), :]
            dd = carry[2] + jnp.sum(dy_ref[chunk(c), :] * u_ref[chunk(c), :], axis=0, keepdims=True)
            return carry[0] + _dot(ub, grb, TN), carry[1] + _dot(ub, gib, TN), dd

        zb = jnp.zeros((LANES, sh), F32)
        dbr, dbi, dd = lax.fori_loop(0, R // rc, d_chunk, (zb, zb, jnp.zeros((1, LANES), F32)))
        dbdr_ref[...] = dbr
        dbdi_ref[...] = dbi
        dd_ref[...] = dd

    bd_sh = jax.ShapeDtypeStruct((nj, LANES, STATE_COLS), F32)
    cd_sh = jax.ShapeDtypeStruct((nj, STATE_COLS, LANES), F32)
    a_sh = jax.ShapeDtypeStruct((nj, 1, STATE_COLS), F32)
    return _call(
        body, "s5_bwd", (nj,),
        [sp["rows"], sp["rows"], sp["bd"], sp["bd"], sp["cd"], sp["cd"], sp["a"], sp["a"],
         sp["vec"], sp["init"], sp["init"]],
        [sp["rows"], sp["bd"], sp["bd"], sp["cd"], sp["cd"], sp["a"], sp["a"], sp["vec"]],
        [jax.ShapeDtypeStruct((R, D), F32), bd_sh, bd_sh, cd_sh, cd_sh, a_sh, a_sh,
         jax.ShapeDtypeStruct((1, D), F32)],
        (u, dy, bdr, bdi, cdr, cdi, are, aim, dsk, init_re, init_im),
        scratch=[pltpu.VMEM((R, sh), F32) for _ in range(4)], comm=comm)


def _glu_fwd(yg, wa, wb, h):
    T, D = yg.shape
    N = wa.shape[1]
    bm = min(ROW_TILE, T)
    bn = min(ROW_TILE, N)
    wspec = pl.BlockSpec((D, bn), lambda i, j, k: (0, j))
    ospec = pl.BlockSpec((bm, bn), lambda i, j, k: (i, j))

    def epi(accs, ex, outs):
        pa, pb = accs
        outs[0][...] = ex[0][...] + pa * _sig(pb)
        outs[1][...] = pa.astype(BF)
        outs[2][...] = pb.astype(BF)

    return _mm("glu_fwd", (T // bm, N // bn, 1),
               [(yg, pl.BlockSpec((bm, D), lambda i, j, k: (i, 0))), (wa, wspec), (wb, wspec)],
               [(0, 1, NN, 0), (0, 2, NN, 1)], 2, None, [(h, ospec)],
               [(jax.ShapeDtypeStruct((T, N), F32), ospec), (jax.ShapeDtypeStruct((T, N), BF), ospec),
                (jax.ShapeDtypeStruct((T, N), BF), ospec)], epi)


def _glu_bwd_gates(dz, pa, pb):
    T, D = dz.shape
    bm = min(ROW_TILE, T)

    def body(dz_ref, pa_ref, pb_ref, dpa_ref, dpb_ref):
        dz = dz_ref[...]
        sg = _sig(pb_ref[...].astype(F32))
        dpa_ref[...] = (dz * sg).astype(BF)
        dpb_ref[...] = (dz * pa_ref[...].astype(F32) * sg * (1.0 - sg)).astype(BF)

    row = pl.BlockSpec((bm, D), lambda i: (i, 0))
    return pl.pallas_call(
        body, grid=(T // bm,), in_specs=[row] * 3, out_specs=[row] * 2,
        out_shape=[jax.ShapeDtypeStruct((T, D), BF)] * 2, name="glu_bwd_gates",
        compiler_params=_cparams(1))(dz, pa, pb)


def _glu_bwd_y(dpa, dpb, wa, wb, y_pre, comm=None):
    T, N = dpa.shape
    D = wa.shape[0]
    bm = min(ROW_TILE, T)
    bn = min(ROW_TILE, D)
    aspec = pl.BlockSpec((bm, N), lambda i, j, k: (i, 0))
    wspec = pl.BlockSpec((bn, N), lambda i, j, k: (j, 0))
    ospec = pl.BlockSpec((bm, bn), lambda i, j, k: (i, j))

    def epi(accs, ex, outs):
        outs[0][...] = accs[0] * _gelu_grad(ex[0][...])

    return _mm("glu_bwd_y", (T // bm, D // bn, 1), [(dpa, aspec), (wa, wspec), (dpb, aspec), (wb, wspec)],
               [(0, 1, NT, 0), (2, 3, NT, 0)], 1, None, [(y_pre, ospec)],
               [(jax.ShapeDtypeStruct((T, D), F32), ospec)], epi, comm=comm)[0]


def _block_diag_in(x, nj):
    g = GROUPS_PER_BLOCK
    x = x.reshape(nj, g, 1, S5_GROUP, S5_STATE)
    eye = jnp.eye(g, dtype=bool)[None, :, :, None, None]
    full = jnp.where(eye, x, 0.0)
    return full.transpose(0, 1, 3, 2, 4).reshape(nj, g * S5_GROUP, g * S5_STATE)


def _block_diag_out(x, nj):
    return _block_diag_in(x, nj).transpose(0, 2, 1)


def _diag_of_in(m, nj):
    g = GROUPS_PER_BLOCK
    m5 = m.reshape(nj, g, S5_GROUP, g, S5_STATE)
    d = jnp.diagonal(m5, axis1=1, axis2=3)
    return d.transpose(0, 3, 1, 2).reshape(nj * g, S5_GROUP, S5_STATE)


def _mixer_s5_fwd(h, gain, p, dsk, wa, wb, nb, seq, comm_s5=None):
    T, D = h.shape
    nj = D // LANES
    nseg = SUBLANES // nb
    seg_len = seq // nseg
    G = p["s5_lambda_re"].shape[1]
    lam_re = p["s5_lambda_re"].reshape(G, 1, S5_STATE)
    lam_im = p["s5_lambda_im"].reshape(G, 1, S5_STATE)
    log_dt = p["s5_log_dt"].reshape(G, 1, 1)
    bt_re = p["s5_b_re"][0].transpose(0, 2, 1)
    bt_im = p["s5_b_im"][0].transpose(0, 2, 1)
    ar, ai, bbr, bbi = _s5_discretize(lam_re, lam_im, log_dt, bt_re, bt_im)
    mats = (_block_diag_in(bbr, nj).astype(BF), _block_diag_in(bbi, nj).astype(BF),
            _block_diag_out(p["s5_c_re"][0], nj).astype(BF),
            _block_diag_out(-p["s5_c_im"][0], nj).astype(BF),
            ar.reshape(nj, 1, STATE_COLS), ai.reshape(nj, 1, STATE_COLS), dsk)
    h_seg = _to_seg(h, seg_len)
    u = _rms_fwd(h_seg, gain, F32)
    y_pre, yg, init_re, init_im = _s5_fwd(u, mats, seg_len, nseg, comm=comm_s5)
    h_out, pa, pb = _glu_fwd(yg, wa, wb, h_seg)
    disc_in = (lam_re, lam_im, log_dt, bt_re, bt_im)
    return _to_tok(h_out, seg_len), (h_seg, u, mats, y_pre, yg, init_re, init_im, pa, pb, disc_in, seg_len, nseg)


def _mixer_s5_bwd(dh, saved, gain, wa, wb, reduce_start, carry):
    h_seg, u, mats, y_pre, yg, init_re, init_im, pa, pb, disc_in, seg_len, nseg = saved
    T, D = h_seg.shape
    nj = D // LANES
    G = nj * GROUPS_PER_BLOCK
    dh_seg = _to_seg(dh, seg_len)
    dpa, dpb = _glu_bwd_gates(dh_seg, pa, pb)
    dy = _glu_bwd_y(dpa, dpb, wa, wb, y_pre)
    dwa, dwb = _mm_tn("glu_dw", yg, [dpa, dpb])
    comm = _merge_comms(reduce_start(["s5_glu_wa", "s5_glu_wb"],
                                     [dwa.reshape(N_DEV, -1, D), dwb.reshape(N_DEV, -1, D)]) + [carry])
    du, dbdr, dbdi, dcdr, dcdi, dar, dai, dd = _s5_bwd(u, dy, mats, init_re, init_im, seg_len, nseg, comm=comm)
    d_bbr = _diag_of_in(dbdr, nj)
    d_bbi = _diag_of_in(dbdi, nj)
    d_c_re = _diag_of_in(dcdr.transpose(0, 2, 1), nj)
    d_c_im = -_diag_of_in(dcdi.transpose(0, 2, 1), nj)
    dlr, dli, dld, dbr, dbi = _s5_discretize_bwd(
        *disc_in, dar.reshape(G, 1, S5_STATE), dai.reshape(G, 1, S5_STATE), d_bbr, d_bbi)
    small = {"s5_lambda_re": dlr.reshape(1, G, S5_STATE), "s5_lambda_im": dli.reshape(1, G, S5_STATE),
             "s5_log_dt": dld.reshape(1, G),
             "s5_b_re": dbr.transpose(0, 2, 1)[None], "s5_b_im": dbi.transpose(0, 2, 1)[None],
             "s5_c_re": d_c_re[None], "s5_c_im": d_c_im[None], "s5_d": dd}
    dh_in, _, dgain = _rms_bwd(du, h_seg, gain, dh_seg)
    dh_in = _to_tok(dh_in, seg_len)
    return dh_in, dh_in.astype(BF), dgain, small


def _mesh_pos():
    return lax.axis_index("x"), lax.axis_index("y"), lax.axis_index("c")


class _Gather:
    def __init__(self, srcs, slots, send_sems, recv_sems):
        self.srcs, self.slots, self.send_sems, self.recv_sems = srcs, slots, send_sems, recv_sems
        x, y, c = _mesh_pos()
        self.c = c
        self.me, self.sib = (x, y, c), (x, y, 1 - c)
        self.chips = [(1 - x, y), (x, 1 - y), (1 - x, 1 - y)]

    def copy(self, a, k, block, to, own=False):
        dst = self.slots[a].at[4 * block[0] + 2 * block[1] + block[2]]
        return pltpu.make_async_remote_copy(
            src_ref=self.srcs[a] if own else dst, dst_ref=dst, send_sem=self.send_sems.at[7 * a + k],
            recv_sem=self.recv_sems.at[7 * a + k], device_id=to, device_id_type=MESH)

    def own_copies(self, a):
        cps = [self.copy(a, 0, self.me, self.sib, own=True)]
        return cps + [self.copy(a, 1 + j, self.me, (*chip, self.c), own=True) for j, chip in enumerate(self.chips)]

    def start(self):
        for a in range(len(self.srcs)):
            for cp in self.own_copies(a):
                cp.start()

    def finish(self):
        n = len(self.srcs)
        for a in range(n):
            for j, chip in enumerate(self.chips):
                self.copy(a, 1 + j, (*chip, self.c), self.me).wait_recv()
                self.copy(a, 4 + j, (*chip, self.c), self.sib).start()
        for a in range(n):
            self.copy(a, 0, self.sib, self.me).wait_recv()
            for j, chip in enumerate(self.chips):
                self.copy(a, 4 + j, (*chip, 1 - self.c), self.me).wait_recv()
        for a in range(n):
            for cp in self.own_copies(a):
                cp.wait_send()
            for j, chip in enumerate(self.chips):
                self.copy(a, 4 + j, (*chip, self.c), self.sib).wait_send()


def _gather_comm(arrs):
    n = len(arrs)

    def local(xs, outs, sems, a):
        x, y, c = _mesh_pos()
        return pltpu.make_async_copy(xs[a], outs[a].at[4 * x + 2 * y + c], sems[2].at[a])

    def start(xs, outs, sems):
        for a in range(n):
            local(xs, outs, sems, a).start()
        _Gather(xs, outs, sems[0], sems[1]).start()

    def finish(xs, outs, sems):
        _Gather(xs, outs, sems[0], sems[1]).finish()
        for a in range(n):
            local(xs, outs, sems, a).wait()

    return _Comm(list(arrs), [jax.ShapeDtypeStruct((N_DEV,) + a.shape, a.dtype) for a in arrs],
                 [pltpu.SemaphoreType.DMA((7 * n,)), pltpu.SemaphoreType.DMA((7 * n,)),
                  pltpu.SemaphoreType.DMA((n,))], start, finish)


def _exchange_comm(parts):
    n = len(parts)

    def copies(ps, outs, sems):
        x, y, c = _mesh_pos()
        cps = []
        for a in range(n):
            for j in range(1, 4):
                to = (jnp.bitwise_xor(x, j // 2), jnp.bitwise_xor(y, j % 2), c)
                cps.append(pltpu.make_async_remote_copy(
                    src_ref=ps[a].at[j], dst_ref=outs[a].at[j - 1], send_sem=sems[0].at[3 * a + j - 1],
                    recv_sem=sems[1].at[3 * a + j - 1], device_id=to, device_id_type=MESH))
        return cps

    def start(ps, outs, sems):
        for cp in copies(ps, outs, sems):
            cp.start()

    def finish(ps, outs, sems):
        for cp in copies(ps, outs, sems):
            cp.wait()

    return _Comm(list(parts), [jax.ShapeDtypeStruct((3,) + p.shape[1:], p.dtype) for p in parts],
                 [pltpu.SemaphoreType.DMA((3 * n,)), pltpu.SemaphoreType.DMA((3 * n,))], start, finish)


def _run_comm(comm, name):
    ci, co = len(comm.ins), len(comm.outs)

    def body(*refs):
        comm.start(refs[:ci], refs[ci:ci + co], refs[ci + co:])
        comm.finish(refs[:ci], refs[ci:ci + co], refs[ci + co:])

    any_spec = pl.BlockSpec(memory_space=pl.ANY)
    comm.set_results(pl.pallas_call(
        body, in_specs=[any_spec] * ci, out_specs=[any_spec] * co, out_shape=list(comm.outs),
        scratch_shapes=list(comm.sems), name=name, compiler_params=_cparams(0))(*comm.ins))


def _pair_exchange(grads, name):
    n = len(grads)

    def body(*refs):
        gs, outs = refs[:n], refs[n:2 * n]
        send_sems, recv_sems = refs[2 * n:]
        x, y, c = _mesh_pos()
        copies = []
        for a in range(n):
            for k in range(4):
                copies.append(pltpu.make_async_remote_copy(
                    src_ref=gs[a].at[2 * k + 1 - c], dst_ref=outs[a].at[k], send_sem=send_sems.at[4 * a + k],
                    recv_sem=recv_sems.at[4 * a + k], device_id=(x, y, 1 - c), device_id_type=MESH))
        for cp in copies:
            cp.start()
        for cp in copies:
            cp.wait()

    any_spec = pl.BlockSpec(memory_space=pl.ANY)
    return pl.pallas_call(
        body, in_specs=[any_spec] * n, out_specs=[any_spec] * n,
        out_shape=[jax.ShapeDtypeStruct((4,) + g.shape[1:], g.dtype) for g in grads],
        scratch_shapes=[pltpu.SemaphoreType.DMA((4 * n,)), pltpu.SemaphoreType.DMA((4 * n,))],
        name=name, compiler_params=_cparams(0))(*grads)


def _pair_sum(grad, recv, pos):
    _, R, C = grad.shape
    br = _row_block(R, C, PAIR_SUM_ELEMS)

    def body(pos_ref, g_ref, r_ref, o_ref):
        o_ref[...] = (g_ref[...].astype(F32) + r_ref[...].astype(F32)).astype(BF)

    def chip(j, p):
        return jnp.bitwise_xor(p[1], j)

    return pl.pallas_call(
        body, grid_spec=pltpu.PrefetchScalarGridSpec(
            num_scalar_prefetch=1, grid=(4, R // br),
            in_specs=[pl.BlockSpec((None, br, C), lambda j, i, p: (2 * chip(j, p) + p[0], i, 0)),
                      pl.BlockSpec((None, br, C), lambda j, i, p: (chip(j, p), i, 0))],
            out_specs=pl.BlockSpec((None, br, C), lambda j, i, p: (j, i, 0))),
        out_shape=jax.ShapeDtypeStruct((4, R, C), BF), name="pair_sum", compiler_params=_cparams(2))(pos, grad, recv)


def _adamw(w, g, m, v):
    m = ADAM_B1 * m + (1.0 - ADAM_B1) * g
    v = ADAM_B2 * v + (1.0 - ADAM_B2) * (g * g)
    m_hat = m / (1.0 - ADAM_B1 ** ADAM_STEP)
    v_hat = v / (1.0 - ADAM_B2 ** ADAM_STEP)
    return -ADAM_LR * (m_hat / (jnp.sqrt(v_hat) + ADAM_EPS) + ADAM_WD * w), m, v


def _adamw_piece(w, m, v, piece, part, recv, bufs):
    _, R, C = w.shape
    br = _row_block(R, C)

    def body(w_ref, m_ref, v_ref, p_ref, r_ref, b0, b1, b2, b3, g_ref, d_ref, nm_ref, nv_ref):
        g = p_ref[...].astype(F32)
        for j in range(3):
            g = g + r_ref[j].astype(F32)
        d, nm, nv = _adamw(w_ref[...], g, m_ref[...], v_ref[...])
        g_ref[...] = g
        d_ref[...] = d
        nm_ref[...] = nm
        nv_ref[...] = nv

    row = pl.BlockSpec((None, br, C), lambda i: (piece, i, 0))
    any_spec = pl.BlockSpec(memory_space=pl.ANY)
    return pl.pallas_call(
        body, grid=(R // br,),
        in_specs=[row, row, row, pl.BlockSpec((None, br, C), lambda i: (0, i, 0)),
                  pl.BlockSpec((3, br, C), lambda i: (0, i, 0))] + [any_spec] * 4,
        out_specs=[row] * 4, out_shape=[jax.ShapeDtypeStruct(w.shape, F32)] * 4,
        input_output_aliases={5: 0, 6: 1, 7: 2, 8: 3}, name="adamw_piece",
        compiler_params=_cparams(1))(w, m, v, part, recv, *bufs)


def _all_reduce_small(x):
    rows = x.shape[0]

    def body(x_ref, o_ref, buf, send_sems, recv_sems):
        xp, yp, cp = _mesh_pos()
        buf[4 * xp + 2 * yp + cp] = x_ref[...]
        gather = _Gather([x_ref], [buf], send_sems, recv_sems)
        gather.start()
        gather.finish()
        acc = buf[0]
        for d in range(1, N_DEV):
            acc = acc + buf[d]
        o_ref[...] = acc

    vm = pl.BlockSpec(memory_space=pltpu.VMEM)
    return pl.pallas_call(
        body, in_specs=[vm], out_specs=vm, out_shape=jax.ShapeDtypeStruct(x.shape, F32),
        scratch_shapes=[pltpu.VMEM((N_DEV, rows, LANES), F32), pltpu.SemaphoreType.DMA((7,)),
                        pltpu.SemaphoreType.DMA((7,))],
        name="all_reduce_small", compiler_params=_cparams(0))(x)


def _adamw_small(w, g, m, v):
    def body(w_ref, g_ref, m_ref, v_ref, d_ref, nm_ref, nv_ref):
        d, nm, nv = _adamw(w_ref[...], g_ref[...], m_ref[...], v_ref[...])
        d_ref[...] = d
        nm_ref[...] = nm
        nv_ref[...] = nv

    sh = jax.ShapeDtypeStruct(w.shape, F32)
    return pl.pallas_call(body, out_shape=[sh] * 3, name="adamw_small", compiler_params=_cparams(0))(w, g, m, v)


def _pack(arrs):
    flat = jnp.concatenate([a.reshape(-1).astype(F32) for a in arrs])
    rows = -(-flat.shape[0] // (SUBLANES * LANES)) * SUBLANES
    return jnp.pad(flat, (0, rows * LANES - flat.shape[0])).reshape(rows, LANES)


def _unpack(buf, shapes):
    flat = buf.reshape(-1)
    out, off = [], 0
    for s in shapes:
        n = 1
        for d in s:
            n *= d
        out.append(flat[off:off + n].reshape(s))
        off += n
    return out


BIG = ("ffn_w1", "ffn_w3", "ffn_w2", "ab_w_in", "ab_w_out", "s5_glu_wa", "s5_glu_wb")
NAMES = ("ln_ffn_pre", "ln_mix", "ln_ffn_post", "ln_final", "ffn_w1", "ffn_w3", "ffn_w2", "ab_w_in",
         "ab_conv_w", "ab_w_out", "s5_lambda_re", "s5_lambda_im", "s5_log_dt", "s5_b_re", "s5_b_im",
         "s5_c_re", "s5_c_im", "s5_d", "s5_glu_wa", "s5_glu_wb")


def kernel(x, ln_ffn_pre, ln_mix, ln_ffn_post, ln_final, ffn_w1, ffn_w3, ffn_w2, ab_w_in, ab_conv_w, ab_w_out, s5_lambda_re, s5_lambda_im, s5_log_dt, s5_b_re, s5_b_im, s5_c_re, s5_c_im, s5_d, s5_glu_wa, s5_glu_wb, loss_target, m_ln_ffn_pre, m_ln_mix, m_ln_ffn_post, m_ln_final, m_ffn_w1, m_ffn_w3, m_ffn_w2, m_ab_w_in, m_ab_conv_w, m_ab_w_out, m_s5_lambda_re, m_s5_lambda_im, m_s5_log_dt, m_s5_b_re, m_s5_b_im, m_s5_c_re, m_s5_c_im, m_s5_d, m_s5_glu_wa, m_s5_glu_wb, v_ln_ffn_pre, v_ln_mix, v_ln_ffn_post, v_ln_final, v_ffn_w1, v_ffn_w3, v_ffn_w2, v_ab_w_in, v_ab_conv_w, v_ab_w_out, v_s5_lambda_re, v_s5_lambda_im, v_s5_log_dt, v_s5_b_re, v_s5_b_im, v_s5_c_re, v_s5_c_im, v_s5_d, v_s5_glu_wa, v_s5_glu_wb):
    w = dict(zip(NAMES, (ln_ffn_pre, ln_mix, ln_ffn_post, ln_final, ffn_w1, ffn_w3, ffn_w2, ab_w_in, ab_conv_w,
                         ab_w_out, s5_lambda_re, s5_lambda_im, s5_log_dt, s5_b_re, s5_b_im, s5_c_re, s5_c_im,
                         s5_d, s5_glu_wa, s5_glu_wb)))
    mom = dict(zip(NAMES, (m_ln_ffn_pre, m_ln_mix, m_ln_ffn_post, m_ln_final, m_ffn_w1, m_ffn_w3, m_ffn_w2,
                           m_ab_w_in, m_ab_conv_w, m_ab_w_out, m_s5_lambda_re, m_s5_lambda_im, m_s5_log_dt,
                           m_s5_b_re, m_s5_b_im, m_s5_c_re, m_s5_c_im, m_s5_d, m_s5_glu_wa, m_s5_glu_wb)))
    var = dict(zip(NAMES, (v_ln_ffn_pre, v_ln_mix, v_ln_ffn_post, v_ln_final, v_ffn_w1, v_ffn_w3, v_ffn_w2,
                           v_ab_w_in, v_ab_conv_w, v_ab_w_out, v_s5_lambda_re, v_s5_lambda_im, v_s5_log_dt,
                           v_s5_b_re, v_s5_b_im, v_s5_c_re, v_s5_c_im, v_s5_d, v_s5_glu_wa, v_s5_glu_wb)))
    nb, seq, D = x.shape
    T = nb * seq
    assert ln_mix.shape[0] == 2 and ab_w_in.shape[0] == 1 and s5_glu_wa.shape[0] == 1
    xc, yc, cc = _mesh_pos()
    dev = 4 * xc + 2 * yc + cc
    pos = jnp.stack([cc, 2 * xc + yc]).astype(jnp.int32)
    bq = min(ATTN_TILE, seq)
    tabs =_rope_tables(seq) + (_branch_bias(seq // bq, bq),)

    def ffn_piece(k, li, fj):
        return w[k][li, fj].astype(BF)

    g0 = _gather_comm([ffn_piece("ffn_w1", 0, 0), ffn_piece("ffn_w3", 0, 0), ab_conv_w[0], s5_d])
    _run_comm(g0, "gather_first")
    w1, w3 = {(0, 0): g0.results[0]}, {(0, 0): g0.results[1]}
    w2 = {}
    conv_w = g0.results[2].transpose(1, 0, 2).reshape(3, -1)
    dsk = g0.results[3].reshape(1, D)
    gains = {k: [w[k][i:i + 1] for i in range(2)] for k in ("ln_ffn_pre", "ln_mix", "ln_ffn_post")}

    h = x.reshape(T, D)
    saved = {}

    def ffn_fwd(h, gain, key, tag, comm_up, comm_down, after_up):
        n = _rms_fwd(h, gain, BF)
        t1, t3, g = _ffn_up(n, w1[key], w3[key], comm=comm_up)
        after_up()
        saved[tag] = (h, n, t1, t3, g)
        return _ffn_down(g, w2[key], h, comm=comm_down)

    c_up = _gather_comm([ffn_piece("ffn_w2", 0, 0), ab_w_out[0].astype(BF)])
    c_dn = _gather_comm([ab_w_in[0].astype(BF)])
    h = ffn_fwd(h, gains["ln_ffn_pre"][0], (0, 0), "pre0", c_up, c_dn,
                lambda: w2.update({(0, 0): c_up.results[0]}))
    wout = c_up.results[1].reshape(-1, D)
    wing = c_dn.results[0]
    c_proj = _gather_comm([ffn_piece("ffn_w1", 0, 1)])
    c_attn = _gather_comm([ffn_piece("ffn_w3", 0, 1), s5_glu_wa[0].astype(BF)])
    c_out = _gather_comm([s5_glu_wb[0].astype(BF)])
    h, saved["mix0"] = _mixer_ab_fwd(h, gains["ln_mix"][0], wing, conv_w, wout, tabs, nb, seq, c_proj, c_attn, c_out)
    w1[(0, 1)] = c_proj.results[0]
    w3[(0, 1)] = c_attn.results[0]
    wa = c_attn.results[1].reshape(-1, D)
    wb = c_out.results[0].reshape(-1, D)
    c_up2 = _gather_comm([ffn_piece("ffn_w2", 0, 1), ffn_piece("ffn_w1", 1, 0)])
    c_dn = _gather_comm([ffn_piece("ffn_w3", 1, 0)])
    h = ffn_fwd(h, gains["ln_ffn_post"][0], (0, 1), "post0", c_up2, c_dn,
                lambda: w2.update({(0, 1): c_up2.results[0]}))
    w1[(1, 0)] = c_up2.results[1]
    w3[(1, 0)] = c_dn.results[0]
    c_up3 = _gather_comm([ffn_piece("ffn_w2", 1, 0), ffn_piece("ffn_w1", 1, 1)])
    c_dn = _gather_comm([ffn_piece("ffn_w3", 1, 1)])
    h = ffn_fwd(h, gains["ln_ffn_pre"][1], (1, 0), "pre1", c_up3, c_dn,
                lambda: w2.update({(1, 0): c_up3.results[0]}))
    w1[(1, 1)] = c_up3.results[1]
    w3[(1, 1)] = c_dn.results[0]
    c_s5 = _gather_comm([ffn_piece("ffn_w2", 1, 1)])
    h, saved["mix1"] = _mixer_s5_fwd(h, gains["ln_mix"][1], w, dsk, wa, wb, nb, seq, c_s5)
    w2[(1, 1)] = c_s5.results[0]
    h = ffn_fwd(h, gains["ln_ffn_post"][1], (1, 1), "post1", None, None, lambda: None)
    dh, dhb, d_ln_final, loss_part = _loss_head(h, ln_final.reshape(1, D), loss_target.reshape(T, D))
    loss = lax.psum(loss_part[0, 0], ("x", "y", "c"))

    reduced = {}

    def reduce_start(names, grads):
        recv = _pair_exchange(grads, "pair_exchange")
        comms = []
        for nm, g, r in zip(names, grads, recv):
            part = _pair_sum(g, r, pos)
            comms.append(_exchange_comm([part]))
            reduced[nm] = (part, comms[-1])
        return comms

    def ffn_bwd(dh, dhb, key, tag, gain, carry, is_last=False):
        h_in, n, t1, t3, g = saved[tag]
        da1, da3 = _ffn_bwd_hidden(dhb, w2[key], t1, t3, comm=carry)
        c2, = reduce_start([("ffn_w2",) + key], [_ffn_dw2(g, dhb)])
        dw1, dw3 = _ffn_dw13(n, da1, da3, comm=c2)
        c1, c3 = reduce_start([("ffn_w1",) + key, ("ffn_w3",) + key], [dw1, dw3])
        dn = _ffn_dn(da1, da3, w1[key], w3[key], comm=_merge_comms([c1, c3]) if is_last else c1)
        return list(_rms_bwd(dn, h_in, gain, dh)) + [None if is_last else c3]

    g_small = {"ln_final": d_ln_final.reshape(D)}
    g_ln = {k: [None, None] for k in gains}
    dh, dhb, g_ln["ln_ffn_post"][1], carry = ffn_bwd(dh, dhb, (1, 1), "post1", gains["ln_ffn_post"][1], None)
    dh, dhb, g_ln["ln_mix"][1], s5_small = _mixer_s5_bwd(
        dh, saved["mix1"], gains["ln_mix"][1], wa, wb, reduce_start, carry)
    g_small.update(s5_small)
    dh, dhb, g_ln["ln_ffn_pre"][1], carry = ffn_bwd(dh, dhb, (1, 0), "pre1", gains["ln_ffn_pre"][1], None)
    dh, dhb, g_ln["ln_ffn_post"][0], carry = ffn_bwd(dh, dhb, (0, 1), "post0", gains["ln_ffn_post"][0], carry)
    dh, dhb, g_ln["ln_mix"][0], g_small["ab_conv_w"], carry = _mixer_ab_bwd(
        dh, dhb, saved["mix0"], gains["ln_mix"][0], wing, conv_w, wout, tabs, nb, seq, reduce_start, carry)
    dh, dhb, g_ln["ln_ffn_pre"][0], _ = ffn_bwd(dh, dhb, (0, 0), "pre0", gains["ln_ffn_pre"][0], carry, is_last=True)
    grad_x = dh.reshape(nb, seq, D)
    for k in g_ln:
        g_small[k] = jnp.concatenate(g_ln[k], axis=0)

    out = {}
    for k in BIG:
        transposed = k in ("ffn_w1", "ffn_w3")
        pieces = [(li, fj) for li in range(2) for fj in range(2)] if w[k].ndim == 4 else [None]

        def view(a):
            a = a.swapaxes(-1, -2) if transposed else a
            return a.reshape(len(pieces), -1, a.shape[-1])

        w3d, m3d, v3d = view(w[k]), view(mom[k]), view(var[k])
        bufs = [lax.empty(w3d.shape, F32) for _ in range(4)]
        for q, key in enumerate(pieces):
            part, comm = reduced[k if key is None else (k,) + key]
            bufs = _adamw_piece(w3d, m3d, v3d, q, part, comm.results[0], bufs)
        if transposed:
            out[k] = [t.reshape(w[k].shape[:2] + w3d.shape[1:]).swapaxes(-1, -2) for t in bufs]
        else:
            out[k] = [t.reshape(w[k].shape) for t in bufs]

    small_names = [k for k in NAMES if k not in BIG]
    red = _unpack(_all_reduce_small(_pack([g_small[k] for k in small_names])),
                  [g_small[k].shape for k in small_names])
    g_red = dict(zip(small_names, red))
    cw = w["ab_conv_w"].shape[-1]
    g_red["ab_conv_w"] = lax.dynamic_slice_in_dim(g_red["ab_conv_w"], dev * cw, cw, axis=1)[None]
    dsz = w["s5_d"].shape[-1]
    g_red["s5_d"] = lax.dynamic_slice_in_dim(g_red["s5_d"].reshape(1, -1), dev * dsz, dsz, axis=1)
    shapes = [w[k].shape for k in small_names]
    g_red = {k: g_red[k].reshape(w[k].shape) for k in small_names}
    d_s, m_s, v_s = _adamw_small(_pack([w[k] for k in small_names]), _pack([g_red[k] for k in small_names]),
                                 _pack([mom[k] for k in small_names]), _pack([var[k] for k in small_names]))
    for k, d, nm, nv in zip(small_names, _unpack(d_s, shapes), _unpack(m_s, shapes), _unpack(v_s, shapes)):
        out[k] = [g_red[k], d, nm, nv]

    return (loss, grad_x, *[out[k][0] for k in NAMES], *[out[k][1] for k in NAMES],
            *[out[k][2] for k in NAMES], *[out[k][3] for k in NAMES])
```

```python
import jax
import jax.numpy as jnp
from jax import lax
from jax.experimental import pallas as pl
from jax.experimental.pallas import tpu as pltpu

F32, BF = jnp.float32, jnp.bfloat16
N_DEV = 8
MESH = pl.DeviceIdType.MESH
LANES = 128
SUBLANES = 8
VMEM_LIMIT = 56 * 2 ** 20
ROW_TILE = 512
FFN_ROW_TILE = 1024
COL_TILE = 512
ATTN_TILE = 512
SCAN_UNROLL = 4
ELEMS_PER_BLOCK = 256 * 1024
PAIR_SUM_ELEMS = 2048 * 1024
RMS_EPS = 1e-6
ROPE_THETA = 10000.0
NEG_INF = -1e30
S5_STATE = 64
S5_GROUP = 16
GROUPS_PER_BLOCK = LANES // S5_GROUP
STATE_COLS = GROUPS_PER_BLOCK * S5_STATE
DILATED_PATTERN = ((128, 1), (512, 4), (2048, 16))
ADAM_LR, ADAM_B1, ADAM_B2, ADAM_EPS, ADAM_WD, ADAM_STEP = 0.001, 0.9, 0.999, 1e-08, 0.01, 10
GELU_C = 0.7978845608028654
GELU_A = 0.044715


def _cparams(n_grid, vmem=VMEM_LIMIT):
    sem = ("arbitrary",) * n_grid if n_grid else None
    return pltpu.CompilerParams(dimension_semantics=sem, vmem_limit_bytes=vmem)


def _sig(x):
    return 1.0 / (1.0 + jnp.exp(-x))


def _gelu(x):
    return 0.5 * x * (1.0 + jnp.tanh(GELU_C * (x + GELU_A * x * x * x)))


def _gelu_grad(x):
    t = jnp.tanh(GELU_C * (x + GELU_A * x * x * x))
    return 0.5 * (1.0 + t) + 0.5 * x * (1.0 - t * t) * GELU_C * (1.0 + 3.0 * GELU_A * x * x)


def _dot(a, b, dims):
    a = a if a.dtype == BF else a.astype(BF)
    b = b if b.dtype == BF else b.astype(BF)
    return lax.dot_general(a, b, (dims, ((), ())), preferred_element_type=F32)


NN = ((1,), (0,))
NT = ((1,), (1,))
TN = ((0,), (0,))


def _row_block(rows, cols, elems=ELEMS_PER_BLOCK, mult=16):
    cap = max(mult, elems // cols)
    best = None
    for b in range(mult, min(rows, cap) + 1, mult):
        if rows % b == 0:
            best = b
    return rows if best is None else best


class _Comm:
    def __init__(self, ins, outs, sems, start, finish, members=()):
        self.ins, self.outs, self.sems, self.start, self.finish = ins, outs, sems, start, finish
        self.members = members
        self.results = None

    def set_results(self, res):
        self.results = list(res)
        off = 0
        for m in self.members:
            m.set_results(res[off:off + len(m.outs)])
            off += len(m.outs)


def _merge_comms(comms):
    comms = [c for c in comms if c is not None]
    if len(comms) < 2:
        return comms[0] if comms else None

    def each(fn_name, ins, outs, sems):
        i = o = s = 0
        for c in comms:
            ni, no, ns = len(c.ins), len(c.outs), len(c.sems)
            getattr(c, fn_name)(ins[i:i + ni], outs[o:o + no], sems[s:s + ns])
            i, o, s = i + ni, o + no, s + ns

    return _Comm([a for c in comms for a in c.ins], [a for c in comms for a in c.outs],
                 [a for c in comms for a in c.sems],
                 lambda ins, outs, sems: each("start", ins, outs, sems),
                 lambda ins, outs, sems: each("finish", ins, outs, sems), members=tuple(comms))


def _call(body, name, grid, in_specs, out_specs, out_shape, args, scratch=(), comm=None):
    in_specs, out_specs, out_shape, scratch = list(in_specs), list(out_specs), list(out_shape), list(scratch)
    if comm is None:
        return pl.pallas_call(body, grid=grid, in_specs=in_specs, out_specs=out_specs, out_shape=out_shape,
                              scratch_shapes=scratch, name=name, compiler_params=_cparams(len(grid)))(*args)
    n_in, n_out, n_sc = len(in_specs), len(out_specs), len(scratch)
    ci, co = len(comm.ins), len(comm.outs)

    def hosted(*refs):
        ins, refs = refs[:n_in], refs[n_in:]
        cins, refs = refs[:ci], refs[ci:]
        outs, refs = refs[:n_out], refs[n_out:]
        couts, refs = refs[:co], refs[co:]
        sc, csems = refs[:n_sc], refs[n_sc:]
        first = last = None
        for d, n in enumerate(grid):
            p = pl.program_id(d)
            first = (p == 0) if first is None else first & (p == 0)
            last = (p == n - 1) if last is None else last & (p == n - 1)

        @pl.when(first)
        def _():
            comm.start(cins, couts, csems)

        body(*ins, *outs, *sc)

        @pl.when(last)
        def _():
            comm.finish(cins, couts, csems)

    any_spec = pl.BlockSpec(memory_space=pl.ANY)
    res = pl.pallas_call(
        hosted, grid=grid, in_specs=in_specs + [any_spec] * ci, out_specs=out_specs + [any_spec] * co,
        out_shape=out_shape + list(comm.outs), scratch_shapes=scratch + list(comm.sems), name=name,
        compiler_params=_cparams(len(grid)))(*args, *comm.ins)
    comm.set_results(res[n_out:])
    return list(res[:n_out])


def _mm(name, grid, operands, pairs, n_acc, acc_shape, extras, outs, epilogue, comm=None, nrow=1, ncol=1):
    nk = grid[2]
    n_op, n_ex, n_out = len(operands), len(extras), len(outs)

    def part_of(ref, dim, t, n):
        if n == 1:
            return ref
        size = ref.shape[dim] // n
        idx = [slice(None)] * len(ref.shape)
        idx[dim] = pl.ds(t * size, size)
        return ref.at[tuple(idx)]

    def tile_of(ref, r, c):
        return part_of(part_of(ref, 0, r, nrow), 1, c, ncol)

    def products(op, r, c):
        parts = [None] * n_acc
        for ai, bi, dims, ci in pairs:
            a = part_of(op[ai], 1 - dims[0][0], r, nrow)
            b = part_of(op[bi], 1 - dims[1][0], c, ncol)
            d = _dot(a[...], b[...], dims)
            parts[ci] = d if parts[ci] is None else parts[ci] + d
        return parts

    def body(*refs):
        op = refs[:n_op]
        ex = refs[n_op:n_op + n_ex]
        out = refs[n_op + n_ex:n_op + n_ex + n_out]
        acc = refs[n_op + n_ex + n_out:]
        tiles = [(r, c) for r in range(nrow) for c in range(ncol)]

        def views(refs_, t):
            return [tile_of(q, *t) for q in refs_]

        if nk == 1:
            parts = products(op, *tiles[0])
            for q, t in enumerate(tiles):
                nxt = products(op, *tiles[q + 1]) if q + 1 < len(tiles) else None
                epilogue(parts, views(ex, t), views(out, t))
                parts = nxt
            return
        k = pl.program_id(2)

        @pl.when(k == 0)
        def _():
            for q in acc:
                q[...] = jnp.zeros_like(q)

        for t in tiles:
            parts = products(op, *t)
            for q, p in zip(views(acc, t), parts):
                q[...] += p

        @pl.when(k == nk - 1)
        def _():
            for t in tiles:
                epilogue([q[...] for q in views(acc, t)], views(ex, t), views(out, t))

    return _call(body, name, grid, [s for _, s in operands] + [s for _, s in extras], [s for _, s in outs],
                 [sh for sh, _ in outs], [a for a, _ in operands] + [a for a, _ in extras],
                 scratch=[pltpu.VMEM(acc_shape, F32) for _ in range(n_acc if nk > 1 else 0)], comm=comm)


def _to_seg(a, seg_len):
    T, D = a.shape
    return a.reshape(SUBLANES, seg_len, D).transpose(1, 0, 2).reshape(T, D)


def _to_tok(a, seg_len):
    T, D = a.shape
    return a.reshape(seg_len, SUBLANES, D).transpose(1, 0, 2).reshape(T, D)


def _rms_fwd(h, gain, out_dtype):
    T, D = h.shape
    bm = min(ROW_TILE, T)

    def body(h_ref, g_ref, o_ref):
        x = h_ref[...]
        r = lax.rsqrt(jnp.mean(x * x, axis=-1, keepdims=True) + RMS_EPS)
        o_ref[...] = (x * r * g_ref[...]).astype(out_dtype)

    row = pl.BlockSpec((bm, D), lambda i: (i, 0))
    return pl.pallas_call(
        body, grid=(T // bm,), in_specs=[row, pl.BlockSpec((1, D), lambda i: (0, 0))],
        out_specs=row, out_shape=jax.ShapeDtypeStruct((T, D), out_dtype), name="rms_fwd",
        compiler_params=_cparams(1))(h, gain)


def _rms_bwd_rows(dn, x, g):
    r = lax.rsqrt(jnp.mean(x * x, axis=-1, keepdims=True) + RMS_EPS)
    xh = x * r
    dng = dn * g
    dx = r * (dng - xh * jnp.mean(dng * xh, axis=-1, keepdims=True))
    return dx, jnp.sum(dn * xh, axis=0, keepdims=True)


def _rms_bwd(dn, h, gain, dh_up):
    T, D = h.shape
    bm = min(ROW_TILE, T)

    def body(dn_ref, h_ref, g_ref, up_ref, dh_ref, dhb_ref, dg_ref):
        dx, dg = _rms_bwd_rows(dn_ref[...], h_ref[...], g_ref[...])
        dh = up_ref[...] + dx
        dh_ref[...] = dh
        dhb_ref[...] = dh.astype(BF)

        @pl.when(pl.program_id(0) == 0)
        def _():
            dg_ref[...] = jnp.zeros_like(dg_ref)

        dg_ref[...] += dg

    row = pl.BlockSpec((bm, D), lambda i: (i, 0))
    vec = pl.BlockSpec((1, D), lambda i: (0, 0))
    return pl.pallas_call(
        body, grid=(T // bm,), in_specs=[row, row, vec, row], out_specs=[row, row, vec],
        out_shape=[jax.ShapeDtypeStruct((T, D), F32), jax.ShapeDtypeStruct((T, D), BF),
                   jax.ShapeDtypeStruct((1, D), F32)],
        name="rms_bwd", compiler_params=_cparams(1))(dn, h, gain, dh_up)


def _loss_head(h, gain, target):
    T, D = h.shape
    bm = min(ROW_TILE, T)

    def body(h_ref, g_ref, t_ref, dh_ref, dhb_ref, dg_ref, loss_ref):
        x = h_ref[...]
        g = g_ref[...]
        r = lax.rsqrt(jnp.mean(x * x, axis=-1, keepdims=True) + RMS_EPS)
        err = x * r * g - t_ref[...]
        part = 0.5 * jnp.sum(jnp.sum(err * err, axis=-1, keepdims=True), axis=0, keepdims=True) / D
        dx, dg = _rms_bwd_rows(err / D, x, g)
        dh_ref[...] = dx
        dhb_ref[...] = dx.astype(BF)

        @pl.when(pl.program_id(0) == 0)
        def _():
            dg_ref[...] = jnp.zeros_like(dg_ref)
            loss_ref[...] = jnp.zeros_like(loss_ref)

        dg_ref[...] += dg
        loss_ref[...] += jnp.broadcast_to(part, loss_ref.shape)

    row = pl.BlockSpec((bm, D), lambda i: (i, 0))
    vec = pl.BlockSpec((1, D), lambda i: (0, 0))
    return pl.pallas_call(
        body, grid=(T // bm,), in_specs=[row, vec, row],
        out_specs=[row, row, vec, pl.BlockSpec((SUBLANES, LANES), lambda i: (0, 0))],
        out_shape=[jax.ShapeDtypeStruct((T, D), F32), jax.ShapeDtypeStruct((T, D), BF),
                   jax.ShapeDtypeStruct((1, D), F32), jax.ShapeDtypeStruct((SUBLANES, LANES), F32)],
        name="loss_head", compiler_params=_cparams(1))(h, gain, target)


def _ffn_up(n, w1g, w3g, comm=None):
    T, D = n.shape
    fs = w1g.shape[-1]
    bm = min(FFN_ROW_TILE, T)
    wspec = pl.BlockSpec((None, D, fs), lambda s, i, k: (s, 0, 0))
    ospec = pl.BlockSpec((None, bm, fs), lambda s, i, k: (s, i, 0))

    def epi(accs, ex, outs):
        a1, a3 = accs
        sg = _sig(a1)
        silu = a1 * sg
        outs[0][...] = (a3 * sg * (1.0 + a1 * (1.0 - sg))).astype(BF)
        outs[1][...] = silu.astype(BF)
        outs[2][...] = (silu * a3).astype(BF)

    sh = jax.ShapeDtypeStruct((N_DEV, T, fs), BF)
    return _mm("ffn_up", (N_DEV, T // bm, 1),
               [(n, pl.BlockSpec((bm, D), lambda s, i, k: (i, 0))), (w1g, wspec), (w3g, wspec)],
               [(0, 1, NN, 0), (0, 2, NN, 1)], 2, None, [], [(sh, ospec)] * 3, epi, comm=comm,
               nrow=max(1, bm // ROW_TILE))


def _ffn_down(g, w2g, h, comm=None):
    _, T, fs = g.shape
    D = h.shape[1]
    bm = min(FFN_ROW_TILE, T)
    row = pl.BlockSpec((bm, D), lambda i, j, s: (i, 0))

    def epi(accs, ex, outs):
        outs[0][...] = ex[0][...] + 0.5 * accs[0]

    return _mm("ffn_down", (T // bm, 1, N_DEV),
               [(g, pl.BlockSpec((None, bm, fs), lambda i, j, s: (s, i, 0))),
                (w2g, pl.BlockSpec((None, fs, D), lambda i, j, s: (s, 0, 0)))],
               [(0, 1, NN, 0)], 1, (bm, D), [(h, row)],
               [(jax.ShapeDtypeStruct((T, D), F32), row)], epi, comm=comm,
               nrow=max(1, bm // ROW_TILE), ncol=max(1, D // COL_TILE))[0]


def _ffn_bwd_hidden(dhb, w2g, t1, t3, comm=None):
    T, D = dhb.shape
    fs = t1.shape[-1]
    bm = min(FFN_ROW_TILE, T)
    aspec = pl.BlockSpec((None, bm, fs), lambda s, i, k: (s, i, 0))

    def epi(accs, ex, outs):
        dg = 0.5 * accs[0]
        outs[0][...] = (dg * ex[0][...].astype(F32)).astype(BF)
        outs[1][...] = (dg * ex[1][...].astype(F32)).astype(BF)

    sh = jax.ShapeDtypeStruct((N_DEV, T, fs), BF)
    return _mm("ffn_bwd_hidden", (N_DEV, T // bm, 1),
               [(dhb, pl.BlockSpec((bm, D), lambda s, i, k: (i, 0))),
                (w2g, pl.BlockSpec((None, fs, D), lambda s, i, k: (s, 0, 0)))],
               [(0, 1, NT, 0)], 1, None, [(t1, aspec), (t3, aspec)], [(sh, aspec)] * 2, epi, comm=comm,
               nrow=max(1, bm // ROW_TILE))


def _ffn_dw2(g, dhb):
    _, T, fs = g.shape
    D = dhb.shape[1]
    bn = min(COL_TILE, D)

    def epi(accs, ex, outs):
        outs[0][...] = (0.5 * accs[0]).astype(BF)

    return _mm("ffn_dw2", (N_DEV, D // bn, 1),
               [(g, pl.BlockSpec((None, T, fs), lambda s, j, k: (s, 0, 0))),
                (dhb, pl.BlockSpec((T, bn), lambda s, j, k: (0, j)))],
               [(0, 1, TN, 0)], 1, None, [],
               [(jax.ShapeDtypeStruct((N_DEV, fs, D), BF), pl.BlockSpec((None, fs, bn), lambda s, j, k: (s, 0, j)))],
               epi)[0]


def _ffn_dw13(n, da1, da3, comm=None):
    T, D = n.shape
    fs = da1.shape[-1]
    bn = min(COL_TILE, D)
    dspec = pl.BlockSpec((None, T, fs), lambda s, j, k: (s, 0, 0))
    ospec = pl.BlockSpec((None, fs, bn), lambda s, j, k: (s, 0, j))

    def epi(accs, ex, outs):
        outs[0][...] = accs[0].astype(BF)
        outs[1][...] = accs[1].astype(BF)

    sh = jax.ShapeDtypeStruct((N_DEV, fs, D), BF)
    return _mm("ffn_dw13", (N_DEV, D // bn, 1),
               [(da1, dspec), (da3, dspec), (n, pl.BlockSpec((T, bn), lambda s, j, k: (0, j)))],
               [(0, 2, TN, 0), (1, 2, TN, 1)], 2, None, [], [(sh, ospec)] * 2, epi, comm=comm)


def _ffn_dn(da1, da3, w1g, w3g, comm=None):
    _, T, fs = da1.shape
    D = w1g.shape[-2]
    bm = min(FFN_ROW_TILE, T)
    dspec = pl.BlockSpec((None, bm, fs), lambda i, j, s: (s, i, 0))
    wspec = pl.BlockSpec((None, D, fs), lambda i, j, s: (s, 0, 0))
    row = pl.BlockSpec((bm, D), lambda i, j, s: (i, 0))

    def epi(accs, ex, outs):
        outs[0][...] = accs[0]

    return _mm("ffn_dn", (T // bm, 1, N_DEV),
               [(da1, dspec), (w1g, wspec), (da3, dspec), (w3g, wspec)],
               [(0, 1, NT, 0), (2, 3, NT, 0)], 1, (bm, D), [],
               [(jax.ShapeDtypeStruct((T, D), F32), row)], epi, comm=comm,
               nrow=max(1, bm // ROW_TILE), ncol=max(1, D // COL_TILE))[0]


def _rope_tables(seq):
    half = LANES // 2
    inv = ROPE_THETA ** (-jnp.arange(0, half, dtype=F32) * 2.0 / LANES)
    ang = jnp.arange(seq, dtype=F32)[:, None] * inv[None, :]
    cos, sin = jnp.cos(ang), jnp.sin(ang)
    return jnp.concatenate([cos, cos], axis=1), jnp.concatenate([-sin, sin], axis=1)


def _branch_bias(nq, bq):
    d = (jnp.arange(nq)[:, None, None] * bq + jnp.arange(bq)[None, :, None]
         - jnp.arange(bq)[None, None, :])
    mult = jnp.zeros(d.shape, F32)
    for window, dil in DILATED_PATTERN:
        mult = mult + ((d >= 0) & (d % dil == 0) & (d <= window)).astype(F32)
    return jnp.where(mult > 0, jnp.log(jnp.maximum(mult, 1.0)), NEG_INF)


def _proj_fwd(u, wing, comm=None):
    T, D = u.shape
    ws = wing.shape[-1]
    bm = min(ROW_TILE, T)

    def epi(accs, ex, outs):
        outs[0][...] = accs[0]

    return _mm("proj_fwd", (N_DEV, T // bm, 1),
               [(u, pl.BlockSpec((bm, D), lambda s, i, k: (i, 0))),
                (wing, pl.BlockSpec((None, D, ws), lambda s, i, k: (s, 0, 0)))],
               [(0, 1, NN, 0)], 1, None, [],
               [(jax.ShapeDtypeStruct((T, N_DEV * ws), F32),
                 pl.BlockSpec((bm, ws), lambda s, i, k: (i, s)))], epi, comm=comm)[0]


def _rope_fwd(proj, cosf, sinf, seq, nh):
    T = proj.shape[0]
    bs = min(ROW_TILE, seq)
    nst = seq // bs
    scale = LANES ** -0.5

    def body(x_ref, c_ref, s_ref, o_ref):
        j = pl.program_id(1)
        t = x_ref[...]
        rot = t * c_ref[...] + pltpu.roll(t, LANES // 2, 1) * s_ref[...]
        rot = rot * jnp.where(j < nh, scale, 1.0)
        o_ref[...] = jnp.where(j < 2 * nh, rot, t).astype(BF)

    blk = pl.BlockSpec((bs, LANES), lambda r, j: (r, j))
    tab = pl.BlockSpec((bs, LANES), lambda r, j: (r % nst, 0))
    return pl.pallas_call(
        body, grid=(T // bs, 3 * nh), in_specs=[blk, tab, tab], out_specs=blk,
        out_shape=jax.ShapeDtypeStruct((T, 3 * nh * LANES), BF), name="rope_fwd",
        compiler_params=_cparams(2))(proj, cosf, sinf)


def _attn_fwd(qkv, bias, nb, seq, nh, comm=None):
    T = nb * seq
    bq = bias.shape[1]
    nq = seq // bq

    def body(q_ref, k_ref, v_ref, b_ref, o_ref, lse_ref):
        qi = pl.program_id(2)
        q = q_ref[...]

        def step(kj, carry):
            m, l, acc = carry
            rows = pl.ds(pl.multiple_of(kj * bq, bq), bq)
            s = _dot(q, k_ref[rows, :], NT) + b_ref[qi - kj]
            m_new = jnp.maximum(m, jnp.max(s, axis=1, keepdims=True))
            p = jnp.exp(s - m_new)
            alpha = jnp.exp(m - m_new)
            l = alpha * l + jnp.sum(p, axis=1, keepdims=True)
            acc = alpha * acc + _dot(p, v_ref[rows, :], NN)
            return m_new, l, acc

        init = (jnp.full((bq, 1), NEG_INF, F32), jnp.zeros((bq, 1), F32), jnp.zeros((bq, LANES), F32))
        m, l, acc = lax.fori_loop(0, qi + 1, step, init)
        o_ref[...] = (acc / l).astype(BF)
        lse_ref[...] = m + jnp.log(l)

    return _call(
        body, "attn_fwd", (nb, nh, nq),
        [pl.BlockSpec((bq, LANES), lambda b, h, i: (b * nq + i, h)),
         pl.BlockSpec((seq, LANES), lambda b, h, i: (b, nh + h)),
         pl.BlockSpec((seq, LANES), lambda b, h, i: (b, 2 * nh + h)),
         pl.BlockSpec((nq, bq, bq), lambda b, h, i: (0, 0, 0))],
        [pl.BlockSpec((bq, LANES), lambda b, h, i: (b * nq + i, h)),
         pl.BlockSpec((None, bq, 1), lambda b, h, i: (h, b * nq + i, 0))],
        [jax.ShapeDtypeStruct((T, 2 * nh * LANES), BF), jax.ShapeDtypeStruct((nh, T, 1), F32)],
        (qkv, qkv, qkv, bias), comm=comm)


def _attn_bwd_dq(qkv, cat, dcat, lse, bias, nb, seq, nh, comm=None):
    T = nb * seq
    bq = bias.shape[1]
    nq = seq // bq

    def body(q_ref, k_ref, v_ref, o_ref, do_ref, lse_ref, b_ref, dq_ref, delta_ref):
        qi = pl.program_id(2)
        q = q_ref[...]
        do = do_ref[...]
        dob = do.astype(BF)
        lse_t = lse_ref[...]
        delta = jnp.sum(do * o_ref[...].astype(F32), axis=1, keepdims=True)
        delta_ref[...] = delta

        def step(kj, dq):
            rows = pl.ds(pl.multiple_of(kj * bq, bq), bq)
            k = k_ref[rows, :]
            p = jnp.exp(_dot(q, k, NT) + b_ref[qi - kj] - lse_t)
            ds = p * (_dot(dob, v_ref[rows, :], NT) - delta)
            return dq + _dot(ds, k, NN)

        dq_ref[...] = lax.fori_loop(0, qi + 1, step, jnp.zeros((bq, LANES), F32))

    tile = pl.BlockSpec((bq, LANES), lambda b, h, i: (b * nq + i, h))
    stat = pl.BlockSpec((None, bq, 1), lambda b, h, i: (h, b * nq + i, 0))
    return _call(
        body, "attn_bwd_dq", (nb, nh, nq),
        [tile, pl.BlockSpec((seq, LANES), lambda b, h, i: (b, nh + h)),
         pl.BlockSpec((seq, LANES), lambda b, h, i: (b, 2 * nh + h)), tile, tile, stat,
         pl.BlockSpec((nq, bq, bq), lambda b, h, i: (0, 0, 0))],
        [tile, stat],
        [jax.ShapeDtypeStruct((T, nh * LANES), F32), jax.ShapeDtypeStruct((nh, T, 1), F32)],
        (qkv, qkv, qkv, cat, dcat, lse, bias), comm=comm)


def _attn_bwd_dkv(qkv, dcat, lse, delta, bias, nb, seq, nh):
    T = nb * seq
    bq = bias.shape[1]
    nq = seq // bq

    def body(k_ref, v_ref, q_ref, do_ref, lse_ref, delta_ref, b_ref, dk_ref, dv_ref):
        kj = pl.program_id(2)
        k = k_ref[...]
        v = v_ref[...]

        def step(qi, carry):
            dk, dv = carry
            rows = pl.ds(pl.multiple_of(qi * bq, bq), bq)
            q = q_ref[rows, :]
            dob = do_ref[rows, :].astype(BF)
            p = jnp.exp(_dot(q, k, NT) + b_ref[qi - kj] - lse_ref[rows, :])
            dv = dv + _dot(p, dob, TN)
            ds = p * (_dot(dob, v, NT) - delta_ref[rows, :])
            return dk + _dot(ds, q, TN), dv

        z = jnp.zeros((bq, LANES), F32)
        dk, dv = lax.fori_loop(kj, nq, step, (z, z))
        dk_ref[...] = dk
        dv_ref[...] = dv

    stat = pl.BlockSpec((None, seq, 1), lambda b, h, i: (h, b, 0))
    out = pl.BlockSpec((bq, LANES), lambda b, h, i: (b * nq + i, h))
    sh = jax.ShapeDtypeStruct((T, nh * LANES), F32)
    return pl.pallas_call(
        body, grid=(nb, nh, nq),
        in_specs=[pl.BlockSpec((bq, LANES), lambda b, h, i: (b * nq + i, nh + h)),
                  pl.BlockSpec((bq, LANES), lambda b, h, i: (b * nq + i, 2 * nh + h)),
                  pl.BlockSpec((seq, LANES), lambda b, h, i: (b, h)),
                  pl.BlockSpec((seq, LANES), lambda b, h, i: (b, h)),
                  stat, stat,
                  pl.BlockSpec((nq, bq, bq), lambda b, h, i: (0, 0, 0))],
        out_specs=[out, out], out_shape=[sh, sh],
        name="attn_bwd_dkv", compiler_params=_cparams(3))(qkv, qkv, qkv, dcat, lse, delta, bias)


def _conv_parts(gc, xin, w_ref):
    w = [w_ref[k:k + 1, :] for k in range(3)]
    u = gc * xin
    row = lax.broadcasted_iota(jnp.int32, u.shape, 0)
    u1 = jnp.where(row >= 1, pltpu.roll(u, 1, 0), 0.0)
    u2 = jnp.where(row >= 2, pltpu.roll(u, 2, 0), 0.0)
    return u, u1, u2, w[0] * u2 + w[1] * u1 + w[2] * u, w, row


def _conv_fwd(proj, conv_w, cat, nb, seq, width):
    cw = min(2 * LANES, width)
    nc = width // cw

    def body(gb_ref, gc_ref, x_ref, w_ref, cat_ref, o_ref):
        _, _, _, conv, _, _ = _conv_parts(gc_ref[...], x_ref[...], w_ref)
        o_ref[...] = (gb_ref[...] * conv).astype(BF)

    def sec(k):
        return pl.BlockSpec((seq, cw), lambda b, c: (b, k * nc + c))

    return pl.pallas_call(
        body, grid=(nb, nc),
        in_specs=[sec(3), sec(4), sec(5), pl.BlockSpec((3, cw), lambda b, c: (0, c)),
                  pl.BlockSpec(memory_space=pl.ANY)],
        out_specs=pl.BlockSpec((seq, cw), lambda b, c: (b, nc + c)),
        out_shape=jax.ShapeDtypeStruct(cat.shape, BF), input_output_aliases={4: 0},
        name="conv_fwd", compiler_params=_cparams(2))(proj, proj, proj, conv_w, cat)


def _conv_bwd(proj, conv_w, dcat, nb, seq, width):
    cw = min(2 * LANES, width)
    nc = width // cw
    T = nb * seq

    def body(gb_ref, gc_ref, x_ref, w_ref, d_ref, dgb_ref, dgc_ref, dx_ref, dw_ref):
        gc = gc_ref[...]
        xin = x_ref[...]
        u, u1, u2, conv, w, row = _conv_parts(gc, xin, w_ref)
        dsc = d_ref[...]
        dgb_ref[...] = dsc * conv
        dconv = dsc * gb_ref[...]
        d1 = jnp.where(row < seq - 1, pltpu.roll(dconv, seq - 1, 0), 0.0)
        d2 = jnp.where(row < seq - 2, pltpu.roll(dconv, seq - 2, 0), 0.0)
        du = w[2] * dconv + w[1] * d1 + w[0] * d2
        dgc_ref[...] = du * xin
        dx_ref[...] = du * gc

        @pl.when(pl.program_id(1) == 0)
        def _():
            dw_ref[...] = jnp.zeros_like(dw_ref)

        dw_ref[0:1, :] += jnp.sum(dconv * u2, axis=0, keepdims=True)
        dw_ref[1:2, :] += jnp.sum(dconv * u1, axis=0, keepdims=True)
        dw_ref[2:3, :] += jnp.sum(dconv * u, axis=0, keepdims=True)

    def sec(k):
        return pl.BlockSpec((seq, cw), lambda c, b: (b, k * nc + c))

    out = pl.BlockSpec((seq, cw), lambda c, b: (b, c))
    wsp = pl.BlockSpec((3, cw), lambda c, b: (0, c))
    sh = jax.ShapeDtypeStruct((T, width), F32)
    return pl.pallas_call(
        body, grid=(nc, nb), in_specs=[sec(3), sec(4), sec(5), wsp, sec(1)],
        out_specs=[out, out, out, wsp], out_shape=[sh, sh, sh, jax.ShapeDtypeStruct((3, width), F32)],
        name="conv_bwd", compiler_params=_cparams(2))(proj, proj, proj, conv_w, dcat)


def _assemble_dproj(dq, dk, dv, dgb, dgc, dxin, cosf, sinf, seq):
    T, width = dq.shape
    nh = width // LANES
    bs = min(256, seq)
    nst = seq // bs
    scale = LANES ** -0.5

    def body(dq_ref, dk_ref, dv_ref, dgb_ref, dgc_ref, dx_ref, c_ref, s_ref, o_ref):
        sec = pl.program_id(1)
        c = c_ref[...]
        s = s_ref[...]

        def unrope(ref, mul):
            for h in range(nh):
                cols = slice(h * LANES, (h + 1) * LANES)
                t = ref[:, cols]
                o_ref[:, cols] = ((t * c + pltpu.roll(t * s, LANES // 2, 1)) * mul).astype(BF)

        @pl.when(sec == 0)
        def _():
            unrope(dq_ref, scale)

        @pl.when(sec == 1)
        def _():
            unrope(dk_ref, 1.0)

        for k, ref in ((2, dv_ref), (3, dgb_ref), (4, dgc_ref), (5, dx_ref)):
            @pl.when(sec == k)
            def _(ref=ref):
                o_ref[...] = ref[...].astype(BF)

    blk = pl.BlockSpec((bs, width), lambda r, k: (r, 0))
    tab = pl.BlockSpec((bs, LANES), lambda r, k: (r % nst, 0))
    return pl.pallas_call(
        body, grid=(T // bs, 6), in_specs=[blk] * 6 + [tab, tab],
        out_specs=pl.BlockSpec((bs, width), lambda r, k: (r, k)),
        out_shape=jax.ShapeDtypeStruct((T, 6 * width), BF), name="assemble_dproj",
        compiler_params=_cparams(2))(dq, dk, dv, dgb, dgc, dxin, cosf, sinf)


def _res_mm(name, a, w, h, comm=None):
    T, K = a.shape
    N = w.shape[1]
    bm = min(ROW_TILE, T)
    bk = min(ROW_TILE, K)
    row = pl.BlockSpec((bm, N), lambda i, j, k: (i, 0))

    def epi(accs, ex, outs):
        outs[0][...] = ex[0][...] + accs[0]

    return _mm(name, (T // bm, 1, K // bk),
               [(a, pl.BlockSpec((bm, bk), lambda i, j, k: (i, k))),
                (w, pl.BlockSpec((bk, N), lambda i, j, k: (k, 0)))],
               [(0, 1, NN, 0)], 1, (bm, N), [(h, row)],
               [(jax.ShapeDtypeStruct((T, N), F32), row)], epi, comm=comm, ncol=max(1, N // COL_TILE))[0]


def _mm_nt(name, a, w, out_dtype):
    T, K = a.shape
    N = w.shape[0]
    bm = min(ROW_TILE, T)
    bn = min(ROW_TILE, N)

    def epi(accs, ex, outs):
        outs[0][...] = accs[0].astype(out_dtype)

    return _mm(name, (T // bm, N // bn, 1),
               [(a, pl.BlockSpec((bm, K), lambda i, j, k: (i, 0))),
                (w, pl.BlockSpec((bn, K), lambda i, j, k: (j, 0)))],
               [(0, 1, NT, 0)], 1, None, [],
               [(jax.ShapeDtypeStruct((T, N), out_dtype), pl.BlockSpec((bm, bn), lambda i, j, k: (i, j)))],
               epi)[0]


def _mm_tn(name, a, bs_list):
    T, M = a.shape
    N = bs_list[0].shape[1]
    bmr = min(COL_TILE, M)
    bn = min(COL_TILE, N)
    n = len(bs_list)

    def epi(accs, ex, outs):
        for q in range(n):
            outs[q][...] = accs[q].astype(BF)

    ops = [(a, pl.BlockSpec((T, bmr), lambda r, j, k: (0, r)))]
    ops += [(b, pl.BlockSpec((T, bn), lambda r, j, k: (0, j))) for b in bs_list]
    return _mm(name, (M // bmr, N // bn, 1), ops, [(0, 1 + q, TN, q) for q in range(n)], n, None, [],
               [(jax.ShapeDtypeStruct((M, N), BF), pl.BlockSpec((bmr, bn), lambda r, j, k: (r, j)))] * n, epi)


def _proj_bwd_x(dproj, wing):
    T = dproj.shape[0]
    _, D, ws = wing.shape
    bm = min(ROW_TILE, T)
    row = pl.BlockSpec((bm, D), lambda i, j, s: (i, 0))

    def epi(accs, ex, outs):
        outs[0][...] = accs[0]

    return _mm("proj_bwd_x", (T // bm, 1, N_DEV),
               [(dproj, pl.BlockSpec((bm, ws), lambda i, j, s: (i, s))),
                (wing, pl.BlockSpec((None, D, ws), lambda i, j, s: (s, 0, 0)))],
               [(0, 1, NT, 0)], 1, (bm, D), [], [(jax.ShapeDtypeStruct((T, D), F32), row)], epi,
               ncol=max(1, D // COL_TILE))[0]


def _proj_dw(u, dproj, ws):
    T, D = u.shape
    bmr = min(COL_TILE, D)

    def epi(accs, ex, outs):
        outs[0][...] = accs[0].astype(BF)

    return _mm("proj_dw", (N_DEV, D // bmr, 1),
               [(u, pl.BlockSpec((T, bmr), lambda s, r, k: (0, r))),
                (dproj, pl.BlockSpec((T, ws), lambda s, r, k: (0, s)))],
               [(0, 1, TN, 0)], 1, None, [],
               [(jax.ShapeDtypeStruct((N_DEV, D, ws), BF),
                 pl.BlockSpec((None, bmr, ws), lambda s, r, k: (s, r, 0)))], epi)[0]


def _mixer_ab_fwd(h, gain, wing, conv_w, wout, tabs, nb, seq, comm_proj=None, comm_attn=None, comm_out=None):
    cosf, sinf, bias = tabs
    width = wing.shape[-1] * N_DEV // 6
    nh = width // LANES
    u = _rms_fwd(h, gain, BF)
    proj = _proj_fwd(u, wing, comm=comm_proj)
    qkv = _rope_fwd(proj, cosf, sinf, seq, nh)
    cat, lse = _attn_fwd(qkv, bias, nb, seq, nh, comm=comm_attn)
    cat = _conv_fwd(proj, conv_w, cat, nb, seq, width)
    return _res_mm("outproj_fwd", cat, wout, h, comm=comm_out), (h, u, proj, qkv, cat, lse)


def _mixer_ab_bwd(dh, dhb, saved, gain, wing, conv_w, wout, tabs, nb, seq, reduce_start, carry):
    cosf, sinf, bias = tabs
    h, u, proj, qkv, cat, lse = saved
    D = h.shape[1]
    ws = wing.shape[-1]
    width = ws * N_DEV // 6
    nh = width // LANES
    dcat = _mm_nt("outproj_bwd_x", dhb, wout, F32)
    dwout = _mm_tn("outproj_dw", cat, [dhb])[0]
    comm = _merge_comms(reduce_start(["ab_w_out"], [dwout.reshape(N_DEV, -1, D)]) + [carry])
    dq, delta = _attn_bwd_dq(qkv, cat, dcat, lse, bias, nb, seq, nh, comm=comm)
    dk, dv = _attn_bwd_dkv(qkv, dcat, lse, delta, bias, nb, seq, nh)
    dgb, dgc, dxin, dconvw = _conv_bwd(proj, conv_w, dcat, nb, seq, width)
    dproj = _assemble_dproj(dq, dk, dv, dgb, dgc, dxin, cosf, sinf, seq)
    du = _proj_bwd_x(dproj, wing)
    comm, = reduce_start(["ab_w_in"], [_proj_dw(u, dproj, ws)])
    dh_in, dhb_in, dgain = _rms_bwd(du, h, gain, dh)
    return dh_in, dhb_in, dgain, dconvw, comm


def _s5_zoh(lr, li, log_dt):
    dt = jnp.exp(log_dt)
    mag = jnp.exp(lr * dt)
    ar = mag * jnp.cos(li * dt)
    ai = mag * jnp.sin(li * dt)
    den = lr * lr + li * li
    return dt, ar, ai, den, ((ar - 1.0) * lr + ai * li) / den, (ai * lr - (ar - 1.0) * li) / den


def _s5_discretize(lam_re, lam_im, log_dt, bt_re, bt_im):
    def body(lr_ref, li_ref, ld_ref, br_ref, bi_ref, ar_ref, ai_ref, bbr_ref, bbi_ref):
        _, ar, ai, _, fr, fi = _s5_zoh(lr_ref[...], li_ref[...], ld_ref[...])
        ar_ref[...] = ar
        ai_ref[...] = ai
        bbr_ref[...] = fr * br_ref[...] - fi * bi_ref[...]
        bbi_ref[...] = fr * bi_ref[...] + fi * br_ref[...]

    small = jax.ShapeDtypeStruct(lam_re.shape, F32)
    big = jax.ShapeDtypeStruct(bt_re.shape, F32)
    return pl.pallas_call(body, out_shape=[small, small, big, big], name="s5_discretize",
                          compiler_params=_cparams(0))(lam_re, lam_im, log_dt, bt_re, bt_im)


def _s5_discretize_bwd(lam_re, lam_im, log_dt, bt_re, bt_im, d_ar, d_ai, d_bbr, d_bbi):

    def body(lr_ref, li_ref, ld_ref, br_ref, bi_ref, dar_ref, dai_ref, dbbr_ref, dbbi_ref,
             dlr_ref, dli_ref, dld_ref, dbr_ref, dbi_ref):
        lr, li = lr_ref[...], li_ref[...]
        dt, ar, ai, den, fr, fi = _s5_zoh(lr, li, ld_ref[...])
        br, bi = br_ref[...], bi_ref[...]
        dbbr, dbbi = dbbr_ref[...], dbbi_ref[...]
        dbr_ref[...] = dbbr * fr + dbbi * fi
        dbi_ref[...] = dbbi * fr - dbbr * fi
        dfr = jnp.sum(dbbr * br + dbbi * bi, axis=1, keepdims=True)
        dfi = jnp.sum(dbbi * br - dbbr * bi, axis=1, keepdims=True)
        dnr = dfr / den
        dni = dfi / den
        dden = -(dfr * fr + dfi * fi) / den
        dar = dar_ref[...] + dnr * lr - dni * li
        dai = dai_ref[...] + dnr * li + dni * lr
        dlr_ref[...] = dnr * (ar - 1.0) + dni * ai + 2.0 * dden * lr + dt * (dar * ar + dai * ai)
        dli_ref[...] = dnr * ai - dni * (ar - 1.0) + 2.0 * dden * li + dt * (dai * ar - dar * ai)
        ddt = jnp.sum(dar * (lr * ar - li * ai) + dai * (lr * ai + li * ar), axis=2, keepdims=True)
        dld_ref[...] = ddt * dt

    small = jax.ShapeDtypeStruct(lam_re.shape, F32)
    big = jax.ShapeDtypeStruct(bt_re.shape, F32)
    return pl.pallas_call(
        body, out_shape=[small, small, jax.ShapeDtypeStruct(log_dt.shape, F32), big, big],
        name="s5_discretize_bwd", compiler_params=_cparams(0))(
            lam_re, lam_im, log_dt, bt_re, bt_im, d_ar, d_ai, d_bbr, d_bbi)


def _rows8(t):
    return pl.ds(pl.multiple_of(t * SUBLANES, SUBLANES), SUBLANES)


def _cmul_add(ar, ai, sr, si, br, bi):
    return ar * sr - ai * si + br, ar * si + ai * sr + bi


def _pair_step(a, a2, p2, p1, b_prev, b0, b1):
    c0 = _cmul_add(*a, *b_prev, *b0)
    c1 = _cmul_add(*a, *b0, *b1)
    return _cmul_add(*a2, *p2, *c0), _cmul_add(*a2, *p1, *c1)


def _csquare(a):
    return a[0] * a[0] - a[1] * a[1], 2.0 * a[0] * a[1]


def _scan_pairs(a, read, write, init, n_pairs, index):
    a2 = _csquare(a)
    zero = jnp.zeros_like(init[0])

    def trip(i, c):
        t0, t1 = index(i)
        b1 = read(t1)
        s0, s1 = _pair_step(a, a2, c[0:2], c[2:4], c[4:6], read(t0), b1)
        if write is not None:
            write(t0, s0)
            write(t1, s1)
        return (*s0, *s1, *b1)

    out = lax.fori_loop(0, n_pairs, trip, (zero, zero, *init, *init), unroll=SCAN_UNROLL // 2)
    return out[2], out[3]


def _cpow(ar, ai, n):
    rr = ri = None
    while n:
        if n & 1:
            rr, ri = (ar, ai) if rr is None else (rr * ar - ri * ai, rr * ai + ri * ar)
        ar, ai = ar * ar - ai * ai, 2.0 * ar * ai
        n >>= 1
    return rr, ri


def _s5_specs(R, nj):
    sh = STATE_COLS
    return dict(
        rows=pl.BlockSpec((R, LANES), lambda j: (0, j)),
        bd=pl.BlockSpec((None, LANES, sh), lambda j: (j, 0, 0)),
        cd=pl.BlockSpec((None, sh, LANES), lambda j: (j, 0, 0)),
        a=pl.BlockSpec((None, 1, sh), lambda j: (j, 0, 0)),
        vec=pl.BlockSpec((1, LANES), lambda j: (0, j)),
        init=pl.BlockSpec((None, SUBLANES, sh), lambda j: (j, 0, 0)))


def _s5_fwd(u, mats, seg_len, nseg, comm=None):
    bdr, bdi, cdr, cdi, are, aim, dsk = mats
    R, D = u.shape
    nj = D // LANES
    sh = STATE_COLS
    rc = min(R, 512)
    sp = _s5_specs(R, nj)

    def body(u_ref, bdr_ref, bdi_ref, cdr_ref, cdi_ref, ar_ref, ai_ref, d_ref,
             y_ref, yg_ref, ir_ref, ii_ref, sre, sim):
        ar = jnp.broadcast_to(ar_ref[...], (SUBLANES, sh))
        ai = jnp.broadcast_to(ai_ref[...], (SUBLANES, sh))

        def bu_chunk(c, _):
            rows = pl.ds(pl.multiple_of(c * rc, rc), rc)
            ub = u_ref[rows, :].astype(BF)
            sre[rows, :] = _dot(ub, bdr_ref[...], NN)
            sim[rows, :] = _dot(ub, bdi_ref[...], NN)
            return 0

        lax.fori_loop(0, R // rc, bu_chunk, 0)
        z = jnp.zeros((SUBLANES, sh), F32)

        def read(t):
            return sre[_rows8(t), :], sim[_rows8(t), :]

        def write(t, s):
            sre[_rows8(t), :] = s[0]
            sim[_rows8(t), :] = s[1]

        def forward(i):
            return 2 * i, 2 * i + 1

        er, ei = _scan_pairs((ar, ai), read, None, (z, z), seg_len // 2, forward)
        pr, pi = _cpow(ar, ai, seg_len)
        first = (lax.broadcasted_iota(jnp.int32, (SUBLANES, sh), 0) & (nseg - 1)) == 0

        def prev(x):
            return jnp.where(first, 0.0, pltpu.roll(x, 1, 0))

        xr, xi = er, ei
        for _ in range(nseg - 1):
            xr, xi = _cmul_add(pr, pi, prev(xr), prev(xi), er, ei)
        i_r, i_i = prev(xr), prev(xi)
        ir_ref[...] = i_r
        ii_ref[...] = i_i
        _scan_pairs((ar, ai), read, write, (i_r, i_i), seg_len // 2, forward)

        def y_chunk(c, _):
            rows = pl.ds(pl.multiple_of(c * rc, rc), rc)
            y = _dot(sre[rows, :], cdr_ref[...], NN) + _dot(sim[rows, :], cdi_ref[...], NN)
            y = y + d_ref[...] * u_ref[rows, :]
            y_ref[rows, :] = y
            yg_ref[rows, :] = _gelu(y).astype(BF)
            return 0

        lax.fori_loop(0, R // rc, y_chunk, 0)

    init_sh = jax.ShapeDtypeStruct((nj, SUBLANES, STATE_COLS), F32)
    return _call(
        body, "s5_fwd", (nj,),
        [sp["rows"], sp["bd"], sp["bd"], sp["cd"], sp["cd"], sp["a"], sp["a"], sp["vec"]],
        [sp["rows"], sp["rows"], sp["init"], sp["init"]],
        [jax.ShapeDtypeStruct((R, D), F32), jax.ShapeDtypeStruct((R, D), BF), init_sh, init_sh],
        (u, bdr, bdi, cdr, cdi, are, aim, dsk),
        scratch=[pltpu.VMEM((R, sh), F32) for _ in range(2)], comm=comm)


def _s5_bwd(u, dy, mats, init_re, init_im, seg_len, nseg, comm=None):
    bdr, bdi, cdr, cdi, are, aim, dsk = mats
    R, D = u.shape
    nj = D // LANES
    sh = STATE_COLS
    rc = min(R, 512)
    sp = _s5_specs(R, nj)

    def body(u_ref, dy_ref, bdr_ref, bdi_ref, cdr_ref, cdi_ref, ar_ref, ai_ref, d_ref, ir_ref, ii_ref,
             du_ref, dbdr_ref, dbdi_ref, dcdr_ref, dcdi_ref, dar_ref, dai_ref, dd_ref,
             sre, sim, gre, gim):
        ar = jnp.broadcast_to(ar_ref[...], (SUBLANES, sh))
        ai = jnp.broadcast_to(ai_ref[...], (SUBLANES, sh))
        i_r, i_i = ir_ref[...], ii_ref[...]

        def chunk(c):
            return pl.ds(pl.multiple_of(c * rc, rc), rc)

        def bu_chunk(c, _):
            ub = u_ref[chunk(c), :].astype(BF)
            sre[chunk(c), :] = _dot(ub, bdr_ref[...], NN)
            sim[chunk(c), :] = _dot(ub, bdi_ref[...], NN)
            return 0

        lax.fori_loop(0, R // rc, bu_chunk, 0)

        def read_s(t):
            return sre[_rows8(t), :], sim[_rows8(t), :]

        def write_s(t, s):
            sre[_rows8(t), :] = s[0]
            sim[_rows8(t), :] = s[1]

        _scan_pairs((ar, ai), read_s, write_s, (i_r, i_i), seg_len // 2, lambda i: (2 * i, 2 * i + 1))

        def c_chunk(c, carry):
            dyb = dy_ref[chunk(c), :].astype(BF)
            gre[chunk(c), :] = _dot(dyb, cdr_ref[...], NT)
            gim[chunk(c), :] = _dot(dyb, cdi_ref[...], NT)
            return (carry[0] + _dot(sre[chunk(c), :], dyb, TN), carry[1] + _dot(sim[chunk(c), :], dyb, TN))

        zc = jnp.zeros((sh, LANES), F32)
        dcr, dci = lax.fori_loop(0, R // rc, c_chunk, (zc, zc))
        dcdr_ref[...] = dcr
        dcdi_ref[...] = dci

        a_adj = (ar, -ai)

        def read_g(t):
            return gre[_rows8(t), :], gim[_rows8(t), :]

        def backward(i):
            return seg_len - 1 - 2 * i, seg_len - 2 - 2 * i

        z = jnp.zeros((SUBLANES, sh), F32)
        fr, fi = _scan_pairs(a_adj, read_g, None, (z, z), seg_len // 2, backward)
        pr, pi = _cpow(ar, ai, seg_len)
        last =(lax.broadcasted_iota(jnp.int32, (SUBLANES, sh), 0) & (nseg - 1)) == nseg - 1

        def nxt(x):
            return jnp.where(last, 0.0, pltpu.roll(x, SUBLANES - 1, 0))

        xr, xi = fr, fi
        for _ in range(nseg - 1):
            xr, xi = _cmul_add(pr, -pi, nxt(xr), nxt(xi), fr, fi)
        g0r, g0i = nxt(xr), nxt(xi)

        a2_adj = _csquare(a_adj)

        def adj_trip(i, c, s_before_t1=None):
            t0, t1 = backward(i)
            g1 = read_g(t1)
            G0, G1 = _pair_step(a_adj, a2_adj, c[0:2], c[2:4], c[4:6], read_g(t0), g1)
            gre[_rows8(t0), :], gim[_rows8(t0), :] = G0
            gre[_rows8(t1), :], gim[_rows8(t1), :] = G1
            s0 = read_s(t1)
            s1 = read_s(t1 - 1) if s_before_t1 is None else s_before_t1
            dar = c[6] + s0[0] * G0[0] + s0[1] * G0[1] + s1[0] * G1[0] + s1[1] * G1[1]
            dai = c[7] + s0[0] * G0[1] - s0[1] * G0[0] + s1[0] * G1[1] - s1[1] * G1[0]
            return (*G0, *G1, *g1, dar, dai)

        carry = lax.fori_loop(0, seg_len // 2 - 1, adj_trip, (z, z, g0r, g0i, g0r, g0i, z, z))
        carry = adj_trip(seg_len // 2 - 1, carry, s_before_t1=(i_r, i_i))
        dar_ref[...] = jnp.sum(carry[6], axis=0, keepdims=True)
        dai_ref[...] = jnp.sum(carry[7], axis=0, keepdims=True)

        def d_chunk(c, carry):
            ub = u_ref[chunk(c), :].astype(BF)
            grb = gre[chunk(c), :].astype(BF)
            gib = gim[chunk(c), :].astype(BF)
            du = _dot(grb, bdr_ref[...], NT) + _dot(gib, bdi_ref[...], NT)
            du_ref[chunk(c), :] = du + d_ref[...] * dy_ref[chunk(c), :]
            dd = carry[2] + jnp.sum(dy_ref[chunk(c), :] * u_ref[chunk(c), :], axis=0, keepdims=True)
            return carry[0] + _dot(ub, grb, TN), carry[1] + _dot(ub, gib, TN), dd

        zb = jnp.zeros((LANES, sh), F32)
        dbr, dbi, dd = lax.fori_loop(0, R // rc, d_chunk, (zb, zb, jnp.zeros((1, LANES), F32)))
        dbdr_ref[...] = dbr
        dbdi_ref[...] = dbi
        dd_ref[...] = dd

    bd_sh = jax.ShapeDtypeStruct((nj, LANES, STATE_COLS), F32)
    cd_sh = jax.ShapeDtypeStruct((nj, STATE_COLS, LANES), F32)
    a_sh = jax.ShapeDtypeStruct((nj, 1, STATE_COLS), F32)
    return _call(
        body, "s5_bwd", (nj,),
        [sp["rows"], sp["rows"], sp["bd"], sp["bd"], sp["cd"], sp["cd"], sp["a"], sp["a"],
         sp["vec"], sp["init"], sp["init"]],
        [sp["rows"], sp["bd"], sp["bd"], sp["cd"], sp["cd"], sp["a"], sp["a"], sp["vec"]],
        [jax.ShapeDtypeStruct((R, D), F32), bd_sh, bd_sh, cd_sh, cd_sh, a_sh, a_sh,
         jax.ShapeDtypeStruct((1, D), F32)],
        (u, dy, bdr, bdi, cdr, cdi, are, aim, dsk, init_re, init_im),
        scratch=[pltpu.VMEM((R, sh), F32) for _ in range(4)], comm=comm)


def _glu_fwd(yg, wa, wb, h):
    T, D = yg.shape
    N = wa.shape[1]
    bm = min(ROW_TILE, T)
    bn = min(ROW_TILE, N)
    wspec = pl.BlockSpec((D, bn), lambda i, j, k: (0, j))
    ospec = pl.BlockSpec((bm, bn), lambda i, j, k: (i, j))

    def epi(accs, ex, outs):
        pa, pb = accs
        outs[0][...] = ex[0][...] + pa * _sig(pb)
        outs[1][...] = pa.astype(BF)
        outs[2][...] = pb.astype(BF)

    return _mm("glu_fwd", (T // bm, N // bn, 1),
               [(yg, pl.BlockSpec((bm, D), lambda i, j, k: (i, 0))), (wa, wspec), (wb, wspec)],
               [(0, 1, NN, 0), (0, 2, NN, 1)], 2, None, [(h, ospec)],
               [(jax.ShapeDtypeStruct((T, N), F32), ospec), (jax.ShapeDtypeStruct((T, N), BF), ospec),
                (jax.ShapeDtypeStruct((T, N), BF), ospec)], epi)


def _glu_bwd_gates(dz, pa, pb):
    T, D = dz.shape
    bm = min(ROW_TILE, T)

    def body(dz_ref, pa_ref, pb_ref, dpa_ref, dpb_ref):
        dz = dz_ref[...]
        sg = _sig(pb_ref[...].astype(F32))
        dpa_ref[...] = (dz * sg).astype(BF)
        dpb_ref[...] = (dz * pa_ref[...].astype(F32) * sg * (1.0 - sg)).astype(BF)

    row = pl.BlockSpec((bm, D), lambda i: (i, 0))
    return pl.pallas_call(
        body, grid=(T // bm,), in_specs=[row] * 3, out_specs=[row] * 2,
        out_shape=[jax.ShapeDtypeStruct((T, D), BF)] * 2, name="glu_bwd_gates",
        compiler_params=_cparams(1))(dz, pa, pb)


def _glu_bwd_y(dpa, dpb, wa, wb, y_pre, comm=None):
    T, N = dpa.shape
    D = wa.shape[0]
    bm = min(ROW_TILE, T)
    bn = min(ROW_TILE, D)
    aspec = pl.BlockSpec((bm, N), lambda i, j, k: (i, 0))
    wspec = pl.BlockSpec((bn, N), lambda i, j, k: (j, 0))
    ospec = pl.BlockSpec((bm, bn), lambda i, j, k: (i, j))

    def epi(accs, ex, outs):
        outs[0][...] = accs[0] * _gelu_grad(ex[0][...])

    return _mm("glu_bwd_y", (T // bm, D // bn, 1), [(dpa, aspec), (wa, wspec), (dpb, aspec), (wb, wspec)],
               [(0, 1, NT, 0), (2, 3, NT, 0)], 1, None, [(y_pre, ospec)],
               [(jax.ShapeDtypeStruct((T, D), F32), ospec)], epi, comm=comm)[0]


def _block_diag_in(x, nj):
    g = GROUPS_PER_BLOCK
    x = x.reshape(nj, g, 1, S5_GROUP, S5_STATE)
    eye = jnp.eye(g, dtype=bool)[None, :, :, None, None]
    full = jnp.where(eye, x, 0.0)
    return full.transpose(0, 1, 3, 2, 4).reshape(nj, g * S5_GROUP, g * S5_STATE)


def _block_diag_out(x, nj):
    return _block_diag_in(x, nj).transpose(0, 2, 1)


def _diag_of_in(m, nj):
    g = GROUPS_PER_BLOCK
    m5 = m.reshape(nj, g, S5_GROUP, g, S5_STATE)
    d = jnp.diagonal(m5, axis1=1, axis2=3)
    return d.transpose(0, 3, 1, 2).reshape(nj * g, S5_GROUP, S5_STATE)


def _mixer_s5_fwd(h, gain, p, dsk, wa, wb, nb, seq, comm_s5=None):
    T, D = h.shape
    nj = D // LANES
    nseg = SUBLANES // nb
    seg_len = seq // nseg
    G = p["s5_lambda_re"].shape[1]
    lam_re = p["s5_lambda_re"].reshape(G, 1, S5_STATE)
    lam_im = p["s5_lambda_im"].reshape(G, 1, S5_STATE)
    log_dt = p["s5_log_dt"].reshape(G, 1, 1)
    bt_re = p["s5_b_re"][0].transpose(0, 2, 1)
    bt_im = p["s5_b_im"][0].transpose(0, 2, 1)
    ar, ai, bbr, bbi = _s5_discretize(lam_re, lam_im, log_dt, bt_re, bt_im)
    mats = (_block_diag_in(bbr, nj).astype(BF), _block_diag_in(bbi, nj).astype(BF),
            _block_diag_out(p["s5_c_re"][0], nj).astype(BF),
            _block_diag_out(-p["s5_c_im"][0], nj).astype(BF),
            ar.reshape(nj, 1, STATE_COLS), ai.reshape(nj, 1, STATE_COLS), dsk)
    h_seg = _to_seg(h, seg_len)
    u = _rms_fwd(h_seg, gain, F32)
    y_pre, yg, init_re, init_im = _s5_fwd(u, mats, seg_len, nseg, comm=comm_s5)
    h_out, pa, pb = _glu_fwd(yg, wa, wb, h_seg)
    disc_in = (lam_re, lam_im, log_dt, bt_re, bt_im)
    return _to_tok(h_out, seg_len), (h_seg, u, mats, y_pre, yg, init_re, init_im, pa, pb, disc_in, seg_len, nseg)


def _mixer_s5_bwd(dh, saved, gain, wa, wb, reduce_start, carry):
    h_seg, u, mats, y_pre, yg, init_re, init_im, pa, pb, disc_in, seg_len, nseg = saved
    T, D = h_seg.shape
    nj = D // LANES
    G = nj * GROUPS_PER_BLOCK
    dh_seg = _to_seg(dh, seg_len)
    dpa, dpb = _glu_bwd_gates(dh_seg, pa, pb)
    dy = _glu_bwd_y(dpa, dpb, wa, wb, y_pre)
    dwa, dwb = _mm_tn("glu_dw", yg, [dpa, dpb])
    comm = _merge_comms(reduce_start(["s5_glu_wa", "s5_glu_wb"],
                                     [dwa.reshape(N_DEV, -1, D), dwb.reshape(N_DEV, -1, D)]) + [carry])
    du, dbdr, dbdi, dcdr, dcdi, dar, dai, dd = _s5_bwd(u, dy, mats, init_re, init_im, seg_len, nseg, comm=comm)
    d_bbr = _diag_of_in(dbdr, nj)
    d_bbi = _diag_of_in(dbdi, nj)
    d_c_re = _diag_of_in(dcdr.transpose(0, 2, 1), nj)
    d_c_im = -_diag_of_in(dcdi.transpose(0, 2, 1), nj)
    dlr, dli, dld, dbr, dbi = _s5_discretize_bwd(
        *disc_in, dar.reshape(G, 1, S5_STATE), dai.reshape(G, 1, S5_STATE), d_bbr, d_bbi)
    small = {"s5_lambda_re": dlr.reshape(1, G, S5_STATE), "s5_lambda_im": dli.reshape(1, G, S5_STATE),
             "s5_log_dt": dld.reshape(1, G),
             "s5_b_re": dbr.transpose(0, 2, 1)[None], "s5_b_im": dbi.transpose(0, 2, 1)[None],
             "s5_c_re": d_c_re[None], "s5_c_im": d_c_im[None], "s5_d": dd}
    dh_in, _, dgain = _rms_bwd(du, h_seg, gain, dh_seg)
    dh_in = _to_tok(dh_in, seg_len)
    return dh_in, dh_in.astype(BF), dgain, small


def _mesh_pos():
    return lax.axis_index("x"), lax.axis_index("y"), lax.axis_index("c")


class _Gather:
    def __init__(self, srcs, slots, send_sems, recv_sems):
        self.srcs, self.slots, self.send_sems, self.recv_sems = srcs, slots, send_sems, recv_sems
        x, y, c = _mesh_pos()
        self.c = c
        self.me, self.sib = (x, y, c), (x, y, 1 - c)
        self.chips = [(1 - x, y), (x, 1 - y), (1 - x, 1 - y)]

    def copy(self, a, k, block, to, own=False):
        dst = self.slots[a].at[4 * block[0] + 2 * block[1] + block[2]]
        return pltpu.make_async_remote_copy(
            src_ref=self.srcs[a] if own else dst, dst_ref=dst, send_sem=self.send_sems.at[7 * a + k],
            recv_sem=self.recv_sems.at[7 * a + k], device_id=to, device_id_type=MESH)

    def own_copies(self, a):
        cps = [self.copy(a, 0, self.me, self.sib, own=True)]
        return cps + [self.copy(a, 1 + j, self.me, (*chip, self.c), own=True) for j, chip in enumerate(self.chips)]

    def start(self):
        for a in range(len(self.srcs)):
            for cp in self.own_copies(a):
                cp.start()

    def finish(self):
        n = len(self.srcs)
        for a in range(n):
            for j, chip in enumerate(self.chips):
                self.copy(a, 1 + j, (*chip, self.c), self.me).wait_recv()
                self.copy(a, 4 + j, (*chip, self.c), self.sib).start()
        for a in range(n):
            self.copy(a, 0, self.sib, self.me).wait_recv()
            for j, chip in enumerate(self.chips):
                self.copy(a, 4 + j, (*chip, 1 - self.c), self.me).wait_recv()
        for a in range(n):
            for cp in self.own_copies(a):
                cp.wait_send()
            for j, chip in enumerate(self.chips):
                self.copy(a, 4 + j, (*chip, self.c), self.sib).wait_send()


def _gather_comm(arrs):
    n = len(arrs)

    def local(xs, outs, sems, a):
        x, y, c = _mesh_pos()
        return pltpu.make_async_copy(xs[a], outs[a].at[4 * x + 2 * y + c], sems[2].at[a])

    def start(xs, outs, sems):
        for a in range(n):
            local(xs, outs, sems, a).start()
        _Gather(xs, outs, sems[0], sems[1]).start()

    def finish(xs, outs, sems):
        _Gather(xs, outs, sems[0], sems[1]).finish()
        for a in range(n):
            local(xs, outs, sems, a).wait()

    return _Comm(list(arrs), [jax.ShapeDtypeStruct((N_DEV,) + a.shape, a.dtype) for a in arrs],
                 [pltpu.SemaphoreType.DMA((7 * n,)), pltpu.SemaphoreType.DMA((7 * n,)),
                  pltpu.SemaphoreType.DMA((n,))], start, finish)


def _exchange_comm(parts):
    n = len(parts)

    def copies(ps, outs, sems):
        x, y, c = _mesh_pos()
        cps = []
        for a in range(n):
            for j in range(1, 4):
                to = (jnp.bitwise_xor(x, j // 2), jnp.bitwise_xor(y, j % 2), c)
                cps.append(pltpu.make_async_remote_copy(
                    src_ref=ps[a].at[j], dst_ref=outs[a].at[j - 1], send_sem=sems[0].at[3 * a + j - 1],
                    recv_sem=sems[1].at[3 * a + j - 1], device_id=to, device_id_type=MESH))
        return cps

    def start(ps, outs, sems):
        for cp in copies(ps, outs, sems):
            cp.start()

    def finish(ps, outs, sems):
        for cp in copies(ps, outs, sems):
            cp.wait()

    return _Comm(list(parts), [jax.ShapeDtypeStruct((3,) + p.shape[1:], p.dtype) for p in parts],
                 [pltpu.SemaphoreType.DMA((3 * n,)), pltpu.SemaphoreType.DMA((3 * n,))], start, finish)


def _run_comm(comm, name):
    ci, co = len(comm.ins), len(comm.outs)

    def body(*refs):
        comm.start(refs[:ci], refs[ci:ci + co], refs[ci + co:])
        comm.finish(refs[:ci], refs[ci:ci + co], refs[ci + co:])

    any_spec = pl.BlockSpec(memory_space=pl.ANY)
    comm.set_results(pl.pallas_call(
        body, in_specs=[any_spec] * ci, out_specs=[any_spec] * co, out_shape=list(comm.outs),
        scratch_shapes=list(comm.sems), name=name, compiler_params=_cparams(0))(*comm.ins))


def _pair_exchange(grads, name):
    n = len(grads)

    def body(*refs):
        gs, outs = refs[:n], refs[n:2 * n]
        send_sems, recv_sems = refs[2 * n:]
        x, y, c = _mesh_pos()
        copies = []
        for a in range(n):
            for k in range(4):
                copies.append(pltpu.make_async_remote_copy(
                    src_ref=gs[a].at[2 * k + 1 - c], dst_ref=outs[a].at[k], send_sem=send_sems.at[4 * a + k],
                    recv_sem=recv_sems.at[4 * a + k], device_id=(x, y, 1 - c), device_id_type=MESH))
        for cp in copies:
            cp.start()
        for cp in copies:
            cp.wait()

    any_spec = pl.BlockSpec(memory_space=pl.ANY)
    return pl.pallas_call(
        body, in_specs=[any_spec] * n, out_specs=[any_spec] * n,
        out_shape=[jax.ShapeDtypeStruct((4,) + g.shape[1:], g.dtype) for g in grads],
        scratch_shapes=[pltpu.SemaphoreType.DMA((4 * n,)), pltpu.SemaphoreType.DMA((4 * n,))],
        name=name, compiler_params=_cparams(0))(*grads)


def _pair_sum(grad, recv, pos):
    _, R, C = grad.shape
    br = _row_block(R, C, PAIR_SUM_ELEMS)

    def body(pos_ref, g_ref, r_ref, o_ref):
        o_ref[...] = (g_ref[...].astype(F32) + r_ref[...].astype(F32)).astype(BF)

    def chip(j, p):
        return jnp.bitwise_xor(p[1], j)

    return pl.pallas_call(
        body, grid_spec=pltpu.PrefetchScalarGridSpec(
            num_scalar_prefetch=1, grid=(4, R // br),
            in_specs=[pl.BlockSpec((None, br, C), lambda j, i, p: (2 * chip(j, p) + p[0], i, 0)),
                      pl.BlockSpec((None, br, C), lambda j, i, p: (chip(j, p), i, 0))],
            out_specs=pl.BlockSpec((None, br, C), lambda j, i, p: (j, i, 0))),
        out_shape=jax.ShapeDtypeStruct((4, R, C), BF), name="pair_sum", compiler_params=_cparams(2))(pos, grad, recv)


def _adamw(w, g, m, v):
    m = ADAM_B1 * m + (1.0 - ADAM_B1) * g
    v = ADAM_B2 * v + (1.0 - ADAM_B2) * (g * g)
    m_hat = m / (1.0 - ADAM_B1 ** ADAM_STEP)
    v_hat = v / (1.0 - ADAM_B2 ** ADAM_STEP)
    return -ADAM_LR * (m_hat / (jnp.sqrt(v_hat) + ADAM_EPS) + ADAM_WD * w), m, v


def _adamw_piece(w, m, v, piece, part, recv, bufs):
    _, R, C = w.shape
    br = _row_block(R, C)

    def body(w_ref, m_ref, v_ref, p_ref, r_ref, b0, b1, b2, b3, g_ref, d_ref, nm_ref, nv_ref):
        g = p_ref[...].astype(F32)
        for j in range(3):
            g = g + r_ref[j].astype(F32)
        d, nm, nv = _adamw(w_ref[...], g, m_ref[...], v_ref[...])
        g_ref[...] = g
        d_ref[...] = d
        nm_ref[...] = nm
        nv_ref[...] = nv

    row = pl.BlockSpec((None, br, C), lambda i: (piece, i, 0))
    any_spec = pl.BlockSpec(memory_space=pl.ANY)
    return pl.pallas_call(
        body, grid=(R // br,),
        in_specs=[row, row, row, pl.BlockSpec((None, br, C), lambda i: (0, i, 0)),
                  pl.BlockSpec((3, br, C), lambda i: (0, i, 0))] + [any_spec] * 4,
        out_specs=[row] * 4, out_shape=[jax.ShapeDtypeStruct(w.shape, F32)] * 4,
        input_output_aliases={5: 0, 6: 1, 7: 2, 8: 3}, name="adamw_piece",
        compiler_params=_cparams(1))(w, m, v, part, recv, *bufs)


def _all_reduce_small(x):
    rows = x.shape[0]

    def body(x_ref, o_ref, buf, send_sems, recv_sems):
        xp, yp, cp = _mesh_pos()
        buf[4 * xp + 2 * yp + cp] = x_ref[...]
        gather = _Gather([x_ref], [buf], send_sems, recv_sems)
        gather.start()
        gather.finish()
        acc = buf[0]
        for d in range(1, N_DEV):
            acc = acc + buf[d]
        o_ref[...] = acc

    vm = pl.BlockSpec(memory_space=pltpu.VMEM)
    return pl.pallas_call(
        body, in_specs=[vm], out_specs=vm, out_shape=jax.ShapeDtypeStruct(x.shape, F32),
        scratch_shapes=[pltpu.VMEM((N_DEV, rows, LANES), F32), pltpu.SemaphoreType.DMA((7,)),
                        pltpu.SemaphoreType.DMA((7,))],
        name="all_reduce_small", compiler_params=_cparams(0))(x)


def _adamw_small(w, g, m, v):
    def body(w_ref, g_ref, m_ref, v_ref, d_ref, nm_ref, nv_ref):
        d, nm, nv = _adamw(w_ref[...], g_ref[...], m_ref[...], v_ref[...])
        d_ref[...] = d
        nm_ref[...] = nm
        nv_ref[...] = nv

    sh = jax.ShapeDtypeStruct(w.shape, F32)
    return pl.pallas_call(body, out_shape=[sh] * 3, name="adamw_small", compiler_params=_cparams(0))(w, g, m, v)


def _pack(arrs):
    flat = jnp.concatenate([a.reshape(-1).astype(F32) for a in arrs])
    rows = -(-flat.shape[0] // (SUBLANES * LANES)) * SUBLANES
    return jnp.pad(flat, (0, rows * LANES - flat.shape[0])).reshape(rows, LANES)


def _unpack(buf, shapes):
    flat = buf.reshape(-1)
    out, off = [], 0
    for s in shapes:
        n = 1
        for d in s:
            n *= d
        out.append(flat[off:off + n].reshape(s))
        off += n
    return out


BIG = ("ffn_w1", "ffn_w3", "ffn_w2", "ab_w_in", "ab_w_out", "s5_glu_wa", "s5_glu_wb")
NAMES = ("ln_ffn_pre", "ln_mix", "ln_ffn_post", "ln_final", "ffn_w1", "ffn_w3", "ffn_w2", "ab_w_in",
         "ab_conv_w", "ab_w_out", "s5_lambda_re", "s5_lambda_im", "s5_log_dt", "s5_b_re", "s5_b_im",
         "s5_c_re", "s5_c_im", "s5_d", "s5_glu_wa", "s5_glu_wb")


def kernel(x, ln_ffn_pre, ln_mix, ln_ffn_post, ln_final, ffn_w1, ffn_w3, ffn_w2, ab_w_in, ab_conv_w, ab_w_out, s5_lambda_re, s5_lambda_im, s5_log_dt, s5_b_re, s5_b_im, s5_c_re, s5_c_im, s5_d, s5_glu_wa, s5_glu_wb, loss_target, m_ln_ffn_pre, m_ln_mix, m_ln_ffn_post, m_ln_final, m_ffn_w1, m_ffn_w3, m_ffn_w2, m_ab_w_in, m_ab_conv_w, m_ab_w_out, m_s5_lambda_re, m_s5_lambda_im, m_s5_log_dt, m_s5_b_re, m_s5_b_im, m_s5_c_re, m_s5_c_im, m_s5_d, m_s5_glu_wa, m_s5_glu_wb, v_ln_ffn_pre, v_ln_mix, v_ln_ffn_post, v_ln_final, v_ffn_w1, v_ffn_w3, v_ffn_w2, v_ab_w_in, v_ab_conv_w, v_ab_w_out, v_s5_lambda_re, v_s5_lambda_im, v_s5_log_dt, v_s5_b_re, v_s5_b_im, v_s5_c_re, v_s5_c_im, v_s5_d, v_s5_glu_wa, v_s5_glu_wb):
    w = dict(zip(NAMES, (ln_ffn_pre, ln_mix, ln_ffn_post, ln_final, ffn_w1, ffn_w3, ffn_w2, ab_w_in, ab_conv_w,
                         ab_w_out, s5_lambda_re, s5_lambda_im, s5_log_dt, s5_b_re, s5_b_im, s5_c_re, s5_c_im,
                         s5_d, s5_glu_wa, s5_glu_wb)))
    mom = dict(zip(NAMES, (m_ln_ffn_pre, m_ln_mix, m_ln_ffn_post, m_ln_final, m_ffn_w1, m_ffn_w3, m_ffn_w2,
                           m_ab_w_in, m_ab_conv_w, m_ab_w_out, m_s5_lambda_re, m_s5_lambda_im, m_s5_log_dt,
                           m_s5_b_re, m_s5_b_im, m_s5_c_re, m_s5_c_im, m_s5_d, m_s5_glu_wa, m_s5_glu_wb)))
    var = dict(zip(NAMES, (v_ln_ffn_pre, v_ln_mix, v_ln_ffn_post, v_ln_final, v_ffn_w1, v_ffn_w3, v_ffn_w2,
                           v_ab_w_in, v_ab_conv_w, v_ab_w_out, v_s5_lambda_re, v_s5_lambda_im, v_s5_log_dt,
                           v_s5_b_re, v_s5_b_im, v_s5_c_re, v_s5_c_im, v_s5_d, v_s5_glu_wa, v_s5_glu_wb)))
    nb, seq, D = x.shape
    T = nb * seq
    assert ln_mix.shape[0] == 2 and ab_w_in.shape[0] == 1 and s5_glu_wa.shape[0] == 1
    xc, yc, cc = _mesh_pos()
    dev = 4 * xc + 2 * yc + cc
    pos = jnp.stack([cc, 2 * xc + yc]).astype(jnp.int32)
    bq = min(ATTN_TILE, seq)
    tabs =_rope_tables(seq) + (_branch_bias(seq // bq, bq),)

    def ffn_piece(k, li, fj):
        return w[k][li, fj].astype(BF)

    g0 = _gather_comm([ffn_piece("ffn_w1", 0, 0), ffn_piece("ffn_w3", 0, 0), ab_conv_w[0], s5_d])
    _run_comm(g0, "gather_first")
    w1, w3 = {(0, 0): g0.results[0]}, {(0, 0): g0.results[1]}
    w2 = {}
    conv_w = g0.results[2].transpose(1, 0, 2).reshape(3, -1)
    dsk = g0.results[3].reshape(1, D)
    gains = {k: [w[k][i:i + 1] for i in range(2)] for k in ("ln_ffn_pre", "ln_mix", "ln_ffn_post")}

    h = x.reshape(T, D)
    saved = {}

    def ffn_fwd(h, gain, key, tag, comm_up, comm_down, after_up):
        n = _rms_fwd(h, gain, BF)
        t1, t3, g = _ffn_up(n, w1[key], w3[key], comm=comm_up)
        after_up()
        saved[tag] = (h, n, t1, t3, g)
        return _ffn_down(g, w2[key], h, comm=comm_down)

    c_up = _gather_comm([ffn_piece("ffn_w2", 0, 0), ab_w_out[0].astype(BF)])
    c_dn = _gather_comm([ab_w_in[0].astype(BF)])
    h = ffn_fwd(h, gains["ln_ffn_pre"][0], (0, 0), "pre0", c_up, c_dn,
                lambda: w2.update({(0, 0): c_up.results[0]}))
    wout = c_up.results[1].reshape(-1, D)
    wing = c_dn.results[0]
    c_proj = _gather_comm([ffn_piece("ffn_w1", 0, 1)])
    c_attn = _gather_comm([ffn_piece("ffn_w3", 0, 1), s5_glu_wa[0].astype(BF)])
    c_out = _gather_comm([s5_glu_wb[0].astype(BF)])
    h, saved["mix0"] = _mixer_ab_fwd(h, gains["ln_mix"][0], wing, conv_w, wout, tabs, nb, seq, c_proj, c_attn, c_out)
    w1[(0, 1)] = c_proj.results[0]
    w3[(0, 1)] = c_attn.results[0]
    wa = c_attn.results[1].reshape(-1, D)
    wb = c_out.results[0].reshape(-1, D)
    c_up2 = _gather_comm([ffn_piece("ffn_w2", 0, 1), ffn_piece("ffn_w1", 1, 0)])
    c_dn = _gather_comm([ffn_piece("ffn_w3", 1, 0)])
    h = ffn_fwd(h, gains["ln_ffn_post"][0], (0, 1), "post0", c_up2, c_dn,
                lambda: w2.update({(0, 1): c_up2.results[0]}))
    w1[(1, 0)] = c_up2.results[1]
    w3[(1, 0)] = c_dn.results[0]
    c_up3 = _gather_comm([ffn_piece("ffn_w2", 1, 0), ffn_piece("ffn_w1", 1, 1)])
    c_dn = _gather_comm([ffn_piece("ffn_w3", 1, 1)])
    h = ffn_fwd(h, gains["ln_ffn_pre"][1], (1, 0), "pre1", c_up3, c_dn,
                lambda: w2.update({(1, 0): c_up3.results[0]}))
    w1[(1, 1)] = c_up3.results[1]
    w3[(1, 1)] = c_dn.results[0]
    c_s5 = _gather_comm([ffn_piece("ffn_w2", 1, 1)])
    h, saved["mix1"] = _mixer_s5_fwd(h, gains["ln_mix"][1], w, dsk, wa, wb, nb, seq, c_s5)
    w2[(1, 1)] = c_s5.results[0]
    h = ffn_fwd(h, gains["ln_ffn_post"][1], (1, 1), "post1", None, None, lambda: None)
    dh, dhb, d_ln_final, loss_part = _loss_head(h, ln_final.reshape(1, D), loss_target.reshape(T, D))
    loss = lax.psum(loss_part[0, 0], ("x", "y", "c"))

    reduced = {}

    def reduce_start(names, grads):
        recv = _pair_exchange(grads, "pair_exchange")
        comms = []
        for nm, g, r in zip(names, grads, recv):
            part = _pair_sum(g, r, pos)
            comms.append(_exchange_comm([part]))
            reduced[nm] = (part, comms[-1])
        return comms

    def ffn_bwd(dh, dhb, key, tag, gain, carry, is_last=False):
        h_in, n, t1, t3, g = saved[tag]
        da1, da3 = _ffn_bwd_hidden(dhb, w2[key], t1, t3, comm=carry)
        c2, = reduce_start([("ffn_w2",) + key], [_ffn_dw2(g, dhb)])
        dw1, dw3 = _ffn_dw13(n, da1, da3, comm=c2)
        c1, c3 = reduce_start([("ffn_w1",) + key, ("ffn_w3",) + key], [dw1, dw3])
        dn = _ffn_dn(da1, da3, w1[key], w3[key], comm=_merge_comms([c1, c3]) if is_last else c1)
        return list(_rms_bwd(dn, h_in, gain, dh)) + [None if is_last else c3]

    g_small = {"ln_final": d_ln_final.reshape(D)}
    g_ln = {k: [None, None] for k in gains}
    dh, dhb, g_ln["ln_ffn_post"][1], carry = ffn_bwd(dh, dhb, (1, 1), "post1", gains["ln_ffn_post"][1], None)
    dh, dhb, g_ln["ln_mix"][1], s5_small = _mixer_s5_bwd(
        dh, saved["mix1"], gains["ln_mix"][1], wa, wb, reduce_start, carry)
    g_small.update(s5_small)
    dh, dhb, g_ln["ln_ffn_pre"][1], carry = ffn_bwd(dh, dhb, (1, 0), "pre1", gains["ln_ffn_pre"][1], None)
    dh, dhb, g_ln["ln_ffn_post"][0], carry = ffn_bwd(dh, dhb, (0, 1), "post0", gains["ln_ffn_post"][0], carry)
    dh, dhb, g_ln["ln_mix"][0], g_small["ab_conv_w"], carry = _mixer_ab_bwd(
        dh, dhb, saved["mix0"], gains["ln_mix"][0], wing, conv_w, wout, tabs, nb, seq, reduce_start, carry)
    dh, dhb, g_ln["ln_ffn_pre"][0], _ = ffn_bwd(dh, dhb, (0, 0), "pre0", gains["ln_ffn_pre"][0], carry, is_last=True)
    grad_x = dh.reshape(nb, seq, D)
    for k in g_ln:
        g_small[k] = jnp.concatenate(g_ln[k], axis=0)

    out = {}
    for k in BIG:
        transposed = k in ("ffn_w1", "ffn_w3")
        pieces = [(li, fj) for li in range(2) for fj in range(2)] if w[k].ndim == 4 else [None]

        def view(a):
            a = a.swapaxes(-1, -2) if transposed else a
            return a.reshape(len(pieces), -1, a.shape[-1])

        w3d, m3d, v3d = view(w[k]), view(mom[k]), view(var[k])
        bufs = [lax.empty(w3d.shape, F32) for _ in range(4)]
        for q, key in enumerate(pieces):
            part, comm = reduced[k if key is None else (k,) + key]
            bufs = _adamw_piece(w3d, m3d, v3d, q, part, comm.results[0], bufs)
        if transposed:
            out[k] = [t.reshape(w[k].shape[:2] + w3d.shape[1:]).swapaxes(-1, -2) for t in bufs]
        else:
            out[k] = [t.reshape(w[k].shape) for t in bufs]

    small_names = [k for k in NAMES if k not in BIG]
    red = _unpack(_all_reduce_small(_pack([g_small[k] for k in small_names])),
                  [g_small[k].shape for k in small_names])
    g_red = dict(zip(small_names, red))
    cw = w["ab_conv_w"].shape[-1]
    g_red["ab_conv_w"] = lax.dynamic_slice_in_dim(g_red["ab_conv_w"], dev * cw, cw, axis=1)[None]
    dsz = w["s5_d"].shape[-1]
    g_red["s5_d"] = lax.dynamic_slice_in_dim(g_red["s5_d"].reshape(1, -1), dev * dsz, dsz, axis=1)
    shapes = [w[k].shape for k in small_names]
    g_red = {k: g_red[k].reshape(w[k].shape) for k in small_names}
    d_s, m_s, v_s = _adamw_small(_pack([w[k] for k in small_names]), _pack([g_red[k] for k in small_names]),
                                 _pack([mom[k] for k in small_names]), _pack([var[k] for k in small_names]))
    for k, d, nm, nv in zip(small_names, _unpack(d_s, shapes), _unpack(m_s, shapes), _unpack(v_s, shapes)):
        out[k] = [g_red[k], d, nm, nv]

    return (loss, grad_x, *[out[k][0] for k in NAMES], *[out[k][1] for k in NAMES],
            *[out[k][2] for k in NAMES], *[out[k][3] for k in NAMES])
```

```python
import jax
import jax.numpy as jnp
from jax import lax
from jax.experimental import pallas as pl
from jax.experimental.pallas import tpu as pltpu

F32, BF = jnp.float32, jnp.bfloat16
N_DEV = 8
MESH = pl.DeviceIdType.MESH
LANES = 128
SUBLANES = 8
VMEM_LIMIT = 56 * 2 ** 20
ROW_TILE = 512
FFN_ROW_TILE = 1024
COL_TILE = 512
ATTN_TILE = 512
SCAN_UNROLL = 4
ELEMS_PER_BLOCK = 256 * 1024
PAIR_SUM_ELEMS = 2048 * 1024
RMS_EPS = 1e-6
ROPE_THETA = 10000.0
NEG_INF = -1e30
S5_STATE = 64
S5_GROUP = 16
GROUPS_PER_BLOCK = LANES // S5_GROUP
STATE_COLS = GROUPS_PER_BLOCK * S5_STATE
DILATED_PATTERN = ((128, 1), (512, 4), (2048, 16))
ADAM_LR, ADAM_B1, ADAM_B2, ADAM_EPS, ADAM_WD, ADAM_STEP = 0.001, 0.9, 0.999, 1e-08, 0.01, 10
GELU_C = 0.7978845608028654
GELU_A = 0.044715


def _cparams(n_grid, vmem=VMEM_LIMIT):
    sem = ("arbitrary",) * n_grid if n_grid else None
    return pltpu.CompilerParams(dimension_semantics=sem, vmem_limit_bytes=vmem)


def _sig(x):
    return 1.0 / (1.0 + jnp.exp(-x))


def _gelu(x):
    return 0.5 * x * (1.0 + jnp.tanh(GELU_C * (x + GELU_A * x * x * x)))


def _gelu_grad(x):
    t = jnp.tanh(GELU_C * (x + GELU_A * x * x * x))
    return 0.5 * (1.0 + t) + 0.5 * x * (1.0 - t * t) * GELU_C * (1.0 + 3.0 * GELU_A * x * x)


def _dot(a, b, dims):
    a = a if a.dtype == BF else a.astype(BF)
    b = b if b.dtype == BF else b.astype(BF)
    return lax.dot_general(a, b, (dims, ((), ())), preferred_element_type=F32)


NN = ((1,), (0,))
NT = ((1,), (1,))
TN = ((0,), (0,))


def _row_block(rows, cols, elems=ELEMS_PER_BLOCK, mult=16):
    cap = max(mult, elems // cols)
    best = None
    for b in range(mult, min(rows, cap) + 1, mult):
        if rows % b == 0:
            best = b
    return rows if best is None else best


class _Comm:
    def __init__(self, ins, outs, sems, start, finish, members=()):
        self.ins, self.outs, self.sems, self.start, self.finish = ins, outs, sems, start, finish
        self.members = members
        self.results = None

    def set_results(self, res):
        self.results = list(res)
        off = 0
        for m in self.members:
            m.set_results(res[off:off + len(m.outs)])
            off += len(m.outs)


def _merge_comms(comms):
    comms = [c for c in comms if c is not None]
    if len(comms) < 2:
        return comms[0] if comms else None

    def each(fn_name, ins, outs, sems):
        i = o = s = 0
        for c in comms:
            ni, no, ns = len(c.ins), len(c.outs), len(c.sems)
            getattr(c, fn_name)(ins[i:i + ni], outs[o:o + no], sems[s:s + ns])
            i, o, s = i + ni, o + no, s + ns

    return _Comm([a for c in comms for a in c.ins], [a for c in comms for a in c.outs],
                 [a for c in comms for a in c.sems],
                 lambda ins, outs, sems: each("start", ins, outs, sems),
                 lambda ins, outs, sems: each("finish", ins, outs, sems), members=tuple(comms))


def _call(body, name, grid, in_specs, out_specs, out_shape, args, scratch=(), comm=None):
    in_specs, out_specs, out_shape, scratch = list(in_specs), list(out_specs), list(out_shape), list(scratch)
    if comm is None:
        return pl.pallas_call(body, grid=grid, in_specs=in_specs, out_specs=out_specs, out_shape=out_shape,
                              scratch_shapes=scratch, name=name, compiler_params=_cparams(len(grid)))(*args)
    n_in, n_out, n_sc = len(in_specs), len(out_specs), len(scratch)
    ci, co = len(comm.ins), len(comm.outs)

    def hosted(*refs):
        ins, refs = refs[:n_in], refs[n_in:]
        cins, refs = refs[:ci], refs[ci:]
        outs, refs = refs[:n_out], refs[n_out:]
        couts, refs = refs[:co], refs[co:]
        sc, csems = refs[:n_sc], refs[n_sc:]
        first = last = None
        for d, n in enumerate(grid):
            p = pl.program_id(d)
            first = (p == 0) if first is None else first & (p == 0)
            last = (p == n - 1) if last is None else last & (p == n - 1)

        @pl.when(first)
        def _():
            comm.start(cins, couts, csems)

        body(*ins, *outs, *sc)

        @pl.when(last)
        def _():
            comm.finish(cins, couts, csems)

    any_spec = pl.BlockSpec(memory_space=pl.ANY)
    res = pl.pallas_call(
        hosted, grid=grid, in_specs=in_specs + [any_spec] * ci, out_specs=out_specs + [any_spec] * co,
        out_shape=out_shape + list(comm.outs), scratch_shapes=scratch + list(comm.sems), name=name,
        compiler_params=_cparams(len(grid)))(*args, *comm.ins)
    comm.set_results(res[n_out:])
    return list(res[:n_out])


def _mm(name, grid, operands, pairs, n_acc, acc_shape, extras, outs, epilogue, comm=None, nrow=1, ncol=1):
    nk = grid[2]
    n_op, n_ex, n_out = len(operands), len(extras), len(outs)

    def part_of(ref, dim, t, n):
        if n == 1:
            return ref
        size = ref.shape[dim] // n
        idx = [slice(None)] * len(ref.shape)
        idx[dim] = pl.ds(t * size, size)
        return ref.at[tuple(idx)]

    def tile_of(ref, r, c):
        return part_of(part_of(ref, 0, r, nrow), 1, c, ncol)

    def products(op, r, c):
        parts = [None] * n_acc
        for ai, bi, dims, ci in pairs:
            a = part_of(op[ai], 1 - dims[0][0], r, nrow)
            b = part_of(op[bi], 1 - dims[1][0], c, ncol)
            d = _dot(a[...], b[...], dims)
            parts[ci] = d if parts[ci] is None else parts[ci] + d
        return parts

    def body(*refs):
        op = refs[:n_op]
        ex = refs[n_op:n_op + n_ex]
        out = refs[n_op + n_ex:n_op + n_ex + n_out]
        acc = refs[n_op + n_ex + n_out:]
        tiles = [(r, c) for r in range(nrow) for c in range(ncol)]

        def views(refs_, t):
            return [tile_of(q, *t) for q in refs_]

        if nk == 1:
            parts = products(op, *tiles[0])
            for q, t in enumerate(tiles):
                nxt = products(op, *tiles[q + 1]) if q + 1 < len(tiles) else None
                epilogue(parts, views(ex, t), views(out, t))
                parts = nxt
            return
        k = pl.program_id(2)

        @pl.when(k == 0)
        def _():
            for q in acc:
                q[...] = jnp.zeros_like(q)

        for t in tiles:
            parts = products(op, *t)
            for q, p in zip(views(acc, t), parts):
                q[...] += p

        @pl.when(k == nk - 1)
        def _():
            for t in tiles:
                epilogue([q[...] for q in views(acc, t)], views(ex, t), views(out, t))

    return _call(body, name, grid, [s for _, s in operands] + [s for _, s in extras], [s for _, s in outs],
                 [sh for sh, _ in outs], [a for a, _ in operands] + [a for a, _ in extras],
                 scratch=[pltpu.VMEM(acc_shape, F32) for _ in range(n_acc if nk > 1 else 0)], comm=comm)


def _to_seg(a, seg_len):
    T, D = a.shape
    return a.reshape(SUBLANES, seg_len, D).transpose(1, 0, 2).reshape(T, D)


def _to_tok(a, seg_len):
    T, D = a.shape
    return a.reshape(seg_len, SUBLANES, D).transpose(1, 0, 2).reshape(T, D)


def _rms_fwd(h, gain, out_dtype):
    T, D = h.shape
    bm = min(ROW_TILE, T)

    def body(h_ref, g_ref, o_ref):
        x = h_ref[...]
        r = lax.rsqrt(jnp.mean(x * x, axis=-1, keepdims=True) + RMS_EPS)
        o_ref[...] = (x * r * g_ref[...]).astype(out_dtype)

    row = pl.BlockSpec((bm, D), lambda i: (i, 0))
    return pl.pallas_call(
        body, grid=(T // bm,), in_specs=[row, pl.BlockSpec((1, D), lambda i: (0, 0))],
        out_specs=row, out_shape=jax.ShapeDtypeStruct((T, D), out_dtype), name="rms_fwd",
        compiler_params=_cparams(1))(h, gain)


def _rms_bwd_rows(dn, x, g):
    r = lax.rsqrt(jnp.mean(x * x, axis=-1, keepdims=True) + RMS_EPS)
    xh = x * r
    dng = dn * g
    dx = r * (dng - xh * jnp.mean(dng * xh, axis=-1, keepdims=True))
    return dx, jnp.sum(dn * xh, axis=0, keepdims=True)


def _rms_bwd(dn, h, gain, dh_up):
    T, D = h.shape
    bm = min(ROW_TILE, T)

    def body(dn_ref, h_ref, g_ref, up_ref, dh_ref, dhb_ref, dg_ref):
        dx, dg = _rms_bwd_rows(dn_ref[...], h_ref[...], g_ref[...])
        dh = up_ref[...] + dx
        dh_ref[...] = dh
        dhb_ref[...] = dh.astype(BF)

        @pl.when(pl.program_id(0) == 0)
        def _():
            dg_ref[...] = jnp.zeros_like(dg_ref)

        dg_ref[...] += dg

    row = pl.BlockSpec((bm, D), lambda i: (i, 0))
    vec = pl.BlockSpec((1, D), lambda i: (0, 0))
    return pl.pallas_call(
        body, grid=(T // bm,), in_specs=[row, row, vec, row], out_specs=[row, row, vec],
        out_shape=[jax.ShapeDtypeStruct((T, D), F32), jax.ShapeDtypeStruct((T, D), BF),
                   jax.ShapeDtypeStruct((1, D), F32)],
        name="rms_bwd", compiler_params=_cparams(1))(dn, h, gain, dh_up)


def _loss_head(h, gain, target):
    T, D = h.shape
    bm = min(ROW_TILE, T)

    def body(h_ref, g_ref, t_ref, dh_ref, dhb_ref, dg_ref, loss_ref):
        x = h_ref[...]
        g = g_ref[...]
        r = lax.rsqrt(jnp.mean(x * x, axis=-1, keepdims=True) + RMS_EPS)
        err = x * r * g - t_ref[...]
        part = 0.5 * jnp.sum(jnp.sum(err * err, axis=-1, keepdims=True), axis=0, keepdims=True) / D
        dx, dg = _rms_bwd_rows(err / D, x, g)
        dh_ref[...] = dx
        dhb_ref[...] = dx.astype(BF)

        @pl.when(pl.program_id(0) == 0)
        def _():
            dg_ref[...] = jnp.zeros_like(dg_ref)
            loss_ref[...] = jnp.zeros_like(loss_ref)

        dg_ref[...] += dg
        loss_ref[...] += jnp.broadcast_to(part, loss_ref.shape)

    row = pl.BlockSpec((bm, D), lambda i: (i, 0))
    vec = pl.BlockSpec((1, D), lambda i: (0, 0))
    return pl.pallas_call(
        body, grid=(T // bm,), in_specs=[row, vec, row],
        out_specs=[row, row, vec, pl.BlockSpec((SUBLANES, LANES), lambda i: (0, 0))],
        out_shape=[jax.ShapeDtypeStruct((T, D), F32), jax.ShapeDtypeStruct((T, D), BF),
                   jax.ShapeDtypeStruct((1, D), F32), jax.ShapeDtypeStruct((SUBLANES, LANES), F32)],
        name="loss_head", compiler_params=_cparams(1))(h, gain, target)


def _ffn_up(n, w1g, w3g, comm=None):
    T, D = n.shape
    fs = w1g.shape[-1]
    bm = min(FFN_ROW_TILE, T)
    wspec = pl.BlockSpec((None, D, fs), lambda s, i, k: (s, 0, 0))
    ospec = pl.BlockSpec((None, bm, fs), lambda s, i, k: (s, i, 0))

    def epi(accs, ex, outs):
        a1, a3 = accs
        sg = _sig(a1)
        silu = a1 * sg
        outs[0][...] = (a3 * sg * (1.0 + a1 * (1.0 - sg))).astype(BF)
        outs[1][...] = silu.astype(BF)
        outs[2][...] = (silu * a3).astype(BF)

    sh = jax.ShapeDtypeStruct((N_DEV, T, fs), BF)
    return _mm("ffn_up", (N_DEV, T // bm, 1),
               [(n, pl.BlockSpec((bm, D), lambda s, i, k: (i, 0))), (w1g, wspec), (w3g, wspec)],
               [(0, 1, NN, 0), (0, 2, NN, 1)], 2, None, [], [(sh, ospec)] * 3, epi, comm=comm,
               nrow=max(1, bm // ROW_TILE))


def _ffn_down(g, w2g, h, comm=None):
    _, T, fs = g.shape
    D = h.shape[1]
    bm = min(FFN_ROW_TILE, T)
    row = pl.BlockSpec((bm, D), lambda i, j, s: (i, 0))

    def epi(accs, ex, outs):
        outs[0][...] = ex[0][...] + 0.5 * accs[0]

    return _mm("ffn_down", (T // bm, 1, N_DEV),
               [(g, pl.BlockSpec((None, bm, fs), lambda i, j, s: (s, i, 0))),
                (w2g, pl.BlockSpec((None, fs, D), lambda i, j, s: (s, 0, 0)))],
               [(0, 1, NN, 0)], 1, (bm, D), [(h, row)],
               [(jax.ShapeDtypeStruct((T, D), F32), row)], epi, comm=comm,
               nrow=max(1, bm // ROW_TILE), ncol=max(1, D // COL_TILE))[0]


def _ffn_bwd_hidden(dhb, w2g, t1, t3, comm=None):
    T, D = dhb.shape
    fs = t1.shape[-1]
    bm = min(FFN_ROW_TILE, T)
    aspec = pl.BlockSpec((None, bm, fs), lambda s, i, k: (s, i, 0))

    def epi(accs, ex, outs):
        dg = 0.5 * accs[0]
        outs[0][...] = (dg * ex[0][...].astype(F32)).astype(BF)
        outs[1][...] = (dg * ex[1][...].astype(F32)).astype(BF)

    sh = jax.ShapeDtypeStruct((N_DEV, T, fs), BF)
    return _mm("ffn_bwd_hidden", (N_DEV, T // bm, 1),
               [(dhb, pl.BlockSpec((bm, D), lambda s, i, k: (i, 0))),
                (w2g, pl.BlockSpec((None, fs, D), lambda s, i, k: (s, 0, 0)))],
               [(0, 1, NT, 0)], 1, None, [(t1, aspec), (t3, aspec)], [(sh, aspec)] * 2, epi, comm=comm,
               nrow=max(1, bm // ROW_TILE))


def _ffn_dw2(g, dhb):
    _, T, fs = g.shape
    D = dhb.shape[1]
    bn = min(COL_TILE, D)

    def epi(accs, ex, outs):
        outs[0][...] = (0.5 * accs[0]).astype(BF)

    return _mm("ffn_dw2", (N_DEV, D // bn, 1),
               [(g, pl.BlockSpec((None, T, fs), lambda s, j, k: (s, 0, 0))),
                (dhb, pl.BlockSpec((T, bn), lambda s, j, k: (0, j)))],
               [(0, 1, TN, 0)], 1, None, [],
               [(jax.ShapeDtypeStruct((N_DEV, fs, D), BF), pl.BlockSpec((None, fs, bn), lambda s, j, k: (s, 0, j)))],
               epi)[0]


def _ffn_dw13(n, da1, da3, comm=None):
    T, D = n.shape
    fs = da1.shape[-1]
    bn = min(COL_TILE, D)
    dspec = pl.BlockSpec((None, T, fs), lambda s, j, k: (s, 0, 0))
    ospec = pl.BlockSpec((None, fs, bn), lambda s, j, k: (s, 0, j))

    def epi(accs, ex, outs):
        outs[0][...] = accs[0].astype(BF)
        outs[1][...] = accs[1].astype(BF)

    sh = jax.ShapeDtypeStruct((N_DEV, fs, D), BF)
    return _mm("ffn_dw13", (N_DEV, D // bn, 1),
               [(da1, dspec), (da3, dspec), (n, pl.BlockSpec((T, bn), lambda s, j, k: (0, j)))],
               [(0, 2, TN, 0), (1, 2, TN, 1)], 2, None, [], [(sh, ospec)] * 2, epi, comm=comm)


def _ffn_dn(da1, da3, w1g, w3g, comm=None):
    _, T, fs = da1.shape
    D = w1g.shape[-2]
    bm = min(FFN_ROW_TILE, T)
    dspec = pl.BlockSpec((None, bm, fs), lambda i, j, s: (s, i, 0))
    wspec = pl.BlockSpec((None, D, fs), lambda i, j, s: (s, 0, 0))
    row = pl.BlockSpec((bm, D), lambda i, j, s: (i, 0))

    def epi(accs, ex, outs):
        outs[0][...] = accs[0]

    return _mm("ffn_dn", (T // bm, 1, N_DEV),
               [(da1, dspec), (w1g, wspec), (da3, dspec), (w3g, wspec)],
               [(0, 1, NT, 0), (2, 3, NT, 0)], 1, (bm, D), [],
               [(jax.ShapeDtypeStruct((T, D), F32), row)], epi, comm=comm,
               nrow=max(1, bm // ROW_TILE), ncol=max(1, D // COL_TILE))[0]


def _rope_tables(seq):
    half = LANES // 2
    inv = ROPE_THETA ** (-jnp.arange(0, half, dtype=F32) * 2.0 / LANES)
    ang = jnp.arange(seq, dtype=F32)[:, None] * inv[None, :]
    cos, sin = jnp.cos(ang), jnp.sin(ang)
    return jnp.concatenate([cos, cos], axis=1), jnp.concatenate([-sin, sin], axis=1)


def _branch_bias(nq, bq):
    d = (jnp.arange(nq)[:, None, None] * bq + jnp.arange(bq)[None, :, None]
         - jnp.arange(bq)[None, None, :])
    mult = jnp.zeros(d.shape, F32)
    for window, dil in DILATED_PATTERN:
        mult = mult + ((d >= 0) & (d % dil == 0) & (d <= window)).astype(F32)
    return jnp.where(mult > 0, jnp.log(jnp.maximum(mult, 1.0)), NEG_INF)


def _proj_fwd(u, wing, comm=None):
    T, D = u.shape
    ws = wing.shape[-1]
    bm = min(ROW_TILE, T)

    def epi(accs, ex, outs):
        outs[0][...] = accs[0]

    return _mm("proj_fwd", (N_DEV, T // bm, 1),
               [(u, pl.BlockSpec((bm, D), lambda s, i, k: (i, 0))),
                (wing, pl.BlockSpec((None, D, ws), lambda s, i, k: (s, 0, 0)))],
               [(0, 1, NN, 0)], 1, None, [],
               [(jax.ShapeDtypeStruct((T, N_DEV * ws), F32),
                 pl.BlockSpec((bm, ws), lambda s, i, k: (i, s)))], epi, comm=comm)[0]


def _rope_fwd(proj, cosf, sinf, seq, nh):
    T = proj.shape[0]
    bs = min(ROW_TILE, seq)
    nst = seq // bs
    scale = LANES ** -0.5

    def body(x_ref, c_ref, s_ref, o_ref):
        j = pl.program_id(1)
        t = x_ref[...]
        rot = t * c_ref[...] + pltpu.roll(t, LANES // 2, 1) * s_ref[...]
        rot = rot * jnp.where(j < nh, scale, 1.0)
        o_ref[...] = jnp.where(j < 2 * nh, rot, t).astype(BF)

    blk = pl.BlockSpec((bs, LANES), lambda r, j: (r, j))
    tab = pl.BlockSpec((bs, LANES), lambda r, j: (r % nst, 0))
    return pl.pallas_call(
        body, grid=(T // bs, 3 * nh), in_specs=[blk, tab, tab], out_specs=blk,
        out_shape=jax.ShapeDtypeStruct((T, 3 * nh * LANES), BF), name="rope_fwd",
        compiler_params=_cparams(2))(proj, cosf, sinf)


def _attn_fwd(qkv, bias, nb, seq, nh, comm=None):
    T = nb * seq
    bq = bias.shape[1]
    nq = seq // bq

    def body(q_ref, k_ref, v_ref, b_ref, o_ref, lse_ref):
        qi = pl.program_id(2)
        q = q_ref[...]

        def step(kj, carry):
            m, l, acc = carry
            rows = pl.ds(pl.multiple_of(kj * bq, bq), bq)
            s = _dot(q, k_ref[rows, :], NT) + b_ref[qi - kj]
            m_new = jnp.maximum(m, jnp.max(s, axis=1, keepdims=True))
            p = jnp.exp(s - m_new)
            alpha = jnp.exp(m - m_new)
            l = alpha * l + jnp.sum(p, axis=1, keepdims=True)
            acc = alpha * acc + _dot(p, v_ref[rows, :], NN)
            return m_new, l, acc

        init = (jnp.full((bq, 1), NEG_INF, F32), jnp.zeros((bq, 1), F32), jnp.zeros((bq, LANES), F32))
        m, l, acc = lax.fori_loop(0, qi + 1, step, init)
        o_ref[...] = (acc / l).astype(BF)
        lse_ref[...] = m + jnp.log(l)

    return _call(
        body, "attn_fwd", (nb, nh, nq),
        [pl.BlockSpec((bq, LANES), lambda b, h, i: (b * nq + i, h)),
         pl.BlockSpec((seq, LANES), lambda b, h, i: (b, nh + h)),
         pl.BlockSpec((seq, LANES), lambda b, h, i: (b, 2 * nh + h)),
         pl.BlockSpec((nq, bq, bq), lambda b, h, i: (0, 0, 0))],
        [pl.BlockSpec((bq, LANES), lambda b, h, i: (b * nq + i, h)),
         pl.BlockSpec((None, bq, 1), lambda b, h, i: (h, b * nq + i, 0))],
        [jax.ShapeDtypeStruct((T, 2 * nh * LANES), BF), jax.ShapeDtypeStruct((nh, T, 1), F32)],
        (qkv, qkv, qkv, bias), comm=comm)


def _attn_bwd_dq(qkv, cat, dcat, lse, bias, nb, seq, nh, comm=None):
    T = nb * seq
    bq = bias.shape[1]
    nq = seq // bq

    def body(q_ref, k_ref, v_ref, o_ref, do_ref, lse_ref, b_ref, dq_ref, delta_ref):
        qi = pl.program_id(2)
        q = q_ref[...]
        do = do_ref[...]
        dob = do.astype(BF)
        lse_t = lse_ref[...]
        delta = jnp.sum(do * o_ref[...].astype(F32), axis=1, keepdims=True)
        delta_ref[...] = delta

        def step(kj, dq):
            rows = pl.ds(pl.multiple_of(kj * bq, bq), bq)
            k = k_ref[rows, :]
            p = jnp.exp(_dot(q, k, NT) + b_ref[qi - kj] - lse_t)
            ds = p * (_dot(dob, v_ref[rows, :], NT) - delta)
            return dq + _dot(ds, k, NN)

        dq_ref[...] = lax.fori_loop(0, qi + 1, step, jnp.zeros((bq, LANES), F32))

    tile = pl.BlockSpec((bq, LANES), lambda b, h, i: (b * nq + i, h))
    stat = pl.BlockSpec((None, bq, 1), lambda b, h, i: (h, b * nq + i, 0))
    return _call(
        body, "attn_bwd_dq", (nb, nh, nq),
        [tile, pl.BlockSpec((seq, LANES), lambda b, h, i: (b, nh + h)),
         pl.BlockSpec((seq, LANES), lambda b, h, i: (b, 2 * nh + h)), tile, tile, stat,
         pl.BlockSpec((nq, bq, bq), lambda b, h, i: (0, 0, 0))],
        [tile, stat],
        [jax.ShapeDtypeStruct((T, nh * LANES), F32), jax.ShapeDtypeStruct((nh, T, 1), F32)],
        (qkv, qkv, qkv, cat, dcat, lse, bias), comm=comm)


def _attn_bwd_dkv(qkv, dcat, lse, delta, bias, nb, seq, nh):
    T = nb * seq
    bq = bias.shape[1]
    nq = seq // bq

    def body(k_ref, v_ref, q_ref, do_ref, lse_ref, delta_ref, b_ref, dk_ref, dv_ref):
        kj = pl.program_id(2)
        k = k_ref[...]
        v = v_ref[...]

        def step(qi, carry):
            dk, dv = carry
            rows = pl.ds(pl.multiple_of(qi * bq, bq), bq)
            q = q_ref[rows, :]
            dob = do_ref[rows, :].astype(BF)
            p = jnp.exp(_dot(q, k, NT) + b_ref[qi - kj] - lse_ref[rows, :])
            dv = dv + _dot(p, dob, TN)
            ds = p * (_dot(dob, v, NT) - delta_ref[rows, :])
            return dk + _dot(ds, q, TN), dv

        z = jnp.zeros((bq, LANES), F32)
        dk, dv = lax.fori_loop(kj, nq, step, (z, z))
        dk_ref[...] = dk
        dv_ref[...] = dv

    stat = pl.BlockSpec((None, seq, 1), lambda b, h, i: (h, b, 0))
    out = pl.BlockSpec((bq, LANES), lambda b, h, i: (b * nq + i, h))
    sh = jax.ShapeDtypeStruct((T, nh * LANES), F32)
    return pl.pallas_call(
        body, grid=(nb, nh, nq),
        in_specs=[pl.BlockSpec((bq, LANES), lambda b, h, i: (b * nq + i, nh + h)),
                  pl.BlockSpec((bq, LANES), lambda b, h, i: (b * nq + i, 2 * nh + h)),
                  pl.BlockSpec((seq, LANES), lambda b, h, i: (b, h)),
                  pl.BlockSpec((seq, LANES), lambda b, h, i: (b, h)),
                  stat, stat,
                  pl.BlockSpec((nq, bq, bq), lambda b, h, i: (0, 0, 0))],
        out_specs=[out, out], out_shape=[sh, sh],
        name="attn_bwd_dkv", compiler_params=_cparams(3))(qkv, qkv, qkv, dcat, lse, delta, bias)


def _conv_parts(gc, xin, w_ref):
    w = [w_ref[k:k + 1, :] for k in range(3)]
    u = gc * xin
    row = lax.broadcasted_iota(jnp.int32, u.shape, 0)
    u1 = jnp.where(row >= 1, pltpu.roll(u, 1, 0), 0.0)
    u2 = jnp.where(row >= 2, pltpu.roll(u, 2, 0), 0.0)
    return u, u1, u2, w[0] * u2 + w[1] * u1 + w[2] * u, w, row


def _conv_fwd(proj, conv_w, cat, nb, seq, width):
    cw = min(2 * LANES, width)
    nc = width // cw

    def body(gb_ref, gc_ref, x_ref, w_ref, cat_ref, o_ref):
        _, _, _, conv, _, _ = _conv_parts(gc_ref[...], x_ref[...], w_ref)
        o_ref[...] = (gb_ref[...] * conv).astype(BF)

    def sec(k):
        return pl.BlockSpec((seq, cw), lambda b, c: (b, k * nc + c))

    return pl.pallas_call(
        body, grid=(nb, nc),
        in_specs=[sec(3), sec(4), sec(5), pl.BlockSpec((3, cw), lambda b, c: (0, c)),
                  pl.BlockSpec(memory_space=pl.ANY)],
        out_specs=pl.BlockSpec((seq, cw), lambda b, c: (b, nc + c)),
        out_shape=jax.ShapeDtypeStruct(cat.shape, BF), input_output_aliases={4: 0},
        name="conv_fwd", compiler_params=_cparams(2))(proj, proj, proj, conv_w, cat)


def _conv_bwd(proj, conv_w, dcat, nb, seq, width):
    cw = min(2 * LANES, width)
    nc = width // cw
    T = nb * seq

    def body(gb_ref, gc_ref, x_ref, w_ref, d_ref, dgb_ref, dgc_ref, dx_ref, dw_ref):
        gc = gc_ref[...]
        xin = x_ref[...]
        u, u1, u2, conv, w, row = _conv_parts(gc, xin, w_ref)
        dsc = d_ref[...]
        dgb_ref[...] = dsc * conv
        dconv = dsc * gb_ref[...]
        d1 = jnp.where(row < seq - 1, pltpu.roll(dconv, seq - 1, 0), 0.0)
        d2 = jnp.where(row < seq - 2, pltpu.roll(dconv, seq - 2, 0), 0.0)
        du = w[2] * dconv + w[1] * d1 + w[0] * d2
        dgc_ref[...] = du * xin
        dx_ref[...] = du * gc

        @pl.when(pl.program_id(1) == 0)
        def _():
            dw_ref[...] = jnp.zeros_like(dw_ref)

        dw_ref[0:1, :] += jnp.sum(dconv * u2, axis=0, keepdims=True)
        dw_ref[1:2, :] += jnp.sum(dconv * u1, axis=0, keepdims=True)
        dw_ref[2:3, :] += jnp.sum(dconv * u, axis=0, keepdims=True)

    def sec(k):
        return pl.BlockSpec((seq, cw), lambda c, b: (b, k * nc + c))

    out = pl.BlockSpec((seq, cw), lambda c, b: (b, c))
    wsp = pl.BlockSpec((3, cw), lambda c, b: (0, c))
    sh = jax.ShapeDtypeStruct((T, width), F32)
    return pl.pallas_call(
        body, grid=(nc, nb), in_specs=[sec(3), sec(4), sec(5), wsp, sec(1)],
        out_specs=[out, out, out, wsp], out_shape=[sh, sh, sh, jax.ShapeDtypeStruct((3, width), F32)],
        name="conv_bwd", compiler_params=_cparams(2))(proj, proj, proj, conv_w, dcat)


def _assemble_dproj(dq, dk, dv, dgb, dgc, dxin, cosf, sinf, seq):
    T, width = dq.shape
    nh = width // LANES
    bs = min(256, seq)
    nst = seq // bs
    scale = LANES ** -0.5

    def body(dq_ref, dk_ref, dv_ref, dgb_ref, dgc_ref, dx_ref, c_ref, s_ref, o_ref):
        sec = pl.program_id(1)
        c = c_ref[...]
        s = s_ref[...]

        def unrope(ref, mul):
            for h in range(nh):
                cols = slice(h * LANES, (h + 1) * LANES)
                t = ref[:, cols]
                o_ref[:, cols] = ((t * c + pltpu.roll(t * s, LANES // 2, 1)) * mul).astype(BF)

        @pl.when(sec == 0)
        def _():
            unrope(dq_ref, scale)

        @pl.when(sec == 1)
        def _():
            unrope(dk_ref, 1.0)

        for k, ref in ((2, dv_ref), (3, dgb_ref), (4, dgc_ref), (5, dx_ref)):
            @pl.when(sec == k)
            def _(ref=ref):
                o_ref[...] = ref[...].astype(BF)

    blk = pl.BlockSpec((bs, width), lambda r, k: (r, 0))
    tab = pl.BlockSpec((bs, LANES), lambda r, k: (r % nst, 0))
    return pl.pallas_call(
        body, grid=(T // bs, 6), in_specs=[blk] * 6 + [tab, tab],
        out_specs=pl.BlockSpec((bs, width), lambda r, k: (r, k)),
        out_shape=jax.ShapeDtypeStruct((T, 6 * width), BF), name="assemble_dproj",
        compiler_params=_cparams(2))(dq, dk, dv, dgb, dgc, dxin, cosf, sinf)


def _res_mm(name, a, w, h, comm=None):
    T, K = a.shape
    N = w.shape[1]
    bm = min(ROW_TILE, T)
    bk = min(ROW_TILE, K)
    row = pl.BlockSpec((bm, N), lambda i, j, k: (i, 0))

    def epi(accs, ex, outs):
        outs[0][...] = ex[0][...] + accs[0]

    return _mm(name, (T // bm, 1, K // bk),
               [(a, pl.BlockSpec((bm, bk), lambda i, j, k: (i, k))),
                (w, pl.BlockSpec((bk, N), lambda i, j, k: (k, 0)))],
               [(0, 1, NN, 0)], 1, (bm, N), [(h, row)],
               [(jax.ShapeDtypeStruct((T, N), F32), row)], epi, comm=comm, ncol=max(1, N // COL_TILE))[0]


def _mm_nt(name, a, w, out_dtype):
    T, K = a.shape
    N = w.shape[0]
    bm = min(ROW_TILE, T)
    bn = min(ROW_TILE, N)

    def epi(accs, ex, outs):
        outs[0][...] = accs[0].astype(out_dtype)

    return _mm(name, (T // bm, N // bn, 1),
               [(a, pl.BlockSpec((bm, K), lambda i, j, k: (i, 0))),
                (w, pl.BlockSpec((bn, K), lambda i, j, k: (j, 0)))],
               [(0, 1, NT, 0)], 1, None, [],
               [(jax.ShapeDtypeStruct((T, N), out_dtype), pl.BlockSpec((bm, bn), lambda i, j, k: (i, j)))],
               epi)[0]


def _mm_tn(name, a, bs_list):
    T, M = a.shape
    N = bs_list[0].shape[1]
    bmr = min(COL_TILE, M)
    bn = min(COL_TILE, N)
    n = len(bs_list)

    def epi(accs, ex, outs):
        for q in range(n):
            outs[q][...] = accs[q].astype(BF)

    ops = [(a, pl.BlockSpec((T, bmr), lambda r, j, k: (0, r)))]
    ops += [(b, pl.BlockSpec((T, bn), lambda r, j, k: (0, j))) for b in bs_list]
    return _mm(name, (M // bmr, N // bn, 1), ops, [(0, 1 + q, TN, q) for q in range(n)], n, None, [],
               [(jax.ShapeDtypeStruct((M, N), BF), pl.BlockSpec((bmr, bn), lambda r, j, k: (r, j)))] * n, epi)


def _proj_bwd_x(dproj, wing):
    T = dproj.shape[0]
    _, D, ws = wing.shape
    bm = min(ROW_TILE, T)
    row = pl.BlockSpec((bm, D), lambda i, j, s: (i, 0))

    def epi(accs, ex, outs):
        outs[0][...] = accs[0]

    return _mm("proj_bwd_x", (T // bm, 1, N_DEV),
               [(dproj, pl.BlockSpec((bm, ws), lambda i, j, s: (i, s))),
                (wing, pl.BlockSpec((None, D, ws), lambda i, j, s: (s, 0, 0)))],
               [(0, 1, NT, 0)], 1, (bm, D), [], [(jax.ShapeDtypeStruct((T, D), F32), row)], epi,
               ncol=max(1, D // COL_TILE))[0]


def _proj_dw(u, dproj, ws):
    T, D = u.shape
    bmr = min(COL_TILE, D)

    def epi(accs, ex, outs):
        outs[0][...] = accs[0].astype(BF)

    return _mm("proj_dw", (N_DEV, D // bmr, 1),
               [(u, pl.BlockSpec((T, bmr), lambda s, r, k: (0, r))),
                (dproj, pl.BlockSpec((T, ws), lambda s, r, k: (0, s)))],
               [(0, 1, TN, 0)], 1, None, [],
               [(jax.ShapeDtypeStruct((N_DEV, D, ws), BF),
                 pl.BlockSpec((None, bmr, ws), lambda s, r, k: (s, r, 0)))], epi)[0]


def _mixer_ab_fwd(h, gain, wing, conv_w, wout, tabs, nb, seq, comm_proj=None, comm_attn=None, comm_out=None):
    cosf, sinf, bias = tabs
    width = wing.shape[-1] * N_DEV // 6
    nh = width // LANES
    u = _rms_fwd(h, gain, BF)
    proj = _proj_fwd(u, wing, comm=comm_proj)
    qkv = _rope_fwd(proj, cosf, sinf, seq, nh)
    cat, lse = _attn_fwd(qkv, bias, nb, seq, nh, comm=comm_attn)
    cat = _conv_fwd(proj, conv_w, cat, nb, seq, width)
    return _res_mm("outproj_fwd", cat, wout, h, comm=comm_out), (h, u, proj, qkv, cat, lse)


def _mixer_ab_bwd(dh, dhb, saved, gain, wing, conv_w, wout, tabs, nb, seq, reduce_start, carry):
    cosf, sinf, bias = tabs
    h, u, proj, qkv, cat, lse = saved
    D = h.shape[1]
    ws = wing.shape[-1]
    width = ws * N_DEV // 6
    nh = width // LANES
    dcat = _mm_nt("outproj_bwd_x", dhb, wout, F32)
    dwout = _mm_tn("outproj_dw", cat, [dhb])[0]
    comm = _merge_comms(reduce_start(["ab_w_out"], [dwout.reshape(N_DEV, -1, D)]) + [carry])
    dq, delta = _attn_bwd_dq(qkv, cat, dcat, lse, bias, nb, seq, nh, comm=comm)
    dk, dv = _attn_bwd_dkv(qkv, dcat, lse, delta, bias, nb, seq, nh)
    dgb, dgc, dxin, dconvw = _conv_bwd(proj, conv_w, dcat, nb, seq, width)
    dproj = _assemble_dproj(dq, dk, dv, dgb, dgc, dxin, cosf, sinf, seq)
    du = _proj_bwd_x(dproj, wing)
    comm, = reduce_start(["ab_w_in"], [_proj_dw(u, dproj, ws)])
    dh_in, dhb_in, dgain = _rms_bwd(du, h, gain, dh)
    return dh_in, dhb_in, dgain, dconvw, comm


def _s5_zoh(lr, li, log_dt):
    dt = jnp.exp(log_dt)
    mag = jnp.exp(lr * dt)
    ar = mag * jnp.cos(li * dt)
    ai = mag * jnp.sin(li * dt)
    den = lr * lr + li * li
    return dt, ar, ai, den, ((ar - 1.0) * lr + ai * li) / den, (ai * lr - (ar - 1.0) * li) / den


def _s5_discretize(lam_re, lam_im, log_dt, bt_re, bt_im):
    def body(lr_ref, li_ref, ld_ref, br_ref, bi_ref, ar_ref, ai_ref, bbr_ref, bbi_ref):
        _, ar, ai, _, fr, fi = _s5_zoh(lr_ref[...], li_ref[...], ld_ref[...])
        ar_ref[...] = ar
        ai_ref[...] = ai
        bbr_ref[...] = fr * br_ref[...] - fi * bi_ref[...]
        bbi_ref[...] = fr * bi_ref[...] + fi * br_ref[...]

    small = jax.ShapeDtypeStruct(lam_re.shape, F32)
    big = jax.ShapeDtypeStruct(bt_re.shape, F32)
    return pl.pallas_call(body, out_shape=[small, small, big, big], name="s5_discretize",
                          compiler_params=_cparams(0))(lam_re, lam_im, log_dt, bt_re, bt_im)


def _s5_discretize_bwd(lam_re, lam_im, log_dt, bt_re, bt_im, d_ar, d_ai, d_bbr, d_bbi):

    def body(lr_ref, li_ref, ld_ref, br_ref, bi_ref, dar_ref, dai_ref, dbbr_ref, dbbi_ref,
             dlr_ref, dli_ref, dld_ref, dbr_ref, dbi_ref):
        lr, li = lr_ref[...], li_ref[...]
        dt, ar, ai, den, fr, fi = _s5_zoh(lr, li, ld_ref[...])
        br, bi = br_ref[...], bi_ref[...]
        dbbr, dbbi = dbbr_ref[...], dbbi_ref[...]
        dbr_ref[...] = dbbr * fr + dbbi * fi
        dbi_ref[...] = dbbi * fr - dbbr * fi
        dfr = jnp.sum(dbbr * br + dbbi * bi, axis=1, keepdims=True)
        dfi = jnp.sum(dbbi * br - dbbr * bi, axis=1, keepdims=True)
        dnr = dfr / den
        dni = dfi / den
        dden = -(dfr * fr + dfi * fi) / den
        dar = dar_ref[...] + dnr * lr - dni * li
        dai = dai_ref[...] + dnr * li + dni * lr
        dlr_ref[...] = dnr * (ar - 1.0) + dni * ai + 2.0 * dden * lr + dt * (dar * ar + dai * ai)
        dli_ref[...] = dnr * ai - dni * (ar - 1.0) + 2.0 * dden * li + dt * (dai * ar - dar * ai)
        ddt = jnp.sum(dar * (lr * ar - li * ai) + dai * (lr * ai + li * ar), axis=2, keepdims=True)
        dld_ref[...] = ddt * dt

    small = jax.ShapeDtypeStruct(lam_re.shape, F32)
    big = jax.ShapeDtypeStruct(bt_re.shape, F32)
    return pl.pallas_call(
        body, out_shape=[small, small, jax.ShapeDtypeStruct(log_dt.shape, F32), big, big],
        name="s5_discretize_bwd", compiler_params=_cparams(0))(
            lam_re, lam_im, log_dt, bt_re, bt_im, d_ar, d_ai, d_bbr, d_bbi)


def _rows8(t):
    return pl.ds(pl.multiple_of(t * SUBLANES, SUBLANES), SUBLANES)


def _cmul_add(ar, ai, sr, si, br, bi):
    return ar * sr - ai * si + br, ar * si + ai * sr + bi


def _scan(a, read, write, init, n):
    def step(t, c):
        s = _cmul_add(*a, *c, *read(t))
        if write is not None:
            write(t, s)
        return s

    return lax.fori_loop(0, n, step, init, unroll=SCAN_UNROLL)


def _cpow(ar, ai, n):
    rr = ri = None
    while n:
        if n & 1:
            rr, ri = (ar, ai) if rr is None else (rr * ar - ri * ai, rr * ai + ri * ar)
        ar, ai = ar * ar - ai * ai, 2.0 * ar * ai
        n >>= 1
    return rr, ri


def _s5_specs(R, nj):
    sh = STATE_COLS
    return dict(
        rows=pl.BlockSpec((R, LANES), lambda j: (0, j)),
        bd=pl.BlockSpec((None, LANES, sh), lambda j: (j, 0, 0)),
        cd=pl.BlockSpec((None, sh, LANES), lambda j: (j, 0, 0)),
        a=pl.BlockSpec((None, 1, sh), lambda j: (j, 0, 0)),
        vec=pl.BlockSpec((1, LANES), lambda j: (0, j)),
        init=pl.BlockSpec((None, SUBLANES, sh), lambda j: (j, 0, 0)))


def _s5_fwd(u, mats, seg_len, nseg, comm=None):
    bdr, bdi, cdr, cdi, are, aim, dsk = mats
    R, D = u.shape
    nj = D // LANES
    sh = STATE_COLS
    rc = min(R, 512)
    sp = _s5_specs(R, nj)

    def body(u_ref, bdr_ref, bdi_ref, cdr_ref, cdi_ref, ar_ref, ai_ref, d_ref,
             y_ref, yg_ref, ir_ref, ii_ref, sre, sim):
        ar = jnp.broadcast_to(ar_ref[...], (SUBLANES, sh))
        ai = jnp.broadcast_to(ai_ref[...], (SUBLANES, sh))

        def bu_chunk(c, _):
            rows = pl.ds(pl.multiple_of(c * rc, rc), rc)
            ub = u_ref[rows, :].astype(BF)
            sre[rows, :] = _dot(ub, bdr_ref[...], NN)
            sim[rows, :] = _dot(ub, bdi_ref[...], NN)
            return 0

        lax.fori_loop(0, R // rc, bu_chunk, 0)
        z = jnp.zeros((SUBLANES, sh), F32)

        def read(t):
            return sre[_rows8(t), :], sim[_rows8(t), :]

        def write(t, s):
            sre[_rows8(t), :] = s[0]
            sim[_rows8(t), :] = s[1]

        er, ei = _scan((ar, ai), read, None, (z, z), seg_len)
        pr, pi = _cpow(ar, ai, seg_len)
        first = (lax.broadcasted_iota(jnp.int32, (SUBLANES, sh), 0) & (nseg - 1)) == 0

        def prev(x):
            return jnp.where(first, 0.0, pltpu.roll(x, 1, 0))

        xr, xi = er, ei
        for _ in range(nseg - 1):
            xr, xi = _cmul_add(pr, pi, prev(xr), prev(xi), er, ei)
        i_r, i_i = prev(xr), prev(xi)
        ir_ref[...] = i_r
        ii_ref[...] = i_i
        _scan((ar, ai), read, write, (i_r, i_i), seg_len)

        def y_chunk(c, _):
            rows = pl.ds(pl.multiple_of(c * rc, rc), rc)
            y = _dot(sre[rows, :], cdr_ref[...], NN) + _dot(sim[rows, :], cdi_ref[...], NN)
            y = y + d_ref[...] * u_ref[rows, :]
            y_ref[rows, :] = y
            yg_ref[rows, :] = _gelu(y).astype(BF)
            return 0

        lax.fori_loop(0, R // rc, y_chunk, 0)

    init_sh = jax.ShapeDtypeStruct((nj, SUBLANES, STATE_COLS), F32)
    return _call(
        body, "s5_fwd", (nj,),
        [sp["rows"], sp["bd"], sp["bd"], sp["cd"], sp["cd"], sp["a"], sp["a"], sp["vec"]],
        [sp["rows"], sp["rows"], sp["init"], sp["init"]],
        [jax.ShapeDtypeStruct((R, D), F32), jax.ShapeDtypeStruct((R, D), BF), init_sh, init_sh],
        (u, bdr, bdi, cdr, cdi, are, aim, dsk),
        scratch=[pltpu.VMEM((R, sh), F32) for _ in range(2)], comm=comm)


def _s5_bwd(u, dy, mats, init_re, init_im, seg_len, nseg, comm=None):
    bdr, bdi, cdr, cdi, are, aim, dsk = mats
    R, D = u.shape
    nj = D // LANES
    sh = STATE_COLS
    rc = min(R, 512)
    sp = _s5_specs(R, nj)

    def body(u_ref, dy_ref, bdr_ref, bdi_ref, cdr_ref, cdi_ref, ar_ref, ai_ref, d_ref, ir_ref, ii_ref,
             du_ref, dbdr_ref, dbdi_ref, dcdr_ref, dcdi_ref, dar_ref, dai_ref, dd_ref,
             sre, sim, gre, gim):
        ar = jnp.broadcast_to(ar_ref[...], (SUBLANES, sh))
        ai = jnp.broadcast_to(ai_ref[...], (SUBLANES, sh))
        i_r, i_i = ir_ref[...], ii_ref[...]

        def chunk(c):
            return pl.ds(pl.multiple_of(c * rc, rc), rc)

        def bu_chunk(c, _):
            ub = u_ref[chunk(c), :].astype(BF)
            dyb = dy_ref[chunk(c), :].astype(BF)
            sre[chunk(c), :] = _dot(ub, bdr_ref[...], NN)
            sim[chunk(c), :] = _dot(ub, bdi_ref[...], NN)
            gre[chunk(c), :] = _dot(dyb, cdr_ref[...], NT)
            gim[chunk(c), :] = _dot(dyb, cdi_ref[...], NT)
            return 0

        lax.fori_loop(0, R // rc, bu_chunk, 0)

        def read_s(t):
            return sre[_rows8(t), :], sim[_rows8(t), :]

        def read_g(t):
            return gre[_rows8(t), :], gim[_rows8(t), :]

        def both(i, c):
            s = _cmul_add(ar, ai, c[0], c[1], *read_s(i))
            sre[_rows8(i), :], sim[_rows8(i), :] = s
            return (*s, *_cmul_add(ar, -ai, c[2], c[3], *read_g(seg_len - 1 - i)))

        z = jnp.zeros((SUBLANES, sh), F32)
        _, _, fr, fi = lax.fori_loop(0, seg_len, both, (i_r, i_i, z, z), unroll=SCAN_UNROLL)

        def c_chunk(c, carry):
            dyb = dy_ref[chunk(c), :].astype(BF)
            return (carry[0] + _dot(sre[chunk(c), :], dyb, TN), carry[1] + _dot(sim[chunk(c), :], dyb, TN))

        zc = jnp.zeros((sh, LANES), F32)
        dcr, dci = lax.fori_loop(0, R // rc, c_chunk, (zc, zc))
        dcdr_ref[...] = dcr
        dcdi_ref[...] = dci
        pr, pi = _cpow(ar, ai, seg_len)
        last =(lax.broadcasted_iota(jnp.int32, (SUBLANES, sh), 0) & (nseg - 1)) == nseg - 1

        def nxt(x):
            return jnp.where(last, 0.0, pltpu.roll(x, SUBLANES - 1, 0))

        xr, xi = fr, fi
        for _ in range(nseg - 1):
            xr, xi = _cmul_add(pr, -pi, nxt(xr), nxt(xi), fr, fi)
        g0r, g0i = nxt(xr), nxt(xi)

        def adj_step(t, c, s_before):
            gr, gi = _cmul_add(ar, -ai, c[0], c[1], *read_g(t))
            gre[_rows8(t), :], gim[_rows8(t), :] = gr, gi
            spr, spi = s_before
            return gr, gi, c[2] + spr * gr + spi * gi, c[3] + spr * gi - spi * gr

        carry = lax.fori_loop(0, seg_len - 1, lambda i, c: adj_step(seg_len - 1 - i, c, read_s(seg_len - 2 - i)),
                              (g0r, g0i, z, z))
        carry = adj_step(0, carry, (i_r, i_i))
        dar_ref[...] = jnp.sum(carry[2], axis=0, keepdims=True)
        dai_ref[...] = jnp.sum(carry[3], axis=0, keepdims=True)

        def d_chunk(c, carry):
            ub = u_ref[chunk(c), :].astype(BF)
            grb = gre[chunk(c), :].astype(BF)
            gib = gim[chunk(c), :].astype(BF)
            du = _dot(grb, bdr_ref[...], NT) + _dot(gib, bdi_ref[...], NT)
            du_ref[chunk(c), :] = du + d_ref[...] * dy_ref[chunk(c), :]
            dd = carry[2] + jnp.sum(dy_ref[chunk(c), :] * u_ref[chunk(c), :], axis=0, keepdims=True)
            return carry[0] + _dot(ub, grb, TN), carry[1] + _dot(ub, gib, TN), dd

        zb = jnp.zeros((LANES, sh), F32)
        dbr, dbi, dd = lax.fori_loop(0, R // rc, d_chunk, (zb, zb, jnp.zeros((1, LANES), F32)))
        dbdr_ref[...] = dbr
        dbdi_ref[...] = dbi
        dd_ref[...] = dd

    bd_sh = jax.ShapeDtypeStruct((nj, LANES, STATE_COLS), F32)
    cd_sh = jax.ShapeDtypeStruct((nj, STATE_COLS, LANES), F32)
    a_sh = jax.ShapeDtypeStruct((nj, 1, STATE_COLS), F32)
    return _call(
        body, "s5_bwd", (nj,),
        [sp["rows"], sp["rows"], sp["bd"], sp["bd"], sp["cd"], sp["cd"], sp["a"], sp["a"],
         sp["vec"], sp["init"], sp["init"]],
        [sp["rows"], sp["bd"], sp["bd"], sp["cd"], sp["cd"], sp["a"], sp["a"], sp["vec"]],
        [jax.ShapeDtypeStruct((R, D), F32), bd_sh, bd_sh, cd_sh, cd_sh, a_sh, a_sh,
         jax.ShapeDtypeStruct((1, D), F32)],
        (u, dy, bdr, bdi, cdr, cdi, are, aim, dsk, init_re, init_im),
        scratch=[pltpu.VMEM((R, sh), F32) for _ in range(4)], comm=comm)


def _glu_fwd(yg, wa, wb, h):
    T, D = yg.shape
    N = wa.shape[1]
    bm = min(ROW_TILE, T)
    bn = min(ROW_TILE, N)
    wspec = pl.BlockSpec((D, bn), lambda i, j, k: (0, j))
    ospec = pl.BlockSpec((bm, bn), lambda i, j, k: (i, j))

    def epi(accs, ex, outs):
        pa, pb = accs
        outs[0][...] = ex[0][...] + pa * _sig(pb)
        outs[1][...] = pa.astype(BF)
        outs[2][...] = pb.astype(BF)

    return _mm("glu_fwd", (T // bm, N // bn, 1),
               [(yg, pl.BlockSpec((bm, D), lambda i, j, k: (i, 0))), (wa, wspec), (wb, wspec)],
               [(0, 1, NN, 0), (0, 2, NN, 1)], 2, None, [(h, ospec)],
               [(jax.ShapeDtypeStruct((T, N), F32), ospec), (jax.ShapeDtypeStruct((T, N), BF), ospec),
                (jax.ShapeDtypeStruct((T, N), BF), ospec)], epi)


def _glu_bwd_gates(dz, pa, pb):
    T, D = dz.shape
    bm = min(ROW_TILE, T)

    def body(dz_ref, pa_ref, pb_ref, dpa_ref, dpb_ref):
        dz = dz_ref[...]
        sg = _sig(pb_ref[...].astype(F32))
        dpa_ref[...] = (dz * sg).astype(BF)
        dpb_ref[...] = (dz * pa_ref[...].astype(F32) * sg * (1.0 - sg)).astype(BF)

    row = pl.BlockSpec((bm, D), lambda i: (i, 0))
    return pl.pallas_call(
        body, grid=(T // bm,), in_specs=[row] * 3, out_specs=[row] * 2,
        out_shape=[jax.ShapeDtypeStruct((T, D), BF)] * 2, name="glu_bwd_gates",
        compiler_params=_cparams(1))(dz, pa, pb)


def _glu_bwd_y(dpa, dpb, wa, wb, y_pre, comm=None):
    T, N = dpa.shape
    D = wa.shape[0]
    bm = min(ROW_TILE, T)
    bn = min(ROW_TILE, D)
    aspec = pl.BlockSpec((bm, N), lambda i, j, k: (i, 0))
    wspec = pl.BlockSpec((bn, N), lambda i, j, k: (j, 0))
    ospec = pl.BlockSpec((bm, bn), lambda i, j, k: (i, j))

    def epi(accs, ex, outs):
        outs[0][...] = accs[0] * _gelu_grad(ex[0][...])

    return _mm("glu_bwd_y", (T // bm, D // bn, 1), [(dpa, aspec), (wa, wspec), (dpb, aspec), (wb, wspec)],
               [(0, 1, NT, 0), (2, 3, NT, 0)], 1, None, [(y_pre, ospec)],
               [(jax.ShapeDtypeStruct((T, D), F32), ospec)], epi, comm=comm)[0]


def _block_diag_in(x, nj):
    g = GROUPS_PER_BLOCK
    x = x.reshape(nj, g, 1, S5_GROUP, S5_STATE)
    eye = jnp.eye(g, dtype=bool)[None, :, :, None, None]
    full = jnp.where(eye, x, 0.0)
    return full.transpose(0, 1, 3, 2, 4).reshape(nj, g * S5_GROUP, g * S5_STATE)


def _block_diag_out(x, nj):
    return _block_diag_in(x, nj).transpose(0, 2, 1)


def _diag_of_in(m, nj):
    g = GROUPS_PER_BLOCK
    m5 = m.reshape(nj, g, S5_GROUP, g, S5_STATE)
    d = jnp.diagonal(m5, axis1=1, axis2=3)
    return d.transpose(0, 3, 1, 2).reshape(nj * g, S5_GROUP, S5_STATE)


def _mixer_s5_fwd(h, gain, p, dsk, wa, wb, nb, seq, comm_s5=None):
    T, D = h.shape
    nj = D // LANES
    nseg = SUBLANES // nb
    seg_len = seq // nseg
    G = p["s5_lambda_re"].shape[1]
    lam_re = p["s5_lambda_re"].reshape(G, 1, S5_STATE)
    lam_im = p["s5_lambda_im"].reshape(G, 1, S5_STATE)
    log_dt = p["s5_log_dt"].reshape(G, 1, 1)
    bt_re = p["s5_b_re"][0].transpose(0, 2, 1)
    bt_im = p["s5_b_im"][0].transpose(0, 2, 1)
    ar, ai, bbr, bbi = _s5_discretize(lam_re, lam_im, log_dt, bt_re, bt_im)
    mats = (_block_diag_in(bbr, nj).astype(BF), _block_diag_in(bbi, nj).astype(BF),
            _block_diag_out(p["s5_c_re"][0], nj).astype(BF),
            _block_diag_out(-p["s5_c_im"][0], nj).astype(BF),
            ar.reshape(nj, 1, STATE_COLS), ai.reshape(nj, 1, STATE_COLS), dsk)
    h_seg = _to_seg(h, seg_len)
    u = _rms_fwd(h_seg, gain, F32)
    y_pre, yg, init_re, init_im = _s5_fwd(u, mats, seg_len, nseg, comm=comm_s5)
    h_out, pa, pb = _glu_fwd(yg, wa, wb, h_seg)
    disc_in = (lam_re, lam_im, log_dt, bt_re, bt_im)
    return _to_tok(h_out, seg_len), (h_seg, u, mats, y_pre, yg, init_re, init_im, pa, pb, disc_in, seg_len, nseg)


def _mixer_s5_bwd(dh, saved, gain, wa, wb, reduce_start, carry):
    h_seg, u, mats, y_pre, yg, init_re, init_im, pa, pb, disc_in, seg_len, nseg = saved
    T, D = h_seg.shape
    nj = D // LANES
    G = nj * GROUPS_PER_BLOCK
    dh_seg = _to_seg(dh, seg_len)
    dpa, dpb = _glu_bwd_gates(dh_seg, pa, pb)
    dy = _glu_bwd_y(dpa, dpb, wa, wb, y_pre)
    dwa, dwb = _mm_tn("glu_dw", yg, [dpa, dpb])
    comm = _merge_comms(reduce_start(["s5_glu_wa", "s5_glu_wb"],
                                     [dwa.reshape(N_DEV, -1, D), dwb.reshape(N_DEV, -1, D)]) + [carry])
    du, dbdr, dbdi, dcdr, dcdi, dar, dai, dd = _s5_bwd(u, dy, mats, init_re, init_im, seg_len, nseg, comm=comm)
    d_bbr = _diag_of_in(dbdr, nj)
    d_bbi = _diag_of_in(dbdi, nj)
    d_c_re = _diag_of_in(dcdr.transpose(0, 2, 1), nj)
    d_c_im = -_diag_of_in(dcdi.transpose(0, 2, 1), nj)
    dlr, dli, dld, dbr, dbi = _s5_discretize_bwd(
        *disc_in, dar.reshape(G, 1, S5_STATE), dai.reshape(G, 1, S5_STATE), d_bbr, d_bbi)
    small = {"s5_lambda_re": dlr.reshape(1, G, S5_STATE), "s5_lambda_im": dli.reshape(1, G, S5_STATE),
             "s5_log_dt": dld.reshape(1, G),
             "s5_b_re": dbr.transpose(0, 2, 1)[None], "s5_b_im": dbi.transpose(0, 2, 1)[None],
             "s5_c_re": d_c_re[None], "s5_c_im": d_c_im[None], "s5_d": dd}
    dh_in, _, dgain = _rms_bwd(du, h_seg, gain, dh_seg)
    dh_in = _to_tok(dh_in, seg_len)
    return dh_in, dh_in.astype(BF), dgain, small


def _mesh_pos():
    return lax.axis_index("x"), lax.axis_index("y"), lax.axis_index("c")


class _Gather:
    def __init__(self, srcs, slots, send_sems, recv_sems):
        self.srcs, self.slots, self.send_sems, self.recv_sems = srcs, slots, send_sems, recv_sems
        x, y, c = _mesh_pos()
        self.c = c
        self.me, self.sib = (x, y, c), (x, y, 1 - c)
        self.chips = [(1 - x, y), (x, 1 - y), (1 - x, 1 - y)]

    def copy(self, a, k, block, to, own=False):
        dst = self.slots[a].at[4 * block[0] + 2 * block[1] + block[2]]
        return pltpu.make_async_remote_copy(
            src_ref=self.srcs[a] if own else dst, dst_ref=dst, send_sem=self.send_sems.at[7 * a + k],
            recv_sem=self.recv_sems.at[7 * a + k], device_id=to, device_id_type=MESH)

    def own_copies(self, a):
        cps = [self.copy(a, 0, self.me, self.sib, own=True)]
        return cps + [self.copy(a, 1 + j, self.me, (*chip, self.c), own=True) for j, chip in enumerate(self.chips)]

    def start(self):
        for a in range(len(self.srcs)):
            for cp in self.own_copies(a):
                cp.start()

    def finish(self):
        n = len(self.srcs)
        for a in range(n):
            for j, chip in enumerate(self.chips):
                self.copy(a, 1 + j, (*chip, self.c), self.me).wait_recv()
                self.copy(a, 4 + j, (*chip, self.c), self.sib).start()
        for a in range(n):
            self.copy(a, 0, self.sib, self.me).wait_recv()
            for j, chip in enumerate(self.chips):
                self.copy(a, 4 + j, (*chip, 1 - self.c), self.me).wait_recv()
        for a in range(n):
            for cp in self.own_copies(a):
                cp.wait_send()
            for j, chip in enumerate(self.chips):
                self.copy(a, 4 + j, (*chip, self.c), self.sib).wait_send()


def _gather_comm(arrs):
    n = len(arrs)

    def local(xs, outs, sems, a):
        x, y, c = _mesh_pos()
        return pltpu.make_async_copy(xs[a], outs[a].at[4 * x + 2 * y + c], sems[2].at[a])

    def start(xs, outs, sems):
        for a in range(n):
            local(xs, outs, sems, a).start()
        _Gather(xs, outs, sems[0], sems[1]).start()

    def finish(xs, outs, sems):
        _Gather(xs, outs, sems[0], sems[1]).finish()
        for a in range(n):
            local(xs, outs, sems, a).wait()

    return _Comm(list(arrs), [jax.ShapeDtypeStruct((N_DEV,) + a.shape, a.dtype) for a in arrs],
                 [pltpu.SemaphoreType.DMA((7 * n,)), pltpu.SemaphoreType.DMA((7 * n,)),
                  pltpu.SemaphoreType.DMA((n,))], start, finish)


def _exchange_comm(parts):
    n = len(parts)

    def copies(ps, outs, sems):
        x, y, c = _mesh_pos()
        cps = []
        for a in range(n):
            for j in range(1, 4):
                to = (jnp.bitwise_xor(x, j // 2), jnp.bitwise_xor(y, j % 2), c)
                cps.append(pltpu.make_async_remote_copy(
                    src_ref=ps[a].at[j], dst_ref=outs[a].at[j - 1], send_sem=sems[0].at[3 * a + j - 1],
                    recv_sem=sems[1].at[3 * a + j - 1], device_id=to, device_id_type=MESH))
        return cps

    def start(ps, outs, sems):
        for cp in copies(ps, outs, sems):
            cp.start()

    def finish(ps, outs, sems):
        for cp in copies(ps, outs, sems):
            cp.wait()

    return _Comm(list(parts), [jax.ShapeDtypeStruct((3,) + p.shape[1:], p.dtype) for p in parts],
                 [pltpu.SemaphoreType.DMA((3 * n,)), pltpu.SemaphoreType.DMA((3 * n,))], start, finish)


def _run_comm(comm, name):
    ci, co = len(comm.ins), len(comm.outs)

    def body(*refs):
        comm.start(refs[:ci], refs[ci:ci + co], refs[ci + co:])
        comm.finish(refs[:ci], refs[ci:ci + co], refs[ci + co:])

    any_spec = pl.BlockSpec(memory_space=pl.ANY)
    comm.set_results(pl.pallas_call(
        body, in_specs=[any_spec] * ci, out_specs=[any_spec] * co, out_shape=list(comm.outs),
        scratch_shapes=list(comm.sems), name=name, compiler_params=_cparams(0))(*comm.ins))


def _pair_exchange(grads, name):
    n = len(grads)

    def body(*refs):
        gs, outs = refs[:n], refs[n:2 * n]
        send_sems, recv_sems = refs[2 * n:]
        x, y, c = _mesh_pos()
        copies = []
        for a in range(n):
            for k in range(4):
                copies.append(pltpu.make_async_remote_copy(
                    src_ref=gs[a].at[2 * k + 1 - c], dst_ref=outs[a].at[k], send_sem=send_sems.at[4 * a + k],
                    recv_sem=recv_sems.at[4 * a + k], device_id=(x, y, 1 - c), device_id_type=MESH))
        for cp in copies:
            cp.start()
        for cp in copies:
            cp.wait()

    any_spec = pl.BlockSpec(memory_space=pl.ANY)
    return pl.pallas_call(
        body, in_specs=[any_spec] * n, out_specs=[any_spec] * n,
        out_shape=[jax.ShapeDtypeStruct((4,) + g.shape[1:], g.dtype) for g in grads],
        scratch_shapes=[pltpu.SemaphoreType.DMA((4 * n,)), pltpu.SemaphoreType.DMA((4 * n,))],
        name=name, compiler_params=_cparams(0))(*grads)


def _pair_sum(grad, recv, pos):
    _, R, C = grad.shape
    br = _row_block(R, C, PAIR_SUM_ELEMS)

    def body(pos_ref, g_ref, r_ref, o_ref):
        o_ref[...] = (g_ref[...].astype(F32) + r_ref[...].astype(F32)).astype(BF)

    def chip(j, p):
        return jnp.bitwise_xor(p[1], j)

    return pl.pallas_call(
        body, grid_spec=pltpu.PrefetchScalarGridSpec(
            num_scalar_prefetch=1, grid=(4, R // br),
            in_specs=[pl.BlockSpec((None, br, C), lambda j, i, p: (2 * chip(j, p) + p[0], i, 0)),
                      pl.BlockSpec((None, br, C), lambda j, i, p: (chip(j, p), i, 0))],
            out_specs=pl.BlockSpec((None, br, C), lambda j, i, p: (j, i, 0))),
        out_shape=jax.ShapeDtypeStruct((4, R, C), BF), name="pair_sum", compiler_params=_cparams(2))(pos, grad, recv)


def _adamw(w, g, m, v):
    m = ADAM_B1 * m + (1.0 - ADAM_B1) * g
    v = ADAM_B2 * v + (1.0 - ADAM_B2) * (g * g)
    m_hat = m / (1.0 - ADAM_B1 ** ADAM_STEP)
    v_hat = v / (1.0 - ADAM_B2 ** ADAM_STEP)
    return -ADAM_LR * (m_hat / (jnp.sqrt(v_hat) + ADAM_EPS) + ADAM_WD * w), m, v


def _adamw_piece(w, m, v, piece, part, recv, bufs):
    _, R, C = w.shape
    br = _row_block(R, C)

    def body(w_ref, m_ref, v_ref, p_ref, r_ref, b0, b1, b2, b3, g_ref, d_ref, nm_ref, nv_ref):
        g = p_ref[...].astype(F32)
        for j in range(3):
            g = g + r_ref[j].astype(F32)
        d, nm, nv = _adamw(w_ref[...], g, m_ref[...], v_ref[...])
        g_ref[...] = g
        d_ref[...] = d
        nm_ref[...] = nm
        nv_ref[...] = nv

    row = pl.BlockSpec((None, br, C), lambda i: (piece, i, 0))
    any_spec = pl.BlockSpec(memory_space=pl.ANY)
    return pl.pallas_call(
        body, grid=(R // br,),
        in_specs=[row, row, row, pl.BlockSpec((None, br, C), lambda i: (0, i, 0)),
                  pl.BlockSpec((3, br, C), lambda i: (0, i, 0))] + [any_spec] * 4,
        out_specs=[row] * 4, out_shape=[jax.ShapeDtypeStruct(w.shape, F32)] * 4,
        input_output_aliases={5: 0, 6: 1, 7: 2, 8: 3}, name="adamw_piece",
        compiler_params=_cparams(1))(w, m, v, part, recv, *bufs)


def _all_reduce_small(x):
    rows = x.shape[0]

    def body(x_ref, o_ref, buf, send_sems, recv_sems):
        xp, yp, cp = _mesh_pos()
        buf[4 * xp + 2 * yp + cp] = x_ref[...]
        gather = _Gather([x_ref], [buf], send_sems, recv_sems)
        gather.start()
        gather.finish()
        acc = buf[0]
        for d in range(1, N_DEV):
            acc = acc + buf[d]
        o_ref[...] = acc

    vm = pl.BlockSpec(memory_space=pltpu.VMEM)
    return pl.pallas_call(
        body, in_specs=[vm], out_specs=vm, out_shape=jax.ShapeDtypeStruct(x.shape, F32),
        scratch_shapes=[pltpu.VMEM((N_DEV, rows, LANES), F32), pltpu.SemaphoreType.DMA((7,)),
                        pltpu.SemaphoreType.DMA((7,))],
        name="all_reduce_small", compiler_params=_cparams(0))(x)


def _adamw_small(w, g, m, v):
    def body(w_ref, g_ref, m_ref, v_ref, d_ref, nm_ref, nv_ref):
        d, nm, nv = _adamw(w_ref[...], g_ref[...], m_ref[...], v_ref[...])
        d_ref[...] = d
        nm_ref[...] = nm
        nv_ref[...] = nv

    sh = jax.ShapeDtypeStruct(w.shape, F32)
    return pl.pallas_call(body, out_shape=[sh] * 3, name="adamw_small", compiler_params=_cparams(0))(w, g, m, v)


def _pack(arrs):
    flat = jnp.concatenate([a.reshape(-1).astype(F32) for a in arrs])
    rows = -(-flat.shape[0] // (SUBLANES * LANES)) * SUBLANES
    return jnp.pad(flat, (0, rows * LANES - flat.shape[0])).reshape(rows, LANES)


def _unpack(buf, shapes):
    flat = buf.reshape(-1)
    out, off = [], 0
    for s in shapes:
        n = 1
        for d in s:
            n *= d
        out.append(flat[off:off + n].reshape(s))
        off += n
    return out


BIG = ("ffn_w1", "ffn_w3", "ffn_w2", "ab_w_in", "ab_w_out", "s5_glu_wa", "s5_glu_wb")
NAMES = ("ln_ffn_pre", "ln_mix", "ln_ffn_post", "ln_final", "ffn_w1", "ffn_w3", "ffn_w2", "ab_w_in",
         "ab_conv_w", "ab_w_out", "s5_lambda_re", "s5_lambda_im", "s5_log_dt", "s5_b_re", "s5_b_im",
         "s5_c_re", "s5_c_im", "s5_d", "s5_glu_wa", "s5_glu_wb")


def kernel(x, ln_ffn_pre, ln_mix, ln_ffn_post, ln_final, ffn_w1, ffn_w3, ffn_w2, ab_w_in, ab_conv_w, ab_w_out, s5_lambda_re, s5_lambda_im, s5_log_dt, s5_b_re, s5_b_im, s5_c_re, s5_c_im, s5_d, s5_glu_wa, s5_glu_wb, loss_target, m_ln_ffn_pre, m_ln_mix, m_ln_ffn_post, m_ln_final, m_ffn_w1, m_ffn_w3, m_ffn_w2, m_ab_w_in, m_ab_conv_w, m_ab_w_out, m_s5_lambda_re, m_s5_lambda_im, m_s5_log_dt, m_s5_b_re, m_s5_b_im, m_s5_c_re, m_s5_c_im, m_s5_d, m_s5_glu_wa, m_s5_glu_wb, v_ln_ffn_pre, v_ln_mix, v_ln_ffn_post, v_ln_final, v_ffn_w1, v_ffn_w3, v_ffn_w2, v_ab_w_in, v_ab_conv_w, v_ab_w_out, v_s5_lambda_re, v_s5_lambda_im, v_s5_log_dt, v_s5_b_re, v_s5_b_im, v_s5_c_re, v_s5_c_im, v_s5_d, v_s5_glu_wa, v_s5_glu_wb):
    w = dict(zip(NAMES, (ln_ffn_pre, ln_mix, ln_ffn_post, ln_final, ffn_w1, ffn_w3, ffn_w2, ab_w_in, ab_conv_w,
                         ab_w_out, s5_lambda_re, s5_lambda_im, s5_log_dt, s5_b_re, s5_b_im, s5_c_re, s5_c_im,
                         s5_d, s5_glu_wa, s5_glu_wb)))
    mom = dict(zip(NAMES, (m_ln_ffn_pre, m_ln_mix, m_ln_ffn_post, m_ln_final, m_ffn_w1, m_ffn_w3, m_ffn_w2,
                           m_ab_w_in, m_ab_conv_w, m_ab_w_out, m_s5_lambda_re, m_s5_lambda_im, m_s5_log_dt,
                           m_s5_b_re, m_s5_b_im, m_s5_c_re, m_s5_c_im, m_s5_d, m_s5_glu_wa, m_s5_glu_wb)))
    var = dict(zip(NAMES, (v_ln_ffn_pre, v_ln_mix, v_ln_ffn_post, v_ln_final, v_ffn_w1, v_ffn_w3, v_ffn_w2,
                           v_ab_w_in, v_ab_conv_w, v_ab_w_out, v_s5_lambda_re, v_s5_lambda_im, v_s5_log_dt,
                           v_s5_b_re, v_s5_b_im, v_s5_c_re, v_s5_c_im, v_s5_d, v_s5_glu_wa, v_s5_glu_wb)))
    nb, seq, D = x.shape
    T = nb * seq
    assert ln_mix.shape[0] == 2 and ab_w_in.shape[0] == 1 and s5_glu_wa.shape[0] == 1
    xc, yc, cc = _mesh_pos()
    dev = 4 * xc + 2 * yc + cc
    pos = jnp.stack([cc, 2 * xc + yc]).astype(jnp.int32)
    bq = min(ATTN_TILE, seq)
    tabs =_rope_tables(seq) + (_branch_bias(seq // bq, bq),)

    def ffn_piece(k, li, fj):
        return w[k][li, fj].astype(BF)

    g0 = _gather_comm([ffn_piece("ffn_w1", 0, 0), ffn_piece("ffn_w3", 0, 0), ab_conv_w[0], s5_d])
    _run_comm(g0, "gather_first")
    w1, w3 = {(0, 0): g0.results[0]}, {(0, 0): g0.results[1]}
    w2 = {}
    conv_w = g0.results[2].transpose(1, 0, 2).reshape(3, -1)
    dsk = g0.results[3].reshape(1, D)
    gains = {k: [w[k][i:i + 1] for i in range(2)] for k in ("ln_ffn_pre", "ln_mix", "ln_ffn_post")}

    h = x.reshape(T, D)
    saved = {}

    def ffn_fwd(h, gain, key, tag, comm_up, comm_down, after_up):
        n = _rms_fwd(h, gain, BF)
        t1, t3, g = _ffn_up(n, w1[key], w3[key], comm=comm_up)
        after_up()
        saved[tag] = (h, n, t1, t3, g)
        return _ffn_down(g, w2[key], h, comm=comm_down)

    c_up = _gather_comm([ffn_piece("ffn_w2", 0, 0), ab_w_out[0].astype(BF)])
    c_dn = _gather_comm([ab_w_in[0].astype(BF)])
    h = ffn_fwd(h, gains["ln_ffn_pre"][0], (0, 0), "pre0", c_up, c_dn,
                lambda: w2.update({(0, 0): c_up.results[0]}))
    wout = c_up.results[1].reshape(-1, D)
    wing = c_dn.results[0]
    c_proj = _gather_comm([ffn_piece("ffn_w1", 0, 1)])
    c_attn = _gather_comm([ffn_piece("ffn_w3", 0, 1), s5_glu_wa[0].astype(BF)])
    c_out = _gather_comm([s5_glu_wb[0].astype(BF)])
    h, saved["mix0"] = _mixer_ab_fwd(h, gains["ln_mix"][0], wing, conv_w, wout, tabs, nb, seq, c_proj, c_attn, c_out)
    w1[(0, 1)] = c_proj.results[0]
    w3[(0, 1)] = c_attn.results[0]
    wa = c_attn.results[1].reshape(-1, D)
    wb = c_out.results[0].reshape(-1, D)
    c_up2 = _gather_comm([ffn_piece("ffn_w2", 0, 1), ffn_piece("ffn_w1", 1, 0)])
    c_dn = _gather_comm([ffn_piece("ffn_w3", 1, 0)])
    h = ffn_fwd(h, gains["ln_ffn_post"][0], (0, 1), "post0", c_up2, c_dn,
                lambda: w2.update({(0, 1): c_up2.results[0]}))
    w1[(1, 0)] = c_up2.results[1]
    w3[(1, 0)] = c_dn.results[0]
    c_up3 = _gather_comm([ffn_piece("ffn_w2", 1, 0), ffn_piece("ffn_w1", 1, 1)])
    c_dn = _gather_comm([ffn_piece("ffn_w3", 1, 1)])
    h = ffn_fwd(h, gains["ln_ffn_pre"][1], (1, 0), "pre1", c_up3, c_dn,
                lambda: w2.update({(1, 0): c_up3.results[0]}))
    w1[(1, 1)] = c_up3.results[1]
    w3[(1, 1)] = c_dn.results[0]
    c_s5 = _gather_comm([ffn_piece("ffn_w2", 1, 1)])
    h, saved["mix1"] = _mixer_s5_fwd(h, gains["ln_mix"][1], w, dsk, wa, wb, nb, seq, c_s5)
    w2[(1, 1)] = c_s5.results[0]
    h = ffn_fwd(h, gains["ln_ffn_post"][1], (1, 1), "post1", None, None, lambda: None)
    dh, dhb, d_ln_final, loss_part = _loss_head(h, ln_final.reshape(1, D), loss_target.reshape(T, D))
    loss = lax.psum(loss_part[0, 0], ("x", "y", "c"))

    reduced = {}

    def reduce_start(names, grads):
        recv = _pair_exchange(grads, "pair_exchange")
        comms = []
        for nm, g, r in zip(names, grads, recv):
            part = _pair_sum(g, r, pos)
            comms.append(_exchange_comm([part]))
            reduced[nm] = (part, comms[-1])
        return comms

    def ffn_bwd(dh, dhb, key, tag, gain, carry, is_last=False):
        h_in, n, t1, t3, g = saved[tag]
        da1, da3 = _ffn_bwd_hidden(dhb, w2[key], t1, t3, comm=carry)
        c2, = reduce_start([("ffn_w2",) + key], [_ffn_dw2(g, dhb)])
        dw1, dw3 = _ffn_dw13(n, da1, da3, comm=c2)
        c1, c3 = reduce_start([("ffn_w1",) + key, ("ffn_w3",) + key], [dw1, dw3])
        dn = _ffn_dn(da1, da3, w1[key], w3[key], comm=_merge_comms([c1, c3]) if is_last else c1)
        return list(_rms_bwd(dn, h_in, gain, dh)) + [None if is_last else c3]

    g_small = {"ln_final": d_ln_final.reshape(D)}
    g_ln = {k: [None, None] for k in gains}
    dh, dhb, g_ln["ln_ffn_post"][1], carry = ffn_bwd(dh, dhb, (1, 1), "post1", gains["ln_ffn_post"][1], None)
    dh, dhb, g_ln["ln_mix"][1], s5_small = _mixer_s5_bwd(
        dh, saved["mix1"], gains["ln_mix"][1], wa, wb, reduce_start, carry)
    g_small.update(s5_small)
    dh, dhb, g_ln["ln_ffn_pre"][1], carry = ffn_bwd(dh, dhb, (1, 0), "pre1", gains["ln_ffn_pre"][1], None)
    dh, dhb, g_ln["ln_ffn_post"][0], carry = ffn_bwd(dh, dhb, (0, 1), "post0", gains["ln_ffn_post"][0], carry)
    dh, dhb, g_ln["ln_mix"][0], g_small["ab_conv_w"], carry = _mixer_ab_bwd(
        dh, dhb, saved["mix0"], gains["ln_mix"][0], wing, conv_w, wout, tabs, nb, seq, reduce_start, carry)
    dh, dhb, g_ln["ln_ffn_pre"][0], _ = ffn_bwd(dh, dhb, (0, 0), "pre0", gains["ln_ffn_pre"][0], carry, is_last=True)
    grad_x = dh.reshape(nb, seq, D)
    for k in g_ln:
        g_small[k] = jnp.concatenate(g_ln[k], axis=0)

    out = {}
    for k in BIG:
        transposed = k in ("ffn_w1", "ffn_w3")
        pieces = [(li, fj) for li in range(2) for fj in range(2)] if w[k].ndim == 4 else [None]

        def view(a):
            a = a.swapaxes(-1, -2) if transposed else a
            return a.reshape(len(pieces), -1, a.shape[-1])

        w3d, m3d, v3d = view(w[k]), view(mom[k]), view(var[k])
        bufs = [lax.empty(w3d.shape, F32) for _ in range(4)]
        for q, key in enumerate(pieces):
            part, comm = reduced[k if key is None else (k,) + key]
            bufs = _adamw_piece(w3d, m3d, v3d, q, part, comm.results[0], bufs)
        if transposed:
            out[k] = [t.reshape(w[k].shape[:2] + w3d.shape[1:]).swapaxes(-1, -2) for t in bufs]
        else:
            out[k] = [t.reshape(w[k].shape) for t in bufs]

    small_names = [k for k in NAMES if k not in BIG]
    red = _unpack(_all_reduce_small(_pack([g_small[k] for k in small_names])),
                  [g_small[k].shape for k in small_names])
    g_red = dict(zip(small_names, red))
    cw = w["ab_conv_w"].shape[-1]
    g_red["ab_conv_w"] = lax.dynamic_slice_in_dim(g_red["ab_conv_w"], dev * cw, cw, axis=1)[None]
    dsz = w["s5_d"].shape[-1]
    g_red["s5_d"] = lax.dynamic_slice_in_dim(g_red["s5_d"].reshape(1, -1), dev * dsz, dsz, axis=1)
    shapes = [w[k].shape for k in small_names]
    g_red = {k: g_red[k].reshape(w[k].shape) for k in small_names}
    d_s, m_s, v_s = _adamw_small(_pack([w[k] for k in small_names]), _pack([g_red[k] for k in small_names]),
                                 _pack([mom[k] for k in small_names]), _pack([var[k] for k in small_names]))
    for k, d, nm, nv in zip(small_names, _unpack(d_s, shapes), _unpack(m_s, shapes), _unpack(v_s, shapes)):
        out[k] = [g_red[k], d, nm, nv]

    return (loss, grad_x, *[out[k][0] for k in NAMES], *[out[k][1] for k in NAMES],
            *[out[k][2] for k in NAMES], *[out[k][3] for k in NAMES])
```

```python
import jax
import jax.numpy as jnp
from jax import lax
from jax.experimental import pallas as pl
from jax.experimental.pallas import tpu as pltpu

F32, BF = jnp.float32, jnp.bfloat16
N_DEV = 8
MESH = pl.DeviceIdType.MESH
LANES = 128
SUBLANES = 8
VMEM_LIMIT = 56 * 2 ** 20
ROW_TILE = 512
FFN_ROW_TILE = 1024
COL_TILE = 512
ATTN_TILE = 512
SCAN_UNROLL = 4
ELEMS_PER_BLOCK = 256 * 1024
PAIR_SUM_ELEMS = 2048 * 1024
RMS_EPS = 1e-6
ROPE_THETA = 10000.0
NEG_INF = -1e30
S5_STATE = 64
S5_GROUP = 16
GROUPS_PER_BLOCK = LANES // S5_GROUP
STATE_COLS = GROUPS_PER_BLOCK * S5_STATE
DILATED_PATTERN = ((128, 1), (512, 4), (2048, 16))
ADAM_LR, ADAM_B1, ADAM_B2, ADAM_EPS, ADAM_WD, ADAM_STEP = 0.001, 0.9, 0.999, 1e-08, 0.01, 10
GELU_C = 0.7978845608028654
GELU_A = 0.044715


def _cparams(n_grid, vmem=VMEM_LIMIT):
    sem = ("arbitrary",) * n_grid if n_grid else None
    return pltpu.CompilerParams(dimension_semantics=sem, vmem_limit_bytes=vmem)


def _sig(x):
    return 1.0 / (1.0 + jnp.exp(-x))


def _gelu(x):
    return 0.5 * x * (1.0 + jnp.tanh(GELU_C * (x + GELU_A * x * x * x)))


def _gelu_grad(x):
    t = jnp.tanh(GELU_C * (x + GELU_A * x * x * x))
    return 0.5 * (1.0 + t) + 0.5 * x * (1.0 - t * t) * GELU_C * (1.0 + 3.0 * GELU_A * x * x)


def _dot(a, b, dims):
    a = a if a.dtype == BF else a.astype(BF)
    b = b if b.dtype == BF else b.astype(BF)
    return lax.dot_general(a, b, (dims, ((), ())), preferred_element_type=F32)


NN = ((1,), (0,))
NT = ((1,), (1,))
TN = ((0,), (0,))


def _row_block(rows, cols, elems=ELEMS_PER_BLOCK, mult=16):
    cap = max(mult, elems // cols)
    best = None
    for b in range(mult, min(rows, cap) + 1, mult):
        if rows % b == 0:
            best = b
    return rows if best is None else best


class _Comm:
    def __init__(self, ins, outs, sems, start, finish, members=()):
        self.ins, self.outs, self.sems, self.start, self.finish = ins, outs, sems, start, finish
        self.members = members
        self.results = None

    def set_results(self, res):
        self.results = list(res)
        off = 0
        for m in self.members:
            m.set_results(res[off:off + len(m.outs)])
            off += len(m.outs)


def _merge_comms(comms):
    comms = [c for c in comms if c is not None]
    if len(comms) < 2:
        return comms[0] if comms else None

    def each(fn_name, ins, outs, sems):
        i = o = s = 0
        for c in comms:
            ni, no, ns = len(c.ins), len(c.outs), len(c.sems)
            getattr(c, fn_name)(ins[i:i + ni], outs[o:o + no], sems[s:s + ns])
            i, o, s = i + ni, o + no, s + ns

    return _Comm([a for c in comms for a in c.ins], [a for c in comms for a in c.outs],
                 [a for c in comms for a in c.sems],
                 lambda ins, outs, sems: each("start", ins, outs, sems),
                 lambda ins, outs, sems: each("finish", ins, outs, sems), members=tuple(comms))


def _call(body, name, grid, in_specs, out_specs, out_shape, args, scratch=(), comm=None):
    in_specs, out_specs, out_shape, scratch = list(in_specs), list(out_specs), list(out_shape), list(scratch)
    if comm is None:
        return pl.pallas_call(body, grid=grid, in_specs=in_specs, out_specs=out_specs, out_shape=out_shape,
                              scratch_shapes=scratch, name=name, compiler_params=_cparams(len(grid)))(*args)
    n_in, n_out, n_sc = len(in_specs), len(out_specs), len(scratch)
    ci, co = len(comm.ins), len(comm.outs)

    def hosted(*refs):
        ins, refs = refs[:n_in], refs[n_in:]
        cins, refs = refs[:ci], refs[ci:]
        outs, refs = refs[:n_out], refs[n_out:]
        couts, refs = refs[:co], refs[co:]
        sc, csems = refs[:n_sc], refs[n_sc:]
        first = last = None
        for d, n in enumerate(grid):
            p = pl.program_id(d)
            first = (p == 0) if first is None else first & (p == 0)
            last = (p == n - 1) if last is None else last & (p == n - 1)

        @pl.when(first)
        def _():
            comm.start(cins, couts, csems)

        body(*ins, *outs, *sc)

        @pl.when(last)
        def _():
            comm.finish(cins, couts, csems)

    any_spec = pl.BlockSpec(memory_space=pl.ANY)
    res = pl.pallas_call(
        hosted, grid=grid, in_specs=in_specs + [any_spec] * ci, out_specs=out_specs + [any_spec] * co,
        out_shape=out_shape + list(comm.outs), scratch_shapes=scratch + list(comm.sems), name=name,
        compiler_params=_cparams(len(grid)))(*args, *comm.ins)
    comm.set_results(res[n_out:])
    return list(res[:n_out])


def _mm(name, grid, operands, pairs, n_acc, acc_shape, extras, outs, epilogue, comm=None, nrow=1, ncol=1,
        whole_tile_epilogue=False):
    nk = grid[2]
    n_op, n_ex, n_out = len(operands), len(extras), len(outs)

    def part_of(ref, dim, t, n):
        if n == 1:
            return ref
        size = ref.shape[dim] // n
        idx = [slice(None)] * len(ref.shape)
        idx[dim] = pl.ds(t * size, size)
        return ref.at[tuple(idx)]

    def tile_of(ref, r, c):
        return part_of(part_of(ref, 0, r, nrow), 1, c, ncol)

    def products(op, r, c):
        parts = [None] * n_acc
        for ai, bi, dims, ci in pairs:
            a = part_of(op[ai], 1 - dims[0][0], r, nrow)
            b = part_of(op[bi], 1 - dims[1][0], c, ncol)
            d = _dot(a[...], b[...], dims)
            parts[ci] = d if parts[ci] is None else parts[ci] + d
        return parts

    def body(*refs):
        op = refs[:n_op]
        ex = refs[n_op:n_op + n_ex]
        out = refs[n_op + n_ex:n_op + n_ex + n_out]
        acc = refs[n_op + n_ex + n_out:]
        tiles = [(r, c) for r in range(nrow) for c in range(ncol)]

        def views(refs_, t):
            return [tile_of(q, *t) for q in refs_]

        if nk == 1:
            parts = products(op, *tiles[0])
            for q, t in enumerate(tiles):
                nxt = products(op, *tiles[q + 1]) if q + 1 < len(tiles) else None
                epilogue(parts, views(ex, t), views(out, t))
                parts = nxt
            return
        k = pl.program_id(2)

        @pl.when(k == 0)
        def _():
            for q in acc:
                q[...] = jnp.zeros_like(q)

        for t in tiles:
            parts = products(op, *t)
            for q, p in zip(views(acc, t), parts):
                q[...] += p

        @pl.when(k == nk - 1)
        def _():
            if whole_tile_epilogue:
                epilogue(acc, ex, out)
                return
            for t in tiles:
                epilogue([q[...] for q in views(acc, t)], views(ex, t), views(out, t))

    return _call(body, name, grid, [s for _, s in operands] + [s for _, s in extras], [s for _, s in outs],
                 [sh for sh, _ in outs], [a for a, _ in operands] + [a for a, _ in extras],
                 scratch=[pltpu.VMEM(acc_shape, F32) for _ in range(n_acc if nk > 1 else 0)], comm=comm)


def _to_seg(a, seg_len):
    T, D = a.shape
    return a.reshape(SUBLANES, seg_len, D).transpose(1, 0, 2).reshape(T, D)


def _to_tok(a, seg_len):
    T, D = a.shape
    return a.reshape(seg_len, SUBLANES, D).transpose(1, 0, 2).reshape(T, D)


def _rms_fwd(h, gain, out_dtype):
    T, D = h.shape
    bm = min(ROW_TILE, T)

    def body(h_ref, g_ref, o_ref):
        x = h_ref[...]
        r = lax.rsqrt(jnp.mean(x * x, axis=-1, keepdims=True) + RMS_EPS)
        o_ref[...] = (x * r * g_ref[...]).astype(out_dtype)

    row = pl.BlockSpec((bm, D), lambda i: (i, 0))
    return pl.pallas_call(
        body, grid=(T // bm,), in_specs=[row, pl.BlockSpec((1, D), lambda i: (0, 0))],
        out_specs=row, out_shape=jax.ShapeDtypeStruct((T, D), out_dtype), name="rms_fwd",
        compiler_params=_cparams(1))(h, gain)


def _rms_bwd_rows(dn, x, g):
    r = lax.rsqrt(jnp.mean(x * x, axis=-1, keepdims=True) + RMS_EPS)
    xh = x * r
    dng = dn * g
    dx = r * (dng - xh * jnp.mean(dng * xh, axis=-1, keepdims=True))
    return dx, jnp.sum(dn * xh, axis=0, keepdims=True)


def _rms_bwd(dn, h, gain, dh_up):
    T, D = h.shape
    bm = min(ROW_TILE, T)

    def body(dn_ref, h_ref, g_ref, up_ref, dh_ref, dhb_ref, dg_ref):
        dx, dg = _rms_bwd_rows(dn_ref[...], h_ref[...], g_ref[...])
        dh = up_ref[...] + dx
        dh_ref[...] = dh
        dhb_ref[...] = dh.astype(BF)

        @pl.when(pl.program_id(0) == 0)
        def _():
            dg_ref[...] = jnp.zeros_like(dg_ref)

        dg_ref[...] += dg

    row = pl.BlockSpec((bm, D), lambda i: (i, 0))
    vec = pl.BlockSpec((1, D), lambda i: (0, 0))
    return pl.pallas_call(
        body, grid=(T // bm,), in_specs=[row, row, vec, row], out_specs=[row, row, vec],
        out_shape=[jax.ShapeDtypeStruct((T, D), F32), jax.ShapeDtypeStruct((T, D), BF),
                   jax.ShapeDtypeStruct((1, D), F32)],
        name="rms_bwd", compiler_params=_cparams(1))(dn, h, gain, dh_up)


def _loss_head(h, gain, target):
    T, D = h.shape
    bm = min(ROW_TILE, T)

    def body(h_ref, g_ref, t_ref, dh_ref, dhb_ref, dg_ref, loss_ref):
        x = h_ref[...]
        g = g_ref[...]
        r = lax.rsqrt(jnp.mean(x * x, axis=-1, keepdims=True) + RMS_EPS)
        err = x * r * g - t_ref[...]
        part = 0.5 * jnp.sum(jnp.sum(err * err, axis=-1, keepdims=True), axis=0, keepdims=True) / D
        dx, dg = _rms_bwd_rows(err / D, x, g)
        dh_ref[...] = dx
        dhb_ref[...] = dx.astype(BF)

        @pl.when(pl.program_id(0) == 0)
        def _():
            dg_ref[...] = jnp.zeros_like(dg_ref)
            loss_ref[...] = jnp.zeros_like(loss_ref)

        dg_ref[...] += dg
        loss_ref[...] += jnp.broadcast_to(part, loss_ref.shape)

    row = pl.BlockSpec((bm, D), lambda i: (i, 0))
    vec = pl.BlockSpec((1, D), lambda i: (0, 0))
    return pl.pallas_call(
        body, grid=(T // bm,), in_specs=[row, vec, row],
        out_specs=[row, row, vec, pl.BlockSpec((SUBLANES, LANES), lambda i: (0, 0))],
        out_shape=[jax.ShapeDtypeStruct((T, D), F32), jax.ShapeDtypeStruct((T, D), BF),
                   jax.ShapeDtypeStruct((1, D), F32), jax.ShapeDtypeStruct((SUBLANES, LANES), F32)],
        name="loss_head", compiler_params=_cparams(1))(h, gain, target)


def _ffn_up(n, w1g, w3g, comm=None):
    T, D = n.shape
    fs = w1g.shape[-1]
    bm = min(FFN_ROW_TILE, T)
    wspec = pl.BlockSpec((None, D, fs), lambda s, i, k: (s, 0, 0))
    ospec = pl.BlockSpec((None, bm, fs), lambda s, i, k: (s, i, 0))

    def epi(accs, ex, outs):
        a1, a3 = accs
        sg = _sig(a1)
        silu = a1 * sg
        outs[0][...] = (a3 * sg * (1.0 + a1 * (1.0 - sg))).astype(BF)
        outs[1][...] = silu.astype(BF)
        outs[2][...] = (silu * a3).astype(BF)

    sh = jax.ShapeDtypeStruct((N_DEV, T, fs), BF)
    return _mm("ffn_up", (N_DEV, T // bm, 1),
               [(n, pl.BlockSpec((bm, D), lambda s, i, k: (i, 0))), (w1g, wspec), (w3g, wspec)],
               [(0, 1, NN, 0), (0, 2, NN, 1)], 2, None, [], [(sh, ospec)] * 3, epi, comm=comm,
               nrow=max(1, bm // ROW_TILE))


def _ffn_down(g, w2g, h, comm=None):
    _, T, fs = g.shape
    D = h.shape[1]
    bm = min(FFN_ROW_TILE, T)
    row = pl.BlockSpec((bm, D), lambda i, j, s: (i, 0))

    def epi(accs, ex, outs):
        outs[0][...] = ex[0][...] + 0.5 * accs[0]

    return _mm("ffn_down", (T // bm, 1, N_DEV),
               [(g, pl.BlockSpec((None, bm, fs), lambda i, j, s: (s, i, 0))),
                (w2g, pl.BlockSpec((None, fs, D), lambda i, j, s: (s, 0, 0)))],
               [(0, 1, NN, 0)], 1, (bm, D), [(h, row)],
               [(jax.ShapeDtypeStruct((T, D), F32), row)], epi, comm=comm,
               nrow=max(1, bm // ROW_TILE), ncol=max(1, D // COL_TILE))[0]


def _ffn_bwd_hidden(dhb, w2g, t1, t3, comm=None):
    T, D = dhb.shape
    fs = t1.shape[-1]
    bm = min(FFN_ROW_TILE, T)
    aspec = pl.BlockSpec((None, bm, fs), lambda s, i, k: (s, i, 0))

    def epi(accs, ex, outs):
        dg = 0.5 * accs[0]
        outs[0][...] = (dg * ex[0][...].astype(F32)).astype(BF)
        outs[1][...] = (dg * ex[1][...].astype(F32)).astype(BF)

    sh = jax.ShapeDtypeStruct((N_DEV, T, fs), BF)
    return _mm("ffn_bwd_hidden", (N_DEV, T // bm, 1),
               [(dhb, pl.BlockSpec((bm, D), lambda s, i, k: (i, 0))),
                (w2g, pl.BlockSpec((None, fs, D), lambda s, i, k: (s, 0, 0)))],
               [(0, 1, NT, 0)], 1, None, [(t1, aspec), (t3, aspec)], [(sh, aspec)] * 2, epi, comm=comm,
               nrow=max(1, bm // ROW_TILE))


def _ffn_dw2(g, dhb, comm=None):
    _, T, fs = g.shape
    D = dhb.shape[1]
    bn = min(COL_TILE, D)

    def epi(accs, ex, outs):
        outs[0][...] = (0.5 * accs[0]).astype(BF)

    return _mm("ffn_dw2", (N_DEV, D // bn, 1),
               [(g, pl.BlockSpec((None, T, fs), lambda s, j, k: (s, 0, 0))),
                (dhb, pl.BlockSpec((T, bn), lambda s, j, k: (0, j)))],
               [(0, 1, TN, 0)], 1, None, [],
               [(jax.ShapeDtypeStruct((N_DEV, fs, D), BF), pl.BlockSpec((None, fs, bn), lambda s, j, k: (s, 0, j)))],
               epi, comm=comm)[0]


def _ffn_dw13(n, da1, da3, comm=None):
    T, D = n.shape
    fs = da1.shape[-1]
    bn = min(COL_TILE, D)
    dspec = pl.BlockSpec((None, T, fs), lambda s, j, k: (s, 0, 0))
    ospec = pl.BlockSpec((None, fs, bn), lambda s, j, k: (s, 0, j))

    def epi(accs, ex, outs):
        outs[0][...] = accs[0].astype(BF)
        outs[1][...] = accs[1].astype(BF)

    sh = jax.ShapeDtypeStruct((N_DEV, fs, D), BF)
    return _mm("ffn_dw13", (N_DEV, D // bn, 1),
               [(da1, dspec), (da3, dspec), (n, pl.BlockSpec((T, bn), lambda s, j, k: (0, j)))],
               [(0, 2, TN, 0), (1, 2, TN, 1)], 2, None, [], [(sh, ospec)] * 2, epi, comm=comm)


def _ffn_dn_rms(da1, da3, w1g, w3g, h, gain, dh_up, comm=None):
    _, T, fs = da1.shape
    D = w1g.shape[-2]
    bm = min(ROW_TILE, T)
    rows_per_pass = min(64, bm)
    dspec = pl.BlockSpec((None, bm, fs), lambda i, j, s: (s, i, 0))
    wspec = pl.BlockSpec((None, D, fs), lambda i, j, s: (s, 0, 0))
    row = pl.BlockSpec((bm, D), lambda i, j, s: (i, 0))
    vec = pl.BlockSpec((1, D), lambda i, j, s: (0, 0))

    def epi(acc, ex, outs):
        h_ref, g_ref, up_ref = ex
        dh_ref, dhb_ref, dg_ref = outs

        @pl.when(pl.program_id(0) == 0)
        def _():
            dg_ref[...] = jnp.zeros_like(dg_ref)

        g = g_ref[...]
        dg = jnp.zeros((1, D), F32)
        for r in range(bm // rows_per_pass):
            rows = pl.ds(r * rows_per_pass, rows_per_pass)
            dx, dg_r = _rms_bwd_rows(acc[0][rows, :], h_ref[rows, :], g)
            dh = up_ref[rows, :] + dx
            dh_ref[rows, :] = dh
            dhb_ref[rows, :] = dh.astype(BF)
            dg = dg + dg_r
        dg_ref[...] += dg

    return _mm("ffn_dn_rms", (T // bm, 1, N_DEV),
               [(da1, dspec), (w1g, wspec), (da3, dspec), (w3g, wspec)],
               [(0, 1, NT, 0), (2, 3, NT, 0)], 1, (bm, D), [(h, row), (gain, vec), (dh_up, row)],
               [(jax.ShapeDtypeStruct((T, D), F32), row), (jax.ShapeDtypeStruct((T, D), BF), row),
                (jax.ShapeDtypeStruct((1, D), F32), vec)],
               epi, comm=comm, ncol=max(1, D // COL_TILE), whole_tile_epilogue=True)


def _rope_tables(seq):
    half = LANES // 2
    inv = ROPE_THETA ** (-jnp.arange(0, half, dtype=F32) * 2.0 / LANES)
    ang = jnp.arange(seq, dtype=F32)[:, None] * inv[None, :]
    cos, sin = jnp.cos(ang), jnp.sin(ang)
    return jnp.concatenate([cos, cos], axis=1), jnp.concatenate([-sin, sin], axis=1)


def _branch_bias(nq, bq):
    d = (jnp.arange(nq)[:, None, None] * bq + jnp.arange(bq)[None, :, None]
         - jnp.arange(bq)[None, None, :])
    mult = jnp.zeros(d.shape, F32)
    for window, dil in DILATED_PATTERN:
        mult = mult + ((d >= 0) & (d % dil == 0) & (d <= window)).astype(F32)
    return jnp.where(mult > 0, jnp.log(jnp.maximum(mult, 1.0)), NEG_INF)


def _proj_fwd(u, wing, comm=None):
    T, D = u.shape
    ws = wing.shape[-1]
    bm = min(ROW_TILE, T)

    def epi(accs, ex, outs):
        outs[0][...] = accs[0]

    return _mm("proj_fwd", (N_DEV, T // bm, 1),
               [(u, pl.BlockSpec((bm, D), lambda s, i, k: (i, 0))),
                (wing, pl.BlockSpec((None, D, ws), lambda s, i, k: (s, 0, 0)))],
               [(0, 1, NN, 0)], 1, None, [],
               [(jax.ShapeDtypeStruct((T, N_DEV * ws), F32),
                 pl.BlockSpec((bm, ws), lambda s, i, k: (i, s)))], epi, comm=comm)[0]


def _rope_fwd(proj, cosf, sinf, seq, nh):
    T = proj.shape[0]
    bs = min(ROW_TILE, seq)
    nst = seq // bs
    scale = LANES ** -0.5

    def body(x_ref, c_ref, s_ref, o_ref):
        j = pl.program_id(1)
        t = x_ref[...]
        rot = t * c_ref[...] + pltpu.roll(t, LANES // 2, 1) * s_ref[...]
        rot = rot * jnp.where(j < nh, scale, 1.0)
        o_ref[...] = jnp.where(j < 2 * nh, rot, t).astype(BF)

    blk = pl.BlockSpec((bs, LANES), lambda r, j: (r, j))
    tab = pl.BlockSpec((bs, LANES), lambda r, j: (r % nst, 0))
    return pl.pallas_call(
        body, grid=(T // bs, 3 * nh), in_specs=[blk, tab, tab], out_specs=blk,
        out_shape=jax.ShapeDtypeStruct((T, 3 * nh * LANES), BF), name="rope_fwd",
        compiler_params=_cparams(2))(proj, cosf, sinf)


def _attn_fwd(qkv, bias, nb, seq, nh, comm=None):
    T = nb * seq
    bq = bias.shape[1]
    nq = seq // bq

    def body(q_ref, k_ref, v_ref, b_ref, o_ref, lse_ref):
        qi = pl.program_id(2)
        q = q_ref[...]

        def step(kj, carry):
            m, l, acc = carry
            rows = pl.ds(pl.multiple_of(kj * bq, bq), bq)
            s = _dot(q, k_ref[rows, :], NT) + b_ref[qi - kj]
            m_new = jnp.maximum(m, jnp.max(s, axis=1, keepdims=True))
            p = jnp.exp(s - m_new)
            alpha = jnp.exp(m - m_new)
            l = alpha * l + jnp.sum(p, axis=1, keepdims=True)
            acc = alpha * acc + _dot(p, v_ref[rows, :], NN)
            return m_new, l, acc

        init = (jnp.full((bq, 1), NEG_INF, F32), jnp.zeros((bq, 1), F32), jnp.zeros((bq, LANES), F32))
        m, l, acc = lax.fori_loop(0, qi + 1, step, init)
        o_ref[...] = (acc / l).astype(BF)
        lse_ref[...] = m + jnp.log(l)

    return _call(
        body, "attn_fwd", (nb, nh, nq),
        [pl.BlockSpec((bq, LANES), lambda b, h, i: (b * nq + i, h)),
         pl.BlockSpec((seq, LANES), lambda b, h, i: (b, nh + h)),
         pl.BlockSpec((seq, LANES), lambda b, h, i: (b, 2 * nh + h)),
         pl.BlockSpec((nq, bq, bq), lambda b, h, i: (0, 0, 0))],
        [pl.BlockSpec((bq, LANES), lambda b, h, i: (b * nq + i, h)),
         pl.BlockSpec((None, bq, 1), lambda b, h, i: (h, b * nq + i, 0))],
        [jax.ShapeDtypeStruct((T, 2 * nh * LANES), BF), jax.ShapeDtypeStruct((nh, T, 1), F32)],
        (qkv, qkv, qkv, bias), comm=comm)


def _attn_bwd_dq(qkv, cat, dcat, lse, bias, nb, seq, nh, comm=None):
    T = nb * seq
    bq = bias.shape[1]
    nq = seq // bq

    def body(q_ref, k_ref, v_ref, o_ref, do_ref, lse_ref, b_ref, dq_ref, delta_ref):
        qi = pl.program_id(2)
        q = q_ref[...]
        do = do_ref[...]
        dob = do.astype(BF)
        lse_t = lse_ref[...]
        delta = jnp.sum(do * o_ref[...].astype(F32), axis=1, keepdims=True)
        delta_ref[...] = delta

        def step(kj, dq):
            rows = pl.ds(pl.multiple_of(kj * bq, bq), bq)
            k = k_ref[rows, :]
            p = jnp.exp(_dot(q, k, NT) + b_ref[qi - kj] - lse_t)
            ds = p * (_dot(dob, v_ref[rows, :], NT) - delta)
            return dq + _dot(ds, k, NN)

        dq_ref[...] = lax.fori_loop(0, qi + 1, step, jnp.zeros((bq, LANES), F32))

    tile = pl.BlockSpec((bq, LANES), lambda b, h, i: (b * nq + i, h))
    stat = pl.BlockSpec((None, bq, 1), lambda b, h, i: (h, b * nq + i, 0))
    return _call(
        body, "attn_bwd_dq", (nb, nh, nq),
        [tile, pl.BlockSpec((seq, LANES), lambda b, h, i: (b, nh + h)),
         pl.BlockSpec((seq, LANES), lambda b, h, i: (b, 2 * nh + h)), tile, tile, stat,
         pl.BlockSpec((nq, bq, bq), lambda b, h, i: (0, 0, 0))],
        [tile, stat],
        [jax.ShapeDtypeStruct((T, nh * LANES), F32), jax.ShapeDtypeStruct((nh, T, 1), F32)],
        (qkv, qkv, qkv, cat, dcat, lse, bias), comm=comm)


def _attn_bwd_dkv(qkv, dcat, lse, delta, bias, nb, seq, nh):
    T = nb * seq
    bq = bias.shape[1]
    nq = seq // bq

    def body(k_ref, v_ref, q_ref, do_ref, lse_ref, delta_ref, b_ref, dk_ref, dv_ref):
        kj = pl.program_id(2)
        k = k_ref[...]
        v = v_ref[...]

        def step(qi, carry):
            dk, dv = carry
            rows = pl.ds(pl.multiple_of(qi * bq, bq), bq)
            q = q_ref[rows, :]
            dob = do_ref[rows, :].astype(BF)
            p = jnp.exp(_dot(q, k, NT) + b_ref[qi - kj] - lse_ref[rows, :])
            dv = dv + _dot(p, dob, TN)
            ds = p * (_dot(dob, v, NT) - delta_ref[rows, :])
            return dk + _dot(ds, q, TN), dv

        z = jnp.zeros((bq, LANES), F32)
        dk, dv = lax.fori_loop(kj, nq, step, (z, z))
        dk_ref[...] = dk
        dv_ref[...] = dv

    stat = pl.BlockSpec((None, seq, 1), lambda b, h, i: (h, b, 0))
    out = pl.BlockSpec((bq, LANES), lambda b, h, i: (b * nq + i, h))
    sh = jax.ShapeDtypeStruct((T, nh * LANES), F32)
    return pl.pallas_call(
        body, grid=(nb, nh, nq),
        in_specs=[pl.BlockSpec((bq, LANES), lambda b, h, i: (b * nq + i, nh + h)),
                  pl.BlockSpec((bq, LANES), lambda b, h, i: (b * nq + i, 2 * nh + h)),
                  pl.BlockSpec((seq, LANES), lambda b, h, i: (b, h)),
                  pl.BlockSpec((seq, LANES), lambda b, h, i: (b, h)),
                  stat, stat,
                  pl.BlockSpec((nq, bq, bq), lambda b, h, i: (0, 0, 0))],
        out_specs=[out, out], out_shape=[sh, sh],
        name="attn_bwd_dkv", compiler_params=_cparams(3))(qkv, qkv, qkv, dcat, lse, delta, bias)


def _conv_parts(gc, xin, w_ref):
    w = [w_ref[k:k + 1, :] for k in range(3)]
    u = gc * xin
    row = lax.broadcasted_iota(jnp.int32, u.shape, 0)
    u1 = jnp.where(row >= 1, pltpu.roll(u, 1, 0), 0.0)
    u2 = jnp.where(row >= 2, pltpu.roll(u, 2, 0), 0.0)
    return u, u1, u2, w[0] * u2 + w[1] * u1 + w[2] * u, w, row


def _conv_fwd(proj, conv_w, cat, nb, seq, width):
    cw = min(2 * LANES, width)
    nc = width // cw

    def body(gb_ref, gc_ref, x_ref, w_ref, cat_ref, o_ref):
        _, _, _, conv, _, _ = _conv_parts(gc_ref[...], x_ref[...], w_ref)
        o_ref[...] = (gb_ref[...] * conv).astype(BF)

    def sec(k):
        return pl.BlockSpec((seq, cw), lambda b, c: (b, k * nc + c))

    return pl.pallas_call(
        body, grid=(nb, nc),
        in_specs=[sec(3), sec(4), sec(5), pl.BlockSpec((3, cw), lambda b, c: (0, c)),
                  pl.BlockSpec(memory_space=pl.ANY)],
        out_specs=pl.BlockSpec((seq, cw), lambda b, c: (b, nc + c)),
        out_shape=jax.ShapeDtypeStruct(cat.shape, BF), input_output_aliases={4: 0},
        name="conv_fwd", compiler_params=_cparams(2))(proj, proj, proj, conv_w, cat)


def _conv_bwd(proj, conv_w, dcat, nb, seq, width):
    cw = min(2 * LANES, width)
    nc = width // cw
    T = nb * seq

    def body(gb_ref, gc_ref, x_ref, w_ref, d_ref, dgb_ref, dgc_ref, dx_ref, dw_ref):
        gc = gc_ref[...]
        xin = x_ref[...]
        u, u1, u2, conv, w, row = _conv_parts(gc, xin, w_ref)
        dsc = d_ref[...]
        dgb_ref[...] = dsc * conv
        dconv = dsc * gb_ref[...]
        d1 = jnp.where(row < seq - 1, pltpu.roll(dconv, seq - 1, 0), 0.0)
        d2 = jnp.where(row < seq - 2, pltpu.roll(dconv, seq - 2, 0), 0.0)
        du = w[2] * dconv + w[1] * d1 + w[0] * d2
        dgc_ref[...] = du * xin
        dx_ref[...] = du * gc

        @pl.when(pl.program_id(1) == 0)
        def _():
            dw_ref[...] = jnp.zeros_like(dw_ref)

        dw_ref[0:1, :] += jnp.sum(dconv * u2, axis=0, keepdims=True)
        dw_ref[1:2, :] += jnp.sum(dconv * u1, axis=0, keepdims=True)
        dw_ref[2:3, :] += jnp.sum(dconv * u, axis=0, keepdims=True)

    def sec(k):
        return pl.BlockSpec((seq, cw), lambda c, b: (b, k * nc + c))

    out = pl.BlockSpec((seq, cw), lambda c, b: (b, c))
    wsp = pl.BlockSpec((3, cw), lambda c, b: (0, c))
    sh = jax.ShapeDtypeStruct((T, width), F32)
    return pl.pallas_call(
        body, grid=(nc, nb), in_specs=[sec(3), sec(4), sec(5), wsp, sec(1)],
        out_specs=[out, out, out, wsp], out_shape=[sh, sh, sh, jax.ShapeDtypeStruct((3, width), F32)],
        name="conv_bwd", compiler_params=_cparams(2))(proj, proj, proj, conv_w, dcat)


def _assemble_dproj(dq, dk, dv, dgb, dgc, dxin, cosf, sinf, seq):
    T, width = dq.shape
    nh = width // LANES
    bs = min(256, seq)
    nst = seq // bs
    scale = LANES ** -0.5

    def body(dq_ref, dk_ref, dv_ref, dgb_ref, dgc_ref, dx_ref, c_ref, s_ref, o_ref):
        sec = pl.program_id(1)
        c = c_ref[...]
        s = s_ref[...]

        def unrope(ref, mul):
            for h in range(nh):
                cols = slice(h * LANES, (h + 1) * LANES)
                t = ref[:, cols]
                o_ref[:, cols] = ((t * c + pltpu.roll(t * s, LANES // 2, 1)) * mul).astype(BF)

        @pl.when(sec == 0)
        def _():
            unrope(dq_ref, scale)

        @pl.when(sec == 1)
        def _():
            unrope(dk_ref, 1.0)

        for k, ref in ((2, dv_ref), (3, dgb_ref), (4, dgc_ref), (5, dx_ref)):
            @pl.when(sec == k)
            def _(ref=ref):
                o_ref[...] = ref[...].astype(BF)

    blk = pl.BlockSpec((bs, width), lambda r, k: (r, 0))
    tab = pl.BlockSpec((bs, LANES), lambda r, k: (r % nst, 0))
    return pl.pallas_call(
        body, grid=(T // bs, 6), in_specs=[blk] * 6 + [tab, tab],
        out_specs=pl.BlockSpec((bs, width), lambda r, k: (r, k)),
        out_shape=jax.ShapeDtypeStruct((T, 6 * width), BF), name="assemble_dproj",
        compiler_params=_cparams(2))(dq, dk, dv, dgb, dgc, dxin, cosf, sinf)


def _res_mm(name, a, w, h, comm=None):
    T, K = a.shape
    N = w.shape[1]
    bm = min(ROW_TILE, T)
    bk = min(ROW_TILE, K)
    row = pl.BlockSpec((bm, N), lambda i, j, k: (i, 0))

    def epi(accs, ex, outs):
        outs[0][...] = ex[0][...] + accs[0]

    return _mm(name, (T // bm, 1, K // bk),
               [(a, pl.BlockSpec((bm, bk), lambda i, j, k: (i, k))),
                (w, pl.BlockSpec((bk, N), lambda i, j, k: (k, 0)))],
               [(0, 1, NN, 0)], 1, (bm, N), [(h, row)],
               [(jax.ShapeDtypeStruct((T, N), F32), row)], epi, comm=comm, ncol=max(1, N // COL_TILE))[0]


def _mm_nt(name, a, w, out_dtype):
    T, K = a.shape
    N = w.shape[0]
    bm = min(ROW_TILE, T)
    bn = min(ROW_TILE, N)

    def epi(accs, ex, outs):
        outs[0][...] = accs[0].astype(out_dtype)

    return _mm(name, (T // bm, N // bn, 1),
               [(a, pl.BlockSpec((bm, K), lambda i, j, k: (i, 0))),
                (w, pl.BlockSpec((bn, K), lambda i, j, k: (j, 0)))],
               [(0, 1, NT, 0)], 1, None, [],
               [(jax.ShapeDtypeStruct((T, N), out_dtype), pl.BlockSpec((bm, bn), lambda i, j, k: (i, j)))],
               epi)[0]


def _mm_tn(name, a, bs_list):
    T, M = a.shape
    N = bs_list[0].shape[1]
    bmr = min(COL_TILE, M)
    bn = min(COL_TILE, N)
    n = len(bs_list)

    def epi(accs, ex, outs):
        for q in range(n):
            outs[q][...] = accs[q].astype(BF)

    ops = [(a, pl.BlockSpec((T, bmr), lambda r, j, k: (0, r)))]
    ops += [(b, pl.BlockSpec((T, bn), lambda r, j, k: (0, j))) for b in bs_list]
    return _mm(name, (M // bmr, N // bn, 1), ops, [(0, 1 + q, TN, q) for q in range(n)], n, None, [],
               [(jax.ShapeDtypeStruct((M, N), BF), pl.BlockSpec((bmr, bn), lambda r, j, k: (r, j)))] * n, epi)


def _proj_bwd_x(dproj, wing):
    T = dproj.shape[0]
    _, D, ws = wing.shape
    bm = min(ROW_TILE, T)
    row = pl.BlockSpec((bm, D), lambda i, j, s: (i, 0))

    def epi(accs, ex, outs):
        outs[0][...] = accs[0]

    return _mm("proj_bwd_x", (T // bm, 1, N_DEV),
               [(dproj, pl.BlockSpec((bm, ws), lambda i, j, s: (i, s))),
                (wing, pl.BlockSpec((None, D, ws), lambda i, j, s: (s, 0, 0)))],
               [(0, 1, NT, 0)], 1, (bm, D), [], [(jax.ShapeDtypeStruct((T, D), F32), row)], epi,
               ncol=max(1, D // COL_TILE))[0]


def _proj_dw(u, dproj, ws):
    T, D = u.shape
    bmr = min(COL_TILE, D)

    def epi(accs, ex, outs):
        outs[0][...] = accs[0].astype(BF)

    return _mm("proj_dw", (N_DEV, D // bmr, 1),
               [(u, pl.BlockSpec((T, bmr), lambda s, r, k: (0, r))),
                (dproj, pl.BlockSpec((T, ws), lambda s, r, k: (0, s)))],
               [(0, 1, TN, 0)], 1, None, [],
               [(jax.ShapeDtypeStruct((N_DEV, D, ws), BF),
                 pl.BlockSpec((None, bmr, ws), lambda s, r, k: (s, r, 0)))], epi)[0]


def _mixer_ab_fwd(h, gain, wing, conv_w, wout, tabs, nb, seq, comm_proj=None, comm_attn=None, comm_out=None):
    cosf, sinf, bias = tabs
    width = wing.shape[-1] * N_DEV // 6
    nh = width // LANES
    u = _rms_fwd(h, gain, BF)
    proj = _proj_fwd(u, wing, comm=comm_proj)
    qkv = _rope_fwd(proj, cosf, sinf, seq, nh)
    cat, lse = _attn_fwd(qkv, bias, nb, seq, nh, comm=comm_attn)
    cat = _conv_fwd(proj, conv_w, cat, nb, seq, width)
    return _res_mm("outproj_fwd", cat, wout, h, comm=comm_out), (h, u, proj, qkv, cat, lse)


def _mixer_ab_bwd(dh, dhb, saved, gain, wing, conv_w, wout, tabs, nb, seq, reduce_start, carry):
    cosf, sinf, bias = tabs
    h, u, proj, qkv, cat, lse = saved
    D = h.shape[1]
    ws = wing.shape[-1]
    width = ws * N_DEV // 6
    nh = width // LANES
    dcat = _mm_nt("outproj_bwd_x", dhb, wout, F32)
    dwout = _mm_tn("outproj_dw", cat, [dhb])[0]
    comm = _merge_comms(reduce_start(["ab_w_out"], [dwout.reshape(N_DEV, -1, D)]) + [carry])
    dq, delta = _attn_bwd_dq(qkv, cat, dcat, lse, bias, nb, seq, nh, comm=comm)
    dk, dv = _attn_bwd_dkv(qkv, dcat, lse, delta, bias, nb, seq, nh)
    dgb, dgc, dxin, dconvw = _conv_bwd(proj, conv_w, dcat, nb, seq, width)
    dproj = _assemble_dproj(dq, dk, dv, dgb, dgc, dxin, cosf, sinf, seq)
    du = _proj_bwd_x(dproj, wing)
    comm, = reduce_start(["ab_w_in"], [_proj_dw(u, dproj, ws)])
    dh_in, dhb_in, dgain = _rms_bwd(du, h, gain, dh)
    return dh_in, dhb_in, dgain, dconvw, comm


def _s5_zoh(lr, li, log_dt):
    dt = jnp.exp(log_dt)
    mag = jnp.exp(lr * dt)
    ar = mag * jnp.cos(li * dt)
    ai = mag * jnp.sin(li * dt)
    den = lr * lr + li * li
    return dt, ar, ai, den, ((ar - 1.0) * lr + ai * li) / den, (ai * lr - (ar - 1.0) * li) / den


def _s5_discretize(lam_re, lam_im, log_dt, bt_re, bt_im):
    def body(lr_ref, li_ref, ld_ref, br_ref, bi_ref, ar_ref, ai_ref, bbr_ref, bbi_ref):
        _, ar, ai, _, fr, fi = _s5_zoh(lr_ref[...], li_ref[...], ld_ref[...])
        ar_ref[...] = ar
        ai_ref[...] = ai
        bbr_ref[...] = fr * br_ref[...] - fi * bi_ref[...]
        bbi_ref[...] = fr * bi_ref[...] + fi * br_ref[...]

    small = jax.ShapeDtypeStruct(lam_re.shape, F32)
    big = jax.ShapeDtypeStruct(bt_re.shape, F32)
    return pl.pallas_call(body, out_shape=[small, small, big, big], name="s5_discretize",
                          compiler_params=_cparams(0))(lam_re, lam_im, log_dt, bt_re, bt_im)


def _s5_discretize_bwd(lam_re, lam_im, log_dt, bt_re, bt_im, d_ar, d_ai, d_bbr, d_bbi):

    def body(lr_ref, li_ref, ld_ref, br_ref, bi_ref, dar_ref, dai_ref, dbbr_ref, dbbi_ref,
             dlr_ref, dli_ref, dld_ref, dbr_ref, dbi_ref):
        lr, li = lr_ref[...], li_ref[...]
        dt, ar, ai, den, fr, fi = _s5_zoh(lr, li, ld_ref[...])
        br, bi = br_ref[...], bi_ref[...]
        dbbr, dbbi = dbbr_ref[...], dbbi_ref[...]
        dbr_ref[...] = dbbr * fr + dbbi * fi
        dbi_ref[...] = dbbi * fr - dbbr * fi
        dfr = jnp.sum(dbbr * br + dbbi * bi, axis=1, keepdims=True)
        dfi = jnp.sum(dbbi * br - dbbr * bi, axis=1, keepdims=True)
        dnr = dfr / den
        dni = dfi / den
        dden = -(dfr * fr + dfi * fi) / den
        dar = dar_ref[...] + dnr * lr - dni * li
        dai = dai_ref[...] + dnr * li + dni * lr
        dlr_ref[...] = dnr * (ar - 1.0) + dni * ai + 2.0 * dden * lr + dt * (dar * ar + dai * ai)
        dli_ref[...] = dnr * ai - dni * (ar - 1.0) + 2.0 * dden * li + dt * (dai * ar - dar * ai)
        ddt = jnp.sum(dar * (lr * ar - li * ai) + dai * (lr * ai + li * ar), axis=2, keepdims=True)
        dld_ref[...] = ddt * dt

    small = jax.ShapeDtypeStruct(lam_re.shape, F32)
    big = jax.ShapeDtypeStruct(bt_re.shape, F32)
    return pl.pallas_call(
        body, out_shape=[small, small, jax.ShapeDtypeStruct(log_dt.shape, F32), big, big],
        name="s5_discretize_bwd", compiler_params=_cparams(0))(
            lam_re, lam_im, log_dt, bt_re, bt_im, d_ar, d_ai, d_bbr, d_bbi)


def _rows8(t):
    return pl.ds(pl.multiple_of(t * SUBLANES, SUBLANES), SUBLANES)


def _cmul_add(ar, ai, sr, si, br, bi):
    return ar * sr - ai * si + br, ar * si + ai * sr + bi


def _scan(a, read, write, init, n):
    def step(t, c):
        s = _cmul_add(*a, *c, *read(t))
        if write is not None:
            write(t, s)
        return s

    return lax.fori_loop(0, n, step, init, unroll=SCAN_UNROLL)


def _cpow(ar, ai, n):
    rr = ri = None
    while n:
        if n & 1:
            rr, ri = (ar, ai) if rr is None else (rr * ar - ri * ai, rr * ai + ri * ar)
        ar, ai = ar * ar - ai * ai, 2.0 * ar * ai
        n >>= 1
    return rr, ri


def _s5_specs(R, nj):
    sh = STATE_COLS
    return dict(
        rows=pl.BlockSpec((R, LANES), lambda j: (0, j)),
        bd=pl.BlockSpec((None, LANES, sh), lambda j: (j, 0, 0)),
        cd=pl.BlockSpec((None, sh, LANES), lambda j: (j, 0, 0)),
        a=pl.BlockSpec((None, 1, sh), lambda j: (j, 0, 0)),
        vec=pl.BlockSpec((1, LANES), lambda j: (0, j)),
        init=pl.BlockSpec((None, SUBLANES, sh), lambda j: (j, 0, 0)))


def _s5_fwd(u, mats, seg_len, nseg, comm=None):
    bdr, bdi, cdr, cdi, are, aim, dsk = mats
    R, D = u.shape
    nj = D // LANES
    sh = STATE_COLS
    rc = min(R, 512)
    sp = _s5_specs(R, nj)

    def body(u_ref, bdr_ref, bdi_ref, cdr_ref, cdi_ref, ar_ref, ai_ref, d_ref,
             y_ref, yg_ref, ir_ref, ii_ref, sre, sim):
        ar = jnp.broadcast_to(ar_ref[...], (SUBLANES, sh))
        ai = jnp.broadcast_to(ai_ref[...], (SUBLANES, sh))

        def bu_chunk(c, _):
            rows = pl.ds(pl.multiple_of(c * rc, rc), rc)
            ub = u_ref[rows, :].astype(BF)
            sre[rows, :] = _dot(ub, bdr_ref[...], NN)
            sim[rows, :] = _dot(ub, bdi_ref[...], NN)
            return 0

        lax.fori_loop(0, R // rc, bu_chunk, 0)
        z = jnp.zeros((SUBLANES, sh), F32)

        def read(t):
            return sre[_rows8(t), :], sim[_rows8(t), :]

        def write(t, s):
            sre[_rows8(t), :] = s[0]
            sim[_rows8(t), :] = s[1]

        er, ei = _scan((ar, ai), read, None, (z, z), seg_len)
        pr, pi = _cpow(ar, ai, seg_len)
        first = (lax.broadcasted_iota(jnp.int32, (SUBLANES, sh), 0) & (nseg - 1)) == 0

        def prev(x):
            return jnp.where(first, 0.0, pltpu.roll(x, 1, 0))

        xr, xi = er, ei
        for _ in range(nseg - 1):
            xr, xi = _cmul_add(pr, pi, prev(xr), prev(xi), er, ei)
        i_r, i_i = prev(xr), prev(xi)
        ir_ref[...] = i_r
        ii_ref[...] = i_i
        _scan((ar, ai), read, write, (i_r, i_i), seg_len)

        def y_chunk(c, _):
            rows = pl.ds(pl.multiple_of(c * rc, rc), rc)
            y = _dot(sre[rows, :], cdr_ref[...], NN) + _dot(sim[rows, :], cdi_ref[...], NN)
            y = y + d_ref[...] * u_ref[rows, :]
            y_ref[rows, :] = y
            yg_ref[rows, :] = _gelu(y).astype(BF)
            return 0

        lax.fori_loop(0, R // rc, y_chunk, 0)

    init_sh = jax.ShapeDtypeStruct((nj, SUBLANES, STATE_COLS), F32)
    return _call(
        body, "s5_fwd", (nj,),
        [sp["rows"], sp["bd"], sp["bd"], sp["cd"], sp["cd"], sp["a"], sp["a"], sp["vec"]],
        [sp["rows"], sp["rows"], sp["init"], sp["init"]],
        [jax.ShapeDtypeStruct((R, D), F32), jax.ShapeDtypeStruct((R, D), BF), init_sh, init_sh],
        (u, bdr, bdi, cdr, cdi, are, aim, dsk),
        scratch=[pltpu.VMEM((R, sh), F32) for _ in range(2)], comm=comm)


def _s5_bwd(u, dy, mats, init_re, init_im, seg_len, nseg, comm=None):
    bdr, bdi, cdr, cdi, are, aim, dsk = mats
    R, D = u.shape
    nj = D // LANES
    sh = STATE_COLS
    rc = min(R, 512)
    sp = _s5_specs(R, nj)

    def body(u_ref, dy_ref, bdr_ref, bdi_ref, cdr_ref, cdi_ref, ar_ref, ai_ref, d_ref, ir_ref, ii_ref,
             du_ref, dbdr_ref, dbdi_ref, dcdr_ref, dcdi_ref, dar_ref, dai_ref, dd_ref,
             sre, sim, gre, gim):
        ar = jnp.broadcast_to(ar_ref[...], (SUBLANES, sh))
        ai = jnp.broadcast_to(ai_ref[...], (SUBLANES, sh))
        i_r, i_i = ir_ref[...], ii_ref[...]

        def chunk(c):
            return pl.ds(pl.multiple_of(c * rc, rc), rc)

        def bu_chunk(c, _):
            ub = u_ref[chunk(c), :].astype(BF)
            dyb = dy_ref[chunk(c), :].astype(BF)
            sre[chunk(c), :] = _dot(ub, bdr_ref[...], NN)
            sim[chunk(c), :] = _dot(ub, bdi_ref[...], NN)
            gre[chunk(c), :] = _dot(dyb, cdr_ref[...], NT)
            gim[chunk(c), :] = _dot(dyb, cdi_ref[...], NT)
            return 0

        lax.fori_loop(0, R // rc, bu_chunk, 0)

        def read_s(t):
            return sre[_rows8(t), :], sim[_rows8(t), :]

        def read_g(t):
            return gre[_rows8(t), :], gim[_rows8(t), :]

        def both(i, c):
            s = _cmul_add(ar, ai, c[0], c[1], *read_s(i))
            sre[_rows8(i), :], sim[_rows8(i), :] = s
            return (*s, *_cmul_add(ar, -ai, c[2], c[3], *read_g(seg_len - 1 - i)))

        z = jnp.zeros((SUBLANES, sh), F32)
        _, _, fr, fi = lax.fori_loop(0, seg_len, both, (i_r, i_i, z, z), unroll=SCAN_UNROLL)

        def c_chunk(c, carry):
            dyb = dy_ref[chunk(c), :].astype(BF)
            return (carry[0] + _dot(sre[chunk(c), :], dyb, TN), carry[1] + _dot(sim[chunk(c), :], dyb, TN))

        zc = jnp.zeros((sh, LANES), F32)
        dcr, dci = lax.fori_loop(0, R // rc, c_chunk, (zc, zc))
        dcdr_ref[...] = dcr
        dcdi_ref[...] = dci
        pr, pi = _cpow(ar, ai, seg_len)
        last =(lax.broadcasted_iota(jnp.int32, (SUBLANES, sh), 0) & (nseg - 1)) == nseg - 1

        def nxt(x):
            return jnp.where(last, 0.0, pltpu.roll(x, SUBLANES - 1, 0))

        xr, xi = fr, fi
        for _ in range(nseg - 1):
            xr, xi = _cmul_add(pr, -pi, nxt(xr), nxt(xi), fr, fi)
        g0r, g0i = nxt(xr), nxt(xi)

        def adj_step(t, c, s_before):
            gr, gi = _cmul_add(ar, -ai, c[0], c[1], *read_g(t))
            gre[_rows8(t), :], gim[_rows8(t), :] = gr, gi
            spr, spi = s_before
            return gr, gi, c[2] + spr * gr + spi * gi, c[3] + spr * gi - spi * gr

        carry = lax.fori_loop(0, seg_len - 1, lambda i, c: adj_step(seg_len - 1 - i, c, read_s(seg_len - 2 - i)),
                              (g0r, g0i, z, z))
        carry = adj_step(0, carry, (i_r, i_i))
        dar_ref[...] = jnp.sum(carry[2], axis=0, keepdims=True)
        dai_ref[...] = jnp.sum(carry[3], axis=0, keepdims=True)

        def d_chunk(c, carry):
            ub = u_ref[chunk(c), :].astype(BF)
            grb = gre[chunk(c), :].astype(BF)
            gib = gim[chunk(c), :].astype(BF)
            du = _dot(grb, bdr_ref[...], NT) + _dot(gib, bdi_ref[...], NT)
            du_ref[chunk(c), :] = du + d_ref[...] * dy_ref[chunk(c), :]
            dd = carry[2] + jnp.sum(dy_ref[chunk(c), :] * u_ref[chunk(c), :], axis=0, keepdims=True)
            return carry[0] + _dot(ub, grb, TN), carry[1] + _dot(ub, gib, TN), dd

        zb = jnp.zeros((LANES, sh), F32)
        dbr, dbi, dd = lax.fori_loop(0, R // rc, d_chunk, (zb, zb, jnp.zeros((1, LANES), F32)))
        dbdr_ref[...] = dbr
        dbdi_ref[...] = dbi
        dd_ref[...] = dd

    bd_sh = jax.ShapeDtypeStruct((nj, LANES, STATE_COLS), F32)
    cd_sh = jax.ShapeDtypeStruct((nj, STATE_COLS, LANES), F32)
    a_sh = jax.ShapeDtypeStruct((nj, 1, STATE_COLS), F32)
    return _call(
        body, "s5_bwd", (nj,),
        [sp["rows"], sp["rows"], sp["bd"], sp["bd"], sp["cd"], sp["cd"], sp["a"], sp["a"],
         sp["vec"], sp["init"], sp["init"]],
        [sp["rows"], sp["bd"], sp["bd"], sp["cd"], sp["cd"], sp["a"], sp["a"], sp["vec"]],
        [jax.ShapeDtypeStruct((R, D), F32), bd_sh, bd_sh, cd_sh, cd_sh, a_sh, a_sh,
         jax.ShapeDtypeStruct((1, D), F32)],
        (u, dy, bdr, bdi, cdr, cdi, are, aim, dsk, init_re, init_im),
        scratch=[pltpu.VMEM((R, sh), F32) for _ in range(4)], comm=comm)


def _glu_fwd(yg, wa, wb, h):
    T, D = yg.shape
    N = wa.shape[1]
    bm = min(ROW_TILE, T)
    bn = min(ROW_TILE, N)
    wspec = pl.BlockSpec((D, bn), lambda i, j, k: (0, j))
    ospec = pl.BlockSpec((bm, bn), lambda i, j, k: (i, j))

    def epi(accs, ex, outs):
        pa, pb = accs
        outs[0][...] = ex[0][...] + pa * _sig(pb)
        outs[1][...] = pa.astype(BF)
        outs[2][...] = pb.astype(BF)

    return _mm("glu_fwd", (T // bm, N // bn, 1),
               [(yg, pl.BlockSpec((bm, D), lambda i, j, k: (i, 0))), (wa, wspec), (wb, wspec)],
               [(0, 1, NN, 0), (0, 2, NN, 1)], 2, None, [(h, ospec)],
               [(jax.ShapeDtypeStruct((T, N), F32), ospec), (jax.ShapeDtypeStruct((T, N), BF), ospec),
                (jax.ShapeDtypeStruct((T, N), BF), ospec)], epi)


def _glu_bwd_gates(dz, pa, pb):
    T, D = dz.shape
    bm = min(ROW_TILE, T)

    def body(dz_ref, pa_ref, pb_ref, dpa_ref, dpb_ref):
        dz = dz_ref[...]
        sg = _sig(pb_ref[...].astype(F32))
        dpa_ref[...] = (dz * sg).astype(BF)
        dpb_ref[...] = (dz * pa_ref[...].astype(F32) * sg * (1.0 - sg)).astype(BF)

    row = pl.BlockSpec((bm, D), lambda i: (i, 0))
    return pl.pallas_call(
        body, grid=(T // bm,), in_specs=[row] * 3, out_specs=[row] * 2,
        out_shape=[jax.ShapeDtypeStruct((T, D), BF)] * 2, name="glu_bwd_gates",
        compiler_params=_cparams(1))(dz, pa, pb)


def _glu_bwd_y(dpa, dpb, wa, wb, y_pre, comm=None):
    T, N = dpa.shape
    D = wa.shape[0]
    bm = min(ROW_TILE, T)
    bn = min(ROW_TILE, D)
    aspec = pl.BlockSpec((bm, N), lambda i, j, k: (i, 0))
    wspec = pl.BlockSpec((bn, N), lambda i, j, k: (j, 0))
    ospec = pl.BlockSpec((bm, bn), lambda i, j, k: (i, j))

    def epi(accs, ex, outs):
        outs[0][...] = accs[0] * _gelu_grad(ex[0][...])

    return _mm("glu_bwd_y", (T // bm, D // bn, 1), [(dpa, aspec), (wa, wspec), (dpb, aspec), (wb, wspec)],
               [(0, 1, NT, 0), (2, 3, NT, 0)], 1, None, [(y_pre, ospec)],
               [(jax.ShapeDtypeStruct((T, D), F32), ospec)], epi, comm=comm)[0]


def _block_diag_in(x, nj):
    g = GROUPS_PER_BLOCK
    x = x.reshape(nj, g, 1, S5_GROUP, S5_STATE)
    eye = jnp.eye(g, dtype=bool)[None, :, :, None, None]
    full = jnp.where(eye, x, 0.0)
    return full.transpose(0, 1, 3, 2, 4).reshape(nj, g * S5_GROUP, g * S5_STATE)


def _block_diag_out(x, nj):
    return _block_diag_in(x, nj).transpose(0, 2, 1)


def _diag_of_in(m, nj):
    g = GROUPS_PER_BLOCK
    m5 = m.reshape(nj, g, S5_GROUP, g, S5_STATE)
    d = jnp.diagonal(m5, axis1=1, axis2=3)
    return d.transpose(0, 3, 1, 2).reshape(nj * g, S5_GROUP, S5_STATE)


def _mixer_s5_fwd(h, gain, p, dsk, wa, wb, nb, seq, comm_s5=None):
    T, D = h.shape
    nj = D // LANES
    nseg = SUBLANES // nb
    seg_len = seq // nseg
    G = p["s5_lambda_re"].shape[1]
    lam_re = p["s5_lambda_re"].reshape(G, 1, S5_STATE)
    lam_im = p["s5_lambda_im"].reshape(G, 1, S5_STATE)
    log_dt = p["s5_log_dt"].reshape(G, 1, 1)
    bt_re = p["s5_b_re"][0].transpose(0, 2, 1)
    bt_im = p["s5_b_im"][0].transpose(0, 2, 1)
    ar, ai, bbr, bbi = _s5_discretize(lam_re, lam_im, log_dt, bt_re, bt_im)
    mats = (_block_diag_in(bbr, nj).astype(BF), _block_diag_in(bbi, nj).astype(BF),
            _block_diag_out(p["s5_c_re"][0], nj).astype(BF),
            _block_diag_out(-p["s5_c_im"][0], nj).astype(BF),
            ar.reshape(nj, 1, STATE_COLS), ai.reshape(nj, 1, STATE_COLS), dsk)
    h_seg = _to_seg(h, seg_len)
    u = _rms_fwd(h_seg, gain, F32)
    y_pre, yg, init_re, init_im = _s5_fwd(u, mats, seg_len, nseg, comm=comm_s5)
    h_out, pa, pb = _glu_fwd(yg, wa, wb, h_seg)
    disc_in = (lam_re, lam_im, log_dt, bt_re, bt_im)
    return _to_tok(h_out, seg_len), (h_seg, u, mats, y_pre, yg, init_re, init_im, pa, pb, disc_in, seg_len, nseg)


def _mixer_s5_bwd(dh, saved, gain, wa, wb, reduce_start, carry):
    h_seg, u, mats, y_pre, yg, init_re, init_im, pa, pb, disc_in, seg_len, nseg = saved
    T, D = h_seg.shape
    nj = D // LANES
    G = nj * GROUPS_PER_BLOCK
    dh_seg = _to_seg(dh, seg_len)
    dpa, dpb = _glu_bwd_gates(dh_seg, pa, pb)
    dy = _glu_bwd_y(dpa, dpb, wa, wb, y_pre)
    dwa, dwb = _mm_tn("glu_dw", yg, [dpa, dpb])
    comm = _merge_comms(reduce_start(["s5_glu_wa", "s5_glu_wb"],
                                     [dwa.reshape(N_DEV, -1, D), dwb.reshape(N_DEV, -1, D)]) + [carry])
    du, dbdr, dbdi, dcdr, dcdi, dar, dai, dd = _s5_bwd(u, dy, mats, init_re, init_im, seg_len, nseg, comm=comm)
    d_bbr = _diag_of_in(dbdr, nj)
    d_bbi = _diag_of_in(dbdi, nj)
    d_c_re = _diag_of_in(dcdr.transpose(0, 2, 1), nj)
    d_c_im = -_diag_of_in(dcdi.transpose(0, 2, 1), nj)
    dlr, dli, dld, dbr, dbi = _s5_discretize_bwd(
        *disc_in, dar.reshape(G, 1, S5_STATE), dai.reshape(G, 1, S5_STATE), d_bbr, d_bbi)
    small = {"s5_lambda_re": dlr.reshape(1, G, S5_STATE), "s5_lambda_im": dli.reshape(1, G, S5_STATE),
             "s5_log_dt": dld.reshape(1, G),
             "s5_b_re": dbr.transpose(0, 2, 1)[None], "s5_b_im": dbi.transpose(0, 2, 1)[None],
             "s5_c_re": d_c_re[None], "s5_c_im": d_c_im[None], "s5_d": dd}
    dh_in, _, dgain = _rms_bwd(du, h_seg, gain, dh_seg)
    dh_in = _to_tok(dh_in, seg_len)
    return dh_in, dh_in.astype(BF), dgain, small


def _mesh_pos():
    return lax.axis_index("x"), lax.axis_index("y"), lax.axis_index("c")


class _Gather:
    def __init__(self, srcs, slots, send_sems, recv_sems):
        self.srcs, self.slots, self.send_sems, self.recv_sems = srcs, slots, send_sems, recv_sems
        x, y, c = _mesh_pos()
        self.c = c
        self.me, self.sib = (x, y, c), (x, y, 1 - c)
        self.chips = [(1 - x, y), (x, 1 - y), (1 - x, 1 - y)]

    def copy(self, a, k, block, to, own=False):
        dst = self.slots[a].at[4 * block[0] + 2 * block[1] + block[2]]
        return pltpu.make_async_remote_copy(
            src_ref=self.srcs[a] if own else dst, dst_ref=dst, send_sem=self.send_sems.at[7 * a + k],
            recv_sem=self.recv_sems.at[7 * a + k], device_id=to, device_id_type=MESH)

    def own_copies(self, a):
        cps = [self.copy(a, 0, self.me, self.sib, own=True)]
        return cps + [self.copy(a, 1 + j, self.me, (*chip, self.c), own=True) for j, chip in enumerate(self.chips)]

    def start(self):
        for a in range(len(self.srcs)):
            for cp in self.own_copies(a):
                cp.start()

    def finish(self):
        n = len(self.srcs)
        for a in range(n):
            for j, chip in enumerate(self.chips):
                self.copy(a, 1 + j, (*chip, self.c), self.me).wait_recv()
                self.copy(a, 4 + j, (*chip, self.c), self.sib).start()
        for a in range(n):
            self.copy(a, 0, self.sib, self.me).wait_recv()
            for j, chip in enumerate(self.chips):
                self.copy(a, 4 + j, (*chip, 1 - self.c), self.me).wait_recv()
        for a in range(n):
            for cp in self.own_copies(a):
                cp.wait_send()
            for j, chip in enumerate(self.chips):
                self.copy(a, 4 + j, (*chip, self.c), self.sib).wait_send()


def _gather_comm(arrs):
    n = len(arrs)

    def local(xs, outs, sems, a):
        x, y, c = _mesh_pos()
        return pltpu.make_async_copy(xs[a], outs[a].at[4 * x + 2 * y + c], sems[2].at[a])

    def start(xs, outs, sems):
        for a in range(n):
            local(xs, outs, sems, a).start()
        _Gather(xs, outs, sems[0], sems[1]).start()

    def finish(xs, outs, sems):
        _Gather(xs, outs, sems[0], sems[1]).finish()
        for a in range(n):
            local(xs, outs, sems, a).wait()

    return _Comm(list(arrs), [jax.ShapeDtypeStruct((N_DEV,) + a.shape, a.dtype) for a in arrs],
                 [pltpu.SemaphoreType.DMA((7 * n,)), pltpu.SemaphoreType.DMA((7 * n,)),
                  pltpu.SemaphoreType.DMA((n,))], start, finish)


def _exchange_comm(parts):
    n = len(parts)

    def copies(ps, outs, sems):
        x, y, c = _mesh_pos()
        cps = []
        for a in range(n):
            for j in range(1, 4):
                to = (jnp.bitwise_xor(x, j // 2), jnp.bitwise_xor(y, j % 2), c)
                cps.append(pltpu.make_async_remote_copy(
                    src_ref=ps[a].at[j], dst_ref=outs[a].at[j - 1], send_sem=sems[0].at[3 * a + j - 1],
                    recv_sem=sems[1].at[3 * a + j - 1], device_id=to, device_id_type=MESH))
        return cps

    def start(ps, outs, sems):
        for cp in copies(ps, outs, sems):
            cp.start()

    def finish(ps, outs, sems):
        for cp in copies(ps, outs, sems):
            cp.wait()

    return _Comm(list(parts), [jax.ShapeDtypeStruct((3,) + p.shape[1:], p.dtype) for p in parts],
                 [pltpu.SemaphoreType.DMA((3 * n,)), pltpu.SemaphoreType.DMA((3 * n,))], start, finish)


def _run_comm(comm, name):
    ci, co = len(comm.ins), len(comm.outs)

    def body(*refs):
        comm.start(refs[:ci], refs[ci:ci + co], refs[ci + co:])
        comm.finish(refs[:ci], refs[ci:ci + co], refs[ci + co:])

    any_spec = pl.BlockSpec(memory_space=pl.ANY)
    comm.set_results(pl.pallas_call(
        body, in_specs=[any_spec] * ci, out_specs=[any_spec] * co, out_shape=list(comm.outs),
        scratch_shapes=list(comm.sems), name=name, compiler_params=_cparams(0))(*comm.ins))


def _pair_exchange(grads, name):
    n = len(grads)

    def body(*refs):
        gs, outs = refs[:n], refs[n:2 * n]
        send_sems, recv_sems = refs[2 * n:]
        x, y, c = _mesh_pos()
        copies = []
        for a in range(n):
            for k in range(4):
                copies.append(pltpu.make_async_remote_copy(
                    src_ref=gs[a].at[2 * k + 1 - c], dst_ref=outs[a].at[k], send_sem=send_sems.at[4 * a + k],
                    recv_sem=recv_sems.at[4 * a + k], device_id=(x, y, 1 - c), device_id_type=MESH))
        for cp in copies:
            cp.start()
        for cp in copies:
            cp.wait()

    any_spec = pl.BlockSpec(memory_space=pl.ANY)
    return pl.pallas_call(
        body, in_specs=[any_spec] * n, out_specs=[any_spec] * n,
        out_shape=[jax.ShapeDtypeStruct((4,) + g.shape[1:], g.dtype) for g in grads],
        scratch_shapes=[pltpu.SemaphoreType.DMA((4 * n,)), pltpu.SemaphoreType.DMA((4 * n,))],
        name=name, compiler_params=_cparams(0))(*grads)


def _pair_sum(grad, recv, pos):
    _, R, C = grad.shape
    br = _row_block(R, C, PAIR_SUM_ELEMS)

    def body(pos_ref, g_ref, r_ref, o_ref):
        o_ref[...] = (g_ref[...].astype(F32) + r_ref[...].astype(F32)).astype(BF)

    def chip(j, p):
        return jnp.bitwise_xor(p[1], j)

    return pl.pallas_call(
        body, grid_spec=pltpu.PrefetchScalarGridSpec(
            num_scalar_prefetch=1, grid=(4, R // br),
            in_specs=[pl.BlockSpec((None, br, C), lambda j, i, p: (2 * chip(j, p) + p[0], i, 0)),
                      pl.BlockSpec((None, br, C), lambda j, i, p: (chip(j, p), i, 0))],
            out_specs=pl.BlockSpec((None, br, C), lambda j, i, p: (j, i, 0))),
        out_shape=jax.ShapeDtypeStruct((4, R, C), BF), name="pair_sum", compiler_params=_cparams(2))(pos, grad, recv)


def _adamw(w, g, m, v):
    m = ADAM_B1 * m + (1.0 - ADAM_B1) * g
    v = ADAM_B2 * v + (1.0 - ADAM_B2) * (g * g)
    m_hat = m / (1.0 - ADAM_B1 ** ADAM_STEP)
    v_hat = v / (1.0 - ADAM_B2 ** ADAM_STEP)
    return -ADAM_LR * (m_hat / (jnp.sqrt(v_hat) + ADAM_EPS) + ADAM_WD * w), m, v


def _adamw_piece(w, m, v, piece, part, recv, bufs):
    _, R, C = w.shape
    br = _row_block(R, C)

    def body(w_ref, m_ref, v_ref, p_ref, r_ref, b0, b1, b2, b3, g_ref, d_ref, nm_ref, nv_ref):
        g = p_ref[...].astype(F32)
        for j in range(3):
            g = g + r_ref[j].astype(F32)
        d, nm, nv = _adamw(w_ref[...], g, m_ref[...], v_ref[...])
        g_ref[...] = g
        d_ref[...] = d
        nm_ref[...] = nm
        nv_ref[...] = nv

    row = pl.BlockSpec((None, br, C), lambda i: (piece, i, 0))
    any_spec = pl.BlockSpec(memory_space=pl.ANY)
    return pl.pallas_call(
        body, grid=(R // br,),
        in_specs=[row, row, row, pl.BlockSpec((None, br, C), lambda i: (0, i, 0)),
                  pl.BlockSpec((3, br, C), lambda i: (0, i, 0))] + [any_spec] * 4,
        out_specs=[row] * 4, out_shape=[jax.ShapeDtypeStruct(w.shape, F32)] * 4,
        input_output_aliases={5: 0, 6: 1, 7: 2, 8: 3}, name="adamw_piece",
        compiler_params=_cparams(1))(w, m, v, part, recv, *bufs)


def _all_reduce_small(x):
    rows = x.shape[0]

    def body(x_ref, o_ref, buf, send_sems, recv_sems):
        xp, yp, cp = _mesh_pos()
        buf[4 * xp + 2 * yp + cp] = x_ref[...]
        gather = _Gather([x_ref], [buf], send_sems, recv_sems)
        gather.start()
        gather.finish()
        acc = buf[0]
        for d in range(1, N_DEV):
            acc = acc + buf[d]
        o_ref[...] = acc

    vm = pl.BlockSpec(memory_space=pltpu.VMEM)
    return pl.pallas_call(
        body, in_specs=[vm], out_specs=vm, out_shape=jax.ShapeDtypeStruct(x.shape, F32),
        scratch_shapes=[pltpu.VMEM((N_DEV, rows, LANES), F32), pltpu.SemaphoreType.DMA((7,)),
                        pltpu.SemaphoreType.DMA((7,))],
        name="all_reduce_small", compiler_params=_cparams(0))(x)


def _sum_slots(x):
    def body(x_ref, o_ref):
        acc = x_ref[0]
        for d in range(1, N_DEV):
            acc = acc + x_ref[d]
        o_ref[...] = acc

    return pl.pallas_call(body, out_shape=jax.ShapeDtypeStruct(x.shape[1:], F32), name="sum_slots",
                          compiler_params=_cparams(0))(x)


def _adamw_small(w, g, m, v):
    def body(w_ref, g_ref, m_ref, v_ref, d_ref, nm_ref, nv_ref):
        d, nm, nv = _adamw(w_ref[...], g_ref[...], m_ref[...], v_ref[...])
        d_ref[...] = d
        nm_ref[...] = nm
        nv_ref[...] = nv

    sh = jax.ShapeDtypeStruct(w.shape, F32)
    return pl.pallas_call(body, out_shape=[sh] * 3, name="adamw_small", compiler_params=_cparams(0))(w, g, m, v)


def _pack(arrs):
    flat = jnp.concatenate([a.reshape(-1).astype(F32) for a in arrs])
    rows = -(-flat.shape[0] // (SUBLANES * LANES)) * SUBLANES
    return jnp.pad(flat, (0, rows * LANES - flat.shape[0])).reshape(rows, LANES)


def _unpack(buf, shapes):
    flat = buf.reshape(-1)
    out, off = [], 0
    for s in shapes:
        n = 1
        for d in s:
            n *= d
        out.append(flat[off:off + n].reshape(s))
        off += n
    return out


BIG = ("ffn_w1", "ffn_w3", "ffn_w2", "ab_w_in", "ab_w_out", "s5_glu_wa", "s5_glu_wb")
NAMES = ("ln_ffn_pre", "ln_mix", "ln_ffn_post", "ln_final", "ffn_w1", "ffn_w3", "ffn_w2", "ab_w_in",
         "ab_conv_w", "ab_w_out", "s5_lambda_re", "s5_lambda_im", "s5_log_dt", "s5_b_re", "s5_b_im",
         "s5_c_re", "s5_c_im", "s5_d", "s5_glu_wa", "s5_glu_wb")


def kernel(x, ln_ffn_pre, ln_mix, ln_ffn_post, ln_final, ffn_w1, ffn_w3, ffn_w2, ab_w_in, ab_conv_w, ab_w_out, s5_lambda_re, s5_lambda_im, s5_log_dt, s5_b_re, s5_b_im, s5_c_re, s5_c_im, s5_d, s5_glu_wa, s5_glu_wb, loss_target, m_ln_ffn_pre, m_ln_mix, m_ln_ffn_post, m_ln_final, m_ffn_w1, m_ffn_w3, m_ffn_w2, m_ab_w_in, m_ab_conv_w, m_ab_w_out, m_s5_lambda_re, m_s5_lambda_im, m_s5_log_dt, m_s5_b_re, m_s5_b_im, m_s5_c_re, m_s5_c_im, m_s5_d, m_s5_glu_wa, m_s5_glu_wb, v_ln_ffn_pre, v_ln_mix, v_ln_ffn_post, v_ln_final, v_ffn_w1, v_ffn_w3, v_ffn_w2, v_ab_w_in, v_ab_conv_w, v_ab_w_out, v_s5_lambda_re, v_s5_lambda_im, v_s5_log_dt, v_s5_b_re, v_s5_b_im, v_s5_c_re, v_s5_c_im, v_s5_d, v_s5_glu_wa, v_s5_glu_wb):
    w = dict(zip(NAMES, (ln_ffn_pre, ln_mix, ln_ffn_post, ln_final, ffn_w1, ffn_w3, ffn_w2, ab_w_in, ab_conv_w,
                         ab_w_out, s5_lambda_re, s5_lambda_im, s5_log_dt, s5_b_re, s5_b_im, s5_c_re, s5_c_im,
                         s5_d, s5_glu_wa, s5_glu_wb)))
    mom = dict(zip(NAMES, (m_ln_ffn_pre, m_ln_mix, m_ln_ffn_post, m_ln_final, m_ffn_w1, m_ffn_w3, m_ffn_w2,
                           m_ab_w_in, m_ab_conv_w, m_ab_w_out, m_s5_lambda_re, m_s5_lambda_im, m_s5_log_dt,
                           m_s5_b_re, m_s5_b_im, m_s5_c_re, m_s5_c_im, m_s5_d, m_s5_glu_wa, m_s5_glu_wb)))
    var = dict(zip(NAMES, (v_ln_ffn_pre, v_ln_mix, v_ln_ffn_post, v_ln_final, v_ffn_w1, v_ffn_w3, v_ffn_w2,
                           v_ab_w_in, v_ab_conv_w, v_ab_w_out, v_s5_lambda_re, v_s5_lambda_im, v_s5_log_dt,
                           v_s5_b_re, v_s5_b_im, v_s5_c_re, v_s5_c_im, v_s5_d, v_s5_glu_wa, v_s5_glu_wb)))
    nb, seq, D = x.shape
    T = nb * seq
    assert ln_mix.shape[0] == 2 and ab_w_in.shape[0] == 1 and s5_glu_wa.shape[0] == 1
    xc, yc, cc = _mesh_pos()
    dev = 4 * xc + 2 * yc + cc
    pos = jnp.stack([cc, 2 * xc + yc]).astype(jnp.int32)
    bq = min(ATTN_TILE, seq)
    tabs =_rope_tables(seq) + (_branch_bias(seq // bq, bq),)

    def ffn_piece(k, li, fj):
        return w[k][li, fj].astype(BF)

    g0 = _gather_comm([ffn_piece("ffn_w1", 0, 0), ffn_piece("ffn_w3", 0, 0), ab_conv_w[0], s5_d])
    _run_comm(g0, "gather_first")
    w1, w3 = {(0, 0): g0.results[0]}, {(0, 0): g0.results[1]}
    w2 = {}
    conv_w = g0.results[2].transpose(1, 0, 2).reshape(3, -1)
    dsk = g0.results[3].reshape(1, D)
    gains = {k: [w[k][i:i + 1] for i in range(2)] for k in ("ln_ffn_pre", "ln_mix", "ln_ffn_post")}

    h = x.reshape(T, D)
    saved = {}

    def ffn_fwd(h, gain, key, tag, comm_up, comm_down, after_up):
        n = _rms_fwd(h, gain, BF)
        t1, t3, g = _ffn_up(n, w1[key], w3[key], comm=comm_up)
        after_up()
        saved[tag] = (h, n, t1, t3, g)
        return _ffn_down(g, w2[key], h, comm=comm_down)

    c_up = _gather_comm([ffn_piece("ffn_w2", 0, 0), ab_w_out[0].astype(BF)])
    c_dn = _gather_comm([ab_w_in[0].astype(BF)])
    h = ffn_fwd(h, gains["ln_ffn_pre"][0], (0, 0), "pre0", c_up, c_dn,
                lambda: w2.update({(0, 0): c_up.results[0]}))
    wout = c_up.results[1].reshape(-1, D)
    wing = c_dn.results[0]
    c_proj = _gather_comm([ffn_piece("ffn_w1", 0, 1)])
    c_attn = _gather_comm([ffn_piece("ffn_w3", 0, 1), s5_glu_wa[0].astype(BF)])
    c_out = _gather_comm([s5_glu_wb[0].astype(BF)])
    h, saved["mix0"] = _mixer_ab_fwd(h, gains["ln_mix"][0], wing, conv_w, wout, tabs, nb, seq, c_proj, c_attn, c_out)
    w1[(0, 1)] = c_proj.results[0]
    w3[(0, 1)] = c_attn.results[0]
    wa = c_attn.results[1].reshape(-1, D)
    wb = c_out.results[0].reshape(-1, D)
    c_up2 = _gather_comm([ffn_piece("ffn_w2", 0, 1), ffn_piece("ffn_w1", 1, 0)])
    c_dn = _gather_comm([ffn_piece("ffn_w3", 1, 0)])
    h = ffn_fwd(h, gains["ln_ffn_post"][0], (0, 1), "post0", c_up2, c_dn,
                lambda: w2.update({(0, 1): c_up2.results[0]}))
    w1[(1, 0)] = c_up2.results[1]
    w3[(1, 0)] = c_dn.results[0]
    c_up3 = _gather_comm([ffn_piece("ffn_w2", 1, 0), ffn_piece("ffn_w1", 1, 1)])
    c_dn = _gather_comm([ffn_piece("ffn_w3", 1, 1)])
    h = ffn_fwd(h, gains["ln_ffn_pre"][1], (1, 0), "pre1", c_up3, c_dn,
                lambda: w2.update({(1, 0): c_up3.results[0]}))
    w1[(1, 1)] = c_up3.results[1]
    w3[(1, 1)] = c_dn.results[0]
    c_s5 = _gather_comm([ffn_piece("ffn_w2", 1, 1)])
    h, saved["mix1"] = _mixer_s5_fwd(h, gains["ln_mix"][1], w, dsk, wa, wb, nb, seq, c_s5)
    w2[(1, 1)] = c_s5.results[0]
    h = ffn_fwd(h, gains["ln_ffn_post"][1], (1, 1), "post1", None, None, lambda: None)
    dh, dhb, d_ln_final, loss_part = _loss_head(h, ln_final.reshape(1, D), loss_target.reshape(T, D))
    loss = lax.psum(loss_part[0, 0], ("x", "y", "c"))

    reduced = {}

    def reduce_start(names, grads):
        recv = _pair_exchange(grads, "pair_exchange")
        comms = []
        for nm, g, r in zip(names, grads, recv):
            part = _pair_sum(g, r, pos)
            comms.append(_exchange_comm([part]))
            reduced[nm] = (part, comms[-1])
        return comms

    def ffn_bwd(dh, dhb, key, tag, gain, carry, is_last=False, comm_dw2=None):
        h_in, n, t1, t3, g = saved[tag]
        da1, da3 = _ffn_bwd_hidden(dhb, w2[key], t1, t3, comm=carry)
        c2, = reduce_start([("ffn_w2",) + key], [_ffn_dw2(g, dhb, comm=comm_dw2)])
        dw1, dw3 = _ffn_dw13(n, da1, da3, comm=c2)
        c1, c3 = reduce_start([("ffn_w1",) + key, ("ffn_w3",) + key], [dw1, dw3])
        res = _ffn_dn_rms(da1, da3, w1[key], w3[key], h_in, gain, dh,
                          comm=_merge_comms([c1, c3]) if is_last else c1)
        return list(res) + [None if is_last else c3]

    g_small = {"ln_final": d_ln_final.reshape(D)}
    g_ln = {k: [None, None] for k in gains}
    dh, dhb, g_ln["ln_ffn_post"][1], carry = ffn_bwd(dh, dhb, (1, 1), "post1", gains["ln_ffn_post"][1], None)
    dh, dhb, g_ln["ln_mix"][1], s5_small = _mixer_s5_bwd(
        dh, saved["mix1"], gains["ln_mix"][1], wa, wb, reduce_start, carry)
    s5_names = list(s5_small)
    c_s5_grads = _gather_comm([_pack([s5_small[k] for k in s5_names])])
    dh, dhb, g_ln["ln_ffn_pre"][1], carry = ffn_bwd(dh, dhb, (1, 0), "pre1", gains["ln_ffn_pre"][1], None,
                                                    comm_dw2=c_s5_grads)
    g_red = dict(zip(s5_names, _unpack(_sum_slots(c_s5_grads.results[0]), [s5_small[k].shape for k in s5_names])))
    dh, dhb, g_ln["ln_ffn_post"][0], carry = ffn_bwd(dh, dhb, (0, 1), "post0", gains["ln_ffn_post"][0], carry)
    dh, dhb, g_ln["ln_mix"][0], g_small["ab_conv_w"], carry = _mixer_ab_bwd(
        dh, dhb, saved["mix0"], gains["ln_mix"][0], wing, conv_w, wout, tabs, nb, seq, reduce_start, carry)
    dh, dhb, g_ln["ln_ffn_pre"][0], _ = ffn_bwd(dh, dhb, (0, 0), "pre0", gains["ln_ffn_pre"][0], carry, is_last=True)
    grad_x = dh.reshape(nb, seq, D)
    for k in g_ln:
        g_small[k] = jnp.concatenate(g_ln[k], axis=0)

    out = {}
    for k in BIG:
        transposed = k in ("ffn_w1", "ffn_w3")
        pieces = [(li, fj) for li in range(2) for fj in range(2)] if w[k].ndim == 4 else [None]

        def view(a):
            a = a.swapaxes(-1, -2) if transposed else a
            return a.reshape(len(pieces), -1, a.shape[-1])

        w3d, m3d, v3d = view(w[k]), view(mom[k]), view(var[k])
        bufs = [lax.empty(w3d.shape, F32) for _ in range(4)]
        for q, key in enumerate(pieces):
            part, comm = reduced[k if key is None else (k,) + key]
            bufs = _adamw_piece(w3d, m3d, v3d, q, part, comm.results[0], bufs)
        if transposed:
            out[k] = [t.reshape(w[k].shape[:2] + w3d.shape[1:]).swapaxes(-1, -2) for t in bufs]
        else:
            out[k] = [t.reshape(w[k].shape) for t in bufs]

    small_names = [k for k in NAMES if k not in BIG]
    late_names = [k for k in small_names if k not in g_red]
    g_red.update(zip(late_names, _unpack(_all_reduce_small(_pack([g_small[k] for k in late_names])),
                                         [g_small[k].shape for k in late_names])))
    cw = w["ab_conv_w"].shape[-1]
    g_red["ab_conv_w"] = lax.dynamic_slice_in_dim(g_red["ab_conv_w"], dev * cw, cw, axis=1)[None]
    dsz = w["s5_d"].shape[-1]
    g_red["s5_d"] = lax.dynamic_slice_in_dim(g_red["s5_d"].reshape(1, -1), dev * dsz, dsz, axis=1)
    shapes = [w[k].shape for k in small_names]
    g_red = {k: g_red[k].reshape(w[k].shape) for k in small_names}
    d_s, m_s, v_s = _adamw_small(_pack([w[k] for k in small_names]), _pack([g_red[k] for k in small_names]),
                                 _pack([mom[k] for k in small_names]), _pack([var[k] for k in small_names]))
    for k, d, nm, nv in zip(small_names, _unpack(d_s, shapes), _unpack(m_s, shapes), _unpack(v_s, shapes)):
        out[k] = [g_red[k], d, nm, nv]

    return (loss, grad_x, *[out[k][0] for k in NAMES], *[out[k][1] for k in NAMES],
            *[out[k][2] for k in NAMES], *[out[k][3] for k in NAMES])
```

```python
import jax
import jax.numpy as jnp
from jax import lax
from jax.experimental import pallas as pl
from jax.experimental.pallas import tpu as pltpu

F32, BF = jnp.float32, jnp.bfloat16
N_DEV = 8
MESH = pl.DeviceIdType.MESH
LANES = 128
SUBLANES = 8
VMEM_LIMIT = 56 * 2 ** 20
ROW_TILE = 512
FFN_ROW_TILE = 1024
COL_TILE = 512
ATTN_TILE = 512
SCAN_UNROLL = 4
ELEMS_PER_BLOCK = 256 * 1024
PAIR_SUM_ELEMS = 2048 * 1024
RMS_EPS = 1e-6
ROPE_THETA = 10000.0
NEG_INF = -1e30
S5_STATE = 64
S5_GROUP = 16
GROUPS_PER_BLOCK = LANES // S5_GROUP
STATE_COLS = GROUPS_PER_BLOCK * S5_STATE
DILATED_PATTERN = ((128, 1), (512, 4), (2048, 16))
ADAM_LR, ADAM_B1, ADAM_B2, ADAM_EPS, ADAM_WD, ADAM_STEP = 0.001, 0.9, 0.999, 1e-08, 0.01, 10
GELU_C = 0.7978845608028654
GELU_A = 0.044715


def _cparams(n_grid, vmem=VMEM_LIMIT):
    sem = ("arbitrary",) * n_grid if n_grid else None
    return pltpu.CompilerParams(dimension_semantics=sem, vmem_limit_bytes=vmem)


def _sig(x):
    return 1.0 / (1.0 + jnp.exp(-x))


def _gelu(x):
    return 0.5 * x * (1.0 + jnp.tanh(GELU_C * (x + GELU_A * x * x * x)))


def _gelu_grad(x):
    t = jnp.tanh(GELU_C * (x + GELU_A * x * x * x))
    return 0.5 * (1.0 + t) + 0.5 * x * (1.0 - t * t) * GELU_C * (1.0 + 3.0 * GELU_A * x * x)


def _dot(a, b, dims):
    a = a if a.dtype == BF else a.astype(BF)
    b = b if b.dtype == BF else b.astype(BF)
    return lax.dot_general(a, b, (dims, ((), ())), preferred_element_type=F32)


NN = ((1,), (0,))
NT = ((1,), (1,))
TN = ((0,), (0,))


def _row_block(rows, cols, elems=ELEMS_PER_BLOCK, mult=16):
    cap = max(mult, elems // cols)
    best = None
    for b in range(mult, min(rows, cap) + 1, mult):
        if rows % b == 0:
            best = b
    return rows if best is None else best


class _Comm:
    def __init__(self, ins, outs, sems, start, finish, members=()):
        self.ins, self.outs, self.sems, self.start, self.finish = ins, outs, sems, start, finish
        self.members = members
        self.results = None

    def set_results(self, res):
        self.results = list(res)
        off = 0
        for m in self.members:
            m.set_results(res[off:off + len(m.outs)])
            off += len(m.outs)


def _merge_comms(comms):
    comms = [c for c in comms if c is not None]
    if len(comms) < 2:
        return comms[0] if comms else None

    def each(fn_name, ins, outs, sems):
        i = o = s = 0
        for c in comms:
            ni, no, ns = len(c.ins), len(c.outs), len(c.sems)
            getattr(c, fn_name)(ins[i:i + ni], outs[o:o + no], sems[s:s + ns])
            i, o, s = i + ni, o + no, s + ns

    return _Comm([a for c in comms for a in c.ins], [a for c in comms for a in c.outs],
                 [a for c in comms for a in c.sems],
                 lambda ins, outs, sems: each("start", ins, outs, sems),
                 lambda ins, outs, sems: each("finish", ins, outs, sems), members=tuple(comms))


def _call(body, name, grid, in_specs, out_specs, out_shape, args, scratch=(), comm=None):
    in_specs, out_specs, out_shape, scratch = list(in_specs), list(out_specs), list(out_shape), list(scratch)
    if comm is None:
        return pl.pallas_call(body, grid=grid, in_specs=in_specs, out_specs=out_specs, out_shape=out_shape,
                              scratch_shapes=scratch, name=name, compiler_params=_cparams(len(grid)))(*args)
    n_in, n_out, n_sc = len(in_specs), len(out_specs), len(scratch)
    ci, co = len(comm.ins), len(comm.outs)

    def hosted(*refs):
        ins, refs = refs[:n_in], refs[n_in:]
        cins, refs = refs[:ci], refs[ci:]
        outs, refs = refs[:n_out], refs[n_out:]
        couts, refs = refs[:co], refs[co:]
        sc, csems = refs[:n_sc], refs[n_sc:]
        first = last = None
        for d, n in enumerate(grid):
            p = pl.program_id(d)
            first = (p == 0) if first is None else first & (p == 0)
            last = (p == n - 1) if last is None else last & (p == n - 1)

        @pl.when(first)
        def _():
            comm.start(cins, couts, csems)

        body(*ins, *outs, *sc)

        @pl.when(last)
        def _():
            comm.finish(cins, couts, csems)

    any_spec = pl.BlockSpec(memory_space=pl.ANY)
    res = pl.pallas_call(
        hosted, grid=grid, in_specs=in_specs + [any_spec] * ci, out_specs=out_specs + [any_spec] * co,
        out_shape=out_shape + list(comm.outs), scratch_shapes=scratch + list(comm.sems), name=name,
        compiler_params=_cparams(len(grid)))(*args, *comm.ins)
    comm.set_results(res[n_out:])
    return list(res[:n_out])


def _mm(name, grid, operands, pairs, n_acc, acc_shape, extras, outs, epilogue, comm=None, nrow=1, ncol=1,
        whole_tile_epilogue=False):
    nk = grid[2]
    n_op, n_ex, n_out = len(operands), len(extras), len(outs)

    def part_of(ref, dim, t, n):
        if n == 1:
            return ref
        size = ref.shape[dim] // n
        idx = [slice(None)] * len(ref.shape)
        idx[dim] = pl.ds(t * size, size)
        return ref.at[tuple(idx)]

    def tile_of(ref, r, c):
        return part_of(part_of(ref, 0, r, nrow), 1, c, ncol)

    def products(op, r, c):
        parts = [None] * n_acc
        for ai, bi, dims, ci in pairs:
            a = part_of(op[ai], 1 - dims[0][0], r, nrow)
            b = part_of(op[bi], 1 - dims[1][0], c, ncol)
            d = _dot(a[...], b[...], dims)
            parts[ci] = d if parts[ci] is None else parts[ci] + d
        return parts

    def body(*refs):
        op = refs[:n_op]
        ex = refs[n_op:n_op + n_ex]
        out = refs[n_op + n_ex:n_op + n_ex + n_out]
        acc = refs[n_op + n_ex + n_out:]
        tiles = [(r, c) for r in range(nrow) for c in range(ncol)]

        def views(refs_, t):
            return [tile_of(q, *t) for q in refs_]

        if nk == 1:
            parts = products(op, *tiles[0])
            for q, t in enumerate(tiles):
                nxt = products(op, *tiles[q + 1]) if q + 1 < len(tiles) else None
                epilogue(parts, views(ex, t), views(out, t))
                parts = nxt
            return
        k = pl.program_id(2)

        @pl.when(k == 0)
        def _():
            for q in acc:
                q[...] = jnp.zeros_like(q)

        for t in tiles:
            parts = products(op, *t)
            for q, p in zip(views(acc, t), parts):
                q[...] += p

        @pl.when(k == nk - 1)
        def _():
            if whole_tile_epilogue:
                epilogue(acc, ex, out)
                return
            for t in tiles:
                epilogue([q[...] for q in views(acc, t)], views(ex, t), views(out, t))

    return _call(body, name, grid, [s for _, s in operands] + [s for _, s in extras], [s for _, s in outs],
                 [sh for sh, _ in outs], [a for a, _ in operands] + [a for a, _ in extras],
                 scratch=[pltpu.VMEM(acc_shape, F32) for _ in range(n_acc if nk > 1 else 0)], comm=comm)


def _to_seg(a, seg_len):
    T, D = a.shape
    return a.reshape(SUBLANES, seg_len, D).transpose(1, 0, 2).reshape(T, D)


def _to_tok(a, seg_len):
    T, D = a.shape
    return a.reshape(seg_len, SUBLANES, D).transpose(1, 0, 2).reshape(T, D)


def _rms_fwd(h, gain, out_dtype):
    T, D = h.shape
    bm = min(ROW_TILE, T)

    def body(h_ref, g_ref, o_ref):
        x = h_ref[...]
        r = lax.rsqrt(jnp.mean(x * x, axis=-1, keepdims=True) + RMS_EPS)
        o_ref[...] = (x * r * g_ref[...]).astype(out_dtype)

    row = pl.BlockSpec((bm, D), lambda i: (i, 0))
    return pl.pallas_call(
        body, grid=(T // bm,), in_specs=[row, pl.BlockSpec((1, D), lambda i: (0, 0))],
        out_specs=row, out_shape=jax.ShapeDtypeStruct((T, D), out_dtype), name="rms_fwd",
        compiler_params=_cparams(1))(h, gain)


def _rms_bwd_rows(dn, x, g):
    r = lax.rsqrt(jnp.mean(x * x, axis=-1, keepdims=True) + RMS_EPS)
    xh = x * r
    dng = dn * g
    dx = r * (dng - xh * jnp.mean(dng * xh, axis=-1, keepdims=True))
    return dx, jnp.sum(dn * xh, axis=0, keepdims=True)


def _rms_bwd(dn, h, gain, dh_up):
    T, D = h.shape
    bm = min(ROW_TILE, T)

    def body(dn_ref, h_ref, g_ref, up_ref, dh_ref, dhb_ref, dg_ref):
        dx, dg = _rms_bwd_rows(dn_ref[...], h_ref[...], g_ref[...])
        dh = up_ref[...] + dx
        dh_ref[...] = dh
        dhb_ref[...] = dh.astype(BF)

        @pl.when(pl.program_id(0) == 0)
        def _():
            dg_ref[...] = jnp.zeros_like(dg_ref)

        dg_ref[...] += dg

    row = pl.BlockSpec((bm, D), lambda i: (i, 0))
    vec = pl.BlockSpec((1, D), lambda i: (0, 0))
    return pl.pallas_call(
        body, grid=(T // bm,), in_specs=[row, row, vec, row], out_specs=[row, row, vec],
        out_shape=[jax.ShapeDtypeStruct((T, D), F32), jax.ShapeDtypeStruct((T, D), BF),
                   jax.ShapeDtypeStruct((1, D), F32)],
        name="rms_bwd", compiler_params=_cparams(1))(dn, h, gain, dh_up)


def _loss_head(h, gain, target):
    T, D = h.shape
    bm = min(ROW_TILE, T)

    def body(h_ref, g_ref, t_ref, dh_ref, dhb_ref, dg_ref, loss_ref):
        x = h_ref[...]
        g = g_ref[...]
        r = lax.rsqrt(jnp.mean(x * x, axis=-1, keepdims=True) + RMS_EPS)
        err = x * r * g - t_ref[...]
        part = 0.5 * jnp.sum(jnp.sum(err * err, axis=-1, keepdims=True), axis=0, keepdims=True) / D
        dx, dg = _rms_bwd_rows(err / D, x, g)
        dh_ref[...] = dx
        dhb_ref[...] = dx.astype(BF)

        @pl.when(pl.program_id(0) == 0)
        def _():
            dg_ref[...] = jnp.zeros_like(dg_ref)
            loss_ref[...] = jnp.zeros_like(loss_ref)

        dg_ref[...] += dg
        loss_ref[...] += jnp.broadcast_to(part, loss_ref.shape)

    row = pl.BlockSpec((bm, D), lambda i: (i, 0))
    vec = pl.BlockSpec((1, D), lambda i: (0, 0))
    return pl.pallas_call(
        body, grid=(T // bm,), in_specs=[row, vec, row],
        out_specs=[row, row, vec, pl.BlockSpec((SUBLANES, LANES), lambda i: (0, 0))],
        out_shape=[jax.ShapeDtypeStruct((T, D), F32), jax.ShapeDtypeStruct((T, D), BF),
                   jax.ShapeDtypeStruct((1, D), F32), jax.ShapeDtypeStruct((SUBLANES, LANES), F32)],
        name="loss_head", compiler_params=_cparams(1))(h, gain, target)


def _ffn_up(n, w1g, w3g, comm=None):
    T, D = n.shape
    fs = w1g.shape[-1]
    bm = min(FFN_ROW_TILE, T)
    wspec = pl.BlockSpec((None, D, fs), lambda s, i, k: (s, 0, 0))
    ospec = pl.BlockSpec((None, bm, fs), lambda s, i, k: (s, i, 0))

    def epi(accs, ex, outs):
        a1, a3 = accs
        sg = _sig(a1)
        silu = a1 * sg
        outs[0][...] = (a3 * sg * (1.0 + a1 * (1.0 - sg))).astype(BF)
        outs[1][...] = silu.astype(BF)
        outs[2][...] = (silu * a3).astype(BF)

    sh = jax.ShapeDtypeStruct((N_DEV, T, fs), BF)
    return _mm("ffn_up", (N_DEV, T // bm, 1),
               [(n, pl.BlockSpec((bm, D), lambda s, i, k: (i, 0))), (w1g, wspec), (w3g, wspec)],
               [(0, 1, NN, 0), (0, 2, NN, 1)], 2, None, [], [(sh, ospec)] * 3, epi, comm=comm,
               nrow=max(1, bm // ROW_TILE))


def _ffn_down(g, w2g, h, comm=None):
    _, T, fs = g.shape
    D = h.shape[1]
    bm = min(FFN_ROW_TILE, T)
    row = pl.BlockSpec((bm, D), lambda i, j, s: (i, 0))

    def epi(accs, ex, outs):
        outs[0][...] = ex[0][...] + 0.5 * accs[0]

    return _mm("ffn_down", (T // bm, 1, N_DEV),
               [(g, pl.BlockSpec((None, bm, fs), lambda i, j, s: (s, i, 0))),
                (w2g, pl.BlockSpec((None, fs, D), lambda i, j, s: (s, 0, 0)))],
               [(0, 1, NN, 0)], 1, (bm, D), [(h, row)],
               [(jax.ShapeDtypeStruct((T, D), F32), row)], epi, comm=comm,
               nrow=max(1, bm // ROW_TILE), ncol=max(1, D // COL_TILE))[0]


def _ffn_bwd_hidden(dhb, w2g, t1, t3, comm=None):
    T, D = dhb.shape
    fs = t1.shape[-1]
    bm = min(FFN_ROW_TILE, T)
    aspec = pl.BlockSpec((None, bm, fs), lambda s, i, k: (s, i, 0))

    def epi(accs, ex, outs):
        dg = 0.5 * accs[0]
        outs[0][...] = (dg * ex[0][...].astype(F32)).astype(BF)
        outs[1][...] = (dg * ex[1][...].astype(F32)).astype(BF)

    sh = jax.ShapeDtypeStruct((N_DEV, T, fs), BF)
    return _mm("ffn_bwd_hidden", (N_DEV, T // bm, 1),
               [(dhb, pl.BlockSpec((bm, D), lambda s, i, k: (i, 0))),
                (w2g, pl.BlockSpec((None, fs, D), lambda s, i, k: (s, 0, 0)))],
               [(0, 1, NT, 0)], 1, None, [(t1, aspec), (t3, aspec)], [(sh, aspec)] * 2, epi, comm=comm,
               nrow=max(1, bm // ROW_TILE))


def _ffn_dw2(g, dhb, comm=None):
    _, T, fs = g.shape
    D = dhb.shape[1]
    bn = min(COL_TILE, D)

    def epi(accs, ex, outs):
        outs[0][...] = (0.5 * accs[0]).astype(BF)

    return _mm("ffn_dw2", (N_DEV, D // bn, 1),
               [(g, pl.BlockSpec((None, T, fs), lambda s, j, k: (s, 0, 0))),
                (dhb, pl.BlockSpec((T, bn), lambda s, j, k: (0, j)))],
               [(0, 1, TN, 0)], 1, None, [],
               [(jax.ShapeDtypeStruct((N_DEV, fs, D), BF), pl.BlockSpec((None, fs, bn), lambda s, j, k: (s, 0, j)))],
               epi, comm=comm)[0]


def _ffn_dw13(n, da1, da3, comm=None):
    T, D = n.shape
    fs = da1.shape[-1]
    bn = min(COL_TILE, D)
    dspec = pl.BlockSpec((None, T, fs), lambda s, j, k: (s, 0, 0))
    ospec = pl.BlockSpec((None, fs, bn), lambda s, j, k: (s, 0, j))

    def epi(accs, ex, outs):
        outs[0][...] = accs[0].astype(BF)
        outs[1][...] = accs[1].astype(BF)

    sh = jax.ShapeDtypeStruct((N_DEV, fs, D), BF)
    return _mm("ffn_dw13", (N_DEV, D // bn, 1),
               [(da1, dspec), (da3, dspec), (n, pl.BlockSpec((T, bn), lambda s, j, k: (0, j)))],
               [(0, 2, TN, 0), (1, 2, TN, 1)], 2, None, [], [(sh, ospec)] * 2, epi, comm=comm)


def _ffn_dn_rms(da1, da3, w1g, w3g, h, gain, dh_up, comm=None):
    _, T, fs = da1.shape
    D = w1g.shape[-2]
    bm = min(ROW_TILE, T)
    rows_per_pass = min(64, bm)
    dspec = pl.BlockSpec((None, bm, fs), lambda i, j, s: (s, i, 0))
    wspec = pl.BlockSpec((None, D, fs), lambda i, j, s: (s, 0, 0))
    row = pl.BlockSpec((bm, D), lambda i, j, s: (i, 0))
    vec = pl.BlockSpec((1, D), lambda i, j, s: (0, 0))

    def epi(acc, ex, outs):
        h_ref, g_ref, up_ref = ex
        dh_ref, dhb_ref, dg_ref = outs

        @pl.when(pl.program_id(0) == 0)
        def _():
            dg_ref[...] = jnp.zeros_like(dg_ref)

        g = g_ref[...]
        dg = jnp.zeros((1, D), F32)
        for r in range(bm // rows_per_pass):
            rows = pl.ds(r * rows_per_pass, rows_per_pass)
            dx, dg_r = _rms_bwd_rows(acc[0][rows, :], h_ref[rows, :], g)
            dh = up_ref[rows, :] + dx
            dh_ref[rows, :] = dh
            dhb_ref[rows, :] = dh.astype(BF)
            dg = dg + dg_r
        dg_ref[...] += dg

    return _mm("ffn_dn_rms", (T // bm, 1, N_DEV),
               [(da1, dspec), (w1g, wspec), (da3, dspec), (w3g, wspec)],
               [(0, 1, NT, 0), (2, 3, NT, 0)], 1, (bm, D), [(h, row), (gain, vec), (dh_up, row)],
               [(jax.ShapeDtypeStruct((T, D), F32), row), (jax.ShapeDtypeStruct((T, D), BF), row),
                (jax.ShapeDtypeStruct((1, D), F32), vec)],
               epi, comm=comm, ncol=max(1, D // COL_TILE), whole_tile_epilogue=True)


def _rope_tables(seq):
    half = LANES // 2
    inv = ROPE_THETA ** (-jnp.arange(0, half, dtype=F32) * 2.0 / LANES)
    ang = jnp.arange(seq, dtype=F32)[:, None] * inv[None, :]
    cos, sin = jnp.cos(ang), jnp.sin(ang)
    return jnp.concatenate([cos, cos], axis=1), jnp.concatenate([-sin, sin], axis=1)


def _branch_bias(nq, bq):
    d = (jnp.arange(nq)[:, None, None] * bq + jnp.arange(bq)[None, :, None]
         - jnp.arange(bq)[None, None, :])
    mult = jnp.zeros(d.shape, F32)
    for window, dil in DILATED_PATTERN:
        mult = mult + ((d >= 0) & (d % dil == 0) & (d <= window)).astype(F32)
    return jnp.where(mult > 0, jnp.log(jnp.maximum(mult, 1.0)), NEG_INF)


def _proj_fwd(u, wing, comm=None):
    T, D = u.shape
    ws = wing.shape[-1]
    bm = min(ROW_TILE, T)

    def epi(accs, ex, outs):
        outs[0][...] = accs[0]

    return _mm("proj_fwd", (N_DEV, T // bm, 1),
               [(u, pl.BlockSpec((bm, D), lambda s, i, k: (i, 0))),
                (wing, pl.BlockSpec((None, D, ws), lambda s, i, k: (s, 0, 0)))],
               [(0, 1, NN, 0)], 1, None, [],
               [(jax.ShapeDtypeStruct((T, N_DEV * ws), F32),
                 pl.BlockSpec((bm, ws), lambda s, i, k: (i, s)))], epi, comm=comm)[0]


def _rope_fwd(proj, cosf, sinf, seq, nh):
    T = proj.shape[0]
    bs = min(ROW_TILE, seq)
    nst = seq // bs
    scale = LANES ** -0.5

    def body(x_ref, c_ref, s_ref, o_ref):
        j = pl.program_id(1)
        t = x_ref[...]
        rot = t * c_ref[...] + pltpu.roll(t, LANES // 2, 1) * s_ref[...]
        rot = rot * jnp.where(j < nh, scale, 1.0)
        o_ref[...] = jnp.where(j < 2 * nh, rot, t).astype(BF)

    blk = pl.BlockSpec((bs, LANES), lambda r, j: (r, j))
    tab = pl.BlockSpec((bs, LANES), lambda r, j: (r % nst, 0))
    return pl.pallas_call(
        body, grid=(T // bs, 3 * nh), in_specs=[blk, tab, tab], out_specs=blk,
        out_shape=jax.ShapeDtypeStruct((T, 3 * nh * LANES), BF), name="rope_fwd",
        compiler_params=_cparams(2))(proj, cosf, sinf)


def _attn_fwd(qkv, bias, nb, seq, nh, comm=None):
    T = nb * seq
    bq = bias.shape[1]
    nq = seq // bq

    def body(q_ref, k_ref, v_ref, b_ref, o_ref, lse_ref):
        qi = pl.program_id(2)
        q = q_ref[...]

        def step(kj, carry):
            m, l, acc = carry
            rows = pl.ds(pl.multiple_of(kj * bq, bq), bq)
            s = _dot(q, k_ref[rows, :], NT) + b_ref[qi - kj]
            m_new = jnp.maximum(m, jnp.max(s, axis=1, keepdims=True))
            p = jnp.exp(s - m_new)
            alpha = jnp.exp(m - m_new)
            l = alpha * l + jnp.sum(p, axis=1, keepdims=True)
            acc = alpha * acc + _dot(p, v_ref[rows, :], NN)
            return m_new, l, acc

        init = (jnp.full((bq, 1), NEG_INF, F32), jnp.zeros((bq, 1), F32), jnp.zeros((bq, LANES), F32))
        m, l, acc = lax.fori_loop(0, qi + 1, step, init)
        o_ref[...] = (acc / l).astype(BF)
        lse_ref[...] = m + jnp.log(l)

    return _call(
        body, "attn_fwd", (nb, nh, nq),
        [pl.BlockSpec((bq, LANES), lambda b, h, i: (b * nq + i, h)),
         pl.BlockSpec((seq, LANES), lambda b, h, i: (b, nh + h)),
         pl.BlockSpec((seq, LANES), lambda b, h, i: (b, 2 * nh + h)),
         pl.BlockSpec((nq, bq, bq), lambda b, h, i: (0, 0, 0))],
        [pl.BlockSpec((bq, LANES), lambda b, h, i: (b * nq + i, h)),
         pl.BlockSpec((None, bq, 1), lambda b, h, i: (h, b * nq + i, 0))],
        [jax.ShapeDtypeStruct((T, 2 * nh * LANES), BF), jax.ShapeDtypeStruct((nh, T, 1), F32)],
        (qkv, qkv, qkv, bias), comm=comm)


def _attn_bwd(qkv, cat, dcat, lse, bias, nb, seq, nh, comm=None):
    T = nb * seq
    bq = bias.shape[1]
    nq = seq // bq

    def body(k_ref, v_ref, q_ref, o_ref, do_ref, lse_ref, b_ref, dq_ref, dk_ref, dv_ref):
        kj = pl.program_id(2)
        k = k_ref[...]
        v = v_ref[...]

        @pl.when(kj == 0)
        def _():
            dq_ref[...] = jnp.zeros_like(dq_ref)

        def step(qi, carry):
            dk, dv = carry
            rows = pl.ds(pl.multiple_of(qi * bq, bq), bq)
            q = q_ref[rows, :]
            do = do_ref[rows, :]
            dob = do.astype(BF)
            delta = jnp.sum(do * o_ref[rows, :].astype(F32), axis=1, keepdims=True)
            p = jnp.exp(_dot(q, k, NT) + b_ref[qi - kj] - lse_ref[rows, :])
            dv = dv + _dot(p, dob, TN)
            ds = p * (_dot(dob, v, NT) - delta)
            dq_ref[rows, :] += _dot(ds, k, NN)
            return dk + _dot(ds, q, TN), dv

        z = jnp.zeros((bq, LANES), F32)
        dk, dv = lax.fori_loop(kj, nq, step, (z, z))
        dk_ref[...] = dk
        dv_ref[...] = dv

    whole = pl.BlockSpec((seq, LANES), lambda b, h, i: (b, h))
    tile = pl.BlockSpec((bq, LANES), lambda b, h, i: (b * nq + i, h))
    sh = jax.ShapeDtypeStruct((T, nh * LANES), F32)
    return _call(
        body, "attn_bwd", (nb, nh, nq),
        [pl.BlockSpec((bq, LANES), lambda b, h, i: (b * nq + i, nh + h)),
         pl.BlockSpec((bq, LANES), lambda b, h, i: (b * nq + i, 2 * nh + h)),
         whole, whole, whole, pl.BlockSpec((None, seq, 1), lambda b, h, i: (h, b, 0)),
         pl.BlockSpec((nq, bq, bq), lambda b, h, i: (0, 0, 0))],
        [whole, tile, tile], [sh, sh, sh],
        (qkv, qkv, qkv, cat, dcat, lse, bias), comm=comm)


def _conv_parts(gc, xin, w_ref):
    w = [w_ref[k:k + 1, :] for k in range(3)]
    u = gc * xin
    row = lax.broadcasted_iota(jnp.int32, u.shape, 0)
    u1 = jnp.where(row >= 1, pltpu.roll(u, 1, 0), 0.0)
    u2 = jnp.where(row >= 2, pltpu.roll(u, 2, 0), 0.0)
    return u, u1, u2, w[0] * u2 + w[1] * u1 + w[2] * u, w, row


def _conv_fwd(proj, conv_w, cat, nb, seq, width):
    cw = min(2 * LANES, width)
    nc = width // cw

    def body(gb_ref, gc_ref, x_ref, w_ref, cat_ref, o_ref):
        _, _, _, conv, _, _ = _conv_parts(gc_ref[...], x_ref[...], w_ref)
        o_ref[...] = (gb_ref[...] * conv).astype(BF)

    def sec(k):
        return pl.BlockSpec((seq, cw), lambda b, c: (b, k * nc + c))

    return pl.pallas_call(
        body, grid=(nb, nc),
        in_specs=[sec(3), sec(4), sec(5), pl.BlockSpec((3, cw), lambda b, c: (0, c)),
                  pl.BlockSpec(memory_space=pl.ANY)],
        out_specs=pl.BlockSpec((seq, cw), lambda b, c: (b, nc + c)),
        out_shape=jax.ShapeDtypeStruct(cat.shape, BF), input_output_aliases={4: 0},
        name="conv_fwd", compiler_params=_cparams(2))(proj, proj, proj, conv_w, cat)


def _conv_bwd(proj, conv_w, dcat, nb, seq, width):
    cw = min(2 * LANES, width)
    nc = width // cw
    T = nb * seq

    def body(gb_ref, gc_ref, x_ref, w_ref, d_ref, dgb_ref, dgc_ref, dx_ref, dw_ref):
        gc = gc_ref[...]
        xin = x_ref[...]
        u, u1, u2, conv, w, row = _conv_parts(gc, xin, w_ref)
        dsc = d_ref[...]
        dgb_ref[...] = dsc * conv
        dconv = dsc * gb_ref[...]
        d1 = jnp.where(row < seq - 1, pltpu.roll(dconv, seq - 1, 0), 0.0)
        d2 = jnp.where(row < seq - 2, pltpu.roll(dconv, seq - 2, 0), 0.0)
        du = w[2] * dconv + w[1] * d1 + w[0] * d2
        dgc_ref[...] = du * xin
        dx_ref[...] = du * gc

        @pl.when(pl.program_id(1) == 0)
        def _():
            dw_ref[...] = jnp.zeros_like(dw_ref)

        dw_ref[0:1, :] += jnp.sum(dconv * u2, axis=0, keepdims=True)
        dw_ref[1:2, :] += jnp.sum(dconv * u1, axis=0, keepdims=True)
        dw_ref[2:3, :] += jnp.sum(dconv * u, axis=0, keepdims=True)

    def sec(k):
        return pl.BlockSpec((seq, cw), lambda c, b: (b, k * nc + c))

    out = pl.BlockSpec((seq, cw), lambda c, b: (b, c))
    wsp = pl.BlockSpec((3, cw), lambda c, b: (0, c))
    sh = jax.ShapeDtypeStruct((T, width), F32)
    return pl.pallas_call(
        body, grid=(nc, nb), in_specs=[sec(3), sec(4), sec(5), wsp, sec(1)],
        out_specs=[out, out, out, wsp], out_shape=[sh, sh, sh, jax.ShapeDtypeStruct((3, width), F32)],
        name="conv_bwd", compiler_params=_cparams(2))(proj, proj, proj, conv_w, dcat)


def _assemble_dproj(dq, dk, dv, dgb, dgc, dxin, cosf, sinf, seq):
    T, width = dq.shape
    nh = width // LANES
    bs = min(256, seq)
    nst = seq // bs
    scale = LANES ** -0.5

    def body(dq_ref, dk_ref, dv_ref, dgb_ref, dgc_ref, dx_ref, c_ref, s_ref, o_ref):
        sec = pl.program_id(1)
        c = c_ref[...]
        s = s_ref[...]

        def unrope(ref, mul):
            for h in range(nh):
                cols = slice(h * LANES, (h + 1) * LANES)
                t = ref[:, cols]
                o_ref[:, cols] = ((t * c + pltpu.roll(t * s, LANES // 2, 1)) * mul).astype(BF)

        @pl.when(sec == 0)
        def _():
            unrope(dq_ref, scale)

        @pl.when(sec == 1)
        def _():
            unrope(dk_ref, 1.0)

        for k, ref in ((2, dv_ref), (3, dgb_ref), (4, dgc_ref), (5, dx_ref)):
            @pl.when(sec == k)
            def _(ref=ref):
                o_ref[...] = ref[...].astype(BF)

    blk = pl.BlockSpec((bs, width), lambda r, k: (r, 0))
    tab = pl.BlockSpec((bs, LANES), lambda r, k: (r % nst, 0))
    return pl.pallas_call(
        body, grid=(T // bs, 6), in_specs=[blk] * 6 + [tab, tab],
        out_specs=pl.BlockSpec((bs, width), lambda r, k: (r, k)),
        out_shape=jax.ShapeDtypeStruct((T, 6 * width), BF), name="assemble_dproj",
        compiler_params=_cparams(2))(dq, dk, dv, dgb, dgc, dxin, cosf, sinf)


def _res_mm(name, a, w, h, comm=None):
    T, K = a.shape
    N = w.shape[1]
    bm = min(ROW_TILE, T)
    bk = min(ROW_TILE, K)
    row = pl.BlockSpec((bm, N), lambda i, j, k: (i, 0))

    def epi(accs, ex, outs):
        outs[0][...] = ex[0][...] + accs[0]

    return _mm(name, (T // bm, 1, K // bk),
               [(a, pl.BlockSpec((bm, bk), lambda i, j, k: (i, k))),
                (w, pl.BlockSpec((bk, N), lambda i, j, k: (k, 0)))],
               [(0, 1, NN, 0)], 1, (bm, N), [(h, row)],
               [(jax.ShapeDtypeStruct((T, N), F32), row)], epi, comm=comm, ncol=max(1, N // COL_TILE))[0]


def _mm_nt(name, a, w, out_dtype):
    T, K = a.shape
    N = w.shape[0]
    bm = min(ROW_TILE, T)
    bn = min(ROW_TILE, N)

    def epi(accs, ex, outs):
        outs[0][...] = accs[0].astype(out_dtype)

    return _mm(name, (T // bm, N // bn, 1),
               [(a, pl.BlockSpec((bm, K), lambda i, j, k: (i, 0))),
                (w, pl.BlockSpec((bn, K), lambda i, j, k: (j, 0)))],
               [(0, 1, NT, 0)], 1, None, [],
               [(jax.ShapeDtypeStruct((T, N), out_dtype), pl.BlockSpec((bm, bn), lambda i, j, k: (i, j)))],
               epi)[0]


def _mm_tn(name, a, bs_list):
    T, M = a.shape
    N = bs_list[0].shape[1]
    bmr = min(COL_TILE, M)
    bn = min(COL_TILE, N)
    n = len(bs_list)

    def epi(accs, ex, outs):
        for q in range(n):
            outs[q][...] = accs[q].astype(BF)

    ops = [(a, pl.BlockSpec((T, bmr), lambda r, j, k: (0, r)))]
    ops += [(b, pl.BlockSpec((T, bn), lambda r, j, k: (0, j))) for b in bs_list]
    return _mm(name, (M // bmr, N // bn, 1), ops, [(0, 1 + q, TN, q) for q in range(n)], n, None, [],
               [(jax.ShapeDtypeStruct((M, N), BF), pl.BlockSpec((bmr, bn), lambda r, j, k: (r, j)))] * n, epi)


def _proj_bwd_x(dproj, wing):
    T = dproj.shape[0]
    _, D, ws = wing.shape
    bm = min(ROW_TILE, T)
    row = pl.BlockSpec((bm, D), lambda i, j, s: (i, 0))

    def epi(accs, ex, outs):
        outs[0][...] = accs[0]

    return _mm("proj_bwd_x", (T // bm, 1, N_DEV),
               [(dproj, pl.BlockSpec((bm, ws), lambda i, j, s: (i, s))),
                (wing, pl.BlockSpec((None, D, ws), lambda i, j, s: (s, 0, 0)))],
               [(0, 1, NT, 0)], 1, (bm, D), [], [(jax.ShapeDtypeStruct((T, D), F32), row)], epi,
               ncol=max(1, D // COL_TILE))[0]


def _proj_dw(u, dproj, ws):
    T, D = u.shape
    bmr = min(COL_TILE, D)

    def epi(accs, ex, outs):
        outs[0][...] = accs[0].astype(BF)

    return _mm("proj_dw", (N_DEV, D // bmr, 1),
               [(u, pl.BlockSpec((T, bmr), lambda s, r, k: (0, r))),
                (dproj, pl.BlockSpec((T, ws), lambda s, r, k: (0, s)))],
               [(0, 1, TN, 0)], 1, None, [],
               [(jax.ShapeDtypeStruct((N_DEV, D, ws), BF),
                 pl.BlockSpec((None, bmr, ws), lambda s, r, k: (s, r, 0)))], epi)[0]


def _mixer_ab_fwd(h, gain, wing, conv_w, wout, tabs, nb, seq, comm_proj=None, comm_attn=None, comm_out=None):
    cosf, sinf, bias = tabs
    width = wing.shape[-1] * N_DEV // 6
    nh = width // LANES
    u = _rms_fwd(h, gain, BF)
    proj = _proj_fwd(u, wing, comm=comm_proj)
    qkv = _rope_fwd(proj, cosf, sinf, seq, nh)
    cat, lse = _attn_fwd(qkv, bias, nb, seq, nh, comm=comm_attn)
    cat = _conv_fwd(proj, conv_w, cat, nb, seq, width)
    return _res_mm("outproj_fwd", cat, wout, h, comm=comm_out), (h, u, proj, qkv, cat, lse)


def _mixer_ab_bwd(dh, dhb, saved, gain, wing, conv_w, wout, tabs, nb, seq, reduce_start, carry):
    cosf, sinf, bias = tabs
    h, u, proj, qkv, cat, lse = saved
    D = h.shape[1]
    ws = wing.shape[-1]
    width = ws * N_DEV // 6
    nh = width // LANES
    dcat = _mm_nt("outproj_bwd_x", dhb, wout, F32)
    dwout = _mm_tn("outproj_dw", cat, [dhb])[0]
    comm = _merge_comms(reduce_start(["ab_w_out"], [dwout.reshape(N_DEV, -1, D)]) + [carry])
    dq, dk, dv = _attn_bwd(qkv, cat, dcat, lse, bias, nb, seq, nh, comm=comm)
    dgb, dgc, dxin, dconvw = _conv_bwd(proj, conv_w, dcat, nb, seq, width)
    dproj = _assemble_dproj(dq, dk, dv, dgb, dgc, dxin, cosf, sinf, seq)
    du = _proj_bwd_x(dproj, wing)
    comm, = reduce_start(["ab_w_in"], [_proj_dw(u, dproj, ws)])
    dh_in, dhb_in, dgain = _rms_bwd(du, h, gain, dh)
    return dh_in, dhb_in, dgain, dconvw, comm


def _s5_zoh(lr, li, log_dt):
    dt = jnp.exp(log_dt)
    mag = jnp.exp(lr * dt)
    ar = mag * jnp.cos(li * dt)
    ai = mag * jnp.sin(li * dt)
    den = lr * lr + li * li
    return dt, ar, ai, den, ((ar - 1.0) * lr + ai * li) / den, (ai * lr - (ar - 1.0) * li) / den


def _s5_discretize(lam_re, lam_im, log_dt, bt_re, bt_im):
    def body(lr_ref, li_ref, ld_ref, br_ref, bi_ref, ar_ref, ai_ref, bbr_ref, bbi_ref):
        _, ar, ai, _, fr, fi = _s5_zoh(lr_ref[...], li_ref[...], ld_ref[...])
        ar_ref[...] = ar
        ai_ref[...] = ai
        bbr_ref[...] = fr * br_ref[...] - fi * bi_ref[...]
        bbi_ref[...] = fr * bi_ref[...] + fi * br_ref[...]

    small = jax.ShapeDtypeStruct(lam_re.shape, F32)
    big = jax.ShapeDtypeStruct(bt_re.shape, F32)
    return pl.pallas_call(body, out_shape=[small, small, big, big], name="s5_discretize",
                          compiler_params=_cparams(0))(lam_re, lam_im, log_dt, bt_re, bt_im)


def _s5_discretize_bwd(lam_re, lam_im, log_dt, bt_re, bt_im, d_ar, d_ai, d_bbr, d_bbi):

    def body(lr_ref, li_ref, ld_ref, br_ref, bi_ref, dar_ref, dai_ref, dbbr_ref, dbbi_ref,
             dlr_ref, dli_ref, dld_ref, dbr_ref, dbi_ref):
        lr, li = lr_ref[...], li_ref[...]
        dt, ar, ai, den, fr, fi = _s5_zoh(lr, li, ld_ref[...])
        br, bi = br_ref[...], bi_ref[...]
        dbbr, dbbi = dbbr_ref[...], dbbi_ref[...]
        dbr_ref[...] = dbbr * fr + dbbi * fi
        dbi_ref[...] = dbbi * fr - dbbr * fi
        dfr = jnp.sum(dbbr * br + dbbi * bi, axis=1, keepdims=True)
        dfi = jnp.sum(dbbi * br - dbbr * bi, axis=1, keepdims=True)
        dnr = dfr / den
        dni = dfi / den
        dden = -(dfr * fr + dfi * fi) / den
        dar = dar_ref[...] + dnr * lr - dni * li
        dai = dai_ref[...] + dnr * li + dni * lr
        dlr_ref[...] = dnr * (ar - 1.0) + dni * ai + 2.0 * dden * lr + dt * (dar * ar + dai * ai)
        dli_ref[...] = dnr * ai - dni * (ar - 1.0) + 2.0 * dden * li + dt * (dai * ar - dar * ai)
        ddt = jnp.sum(dar * (lr * ar - li * ai) + dai * (lr * ai + li * ar), axis=2, keepdims=True)
        dld_ref[...] = ddt * dt

    small = jax.ShapeDtypeStruct(lam_re.shape, F32)
    big = jax.ShapeDtypeStruct(bt_re.shape, F32)
    return pl.pallas_call(
        body, out_shape=[small, small, jax.ShapeDtypeStruct(log_dt.shape, F32), big, big],
        name="s5_discretize_bwd", compiler_params=_cparams(0))(
            lam_re, lam_im, log_dt, bt_re, bt_im, d_ar, d_ai, d_bbr, d_bbi)


def _rows8(t):
    return pl.ds(pl.multiple_of(t * SUBLANES, SUBLANES), SUBLANES)


def _cmul_add(ar, ai, sr, si, br, bi):
    return ar * sr - ai * si + br, ar * si + ai * sr + bi


def _scan(a, read, write, init, n):
    def step(t, c):
        s = _cmul_add(*a, *c, *read(t))
        if write is not None:
            write(t, s)
        return s

    return lax.fori_loop(0, n, step, init, unroll=SCAN_UNROLL)


def _cpow(ar, ai, n):
    rr = ri = None
    while n:
        if n & 1:
            rr, ri = (ar, ai) if rr is None else (rr * ar - ri * ai, rr * ai + ri * ar)
        ar, ai = ar * ar - ai * ai, 2.0 * ar * ai
        n >>= 1
    return rr, ri


def _s5_specs(R, nj):
    sh = STATE_COLS
    return dict(
        rows=pl.BlockSpec((R, LANES), lambda j: (0, j)),
        bd=pl.BlockSpec((None, LANES, sh), lambda j: (j, 0, 0)),
        cd=pl.BlockSpec((None, sh, LANES), lambda j: (j, 0, 0)),
        a=pl.BlockSpec((None, 1, sh), lambda j: (j, 0, 0)),
        vec=pl.BlockSpec((1, LANES), lambda j: (0, j)),
        init=pl.BlockSpec((None, SUBLANES, sh), lambda j: (j, 0, 0)))


def _s5_fwd(u, mats, seg_len, nseg, comm=None):
    bdr, bdi, cdr, cdi, are, aim, dsk = mats
    R, D = u.shape
    nj = D // LANES
    sh = STATE_COLS
    rc = min(R, 512)
    sp = _s5_specs(R, nj)

    def body(u_ref, bdr_ref, bdi_ref, cdr_ref, cdi_ref, ar_ref, ai_ref, d_ref,
             y_ref, yg_ref, ir_ref, ii_ref, sre, sim):
        ar = jnp.broadcast_to(ar_ref[...], (SUBLANES, sh))
        ai = jnp.broadcast_to(ai_ref[...], (SUBLANES, sh))

        def bu_chunk(c, _):
            rows = pl.ds(pl.multiple_of(c * rc, rc), rc)
            ub = u_ref[rows, :].astype(BF)
            sre[rows, :] = _dot(ub, bdr_ref[...], NN)
            sim[rows, :] = _dot(ub, bdi_ref[...], NN)
            return 0

        lax.fori_loop(0, R // rc, bu_chunk, 0)
        z = jnp.zeros((SUBLANES, sh), F32)

        def read(t):
            return sre[_rows8(t), :], sim[_rows8(t), :]

        def write(t, s):
            sre[_rows8(t), :] = s[0]
            sim[_rows8(t), :] = s[1]

        er, ei = _scan((ar, ai), read, None, (z, z), seg_len)
        pr, pi = _cpow(ar, ai, seg_len)
        first = (lax.broadcasted_iota(jnp.int32, (SUBLANES, sh), 0) & (nseg - 1)) == 0

        def prev(x):
            return jnp.where(first, 0.0, pltpu.roll(x, 1, 0))

        xr, xi = er, ei
        for _ in range(nseg - 1):
            xr, xi = _cmul_add(pr, pi, prev(xr), prev(xi), er, ei)
        i_r, i_i = prev(xr), prev(xi)
        ir_ref[...] = i_r
        ii_ref[...] = i_i
        _scan((ar, ai), read, write, (i_r, i_i), seg_len)

        def y_chunk(c, _):
            rows = pl.ds(pl.multiple_of(c * rc, rc), rc)
            y = _dot(sre[rows, :], cdr_ref[...], NN) + _dot(sim[rows, :], cdi_ref[...], NN)
            y = y + d_ref[...] * u_ref[rows, :]
            y_ref[rows, :] = y
            yg_ref[rows, :] = _gelu(y).astype(BF)
            return 0

        lax.fori_loop(0, R // rc, y_chunk, 0)

    init_sh = jax.ShapeDtypeStruct((nj, SUBLANES, STATE_COLS), F32)
    return _call(
        body, "s5_fwd", (nj,),
        [sp["rows"], sp["bd"], sp["bd"], sp["cd"], sp["cd"], sp["a"], sp["a"], sp["vec"]],
        [sp["rows"], sp["rows"], sp["init"], sp["init"]],
        [jax.ShapeDtypeStruct((R, D), F32), jax.ShapeDtypeStruct((R, D), BF), init_sh, init_sh],
        (u, bdr, bdi, cdr, cdi, are, aim, dsk),
        scratch=[pltpu.VMEM((R, sh), F32) for _ in range(2)], comm=comm)


def _s5_bwd(u, dy, mats, init_re, init_im, seg_len, nseg, comm=None):
    bdr, bdi, cdr, cdi, are, aim, dsk = mats
    R, D = u.shape
    nj = D // LANES
    sh = STATE_COLS
    rc = min(R, 512)
    sp = _s5_specs(R, nj)

    def body(u_ref, dy_ref, bdr_ref, bdi_ref, cdr_ref, cdi_ref, ar_ref, ai_ref, d_ref, ir_ref, ii_ref,
             du_ref, dbdr_ref, dbdi_ref, dcdr_ref, dcdi_ref, dar_ref, dai_ref, dd_ref,
             sre, sim, gre, gim):
        ar = jnp.broadcast_to(ar_ref[...], (SUBLANES, sh))
        ai = jnp.broadcast_to(ai_ref[...], (SUBLANES, sh))
        i_r, i_i = ir_ref[...], ii_ref[...]

        def chunk(c):
            return pl.ds(pl.multiple_of(c * rc, rc), rc)

        def bu_chunk(c, _):
            ub = u_ref[chunk(c), :].astype(BF)
            dyb = dy_ref[chunk(c), :].astype(BF)
            sre[chunk(c), :] = _dot(ub, bdr_ref[...], NN)
            sim[chunk(c), :] = _dot(ub, bdi_ref[...], NN)
            gre[chunk(c), :] = _dot(dyb, cdr_ref[...], NT)
            gim[chunk(c), :] = _dot(dyb, cdi_ref[...], NT)
            return 0

        lax.fori_loop(0, R // rc, bu_chunk, 0)

        def read_s(t):
            return sre[_rows8(t), :], sim[_rows8(t), :]

        def read_g(t):
            return gre[_rows8(t), :], gim[_rows8(t), :]

        def both(i, c):
            s = _cmul_add(ar, ai, c[0], c[1], *read_s(i))
            sre[_rows8(i), :], sim[_rows8(i), :] = s
            return (*s, *_cmul_add(ar, -ai, c[2], c[3], *read_g(seg_len - 1 - i)))

        z = jnp.zeros((SUBLANES, sh), F32)
        _, _, fr, fi = lax.fori_loop(0, seg_len, both, (i_r, i_i, z, z), unroll=SCAN_UNROLL)

        def c_chunk(c, carry):
            dyb = dy_ref[chunk(c), :].astype(BF)
            return (carry[0] + _dot(sre[chunk(c), :], dyb, TN), carry[1] + _dot(sim[chunk(c), :], dyb, TN))

        zc = jnp.zeros((sh, LANES), F32)
        dcr, dci = lax.fori_loop(0, R // rc, c_chunk, (zc, zc))
        dcdr_ref[...] = dcr
        dcdi_ref[...] = dci
        pr, pi = _cpow(ar, ai, seg_len)
        last =(lax.broadcasted_iota(jnp.int32, (SUBLANES, sh), 0) & (nseg - 1)) == nseg - 1

        def nxt(x):
            return jnp.where(last, 0.0, pltpu.roll(x, SUBLANES - 1, 0))

        xr, xi = fr, fi
        for _ in range(nseg - 1):
            xr, xi = _cmul_add(pr, -pi, nxt(xr), nxt(xi), fr, fi)
        g0r, g0i = nxt(xr), nxt(xi)

        def adj_step(t, c, s_before):
            gr, gi = _cmul_add(ar, -ai, c[0], c[1], *read_g(t))
            gre[_rows8(t), :], gim[_rows8(t), :] = gr, gi
            spr, spi = s_before
            return gr, gi, c[2] + spr * gr + spi * gi, c[3] + spr * gi - spi * gr

        carry = lax.fori_loop(0, seg_len - 1, lambda i, c: adj_step(seg_len - 1 - i, c, read_s(seg_len - 2 - i)),
                              (g0r, g0i, z, z))
        carry = adj_step(0, carry, (i_r, i_i))
        dar_ref[...] = jnp.sum(carry[2], axis=0, keepdims=True)
        dai_ref[...] = jnp.sum(carry[3], axis=0, keepdims=True)

        def d_chunk(c, carry):
            ub = u_ref[chunk(c), :].astype(BF)
            grb = gre[chunk(c), :].astype(BF)
            gib = gim[chunk(c), :].astype(BF)
            du = _dot(grb, bdr_ref[...], NT) + _dot(gib, bdi_ref[...], NT)
            du_ref[chunk(c), :] = du + d_ref[...] * dy_ref[chunk(c), :]
            dd = carry[2] + jnp.sum(dy_ref[chunk(c), :] * u_ref[chunk(c), :], axis=0, keepdims=True)
            return carry[0] + _dot(ub, grb, TN), carry[1] + _dot(ub, gib, TN), dd

        zb = jnp.zeros((LANES, sh), F32)
        dbr, dbi, dd = lax.fori_loop(0, R // rc, d_chunk, (zb, zb, jnp.zeros((1, LANES), F32)))
        dbdr_ref[...] = dbr
        dbdi_ref[...] = dbi
        dd_ref[...] = dd

    bd_sh = jax.ShapeDtypeStruct((nj, LANES, STATE_COLS), F32)
    cd_sh = jax.ShapeDtypeStruct((nj, STATE_COLS, LANES), F32)
    a_sh = jax.ShapeDtypeStruct((nj, 1, STATE_COLS), F32)
    return _call(
        body, "s5_bwd", (nj,),
        [sp["rows"], sp["rows"], sp["bd"], sp["bd"], sp["cd"], sp["cd"], sp["a"], sp["a"],
         sp["vec"], sp["init"], sp["init"]],
        [sp["rows"], sp["bd"], sp["bd"], sp["cd"], sp["cd"], sp["a"], sp["a"], sp["vec"]],
        [jax.ShapeDtypeStruct((R, D), F32), bd_sh, bd_sh, cd_sh, cd_sh, a_sh, a_sh,
         jax.ShapeDtypeStruct((1, D), F32)],
        (u, dy, bdr, bdi, cdr, cdi, are, aim, dsk, init_re, init_im),
        scratch=[pltpu.VMEM((R, sh), F32) for _ in range(4)], comm=comm)


def _glu_fwd(yg, wa, wb, h):
    T, D = yg.shape
    N = wa.shape[1]
    bm = min(ROW_TILE, T)
    bn = min(ROW_TILE, N)
    wspec = pl.BlockSpec((D, bn), lambda i, j, k: (0, j))
    ospec = pl.BlockSpec((bm, bn), lambda i, j, k: (i, j))

    def epi(accs, ex, outs):
        pa, pb = accs
        outs[0][...] = ex[0][...] + pa * _sig(pb)
        outs[1][...] = pa.astype(BF)
        outs[2][...] = pb.astype(BF)

    return _mm("glu_fwd", (T // bm, N // bn, 1),
               [(yg, pl.BlockSpec((bm, D), lambda i, j, k: (i, 0))), (wa, wspec), (wb, wspec)],
               [(0, 1, NN, 0), (0, 2, NN, 1)], 2, None, [(h, ospec)],
               [(jax.ShapeDtypeStruct((T, N), F32), ospec), (jax.ShapeDtypeStruct((T, N), BF), ospec),
                (jax.ShapeDtypeStruct((T, N), BF), ospec)], epi)


def _glu_bwd_gates(dz, pa, pb):
    T, D = dz.shape
    bm = min(ROW_TILE, T)

    def body(dz_ref, pa_ref, pb_ref, dpa_ref, dpb_ref):
        dz = dz_ref[...]
        sg = _sig(pb_ref[...].astype(F32))
        dpa_ref[...] = (dz * sg).astype(BF)
        dpb_ref[...] = (dz * pa_ref[...].astype(F32) * sg * (1.0 - sg)).astype(BF)

    row = pl.BlockSpec((bm, D), lambda i: (i, 0))
    return pl.pallas_call(
        body, grid=(T // bm,), in_specs=[row] * 3, out_specs=[row] * 2,
        out_shape=[jax.ShapeDtypeStruct((T, D), BF)] * 2, name="glu_bwd_gates",
        compiler_params=_cparams(1))(dz, pa, pb)


def _glu_bwd_y(dpa, dpb, wa, wb, y_pre, comm=None):
    T, N = dpa.shape
    D = wa.shape[0]
    bm = min(ROW_TILE, T)
    bn = min(ROW_TILE, D)
    aspec = pl.BlockSpec((bm, N), lambda i, j, k: (i, 0))
    wspec = pl.BlockSpec((bn, N), lambda i, j, k: (j, 0))
    ospec = pl.BlockSpec((bm, bn), lambda i, j, k: (i, j))

    def epi(accs, ex, outs):
        outs[0][...] = accs[0] * _gelu_grad(ex[0][...])

    return _mm("glu_bwd_y", (T // bm, D // bn, 1), [(dpa, aspec), (wa, wspec), (dpb, aspec), (wb, wspec)],
               [(0, 1, NT, 0), (2, 3, NT, 0)], 1, None, [(y_pre, ospec)],
               [(jax.ShapeDtypeStruct((T, D), F32), ospec)], epi, comm=comm)[0]


def _block_diag_in(x, nj):
    g = GROUPS_PER_BLOCK
    x = x.reshape(nj, g, 1, S5_GROUP, S5_STATE)
    eye = jnp.eye(g, dtype=bool)[None, :, :, None, None]
    full = jnp.where(eye, x, 0.0)
    return full.transpose(0, 1, 3, 2, 4).reshape(nj, g * S5_GROUP, g * S5_STATE)


def _block_diag_out(x, nj):
    return _block_diag_in(x, nj).transpose(0, 2, 1)


def _diag_of_in(m, nj):
    g = GROUPS_PER_BLOCK
    m5 = m.reshape(nj, g, S5_GROUP, g, S5_STATE)
    d = jnp.diagonal(m5, axis1=1, axis2=3)
    return d.transpose(0, 3, 1, 2).reshape(nj * g, S5_GROUP, S5_STATE)


def _mixer_s5_fwd(h, gain, p, dsk, wa, wb, nb, seq, comm_s5=None):
    T, D = h.shape
    nj = D // LANES
    nseg = SUBLANES // nb
    seg_len = seq // nseg
    G = p["s5_lambda_re"].shape[1]
    lam_re = p["s5_lambda_re"].reshape(G, 1, S5_STATE)
    lam_im = p["s5_lambda_im"].reshape(G, 1, S5_STATE)
    log_dt = p["s5_log_dt"].reshape(G, 1, 1)
    bt_re = p["s5_b_re"][0].transpose(0, 2, 1)
    bt_im = p["s5_b_im"][0].transpose(0, 2, 1)
    ar, ai, bbr, bbi = _s5_discretize(lam_re, lam_im, log_dt, bt_re, bt_im)
    mats = (_block_diag_in(bbr, nj).astype(BF), _block_diag_in(bbi, nj).astype(BF),
            _block_diag_out(p["s5_c_re"][0], nj).astype(BF),
            _block_diag_out(-p["s5_c_im"][0], nj).astype(BF),
            ar.reshape(nj, 1, STATE_COLS), ai.reshape(nj, 1, STATE_COLS), dsk)
    h_seg = _to_seg(h, seg_len)
    u = _rms_fwd(h_seg, gain, F32)
    y_pre, yg, init_re, init_im = _s5_fwd(u, mats, seg_len, nseg, comm=comm_s5)
    h_out, pa, pb = _glu_fwd(yg, wa, wb, h_seg)
    disc_in = (lam_re, lam_im, log_dt, bt_re, bt_im)
    return _to_tok(h_out, seg_len), (h_seg, u, mats, y_pre, yg, init_re, init_im, pa, pb, disc_in, seg_len, nseg)


def _mixer_s5_bwd(dh, saved, gain, wa, wb, reduce_start, carry):
    h_seg, u, mats, y_pre, yg, init_re, init_im, pa, pb, disc_in, seg_len, nseg = saved
    T, D = h_seg.shape
    nj = D // LANES
    G = nj * GROUPS_PER_BLOCK
    dh_seg = _to_seg(dh, seg_len)
    dpa, dpb = _glu_bwd_gates(dh_seg, pa, pb)
    dy = _glu_bwd_y(dpa, dpb, wa, wb, y_pre)
    dwa, dwb = _mm_tn("glu_dw", yg, [dpa, dpb])
    comm = _merge_comms(reduce_start(["s5_glu_wa", "s5_glu_wb"],
                                     [dwa.reshape(N_DEV, -1, D), dwb.reshape(N_DEV, -1, D)]) + [carry])
    du, dbdr, dbdi, dcdr, dcdi, dar, dai, dd = _s5_bwd(u, dy, mats, init_re, init_im, seg_len, nseg, comm=comm)
    d_bbr = _diag_of_in(dbdr, nj)
    d_bbi = _diag_of_in(dbdi, nj)
    d_c_re = _diag_of_in(dcdr.transpose(0, 2, 1), nj)
    d_c_im = -_diag_of_in(dcdi.transpose(0, 2, 1), nj)
    dlr, dli, dld, dbr, dbi = _s5_discretize_bwd(
        *disc_in, dar.reshape(G, 1, S5_STATE), dai.reshape(G, 1, S5_STATE), d_bbr, d_bbi)
    small = {"s5_lambda_re": dlr.reshape(1, G, S5_STATE), "s5_lambda_im": dli.reshape(1, G, S5_STATE),
             "s5_log_dt": dld.reshape(1, G),
             "s5_b_re": dbr.transpose(0, 2, 1)[None], "s5_b_im": dbi.transpose(0, 2, 1)[None],
             "s5_c_re": d_c_re[None], "s5_c_im": d_c_im[None], "s5_d": dd}
    dh_in, _, dgain = _rms_bwd(du, h_seg, gain, dh_seg)
    dh_in = _to_tok(dh_in, seg_len)
    return dh_in, dh_in.astype(BF), dgain, small


def _mesh_pos():
    return lax.axis_index("x"), lax.axis_index("y"), lax.axis_index("c")


class _Gather:
    def __init__(self, srcs, slots, send_sems, recv_sems):
        self.srcs, self.slots, self.send_sems, self.recv_sems = srcs, slots, send_sems, recv_sems
        x, y, c = _mesh_pos()
        self.c = c
        self.me, self.sib = (x, y, c), (x, y, 1 - c)
        self.chips = [(1 - x, y), (x, 1 - y), (1 - x, 1 - y)]

    def copy(self, a, k, block, to, own=False):
        dst = self.slots[a].at[4 * block[0] + 2 * block[1] + block[2]]
        return pltpu.make_async_remote_copy(
            src_ref=self.srcs[a] if own else dst, dst_ref=dst, send_sem=self.send_sems.at[7 * a + k],
            recv_sem=self.recv_sems.at[7 * a + k], device_id=to, device_id_type=MESH)

    def own_copies(self, a):
        cps = [self.copy(a, 0, self.me, self.sib, own=True)]
        return cps + [self.copy(a, 1 + j, self.me, (*chip, self.c), own=True) for j, chip in enumerate(self.chips)]

    def start(self):
        for a in range(len(self.srcs)):
            for cp in self.own_copies(a):
                cp.start()

    def finish(self):
        n = len(self.srcs)
        for a in range(n):
            for j, chip in enumerate(self.chips):
                self.copy(a, 1 + j, (*chip, self.c), self.me).wait_recv()
                self.copy(a, 4 + j, (*chip, self.c), self.sib).start()
        for a in range(n):
            self.copy(a, 0, self.sib, self.me).wait_recv()
            for j, chip in enumerate(self.chips):
                self.copy(a, 4 + j, (*chip, 1 - self.c), self.me).wait_recv()
        for a in range(n):
            for cp in self.own_copies(a):
                cp.wait_send()
            for j, chip in enumerate(self.chips):
                self.copy(a, 4 + j, (*chip, self.c), self.sib).wait_send()


def _gather_comm(arrs):
    n = len(arrs)

    def local(xs, outs, sems, a):
        x, y, c = _mesh_pos()
        return pltpu.make_async_copy(xs[a], outs[a].at[4 * x + 2 * y + c], sems[2].at[a])

    def start(xs, outs, sems):
        for a in range(n):
            local(xs, outs, sems, a).start()
        _Gather(xs, outs, sems[0], sems[1]).start()

    def finish(xs, outs, sems):
        _Gather(xs, outs, sems[0], sems[1]).finish()
        for a in range(n):
            local(xs, outs, sems, a).wait()

    return _Comm(list(arrs), [jax.ShapeDtypeStruct((N_DEV,) + a.shape, a.dtype) for a in arrs],
                 [pltpu.SemaphoreType.DMA((7 * n,)), pltpu.SemaphoreType.DMA((7 * n,)),
                  pltpu.SemaphoreType.DMA((n,))], start, finish)


def _exchange_comm(parts):
    n = len(parts)

    def copies(ps, outs, sems):
        x, y, c = _mesh_pos()
        cps = []
        for a in range(n):
            for j in range(1, 4):
                to = (jnp.bitwise_xor(x, j // 2), jnp.bitwise_xor(y, j % 2), c)
                cps.append(pltpu.make_async_remote_copy(
                    src_ref=ps[a].at[j], dst_ref=outs[a].at[j - 1], send_sem=sems[0].at[3 * a + j - 1],
                    recv_sem=sems[1].at[3 * a + j - 1], device_id=to, device_id_type=MESH))
        return cps

    def start(ps, outs, sems):
        for cp in copies(ps, outs, sems):
            cp.start()

    def finish(ps, outs, sems):
        for cp in copies(ps, outs, sems):
            cp.wait()

    return _Comm(list(parts), [jax.ShapeDtypeStruct((3,) + p.shape[1:], p.dtype) for p in parts],
                 [pltpu.SemaphoreType.DMA((3 * n,)), pltpu.SemaphoreType.DMA((3 * n,))], start, finish)


def _run_comm(comm, name):
    ci, co = len(comm.ins), len(comm.outs)

    def body(*refs):
        comm.start(refs[:ci], refs[ci:ci + co], refs[ci + co:])
        comm.finish(refs[:ci], refs[ci:ci + co], refs[ci + co:])

    any_spec = pl.BlockSpec(memory_space=pl.ANY)
    comm.set_results(pl.pallas_call(
        body, in_specs=[any_spec] * ci, out_specs=[any_spec] * co, out_shape=list(comm.outs),
        scratch_shapes=list(comm.sems), name=name, compiler_params=_cparams(0))(*comm.ins))


def _pair_exchange(grads, name):
    n = len(grads)

    def body(*refs):
        gs, outs = refs[:n], refs[n:2 * n]
        send_sems, recv_sems = refs[2 * n:]
        x, y, c = _mesh_pos()
        copies = []
        for a in range(n):
            for k in range(4):
                copies.append(pltpu.make_async_remote_copy(
                    src_ref=gs[a].at[2 * k + 1 - c], dst_ref=outs[a].at[k], send_sem=send_sems.at[4 * a + k],
                    recv_sem=recv_sems.at[4 * a + k], device_id=(x, y, 1 - c), device_id_type=MESH))
        for cp in copies:
            cp.start()
        for cp in copies:
            cp.wait()

    any_spec = pl.BlockSpec(memory_space=pl.ANY)
    return pl.pallas_call(
        body, in_specs=[any_spec] * n, out_specs=[any_spec] * n,
        out_shape=[jax.ShapeDtypeStruct((4,) + g.shape[1:], g.dtype) for g in grads],
        scratch_shapes=[pltpu.SemaphoreType.DMA((4 * n,)), pltpu.SemaphoreType.DMA((4 * n,))],
        name=name, compiler_params=_cparams(0))(*grads)


def _pair_sum(grad, recv, pos):
    _, R, C = grad.shape
    br = _row_block(R, C, PAIR_SUM_ELEMS)

    def body(pos_ref, g_ref, r_ref, o_ref):
        o_ref[...] = (g_ref[...].astype(F32) + r_ref[...].astype(F32)).astype(BF)

    def chip(j, p):
        return jnp.bitwise_xor(p[1], j)

    return pl.pallas_call(
        body, grid_spec=pltpu.PrefetchScalarGridSpec(
            num_scalar_prefetch=1, grid=(4, R // br),
            in_specs=[pl.BlockSpec((None, br, C), lambda j, i, p: (2 * chip(j, p) + p[0], i, 0)),
                      pl.BlockSpec((None, br, C), lambda j, i, p: (chip(j, p), i, 0))],
            out_specs=pl.BlockSpec((None, br, C), lambda j, i, p: (j, i, 0))),
        out_shape=jax.ShapeDtypeStruct((4, R, C), BF), name="pair_sum", compiler_params=_cparams(2))(pos, grad, recv)


def _adamw(w, g, m, v):
    m = ADAM_B1 * m + (1.0 - ADAM_B1) * g
    v = ADAM_B2 * v + (1.0 - ADAM_B2) * (g * g)
    m_hat = m / (1.0 - ADAM_B1 ** ADAM_STEP)
    v_hat = v / (1.0 - ADAM_B2 ** ADAM_STEP)
    return -ADAM_LR * (m_hat / (jnp.sqrt(v_hat) + ADAM_EPS) + ADAM_WD * w), m, v


def _adamw_piece(w, m, v, piece, part, recv, bufs):
    _, R, C = w.shape
    br = _row_block(R, C)

    def body(w_ref, m_ref, v_ref, p_ref, r_ref, b0, b1, b2, b3, g_ref, d_ref, nm_ref, nv_ref):
        g = p_ref[...].astype(F32)
        for j in range(3):
            g = g + r_ref[j].astype(F32)
        d, nm, nv = _adamw(w_ref[...], g, m_ref[...], v_ref[...])
        g_ref[...] = g
        d_ref[...] = d
        nm_ref[...] = nm
        nv_ref[...] = nv

    row = pl.BlockSpec((None, br, C), lambda i: (piece, i, 0))
    any_spec = pl.BlockSpec(memory_space=pl.ANY)
    return pl.pallas_call(
        body, grid=(R // br,),
        in_specs=[row, row, row, pl.BlockSpec((None, br, C), lambda i: (0, i, 0)),
                  pl.BlockSpec((3, br, C), lambda i: (0, i, 0))] + [any_spec] * 4,
        out_specs=[row] * 4, out_shape=[jax.ShapeDtypeStruct(w.shape, F32)] * 4,
        input_output_aliases={5: 0, 6: 1, 7: 2, 8: 3}, name="adamw_piece",
        compiler_params=_cparams(1))(w, m, v, part, recv, *bufs)


def _all_reduce_small(x):
    rows = x.shape[0]

    def body(x_ref, o_ref, buf, send_sems, recv_sems):
        xp, yp, cp = _mesh_pos()
        buf[4 * xp + 2 * yp + cp] = x_ref[...]
        gather = _Gather([x_ref], [buf], send_sems, recv_sems)
        gather.start()
        gather.finish()
        acc = buf[0]
        for d in range(1, N_DEV):
            acc = acc + buf[d]
        o_ref[...] = acc

    vm = pl.BlockSpec(memory_space=pltpu.VMEM)
    return pl.pallas_call(
        body, in_specs=[vm], out_specs=vm, out_shape=jax.ShapeDtypeStruct(x.shape, F32),
        scratch_shapes=[pltpu.VMEM((N_DEV, rows, LANES), F32), pltpu.SemaphoreType.DMA((7,)),
                        pltpu.SemaphoreType.DMA((7,))],
        name="all_reduce_small", compiler_params=_cparams(0))(x)


def _sum_slots(x):
    def body(x_ref, o_ref):
        acc = x_ref[0]
        for d in range(1, N_DEV):
            acc = acc + x_ref[d]
        o_ref[...] = acc

    return pl.pallas_call(body, out_shape=jax.ShapeDtypeStruct(x.shape[1:], F32), name="sum_slots",
                          compiler_params=_cparams(0))(x)


def _adamw_small(w, g, m, v):
    def body(w_ref, g_ref, m_ref, v_ref, d_ref, nm_ref, nv_ref):
        d, nm, nv = _adamw(w_ref[...], g_ref[...], m_ref[...], v_ref[...])
        d_ref[...] = d
        nm_ref[...] = nm
        nv_ref[...] = nv

    sh = jax.ShapeDtypeStruct(w.shape, F32)
    return pl.pallas_call(body, out_shape=[sh] * 3, name="adamw_small", compiler_params=_cparams(0))(w, g, m, v)


def _pack(arrs):
    flat = jnp.concatenate([a.reshape(-1).astype(F32) for a in arrs])
    rows = -(-flat.shape[0] // (SUBLANES * LANES)) * SUBLANES
    return jnp.pad(flat, (0, rows * LANES - flat.shape[0])).reshape(rows, LANES)


def _unpack(buf, shapes):
    flat = buf.reshape(-1)
    out, off = [], 0
    for s in shapes:
        n = 1
        for d in s:
            n *= d
        out.append(flat[off:off + n].reshape(s))
        off += n
    return out


BIG = ("ffn_w1", "ffn_w3", "ffn_w2", "ab_w_in", "ab_w_out", "s5_glu_wa", "s5_glu_wb")
NAMES = ("ln_ffn_pre", "ln_mix", "ln_ffn_post", "ln_final", "ffn_w1", "ffn_w3", "ffn_w2", "ab_w_in",
         "ab_conv_w", "ab_w_out", "s5_lambda_re", "s5_lambda_im", "s5_log_dt", "s5_b_re", "s5_b_im",
         "s5_c_re", "s5_c_im", "s5_d", "s5_glu_wa", "s5_glu_wb")


def kernel(x, ln_ffn_pre, ln_mix, ln_ffn_post, ln_final, ffn_w1, ffn_w3, ffn_w2, ab_w_in, ab_conv_w, ab_w_out, s5_lambda_re, s5_lambda_im, s5_log_dt, s5_b_re, s5_b_im, s5_c_re, s5_c_im, s5_d, s5_glu_wa, s5_glu_wb, loss_target, m_ln_ffn_pre, m_ln_mix, m_ln_ffn_post, m_ln_final, m_ffn_w1, m_ffn_w3, m_ffn_w2, m_ab_w_in, m_ab_conv_w, m_ab_w_out, m_s5_lambda_re, m_s5_lambda_im, m_s5_log_dt, m_s5_b_re, m_s5_b_im, m_s5_c_re, m_s5_c_im, m_s5_d, m_s5_glu_wa, m_s5_glu_wb, v_ln_ffn_pre, v_ln_mix, v_ln_ffn_post, v_ln_final, v_ffn_w1, v_ffn_w3, v_ffn_w2, v_ab_w_in, v_ab_conv_w, v_ab_w_out, v_s5_lambda_re, v_s5_lambda_im, v_s5_log_dt, v_s5_b_re, v_s5_b_im, v_s5_c_re, v_s5_c_im, v_s5_d, v_s5_glu_wa, v_s5_glu_wb):
    w = dict(zip(NAMES, (ln_ffn_pre, ln_mix, ln_ffn_post, ln_final, ffn_w1, ffn_w3, ffn_w2, ab_w_in, ab_conv_w,
                         ab_w_out, s5_lambda_re, s5_lambda_im, s5_log_dt, s5_b_re, s5_b_im, s5_c_re, s5_c_im,
                         s5_d, s5_glu_wa, s5_glu_wb)))
    mom = dict(zip(NAMES, (m_ln_ffn_pre, m_ln_mix, m_ln_ffn_post, m_ln_final, m_ffn_w1, m_ffn_w3, m_ffn_w2,
                           m_ab_w_in, m_ab_conv_w, m_ab_w_out, m_s5_lambda_re, m_s5_lambda_im, m_s5_log_dt,
                           m_s5_b_re, m_s5_b_im, m_s5_c_re, m_s5_c_im, m_s5_d, m_s5_glu_wa, m_s5_glu_wb)))
    var = dict(zip(NAMES, (v_ln_ffn_pre, v_ln_mix, v_ln_ffn_post, v_ln_final, v_ffn_w1, v_ffn_w3, v_ffn_w2,
                           v_ab_w_in, v_ab_conv_w, v_ab_w_out, v_s5_lambda_re, v_s5_lambda_im, v_s5_log_dt,
                           v_s5_b_re, v_s5_b_im, v_s5_c_re, v_s5_c_im, v_s5_d, v_s5_glu_wa, v_s5_glu_wb)))
    nb, seq, D = x.shape
    T = nb * seq
    assert ln_mix.shape[0] == 2 and ab_w_in.shape[0] == 1 and s5_glu_wa.shape[0] == 1
    xc, yc, cc = _mesh_pos()
    dev = 4 * xc + 2 * yc + cc
    pos = jnp.stack([cc, 2 * xc + yc]).astype(jnp.int32)
    bq = min(ATTN_TILE, seq)
    tabs =_rope_tables(seq) + (_branch_bias(seq // bq, bq),)

    def ffn_piece(k, li, fj):
        return w[k][li, fj].astype(BF)

    g0 = _gather_comm([ffn_piece("ffn_w1", 0, 0), ffn_piece("ffn_w3", 0, 0), ab_conv_w[0], s5_d])
    _run_comm(g0, "gather_first")
    w1, w3 = {(0, 0): g0.results[0]}, {(0, 0): g0.results[1]}
    w2 = {}
    conv_w = g0.results[2].transpose(1, 0, 2).reshape(3, -1)
    dsk = g0.results[3].reshape(1, D)
    gains = {k: [w[k][i:i + 1] for i in range(2)] for k in ("ln_ffn_pre", "ln_mix", "ln_ffn_post")}

    h = x.reshape(T, D)
    saved = {}

    def ffn_fwd(h, gain, key, tag, comm_up, comm_down, after_up):
        n = _rms_fwd(h, gain, BF)
        t1, t3, g = _ffn_up(n, w1[key], w3[key], comm=comm_up)
        after_up()
        saved[tag] = (h, n, t1, t3, g)
        return _ffn_down(g, w2[key], h, comm=comm_down)

    c_up = _gather_comm([ffn_piece("ffn_w2", 0, 0), ab_w_out[0].astype(BF)])
    c_dn = _gather_comm([ab_w_in[0].astype(BF)])
    h = ffn_fwd(h, gains["ln_ffn_pre"][0], (0, 0), "pre0", c_up, c_dn,
                lambda: w2.update({(0, 0): c_up.results[0]}))
    wout = c_up.results[1].reshape(-1, D)
    wing = c_dn.results[0]
    c_proj = _gather_comm([ffn_piece("ffn_w1", 0, 1)])
    c_attn = _gather_comm([ffn_piece("ffn_w3", 0, 1), s5_glu_wa[0].astype(BF)])
    c_out = _gather_comm([s5_glu_wb[0].astype(BF)])
    h, saved["mix0"] = _mixer_ab_fwd(h, gains["ln_mix"][0], wing, conv_w, wout, tabs, nb, seq, c_proj, c_attn, c_out)
    w1[(0, 1)] = c_proj.results[0]
    w3[(0, 1)] = c_attn.results[0]
    wa = c_attn.results[1].reshape(-1, D)
    wb = c_out.results[0].reshape(-1, D)
    c_up2 = _gather_comm([ffn_piece("ffn_w2", 0, 1), ffn_piece("ffn_w1", 1, 0)])
    c_dn = _gather_comm([ffn_piece("ffn_w3", 1, 0)])
    h = ffn_fwd(h, gains["ln_ffn_post"][0], (0, 1), "post0", c_up2, c_dn,
                lambda: w2.update({(0, 1): c_up2.results[0]}))
    w1[(1, 0)] = c_up2.results[1]
    w3[(1, 0)] = c_dn.results[0]
    c_up3 = _gather_comm([ffn_piece("ffn_w2", 1, 0), ffn_piece("ffn_w1", 1, 1)])
    c_dn = _gather_comm([ffn_piece("ffn_w3", 1, 1)])
    h = ffn_fwd(h, gains["ln_ffn_pre"][1], (1, 0), "pre1", c_up3, c_dn,
                lambda: w2.update({(1, 0): c_up3.results[0]}))
    w1[(1, 1)] = c_up3.results[1]
    w3[(1, 1)] = c_dn.results[0]
    c_s5 = _gather_comm([ffn_piece("ffn_w2", 1, 1)])
    h, saved["mix1"] = _mixer_s5_fwd(h, gains["ln_mix"][1], w, dsk, wa, wb, nb, seq, c_s5)
    w2[(1, 1)] = c_s5.results[0]
    h = ffn_fwd(h, gains["ln_ffn_post"][1], (1, 1), "post1", None, None, lambda: None)
    dh, dhb, d_ln_final, loss_part = _loss_head(h, ln_final.reshape(1, D), loss_target.reshape(T, D))
    loss = lax.psum(loss_part[0, 0], ("x", "y", "c"))

    reduced = {}

    def reduce_start(names, grads):
        recv = _pair_exchange(grads, "pair_exchange")
        comms = []
        for nm, g, r in zip(names, grads, recv):
            part = _pair_sum(g, r, pos)
            comms.append(_exchange_comm([part]))
            reduced[nm] = (part, comms[-1])
        return comms

    def ffn_bwd(dh, dhb, key, tag, gain, carry, is_last=False, comm_dw2=None):
        h_in, n, t1, t3, g = saved[tag]
        da1, da3 = _ffn_bwd_hidden(dhb, w2[key], t1, t3, comm=carry)
        c2, = reduce_start([("ffn_w2",) + key], [_ffn_dw2(g, dhb, comm=comm_dw2)])
        dw1, dw3 = _ffn_dw13(n, da1, da3, comm=c2)
        c1, c3 = reduce_start([("ffn_w1",) + key, ("ffn_w3",) + key], [dw1, dw3])
        res = _ffn_dn_rms(da1, da3, w1[key], w3[key], h_in, gain, dh,
                          comm=_merge_comms([c1, c3]) if is_last else c1)
        return list(res) + [None if is_last else c3]

    g_small = {"ln_final": d_ln_final.reshape(D)}
    g_ln = {k: [None, None] for k in gains}
    dh, dhb, g_ln["ln_ffn_post"][1], carry = ffn_bwd(dh, dhb, (1, 1), "post1", gains["ln_ffn_post"][1], None)
    dh, dhb, g_ln["ln_mix"][1], s5_small = _mixer_s5_bwd(
        dh, saved["mix1"], gains["ln_mix"][1], wa, wb, reduce_start, carry)
    s5_names = list(s5_small)
    c_s5_grads = _gather_comm([_pack([s5_small[k] for k in s5_names])])
    dh, dhb, g_ln["ln_ffn_pre"][1], carry = ffn_bwd(dh, dhb, (1, 0), "pre1", gains["ln_ffn_pre"][1], None,
                                                    comm_dw2=c_s5_grads)
    g_red = dict(zip(s5_names, _unpack(_sum_slots(c_s5_grads.results[0]), [s5_small[k].shape for k in s5_names])))
    dh, dhb, g_ln["ln_ffn_post"][0], carry = ffn_bwd(dh, dhb, (0, 1), "post0", gains["ln_ffn_post"][0], carry)
    dh, dhb, g_ln["ln_mix"][0], g_small["ab_conv_w"], carry = _mixer_ab_bwd(
        dh, dhb, saved["mix0"], gains["ln_mix"][0], wing, conv_w, wout, tabs, nb, seq, reduce_start, carry)
    dh, dhb, g_ln["ln_ffn_pre"][0], _ = ffn_bwd(dh, dhb, (0, 0), "pre0", gains["ln_ffn_pre"][0], carry, is_last=True)
    grad_x = dh.reshape(nb, seq, D)
    for k in g_ln:
        g_small[k] = jnp.concatenate(g_ln[k], axis=0)

    out = {}
    for k in BIG:
        transposed = k in ("ffn_w1", "ffn_w3")
        pieces = [(li, fj) for li in range(2) for fj in range(2)] if w[k].ndim == 4 else [None]

        def view(a):
            a = a.swapaxes(-1, -2) if transposed else a
            return a.reshape(len(pieces), -1, a.shape[-1])

        w3d, m3d, v3d = view(w[k]), view(mom[k]), view(var[k])
        bufs = [lax.empty(w3d.shape, F32) for _ in range(4)]
        for q, key in enumerate(pieces):
            part, comm = reduced[k if key is None else (k,) + key]
            bufs = _adamw_piece(w3d, m3d, v3d, q, part, comm.results[0], bufs)
        if transposed:
            out[k] = [t.reshape(w[k].shape[:2] + w3d.shape[1:]).swapaxes(-1, -2) for t in bufs]
        else:
            out[k] = [t.reshape(w[k].shape) for t in bufs]

    small_names = [k for k in NAMES if k not in BIG]
    late_names = [k for k in small_names if k not in g_red]
    g_red.update(zip(late_names, _unpack(_all_reduce_small(_pack([g_small[k] for k in late_names])),
                                         [g_small[k].shape for k in late_names])))
    cw = w["ab_conv_w"].shape[-1]
    g_red["ab_conv_w"] = lax.dynamic_slice_in_dim(g_red["ab_conv_w"], dev * cw, cw, axis=1)[None]
    dsz = w["s5_d"].shape[-1]
    g_red["s5_d"] = lax.dynamic_slice_in_dim(g_red["s5_d"].reshape(1, -1), dev * dsz, dsz, axis=1)
    shapes = [w[k].shape for k in small_names]
    g_red = {k: g_red[k].reshape(w[k].shape) for k in small_names}
    d_s, m_s, v_s = _adamw_small(_pack([w[k] for k in small_names]), _pack([g_red[k] for k in small_names]),
                                 _pack([mom[k] for k in small_names]), _pack([var[k] for k in small_names]))
    for k, d, nm, nv in zip(small_names, _unpack(d_s, shapes), _unpack(m_s, shapes), _unpack(v_s, shapes)):
        out[k] = [g_red[k], d, nm, nv]

    return (loss, grad_x, *[out[k][0] for k in NAMES], *[out[k][1] for k in NAMES],
            *[out[k][2] for k in NAMES], *[out[k][3] for k in NAMES])
```

```python
import jax
import jax.numpy as jnp
from jax import lax
from jax.experimental import pallas as pl
from jax.experimental.pallas import tpu as pltpu

F32, BF = jnp.float32, jnp.bfloat16
N_DEV = 8
MESH = pl.DeviceIdType.MESH
LANES = 128
SUBLANES = 8
VMEM_LIMIT = 56 * 2 ** 20
ROW_TILE = 512
FFN_ROW_TILE = 1024
COL_TILE = 512
ATTN_TILE = 512
SCAN_UNROLL = 4
ELEMS_PER_BLOCK = 256 * 1024
PAIR_SUM_ELEMS = 2048 * 1024
RMS_EPS = 1e-6
ROPE_THETA = 10000.0
NEG_INF = -1e30
S5_STATE = 64
S5_GROUP = 16
GROUPS_PER_BLOCK = LANES // S5_GROUP
STATE_COLS = GROUPS_PER_BLOCK * S5_STATE
DILATED_PATTERN = ((128, 1), (512, 4), (2048, 16))
ADAM_LR, ADAM_B1, ADAM_B2, ADAM_EPS, ADAM_WD, ADAM_STEP = 0.001, 0.9, 0.999, 1e-08, 0.01, 10
GELU_C = 0.7978845608028654
GELU_A = 0.044715


def _cparams(n_grid, vmem=VMEM_LIMIT):
    sem = ("arbitrary",) * n_grid if n_grid else None
    return pltpu.CompilerParams(dimension_semantics=sem, vmem_limit_bytes=vmem)


def _sig(x):
    return 1.0 / (1.0 + jnp.exp(-x))


def _gelu(x):
    return 0.5 * x * (1.0 + jnp.tanh(GELU_C * (x + GELU_A * x * x * x)))


def _gelu_grad(x):
    t = jnp.tanh(GELU_C * (x + GELU_A * x * x * x))
    return 0.5 * (1.0 + t) + 0.5 * x * (1.0 - t * t) * GELU_C * (1.0 + 3.0 * GELU_A * x * x)


def _dot(a, b, dims):
    a = a if a.dtype == BF else a.astype(BF)
    b = b if b.dtype == BF else b.astype(BF)
    return lax.dot_general(a, b, (dims, ((), ())), preferred_element_type=F32)


NN = ((1,), (0,))
NT = ((1,), (1,))
TN = ((0,), (0,))


def _row_block(rows, cols, elems=ELEMS_PER_BLOCK, mult=16):
    cap = max(mult, elems // cols)
    best = None
    for b in range(mult, min(rows, cap) + 1, mult):
        if rows % b == 0:
            best = b
    return rows if best is None else best


class _Comm:
    def __init__(self, ins, outs, sems, start, finish, members=()):
        self.ins, self.outs, self.sems, self.start, self.finish = ins, outs, sems, start, finish
        self.members = members
        self.results = None

    def set_results(self, res):
        self.results = list(res)
        off = 0
        for m in self.members:
            m.set_results(res[off:off + len(m.outs)])
            off += len(m.outs)


def _merge_comms(comms):
    comms = [c for c in comms if c is not None]
    if len(comms) < 2:
        return comms[0] if comms else None

    def each(fn_name, ins, outs, sems):
        i = o = s = 0
        for c in comms:
            ni, no, ns = len(c.ins), len(c.outs), len(c.sems)
            getattr(c, fn_name)(ins[i:i + ni], outs[o:o + no], sems[s:s + ns])
            i, o, s = i + ni, o + no, s + ns

    return _Comm([a for c in comms for a in c.ins], [a for c in comms for a in c.outs],
                 [a for c in comms for a in c.sems],
                 lambda ins, outs, sems: each("start", ins, outs, sems),
                 lambda ins, outs, sems: each("finish", ins, outs, sems), members=tuple(comms))


def _call(body, name, grid, in_specs, out_specs, out_shape, args, scratch=(), comm=None):
    in_specs, out_specs, out_shape, scratch = list(in_specs), list(out_specs), list(out_shape), list(scratch)
    if comm is None:
        return pl.pallas_call(body, grid=grid, in_specs=in_specs, out_specs=out_specs, out_shape=out_shape,
                              scratch_shapes=scratch, name=name, compiler_params=_cparams(len(grid)))(*args)
    n_in, n_out, n_sc = len(in_specs), len(out_specs), len(scratch)
    ci, co = len(comm.ins), len(comm.outs)

    def hosted(*refs):
        ins, refs = refs[:n_in], refs[n_in:]
        cins, refs = refs[:ci], refs[ci:]
        outs, refs = refs[:n_out], refs[n_out:]
        couts, refs = refs[:co], refs[co:]
        sc, csems = refs[:n_sc], refs[n_sc:]
        first = last = None
        for d, n in enumerate(grid):
            p = pl.program_id(d)
            first = (p == 0) if first is None else first & (p == 0)
            last = (p == n - 1) if last is None else last & (p == n - 1)

        @pl.when(first)
        def _():
            comm.start(cins, couts, csems)

        body(*ins, *outs, *sc)

        @pl.when(last)
        def _():
            comm.finish(cins, couts, csems)

    any_spec = pl.BlockSpec(memory_space=pl.ANY)
    res = pl.pallas_call(
        hosted, grid=grid, in_specs=in_specs + [any_spec] * ci, out_specs=out_specs + [any_spec] * co,
        out_shape=out_shape + list(comm.outs), scratch_shapes=scratch + list(comm.sems), name=name,
        compiler_params=_cparams(len(grid)))(*args, *comm.ins)
    comm.set_results(res[n_out:])
    return list(res[:n_out])


def _mm(name, grid, operands, pairs, n_acc, acc_shape, extras, outs, epilogue, comm=None, nrow=1, ncol=1,
        whole_tile_epilogue=False):
    nk = grid[2]
    n_op, n_ex, n_out = len(operands), len(extras), len(outs)

    def part_of(ref, dim, t, n):
        if n == 1:
            return ref
        size = ref.shape[dim] // n
        idx = [slice(None)] * len(ref.shape)
        idx[dim] = pl.ds(t * size, size)
        return ref.at[tuple(idx)]

    def tile_of(ref, r, c):
        return part_of(part_of(ref, 0, r, nrow), 1, c, ncol)

    def products(op, r, c):
        parts = [None] * n_acc
        for ai, bi, dims, ci in pairs:
            a = part_of(op[ai], 1 - dims[0][0], r, nrow)
            b = part_of(op[bi], 1 - dims[1][0], c, ncol)
            d = _dot(a[...], b[...], dims)
            parts[ci] = d if parts[ci] is None else parts[ci] + d
        return parts

    def body(*refs):
        op = refs[:n_op]
        ex = refs[n_op:n_op + n_ex]
        out = refs[n_op + n_ex:n_op + n_ex + n_out]
        acc = refs[n_op + n_ex + n_out:]
        tiles = [(r, c) for r in range(nrow) for c in range(ncol)]

        def views(refs_, t):
            return [tile_of(q, *t) for q in refs_]

        if nk == 1:
            parts = products(op, *tiles[0])
            for q, t in enumerate(tiles):
                nxt = products(op, *tiles[q + 1]) if q + 1 < len(tiles) else None
                epilogue(parts, views(ex, t), views(out, t))
                parts = nxt
            return
        k = pl.program_id(2)

        @pl.when(k == 0)
        def _():
            for q in acc:
                q[...] = jnp.zeros_like(q)

        for t in tiles:
            parts = products(op, *t)
            for q, p in zip(views(acc, t), parts):
                q[...] += p

        @pl.when(k == nk - 1)
        def _():
            if whole_tile_epilogue:
                epilogue(acc, ex, out)
                return
            for t in tiles:
                epilogue([q[...] for q in views(acc, t)], views(ex, t), views(out, t))

    return _call(body, name, grid, [s for _, s in operands] + [s for _, s in extras], [s for _, s in outs],
                 [sh for sh, _ in outs], [a for a, _ in operands] + [a for a, _ in extras],
                 scratch=[pltpu.VMEM(acc_shape, F32) for _ in range(n_acc if nk > 1 else 0)], comm=comm)


def _to_seg(a, seg_len):
    T, D = a.shape
    return a.reshape(SUBLANES, seg_len, D).transpose(1, 0, 2).reshape(T, D)


def _to_tok(a, seg_len):
    T, D = a.shape
    return a.reshape(seg_len, SUBLANES, D).transpose(1, 0, 2).reshape(T, D)


def _rms_fwd(h, gain, out_dtype):
    T, D = h.shape
    bm = min(ROW_TILE, T)

    def body(h_ref, g_ref, o_ref):
        x = h_ref[...]
        r = lax.rsqrt(jnp.mean(x * x, axis=-1, keepdims=True) + RMS_EPS)
        o_ref[...] = (x * r * g_ref[...]).astype(out_dtype)

    row = pl.BlockSpec((bm, D), lambda i: (i, 0))
    return pl.pallas_call(
        body, grid=(T // bm,), in_specs=[row, pl.BlockSpec((1, D), lambda i: (0, 0))],
        out_specs=row, out_shape=jax.ShapeDtypeStruct((T, D), out_dtype), name="rms_fwd",
        compiler_params=_cparams(1))(h, gain)


def _rms_bwd_rows(dn, x, g):
    r = lax.rsqrt(jnp.mean(x * x, axis=-1, keepdims=True) + RMS_EPS)
    xh = x * r
    dng = dn * g
    dx = r * (dng - xh * jnp.mean(dng * xh, axis=-1, keepdims=True))
    return dx, jnp.sum(dn * xh, axis=0, keepdims=True)


def _rms_bwd(dn, h, gain, dh_up):
    T, D = h.shape
    bm = min(ROW_TILE, T)

    def body(dn_ref, h_ref, g_ref, up_ref, dh_ref, dhb_ref, dg_ref):
        dx, dg = _rms_bwd_rows(dn_ref[...], h_ref[...], g_ref[...])
        dh = up_ref[...] + dx
        dh_ref[...] = dh
        dhb_ref[...] = dh.astype(BF)

        @pl.when(pl.program_id(0) == 0)
        def _():
            dg_ref[...] = jnp.zeros_like(dg_ref)

        dg_ref[...] += dg

    row = pl.BlockSpec((bm, D), lambda i: (i, 0))
    vec = pl.BlockSpec((1, D), lambda i: (0, 0))
    return pl.pallas_call(
        body, grid=(T // bm,), in_specs=[row, row, vec, row], out_specs=[row, row, vec],
        out_shape=[jax.ShapeDtypeStruct((T, D), F32), jax.ShapeDtypeStruct((T, D), BF),
                   jax.ShapeDtypeStruct((1, D), F32)],
        name="rms_bwd", compiler_params=_cparams(1))(dn, h, gain, dh_up)


def _loss_head(h, gain, target):
    T, D = h.shape
    bm = min(ROW_TILE, T)

    def body(h_ref, g_ref, t_ref, dh_ref, dhb_ref, dg_ref, loss_ref):
        x = h_ref[...]
        g = g_ref[...]
        r = lax.rsqrt(jnp.mean(x * x, axis=-1, keepdims=True) + RMS_EPS)
        err = x * r * g - t_ref[...]
        part = 0.5 * jnp.sum(jnp.sum(err * err, axis=-1, keepdims=True), axis=0, keepdims=True) / D
        dx, dg = _rms_bwd_rows(err / D, x, g)
        dh_ref[...] = dx
        dhb_ref[...] = dx.astype(BF)

        @pl.when(pl.program_id(0) == 0)
        def _():
            dg_ref[...] = jnp.zeros_like(dg_ref)
            loss_ref[...] = jnp.zeros_like(loss_ref)

        dg_ref[...] += dg
        loss_ref[...] += jnp.broadcast_to(part, loss_ref.shape)

    row = pl.BlockSpec((bm, D), lambda i: (i, 0))
    vec = pl.BlockSpec((1, D), lambda i: (0, 0))
    return pl.pallas_call(
        body, grid=(T // bm,), in_specs=[row, vec, row],
        out_specs=[row, row, vec, pl.BlockSpec((SUBLANES, LANES), lambda i: (0, 0))],
        out_shape=[jax.ShapeDtypeStruct((T, D), F32), jax.ShapeDtypeStruct((T, D), BF),
                   jax.ShapeDtypeStruct((1, D), F32), jax.ShapeDtypeStruct((SUBLANES, LANES), F32)],
        name="loss_head", compiler_params=_cparams(1))(h, gain, target)


def _ffn_up(n, w1g, w3g, comm=None):
    T, D = n.shape
    fs = w1g.shape[-1]
    bm = min(FFN_ROW_TILE, T)
    wspec = pl.BlockSpec((None, D, fs), lambda s, i, k: (s, 0, 0))
    ospec = pl.BlockSpec((None, bm, fs), lambda s, i, k: (s, i, 0))

    def epi(accs, ex, outs):
        a1, a3 = accs
        sg = _sig(a1)
        silu = a1 * sg
        outs[0][...] = (a3 * sg * (1.0 + a1 * (1.0 - sg))).astype(BF)
        outs[1][...] = silu.astype(BF)
        outs[2][...] = (silu * a3).astype(BF)

    sh = jax.ShapeDtypeStruct((N_DEV, T, fs), BF)
    return _mm("ffn_up", (N_DEV, T // bm, 1),
               [(n, pl.BlockSpec((bm, D), lambda s, i, k: (i, 0))), (w1g, wspec), (w3g, wspec)],
               [(0, 1, NN, 0), (0, 2, NN, 1)], 2, None, [], [(sh, ospec)] * 3, epi, comm=comm,
               nrow=max(1, bm // ROW_TILE))


def _ffn_down(g, w2g, h, comm=None):
    _, T, fs = g.shape
    D = h.shape[1]
    bm = min(FFN_ROW_TILE, T)
    row = pl.BlockSpec((bm, D), lambda i, j, s: (i, 0))

    def epi(accs, ex, outs):
        outs[0][...] = ex[0][...] + 0.5 * accs[0]

    return _mm("ffn_down", (T // bm, 1, N_DEV),
               [(g, pl.BlockSpec((None, bm, fs), lambda i, j, s: (s, i, 0))),
                (w2g, pl.BlockSpec((None, fs, D), lambda i, j, s: (s, 0, 0)))],
               [(0, 1, NN, 0)], 1, (bm, D), [(h, row)],
               [(jax.ShapeDtypeStruct((T, D), F32), row)], epi, comm=comm,
               nrow=max(1, bm // ROW_TILE), ncol=max(1, D // COL_TILE))[0]


def _ffn_bwd_hidden(dhb, w2g, t1, t3, comm=None):
    T, D = dhb.shape
    fs = t1.shape[-1]
    bm = min(FFN_ROW_TILE, T)
    aspec = pl.BlockSpec((None, bm, fs), lambda s, i, k: (s, i, 0))

    def epi(accs, ex, outs):
        dg = 0.5 * accs[0]
        outs[0][...] = (dg * ex[0][...].astype(F32)).astype(BF)
        outs[1][...] = (dg * ex[1][...].astype(F32)).astype(BF)

    sh = jax.ShapeDtypeStruct((N_DEV, T, fs), BF)
    return _mm("ffn_bwd_hidden", (N_DEV, T // bm, 1),
               [(dhb, pl.BlockSpec((bm, D), lambda s, i, k: (i, 0))),
                (w2g, pl.BlockSpec((None, fs, D), lambda s, i, k: (s, 0, 0)))],
               [(0, 1, NT, 0)], 1, None, [(t1, aspec), (t3, aspec)], [(sh, aspec)] * 2, epi, comm=comm,
               nrow=max(1, bm // ROW_TILE))


def _ffn_dw2(g, dhb, comm=None):
    _, T, fs = g.shape
    D = dhb.shape[1]
    bn = min(COL_TILE, D)

    def epi(accs, ex, outs):
        outs[0][...] = (0.5 * accs[0]).astype(BF)

    return _mm("ffn_dw2", (N_DEV, D // bn, 1),
               [(g, pl.BlockSpec((None, T, fs), lambda s, j, k: (s, 0, 0))),
                (dhb, pl.BlockSpec((T, bn), lambda s, j, k: (0, j)))],
               [(0, 1, TN, 0)], 1, None, [],
               [(jax.ShapeDtypeStruct((N_DEV, fs, D), BF), pl.BlockSpec((None, fs, bn), lambda s, j, k: (s, 0, j)))],
               epi, comm=comm)[0]


def _ffn_dw13(n, da1, da3, comm=None):
    T, D = n.shape
    fs = da1.shape[-1]
    bn = min(COL_TILE, D)
    dspec = pl.BlockSpec((None, T, fs), lambda s, j, k: (s, 0, 0))
    ospec = pl.BlockSpec((None, fs, bn), lambda s, j, k: (s, 0, j))

    def epi(accs, ex, outs):
        outs[0][...] = accs[0].astype(BF)
        outs[1][...] = accs[1].astype(BF)

    sh = jax.ShapeDtypeStruct((N_DEV, fs, D), BF)
    return _mm("ffn_dw13", (N_DEV, D // bn, 1),
               [(da1, dspec), (da3, dspec), (n, pl.BlockSpec((T, bn), lambda s, j, k: (0, j)))],
               [(0, 2, TN, 0), (1, 2, TN, 1)], 2, None, [], [(sh, ospec)] * 2, epi, comm=comm)


def _ffn_dn_rms(da1, da3, w1g, w3g, h, gain, dh_up, comm=None):
    _, T, fs = da1.shape
    D = w1g.shape[-2]
    bm = min(ROW_TILE, T)
    rows_per_pass = min(64, bm)
    dspec = pl.BlockSpec((None, bm, fs), lambda i, j, s: (s, i, 0))
    wspec = pl.BlockSpec((None, D, fs), lambda i, j, s: (s, 0, 0))
    row = pl.BlockSpec((bm, D), lambda i, j, s: (i, 0))
    vec = pl.BlockSpec((1, D), lambda i, j, s: (0, 0))

    def epi(acc, ex, outs):
        h_ref, g_ref, up_ref = ex
        dh_ref, dhb_ref, dg_ref = outs

        @pl.when(pl.program_id(0) == 0)
        def _():
            dg_ref[...] = jnp.zeros_like(dg_ref)

        g = g_ref[...]
        dg = jnp.zeros((1, D), F32)
        for r in range(bm // rows_per_pass):
            rows = pl.ds(r * rows_per_pass, rows_per_pass)
            dx, dg_r = _rms_bwd_rows(acc[0][rows, :], h_ref[rows, :], g)
            dh = up_ref[rows, :] + dx
            dh_ref[rows, :] = dh
            dhb_ref[rows, :] = dh.astype(BF)
            dg = dg + dg_r
        dg_ref[...] += dg

    return _mm("ffn_dn_rms", (T // bm, 1, N_DEV),
               [(da1, dspec), (w1g, wspec), (da3, dspec), (w3g, wspec)],
               [(0, 1, NT, 0), (2, 3, NT, 0)], 1, (bm, D), [(h, row), (gain, vec), (dh_up, row)],
               [(jax.ShapeDtypeStruct((T, D), F32), row), (jax.ShapeDtypeStruct((T, D), BF), row),
                (jax.ShapeDtypeStruct((1, D), F32), vec)],
               epi, comm=comm, ncol=max(1, D // COL_TILE), whole_tile_epilogue=True)


def _rope_tables(seq):
    half = LANES // 2
    inv = ROPE_THETA ** (-jnp.arange(0, half, dtype=F32) * 2.0 / LANES)
    ang = jnp.arange(seq, dtype=F32)[:, None] * inv[None, :]
    cos, sin = jnp.cos(ang), jnp.sin(ang)
    return jnp.concatenate([cos, cos], axis=1), jnp.concatenate([-sin, sin], axis=1)


def _branch_bias(nq, bq):
    d = (jnp.arange(nq)[:, None, None] * bq + jnp.arange(bq)[None, :, None]
         - jnp.arange(bq)[None, None, :])
    mult = jnp.zeros(d.shape, F32)
    for window, dil in DILATED_PATTERN:
        mult = mult + ((d >= 0) & (d % dil == 0) & (d <= window)).astype(F32)
    return jnp.where(mult > 0, jnp.log(jnp.maximum(mult, 1.0)), NEG_INF)


def _proj_fwd(u, wing, comm=None):
    T, D = u.shape
    ws = wing.shape[-1]
    bm = min(ROW_TILE, T)

    def epi(accs, ex, outs):
        outs[0][...] = accs[0]

    return _mm("proj_fwd", (N_DEV, T // bm, 1),
               [(u, pl.BlockSpec((bm, D), lambda s, i, k: (i, 0))),
                (wing, pl.BlockSpec((None, D, ws), lambda s, i, k: (s, 0, 0)))],
               [(0, 1, NN, 0)], 1, None, [],
               [(jax.ShapeDtypeStruct((T, N_DEV * ws), F32),
                 pl.BlockSpec((bm, ws), lambda s, i, k: (i, s)))], epi, comm=comm)[0]


def _rope_fwd(proj, cosf, sinf, seq, nh):
    T = proj.shape[0]
    bs = min(ROW_TILE, seq)
    nst = seq // bs
    scale = LANES ** -0.5

    def body(x_ref, c_ref, s_ref, o_ref):
        j = pl.program_id(1)
        c = c_ref[...]
        s = s_ref[...]
        mul = jnp.where(j == 0, scale, 1.0)
        for h in range(nh):
            cols = slice(h * LANES, (h + 1) * LANES)
            t = x_ref[:, cols]
            rot = (t * c + pltpu.roll(t, LANES // 2, 1) * s) * mul
            o_ref[:, cols] = jnp.where(j < 2, rot, t).astype(BF)

    blk = pl.BlockSpec((bs, nh * LANES), lambda r, j: (r, j))
    tab = pl.BlockSpec((bs, LANES), lambda r, j: (r % nst, 0))
    return pl.pallas_call(
        body, grid=(T // bs, 3), in_specs=[blk, tab, tab], out_specs=blk,
        out_shape=jax.ShapeDtypeStruct((T, 3 * nh * LANES), BF), name="rope_fwd",
        compiler_params=_cparams(2))(proj, cosf, sinf)


def _attn_fwd(qkv, bias, nb, seq, nh, comm=None):
    T = nb * seq
    bq = bias.shape[1]
    nq = seq // bq

    def body(q_ref, k_ref, v_ref, b_ref, o_ref, lse_ref):
        qi = pl.program_id(2)
        q = q_ref[...]

        def step(kj, carry):
            m, l, acc = carry
            rows = pl.ds(pl.multiple_of(kj * bq, bq), bq)
            s = _dot(q, k_ref[rows, :], NT) + b_ref[qi - kj]
            m_new = jnp.maximum(m, jnp.max(s, axis=1, keepdims=True))
            p = jnp.exp(s - m_new)
            alpha = jnp.exp(m - m_new)
            l = alpha * l + jnp.sum(p, axis=1, keepdims=True)
            acc = alpha * acc + _dot(p, v_ref[rows, :], NN)
            return m_new, l, acc

        init = (jnp.full((bq, 1), NEG_INF, F32), jnp.zeros((bq, 1), F32), jnp.zeros((bq, LANES), F32))
        m, l, acc = lax.fori_loop(0, qi + 1, step, init)
        o_ref[...] = (acc / l).astype(BF)
        lse_ref[...] = m + jnp.log(l)

    return _call(
        body, "attn_fwd", (nb, nh, nq),
        [pl.BlockSpec((bq, LANES), lambda b, h, i: (b * nq + i, h)),
         pl.BlockSpec((seq, LANES), lambda b, h, i: (b, nh + h)),
         pl.BlockSpec((seq, LANES), lambda b, h, i: (b, 2 * nh + h)),
         pl.BlockSpec((nq, bq, bq), lambda b, h, i: (0, 0, 0))],
        [pl.BlockSpec((bq, LANES), lambda b, h, i: (b * nq + i, h)),
         pl.BlockSpec((None, bq, 1), lambda b, h, i: (h, b * nq + i, 0))],
        [jax.ShapeDtypeStruct((T, 2 * nh * LANES), BF), jax.ShapeDtypeStruct((nh, T, 1), F32)],
        (qkv, qkv, qkv, bias), comm=comm)


def _attn_bwd(qkv, cat, dcat, lse, bias, nb, seq, nh, comm=None):
    T = nb * seq
    bq = bias.shape[1]
    nq = seq // bq

    def body(k_ref, v_ref, q_ref, o_ref, do_ref, lse_ref, b_ref, dq_ref, dk_ref, dv_ref):
        kj = pl.program_id(2)
        k = k_ref[...]
        v = v_ref[...]

        @pl.when(kj == 0)
        def _():
            dq_ref[...] = jnp.zeros_like(dq_ref)

        def step(qi, carry):
            dk, dv = carry
            rows = pl.ds(pl.multiple_of(qi * bq, bq), bq)
            q = q_ref[rows, :]
            do = do_ref[rows, :]
            dob = do.astype(BF)
            delta = jnp.sum(do * o_ref[rows, :].astype(F32), axis=1, keepdims=True)
            p = jnp.exp(_dot(q, k, NT) + b_ref[qi - kj] - lse_ref[rows, :])
            dv = dv + _dot(p, dob, TN)
            ds = p * (_dot(dob, v, NT) - delta)
            dq_ref[rows, :] += _dot(ds, k, NN)
            return dk + _dot(ds, q, TN), dv

        z = jnp.zeros((bq, LANES), F32)
        dk, dv = lax.fori_loop(kj, nq, step, (z, z))
        dk_ref[...] = dk
        dv_ref[...] = dv

    whole = pl.BlockSpec((seq, LANES), lambda b, h, i: (b, h))
    tile = pl.BlockSpec((bq, LANES), lambda b, h, i: (b * nq + i, h))
    sh = jax.ShapeDtypeStruct((T, nh * LANES), F32)
    return _call(
        body, "attn_bwd", (nb, nh, nq),
        [pl.BlockSpec((bq, LANES), lambda b, h, i: (b * nq + i, nh + h)),
         pl.BlockSpec((bq, LANES), lambda b, h, i: (b * nq + i, 2 * nh + h)),
         whole, whole, whole, pl.BlockSpec((None, seq, 1), lambda b, h, i: (h, b, 0)),
         pl.BlockSpec((nq, bq, bq), lambda b, h, i: (0, 0, 0))],
        [whole, tile, tile], [sh, sh, sh],
        (qkv, qkv, qkv, cat, dcat, lse, bias), comm=comm)


def _conv_parts(gc, xin, w_ref):
    w = [w_ref[k:k + 1, :] for k in range(3)]
    u = gc * xin
    row = lax.broadcasted_iota(jnp.int32, u.shape, 0)
    u1 = jnp.where(row >= 1, pltpu.roll(u, 1, 0), 0.0)
    u2 = jnp.where(row >= 2, pltpu.roll(u, 2, 0), 0.0)
    return u, u1, u2, w[0] * u2 + w[1] * u1 + w[2] * u, w, row


def _conv_fwd(proj, conv_w, cat, nb, seq, width):
    cw = min(2 * LANES, width)
    nc = width // cw

    def body(gb_ref, gc_ref, x_ref, w_ref, cat_ref, o_ref):
        _, _, _, conv, _, _ = _conv_parts(gc_ref[...], x_ref[...], w_ref)
        o_ref[...] = (gb_ref[...] * conv).astype(BF)

    def sec(k):
        return pl.BlockSpec((seq, cw), lambda b, c: (b, k * nc + c))

    return pl.pallas_call(
        body, grid=(nb, nc),
        in_specs=[sec(3), sec(4), sec(5), pl.BlockSpec((3, cw), lambda b, c: (0, c)),
                  pl.BlockSpec(memory_space=pl.ANY)],
        out_specs=pl.BlockSpec((seq, cw), lambda b, c: (b, nc + c)),
        out_shape=jax.ShapeDtypeStruct(cat.shape, BF), input_output_aliases={4: 0},
        name="conv_fwd", compiler_params=_cparams(2))(proj, proj, proj, conv_w, cat)


def _conv_bwd(proj, conv_w, dcat, nb, seq, width):
    cw = min(2 * LANES, width)
    nc = width // cw
    T = nb * seq

    def body(gb_ref, gc_ref, x_ref, w_ref, d_ref, dgb_ref, dgc_ref, dx_ref, dw_ref):
        gc = gc_ref[...]
        xin = x_ref[...]
        u, u1, u2, conv, w, row = _conv_parts(gc, xin, w_ref)
        dsc = d_ref[...]
        dgb_ref[...] = dsc * conv
        dconv = dsc * gb_ref[...]
        d1 = jnp.where(row < seq - 1, pltpu.roll(dconv, seq - 1, 0), 0.0)
        d2 = jnp.where(row < seq - 2, pltpu.roll(dconv, seq - 2, 0), 0.0)
        du = w[2] * dconv + w[1] * d1 + w[0] * d2
        dgc_ref[...] = du * xin
        dx_ref[...] = du * gc

        @pl.when(pl.program_id(1) == 0)
        def _():
            dw_ref[...] = jnp.zeros_like(dw_ref)

        dw_ref[0:1, :] += jnp.sum(dconv * u2, axis=0, keepdims=True)
        dw_ref[1:2, :] += jnp.sum(dconv * u1, axis=0, keepdims=True)
        dw_ref[2:3, :] += jnp.sum(dconv * u, axis=0, keepdims=True)

    def sec(k):
        return pl.BlockSpec((seq, cw), lambda c, b: (b, k * nc + c))

    out = pl.BlockSpec((seq, cw), lambda c, b: (b, c))
    wsp = pl.BlockSpec((3, cw), lambda c, b: (0, c))
    sh = jax.ShapeDtypeStruct((T, width), F32)
    return pl.pallas_call(
        body, grid=(nc, nb), in_specs=[sec(3), sec(4), sec(5), wsp, sec(1)],
        out_specs=[out, out, out, wsp], out_shape=[sh, sh, sh, jax.ShapeDtypeStruct((3, width), F32)],
        name="conv_bwd", compiler_params=_cparams(2))(proj, proj, proj, conv_w, dcat)


def _assemble_dproj(dq, dk, dv, dgb, dgc, dxin, cosf, sinf, seq):
    T, width = dq.shape
    nh = width // LANES
    bs = min(ROW_TILE, seq)
    nst = seq // bs
    scale = LANES ** -0.5

    def body(dq_ref, dk_ref, dv_ref, dgb_ref, dgc_ref, dx_ref, c_ref, s_ref, o_ref):
        sec = pl.program_id(1)
        c = c_ref[...]
        s = s_ref[...]

        def unrope(ref, mul):
            for h in range(nh):
                cols = slice(h * LANES, (h + 1) * LANES)
                t = ref[:, cols]
                o_ref[:, cols] = ((t * c + pltpu.roll(t * s, LANES // 2, 1)) * mul).astype(BF)

        @pl.when(sec == 0)
        def _():
            unrope(dq_ref, scale)

        @pl.when(sec == 1)
        def _():
            unrope(dk_ref, 1.0)

        for k, ref in ((2, dv_ref), (3, dgb_ref), (4, dgc_ref), (5, dx_ref)):
            @pl.when(sec == k)
            def _(ref=ref):
                o_ref[...] = ref[...].astype(BF)

    blk = pl.BlockSpec((bs, width), lambda r, k: (r, 0))
    tab = pl.BlockSpec((bs, LANES), lambda r, k: (r % nst, 0))
    return pl.pallas_call(
        body, grid=(T // bs, 6), in_specs=[blk] * 6 + [tab, tab],
        out_specs=pl.BlockSpec((bs, width), lambda r, k: (r, k)),
        out_shape=jax.ShapeDtypeStruct((T, 6 * width), BF), name="assemble_dproj",
        compiler_params=_cparams(2))(dq, dk, dv, dgb, dgc, dxin, cosf, sinf)


def _res_mm(name, a, w, h, comm=None):
    T, K = a.shape
    N = w.shape[1]
    bm = min(ROW_TILE, T)
    bk = min(ROW_TILE, K)
    row = pl.BlockSpec((bm, N), lambda i, j, k: (i, 0))

    def epi(accs, ex, outs):
        outs[0][...] = ex[0][...] + accs[0]

    return _mm(name, (T // bm, 1, K // bk),
               [(a, pl.BlockSpec((bm, bk), lambda i, j, k: (i, k))),
                (w, pl.BlockSpec((bk, N), lambda i, j, k: (k, 0)))],
               [(0, 1, NN, 0)], 1, (bm, N), [(h, row)],
               [(jax.ShapeDtypeStruct((T, N), F32), row)], epi, comm=comm, ncol=max(1, N // COL_TILE))[0]


def _mm_nt(name, a, w, out_dtype):
    T, K = a.shape
    N = w.shape[0]
    bm = min(ROW_TILE, T)
    bn = min(ROW_TILE, N)

    def epi(accs, ex, outs):
        outs[0][...] = accs[0].astype(out_dtype)

    return _mm(name, (T // bm, N // bn, 1),
               [(a, pl.BlockSpec((bm, K), lambda i, j, k: (i, 0))),
                (w, pl.BlockSpec((bn, K), lambda i, j, k: (j, 0)))],
               [(0, 1, NT, 0)], 1, None, [],
               [(jax.ShapeDtypeStruct((T, N), out_dtype), pl.BlockSpec((bm, bn), lambda i, j, k: (i, j)))],
               epi)[0]


def _mm_tn(name, a, bs_list):
    T, M = a.shape
    N = bs_list[0].shape[1]
    bmr = min(COL_TILE, M)
    bn = min(COL_TILE, N)
    n = len(bs_list)

    def epi(accs, ex, outs):
        for q in range(n):
            outs[q][...] = accs[q].astype(BF)

    ops = [(a, pl.BlockSpec((T, bmr), lambda r, j, k: (0, r)))]
    ops += [(b, pl.BlockSpec((T, bn), lambda r, j, k: (0, j))) for b in bs_list]
    return _mm(name, (M // bmr, N // bn, 1), ops, [(0, 1 + q, TN, q) for q in range(n)], n, None, [],
               [(jax.ShapeDtypeStruct((M, N), BF), pl.BlockSpec((bmr, bn), lambda r, j, k: (r, j)))] * n, epi)


def _proj_bwd_x(dproj, wing):
    T = dproj.shape[0]
    _, D, ws = wing.shape
    bm = min(ROW_TILE, T)
    row = pl.BlockSpec((bm, D), lambda i, j, s: (i, 0))

    def epi(accs, ex, outs):
        outs[0][...] = accs[0]

    return _mm("proj_bwd_x", (T // bm, 1, N_DEV),
               [(dproj, pl.BlockSpec((bm, ws), lambda i, j, s: (i, s))),
                (wing, pl.BlockSpec((None, D, ws), lambda i, j, s: (s, 0, 0)))],
               [(0, 1, NT, 0)], 1, (bm, D), [], [(jax.ShapeDtypeStruct((T, D), F32), row)], epi,
               ncol=max(1, D // COL_TILE))[0]


def _proj_dw(u, dproj, ws):
    T, D = u.shape
    bmr = min(COL_TILE, D)

    def epi(accs, ex, outs):
        outs[0][...] = accs[0].astype(BF)

    return _mm("proj_dw", (N_DEV, D // bmr, 1),
               [(u, pl.BlockSpec((T, bmr), lambda s, r, k: (0, r))),
                (dproj, pl.BlockSpec((T, ws), lambda s, r, k: (0, s)))],
               [(0, 1, TN, 0)], 1, None, [],
               [(jax.ShapeDtypeStruct((N_DEV, D, ws), BF),
                 pl.BlockSpec((None, bmr, ws), lambda s, r, k: (s, r, 0)))], epi)[0]


def _mixer_ab_fwd(h, gain, wing, conv_w, wout, tabs, nb, seq, comm_proj=None, comm_attn=None, comm_out=None):
    cosf, sinf, bias = tabs
    width = wing.shape[-1] * N_DEV // 6
    nh = width // LANES
    u = _rms_fwd(h, gain, BF)
    proj = _proj_fwd(u, wing, comm=comm_proj)
    qkv = _rope_fwd(proj, cosf, sinf, seq, nh)
    cat, lse = _attn_fwd(qkv, bias, nb, seq, nh, comm=comm_attn)
    cat = _conv_fwd(proj, conv_w, cat, nb, seq, width)
    return _res_mm("outproj_fwd", cat, wout, h, comm=comm_out), (h, u, proj, qkv, cat, lse)


def _mixer_ab_bwd(dh, dhb, saved, gain, wing, conv_w, wout, tabs, nb, seq, reduce_start, carry):
    cosf, sinf, bias = tabs
    h, u, proj, qkv, cat, lse = saved
    D = h.shape[1]
    ws = wing.shape[-1]
    width = ws * N_DEV // 6
    nh = width // LANES
    dcat = _mm_nt("outproj_bwd_x", dhb, wout, F32)
    dwout = _mm_tn("outproj_dw", cat, [dhb])[0]
    comm = _merge_comms(reduce_start(["ab_w_out"], [dwout.reshape(N_DEV, -1, D)]) + [carry])
    dq, dk, dv = _attn_bwd(qkv, cat, dcat, lse, bias, nb, seq, nh, comm=comm)
    dgb, dgc, dxin, dconvw = _conv_bwd(proj, conv_w, dcat, nb, seq, width)
    dproj = _assemble_dproj(dq, dk, dv, dgb, dgc, dxin, cosf, sinf, seq)
    du = _proj_bwd_x(dproj, wing)
    comm, = reduce_start(["ab_w_in"], [_proj_dw(u, dproj, ws)])
    dh_in, dhb_in, dgain = _rms_bwd(du, h, gain, dh)
    return dh_in, dhb_in, dgain, dconvw, comm


def _s5_zoh(lr, li, log_dt):
    dt = jnp.exp(log_dt)
    mag = jnp.exp(lr * dt)
    ar = mag * jnp.cos(li * dt)
    ai = mag * jnp.sin(li * dt)
    den = lr * lr + li * li
    return dt, ar, ai, den, ((ar - 1.0) * lr + ai * li) / den, (ai * lr - (ar - 1.0) * li) / den


def _s5_discretize(lam_re, lam_im, log_dt, bt_re, bt_im):
    def body(lr_ref, li_ref, ld_ref, br_ref, bi_ref, ar_ref, ai_ref, bbr_ref, bbi_ref):
        _, ar, ai, _, fr, fi = _s5_zoh(lr_ref[...], li_ref[...], ld_ref[...])
        ar_ref[...] = ar
        ai_ref[...] = ai
        bbr_ref[...] = fr * br_ref[...] - fi * bi_ref[...]
        bbi_ref[...] = fr * bi_ref[...] + fi * br_ref[...]

    small = jax.ShapeDtypeStruct(lam_re.shape, F32)
    big = jax.ShapeDtypeStruct(bt_re.shape, F32)
    return pl.pallas_call(body, out_shape=[small, small, big, big], name="s5_discretize",
                          compiler_params=_cparams(0))(lam_re, lam_im, log_dt, bt_re, bt_im)


def _s5_discretize_bwd(lam_re, lam_im, log_dt, bt_re, bt_im, d_ar, d_ai, d_bbr, d_bbi):

    def body(lr_ref, li_ref, ld_ref, br_ref, bi_ref, dar_ref, dai_ref, dbbr_ref, dbbi_ref,
             dlr_ref, dli_ref, dld_ref, dbr_ref, dbi_ref):
        lr, li = lr_ref[...], li_ref[...]
        dt, ar, ai, den, fr, fi = _s5_zoh(lr, li, ld_ref[...])
        br, bi = br_ref[...], bi_ref[...]
        dbbr, dbbi = dbbr_ref[...], dbbi_ref[...]
        dbr_ref[...] = dbbr * fr + dbbi * fi
        dbi_ref[...] = dbbi * fr - dbbr * fi
        dfr = jnp.sum(dbbr * br + dbbi * bi, axis=1, keepdims=True)
        dfi = jnp.sum(dbbi * br - dbbr * bi, axis=1, keepdims=True)
        dnr = dfr / den
        dni = dfi / den
        dden = -(dfr * fr + dfi * fi) / den
        dar = dar_ref[...] + dnr * lr - dni * li
        dai = dai_ref[...] + dnr * li + dni * lr
        dlr_ref[...] = dnr * (ar - 1.0) + dni * ai + 2.0 * dden * lr + dt * (dar * ar + dai * ai)
        dli_ref[...] = dnr * ai - dni * (ar - 1.0) + 2.0 * dden * li + dt * (dai * ar - dar * ai)
        ddt = jnp.sum(dar * (lr * ar - li * ai) + dai * (lr * ai + li * ar), axis=2, keepdims=True)
        dld_ref[...] = ddt * dt

    small = jax.ShapeDtypeStruct(lam_re.shape, F32)
    big = jax.ShapeDtypeStruct(bt_re.shape, F32)
    return pl.pallas_call(
        body, out_shape=[small, small, jax.ShapeDtypeStruct(log_dt.shape, F32), big, big],
        name="s5_discretize_bwd", compiler_params=_cparams(0))(
            lam_re, lam_im, log_dt, bt_re, bt_im, d_ar, d_ai, d_bbr, d_bbi)


def _rows8(t):
    return pl.ds(pl.multiple_of(t * SUBLANES, SUBLANES), SUBLANES)


def _cmul_add(ar, ai, sr, si, br, bi):
    return ar * sr - ai * si + br, ar * si + ai * sr + bi


def _scan(a, read, write, init, n):
    def step(t, c):
        s = _cmul_add(*a, *c, *read(t))
        if write is not None:
            write(t, s)
        return s

    return lax.fori_loop(0, n, step, init, unroll=SCAN_UNROLL)


def _cpow(ar, ai, n):
    rr = ri = None
    while n:
        if n & 1:
            rr, ri = (ar, ai) if rr is None else (rr * ar - ri * ai, rr * ai + ri * ar)
        ar, ai = ar * ar - ai * ai, 2.0 * ar * ai
        n >>= 1
    return rr, ri


def _s5_specs(R, nj):
    sh = STATE_COLS
    return dict(
        rows=pl.BlockSpec((R, LANES), lambda j: (0, j)),
        bd=pl.BlockSpec((None, LANES, sh), lambda j: (j, 0, 0)),
        cd=pl.BlockSpec((None, sh, LANES), lambda j: (j, 0, 0)),
        a=pl.BlockSpec((None, 1, sh), lambda j: (j, 0, 0)),
        vec=pl.BlockSpec((1, LANES), lambda j: (0, j)),
        init=pl.BlockSpec((None, SUBLANES, sh), lambda j: (j, 0, 0)))


def _s5_fwd(u, mats, seg_len, nseg, comm=None):
    bdr, bdi, cdr, cdi, are, aim, dsk = mats
    R, D = u.shape
    nj = D // LANES
    sh = STATE_COLS
    rc = min(R, 512)
    sp = _s5_specs(R, nj)

    def body(u_ref, bdr_ref, bdi_ref, cdr_ref, cdi_ref, ar_ref, ai_ref, d_ref,
             y_ref, yg_ref, ir_ref, ii_ref, sre, sim):
        ar = jnp.broadcast_to(ar_ref[...], (SUBLANES, sh))
        ai = jnp.broadcast_to(ai_ref[...], (SUBLANES, sh))

        def bu_chunk(c, _):
            rows = pl.ds(pl.multiple_of(c * rc, rc), rc)
            ub = u_ref[rows, :].astype(BF)
            sre[rows, :] = _dot(ub, bdr_ref[...], NN)
            sim[rows, :] = _dot(ub, bdi_ref[...], NN)
            return 0

        lax.fori_loop(0, R // rc, bu_chunk, 0)
        z = jnp.zeros((SUBLANES, sh), F32)

        def read(t):
            return sre[_rows8(t), :], sim[_rows8(t), :]

        def write(t, s):
            sre[_rows8(t), :] = s[0]
            sim[_rows8(t), :] = s[1]

        er, ei = _scan((ar, ai), read, None, (z, z), seg_len)
        pr, pi = _cpow(ar, ai, seg_len)
        first = (lax.broadcasted_iota(jnp.int32, (SUBLANES, sh), 0) & (nseg - 1)) == 0

        def prev(x):
            return jnp.where(first, 0.0, pltpu.roll(x, 1, 0))

        xr, xi = er, ei
        for _ in range(nseg - 1):
            xr, xi = _cmul_add(pr, pi, prev(xr), prev(xi), er, ei)
        i_r, i_i = prev(xr), prev(xi)
        ir_ref[...] = i_r
        ii_ref[...] = i_i
        _scan((ar, ai), read, write, (i_r, i_i), seg_len)

        def y_chunk(c, _):
            rows = pl.ds(pl.multiple_of(c * rc, rc), rc)
            y = _dot(sre[rows, :], cdr_ref[...], NN) + _dot(sim[rows, :], cdi_ref[...], NN)
            y = y + d_ref[...] * u_ref[rows, :]
            y_ref[rows, :] = y
            yg_ref[rows, :] = _gelu(y).astype(BF)
            return 0

        lax.fori_loop(0, R // rc, y_chunk, 0)

    init_sh = jax.ShapeDtypeStruct((nj, SUBLANES, STATE_COLS), F32)
    return _call(
        body, "s5_fwd", (nj,),
        [sp["rows"], sp["bd"], sp["bd"], sp["cd"], sp["cd"], sp["a"], sp["a"], sp["vec"]],
        [sp["rows"], sp["rows"], sp["init"], sp["init"]],
        [jax.ShapeDtypeStruct((R, D), F32), jax.ShapeDtypeStruct((R, D), BF), init_sh, init_sh],
        (u, bdr, bdi, cdr, cdi, are, aim, dsk),
        scratch=[pltpu.VMEM((R, sh), F32) for _ in range(2)], comm=comm)


def _s5_bwd(u, dy, mats, init_re, init_im, seg_len, nseg, comm=None):
    bdr, bdi, cdr, cdi, are, aim, dsk = mats
    R, D = u.shape
    nj = D // LANES
    sh = STATE_COLS
    rc = min(R, 512)
    sp = _s5_specs(R, nj)

    def body(u_ref, dy_ref, bdr_ref, bdi_ref, cdr_ref, cdi_ref, ar_ref, ai_ref, d_ref, ir_ref, ii_ref,
             du_ref, dbdr_ref, dbdi_ref, dcdr_ref, dcdi_ref, dar_ref, dai_ref, dd_ref,
             sre, sim, gre, gim):
        ar = jnp.broadcast_to(ar_ref[...], (SUBLANES, sh))
        ai = jnp.broadcast_to(ai_ref[...], (SUBLANES, sh))
        i_r, i_i = ir_ref[...], ii_ref[...]

        def chunk(c):
            return pl.ds(pl.multiple_of(c * rc, rc), rc)

        def bu_chunk(c, _):
            ub = u_ref[chunk(c), :].astype(BF)
            dyb = dy_ref[chunk(c), :].astype(BF)
            sre[chunk(c), :] = _dot(ub, bdr_ref[...], NN)
            sim[chunk(c), :] = _dot(ub, bdi_ref[...], NN)
            gre[chunk(c), :] = _dot(dyb, cdr_ref[...], NT)
            gim[chunk(c), :] = _dot(dyb, cdi_ref[...], NT)
            return 0

        lax.fori_loop(0, R // rc, bu_chunk, 0)

        def read_s(t):
            return sre[_rows8(t), :], sim[_rows8(t), :]

        def read_g(t):
            return gre[_rows8(t), :], gim[_rows8(t), :]

        def both(i, c):
            s = _cmul_add(ar, ai, c[0], c[1], *read_s(i))
            sre[_rows8(i), :], sim[_rows8(i), :] = s
            return (*s, *_cmul_add(ar, -ai, c[2], c[3], *read_g(seg_len - 1 - i)))

        z = jnp.zeros((SUBLANES, sh), F32)
        _, _, fr, fi = lax.fori_loop(0, seg_len, both, (i_r, i_i, z, z), unroll=SCAN_UNROLL)

        def c_chunk(c, carry):
            dyb = dy_ref[chunk(c), :].astype(BF)
            return (carry[0] + _dot(sre[chunk(c), :], dyb, TN), carry[1] + _dot(sim[chunk(c), :], dyb, TN))

        zc = jnp.zeros((sh, LANES), F32)
        dcr, dci = lax.fori_loop(0, R // rc, c_chunk, (zc, zc))
        dcdr_ref[...] = dcr
        dcdi_ref[...] = dci
        pr, pi = _cpow(ar, ai, seg_len)
        last =(lax.broadcasted_iota(jnp.int32, (SUBLANES, sh), 0) & (nseg - 1)) == nseg - 1

        def nxt(x):
            return jnp.where(last, 0.0, pltpu.roll(x, SUBLANES - 1, 0))

        xr, xi = fr, fi
        for _ in range(nseg - 1):
            xr, xi = _cmul_add(pr, -pi, nxt(xr), nxt(xi), fr, fi)
        g0r, g0i = nxt(xr), nxt(xi)

        def adj_step(t, c, s_before):
            gr, gi = _cmul_add(ar, -ai, c[0], c[1], *read_g(t))
            gre[_rows8(t), :], gim[_rows8(t), :] = gr, gi
            spr, spi = s_before
            return gr, gi, c[2] + spr * gr + spi * gi, c[3] + spr * gi - spi * gr

        carry = lax.fori_loop(0, seg_len - 1, lambda i, c: adj_step(seg_len - 1 - i, c, read_s(seg_len - 2 - i)),
                              (g0r, g0i, z, z))
        carry = adj_step(0, carry, (i_r, i_i))
        dar_ref[...] = jnp.sum(carry[2], axis=0, keepdims=True)
        dai_ref[...] = jnp.sum(carry[3], axis=0, keepdims=True)

        def d_chunk(c, carry):
            ub = u_ref[chunk(c), :].astype(BF)
            grb = gre[chunk(c), :].astype(BF)
            gib = gim[chunk(c), :].astype(BF)
            du = _dot(grb, bdr_ref[...], NT) + _dot(gib, bdi_ref[...], NT)
            du_ref[chunk(c), :] = du + d_ref[...] * dy_ref[chunk(c), :]
            dd = carry[2] + jnp.sum(dy_ref[chunk(c), :] * u_ref[chunk(c), :], axis=0, keepdims=True)
            return carry[0] + _dot(ub, grb, TN), carry[1] + _dot(ub, gib, TN), dd

        zb = jnp.zeros((LANES, sh), F32)
        dbr, dbi, dd = lax.fori_loop(0, R // rc, d_chunk, (zb, zb, jnp.zeros((1, LANES), F32)))
        dbdr_ref[...] = dbr
        dbdi_ref[...] = dbi
        dd_ref[...] = dd

    bd_sh = jax.ShapeDtypeStruct((nj, LANES, STATE_COLS), F32)
    cd_sh = jax.ShapeDtypeStruct((nj, STATE_COLS, LANES), F32)
    a_sh = jax.ShapeDtypeStruct((nj, 1, STATE_COLS), F32)
    return _call(
        body, "s5_bwd", (nj,),
        [sp["rows"], sp["rows"], sp["bd"], sp["bd"], sp["cd"], sp["cd"], sp["a"], sp["a"],
         sp["vec"], sp["init"], sp["init"]],
        [sp["rows"], sp["bd"], sp["bd"], sp["cd"], sp["cd"], sp["a"], sp["a"], sp["vec"]],
        [jax.ShapeDtypeStruct((R, D), F32), bd_sh, bd_sh, cd_sh, cd_sh, a_sh, a_sh,
         jax.ShapeDtypeStruct((1, D), F32)],
        (u, dy, bdr, bdi, cdr, cdi, are, aim, dsk, init_re, init_im),
        scratch=[pltpu.VMEM((R, sh), F32) for _ in range(4)], comm=comm)


def _glu_fwd(yg, wa, wb, h):
    T, D = yg.shape
    N = wa.shape[1]
    bm = min(ROW_TILE, T)
    bn = min(ROW_TILE, N)
    wspec = pl.BlockSpec((D, bn), lambda i, j, k: (0, j))
    ospec = pl.BlockSpec((bm, bn), lambda i, j, k: (i, j))

    def epi(accs, ex, outs):
        pa, pb = accs
        outs[0][...] = ex[0][...] + pa * _sig(pb)
        outs[1][...] = pa.astype(BF)
        outs[2][...] = pb.astype(BF)

    return _mm("glu_fwd", (T // bm, N // bn, 1),
               [(yg, pl.BlockSpec((bm, D), lambda i, j, k: (i, 0))), (wa, wspec), (wb, wspec)],
               [(0, 1, NN, 0), (0, 2, NN, 1)], 2, None, [(h, ospec)],
               [(jax.ShapeDtypeStruct((T, N), F32), ospec), (jax.ShapeDtypeStruct((T, N), BF), ospec),
                (jax.ShapeDtypeStruct((T, N), BF), ospec)], epi)


def _glu_bwd_gates(dz, pa, pb):
    T, D = dz.shape
    bm = min(ROW_TILE, T)

    def body(dz_ref, pa_ref, pb_ref, dpa_ref, dpb_ref):
        dz = dz_ref[...]
        sg = _sig(pb_ref[...].astype(F32))
        dpa_ref[...] = (dz * sg).astype(BF)
        dpb_ref[...] = (dz * pa_ref[...].astype(F32) * sg * (1.0 - sg)).astype(BF)

    row = pl.BlockSpec((bm, D), lambda i: (i, 0))
    return pl.pallas_call(
        body, grid=(T // bm,), in_specs=[row] * 3, out_specs=[row] * 2,
        out_shape=[jax.ShapeDtypeStruct((T, D), BF)] * 2, name="glu_bwd_gates",
        compiler_params=_cparams(1))(dz, pa, pb)


def _glu_bwd_y(dpa, dpb, wa, wb, y_pre, comm=None):
    T, N = dpa.shape
    D = wa.shape[0]
    bm = min(ROW_TILE, T)
    bn = min(ROW_TILE, D)
    aspec = pl.BlockSpec((bm, N), lambda i, j, k: (i, 0))
    wspec = pl.BlockSpec((bn, N), lambda i, j, k: (j, 0))
    ospec = pl.BlockSpec((bm, bn), lambda i, j, k: (i, j))

    def epi(accs, ex, outs):
        outs[0][...] = accs[0] * _gelu_grad(ex[0][...])

    return _mm("glu_bwd_y", (T // bm, D // bn, 1), [(dpa, aspec), (wa, wspec), (dpb, aspec), (wb, wspec)],
               [(0, 1, NT, 0), (2, 3, NT, 0)], 1, None, [(y_pre, ospec)],
               [(jax.ShapeDtypeStruct((T, D), F32), ospec)], epi, comm=comm)[0]


def _block_diag_in(x, nj):
    g = GROUPS_PER_BLOCK
    x = x.reshape(nj, g, 1, S5_GROUP, S5_STATE)
    eye = jnp.eye(g, dtype=bool)[None, :, :, None, None]
    full = jnp.where(eye, x, 0.0)
    return full.transpose(0, 1, 3, 2, 4).reshape(nj, g * S5_GROUP, g * S5_STATE)


def _block_diag_out(x, nj):
    return _block_diag_in(x, nj).transpose(0, 2, 1)


def _diag_of_in(m, nj):
    g = GROUPS_PER_BLOCK
    m5 = m.reshape(nj, g, S5_GROUP, g, S5_STATE)
    d = jnp.diagonal(m5, axis1=1, axis2=3)
    return d.transpose(0, 3, 1, 2).reshape(nj * g, S5_GROUP, S5_STATE)


def _mixer_s5_fwd(h, gain, p, dsk, wa, wb, nb, seq, comm_s5=None):
    T, D = h.shape
    nj = D // LANES
    nseg = SUBLANES // nb
    seg_len = seq // nseg
    G = p["s5_lambda_re"].shape[1]
    lam_re = p["s5_lambda_re"].reshape(G, 1, S5_STATE)
    lam_im = p["s5_lambda_im"].reshape(G, 1, S5_STATE)
    log_dt = p["s5_log_dt"].reshape(G, 1, 1)
    bt_re = p["s5_b_re"][0].transpose(0, 2, 1)
    bt_im = p["s5_b_im"][0].transpose(0, 2, 1)
    ar, ai, bbr, bbi = _s5_discretize(lam_re, lam_im, log_dt, bt_re, bt_im)
    mats = (_block_diag_in(bbr, nj).astype(BF), _block_diag_in(bbi, nj).astype(BF),
            _block_diag_out(p["s5_c_re"][0], nj).astype(BF),
            _block_diag_out(-p["s5_c_im"][0], nj).astype(BF),
            ar.reshape(nj, 1, STATE_COLS), ai.reshape(nj, 1, STATE_COLS), dsk)
    h_seg = _to_seg(h, seg_len)
    u = _rms_fwd(h_seg, gain, F32)
    y_pre, yg, init_re, init_im = _s5_fwd(u, mats, seg_len, nseg, comm=comm_s5)
    h_out, pa, pb = _glu_fwd(yg, wa, wb, h_seg)
    disc_in = (lam_re, lam_im, log_dt, bt_re, bt_im)
    return _to_tok(h_out, seg_len), (h_seg, u, mats, y_pre, yg, init_re, init_im, pa, pb, disc_in, seg_len, nseg)


def _mixer_s5_bwd(dh, saved, gain, wa, wb, reduce_start, carry):
    h_seg, u, mats, y_pre, yg, init_re, init_im, pa, pb, disc_in, seg_len, nseg = saved
    T, D = h_seg.shape
    nj = D // LANES
    G = nj * GROUPS_PER_BLOCK
    dh_seg = _to_seg(dh, seg_len)
    dpa, dpb = _glu_bwd_gates(dh_seg, pa, pb)
    dy = _glu_bwd_y(dpa, dpb, wa, wb, y_pre)
    dwa, dwb = _mm_tn("glu_dw", yg, [dpa, dpb])
    comm = _merge_comms(reduce_start(["s5_glu_wa", "s5_glu_wb"],
                                     [dwa.reshape(N_DEV, -1, D), dwb.reshape(N_DEV, -1, D)]) + [carry])
    du, dbdr, dbdi, dcdr, dcdi, dar, dai, dd = _s5_bwd(u, dy, mats, init_re, init_im, seg_len, nseg, comm=comm)
    d_bbr = _diag_of_in(dbdr, nj)
    d_bbi = _diag_of_in(dbdi, nj)
    d_c_re = _diag_of_in(dcdr.transpose(0, 2, 1), nj)
    d_c_im = -_diag_of_in(dcdi.transpose(0, 2, 1), nj)
    dlr, dli, dld, dbr, dbi = _s5_discretize_bwd(
        *disc_in, dar.reshape(G, 1, S5_STATE), dai.reshape(G, 1, S5_STATE), d_bbr, d_bbi)
    small = {"s5_lambda_re": dlr.reshape(1, G, S5_STATE), "s5_lambda_im": dli.reshape(1, G, S5_STATE),
             "s5_log_dt": dld.reshape(1, G),
             "s5_b_re": dbr.transpose(0, 2, 1)[None], "s5_b_im": dbi.transpose(0, 2, 1)[None],
             "s5_c_re": d_c_re[None], "s5_c_im": d_c_im[None], "s5_d": dd}
    dh_in, _, dgain = _rms_bwd(du, h_seg, gain, dh_seg)
    dh_in = _to_tok(dh_in, seg_len)
    return dh_in, dh_in.astype(BF), dgain, small


def _mesh_pos():
    return lax.axis_index("x"), lax.axis_index("y"), lax.axis_index("c")


class _Gather:
    def __init__(self, srcs, slots, send_sems, recv_sems):
        self.srcs, self.slots, self.send_sems, self.recv_sems = srcs, slots, send_sems, recv_sems
        x, y, c = _mesh_pos()
        self.c = c
        self.me, self.sib = (x, y, c), (x, y, 1 - c)
        self.chips = [(1 - x, y), (x, 1 - y), (1 - x, 1 - y)]

    def copy(self, a, k, block, to, own=False):
        dst = self.slots[a].at[4 * block[0] + 2 * block[1] + block[2]]
        return pltpu.make_async_remote_copy(
            src_ref=self.srcs[a] if own else dst, dst_ref=dst, send_sem=self.send_sems.at[7 * a + k],
            recv_sem=self.recv_sems.at[7 * a + k], device_id=to, device_id_type=MESH)

    def own_copies(self, a):
        cps = [self.copy(a, 0, self.me, self.sib, own=True)]
        return cps + [self.copy(a, 1 + j, self.me, (*chip, self.c), own=True) for j, chip in enumerate(self.chips)]

    def start(self):
        for a in range(len(self.srcs)):
            for cp in self.own_copies(a):
                cp.start()

    def finish(self):
        n = len(self.srcs)
        for a in range(n):
            for j, chip in enumerate(self.chips):
                self.copy(a, 1 + j, (*chip, self.c), self.me).wait_recv()
                self.copy(a, 4 + j, (*chip, self.c), self.sib).start()
        for a in range(n):
            self.copy(a, 0, self.sib, self.me).wait_recv()
            for j, chip in enumerate(self.chips):
                self.copy(a, 4 + j, (*chip, 1 - self.c), self.me).wait_recv()
        for a in range(n):
            for cp in self.own_copies(a):
                cp.wait_send()
            for j, chip in enumerate(self.chips):
                self.copy(a, 4 + j, (*chip, self.c), self.sib).wait_send()


def _gather_comm(arrs):
    n = len(arrs)

    def local(xs, outs, sems, a):
        x, y, c = _mesh_pos()
        return pltpu.make_async_copy(xs[a], outs[a].at[4 * x + 2 * y + c], sems[2].at[a])

    def start(xs, outs, sems):
        for a in range(n):
            local(xs, outs, sems, a).start()
        _Gather(xs, outs, sems[0], sems[1]).start()

    def finish(xs, outs, sems):
        _Gather(xs, outs, sems[0], sems[1]).finish()
        for a in range(n):
            local(xs, outs, sems, a).wait()

    return _Comm(list(arrs), [jax.ShapeDtypeStruct((N_DEV,) + a.shape, a.dtype) for a in arrs],
                 [pltpu.SemaphoreType.DMA((7 * n,)), pltpu.SemaphoreType.DMA((7 * n,)),
                  pltpu.SemaphoreType.DMA((n,))], start, finish)


def _exchange_comm(parts):
    n = len(parts)

    def copies(ps, outs, sems):
        x, y, c = _mesh_pos()
        cps = []
        for a in range(n):
            for j in range(1, 4):
                to = (jnp.bitwise_xor(x, j // 2), jnp.bitwise_xor(y, j % 2), c)
                cps.append(pltpu.make_async_remote_copy(
                    src_ref=ps[a].at[j], dst_ref=outs[a].at[j - 1], send_sem=sems[0].at[3 * a + j - 1],
                    recv_sem=sems[1].at[3 * a + j - 1], device_id=to, device_id_type=MESH))
        return cps

    def start(ps, outs, sems):
        for cp in copies(ps, outs, sems):
            cp.start()

    def finish(ps, outs, sems):
        for cp in copies(ps, outs, sems):
            cp.wait()

    return _Comm(list(parts), [jax.ShapeDtypeStruct((3,) + p.shape[1:], p.dtype) for p in parts],
                 [pltpu.SemaphoreType.DMA((3 * n,)), pltpu.SemaphoreType.DMA((3 * n,))], start, finish)


def _run_comm(comm, name):
    ci, co = len(comm.ins), len(comm.outs)

    def body(*refs):
        comm.start(refs[:ci], refs[ci:ci + co], refs[ci + co:])
        comm.finish(refs[:ci], refs[ci:ci + co], refs[ci + co:])

    any_spec = pl.BlockSpec(memory_space=pl.ANY)
    comm.set_results(pl.pallas_call(
        body, in_specs=[any_spec] * ci, out_specs=[any_spec] * co, out_shape=list(comm.outs),
        scratch_shapes=list(comm.sems), name=name, compiler_params=_cparams(0))(*comm.ins))


def _pair_exchange(grads, name):
    n = len(grads)

    def body(*refs):
        gs, outs = refs[:n], refs[n:2 * n]
        send_sems, recv_sems = refs[2 * n:]
        x, y, c = _mesh_pos()
        copies = []
        for a in range(n):
            for k in range(4):
                copies.append(pltpu.make_async_remote_copy(
                    src_ref=gs[a].at[2 * k + 1 - c], dst_ref=outs[a].at[k], send_sem=send_sems.at[4 * a + k],
                    recv_sem=recv_sems.at[4 * a + k], device_id=(x, y, 1 - c), device_id_type=MESH))
        for cp in copies:
            cp.start()
        for cp in copies:
            cp.wait()

    any_spec = pl.BlockSpec(memory_space=pl.ANY)
    return pl.pallas_call(
        body, in_specs=[any_spec] * n, out_specs=[any_spec] * n,
        out_shape=[jax.ShapeDtypeStruct((4,) + g.shape[1:], g.dtype) for g in grads],
        scratch_shapes=[pltpu.SemaphoreType.DMA((4 * n,)), pltpu.SemaphoreType.DMA((4 * n,))],
        name=name, compiler_params=_cparams(0))(*grads)


def _pair_sum(grad, recv, pos):
    _, R, C = grad.shape
    br = _row_block(R, C, PAIR_SUM_ELEMS)

    def body(pos_ref, g_ref, r_ref, o_ref):
        o_ref[...] = (g_ref[...].astype(F32) + r_ref[...].astype(F32)).astype(BF)

    def chip(j, p):
        return jnp.bitwise_xor(p[1], j)

    return pl.pallas_call(
        body, grid_spec=pltpu.PrefetchScalarGridSpec(
            num_scalar_prefetch=1, grid=(4, R // br),
            in_specs=[pl.BlockSpec((None, br, C), lambda j, i, p: (2 * chip(j, p) + p[0], i, 0)),
                      pl.BlockSpec((None, br, C), lambda j, i, p: (chip(j, p), i, 0))],
            out_specs=pl.BlockSpec((None, br, C), lambda j, i, p: (j, i, 0))),
        out_shape=jax.ShapeDtypeStruct((4, R, C), BF), name="pair_sum", compiler_params=_cparams(2))(pos, grad, recv)


def _adamw(w, g, m, v):
    m = ADAM_B1 * m + (1.0 - ADAM_B1) * g
    v = ADAM_B2 * v + (1.0 - ADAM_B2) * (g * g)
    m_hat = m / (1.0 - ADAM_B1 ** ADAM_STEP)
    v_hat = v / (1.0 - ADAM_B2 ** ADAM_STEP)
    return -ADAM_LR * (m_hat / (jnp.sqrt(v_hat) + ADAM_EPS) + ADAM_WD * w), m, v


def _adamw_piece(w, m, v, piece, part, recv, bufs):
    _, R, C = w.shape
    br = _row_block(R, C)

    def body(w_ref, m_ref, v_ref, p_ref, r_ref, b0, b1, b2, b3, g_ref, d_ref, nm_ref, nv_ref):
        g = p_ref[...].astype(F32)
        for j in range(3):
            g = g + r_ref[j].astype(F32)
        d, nm, nv = _adamw(w_ref[...], g, m_ref[...], v_ref[...])
        g_ref[...] = g
        d_ref[...] = d
        nm_ref[...] = nm
        nv_ref[...] = nv

    row = pl.BlockSpec((None, br, C), lambda i: (piece, i, 0))
    any_spec = pl.BlockSpec(memory_space=pl.ANY)
    return pl.pallas_call(
        body, grid=(R // br,),
        in_specs=[row, row, row, pl.BlockSpec((None, br, C), lambda i: (0, i, 0)),
                  pl.BlockSpec((3, br, C), lambda i: (0, i, 0))] + [any_spec] * 4,
        out_specs=[row] * 4, out_shape=[jax.ShapeDtypeStruct(w.shape, F32)] * 4,
        input_output_aliases={5: 0, 6: 1, 7: 2, 8: 3}, name="adamw_piece",
        compiler_params=_cparams(1))(w, m, v, part, recv, *bufs)


def _all_reduce_small(x):
    rows = x.shape[0]

    def body(x_ref, o_ref, buf, send_sems, recv_sems):
        xp, yp, cp = _mesh_pos()
        buf[4 * xp + 2 * yp + cp] = x_ref[...]
        gather = _Gather([x_ref], [buf], send_sems, recv_sems)
        gather.start()
        gather.finish()
        acc = buf[0]
        for d in range(1, N_DEV):
            acc = acc + buf[d]
        o_ref[...] = acc

    vm = pl.BlockSpec(memory_space=pltpu.VMEM)
    return pl.pallas_call(
        body, in_specs=[vm], out_specs=vm, out_shape=jax.ShapeDtypeStruct(x.shape, F32),
        scratch_shapes=[pltpu.VMEM((N_DEV, rows, LANES), F32), pltpu.SemaphoreType.DMA((7,)),
                        pltpu.SemaphoreType.DMA((7,))],
        name="all_reduce_small", compiler_params=_cparams(0))(x)


def _sum_slots(x):
    def body(x_ref, o_ref):
        acc = x_ref[0]
        for d in range(1, N_DEV):
            acc = acc + x_ref[d]
        o_ref[...] = acc

    return pl.pallas_call(body, out_shape=jax.ShapeDtypeStruct(x.shape[1:], F32), name="sum_slots",
                          compiler_params=_cparams(0))(x)


def _adamw_small(w, g, m, v):
    def body(w_ref, g_ref, m_ref, v_ref, d_ref, nm_ref, nv_ref):
        d, nm, nv = _adamw(w_ref[...], g_ref[...], m_ref[...], v_ref[...])
        d_ref[...] = d
        nm_ref[...] = nm
        nv_ref[...] = nv

    sh = jax.ShapeDtypeStruct(w.shape, F32)
    return pl.pallas_call(body, out_shape=[sh] * 3, name="adamw_small", compiler_params=_cparams(0))(w, g, m, v)


def _pack(arrs):
    flat = jnp.concatenate([a.reshape(-1).astype(F32) for a in arrs])
    rows = -(-flat.shape[0] // (SUBLANES * LANES)) * SUBLANES
    return jnp.pad(flat, (0, rows * LANES - flat.shape[0])).reshape(rows, LANES)


def _unpack(buf, shapes):
    flat = buf.reshape(-1)
    out, off = [], 0
    for s in shapes:
        n = 1
        for d in s:
            n *= d
        out.append(flat[off:off + n].reshape(s))
        off += n
    return out


BIG = ("ffn_w1", "ffn_w3", "ffn_w2", "ab_w_in", "ab_w_out", "s5_glu_wa", "s5_glu_wb")
NAMES = ("ln_ffn_pre", "ln_mix", "ln_ffn_post", "ln_final", "ffn_w1", "ffn_w3", "ffn_w2", "ab_w_in",
         "ab_conv_w", "ab_w_out", "s5_lambda_re", "s5_lambda_im", "s5_log_dt", "s5_b_re", "s5_b_im",
         "s5_c_re", "s5_c_im", "s5_d", "s5_glu_wa", "s5_glu_wb")


def kernel(x, ln_ffn_pre, ln_mix, ln_ffn_post, ln_final, ffn_w1, ffn_w3, ffn_w2, ab_w_in, ab_conv_w, ab_w_out, s5_lambda_re, s5_lambda_im, s5_log_dt, s5_b_re, s5_b_im, s5_c_re, s5_c_im, s5_d, s5_glu_wa, s5_glu_wb, loss_target, m_ln_ffn_pre, m_ln_mix, m_ln_ffn_post, m_ln_final, m_ffn_w1, m_ffn_w3, m_ffn_w2, m_ab_w_in, m_ab_conv_w, m_ab_w_out, m_s5_lambda_re, m_s5_lambda_im, m_s5_log_dt, m_s5_b_re, m_s5_b_im, m_s5_c_re, m_s5_c_im, m_s5_d, m_s5_glu_wa, m_s5_glu_wb, v_ln_ffn_pre, v_ln_mix, v_ln_ffn_post, v_ln_final, v_ffn_w1, v_ffn_w3, v_ffn_w2, v_ab_w_in, v_ab_conv_w, v_ab_w_out, v_s5_lambda_re, v_s5_lambda_im, v_s5_log_dt, v_s5_b_re, v_s5_b_im, v_s5_c_re, v_s5_c_im, v_s5_d, v_s5_glu_wa, v_s5_glu_wb):
    w = dict(zip(NAMES, (ln_ffn_pre, ln_mix, ln_ffn_post, ln_final, ffn_w1, ffn_w3, ffn_w2, ab_w_in, ab_conv_w,
                         ab_w_out, s5_lambda_re, s5_lambda_im, s5_log_dt, s5_b_re, s5_b_im, s5_c_re, s5_c_im,
                         s5_d, s5_glu_wa, s5_glu_wb)))
    mom = dict(zip(NAMES, (m_ln_ffn_pre, m_ln_mix, m_ln_ffn_post, m_ln_final, m_ffn_w1, m_ffn_w3, m_ffn_w2,
                           m_ab_w_in, m_ab_conv_w, m_ab_w_out, m_s5_lambda_re, m_s5_lambda_im, m_s5_log_dt,
                           m_s5_b_re, m_s5_b_im, m_s5_c_re, m_s5_c_im, m_s5_d, m_s5_glu_wa, m_s5_glu_wb)))
    var = dict(zip(NAMES, (v_ln_ffn_pre, v_ln_mix, v_ln_ffn_post, v_ln_final, v_ffn_w1, v_ffn_w3, v_ffn_w2,
                           v_ab_w_in, v_ab_conv_w, v_ab_w_out, v_s5_lambda_re, v_s5_lambda_im, v_s5_log_dt,
                           v_s5_b_re, v_s5_b_im, v_s5_c_re, v_s5_c_im, v_s5_d, v_s5_glu_wa, v_s5_glu_wb)))
    nb, seq, D = x.shape
    T = nb * seq
    assert ln_mix.shape[0] == 2 and ab_w_in.shape[0] == 1 and s5_glu_wa.shape[0] == 1
    xc, yc, cc = _mesh_pos()
    dev = 4 * xc + 2 * yc + cc
    pos = jnp.stack([cc, 2 * xc + yc]).astype(jnp.int32)
    bq = min(ATTN_TILE, seq)
    tabs =_rope_tables(seq) + (_branch_bias(seq // bq, bq),)

    def ffn_piece(k, li, fj):
        return w[k][li, fj].astype(BF)

    g0 = _gather_comm([ffn_piece("ffn_w1", 0, 0), ffn_piece("ffn_w3", 0, 0), ab_conv_w[0], s5_d])
    _run_comm(g0, "gather_first")
    w1, w3 = {(0, 0): g0.results[0]}, {(0, 0): g0.results[1]}
    w2 = {}
    conv_w = g0.results[2].transpose(1, 0, 2).reshape(3, -1)
    dsk = g0.results[3].reshape(1, D)
    gains = {k: [w[k][i:i + 1] for i in range(2)] for k in ("ln_ffn_pre", "ln_mix", "ln_ffn_post")}

    h = x.reshape(T, D)
    saved = {}

    def ffn_fwd(h, gain, key, tag, comm_up, comm_down, after_up):
        n = _rms_fwd(h, gain, BF)
        t1, t3, g = _ffn_up(n, w1[key], w3[key], comm=comm_up)
        after_up()
        saved[tag] = (h, n, t1, t3, g)
        return _ffn_down(g, w2[key], h, comm=comm_down)

    c_up = _gather_comm([ffn_piece("ffn_w2", 0, 0), ab_w_out[0].astype(BF)])
    c_dn = _gather_comm([ab_w_in[0].astype(BF)])
    h = ffn_fwd(h, gains["ln_ffn_pre"][0], (0, 0), "pre0", c_up, c_dn,
                lambda: w2.update({(0, 0): c_up.results[0]}))
    wout = c_up.results[1].reshape(-1, D)
    wing = c_dn.results[0]
    c_proj = _gather_comm([ffn_piece("ffn_w1", 0, 1)])
    c_attn = _gather_comm([ffn_piece("ffn_w3", 0, 1), s5_glu_wa[0].astype(BF)])
    c_out = _gather_comm([s5_glu_wb[0].astype(BF)])
    h, saved["mix0"] = _mixer_ab_fwd(h, gains["ln_mix"][0], wing, conv_w, wout, tabs, nb, seq, c_proj, c_attn, c_out)
    w1[(0, 1)] = c_proj.results[0]
    w3[(0, 1)] = c_attn.results[0]
    wa = c_attn.results[1].reshape(-1, D)
    wb = c_out.results[0].reshape(-1, D)
    c_up2 = _gather_comm([ffn_piece("ffn_w2", 0, 1), ffn_piece("ffn_w1", 1, 0)])
    c_dn = _gather_comm([ffn_piece("ffn_w3", 1, 0)])
    h = ffn_fwd(h, gains["ln_ffn_post"][0], (0, 1), "post0", c_up2, c_dn,
                lambda: w2.update({(0, 1): c_up2.results[0]}))
    w1[(1, 0)] = c_up2.results[1]
    w3[(1, 0)] = c_dn.results[0]
    c_up3 = _gather_comm([ffn_piece("ffn_w2", 1, 0), ffn_piece("ffn_w1", 1, 1)])
    c_dn = _gather_comm([ffn_piece("ffn_w3", 1, 1)])
    h = ffn_fwd(h, gains["ln_ffn_pre"][1], (1, 0), "pre1", c_up3, c_dn,
                lambda: w2.update({(1, 0): c_up3.results[0]}))
    w1[(1, 1)] = c_up3.results[1]
    w3[(1, 1)] = c_dn.results[0]
    c_s5 = _gather_comm([ffn_piece("ffn_w2", 1, 1)])
    h, saved["mix1"] = _mixer_s5_fwd(h, gains["ln_mix"][1], w, dsk, wa, wb, nb, seq, c_s5)
    w2[(1, 1)] = c_s5.results[0]
    h = ffn_fwd(h, gains["ln_ffn_post"][1], (1, 1), "post1", None, None, lambda: None)
    dh, dhb, d_ln_final, loss_part = _loss_head(h, ln_final.reshape(1, D), loss_target.reshape(T, D))
    loss = lax.psum(loss_part[0, 0], ("x", "y", "c"))

    reduced = {}

    def reduce_start(names, grads):
        recv = _pair_exchange(grads, "pair_exchange")
        comms = []
        for nm, g, r in zip(names, grads, recv):
            part = _pair_sum(g, r, pos)
            comms.append(_exchange_comm([part]))
            reduced[nm] = (part, comms[-1])
        return comms

    def ffn_bwd(dh, dhb, key, tag, gain, carry, is_last=False, comm_dw2=None):
        h_in, n, t1, t3, g = saved[tag]
        da1, da3 = _ffn_bwd_hidden(dhb, w2[key], t1, t3, comm=carry)
        c2, = reduce_start([("ffn_w2",) + key], [_ffn_dw2(g, dhb, comm=comm_dw2)])
        dw1, dw3 = _ffn_dw13(n, da1, da3, comm=c2)
        c1, c3 = reduce_start([("ffn_w1",) + key, ("ffn_w3",) + key], [dw1, dw3])
        res = _ffn_dn_rms(da1, da3, w1[key], w3[key], h_in, gain, dh,
                          comm=_merge_comms([c1, c3]) if is_last else c1)
        return list(res) + [None if is_last else c3]

    g_small = {"ln_final": d_ln_final.reshape(D)}
    g_ln = {k: [None, None] for k in gains}
    dh, dhb, g_ln["ln_ffn_post"][1], carry = ffn_bwd(dh, dhb, (1, 1), "post1", gains["ln_ffn_post"][1], None)
    dh, dhb, g_ln["ln_mix"][1], s5_small = _mixer_s5_bwd(
        dh, saved["mix1"], gains["ln_mix"][1], wa, wb, reduce_start, carry)
    s5_names = list(s5_small)
    c_s5_grads = _gather_comm([_pack([s5_small[k] for k in s5_names])])
    dh, dhb, g_ln["ln_ffn_pre"][1], carry = ffn_bwd(dh, dhb, (1, 0), "pre1", gains["ln_ffn_pre"][1], None,
                                                    comm_dw2=c_s5_grads)
    g_red = dict(zip(s5_names, _unpack(_sum_slots(c_s5_grads.results[0]), [s5_small[k].shape for k in s5_names])))
    dh, dhb, g_ln["ln_ffn_post"][0], carry = ffn_bwd(dh, dhb, (0, 1), "post0", gains["ln_ffn_post"][0], carry)
    dh, dhb, g_ln["ln_mix"][0], g_small["ab_conv_w"], carry = _mixer_ab_bwd(
        dh, dhb, saved["mix0"], gains["ln_mix"][0], wing, conv_w, wout, tabs, nb, seq, reduce_start, carry)
    dh, dhb, g_ln["ln_ffn_pre"][0], _ = ffn_bwd(dh, dhb, (0, 0), "pre0", gains["ln_ffn_pre"][0], carry, is_last=True)
    grad_x = dh.reshape(nb, seq, D)
    for k in g_ln:
        g_small[k] = jnp.concatenate(g_ln[k], axis=0)

    out = {}
    for k in BIG:
        transposed = k in ("ffn_w1", "ffn_w3")
        pieces = [(li, fj) for li in range(2) for fj in range(2)] if w[k].ndim == 4 else [None]

        def view(a):
            a = a.swapaxes(-1, -2) if transposed else a
            return a.reshape(len(pieces), -1, a.shape[-1])

        w3d, m3d, v3d = view(w[k]), view(mom[k]), view(var[k])
        bufs = [lax.empty(w3d.shape, F32) for _ in range(4)]
        for q, key in enumerate(pieces):
            part, comm = reduced[k if key is None else (k,) + key]
            bufs = _adamw_piece(w3d, m3d, v3d, q, part, comm.results[0], bufs)
        if transposed:
            out[k] = [t.reshape(w[k].shape[:2] + w3d.shape[1:]).swapaxes(-1, -2) for t in bufs]
        else:
            out[k] = [t.reshape(w[k].shape) for t in bufs]

    small_names = [k for k in NAMES if k not in BIG]
    late_names = [k for k in small_names if k not in g_red]
    g_red.update(zip(late_names, _unpack(_all_reduce_small(_pack([g_small[k] for k in late_names])),
                                         [g_small[k].shape for k in late_names])))
    cw = w["ab_conv_w"].shape[-1]
    g_red["ab_conv_w"] = lax.dynamic_slice_in_dim(g_red["ab_conv_w"], dev * cw, cw, axis=1)[None]
    dsz = w["s5_d"].shape[-1]
    g_red["s5_d"] = lax.dynamic_slice_in_dim(g_red["s5_d"].reshape(1, -1), dev * dsz, dsz, axis=1)
    shapes = [w[k].shape for k in small_names]
    g_red = {k: g_red[k].reshape(w[k].shape) for k in small_names}
    d_s, m_s, v_s = _adamw_small(_pack([w[k] for k in small_names]), _pack([g_red[k] for k in small_names]),
                                 _pack([mom[k] for k in small_names]), _pack([var[k] for k in small_names]))
    for k, d, nm, nv in zip(small_names, _unpack(d_s, shapes), _unpack(m_s, shapes), _unpack(v_s, shapes)):
        out[k] = [g_red[k], d, nm, nv]

    return (loss, grad_x, *[out[k][0] for k in NAMES], *[out[k][1] for k in NAMES],
            *[out[k][2] for k in NAMES], *[out[k][3] for k in NAMES])
```

```python
import jax
import jax.numpy as jnp
from jax import lax
from jax.experimental import pallas as pl
from jax.experimental.pallas import tpu as pltpu

F32, BF = jnp.float32, jnp.bfloat16
N_DEV = 8
MESH = pl.DeviceIdType.MESH
LANES = 128
SUBLANES = 8
VMEM_LIMIT = 56 * 2 ** 20
ROW_TILE = 512
FFN_ROW_TILE = 1024
COL_TILE = 512
ATTN_TILE = 512
SCAN_UNROLL = 4
ELEMS_PER_BLOCK = 256 * 1024
PAIR_SUM_ELEMS = 2048 * 1024
RMS_EPS = 1e-6
ROPE_THETA = 10000.0
NEG_INF = -1e30
S5_STATE = 64
S5_GROUP = 16
GROUPS_PER_BLOCK = LANES // S5_GROUP
STATE_COLS = GROUPS_PER_BLOCK * S5_STATE
DILATED_PATTERN = ((128, 1), (512, 4), (2048, 16))
ADAM_LR, ADAM_B1, ADAM_B2, ADAM_EPS, ADAM_WD, ADAM_STEP = 0.001, 0.9, 0.999, 1e-08, 0.01, 10
GELU_C = 0.7978845608028654
GELU_A = 0.044715


def _cparams(n_grid, vmem=VMEM_LIMIT):
    sem = ("arbitrary",) * n_grid if n_grid else None
    return pltpu.CompilerParams(dimension_semantics=sem, vmem_limit_bytes=vmem)


def _sig(x):
    return 1.0 / (1.0 + jnp.exp(-x))


def _gelu(x):
    return 0.5 * x * (1.0 + jnp.tanh(GELU_C * (x + GELU_A * x * x * x)))


def _gelu_grad(x):
    t = jnp.tanh(GELU_C * (x + GELU_A * x * x * x))
    return 0.5 * (1.0 + t) + 0.5 * x * (1.0 - t * t) * GELU_C * (1.0 + 3.0 * GELU_A * x * x)


def _dot(a, b, dims):
    a = a if a.dtype == BF else a.astype(BF)
    b = b if b.dtype == BF else b.astype(BF)
    return lax.dot_general(a, b, (dims, ((), ())), preferred_element_type=F32)


NN = ((1,), (0,))
NT = ((1,), (1,))
TN = ((0,), (0,))


def _row_block(rows, cols, elems=ELEMS_PER_BLOCK, mult=16):
    cap = max(mult, elems // cols)
    best = None
    for b in range(mult, min(rows, cap) + 1, mult):
        if rows % b == 0:
            best = b
    return rows if best is None else best


class _Comm:
    def __init__(self, ins, outs, sems, start, finish, members=()):
        self.ins, self.outs, self.sems, self.start, self.finish = ins, outs, sems, start, finish
        self.members = members
        self.results = None

    def set_results(self, res):
        self.results = list(res)
        off = 0
        for m in self.members:
            m.set_results(res[off:off + len(m.outs)])
            off += len(m.outs)


def _merge_comms(comms):
    comms = [c for c in comms if c is not None]
    if len(comms) < 2:
        return comms[0] if comms else None

    def each(fn_name, ins, outs, sems):
        i = o = s = 0
        for c in comms:
            ni, no, ns = len(c.ins), len(c.outs), len(c.sems)
            getattr(c, fn_name)(ins[i:i + ni], outs[o:o + no], sems[s:s + ns])
            i, o, s = i + ni, o + no, s + ns

    return _Comm([a for c in comms for a in c.ins], [a for c in comms for a in c.outs],
                 [a for c in comms for a in c.sems],
                 lambda ins, outs, sems: each("start", ins, outs, sems),
                 lambda ins, outs, sems: each("finish", ins, outs, sems), members=tuple(comms))


def _call(body, name, grid, in_specs, out_specs, out_shape, args, scratch=(), comm=None):
    in_specs, out_specs, out_shape, scratch = list(in_specs), list(out_specs), list(out_shape), list(scratch)
    if comm is None:
        return pl.pallas_call(body, grid=grid, in_specs=in_specs, out_specs=out_specs, out_shape=out_shape,
                              scratch_shapes=scratch, name=name, compiler_params=_cparams(len(grid)))(*args)
    n_in, n_out, n_sc = len(in_specs), len(out_specs), len(scratch)
    ci, co = len(comm.ins), len(comm.outs)

    def hosted(*refs):
        ins, refs = refs[:n_in], refs[n_in:]
        cins, refs = refs[:ci], refs[ci:]
        outs, refs = refs[:n_out], refs[n_out:]
        couts, refs = refs[:co], refs[co:]
        sc, csems = refs[:n_sc], refs[n_sc:]
        first = last = None
        for d, n in enumerate(grid):
            p = pl.program_id(d)
            first = (p == 0) if first is None else first & (p == 0)
            last = (p == n - 1) if last is None else last & (p == n - 1)

        @pl.when(first)
        def _():
            comm.start(cins, couts, csems)

        body(*ins, *outs, *sc)

        @pl.when(last)
        def _():
            comm.finish(cins, couts, csems)

    any_spec = pl.BlockSpec(memory_space=pl.ANY)
    res = pl.pallas_call(
        hosted, grid=grid, in_specs=in_specs + [any_spec] * ci, out_specs=out_specs + [any_spec] * co,
        out_shape=out_shape + list(comm.outs), scratch_shapes=scratch + list(comm.sems), name=name,
        compiler_params=_cparams(len(grid)))(*args, *comm.ins)
    comm.set_results(res[n_out:])
    return list(res[:n_out])


def _mm(name, grid, operands, pairs, n_acc, acc_shape, extras, outs, epilogue, comm=None, nrow=1, ncol=1,
        whole_tile_epilogue=False):
    nk = grid[2]
    n_op, n_ex, n_out = len(operands), len(extras), len(outs)

    def part_of(ref, dim, t, n):
        if n == 1:
            return ref
        size = ref.shape[dim] // n
        idx = [slice(None)] * len(ref.shape)
        idx[dim] = pl.ds(t * size, size)
        return ref.at[tuple(idx)]

    def tile_of(ref, r, c):
        return part_of(part_of(ref, 0, r, nrow), 1, c, ncol)

    def products(op, r, c):
        parts = [None] * n_acc
        for ai, bi, dims, ci in pairs:
            a = part_of(op[ai], 1 - dims[0][0], r, nrow)
            b = part_of(op[bi], 1 - dims[1][0], c, ncol)
            d = _dot(a[...], b[...], dims)
            parts[ci] = d if parts[ci] is None else parts[ci] + d
        return parts

    def body(*refs):
        op = refs[:n_op]
        ex = refs[n_op:n_op + n_ex]
        out = refs[n_op + n_ex:n_op + n_ex + n_out]
        acc = refs[n_op + n_ex + n_out:]
        tiles = [(r, c) for r in range(nrow) for c in range(ncol)]

        def views(refs_, t):
            return [tile_of(q, *t) for q in refs_]

        if nk == 1:
            parts = products(op, *tiles[0])
            for q, t in enumerate(tiles):
                nxt = products(op, *tiles[q + 1]) if q + 1 < len(tiles) else None
                epilogue(parts, views(ex, t), views(out, t))
                parts = nxt
            return
        k = pl.program_id(2)

        @pl.when(k == 0)
        def _():
            for q in acc:
                q[...] = jnp.zeros_like(q)

        for t in tiles:
            parts = products(op, *t)
            for q, p in zip(views(acc, t), parts):
                q[...] += p

        @pl.when(k == nk - 1)
        def _():
            if whole_tile_epilogue:
                epilogue(acc, ex, out)
                return
            for t in tiles:
                epilogue([q[...] for q in views(acc, t)], views(ex, t), views(out, t))

    return _call(body, name, grid, [s for _, s in operands] + [s for _, s in extras], [s for _, s in outs],
                 [sh for sh, _ in outs], [a for a, _ in operands] + [a for a, _ in extras],
                 scratch=[pltpu.VMEM(acc_shape, F32) for _ in range(n_acc if nk > 1 else 0)], comm=comm)


def _to_seg(a, seg_len):
    T, D = a.shape
    return a.reshape(SUBLANES, seg_len, D).transpose(1, 0, 2).reshape(T, D)


def _to_tok(a, seg_len):
    T, D = a.shape
    return a.reshape(seg_len, SUBLANES, D).transpose(1, 0, 2).reshape(T, D)


def _rms_fwd(h, gain, out_dtype):
    T, D = h.shape
    bm = min(ROW_TILE, T)

    def body(h_ref, g_ref, o_ref):
        x = h_ref[...]
        r = lax.rsqrt(jnp.mean(x * x, axis=-1, keepdims=True) + RMS_EPS)
        o_ref[...] = (x * r * g_ref[...]).astype(out_dtype)

    row = pl.BlockSpec((bm, D), lambda i: (i, 0))
    return pl.pallas_call(
        body, grid=(T // bm,), in_specs=[row, pl.BlockSpec((1, D), lambda i: (0, 0))],
        out_specs=row, out_shape=jax.ShapeDtypeStruct((T, D), out_dtype), name="rms_fwd",
        compiler_params=_cparams(1))(h, gain)


def _rms_bwd_rows(dn, x, g):
    r = lax.rsqrt(jnp.mean(x * x, axis=-1, keepdims=True) + RMS_EPS)
    xh = x * r
    dng = dn * g
    dx = r * (dng - xh * jnp.mean(dng * xh, axis=-1, keepdims=True))
    return dx, jnp.sum(dn * xh, axis=0, keepdims=True)


def _rms_bwd(dn, h, gain, dh_up):
    T, D = h.shape
    bm = min(ROW_TILE, T)

    def body(dn_ref, h_ref, g_ref, up_ref, dh_ref, dhb_ref, dg_ref):
        dx, dg = _rms_bwd_rows(dn_ref[...], h_ref[...], g_ref[...])
        dh = up_ref[...] + dx
        dh_ref[...] = dh
        dhb_ref[...] = dh.astype(BF)

        @pl.when(pl.program_id(0) == 0)
        def _():
            dg_ref[...] = jnp.zeros_like(dg_ref)

        dg_ref[...] += dg

    row = pl.BlockSpec((bm, D), lambda i: (i, 0))
    vec = pl.BlockSpec((1, D), lambda i: (0, 0))
    return pl.pallas_call(
        body, grid=(T // bm,), in_specs=[row, row, vec, row], out_specs=[row, row, vec],
        out_shape=[jax.ShapeDtypeStruct((T, D), F32), jax.ShapeDtypeStruct((T, D), BF),
                   jax.ShapeDtypeStruct((1, D), F32)],
        name="rms_bwd", compiler_params=_cparams(1))(dn, h, gain, dh_up)


def _loss_head(h, gain, target):
    T, D = h.shape
    bm = min(ROW_TILE, T)

    def body(h_ref, g_ref, t_ref, dh_ref, dhb_ref, dg_ref, loss_ref):
        x = h_ref[...]
        g = g_ref[...]
        r = lax.rsqrt(jnp.mean(x * x, axis=-1, keepdims=True) + RMS_EPS)
        err = x * r * g - t_ref[...]
        part = 0.5 * jnp.sum(jnp.sum(err * err, axis=-1, keepdims=True), axis=0, keepdims=True) / D
        dx, dg = _rms_bwd_rows(err / D, x, g)
        dh_ref[...] = dx
        dhb_ref[...] = dx.astype(BF)

        @pl.when(pl.program_id(0) == 0)
        def _():
            dg_ref[...] = jnp.zeros_like(dg_ref)
            loss_ref[...] = jnp.zeros_like(loss_ref)

        dg_ref[...] += dg
        loss_ref[...] += jnp.broadcast_to(part, loss_ref.shape)

    row = pl.BlockSpec((bm, D), lambda i: (i, 0))
    vec = pl.BlockSpec((1, D), lambda i: (0, 0))
    return pl.pallas_call(
        body, grid=(T // bm,), in_specs=[row, vec, row],
        out_specs=[row, row, vec, pl.BlockSpec((SUBLANES, LANES), lambda i: (0, 0))],
        out_shape=[jax.ShapeDtypeStruct((T, D), F32), jax.ShapeDtypeStruct((T, D), BF),
                   jax.ShapeDtypeStruct((1, D), F32), jax.ShapeDtypeStruct((SUBLANES, LANES), F32)],
        name="loss_head", compiler_params=_cparams(1))(h, gain, target)


def _ffn_up(n, w1g, w3g, comm=None):
    T, D = n.shape
    fs = w1g.shape[-1]
    bm = min(FFN_ROW_TILE, T)
    wspec = pl.BlockSpec((None, D, fs), lambda s, i, k: (s, 0, 0))
    ospec = pl.BlockSpec((None, bm, fs), lambda s, i, k: (s, i, 0))

    def epi(accs, ex, outs):
        a1, a3 = accs
        sg = _sig(a1)
        silu = a1 * sg
        outs[0][...] = (a3 * sg * (1.0 + a1 * (1.0 - sg))).astype(BF)
        outs[1][...] = silu.astype(BF)
        outs[2][...] = (silu * a3).astype(BF)

    sh = jax.ShapeDtypeStruct((N_DEV, T, fs), BF)
    return _mm("ffn_up", (N_DEV, T // bm, 1),
               [(n, pl.BlockSpec((bm, D), lambda s, i, k: (i, 0))), (w1g, wspec), (w3g, wspec)],
               [(0, 1, NN, 0), (0, 2, NN, 1)], 2, None, [], [(sh, ospec)] * 3, epi, comm=comm,
               nrow=max(1, bm // ROW_TILE))


def _ffn_down(g, w2g, h, comm=None):
    _, T, fs = g.shape
    D = h.shape[1]
    bm = min(FFN_ROW_TILE, T)
    row = pl.BlockSpec((bm, D), lambda i, j, s: (i, 0))

    def epi(accs, ex, outs):
        outs[0][...] = ex[0][...] + 0.5 * accs[0]

    return _mm("ffn_down", (T // bm, 1, N_DEV),
               [(g, pl.BlockSpec((None, bm, fs), lambda i, j, s: (s, i, 0))),
                (w2g, pl.BlockSpec((None, fs, D), lambda i, j, s: (s, 0, 0)))],
               [(0, 1, NN, 0)], 1, (bm, D), [(h, row)],
               [(jax.ShapeDtypeStruct((T, D), F32), row)], epi, comm=comm,
               nrow=max(1, bm // ROW_TILE), ncol=max(1, D // COL_TILE))[0]


def _ffn_bwd_hidden(dhb, w2g, t1, t3, comm=None):
    T, D = dhb.shape
    fs = t1.shape[-1]
    bm = min(FFN_ROW_TILE, T)
    aspec = pl.BlockSpec((None, bm, fs), lambda s, i, k: (s, i, 0))

    def epi(accs, ex, outs):
        dg = 0.5 * accs[0]
        outs[0][...] = (dg * ex[0][...].astype(F32)).astype(BF)
        outs[1][...] = (dg * ex[1][...].astype(F32)).astype(BF)

    sh = jax.ShapeDtypeStruct((N_DEV, T, fs), BF)
    return _mm("ffn_bwd_hidden", (N_DEV, T // bm, 1),
               [(dhb, pl.BlockSpec((bm, D), lambda s, i, k: (i, 0))),
                (w2g, pl.BlockSpec((None, fs, D), lambda s, i, k: (s, 0, 0)))],
               [(0, 1, NT, 0)], 1, None, [(t1, aspec), (t3, aspec)], [(sh, aspec)] * 2, epi, comm=comm,
               nrow=max(1, bm // ROW_TILE))


def _ffn_dw2(g, dhb, comm=None):
    _, T, fs = g.shape
    D = dhb.shape[1]
    bn = min(COL_TILE, D)

    def epi(accs, ex, outs):
        outs[0][...] = (0.5 * accs[0]).astype(BF)

    return _mm("ffn_dw2", (N_DEV, D // bn, 1),
               [(g, pl.BlockSpec((None, T, fs), lambda s, j, k: (s, 0, 0))),
                (dhb, pl.BlockSpec((T, bn), lambda s, j, k: (0, j)))],
               [(0, 1, TN, 0)], 1, None, [],
               [(jax.ShapeDtypeStruct((N_DEV, fs, D), BF), pl.BlockSpec((None, fs, bn), lambda s, j, k: (s, 0, j)))],
               epi, comm=comm)[0]


def _ffn_dw13(n, da1, da3, comm=None):
    T, D = n.shape
    fs = da1.shape[-1]
    bn = min(COL_TILE, D)
    dspec = pl.BlockSpec((None, T, fs), lambda s, j, k: (s, 0, 0))
    ospec = pl.BlockSpec((None, fs, bn), lambda s, j, k: (s, 0, j))

    def epi(accs, ex, outs):
        outs[0][...] = accs[0].astype(BF)
        outs[1][...] = accs[1].astype(BF)

    sh = jax.ShapeDtypeStruct((N_DEV, fs, D), BF)
    return _mm("ffn_dw13", (N_DEV, D // bn, 1),
               [(da1, dspec), (da3, dspec), (n, pl.BlockSpec((T, bn), lambda s, j, k: (0, j)))],
               [(0, 2, TN, 0), (1, 2, TN, 1)], 2, None, [], [(sh, ospec)] * 2, epi, comm=comm)


def _ffn_dn_rms(da1, da3, w1g, w3g, h, gain, dh_up, comm=None):
    _, T, fs = da1.shape
    D = w1g.shape[-2]
    bm = min(ROW_TILE, T)
    rows_per_pass = min(64, bm)
    dspec = pl.BlockSpec((None, bm, fs), lambda i, j, s: (s, i, 0))
    wspec = pl.BlockSpec((None, D, fs), lambda i, j, s: (s, 0, 0))
    row = pl.BlockSpec((bm, D), lambda i, j, s: (i, 0))
    vec = pl.BlockSpec((1, D), lambda i, j, s: (0, 0))

    def epi(acc, ex, outs):
        h_ref, g_ref, up_ref = ex
        dh_ref, dhb_ref, dg_ref = outs

        @pl.when(pl.program_id(0) == 0)
        def _():
            dg_ref[...] = jnp.zeros_like(dg_ref)

        g = g_ref[...]
        dg = jnp.zeros((1, D), F32)
        for r in range(bm // rows_per_pass):
            rows = pl.ds(r * rows_per_pass, rows_per_pass)
            dx, dg_r = _rms_bwd_rows(acc[0][rows, :], h_ref[rows, :], g)
            dh = up_ref[rows, :] + dx
            dh_ref[rows, :] = dh
            dhb_ref[rows, :] = dh.astype(BF)
            dg = dg + dg_r
        dg_ref[...] += dg

    return _mm("ffn_dn_rms", (T // bm, 1, N_DEV),
               [(da1, dspec), (w1g, wspec), (da3, dspec), (w3g, wspec)],
               [(0, 1, NT, 0), (2, 3, NT, 0)], 1, (bm, D), [(h, row), (gain, vec), (dh_up, row)],
               [(jax.ShapeDtypeStruct((T, D), F32), row), (jax.ShapeDtypeStruct((T, D), BF), row),
                (jax.ShapeDtypeStruct((1, D), F32), vec)],
               epi, comm=comm, ncol=max(1, D // COL_TILE), whole_tile_epilogue=True)


def _rope_tables(seq):
    half = LANES // 2
    inv = ROPE_THETA ** (-jnp.arange(0, half, dtype=F32) * 2.0 / LANES)
    ang = jnp.arange(seq, dtype=F32)[:, None] * inv[None, :]
    cos, sin = jnp.cos(ang), jnp.sin(ang)
    return jnp.concatenate([cos, cos], axis=1), jnp.concatenate([-sin, sin], axis=1)


def _branch_bias(nq, bq):
    d = (jnp.arange(nq)[:, None, None] * bq + jnp.arange(bq)[None, :, None]
         - jnp.arange(bq)[None, None, :])
    mult = jnp.zeros(d.shape, F32)
    for window, dil in DILATED_PATTERN:
        mult = mult + ((d >= 0) & (d % dil == 0) & (d <= window)).astype(F32)
    return jnp.where(mult > 0, jnp.log(jnp.maximum(mult, 1.0)), NEG_INF)


def _proj_fwd(u, wing, comm=None):
    T, D = u.shape
    ws = wing.shape[-1]
    bm = min(FFN_ROW_TILE, T)

    def epi(accs, ex, outs):
        outs[0][...] = accs[0]

    return _mm("proj_fwd", (N_DEV, T // bm, 1),
               [(u, pl.BlockSpec((bm, D), lambda s, i, k: (i, 0))),
                (wing, pl.BlockSpec((None, D, ws), lambda s, i, k: (s, 0, 0)))],
               [(0, 1, NN, 0)], 1, None, [],
               [(jax.ShapeDtypeStruct((T, N_DEV * ws), F32),
                 pl.BlockSpec((bm, ws), lambda s, i, k: (i, s)))], epi, comm=comm,
               nrow=max(1, bm // ROW_TILE))[0]


def _rope_fwd(proj, cosf, sinf, seq, nh):
    T = proj.shape[0]
    bs = min(ROW_TILE, seq)
    nst = seq // bs
    scale = LANES ** -0.5

    def body(x_ref, c_ref, s_ref, o_ref):
        j = pl.program_id(1)
        c = c_ref[...]
        s = s_ref[...]
        mul = jnp.where(j == 0, scale, 1.0)
        for h in range(nh):
            cols = slice(h * LANES, (h + 1) * LANES)
            t = x_ref[:, cols]
            rot = (t * c + pltpu.roll(t, LANES // 2, 1) * s) * mul
            o_ref[:, cols] = jnp.where(j < 2, rot, t).astype(BF)

    blk = pl.BlockSpec((bs, nh * LANES), lambda r, j: (r, j))
    tab = pl.BlockSpec((bs, LANES), lambda r, j: (r % nst, 0))
    return pl.pallas_call(
        body, grid=(T // bs, 3), in_specs=[blk, tab, tab], out_specs=blk,
        out_shape=jax.ShapeDtypeStruct((T, 3 * nh * LANES), BF), name="rope_fwd",
        compiler_params=_cparams(2))(proj, cosf, sinf)


def _attn_fwd(qkv, bias, nb, seq, nh, comm=None):
    T = nb * seq
    bq = bias.shape[1]
    nq = seq // bq

    def body(q_ref, k_ref, v_ref, b_ref, o_ref, lse_ref):
        qi = pl.program_id(2)
        q = q_ref[...]

        def step(kj, carry):
            m, l, acc = carry
            rows = pl.ds(pl.multiple_of(kj * bq, bq), bq)
            s = _dot(q, k_ref[rows, :], NT) + b_ref[qi - kj]
            m_new = jnp.maximum(m, jnp.max(s, axis=1, keepdims=True))
            p = jnp.exp(s - m_new)
            alpha = jnp.exp(m - m_new)
            l = alpha * l + jnp.sum(p, axis=1, keepdims=True)
            acc = alpha * acc + _dot(p, v_ref[rows, :], NN)
            return m_new, l, acc

        init = (jnp.full((bq, 1), NEG_INF, F32), jnp.zeros((bq, 1), F32), jnp.zeros((bq, LANES), F32))
        m, l, acc = lax.fori_loop(0, qi + 1, step, init)
        o_ref[...] = (acc / l).astype(BF)
        lse_ref[...] = m + jnp.log(l)

    return _call(
        body, "attn_fwd", (nb, nh, nq),
        [pl.BlockSpec((bq, LANES), lambda b, h, i: (b * nq + i, h)),
         pl.BlockSpec((seq, LANES), lambda b, h, i: (b, nh + h)),
         pl.BlockSpec((seq, LANES), lambda b, h, i: (b, 2 * nh + h)),
         pl.BlockSpec((nq, bq, bq), lambda b, h, i: (0, 0, 0))],
        [pl.BlockSpec((bq, LANES), lambda b, h, i: (b * nq + i, h)),
         pl.BlockSpec((None, bq, 1), lambda b, h, i: (h, b * nq + i, 0))],
        [jax.ShapeDtypeStruct((T, 2 * nh * LANES), BF), jax.ShapeDtypeStruct((nh, T, 1), F32)],
        (qkv, qkv, qkv, bias), comm=comm)


def _attn_bwd(qkv, cat, dcat, lse, bias, nb, seq, nh, comm=None):
    T = nb * seq
    bq = bias.shape[1]
    nq = seq // bq

    def body(k_ref, v_ref, q_ref, o_ref, do_ref, lse_ref, b_ref, dq_ref, dk_ref, dv_ref):
        kj = pl.program_id(2)
        k = k_ref[...]
        v = v_ref[...]

        @pl.when(kj == 0)
        def _():
            dq_ref[...] = jnp.zeros_like(dq_ref)

        def step(qi, carry):
            dk, dv = carry
            rows = pl.ds(pl.multiple_of(qi * bq, bq), bq)
            q = q_ref[rows, :]
            do = do_ref[rows, :]
            dob = do.astype(BF)
            delta = jnp.sum(do * o_ref[rows, :].astype(F32), axis=1, keepdims=True)
            p = jnp.exp(_dot(q, k, NT) + b_ref[qi - kj] - lse_ref[rows, :])
            dv = dv + _dot(p, dob, TN)
            ds = p * (_dot(dob, v, NT) - delta)
            dq_ref[rows, :] += _dot(ds, k, NN)
            return dk + _dot(ds, q, TN), dv

        z = jnp.zeros((bq, LANES), F32)
        dk, dv = lax.fori_loop(kj, nq, step, (z, z))
        dk_ref[...] = dk
        dv_ref[...] = dv

    whole = pl.BlockSpec((seq, LANES), lambda b, h, i: (b, h))
    tile = pl.BlockSpec((bq, LANES), lambda b, h, i: (b * nq + i, h))
    sh = jax.ShapeDtypeStruct((T, nh * LANES), F32)
    return _call(
        body, "attn_bwd", (nb, nh, nq),
        [pl.BlockSpec((bq, LANES), lambda b, h, i: (b * nq + i, nh + h)),
         pl.BlockSpec((bq, LANES), lambda b, h, i: (b * nq + i, 2 * nh + h)),
         whole, whole, whole, pl.BlockSpec((None, seq, 1), lambda b, h, i: (h, b, 0)),
         pl.BlockSpec((nq, bq, bq), lambda b, h, i: (0, 0, 0))],
        [whole, tile, tile], [sh, sh, sh],
        (qkv, qkv, qkv, cat, dcat, lse, bias), comm=comm)


def _conv_parts(gc, xin, w_ref):
    w = [w_ref[k:k + 1, :] for k in range(3)]
    u = gc * xin
    row = lax.broadcasted_iota(jnp.int32, u.shape, 0)
    u1 = jnp.where(row >= 1, pltpu.roll(u, 1, 0), 0.0)
    u2 = jnp.where(row >= 2, pltpu.roll(u, 2, 0), 0.0)
    return u, u1, u2, w[0] * u2 + w[1] * u1 + w[2] * u, w, row


def _conv_fwd(proj, conv_w, cat, nb, seq, width):
    cw = min(2 * LANES, width)
    nc = width // cw

    def body(gb_ref, gc_ref, x_ref, w_ref, cat_ref, o_ref):
        _, _, _, conv, _, _ = _conv_parts(gc_ref[...], x_ref[...], w_ref)
        o_ref[...] = (gb_ref[...] * conv).astype(BF)

    def sec(k):
        return pl.BlockSpec((seq, cw), lambda b, c: (b, k * nc + c))

    return pl.pallas_call(
        body, grid=(nb, nc),
        in_specs=[sec(3), sec(4), sec(5), pl.BlockSpec((3, cw), lambda b, c: (0, c)),
                  pl.BlockSpec(memory_space=pl.ANY)],
        out_specs=pl.BlockSpec((seq, cw), lambda b, c: (b, nc + c)),
        out_shape=jax.ShapeDtypeStruct(cat.shape, BF), input_output_aliases={4: 0},
        name="conv_fwd", compiler_params=_cparams(2))(proj, proj, proj, conv_w, cat)


def _conv_bwd(proj, conv_w, dcat, nb, seq, width):
    cw = min(2 * LANES, width)
    nc = width // cw
    T = nb * seq

    def body(gb_ref, gc_ref, x_ref, w_ref, d_ref, dgb_ref, dgc_ref, dx_ref, dw_ref):
        gc = gc_ref[...]
        xin = x_ref[...]
        u, u1, u2, conv, w, row = _conv_parts(gc, xin, w_ref)
        dsc = d_ref[...]
        dgb_ref[...] = dsc * conv
        dconv = dsc * gb_ref[...]
        d1 = jnp.where(row < seq - 1, pltpu.roll(dconv, seq - 1, 0), 0.0)
        d2 = jnp.where(row < seq - 2, pltpu.roll(dconv, seq - 2, 0), 0.0)
        du = w[2] * dconv + w[1] * d1 + w[0] * d2
        dgc_ref[...] = du * xin
        dx_ref[...] = du * gc

        @pl.when(pl.program_id(1) == 0)
        def _():
            dw_ref[...] = jnp.zeros_like(dw_ref)

        dw_ref[0:1, :] += jnp.sum(dconv * u2, axis=0, keepdims=True)
        dw_ref[1:2, :] += jnp.sum(dconv * u1, axis=0, keepdims=True)
        dw_ref[2:3, :] += jnp.sum(dconv * u, axis=0, keepdims=True)

    def sec(k):
        return pl.BlockSpec((seq, cw), lambda c, b: (b, k * nc + c))

    out = pl.BlockSpec((seq, cw), lambda c, b: (b, c))
    wsp = pl.BlockSpec((3, cw), lambda c, b: (0, c))
    sh = jax.ShapeDtypeStruct((T, width), F32)
    return pl.pallas_call(
        body, grid=(nc, nb), in_specs=[sec(3), sec(4), sec(5), wsp, sec(1)],
        out_specs=[out, out, out, wsp], out_shape=[sh, sh, sh, jax.ShapeDtypeStruct((3, width), F32)],
        name="conv_bwd", compiler_params=_cparams(2))(proj, proj, proj, conv_w, dcat)


def _assemble_dproj(dq, dk, dv, dgb, dgc, dxin, cosf, sinf, seq):
    T, width = dq.shape
    nh = width // LANES
    bs = min(ROW_TILE, seq)
    nst = seq // bs
    scale = LANES ** -0.5

    def body(dq_ref, dk_ref, dv_ref, dgb_ref, dgc_ref, dx_ref, c_ref, s_ref, o_ref):
        sec = pl.program_id(1)
        c = c_ref[...]
        s = s_ref[...]

        def unrope(ref, mul):
            for h in range(nh):
                cols = slice(h * LANES, (h + 1) * LANES)
                t = ref[:, cols]
                o_ref[:, cols] = ((t * c + pltpu.roll(t * s, LANES // 2, 1)) * mul).astype(BF)

        @pl.when(sec == 0)
        def _():
            unrope(dq_ref, scale)

        @pl.when(sec == 1)
        def _():
            unrope(dk_ref, 1.0)

        for k, ref in ((2, dv_ref), (3, dgb_ref), (4, dgc_ref), (5, dx_ref)):
            @pl.when(sec == k)
            def _(ref=ref):
                o_ref[...] = ref[...].astype(BF)

    blk = pl.BlockSpec((bs, width), lambda r, k: (r, 0))
    tab = pl.BlockSpec((bs, LANES), lambda r, k: (r % nst, 0))
    return pl.pallas_call(
        body, grid=(T // bs, 6), in_specs=[blk] * 6 + [tab, tab],
        out_specs=pl.BlockSpec((bs, width), lambda r, k: (r, k)),
        out_shape=jax.ShapeDtypeStruct((T, 6 * width), BF), name="assemble_dproj",
        compiler_params=_cparams(2))(dq, dk, dv, dgb, dgc, dxin, cosf, sinf)


def _res_mm(name, a, w, h, comm=None):
    T, K = a.shape
    N = w.shape[1]
    bm = min(FFN_ROW_TILE, T)
    bk = min(ROW_TILE, K)
    row = pl.BlockSpec((bm, N), lambda i, j, k: (i, 0))

    def epi(accs, ex, outs):
        outs[0][...] = ex[0][...] + accs[0]

    return _mm(name, (T // bm, 1, K // bk),
               [(a, pl.BlockSpec((bm, bk), lambda i, j, k: (i, k))),
                (w, pl.BlockSpec((bk, N), lambda i, j, k: (k, 0)))],
               [(0, 1, NN, 0)], 1, (bm, N), [(h, row)],
               [(jax.ShapeDtypeStruct((T, N), F32), row)], epi, comm=comm,
               nrow=max(1, bm // ROW_TILE), ncol=max(1, N // COL_TILE))[0]


def _mm_nt(name, a, w, out_dtype):
    T, K = a.shape
    N = w.shape[0]
    bm = min(FFN_ROW_TILE, T)
    bn = min(ROW_TILE, N)

    def epi(accs, ex, outs):
        outs[0][...] = accs[0].astype(out_dtype)

    return _mm(name, (T // bm, N // bn, 1),
               [(a, pl.BlockSpec((bm, K), lambda i, j, k: (i, 0))),
                (w, pl.BlockSpec((bn, K), lambda i, j, k: (j, 0)))],
               [(0, 1, NT, 0)], 1, None, [],
               [(jax.ShapeDtypeStruct((T, N), out_dtype), pl.BlockSpec((bm, bn), lambda i, j, k: (i, j)))],
               epi, nrow=max(1, bm // ROW_TILE))[0]


def _mm_tn(name, a, bs_list):
    T, M = a.shape
    N = bs_list[0].shape[1]
    bmr = min(COL_TILE, M)
    bn = min(COL_TILE, N)
    n = len(bs_list)

    def epi(accs, ex, outs):
        for q in range(n):
            outs[q][...] = accs[q].astype(BF)

    ops = [(a, pl.BlockSpec((T, bmr), lambda r, j, k: (0, r)))]
    ops += [(b, pl.BlockSpec((T, bn), lambda r, j, k: (0, j))) for b in bs_list]
    return _mm(name, (M // bmr, N // bn, 1), ops, [(0, 1 + q, TN, q) for q in range(n)], n, None, [],
               [(jax.ShapeDtypeStruct((M, N), BF), pl.BlockSpec((bmr, bn), lambda r, j, k: (r, j)))] * n, epi)


def _proj_bwd_x(dproj, wing):
    T = dproj.shape[0]
    _, D, ws = wing.shape
    bm = min(FFN_ROW_TILE, T)
    row = pl.BlockSpec((bm, D), lambda i, j, s: (i, 0))

    def epi(accs, ex, outs):
        outs[0][...] = accs[0]

    return _mm("proj_bwd_x", (T // bm, 1, N_DEV),
               [(dproj, pl.BlockSpec((bm, ws), lambda i, j, s: (i, s))),
                (wing, pl.BlockSpec((None, D, ws), lambda i, j, s: (s, 0, 0)))],
               [(0, 1, NT, 0)], 1, (bm, D), [], [(jax.ShapeDtypeStruct((T, D), F32), row)], epi,
               nrow=max(1, bm // ROW_TILE), ncol=max(1, D // COL_TILE))[0]


def _proj_dw(u, dproj, ws):
    T, D = u.shape
    bmr = min(COL_TILE, D)

    def epi(accs, ex, outs):
        outs[0][...] = accs[0].astype(BF)

    return _mm("proj_dw", (N_DEV, D // bmr, 1),
               [(u, pl.BlockSpec((T, bmr), lambda s, r, k: (0, r))),
                (dproj, pl.BlockSpec((T, ws), lambda s, r, k: (0, s)))],
               [(0, 1, TN, 0)], 1, None, [],
               [(jax.ShapeDtypeStruct((N_DEV, D, ws), BF),
                 pl.BlockSpec((None, bmr, ws), lambda s, r, k: (s, r, 0)))], epi)[0]


def _mixer_ab_fwd(h, gain, wing, conv_w, wout, tabs, nb, seq, comm_proj=None, comm_attn=None, comm_out=None):
    cosf, sinf, bias = tabs
    width = wing.shape[-1] * N_DEV // 6
    nh = width // LANES
    u = _rms_fwd(h, gain, BF)
    proj = _proj_fwd(u, wing, comm=comm_proj)
    qkv = _rope_fwd(proj, cosf, sinf, seq, nh)
    cat, lse = _attn_fwd(qkv, bias, nb, seq, nh, comm=comm_attn)
    cat = _conv_fwd(proj, conv_w, cat, nb, seq, width)
    return _res_mm("outproj_fwd", cat, wout, h, comm=comm_out), (h, u, proj, qkv, cat, lse)


def _mixer_ab_bwd(dh, dhb, saved, gain, wing, conv_w, wout, tabs, nb, seq, reduce_start, carry):
    cosf, sinf, bias = tabs
    h, u, proj, qkv, cat, lse = saved
    D = h.shape[1]
    ws = wing.shape[-1]
    width = ws * N_DEV // 6
    nh = width // LANES
    dcat = _mm_nt("outproj_bwd_x", dhb, wout, F32)
    dwout = _mm_tn("outproj_dw", cat, [dhb])[0]
    comm = _merge_comms(reduce_start(["ab_w_out"], [dwout.reshape(N_DEV, -1, D)]) + [carry])
    dq, dk, dv = _attn_bwd(qkv, cat, dcat, lse, bias, nb, seq, nh, comm=comm)
    dgb, dgc, dxin, dconvw = _conv_bwd(proj, conv_w, dcat, nb, seq, width)
    dproj = _assemble_dproj(dq, dk, dv, dgb, dgc, dxin, cosf, sinf, seq)
    du = _proj_bwd_x(dproj, wing)
    comm, = reduce_start(["ab_w_in"], [_proj_dw(u, dproj, ws)])
    dh_in, dhb_in, dgain = _rms_bwd(du, h, gain, dh)
    return dh_in, dhb_in, dgain, dconvw, comm


def _s5_zoh(lr, li, log_dt):
    dt = jnp.exp(log_dt)
    mag = jnp.exp(lr * dt)
    ar = mag * jnp.cos(li * dt)
    ai = mag * jnp.sin(li * dt)
    den = lr * lr + li * li
    return dt, ar, ai, den, ((ar - 1.0) * lr + ai * li) / den, (ai * lr - (ar - 1.0) * li) / den


def _s5_discretize(lam_re, lam_im, log_dt, bt_re, bt_im):
    def body(lr_ref, li_ref, ld_ref, br_ref, bi_ref, ar_ref, ai_ref, bbr_ref, bbi_ref):
        _, ar, ai, _, fr, fi = _s5_zoh(lr_ref[...], li_ref[...], ld_ref[...])
        ar_ref[...] = ar
        ai_ref[...] = ai
        bbr_ref[...] = fr * br_ref[...] - fi * bi_ref[...]
        bbi_ref[...] = fr * bi_ref[...] + fi * br_ref[...]

    small = jax.ShapeDtypeStruct(lam_re.shape, F32)
    big = jax.ShapeDtypeStruct(bt_re.shape, F32)
    return pl.pallas_call(body, out_shape=[small, small, big, big], name="s5_discretize",
                          compiler_params=_cparams(0))(lam_re, lam_im, log_dt, bt_re, bt_im)


def _s5_discretize_bwd(lam_re, lam_im, log_dt, bt_re, bt_im, d_ar, d_ai, d_bbr, d_bbi):

    def body(lr_ref, li_ref, ld_ref, br_ref, bi_ref, dar_ref, dai_ref, dbbr_ref, dbbi_ref,
             dlr_ref, dli_ref, dld_ref, dbr_ref, dbi_ref):
        lr, li = lr_ref[...], li_ref[...]
        dt, ar, ai, den, fr, fi = _s5_zoh(lr, li, ld_ref[...])
        br, bi = br_ref[...], bi_ref[...]
        dbbr, dbbi = dbbr_ref[...], dbbi_ref[...]
        dbr_ref[...] = dbbr * fr + dbbi * fi
        dbi_ref[...] = dbbi * fr - dbbr * fi
        dfr = jnp.sum(dbbr * br + dbbi * bi, axis=1, keepdims=True)
        dfi = jnp.sum(dbbi * br - dbbr * bi, axis=1, keepdims=True)
        dnr = dfr / den
        dni = dfi / den
        dden = -(dfr * fr + dfi * fi) / den
        dar = dar_ref[...] + dnr * lr - dni * li
        dai = dai_ref[...] + dnr * li + dni * lr
        dlr_ref[...] = dnr * (ar - 1.0) + dni * ai + 2.0 * dden * lr + dt * (dar * ar + dai * ai)
        dli_ref[...] = dnr * ai - dni * (ar - 1.0) + 2.0 * dden * li + dt * (dai * ar - dar * ai)
        ddt = jnp.sum(dar * (lr * ar - li * ai) + dai * (lr * ai + li * ar), axis=2, keepdims=True)
        dld_ref[...] = ddt * dt

    small = jax.ShapeDtypeStruct(lam_re.shape, F32)
    big = jax.ShapeDtypeStruct(bt_re.shape, F32)
    return pl.pallas_call(
        body, out_shape=[small, small, jax.ShapeDtypeStruct(log_dt.shape, F32), big, big],
        name="s5_discretize_bwd", compiler_params=_cparams(0))(
            lam_re, lam_im, log_dt, bt_re, bt_im, d_ar, d_ai, d_bbr, d_bbi)


def _rows8(t):
    return pl.ds(pl.multiple_of(t * SUBLANES, SUBLANES), SUBLANES)


def _cmul_add(ar, ai, sr, si, br, bi):
    return ar * sr - ai * si + br, ar * si + ai * sr + bi


def _scan(a, read, write, init, n):
    def step(t, c):
        s = _cmul_add(*a, *c, *read(t))
        if write is not None:
            write(t, s)
        return s

    return lax.fori_loop(0, n, step, init, unroll=SCAN_UNROLL)


def _cpow(ar, ai, n):
    rr = ri = None
    while n:
        if n & 1:
            rr, ri = (ar, ai) if rr is None else (rr * ar - ri * ai, rr * ai + ri * ar)
        ar, ai = ar * ar - ai * ai, 2.0 * ar * ai
        n >>= 1
    return rr, ri


def _s5_specs(R, nj):
    sh = STATE_COLS
    return dict(
        rows=pl.BlockSpec((R, LANES), lambda j: (0, j)),
        bd=pl.BlockSpec((None, LANES, sh), lambda j: (j, 0, 0)),
        cd=pl.BlockSpec((None, sh, LANES), lambda j: (j, 0, 0)),
        a=pl.BlockSpec((None, 1, sh), lambda j: (j, 0, 0)),
        vec=pl.BlockSpec((1, LANES), lambda j: (0, j)),
        init=pl.BlockSpec((None, SUBLANES, sh), lambda j: (j, 0, 0)))


def _s5_fwd(u, mats, seg_len, nseg, comm=None):
    bdr, bdi, cdr, cdi, are, aim, dsk = mats
    R, D = u.shape
    nj = D // LANES
    sh = STATE_COLS
    rc = min(R, 512)
    sp = _s5_specs(R, nj)

    def body(u_ref, bdr_ref, bdi_ref, cdr_ref, cdi_ref, ar_ref, ai_ref, d_ref,
             y_ref, yg_ref, ir_ref, ii_ref, sre, sim):
        ar = jnp.broadcast_to(ar_ref[...], (SUBLANES, sh))
        ai = jnp.broadcast_to(ai_ref[...], (SUBLANES, sh))

        def bu_chunk(c, _):
            rows = pl.ds(pl.multiple_of(c * rc, rc), rc)
            ub = u_ref[rows, :].astype(BF)
            sre[rows, :] = _dot(ub, bdr_ref[...], NN)
            sim[rows, :] = _dot(ub, bdi_ref[...], NN)
            return 0

        lax.fori_loop(0, R // rc, bu_chunk, 0)
        z = jnp.zeros((SUBLANES, sh), F32)

        def read(t):
            return sre[_rows8(t), :], sim[_rows8(t), :]

        def write(t, s):
            sre[_rows8(t), :] = s[0]
            sim[_rows8(t), :] = s[1]

        er, ei = _scan((ar, ai), read, None, (z, z), seg_len)
        pr, pi = _cpow(ar, ai, seg_len)
        first = (lax.broadcasted_iota(jnp.int32, (SUBLANES, sh), 0) & (nseg - 1)) == 0

        def prev(x):
            return jnp.where(first, 0.0, pltpu.roll(x, 1, 0))

        xr, xi = er, ei
        for _ in range(nseg - 1):
            xr, xi = _cmul_add(pr, pi, prev(xr), prev(xi), er, ei)
        i_r, i_i = prev(xr), prev(xi)
        ir_ref[...] = i_r
        ii_ref[...] = i_i
        _scan((ar, ai), read, write, (i_r, i_i), seg_len)

        def y_chunk(c, _):
            rows = pl.ds(pl.multiple_of(c * rc, rc), rc)
            y = _dot(sre[rows, :], cdr_ref[...], NN) + _dot(sim[rows, :], cdi_ref[...], NN)
            y = y + d_ref[...] * u_ref[rows, :]
            y_ref[rows, :] = y
            yg_ref[rows, :] = _gelu(y).astype(BF)
            return 0

        lax.fori_loop(0, R // rc, y_chunk, 0)

    init_sh = jax.ShapeDtypeStruct((nj, SUBLANES, STATE_COLS), F32)
    return _call(
        body, "s5_fwd", (nj,),
        [sp["rows"], sp["bd"], sp["bd"], sp["cd"], sp["cd"], sp["a"], sp["a"], sp["vec"]],
        [sp["rows"], sp["rows"], sp["init"], sp["init"]],
        [jax.ShapeDtypeStruct((R, D), F32), jax.ShapeDtypeStruct((R, D), BF), init_sh, init_sh],
        (u, bdr, bdi, cdr, cdi, are, aim, dsk),
        scratch=[pltpu.VMEM((R, sh), F32) for _ in range(2)], comm=comm)


def _s5_bwd(u, dy, mats, init_re, init_im, seg_len, nseg, comm=None):
    bdr, bdi, cdr, cdi, are, aim, dsk = mats
    R, D = u.shape
    nj = D // LANES
    sh = STATE_COLS
    rc = min(R, 512)
    sp = _s5_specs(R, nj)

    def body(u_ref, dy_ref, bdr_ref, bdi_ref, cdr_ref, cdi_ref, ar_ref, ai_ref, d_ref, ir_ref, ii_ref,
             du_ref, dbdr_ref, dbdi_ref, dcdr_ref, dcdi_ref, dar_ref, dai_ref, dd_ref,
             sre, sim, gre, gim):
        ar = jnp.broadcast_to(ar_ref[...], (SUBLANES, sh))
        ai = jnp.broadcast_to(ai_ref[...], (SUBLANES, sh))
        i_r, i_i = ir_ref[...], ii_ref[...]

        def chunk(c):
            return pl.ds(pl.multiple_of(c * rc, rc), rc)

        def bu_chunk(c, _):
            ub = u_ref[chunk(c), :].astype(BF)
            dyb = dy_ref[chunk(c), :].astype(BF)
            sre[chunk(c), :] = _dot(ub, bdr_ref[...], NN)
            sim[chunk(c), :] = _dot(ub, bdi_ref[...], NN)
            gre[chunk(c), :] = _dot(dyb, cdr_ref[...], NT)
            gim[chunk(c), :] = _dot(dyb, cdi_ref[...], NT)
            return 0

        lax.fori_loop(0, R // rc, bu_chunk, 0)

        def read_s(t):
            return sre[_rows8(t), :], sim[_rows8(t), :]

        def read_g(t):
            return gre[_rows8(t), :], gim[_rows8(t), :]

        def both(i, c):
            s = _cmul_add(ar, ai, c[0], c[1], *read_s(i))
            sre[_rows8(i), :], sim[_rows8(i), :] = s
            return (*s, *_cmul_add(ar, -ai, c[2], c[3], *read_g(seg_len - 1 - i)))

        z = jnp.zeros((SUBLANES, sh), F32)
        _, _, fr, fi = lax.fori_loop(0, seg_len, both, (i_r, i_i, z, z), unroll=SCAN_UNROLL)

        def c_chunk(c, carry):
            dyb = dy_ref[chunk(c), :].astype(BF)
            return (carry[0] + _dot(sre[chunk(c), :], dyb, TN), carry[1] + _dot(sim[chunk(c), :], dyb, TN))

        zc = jnp.zeros((sh, LANES), F32)
        dcr, dci = lax.fori_loop(0, R // rc, c_chunk, (zc, zc))
        dcdr_ref[...] = dcr
        dcdi_ref[...] = dci
        pr, pi = _cpow(ar, ai, seg_len)
        last =(lax.broadcasted_iota(jnp.int32, (SUBLANES, sh), 0) & (nseg - 1)) == nseg - 1

        def nxt(x):
            return jnp.where(last, 0.0, pltpu.roll(x, SUBLANES - 1, 0))

        xr, xi = fr, fi
        for _ in range(nseg - 1):
            xr, xi = _cmul_add(pr, -pi, nxt(xr), nxt(xi), fr, fi)
        g0r, g0i = nxt(xr), nxt(xi)

        def adj_step(t, c, s_before):
            gr, gi = _cmul_add(ar, -ai, c[0], c[1], *read_g(t))
            gre[_rows8(t), :], gim[_rows8(t), :] = gr, gi
            spr, spi = s_before
            return gr, gi, c[2] + spr * gr + spi * gi, c[3] + spr * gi - spi * gr

        carry = lax.fori_loop(0, seg_len - 1, lambda i, c: adj_step(seg_len - 1 - i, c, read_s(seg_len - 2 - i)),
                              (g0r, g0i, z, z))
        carry = adj_step(0, carry, (i_r, i_i))
        dar_ref[...] = jnp.sum(carry[2], axis=0, keepdims=True)
        dai_ref[...] = jnp.sum(carry[3], axis=0, keepdims=True)

        def d_chunk(c, carry):
            ub = u_ref[chunk(c), :].astype(BF)
            grb = gre[chunk(c), :].astype(BF)
            gib = gim[chunk(c), :].astype(BF)
            du = _dot(grb, bdr_ref[...], NT) + _dot(gib, bdi_ref[...], NT)
            du_ref[chunk(c), :] = du + d_ref[...] * dy_ref[chunk(c), :]
            dd = carry[2] + jnp.sum(dy_ref[chunk(c), :] * u_ref[chunk(c), :], axis=0, keepdims=True)
            return carry[0] + _dot(ub, grb, TN), carry[1] + _dot(ub, gib, TN), dd

        zb = jnp.zeros((LANES, sh), F32)
        dbr, dbi, dd = lax.fori_loop(0, R // rc, d_chunk, (zb, zb, jnp.zeros((1, LANES), F32)))
        dbdr_ref[...] = dbr
        dbdi_ref[...] = dbi
        dd_ref[...] = dd

    bd_sh = jax.ShapeDtypeStruct((nj, LANES, STATE_COLS), F32)
    cd_sh = jax.ShapeDtypeStruct((nj, STATE_COLS, LANES), F32)
    a_sh = jax.ShapeDtypeStruct((nj, 1, STATE_COLS), F32)
    return _call(
        body, "s5_bwd", (nj,),
        [sp["rows"], sp["rows"], sp["bd"], sp["bd"], sp["cd"], sp["cd"], sp["a"], sp["a"],
         sp["vec"], sp["init"], sp["init"]],
        [sp["rows"], sp["bd"], sp["bd"], sp["cd"], sp["cd"], sp["a"], sp["a"], sp["vec"]],
        [jax.ShapeDtypeStruct((R, D), F32), bd_sh, bd_sh, cd_sh, cd_sh, a_sh, a_sh,
         jax.ShapeDtypeStruct((1, D), F32)],
        (u, dy, bdr, bdi, cdr, cdi, are, aim, dsk, init_re, init_im),
        scratch=[pltpu.VMEM((R, sh), F32) for _ in range(4)], comm=comm)


def _glu_fwd(yg, wa, wb, h):
    T, D = yg.shape
    N = wa.shape[1]
    bm = min(FFN_ROW_TILE, T)
    bn = min(ROW_TILE, N)
    wspec = pl.BlockSpec((D, bn), lambda i, j, k: (0, j))
    ospec = pl.BlockSpec((bm, bn), lambda i, j, k: (i, j))

    def epi(accs, ex, outs):
        pa, pb = accs
        outs[0][...] = ex[0][...] + pa * _sig(pb)
        outs[1][...] = pa.astype(BF)
        outs[2][...] = pb.astype(BF)

    return _mm("glu_fwd", (T // bm, N // bn, 1),
               [(yg, pl.BlockSpec((bm, D), lambda i, j, k: (i, 0))), (wa, wspec), (wb, wspec)],
               [(0, 1, NN, 0), (0, 2, NN, 1)], 2, None, [(h, ospec)],
               [(jax.ShapeDtypeStruct((T, N), F32), ospec), (jax.ShapeDtypeStruct((T, N), BF), ospec),
                (jax.ShapeDtypeStruct((T, N), BF), ospec)], epi, nrow=max(1, bm // ROW_TILE))


def _glu_bwd_gates(dz, pa, pb):
    T, D = dz.shape
    bm = min(ROW_TILE, T)

    def body(dz_ref, pa_ref, pb_ref, dpa_ref, dpb_ref):
        dz = dz_ref[...]
        sg = _sig(pb_ref[...].astype(F32))
        dpa_ref[...] = (dz * sg).astype(BF)
        dpb_ref[...] = (dz * pa_ref[...].astype(F32) * sg * (1.0 - sg)).astype(BF)

    row = pl.BlockSpec((bm, D), lambda i: (i, 0))
    return pl.pallas_call(
        body, grid=(T // bm,), in_specs=[row] * 3, out_specs=[row] * 2,
        out_shape=[jax.ShapeDtypeStruct((T, D), BF)] * 2, name="glu_bwd_gates",
        compiler_params=_cparams(1))(dz, pa, pb)


def _glu_bwd_y(dpa, dpb, wa, wb, y_pre, comm=None):
    T, N = dpa.shape
    D = wa.shape[0]
    bm = min(FFN_ROW_TILE, T)
    bn = min(ROW_TILE, D)
    aspec = pl.BlockSpec((bm, N), lambda i, j, k: (i, 0))
    wspec = pl.BlockSpec((bn, N), lambda i, j, k: (j, 0))
    ospec = pl.BlockSpec((bm, bn), lambda i, j, k: (i, j))

    def epi(accs, ex, outs):
        outs[0][...] = accs[0] * _gelu_grad(ex[0][...])

    return _mm("glu_bwd_y", (T // bm, D // bn, 1), [(dpa, aspec), (wa, wspec), (dpb, aspec), (wb, wspec)],
               [(0, 1, NT, 0), (2, 3, NT, 0)], 1, None, [(y_pre, ospec)],
               [(jax.ShapeDtypeStruct((T, D), F32), ospec)], epi, comm=comm, nrow=max(1, bm // ROW_TILE))[0]


def _block_diag_in(x, nj):
    g = GROUPS_PER_BLOCK
    x = x.reshape(nj, g, 1, S5_GROUP, S5_STATE)
    eye = jnp.eye(g, dtype=bool)[None, :, :, None, None]
    full = jnp.where(eye, x, 0.0)
    return full.transpose(0, 1, 3, 2, 4).reshape(nj, g * S5_GROUP, g * S5_STATE)


def _block_diag_out(x, nj):
    return _block_diag_in(x, nj).transpose(0, 2, 1)


def _diag_of_in(m, nj):
    g = GROUPS_PER_BLOCK
    m5 = m.reshape(nj, g, S5_GROUP, g, S5_STATE)
    d = jnp.diagonal(m5, axis1=1, axis2=3)
    return d.transpose(0, 3, 1, 2).reshape(nj * g, S5_GROUP, S5_STATE)


def _mixer_s5_fwd(h, gain, p, dsk, wa, wb, nb, seq, comm_s5=None):
    T, D = h.shape
    nj = D // LANES
    nseg = SUBLANES // nb
    seg_len = seq // nseg
    G = p["s5_lambda_re"].shape[1]
    lam_re = p["s5_lambda_re"].reshape(G, 1, S5_STATE)
    lam_im = p["s5_lambda_im"].reshape(G, 1, S5_STATE)
    log_dt = p["s5_log_dt"].reshape(G, 1, 1)
    bt_re = p["s5_b_re"][0].transpose(0, 2, 1)
    bt_im = p["s5_b_im"][0].transpose(0, 2, 1)
    ar, ai, bbr, bbi = _s5_discretize(lam_re, lam_im, log_dt, bt_re, bt_im)
    mats = (_block_diag_in(bbr, nj).astype(BF), _block_diag_in(bbi, nj).astype(BF),
            _block_diag_out(p["s5_c_re"][0], nj).astype(BF),
            _block_diag_out(-p["s5_c_im"][0], nj).astype(BF),
            ar.reshape(nj, 1, STATE_COLS), ai.reshape(nj, 1, STATE_COLS), dsk)
    h_seg = _to_seg(h, seg_len)
    u = _rms_fwd(h_seg, gain, F32)
    y_pre, yg, init_re, init_im = _s5_fwd(u, mats, seg_len, nseg, comm=comm_s5)
    h_out, pa, pb = _glu_fwd(yg, wa, wb, h_seg)
    disc_in = (lam_re, lam_im, log_dt, bt_re, bt_im)
    return _to_tok(h_out, seg_len), (h_seg, u, mats, y_pre, yg, init_re, init_im, pa, pb, disc_in, seg_len, nseg)


def _mixer_s5_bwd(dh, saved, gain, wa, wb, reduce_start, carry):
    h_seg, u, mats, y_pre, yg, init_re, init_im, pa, pb, disc_in, seg_len, nseg = saved
    T, D = h_seg.shape
    nj = D // LANES
    G = nj * GROUPS_PER_BLOCK
    dh_seg = _to_seg(dh, seg_len)
    dpa, dpb = _glu_bwd_gates(dh_seg, pa, pb)
    dy = _glu_bwd_y(dpa, dpb, wa, wb, y_pre)
    dwa, dwb = _mm_tn("glu_dw", yg, [dpa, dpb])
    comm = _merge_comms(reduce_start(["s5_glu_wa", "s5_glu_wb"],
                                     [dwa.reshape(N_DEV, -1, D), dwb.reshape(N_DEV, -1, D)]) + [carry])
    du, dbdr, dbdi, dcdr, dcdi, dar, dai, dd = _s5_bwd(u, dy, mats, init_re, init_im, seg_len, nseg, comm=comm)
    d_bbr = _diag_of_in(dbdr, nj)
    d_bbi = _diag_of_in(dbdi, nj)
    d_c_re = _diag_of_in(dcdr.transpose(0, 2, 1), nj)
    d_c_im = -_diag_of_in(dcdi.transpose(0, 2, 1), nj)
    dlr, dli, dld, dbr, dbi = _s5_discretize_bwd(
        *disc_in, dar.reshape(G, 1, S5_STATE), dai.reshape(G, 1, S5_STATE), d_bbr, d_bbi)
    small = {"s5_lambda_re": dlr.reshape(1, G, S5_STATE), "s5_lambda_im": dli.reshape(1, G, S5_STATE),
             "s5_log_dt": dld.reshape(1, G),
             "s5_b_re": dbr.transpose(0, 2, 1)[None], "s5_b_im": dbi.transpose(0, 2, 1)[None],
             "s5_c_re": d_c_re[None], "s5_c_im": d_c_im[None], "s5_d": dd}
    dh_in, _, dgain = _rms_bwd(du, h_seg, gain, dh_seg)
    dh_in = _to_tok(dh_in, seg_len)
    return dh_in, dh_in.astype(BF), dgain, small


def _mesh_pos():
    return lax.axis_index("x"), lax.axis_index("y"), lax.axis_index("c")


class _Gather:
    def __init__(self, srcs, slots, send_sems, recv_sems):
        self.srcs, self.slots, self.send_sems, self.recv_sems = srcs, slots, send_sems, recv_sems
        x, y, c = _mesh_pos()
        self.c = c
        self.me, self.sib = (x, y, c), (x, y, 1 - c)
        self.chips = [(1 - x, y), (x, 1 - y), (1 - x, 1 - y)]

    def copy(self, a, k, block, to, own=False):
        dst = self.slots[a].at[4 * block[0] + 2 * block[1] + block[2]]
        return pltpu.make_async_remote_copy(
            src_ref=self.srcs[a] if own else dst, dst_ref=dst, send_sem=self.send_sems.at[7 * a + k],
            recv_sem=self.recv_sems.at[7 * a + k], device_id=to, device_id_type=MESH)

    def own_copies(self, a):
        cps = [self.copy(a, 0, self.me, self.sib, own=True)]
        return cps + [self.copy(a, 1 + j, self.me, (*chip, self.c), own=True) for j, chip in enumerate(self.chips)]

    def start(self):
        for a in range(len(self.srcs)):
            for cp in self.own_copies(a):
                cp.start()

    def finish(self):
        n = len(self.srcs)
        for a in range(n):
            for j, chip in enumerate(self.chips):
                self.copy(a, 1 + j, (*chip, self.c), self.me).wait_recv()
                self.copy(a, 4 + j, (*chip, self.c), self.sib).start()
        for a in range(n):
            self.copy(a, 0, self.sib, self.me).wait_recv()
            for j, chip in enumerate(self.chips):
                self.copy(a, 4 + j, (*chip, 1 - self.c), self.me).wait_recv()
        for a in range(n):
            for cp in self.own_copies(a):
                cp.wait_send()
            for j, chip in enumerate(self.chips):
                self.copy(a, 4 + j, (*chip, self.c), self.sib).wait_send()


def _gather_comm(arrs):
    n = len(arrs)

    def local(xs, outs, sems, a):
        x, y, c = _mesh_pos()
        return pltpu.make_async_copy(xs[a], outs[a].at[4 * x + 2 * y + c], sems[2].at[a])

    def start(xs, outs, sems):
        for a in range(n):
            local(xs, outs, sems, a).start()
        _Gather(xs, outs, sems[0], sems[1]).start()

    def finish(xs, outs, sems):
        _Gather(xs, outs, sems[0], sems[1]).finish()
        for a in range(n):
            local(xs, outs, sems, a).wait()

    return _Comm(list(arrs), [jax.ShapeDtypeStruct((N_DEV,) + a.shape, a.dtype) for a in arrs],
                 [pltpu.SemaphoreType.DMA((7 * n,)), pltpu.SemaphoreType.DMA((7 * n,)),
                  pltpu.SemaphoreType.DMA((n,))], start, finish)


def _exchange_comm(parts):
    n = len(parts)

    def copies(ps, outs, sems):
        x, y, c = _mesh_pos()
        cps = []
        for a in range(n):
            for j in range(1, 4):
                to = (jnp.bitwise_xor(x, j // 2), jnp.bitwise_xor(y, j % 2), c)
                cps.append(pltpu.make_async_remote_copy(
                    src_ref=ps[a].at[j], dst_ref=outs[a].at[j - 1], send_sem=sems[0].at[3 * a + j - 1],
                    recv_sem=sems[1].at[3 * a + j - 1], device_id=to, device_id_type=MESH))
        return cps

    def start(ps, outs, sems):
        for cp in copies(ps, outs, sems):
            cp.start()

    def finish(ps, outs, sems):
        for cp in copies(ps, outs, sems):
            cp.wait()

    return _Comm(list(parts), [jax.ShapeDtypeStruct((3,) + p.shape[1:], p.dtype) for p in parts],
                 [pltpu.SemaphoreType.DMA((3 * n,)), pltpu.SemaphoreType.DMA((3 * n,))], start, finish)


def _run_comm(comm, name):
    ci, co = len(comm.ins), len(comm.outs)

    def body(*refs):
        comm.start(refs[:ci], refs[ci:ci + co], refs[ci + co:])
        comm.finish(refs[:ci], refs[ci:ci + co], refs[ci + co:])

    any_spec = pl.BlockSpec(memory_space=pl.ANY)
    comm.set_results(pl.pallas_call(
        body, in_specs=[any_spec] * ci, out_specs=[any_spec] * co, out_shape=list(comm.outs),
        scratch_shapes=list(comm.sems), name=name, compiler_params=_cparams(0))(*comm.ins))


def _pair_exchange(grads, name):
    n = len(grads)

    def body(*refs):
        gs, outs = refs[:n], refs[n:2 * n]
        send_sems, recv_sems = refs[2 * n:]
        x, y, c = _mesh_pos()
        copies = []
        for a in range(n):
            for k in range(4):
                copies.append(pltpu.make_async_remote_copy(
                    src_ref=gs[a].at[2 * k + 1 - c], dst_ref=outs[a].at[k], send_sem=send_sems.at[4 * a + k],
                    recv_sem=recv_sems.at[4 * a + k], device_id=(x, y, 1 - c), device_id_type=MESH))
        for cp in copies:
            cp.start()
        for cp in copies:
            cp.wait()

    any_spec = pl.BlockSpec(memory_space=pl.ANY)
    return pl.pallas_call(
        body, in_specs=[any_spec] * n, out_specs=[any_spec] * n,
        out_shape=[jax.ShapeDtypeStruct((4,) + g.shape[1:], g.dtype) for g in grads],
        scratch_shapes=[pltpu.SemaphoreType.DMA((4 * n,)), pltpu.SemaphoreType.DMA((4 * n,))],
        name=name, compiler_params=_cparams(0))(*grads)


def _pair_sum(grad, recv, pos):
    _, R, C = grad.shape
    br = _row_block(R, C, PAIR_SUM_ELEMS)

    def body(pos_ref, g_ref, r_ref, o_ref):
        o_ref[...] = (g_ref[...].astype(F32) + r_ref[...].astype(F32)).astype(BF)

    def chip(j, p):
        return jnp.bitwise_xor(p[1], j)

    return pl.pallas_call(
        body, grid_spec=pltpu.PrefetchScalarGridSpec(
            num_scalar_prefetch=1, grid=(4, R // br),
            in_specs=[pl.BlockSpec((None, br, C), lambda j, i, p: (2 * chip(j, p) + p[0], i, 0)),
                      pl.BlockSpec((None, br, C), lambda j, i, p: (chip(j, p), i, 0))],
            out_specs=pl.BlockSpec((None, br, C), lambda j, i, p: (j, i, 0))),
        out_shape=jax.ShapeDtypeStruct((4, R, C), BF), name="pair_sum", compiler_params=_cparams(2))(pos, grad, recv)


def _adamw(w, g, m, v):
    m = ADAM_B1 * m + (1.0 - ADAM_B1) * g
    v = ADAM_B2 * v + (1.0 - ADAM_B2) * (g * g)
    m_hat = m / (1.0 - ADAM_B1 ** ADAM_STEP)
    v_hat = v / (1.0 - ADAM_B2 ** ADAM_STEP)
    return -ADAM_LR * (m_hat / (jnp.sqrt(v_hat) + ADAM_EPS) + ADAM_WD * w), m, v


def _adamw_piece(w, m, v, piece, part, recv, bufs):
    _, R, C = w.shape
    br = _row_block(R, C)

    def body(w_ref, m_ref, v_ref, p_ref, r_ref, b0, b1, b2, b3, g_ref, d_ref, nm_ref, nv_ref):
        g = p_ref[...].astype(F32)
        for j in range(3):
            g = g + r_ref[j].astype(F32)
        d, nm, nv = _adamw(w_ref[...], g, m_ref[...], v_ref[...])
        g_ref[...] = g
        d_ref[...] = d
        nm_ref[...] = nm
        nv_ref[...] = nv

    row = pl.BlockSpec((None, br, C), lambda i: (piece, i, 0))
    any_spec = pl.BlockSpec(memory_space=pl.ANY)
    return pl.pallas_call(
        body, grid=(R // br,),
        in_specs=[row, row, row, pl.BlockSpec((None, br, C), lambda i: (0, i, 0)),
                  pl.BlockSpec((3, br, C), lambda i: (0, i, 0))] + [any_spec] * 4,
        out_specs=[row] * 4, out_shape=[jax.ShapeDtypeStruct(w.shape, F32)] * 4,
        input_output_aliases={5: 0, 6: 1, 7: 2, 8: 3}, name="adamw_piece",
        compiler_params=_cparams(1))(w, m, v, part, recv, *bufs)


def _all_reduce_small(x):
    rows = x.shape[0]

    def body(x_ref, o_ref, buf, send_sems, recv_sems):
        xp, yp, cp = _mesh_pos()
        buf[4 * xp + 2 * yp + cp] = x_ref[...]
        gather = _Gather([x_ref], [buf], send_sems, recv_sems)
        gather.start()
        gather.finish()
        acc = buf[0]
        for d in range(1, N_DEV):
            acc = acc + buf[d]
        o_ref[...] = acc

    vm = pl.BlockSpec(memory_space=pltpu.VMEM)
    return pl.pallas_call(
        body, in_specs=[vm], out_specs=vm, out_shape=jax.ShapeDtypeStruct(x.shape, F32),
        scratch_shapes=[pltpu.VMEM((N_DEV, rows, LANES), F32), pltpu.SemaphoreType.DMA((7,)),
                        pltpu.SemaphoreType.DMA((7,))],
        name="all_reduce_small", compiler_params=_cparams(0))(x)


def _sum_slots(x):
    def body(x_ref, o_ref):
        acc = x_ref[0]
        for d in range(1, N_DEV):
            acc = acc + x_ref[d]
        o_ref[...] = acc

    return pl.pallas_call(body, out_shape=jax.ShapeDtypeStruct(x.shape[1:], F32), name="sum_slots",
                          compiler_params=_cparams(0))(x)


def _adamw_small(w, g, m, v):
    def body(w_ref, g_ref, m_ref, v_ref, d_ref, nm_ref, nv_ref):
        d, nm, nv = _adamw(w_ref[...], g_ref[...], m_ref[...], v_ref[...])
        d_ref[...] = d
        nm_ref[...] = nm
        nv_ref[...] = nv

    sh = jax.ShapeDtypeStruct(w.shape, F32)
    return pl.pallas_call(body, out_shape=[sh] * 3, name="adamw_small", compiler_params=_cparams(0))(w, g, m, v)


def _pack(arrs):
    flat = jnp.concatenate([a.reshape(-1).astype(F32) for a in arrs])
    rows = -(-flat.shape[0] // (SUBLANES * LANES)) * SUBLANES
    return jnp.pad(flat, (0, rows * LANES - flat.shape[0])).reshape(rows, LANES)


def _unpack(buf, shapes):
    flat = buf.reshape(-1)
    out, off = [], 0
    for s in shapes:
        n = 1
        for d in s:
            n *= d
        out.append(flat[off:off + n].reshape(s))
        off += n
    return out


BIG = ("ffn_w1", "ffn_w3", "ffn_w2", "ab_w_in", "ab_w_out", "s5_glu_wa", "s5_glu_wb")
NAMES = ("ln_ffn_pre", "ln_mix", "ln_ffn_post", "ln_final", "ffn_w1", "ffn_w3", "ffn_w2", "ab_w_in",
         "ab_conv_w", "ab_w_out", "s5_lambda_re", "s5_lambda_im", "s5_log_dt", "s5_b_re", "s5_b_im",
         "s5_c_re", "s5_c_im", "s5_d", "s5_glu_wa", "s5_glu_wb")


def kernel(x, ln_ffn_pre, ln_mix, ln_ffn_post, ln_final, ffn_w1, ffn_w3, ffn_w2, ab_w_in, ab_conv_w, ab_w_out, s5_lambda_re, s5_lambda_im, s5_log_dt, s5_b_re, s5_b_im, s5_c_re, s5_c_im, s5_d, s5_glu_wa, s5_glu_wb, loss_target, m_ln_ffn_pre, m_ln_mix, m_ln_ffn_post, m_ln_final, m_ffn_w1, m_ffn_w3, m_ffn_w2, m_ab_w_in, m_ab_conv_w, m_ab_w_out, m_s5_lambda_re, m_s5_lambda_im, m_s5_log_dt, m_s5_b_re, m_s5_b_im, m_s5_c_re, m_s5_c_im, m_s5_d, m_s5_glu_wa, m_s5_glu_wb, v_ln_ffn_pre, v_ln_mix, v_ln_ffn_post, v_ln_final, v_ffn_w1, v_ffn_w3, v_ffn_w2, v_ab_w_in, v_ab_conv_w, v_ab_w_out, v_s5_lambda_re, v_s5_lambda_im, v_s5_log_dt, v_s5_b_re, v_s5_b_im, v_s5_c_re, v_s5_c_im, v_s5_d, v_s5_glu_wa, v_s5_glu_wb):
    w = dict(zip(NAMES, (ln_ffn_pre, ln_mix, ln_ffn_post, ln_final, ffn_w1, ffn_w3, ffn_w2, ab_w_in, ab_conv_w,
                         ab_w_out, s5_lambda_re, s5_lambda_im, s5_log_dt, s5_b_re, s5_b_im, s5_c_re, s5_c_im,
                         s5_d, s5_glu_wa, s5_glu_wb)))
    mom = dict(zip(NAMES, (m_ln_ffn_pre, m_ln_mix, m_ln_ffn_post, m_ln_final, m_ffn_w1, m_ffn_w3, m_ffn_w2,
                           m_ab_w_in, m_ab_conv_w, m_ab_w_out, m_s5_lambda_re, m_s5_lambda_im, m_s5_log_dt,
                           m_s5_b_re, m_s5_b_im, m_s5_c_re, m_s5_c_im, m_s5_d, m_s5_glu_wa, m_s5_glu_wb)))
    var = dict(zip(NAMES, (v_ln_ffn_pre, v_ln_mix, v_ln_ffn_post, v_ln_final, v_ffn_w1, v_ffn_w3, v_ffn_w2,
                           v_ab_w_in, v_ab_conv_w, v_ab_w_out, v_s5_lambda_re, v_s5_lambda_im, v_s5_log_dt,
                           v_s5_b_re, v_s5_b_im, v_s5_c_re, v_s5_c_im, v_s5_d, v_s5_glu_wa, v_s5_glu_wb)))
    nb, seq, D = x.shape
    T = nb * seq
    assert ln_mix.shape[0] == 2 and ab_w_in.shape[0] == 1 and s5_glu_wa.shape[0] == 1
    xc, yc, cc = _mesh_pos()
    dev = 4 * xc + 2 * yc + cc
    pos = jnp.stack([cc, 2 * xc + yc]).astype(jnp.int32)
    bq = min(ATTN_TILE, seq)
    tabs =_rope_tables(seq) + (_branch_bias(seq // bq, bq),)

    def ffn_piece(k, li, fj):
        return w[k][li, fj].astype(BF)

    g0 = _gather_comm([ffn_piece("ffn_w1", 0, 0), ffn_piece("ffn_w3", 0, 0), ab_conv_w[0], s5_d])
    _run_comm(g0, "gather_first")
    w1, w3 = {(0, 0): g0.results[0]}, {(0, 0): g0.results[1]}
    w2 = {}
    conv_w = g0.results[2].transpose(1, 0, 2).reshape(3, -1)
    dsk = g0.results[3].reshape(1, D)
    gains = {k: [w[k][i:i + 1] for i in range(2)] for k in ("ln_ffn_pre", "ln_mix", "ln_ffn_post")}

    h = x.reshape(T, D)
    saved = {}

    def ffn_fwd(h, gain, key, tag, comm_up, comm_down, after_up):
        n = _rms_fwd(h, gain, BF)
        t1, t3, g = _ffn_up(n, w1[key], w3[key], comm=comm_up)
        after_up()
        saved[tag] = (h, n, t1, t3, g)
        return _ffn_down(g, w2[key], h, comm=comm_down)

    c_up = _gather_comm([ffn_piece("ffn_w2", 0, 0), ab_w_out[0].astype(BF)])
    c_dn = _gather_comm([ab_w_in[0].astype(BF)])
    h = ffn_fwd(h, gains["ln_ffn_pre"][0], (0, 0), "pre0", c_up, c_dn,
                lambda: w2.update({(0, 0): c_up.results[0]}))
    wout = c_up.results[1].reshape(-1, D)
    wing = c_dn.results[0]
    c_proj = _gather_comm([ffn_piece("ffn_w1", 0, 1)])
    c_attn = _gather_comm([ffn_piece("ffn_w3", 0, 1), s5_glu_wa[0].astype(BF)])
    c_out = _gather_comm([s5_glu_wb[0].astype(BF)])
    h, saved["mix0"] = _mixer_ab_fwd(h, gains["ln_mix"][0], wing, conv_w, wout, tabs, nb, seq, c_proj, c_attn, c_out)
    w1[(0, 1)] = c_proj.results[0]
    w3[(0, 1)] = c_attn.results[0]
    wa = c_attn.results[1].reshape(-1, D)
    wb = c_out.results[0].reshape(-1, D)
    c_up2 = _gather_comm([ffn_piece("ffn_w2", 0, 1), ffn_piece("ffn_w1", 1, 0)])
    c_dn = _gather_comm([ffn_piece("ffn_w3", 1, 0)])
    h = ffn_fwd(h, gains["ln_ffn_post"][0], (0, 1), "post0", c_up2, c_dn,
                lambda: w2.update({(0, 1): c_up2.results[0]}))
    w1[(1, 0)] = c_up2.results[1]
    w3[(1, 0)] = c_dn.results[0]
    c_up3 = _gather_comm([ffn_piece("ffn_w2", 1, 0), ffn_piece("ffn_w1", 1, 1)])
    c_dn = _gather_comm([ffn_piece("ffn_w3", 1, 1)])
    h = ffn_fwd(h, gains["ln_ffn_pre"][1], (1, 0), "pre1", c_up3, c_dn,
                lambda: w2.update({(1, 0): c_up3.results[0]}))
    w1[(1, 1)] = c_up3.results[1]
    w3[(1, 1)] = c_dn.results[0]
    c_s5 = _gather_comm([ffn_piece("ffn_w2", 1, 1)])
    h, saved["mix1"] = _mixer_s5_fwd(h, gains["ln_mix"][1], w, dsk, wa, wb, nb, seq, c_s5)
    w2[(1, 1)] = c_s5.results[0]
    h = ffn_fwd(h, gains["ln_ffn_post"][1], (1, 1), "post1", None, None, lambda: None)
    dh, dhb, d_ln_final, loss_part = _loss_head(h, ln_final.reshape(1, D), loss_target.reshape(T, D))
    loss = lax.psum(loss_part[0, 0], ("x", "y", "c"))

    reduced = {}

    def reduce_start(names, grads):
        recv = _pair_exchange(grads, "pair_exchange")
        comms = []
        for nm, g, r in zip(names, grads, recv):
            part = _pair_sum(g, r, pos)
            comms.append(_exchange_comm([part]))
            reduced[nm] = (part, comms[-1])
        return comms

    def ffn_bwd(dh, dhb, key, tag, gain, carry, is_last=False, comm_dw2=None):
        h_in, n, t1, t3, g = saved[tag]
        da1, da3 = _ffn_bwd_hidden(dhb, w2[key], t1, t3, comm=carry)
        c2, = reduce_start([("ffn_w2",) + key], [_ffn_dw2(g, dhb, comm=comm_dw2)])
        dw1, dw3 = _ffn_dw13(n, da1, da3, comm=c2)
        c1, c3 = reduce_start([("ffn_w1",) + key, ("ffn_w3",) + key], [dw1, dw3])
        res = _ffn_dn_rms(da1, da3, w1[key], w3[key], h_in, gain, dh,
                          comm=_merge_comms([c1, c3]) if is_last else c1)
        return list(res) + [None if is_last else c3]

    g_small = {"ln_final": d_ln_final.reshape(D)}
    g_ln = {k: [None, None] for k in gains}
    dh, dhb, g_ln["ln_ffn_post"][1], carry = ffn_bwd(dh, dhb, (1, 1), "post1", gains["ln_ffn_post"][1], None)
    dh, dhb, g_ln["ln_mix"][1], s5_small = _mixer_s5_bwd(
        dh, saved["mix1"], gains["ln_mix"][1], wa, wb, reduce_start, carry)
    s5_names = list(s5_small)
    c_s5_grads = _gather_comm([_pack([s5_small[k] for k in s5_names])])
    dh, dhb, g_ln["ln_ffn_pre"][1], carry = ffn_bwd(dh, dhb, (1, 0), "pre1", gains["ln_ffn_pre"][1], None,
                                                    comm_dw2=c_s5_grads)
    g_red = dict(zip(s5_names, _unpack(_sum_slots(c_s5_grads.results[0]), [s5_small[k].shape for k in s5_names])))
    dh, dhb, g_ln["ln_ffn_post"][0], carry = ffn_bwd(dh, dhb, (0, 1), "post0", gains["ln_ffn_post"][0], carry)
    dh, dhb, g_ln["ln_mix"][0], g_small["ab_conv_w"], carry = _mixer_ab_bwd(
        dh, dhb, saved["mix0"], gains["ln_mix"][0], wing, conv_w, wout, tabs, nb, seq, reduce_start, carry)
    dh, dhb, g_ln["ln_ffn_pre"][0], _ = ffn_bwd(dh, dhb, (0, 0), "pre0", gains["ln_ffn_pre"][0], carry, is_last=True)
    grad_x = dh.reshape(nb, seq, D)
    for k in g_ln:
        g_small[k] = jnp.concatenate(g_ln[k], axis=0)

    out = {}
    for k in BIG:
        transposed = k in ("ffn_w1", "ffn_w3")
        pieces = [(li, fj) for li in range(2) for fj in range(2)] if w[k].ndim == 4 else [None]

        def view(a):
            a = a.swapaxes(-1, -2) if transposed else a
            return a.reshape(len(pieces), -1, a.shape[-1])

        w3d, m3d, v3d = view(w[k]), view(mom[k]), view(var[k])
        bufs = [lax.empty(w3d.shape, F32) for _ in range(4)]
        for q, key in enumerate(pieces):
            part, comm = reduced[k if key is None else (k,) + key]
            bufs = _adamw_piece(w3d, m3d, v3d, q, part, comm.results[0], bufs)
        if transposed:
            out[k] = [t.reshape(w[k].shape[:2] + w3d.shape[1:]).swapaxes(-1, -2) for t in bufs]
        else:
            out[k] = [t.reshape(w[k].shape) for t in bufs]

    small_names = [k for k in NAMES if k not in BIG]
    late_names = [k for k in small_names if k not in g_red]
    g_red.update(zip(late_names, _unpack(_all_reduce_small(_pack([g_small[k] for k in late_names])),
                                         [g_small[k].shape for k in late_names])))
    cw = w["ab_conv_w"].shape[-1]
    g_red["ab_conv_w"] = lax.dynamic_slice_in_dim(g_red["ab_conv_w"], dev * cw, cw, axis=1)[None]
    dsz = w["s5_d"].shape[-1]
    g_red["s5_d"] = lax.dynamic_slice_in_dim(g_red["s5_d"].reshape(1, -1), dev * dsz, dsz, axis=1)
    shapes = [w[k].shape for k in small_names]
    g_red = {k: g_red[k].reshape(w[k].shape) for k in small_names}
    d_s, m_s, v_s = _adamw_small(_pack([w[k] for k in small_names]), _pack([g_red[k] for k in small_names]),
                                 _pack([mom[k] for k in small_names]), _pack([var[k] for k in small_names]))
    for k, d, nm, nv in zip(small_names, _unpack(d_s, shapes), _unpack(m_s, shapes), _unpack(v_s, shapes)):
        out[k] = [g_red[k], d, nm, nv]

    return (loss, grad_x, *[out[k][0] for k in NAMES], *[out[k][1] for k in NAMES],
            *[out[k][2] for k in NAMES], *[out[k][3] for k in NAMES])
```

```python
import jax
import jax.numpy as jnp
from jax import lax
from jax.experimental import pallas as pl
from jax.experimental.pallas import tpu as pltpu

F32, BF = jnp.float32, jnp.bfloat16
N_DEV = 8
MESH = pl.DeviceIdType.MESH
LANES = 128
SUBLANES = 8
VMEM_LIMIT = 56 * 2 ** 20
ROW_TILE = 512
FFN_ROW_TILE = 1024
COL_TILE = 512
ATTN_TILE = 512
SCAN_UNROLL = 4
ELEMS_PER_BLOCK = 512 * 1024
PAIR_SUM_ELEMS = 2048 * 1024
RMS_EPS = 1e-6
ROPE_THETA = 10000.0
NEG_INF = -1e30
S5_STATE = 64
S5_GROUP = 16
GROUPS_PER_BLOCK = LANES // S5_GROUP
STATE_COLS = GROUPS_PER_BLOCK * S5_STATE
DILATED_PATTERN = ((128, 1), (512, 4), (2048, 16))
ADAM_LR, ADAM_B1, ADAM_B2, ADAM_EPS, ADAM_WD, ADAM_STEP = 0.001, 0.9, 0.999, 1e-08, 0.01, 10
GELU_C = 0.7978845608028654
GELU_A = 0.044715


def _cparams(n_grid, vmem=VMEM_LIMIT):
    sem = ("arbitrary",) * n_grid if n_grid else None
    return pltpu.CompilerParams(dimension_semantics=sem, vmem_limit_bytes=vmem)


def _sig(x):
    return 1.0 / (1.0 + jnp.exp(-x))


def _gelu(x):
    return 0.5 * x * (1.0 + jnp.tanh(GELU_C * (x + GELU_A * x * x * x)))


def _gelu_grad(x):
    t = jnp.tanh(GELU_C * (x + GELU_A * x * x * x))
    return 0.5 * (1.0 + t) + 0.5 * x * (1.0 - t * t) * GELU_C * (1.0 + 3.0 * GELU_A * x * x)


def _dot(a, b, dims):
    a = a if a.dtype == BF else a.astype(BF)
    b = b if b.dtype == BF else b.astype(BF)
    return lax.dot_general(a, b, (dims, ((), ())), preferred_element_type=F32)


NN = ((1,), (0,))
NT = ((1,), (1,))
TN = ((0,), (0,))


def _row_block(rows, cols, elems=ELEMS_PER_BLOCK, mult=16):
    cap = max(mult, elems // cols)
    best = None
    for b in range(mult, min(rows, cap) + 1, mult):
        if rows % b == 0:
            best = b
    return rows if best is None else best


class _Comm:
    def __init__(self, ins, outs, sems, start, finish, members=()):
        self.ins, self.outs, self.sems, self.start, self.finish = ins, outs, sems, start, finish
        self.members = members
        self.results = None

    def set_results(self, res):
        self.results = list(res)
        off = 0
        for m in self.members:
            m.set_results(res[off:off + len(m.outs)])
            off += len(m.outs)


def _merge_comms(comms):
    comms = [c for c in comms if c is not None]
    if len(comms) < 2:
        return comms[0] if comms else None

    def each(fn_name, ins, outs, sems):
        i = o = s = 0
        for c in comms:
            ni, no, ns = len(c.ins), len(c.outs), len(c.sems)
            getattr(c, fn_name)(ins[i:i + ni], outs[o:o + no], sems[s:s + ns])
            i, o, s = i + ni, o + no, s + ns

    return _Comm([a for c in comms for a in c.ins], [a for c in comms for a in c.outs],
                 [a for c in comms for a in c.sems],
                 lambda ins, outs, sems: each("start", ins, outs, sems),
                 lambda ins, outs, sems: each("finish", ins, outs, sems), members=tuple(comms))


def _call(body, name, grid, in_specs, out_specs, out_shape, args, scratch=(), comm=None):
    in_specs, out_specs, out_shape, scratch = list(in_specs), list(out_specs), list(out_shape), list(scratch)
    if comm is None:
        return pl.pallas_call(body, grid=grid, in_specs=in_specs, out_specs=out_specs, out_shape=out_shape,
                              scratch_shapes=scratch, name=name, compiler_params=_cparams(len(grid)))(*args)
    n_in, n_out, n_sc = len(in_specs), len(out_specs), len(scratch)
    ci, co = len(comm.ins), len(comm.outs)

    def hosted(*refs):
        ins, refs = refs[:n_in], refs[n_in:]
        cins, refs = refs[:ci], refs[ci:]
        outs, refs = refs[:n_out], refs[n_out:]
        couts, refs = refs[:co], refs[co:]
        sc, csems = refs[:n_sc], refs[n_sc:]
        first = last = None
        for d, n in enumerate(grid):
            p = pl.program_id(d)
            first = (p == 0) if first is None else first & (p == 0)
            last = (p == n - 1) if last is None else last & (p == n - 1)

        @pl.when(first)
        def _():
            comm.start(cins, couts, csems)

        body(*ins, *outs, *sc)

        @pl.when(last)
        def _():
            comm.finish(cins, couts, csems)

    any_spec = pl.BlockSpec(memory_space=pl.ANY)
    res = pl.pallas_call(
        hosted, grid=grid, in_specs=in_specs + [any_spec] * ci, out_specs=out_specs + [any_spec] * co,
        out_shape=out_shape + list(comm.outs), scratch_shapes=scratch + list(comm.sems), name=name,
        compiler_params=_cparams(len(grid)))(*args, *comm.ins)
    comm.set_results(res[n_out:])
    return list(res[:n_out])


def _mm(name, grid, operands, pairs, n_acc, acc_shape, extras, outs, epilogue, comm=None, nrow=1, ncol=1,
        whole_tile_epilogue=False):
    nk = grid[2]
    n_op, n_ex, n_out = len(operands), len(extras), len(outs)

    def part_of(ref, dim, t, n):
        if n == 1:
            return ref
        size = ref.shape[dim] // n
        idx = [slice(None)] * len(ref.shape)
        idx[dim] = pl.ds(t * size, size)
        return ref.at[tuple(idx)]

    def tile_of(ref, r, c):
        return part_of(part_of(ref, 0, r, nrow), 1, c, ncol)

    def products(op, r, c):
        parts = [None] * n_acc
        for ai, bi, dims, ci in pairs:
            a = part_of(op[ai], 1 - dims[0][0], r, nrow)
            b = part_of(op[bi], 1 - dims[1][0], c, ncol)
            d = _dot(a[...], b[...], dims)
            parts[ci] = d if parts[ci] is None else parts[ci] + d
        return parts

    def body(*refs):
        op = refs[:n_op]
        ex = refs[n_op:n_op + n_ex]
        out = refs[n_op + n_ex:n_op + n_ex + n_out]
        acc = refs[n_op + n_ex + n_out:]
        tiles = [(r, c) for r in range(nrow) for c in range(ncol)]

        def views(refs_, t):
            return [tile_of(q, *t) for q in refs_]

        if nk == 1:
            parts = products(op, *tiles[0])
            for q, t in enumerate(tiles):
                nxt = products(op, *tiles[q + 1]) if q + 1 < len(tiles) else None
                epilogue(parts, views(ex, t), views(out, t))
                parts = nxt
            return
        k = pl.program_id(2)

        @pl.when(k == 0)
        def _():
            for q in acc:
                q[...] = jnp.zeros_like(q)

        for t in tiles:
            parts = products(op, *t)
            for q, p in zip(views(acc, t), parts):
                q[...] += p

        @pl.when(k == nk - 1)
        def _():
            if whole_tile_epilogue:
                epilogue(acc, ex, out)
                return
            for t in tiles:
                epilogue([q[...] for q in views(acc, t)], views(ex, t), views(out, t))

    return _call(body, name, grid, [s for _, s in operands] + [s for _, s in extras], [s for _, s in outs],
                 [sh for sh, _ in outs], [a for a, _ in operands] + [a for a, _ in extras],
                 scratch=[pltpu.VMEM(acc_shape, F32) for _ in range(n_acc if nk > 1 else 0)], comm=comm)


def _to_seg(a, seg_len):
    T, D = a.shape
    return a.reshape(SUBLANES, seg_len, D).transpose(1, 0, 2).reshape(T, D)


def _to_tok(a, seg_len):
    T, D = a.shape
    return a.reshape(seg_len, SUBLANES, D).transpose(1, 0, 2).reshape(T, D)


def _rms_fwd(h, gain, out_dtype):
    T, D = h.shape
    bm = min(ROW_TILE, T)

    def body(h_ref, g_ref, o_ref):
        x = h_ref[...]
        r = lax.rsqrt(jnp.mean(x * x, axis=-1, keepdims=True) + RMS_EPS)
        o_ref[...] = (x * r * g_ref[...]).astype(out_dtype)

    row = pl.BlockSpec((bm, D), lambda i: (i, 0))
    return pl.pallas_call(
        body, grid=(T // bm,), in_specs=[row, pl.BlockSpec((1, D), lambda i: (0, 0))],
        out_specs=row, out_shape=jax.ShapeDtypeStruct((T, D), out_dtype), name="rms_fwd",
        compiler_params=_cparams(1))(h, gain)


def _rms_bwd_rows(dn, x, g):
    r = lax.rsqrt(jnp.mean(x * x, axis=-1, keepdims=True) + RMS_EPS)
    xh = x * r
    dng = dn * g
    dx = r * (dng - xh * jnp.mean(dng * xh, axis=-1, keepdims=True))
    return dx, jnp.sum(dn * xh, axis=0, keepdims=True)


def _rms_bwd(dn, h, gain, dh_up):
    T, D = h.shape
    bm = min(ROW_TILE, T)

    def body(dn_ref, h_ref, g_ref, up_ref, dh_ref, dhb_ref, dg_ref):
        dx, dg = _rms_bwd_rows(dn_ref[...], h_ref[...], g_ref[...])
        dh = up_ref[...] + dx
        dh_ref[...] = dh
        dhb_ref[...] = dh.astype(BF)

        @pl.when(pl.program_id(0) == 0)
        def _():
            dg_ref[...] = jnp.zeros_like(dg_ref)

        dg_ref[...] += dg

    row = pl.BlockSpec((bm, D), lambda i: (i, 0))
    vec = pl.BlockSpec((1, D), lambda i: (0, 0))
    return pl.pallas_call(
        body, grid=(T // bm,), in_specs=[row, row, vec, row], out_specs=[row, row, vec],
        out_shape=[jax.ShapeDtypeStruct((T, D), F32), jax.ShapeDtypeStruct((T, D), BF),
                   jax.ShapeDtypeStruct((1, D), F32)],
        name="rms_bwd", compiler_params=_cparams(1))(dn, h, gain, dh_up)


def _loss_head(h, gain, target):
    T, D = h.shape
    bm = min(ROW_TILE, T)

    def body(h_ref, g_ref, t_ref, dh_ref, dhb_ref, dg_ref, loss_ref):
        x = h_ref[...]
        g = g_ref[...]
        r = lax.rsqrt(jnp.mean(x * x, axis=-1, keepdims=True) + RMS_EPS)
        err = x * r * g - t_ref[...]
        part = 0.5 * jnp.sum(jnp.sum(err * err, axis=-1, keepdims=True), axis=0, keepdims=True) / D
        dx, dg = _rms_bwd_rows(err / D, x, g)
        dh_ref[...] = dx
        dhb_ref[...] = dx.astype(BF)

        @pl.when(pl.program_id(0) == 0)
        def _():
            dg_ref[...] = jnp.zeros_like(dg_ref)
            loss_ref[...] = jnp.zeros_like(loss_ref)

        dg_ref[...] += dg
        loss_ref[...] += jnp.broadcast_to(part, loss_ref.shape)

    row = pl.BlockSpec((bm, D), lambda i: (i, 0))
    vec = pl.BlockSpec((1, D), lambda i: (0, 0))
    return pl.pallas_call(
        body, grid=(T // bm,), in_specs=[row, vec, row],
        out_specs=[row, row, vec, pl.BlockSpec((SUBLANES, LANES), lambda i: (0, 0))],
        out_shape=[jax.ShapeDtypeStruct((T, D), F32), jax.ShapeDtypeStruct((T, D), BF),
                   jax.ShapeDtypeStruct((1, D), F32), jax.ShapeDtypeStruct((SUBLANES, LANES), F32)],
        name="loss_head", compiler_params=_cparams(1))(h, gain, target)


def _ffn_up(n, w1g, w3g, comm=None):
    T, D = n.shape
    fs = w1g.shape[-1]
    bm = min(FFN_ROW_TILE, T)
    wspec = pl.BlockSpec((None, D, fs), lambda s, i, k: (s, 0, 0))
    ospec = pl.BlockSpec((None, bm, fs), lambda s, i, k: (s, i, 0))

    def epi(accs, ex, outs):
        a1, a3 = accs
        sg = _sig(a1)
        silu = a1 * sg
        outs[0][...] = (a3 * sg * (1.0 + a1 * (1.0 - sg))).astype(BF)
        outs[1][...] = silu.astype(BF)
        outs[2][...] = (silu * a3).astype(BF)

    sh = jax.ShapeDtypeStruct((N_DEV, T, fs), BF)
    return _mm("ffn_up", (N_DEV, T // bm, 1),
               [(n, pl.BlockSpec((bm, D), lambda s, i, k: (i, 0))), (w1g, wspec), (w3g, wspec)],
               [(0, 1, NN, 0), (0, 2, NN, 1)], 2, None, [], [(sh, ospec)] * 3, epi, comm=comm,
               nrow=max(1, bm // ROW_TILE))


def _ffn_down(g, w2g, h, comm=None):
    _, T, fs = g.shape
    D = h.shape[1]
    bm = min(FFN_ROW_TILE, T)
    row = pl.BlockSpec((bm, D), lambda i, j, s: (i, 0))

    def epi(accs, ex, outs):
        outs[0][...] = ex[0][...] + 0.5 * accs[0]

    return _mm("ffn_down", (T // bm, 1, N_DEV),
               [(g, pl.BlockSpec((None, bm, fs), lambda i, j, s: (s, i, 0))),
                (w2g, pl.BlockSpec((None, fs, D), lambda i, j, s: (s, 0, 0)))],
               [(0, 1, NN, 0)], 1, (bm, D), [(h, row)],
               [(jax.ShapeDtypeStruct((T, D), F32), row)], epi, comm=comm,
               nrow=max(1, bm // ROW_TILE), ncol=max(1, D // COL_TILE))[0]


def _ffn_bwd_hidden(dhb, w2g, t1, t3, comm=None):
    T, D = dhb.shape
    fs = t1.shape[-1]
    bm = min(FFN_ROW_TILE, T)
    aspec = pl.BlockSpec((None, bm, fs), lambda s, i, k: (s, i, 0))

    def epi(accs, ex, outs):
        dg = 0.5 * accs[0]
        outs[0][...] = (dg * ex[0][...].astype(F32)).astype(BF)
        outs[1][...] = (dg * ex[1][...].astype(F32)).astype(BF)

    sh = jax.ShapeDtypeStruct((N_DEV, T, fs), BF)
    return _mm("ffn_bwd_hidden", (N_DEV, T // bm, 1),
               [(dhb, pl.BlockSpec((bm, D), lambda s, i, k: (i, 0))),
                (w2g, pl.BlockSpec((None, fs, D), lambda s, i, k: (s, 0, 0)))],
               [(0, 1, NT, 0)], 1, None, [(t1, aspec), (t3, aspec)], [(sh, aspec)] * 2, epi, comm=comm,
               nrow=max(1, bm // ROW_TILE))


def _ffn_dw2(g, dhb, comm=None):
    _, T, fs = g.shape
    D = dhb.shape[1]
    bn = min(COL_TILE, D)

    def epi(accs, ex, outs):
        outs[0][...] = (0.5 * accs[0]).astype(BF)

    return _mm("ffn_dw2", (N_DEV, D // bn, 1),
               [(g, pl.BlockSpec((None, T, fs), lambda s, j, k: (s, 0, 0))),
                (dhb, pl.BlockSpec((T, bn), lambda s, j, k: (0, j)))],
               [(0, 1, TN, 0)], 1, None, [],
               [(jax.ShapeDtypeStruct((N_DEV, fs, D), BF), pl.BlockSpec((None, fs, bn), lambda s, j, k: (s, 0, j)))],
               epi, comm=comm)[0]


def _ffn_dw13(n, da1, da3, comm=None):
    T, D = n.shape
    fs = da1.shape[-1]
    bn = min(COL_TILE, D)
    dspec = pl.BlockSpec((None, T, fs), lambda s, j, k: (s, 0, 0))
    ospec = pl.BlockSpec((None, fs, bn), lambda s, j, k: (s, 0, j))

    def epi(accs, ex, outs):
        outs[0][...] = accs[0].astype(BF)
        outs[1][...] = accs[1].astype(BF)

    sh = jax.ShapeDtypeStruct((N_DEV, fs, D), BF)
    return _mm("ffn_dw13", (N_DEV, D // bn, 1),
               [(da1, dspec), (da3, dspec), (n, pl.BlockSpec((T, bn), lambda s, j, k: (0, j)))],
               [(0, 2, TN, 0), (1, 2, TN, 1)], 2, None, [], [(sh, ospec)] * 2, epi, comm=comm)


def _ffn_dn_rms(da1, da3, w1g, w3g, h, gain, dh_up, comm=None):
    _, T, fs = da1.shape
    D = w1g.shape[-2]
    bm = min(ROW_TILE, T)
    rows_per_pass = min(64, bm)
    dspec = pl.BlockSpec((None, bm, fs), lambda i, j, s: (s, i, 0))
    wspec = pl.BlockSpec((None, D, fs), lambda i, j, s: (s, 0, 0))
    row = pl.BlockSpec((bm, D), lambda i, j, s: (i, 0))
    vec = pl.BlockSpec((1, D), lambda i, j, s: (0, 0))

    def epi(acc, ex, outs):
        h_ref, g_ref, up_ref = ex
        dh_ref, dhb_ref, dg_ref = outs

        @pl.when(pl.program_id(0) == 0)
        def _():
            dg_ref[...] = jnp.zeros_like(dg_ref)

        g = g_ref[...]
        dg = jnp.zeros((1, D), F32)
        for r in range(bm // rows_per_pass):
            rows = pl.ds(r * rows_per_pass, rows_per_pass)
            dx, dg_r = _rms_bwd_rows(acc[0][rows, :], h_ref[rows, :], g)
            dh = up_ref[rows, :] + dx
            dh_ref[rows, :] = dh
            dhb_ref[rows, :] = dh.astype(BF)
            dg = dg + dg_r
        dg_ref[...] += dg

    return _mm("ffn_dn_rms", (T // bm, 1, N_DEV),
               [(da1, dspec), (w1g, wspec), (da3, dspec), (w3g, wspec)],
               [(0, 1, NT, 0), (2, 3, NT, 0)], 1, (bm, D), [(h, row), (gain, vec), (dh_up, row)],
               [(jax.ShapeDtypeStruct((T, D), F32), row), (jax.ShapeDtypeStruct((T, D), BF), row),
                (jax.ShapeDtypeStruct((1, D), F32), vec)],
               epi, comm=comm, ncol=max(1, D // COL_TILE), whole_tile_epilogue=True)


def _rope_tables(seq):
    half = LANES // 2
    inv = ROPE_THETA ** (-jnp.arange(0, half, dtype=F32) * 2.0 / LANES)
    ang = jnp.arange(seq, dtype=F32)[:, None] * inv[None, :]
    cos, sin = jnp.cos(ang), jnp.sin(ang)
    return jnp.concatenate([cos, cos], axis=1), jnp.concatenate([-sin, sin], axis=1)


def _branch_bias(nq, bq):
    d = (jnp.arange(nq)[:, None, None] * bq + jnp.arange(bq)[None, :, None]
         - jnp.arange(bq)[None, None, :])
    mult = jnp.zeros(d.shape, F32)
    for window, dil in DILATED_PATTERN:
        mult = mult + ((d >= 0) & (d % dil == 0) & (d <= window)).astype(F32)
    return jnp.where(mult > 0, jnp.log(jnp.maximum(mult, 1.0)), NEG_INF)


def _proj_fwd(u, wing, comm=None):
    T, D = u.shape
    ws = wing.shape[-1]
    bm = min(FFN_ROW_TILE, T)

    def epi(accs, ex, outs):
        outs[0][...] = accs[0]

    return _mm("proj_fwd", (N_DEV, T // bm, 1),
               [(u, pl.BlockSpec((bm, D), lambda s, i, k: (i, 0))),
                (wing, pl.BlockSpec((None, D, ws), lambda s, i, k: (s, 0, 0)))],
               [(0, 1, NN, 0)], 1, None, [],
               [(jax.ShapeDtypeStruct((T, N_DEV * ws), F32),
                 pl.BlockSpec((bm, ws), lambda s, i, k: (i, s)))], epi, comm=comm,
               nrow=max(1, bm // ROW_TILE))[0]


def _rope_fwd(proj, cosf, sinf, seq, nh):
    T = proj.shape[0]
    bs = min(ROW_TILE, seq)
    nst = seq // bs
    scale = LANES ** -0.5

    def body(x_ref, c_ref, s_ref, o_ref):
        j = pl.program_id(1)
        c = c_ref[...]
        s = s_ref[...]
        mul = jnp.where(j == 0, scale, 1.0)
        for h in range(nh):
            cols = slice(h * LANES, (h + 1) * LANES)
            t = x_ref[:, cols]
            rot = (t * c + pltpu.roll(t, LANES // 2, 1) * s) * mul
            o_ref[:, cols] = jnp.where(j < 2, rot, t).astype(BF)

    blk = pl.BlockSpec((bs, nh * LANES), lambda r, j: (r, j))
    tab = pl.BlockSpec((bs, LANES), lambda r, j: (r % nst, 0))
    return pl.pallas_call(
        body, grid=(T // bs, 3), in_specs=[blk, tab, tab], out_specs=blk,
        out_shape=jax.ShapeDtypeStruct((T, 3 * nh * LANES), BF), name="rope_fwd",
        compiler_params=_cparams(2))(proj, cosf, sinf)


def _attn_fwd(qkv, bias, nb, seq, nh, comm=None):
    T = nb * seq
    bq = bias.shape[1]
    nq = seq // bq

    def body(q_ref, k_ref, v_ref, b_ref, o_ref, lse_ref):
        qi = pl.program_id(2)
        q = q_ref[...]

        def step(kj, carry):
            m, l, acc = carry
            rows = pl.ds(pl.multiple_of(kj * bq, bq), bq)
            s = _dot(q, k_ref[rows, :], NT) + b_ref[qi - kj]
            m_new = jnp.maximum(m, jnp.max(s, axis=1, keepdims=True))
            p = jnp.exp(s - m_new)
            alpha = jnp.exp(m - m_new)
            l = alpha * l + jnp.sum(p, axis=1, keepdims=True)
            acc = alpha * acc + _dot(p, v_ref[rows, :], NN)
            return m_new, l, acc

        init = (jnp.full((bq, 1), NEG_INF, F32), jnp.zeros((bq, 1), F32), jnp.zeros((bq, LANES), F32))
        m, l, acc = lax.fori_loop(0, qi + 1, step, init)
        o_ref[...] = (acc / l).astype(BF)
        lse_ref[...] = m + jnp.log(l)

    return _call(
        body, "attn_fwd", (nb, nh, nq),
        [pl.BlockSpec((bq, LANES), lambda b, h, i: (b * nq + i, h)),
         pl.BlockSpec((seq, LANES), lambda b, h, i: (b, nh + h)),
         pl.BlockSpec((seq, LANES), lambda b, h, i: (b, 2 * nh + h)),
         pl.BlockSpec((nq, bq, bq), lambda b, h, i: (0, 0, 0))],
        [pl.BlockSpec((bq, LANES), lambda b, h, i: (b * nq + i, h)),
         pl.BlockSpec((None, bq, 1), lambda b, h, i: (h, b * nq + i, 0))],
        [jax.ShapeDtypeStruct((T, 2 * nh * LANES), BF), jax.ShapeDtypeStruct((nh, T, 1), F32)],
        (qkv, qkv, qkv, bias), comm=comm)


def _attn_bwd(qkv, cat, dcat, lse, bias, nb, seq, nh, comm=None):
    T = nb * seq
    bq = bias.shape[1]
    nq = seq // bq

    def body(k_ref, v_ref, q_ref, o_ref, do_ref, lse_ref, b_ref, dq_ref, dk_ref, dv_ref):
        kj = pl.program_id(2)
        k = k_ref[...]
        v = v_ref[...]

        @pl.when(kj == 0)
        def _():
            dq_ref[...] = jnp.zeros_like(dq_ref)

        def step(qi, carry):
            dk, dv = carry
            rows = pl.ds(pl.multiple_of(qi * bq, bq), bq)
            q = q_ref[rows, :]
            do = do_ref[rows, :]
            dob = do.astype(BF)
            delta = jnp.sum(do * o_ref[rows, :].astype(F32), axis=1, keepdims=True)
            p = jnp.exp(_dot(q, k, NT) + b_ref[qi - kj] - lse_ref[rows, :])
            dv = dv + _dot(p, dob, TN)
            ds = p * (_dot(dob, v, NT) - delta)
            dq_ref[rows, :] += _dot(ds, k, NN)
            return dk + _dot(ds, q, TN), dv

        z = jnp.zeros((bq, LANES), F32)
        dk, dv = lax.fori_loop(kj, nq, step, (z, z))
        dk_ref[...] = dk
        dv_ref[...] = dv

    whole = pl.BlockSpec((seq, LANES), lambda b, h, i: (b, h))
    tile = pl.BlockSpec((bq, LANES), lambda b, h, i: (b * nq + i, h))
    sh = jax.ShapeDtypeStruct((T, nh * LANES), F32)
    return _call(
        body, "attn_bwd", (nb, nh, nq),
        [pl.BlockSpec((bq, LANES), lambda b, h, i: (b * nq + i, nh + h)),
         pl.BlockSpec((bq, LANES), lambda b, h, i: (b * nq + i, 2 * nh + h)),
         whole, whole, whole, pl.BlockSpec((None, seq, 1), lambda b, h, i: (h, b, 0)),
         pl.BlockSpec((nq, bq, bq), lambda b, h, i: (0, 0, 0))],
        [whole, tile, tile], [sh, sh, sh],
        (qkv, qkv, qkv, cat, dcat, lse, bias), comm=comm)


def _conv_parts(gc, xin, w_ref):
    w = [w_ref[k:k + 1, :] for k in range(3)]
    u = gc * xin
    row = lax.broadcasted_iota(jnp.int32, u.shape, 0)
    u1 = jnp.where(row >= 1, pltpu.roll(u, 1, 0), 0.0)
    u2 = jnp.where(row >= 2, pltpu.roll(u, 2, 0), 0.0)
    return u, u1, u2, w[0] * u2 + w[1] * u1 + w[2] * u, w, row


def _conv_fwd(proj, conv_w, cat, nb, seq, width):
    cw = min(2 * LANES, width)
    nc = width // cw

    def body(gb_ref, gc_ref, x_ref, w_ref, cat_ref, o_ref):
        _, _, _, conv, _, _ = _conv_parts(gc_ref[...], x_ref[...], w_ref)
        o_ref[...] = (gb_ref[...] * conv).astype(BF)

    def sec(k):
        return pl.BlockSpec((seq, cw), lambda b, c: (b, k * nc + c))

    return pl.pallas_call(
        body, grid=(nb, nc),
        in_specs=[sec(3), sec(4), sec(5), pl.BlockSpec((3, cw), lambda b, c: (0, c)),
                  pl.BlockSpec(memory_space=pl.ANY)],
        out_specs=pl.BlockSpec((seq, cw), lambda b, c: (b, nc + c)),
        out_shape=jax.ShapeDtypeStruct(cat.shape, BF), input_output_aliases={4: 0},
        name="conv_fwd", compiler_params=_cparams(2))(proj, proj, proj, conv_w, cat)


def _conv_bwd(proj, conv_w, dcat, nb, seq, width):
    cw = min(2 * LANES, width)
    nc = width // cw
    T = nb * seq

    def body(gb_ref, gc_ref, x_ref, w_ref, d_ref, dgb_ref, dgc_ref, dx_ref, dw_ref):
        gc = gc_ref[...]
        xin = x_ref[...]
        u, u1, u2, conv, w, row = _conv_parts(gc, xin, w_ref)
        dsc = d_ref[...]
        dgb_ref[...] = dsc * conv
        dconv = dsc * gb_ref[...]
        d1 = jnp.where(row < seq - 1, pltpu.roll(dconv, seq - 1, 0), 0.0)
        d2 = jnp.where(row < seq - 2, pltpu.roll(dconv, seq - 2, 0), 0.0)
        du = w[2] * dconv + w[1] * d1 + w[0] * d2
        dgc_ref[...] = du * xin
        dx_ref[...] = du * gc

        @pl.when(pl.program_id(1) == 0)
        def _():
            dw_ref[...] = jnp.zeros_like(dw_ref)

        dw_ref[0:1, :] += jnp.sum(dconv * u2, axis=0, keepdims=True)
        dw_ref[1:2, :] += jnp.sum(dconv * u1, axis=0, keepdims=True)
        dw_ref[2:3, :] += jnp.sum(dconv * u, axis=0, keepdims=True)

    def sec(k):
        return pl.BlockSpec((seq, cw), lambda c, b: (b, k * nc + c))

    out = pl.BlockSpec((seq, cw), lambda c, b: (b, c))
    wsp = pl.BlockSpec((3, cw), lambda c, b: (0, c))
    sh = jax.ShapeDtypeStruct((T, width), F32)
    return pl.pallas_call(
        body, grid=(nc, nb), in_specs=[sec(3), sec(4), sec(5), wsp, sec(1)],
        out_specs=[out, out, out, wsp], out_shape=[sh, sh, sh, jax.ShapeDtypeStruct((3, width), F32)],
        name="conv_bwd", compiler_params=_cparams(2))(proj, proj, proj, conv_w, dcat)


def _assemble_dproj(dq, dk, dv, dgb, dgc, dxin, cosf, sinf, seq):
    T, width = dq.shape
    nh = width // LANES
    bs = min(ROW_TILE, seq)
    nst = seq // bs
    scale = LANES ** -0.5

    def body(dq_ref, dk_ref, dv_ref, dgb_ref, dgc_ref, dx_ref, c_ref, s_ref, o_ref):
        sec = pl.program_id(1)
        c = c_ref[...]
        s = s_ref[...]

        def unrope(ref, mul):
            for h in range(nh):
                cols = slice(h * LANES, (h + 1) * LANES)
                t = ref[:, cols]
                o_ref[:, cols] = ((t * c + pltpu.roll(t * s, LANES // 2, 1)) * mul).astype(BF)

        @pl.when(sec == 0)
        def _():
            unrope(dq_ref, scale)

        @pl.when(sec == 1)
        def _():
            unrope(dk_ref, 1.0)

        for k, ref in ((2, dv_ref), (3, dgb_ref), (4, dgc_ref), (5, dx_ref)):
            @pl.when(sec == k)
            def _(ref=ref):
                o_ref[...] = ref[...].astype(BF)

    blk = pl.BlockSpec((bs, width), lambda r, k: (r, 0))
    tab = pl.BlockSpec((bs, LANES), lambda r, k: (r % nst, 0))
    return pl.pallas_call(
        body, grid=(T // bs, 6), in_specs=[blk] * 6 + [tab, tab],
        out_specs=pl.BlockSpec((bs, width), lambda r, k: (r, k)),
        out_shape=jax.ShapeDtypeStruct((T, 6 * width), BF), name="assemble_dproj",
        compiler_params=_cparams(2))(dq, dk, dv, dgb, dgc, dxin, cosf, sinf)


def _res_mm(name, a, w, h, comm=None):
    T, K = a.shape
    N = w.shape[1]
    bm = min(FFN_ROW_TILE, T)
    bk = min(ROW_TILE, K)
    row = pl.BlockSpec((bm, N), lambda i, j, k: (i, 0))

    def epi(accs, ex, outs):
        outs[0][...] = ex[0][...] + accs[0]

    return _mm(name, (T // bm, 1, K // bk),
               [(a, pl.BlockSpec((bm, bk), lambda i, j, k: (i, k))),
                (w, pl.BlockSpec((bk, N), lambda i, j, k: (k, 0)))],
               [(0, 1, NN, 0)], 1, (bm, N), [(h, row)],
               [(jax.ShapeDtypeStruct((T, N), F32), row)], epi, comm=comm,
               nrow=max(1, bm // ROW_TILE), ncol=max(1, N // COL_TILE))[0]


def _mm_nt(name, a, w, out_dtype):
    T, K = a.shape
    N = w.shape[0]
    bm = min(FFN_ROW_TILE, T)
    bn = min(ROW_TILE, N)

    def epi(accs, ex, outs):
        outs[0][...] = accs[0].astype(out_dtype)

    return _mm(name, (T // bm, N // bn, 1),
               [(a, pl.BlockSpec((bm, K), lambda i, j, k: (i, 0))),
                (w, pl.BlockSpec((bn, K), lambda i, j, k: (j, 0)))],
               [(0, 1, NT, 0)], 1, None, [],
               [(jax.ShapeDtypeStruct((T, N), out_dtype), pl.BlockSpec((bm, bn), lambda i, j, k: (i, j)))],
               epi, nrow=max(1, bm // ROW_TILE))[0]


def _mm_tn(name, a, bs_list):
    T, M = a.shape
    N = bs_list[0].shape[1]
    bmr = min(COL_TILE, M)
    bn = min(COL_TILE, N)
    n = len(bs_list)

    def epi(accs, ex, outs):
        for q in range(n):
            outs[q][...] = accs[q].astype(BF)

    ops = [(a, pl.BlockSpec((T, bmr), lambda r, j, k: (0, r)))]
    ops += [(b, pl.BlockSpec((T, bn), lambda r, j, k: (0, j))) for b in bs_list]
    return _mm(name, (M // bmr, N // bn, 1), ops, [(0, 1 + q, TN, q) for q in range(n)], n, None, [],
               [(jax.ShapeDtypeStruct((M, N), BF), pl.BlockSpec((bmr, bn), lambda r, j, k: (r, j)))] * n, epi)


def _proj_bwd_x(dproj, wing):
    T = dproj.shape[0]
    _, D, ws = wing.shape
    bm = min(FFN_ROW_TILE, T)
    row = pl.BlockSpec((bm, D), lambda i, j, s: (i, 0))

    def epi(accs, ex, outs):
        outs[0][...] = accs[0]

    return _mm("proj_bwd_x", (T // bm, 1, N_DEV),
               [(dproj, pl.BlockSpec((bm, ws), lambda i, j, s: (i, s))),
                (wing, pl.BlockSpec((None, D, ws), lambda i, j, s: (s, 0, 0)))],
               [(0, 1, NT, 0)], 1, (bm, D), [], [(jax.ShapeDtypeStruct((T, D), F32), row)], epi,
               nrow=max(1, bm // ROW_TILE), ncol=max(1, D // COL_TILE))[0]


def _proj_dw(u, dproj, ws):
    T, D = u.shape
    bmr = min(COL_TILE, D)

    def epi(accs, ex, outs):
        outs[0][...] = accs[0].astype(BF)

    return _mm("proj_dw", (N_DEV, D // bmr, 1),
               [(u, pl.BlockSpec((T, bmr), lambda s, r, k: (0, r))),
                (dproj, pl.BlockSpec((T, ws), lambda s, r, k: (0, s)))],
               [(0, 1, TN, 0)], 1, None, [],
               [(jax.ShapeDtypeStruct((N_DEV, D, ws), BF),
                 pl.BlockSpec((None, bmr, ws), lambda s, r, k: (s, r, 0)))], epi)[0]


def _mixer_ab_fwd(h, gain, wing, conv_w, wout, tabs, nb, seq, comm_proj=None, comm_attn=None, comm_out=None):
    cosf, sinf, bias = tabs
    width = wing.shape[-1] * N_DEV // 6
    nh = width // LANES
    u = _rms_fwd(h, gain, BF)
    proj = _proj_fwd(u, wing, comm=comm_proj)
    qkv = _rope_fwd(proj, cosf, sinf, seq, nh)
    cat, lse = _attn_fwd(qkv, bias, nb, seq, nh, comm=comm_attn)
    cat = _conv_fwd(proj, conv_w, cat, nb, seq, width)
    return _res_mm("outproj_fwd", cat, wout, h, comm=comm_out), (h, u, proj, qkv, cat, lse)


def _mixer_ab_bwd(dh, dhb, saved, gain, wing, conv_w, wout, tabs, nb, seq, reduce_start, carry):
    cosf, sinf, bias = tabs
    h, u, proj, qkv, cat, lse = saved
    D = h.shape[1]
    ws = wing.shape[-1]
    width = ws * N_DEV // 6
    nh = width // LANES
    dcat = _mm_nt("outproj_bwd_x", dhb, wout, F32)
    dwout = _mm_tn("outproj_dw", cat, [dhb])[0]
    comm = _merge_comms(reduce_start(["ab_w_out"], [dwout.reshape(N_DEV, -1, D)]) + [carry])
    dq, dk, dv = _attn_bwd(qkv, cat, dcat, lse, bias, nb, seq, nh, comm=comm)
    dgb, dgc, dxin, dconvw = _conv_bwd(proj, conv_w, dcat, nb, seq, width)
    dproj = _assemble_dproj(dq, dk, dv, dgb, dgc, dxin, cosf, sinf, seq)
    du = _proj_bwd_x(dproj, wing)
    comm, = reduce_start(["ab_w_in"], [_proj_dw(u, dproj, ws)])
    dh_in, dhb_in, dgain = _rms_bwd(du, h, gain, dh)
    return dh_in, dhb_in, dgain, dconvw, comm


def _s5_zoh(lr, li, log_dt):
    dt = jnp.exp(log_dt)
    mag = jnp.exp(lr * dt)
    ar = mag * jnp.cos(li * dt)
    ai = mag * jnp.sin(li * dt)
    den = lr * lr + li * li
    return dt, ar, ai, den, ((ar - 1.0) * lr + ai * li) / den, (ai * lr - (ar - 1.0) * li) / den


def _s5_discretize(lam_re, lam_im, log_dt, bt_re, bt_im):
    def body(lr_ref, li_ref, ld_ref, br_ref, bi_ref, ar_ref, ai_ref, bbr_ref, bbi_ref):
        _, ar, ai, _, fr, fi = _s5_zoh(lr_ref[...], li_ref[...], ld_ref[...])
        ar_ref[...] = ar
        ai_ref[...] = ai
        bbr_ref[...] = fr * br_ref[...] - fi * bi_ref[...]
        bbi_ref[...] = fr * bi_ref[...] + fi * br_ref[...]

    small = jax.ShapeDtypeStruct(lam_re.shape, F32)
    big = jax.ShapeDtypeStruct(bt_re.shape, F32)
    return pl.pallas_call(body, out_shape=[small, small, big, big], name="s5_discretize",
                          compiler_params=_cparams(0))(lam_re, lam_im, log_dt, bt_re, bt_im)


def _s5_discretize_bwd(lam_re, lam_im, log_dt, bt_re, bt_im, d_ar, d_ai, d_bbr, d_bbi):

    def body(lr_ref, li_ref, ld_ref, br_ref, bi_ref, dar_ref, dai_ref, dbbr_ref, dbbi_ref,
             dlr_ref, dli_ref, dld_ref, dbr_ref, dbi_ref):
        lr, li = lr_ref[...], li_ref[...]
        dt, ar, ai, den, fr, fi = _s5_zoh(lr, li, ld_ref[...])
        br, bi = br_ref[...], bi_ref[...]
        dbbr, dbbi = dbbr_ref[...], dbbi_ref[...]
        dbr_ref[...] = dbbr * fr + dbbi * fi
        dbi_ref[...] = dbbi * fr - dbbr * fi
        dfr = jnp.sum(dbbr * br + dbbi * bi, axis=1, keepdims=True)
        dfi = jnp.sum(dbbi * br - dbbr * bi, axis=1, keepdims=True)
        dnr = dfr / den
        dni = dfi / den
        dden = -(dfr * fr + dfi * fi) / den
        dar = dar_ref[...] + dnr * lr - dni * li
        dai = dai_ref[...] + dnr * li + dni * lr
        dlr_ref[...] = dnr * (ar - 1.0) + dni * ai + 2.0 * dden * lr + dt * (dar * ar + dai * ai)
        dli_ref[...] = dnr * ai - dni * (ar - 1.0) + 2.0 * dden * li + dt * (dai * ar - dar * ai)
        ddt = jnp.sum(dar * (lr * ar - li * ai) + dai * (lr * ai + li * ar), axis=2, keepdims=True)
        dld_ref[...] = ddt * dt

    small = jax.ShapeDtypeStruct(lam_re.shape, F32)
    big = jax.ShapeDtypeStruct(bt_re.shape, F32)
    return pl.pallas_call(
        body, out_shape=[small, small, jax.ShapeDtypeStruct(log_dt.shape, F32), big, big],
        name="s5_discretize_bwd", compiler_params=_cparams(0))(
            lam_re, lam_im, log_dt, bt_re, bt_im, d_ar, d_ai, d_bbr, d_bbi)


def _rows8(t):
    if isinstance(t, int):
        return pl.ds(t * SUBLANES, SUBLANES)
    return pl.ds(pl.multiple_of(t * SUBLANES, SUBLANES), SUBLANES)


def _cmul_add(ar, ai, sr, si, br, bi):
    return ar * sr - ai * si + br, ar * si + ai * sr + bi


def _steps(n, step, carry):
    head = n % SCAN_UNROLL
    for i in range(head):
        carry = step(i, carry)

    def trip(j, c):
        for q in range(SCAN_UNROLL):
            c = step(head + j * SCAN_UNROLL + q, c)
        return c

    return lax.fori_loop(0, n // SCAN_UNROLL, trip, carry)


def _scan(a, read, write, init, n):
    def step(t, c):
        s = _cmul_add(*a, *c, *read(t))
        if write is not None:
            write(t, s)
        return s

    return _steps(n, step, init)


def _cpow(ar, ai, n):
    rr = ri = None
    while n:
        if n & 1:
            rr, ri = (ar, ai) if rr is None else (rr * ar - ri * ai, rr * ai + ri * ar)
        ar, ai = ar * ar - ai * ai, 2.0 * ar * ai
        n >>= 1
    return rr, ri


def _s5_specs(R, nj):
    sh = STATE_COLS
    return dict(
        rows=pl.BlockSpec((R, LANES), lambda j: (0, j)),
        bd=pl.BlockSpec((None, LANES, sh), lambda j: (j, 0, 0)),
        cd=pl.BlockSpec((None, sh, LANES), lambda j: (j, 0, 0)),
        a=pl.BlockSpec((None, 1, sh), lambda j: (j, 0, 0)),
        vec=pl.BlockSpec((1, LANES), lambda j: (0, j)),
        init=pl.BlockSpec((None, SUBLANES, sh), lambda j: (j, 0, 0)))


def _s5_fwd(u, mats, seg_len, nseg, comm=None):
    bdr, bdi, cdr, cdi, are, aim, dsk = mats
    R, D = u.shape
    nj = D // LANES
    sh = STATE_COLS
    rc = min(R, 512)
    sp = _s5_specs(R, nj)

    def body(u_ref, bdr_ref, bdi_ref, cdr_ref, cdi_ref, ar_ref, ai_ref, d_ref,
             y_ref, yg_ref, ir_ref, ii_ref, sre, sim):
        ar = jnp.broadcast_to(ar_ref[...], (SUBLANES, sh))
        ai = jnp.broadcast_to(ai_ref[...], (SUBLANES, sh))

        def bu_chunk(c, _):
            rows = pl.ds(pl.multiple_of(c * rc, rc), rc)
            ub = u_ref[rows, :].astype(BF)
            sre[rows, :] = _dot(ub, bdr_ref[...], NN)
            sim[rows, :] = _dot(ub, bdi_ref[...], NN)
            return 0

        lax.fori_loop(0, R // rc, bu_chunk, 0)
        z = jnp.zeros((SUBLANES, sh), F32)

        def read(t):
            return sre[_rows8(t), :], sim[_rows8(t), :]

        def write(t, s):
            sre[_rows8(t), :] = s[0]
            sim[_rows8(t), :] = s[1]

        er, ei = _scan((ar, ai), read, None, (z, z), seg_len)
        pr, pi = _cpow(ar, ai, seg_len)
        first = (lax.broadcasted_iota(jnp.int32, (SUBLANES, sh), 0) & (nseg - 1)) == 0

        def prev(x):
            return jnp.where(first, 0.0, pltpu.roll(x, 1, 0))

        xr, xi = er, ei
        for _ in range(nseg - 1):
            xr, xi = _cmul_add(pr, pi, prev(xr), prev(xi), er, ei)
        i_r, i_i = prev(xr), prev(xi)
        ir_ref[...] = i_r
        ii_ref[...] = i_i
        _scan((ar, ai), read, write, (i_r, i_i), seg_len)

        def y_chunk(c, _):
            rows = pl.ds(pl.multiple_of(c * rc, rc), rc)
            y = _dot(sre[rows, :], cdr_ref[...], NN) + _dot(sim[rows, :], cdi_ref[...], NN)
            y = y + d_ref[...] * u_ref[rows, :]
            y_ref[rows, :] = y
            yg_ref[rows, :] = _gelu(y).astype(BF)
            return 0

        lax.fori_loop(0, R // rc, y_chunk, 0)

    init_sh = jax.ShapeDtypeStruct((nj, SUBLANES, STATE_COLS), F32)
    return _call(
        body, "s5_fwd", (nj,),
        [sp["rows"], sp["bd"], sp["bd"], sp["cd"], sp["cd"], sp["a"], sp["a"], sp["vec"]],
        [sp["rows"], sp["rows"], sp["init"], sp["init"]],
        [jax.ShapeDtypeStruct((R, D), F32), jax.ShapeDtypeStruct((R, D), BF), init_sh, init_sh],
        (u, bdr, bdi, cdr, cdi, are, aim, dsk),
        scratch=[pltpu.VMEM((R, sh), F32) for _ in range(2)], comm=comm)


def _s5_bwd(u, dy, mats, init_re, init_im, seg_len, nseg, comm=None):
    bdr, bdi, cdr, cdi, are, aim, dsk = mats
    R, D = u.shape
    nj = D // LANES
    sh = STATE_COLS
    rc = min(R, 512)
    sp = _s5_specs(R, nj)

    def body(u_ref, dy_ref, bdr_ref, bdi_ref, cdr_ref, cdi_ref, ar_ref, ai_ref, d_ref, ir_ref, ii_ref,
             du_ref, dbdr_ref, dbdi_ref, dcdr_ref, dcdi_ref, dar_ref, dai_ref, dd_ref,
             sre, sim, gre, gim):
        ar = jnp.broadcast_to(ar_ref[...], (SUBLANES, sh))
        ai = jnp.broadcast_to(ai_ref[...], (SUBLANES, sh))
        i_r, i_i = ir_ref[...], ii_ref[...]

        def chunk(c):
            return pl.ds(pl.multiple_of(c * rc, rc), rc)

        def bu_chunk(c, _):
            ub = u_ref[chunk(c), :].astype(BF)
            dyb = dy_ref[chunk(c), :].astype(BF)
            sre[chunk(c), :] = _dot(ub, bdr_ref[...], NN)
            sim[chunk(c), :] = _dot(ub, bdi_ref[...], NN)
            gre[chunk(c), :] = _dot(dyb, cdr_ref[...], NT)
            gim[chunk(c), :] = _dot(dyb, cdi_ref[...], NT)
            return 0

        lax.fori_loop(0, R // rc, bu_chunk, 0)

        def read_s(t):
            return sre[_rows8(t), :], sim[_rows8(t), :]

        def read_g(t):
            return gre[_rows8(t), :], gim[_rows8(t), :]

        def both(i, c):
            s = _cmul_add(ar, ai, c[0], c[1], *read_s(i))
            sre[_rows8(i), :], sim[_rows8(i), :] = s
            return (*s, *_cmul_add(ar, -ai, c[2], c[3], *read_g(seg_len - 1 - i)))

        z = jnp.zeros((SUBLANES, sh), F32)
        _, _, fr, fi = _steps(seg_len, both, (i_r, i_i, z, z))

        def c_chunk(c, carry):
            dyb = dy_ref[chunk(c), :].astype(BF)
            return (carry[0] + _dot(sre[chunk(c), :], dyb, TN), carry[1] + _dot(sim[chunk(c), :], dyb, TN))

        zc = jnp.zeros((sh, LANES), F32)
        dcr, dci = lax.fori_loop(0, R // rc, c_chunk, (zc, zc))
        dcdr_ref[...] = dcr
        dcdi_ref[...] = dci
        pr, pi = _cpow(ar, ai, seg_len)
        last =(lax.broadcasted_iota(jnp.int32, (SUBLANES, sh), 0) & (nseg - 1)) == nseg - 1

        def nxt(x):
            return jnp.where(last, 0.0, pltpu.roll(x, SUBLANES - 1, 0))

        xr, xi = fr, fi
        for _ in range(nseg - 1):
            xr, xi = _cmul_add(pr, -pi, nxt(xr), nxt(xi), fr, fi)
        g0r, g0i = nxt(xr), nxt(xi)

        def adj_step(t, c, s_before):
            gr, gi = _cmul_add(ar, -ai, c[0], c[1], *read_g(t))
            gre[_rows8(t), :], gim[_rows8(t), :] = gr, gi
            spr, spi = s_before
            return gr, gi, c[2] + spr * gr + spi * gi, c[3] + spr * gi - spi * gr

        carry = _steps(seg_len - 1, lambda i, c: adj_step(seg_len - 1 - i, c, read_s(seg_len - 2 - i)),
                       (g0r, g0i, z, z))
        carry = adj_step(0, carry, (i_r, i_i))
        dar_ref[...] = jnp.sum(carry[2], axis=0, keepdims=True)
        dai_ref[...] = jnp.sum(carry[3], axis=0, keepdims=True)

        def d_chunk(c, carry):
            ub = u_ref[chunk(c), :].astype(BF)
            grb = gre[chunk(c), :].astype(BF)
            gib = gim[chunk(c), :].astype(BF)
            du = _dot(grb, bdr_ref[...], NT) + _dot(gib, bdi_ref[...], NT)
            du_ref[chunk(c), :] = du + d_ref[...] * dy_ref[chunk(c), :]
            dd = carry[2] + jnp.sum(dy_ref[chunk(c), :] * u_ref[chunk(c), :], axis=0, keepdims=True)
            return carry[0] + _dot(ub, grb, TN), carry[1] + _dot(ub, gib, TN), dd

        zb = jnp.zeros((LANES, sh), F32)
        dbr, dbi, dd = lax.fori_loop(0, R // rc, d_chunk, (zb, zb, jnp.zeros((1, LANES), F32)))
        dbdr_ref[...] = dbr
        dbdi_ref[...] = dbi
        dd_ref[...] = dd

    bd_sh = jax.ShapeDtypeStruct((nj, LANES, STATE_COLS), F32)
    cd_sh = jax.ShapeDtypeStruct((nj, STATE_COLS, LANES), F32)
    a_sh = jax.ShapeDtypeStruct((nj, 1, STATE_COLS), F32)
    return _call(
        body, "s5_bwd", (nj,),
        [sp["rows"], sp["rows"], sp["bd"], sp["bd"], sp["cd"], sp["cd"], sp["a"], sp["a"],
         sp["vec"], sp["init"], sp["init"]],
        [sp["rows"], sp["bd"], sp["bd"], sp["cd"], sp["cd"], sp["a"], sp["a"], sp["vec"]],
        [jax.ShapeDtypeStruct((R, D), F32), bd_sh, bd_sh, cd_sh, cd_sh, a_sh, a_sh,
         jax.ShapeDtypeStruct((1, D), F32)],
        (u, dy, bdr, bdi, cdr, cdi, are, aim, dsk, init_re, init_im),
        scratch=[pltpu.VMEM((R, sh), F32) for _ in range(4)], comm=comm)


def _glu_fwd(yg, wa, wb, h):
    T, D = yg.shape
    N = wa.shape[1]
    bm = min(FFN_ROW_TILE, T)
    bn = min(ROW_TILE, N)
    wspec = pl.BlockSpec((D, bn), lambda i, j, k: (0, j))
    ospec = pl.BlockSpec((bm, bn), lambda i, j, k: (i, j))

    def epi(accs, ex, outs):
        pa, pb = accs
        outs[0][...] = ex[0][...] + pa * _sig(pb)
        outs[1][...] = pa.astype(BF)
        outs[2][...] = pb.astype(BF)

    return _mm("glu_fwd", (T // bm, N // bn, 1),
               [(yg, pl.BlockSpec((bm, D), lambda i, j, k: (i, 0))), (wa, wspec), (wb, wspec)],
               [(0, 1, NN, 0), (0, 2, NN, 1)], 2, None, [(h, ospec)],
               [(jax.ShapeDtypeStruct((T, N), F32), ospec), (jax.ShapeDtypeStruct((T, N), BF), ospec),
                (jax.ShapeDtypeStruct((T, N), BF), ospec)], epi, nrow=max(1, bm // ROW_TILE))


def _glu_bwd_gates(dz, pa, pb):
    T, D = dz.shape
    bm = min(ROW_TILE, T)

    def body(dz_ref, pa_ref, pb_ref, dpa_ref, dpb_ref):
        dz = dz_ref[...]
        sg = _sig(pb_ref[...].astype(F32))
        dpa_ref[...] = (dz * sg).astype(BF)
        dpb_ref[...] = (dz * pa_ref[...].astype(F32) * sg * (1.0 - sg)).astype(BF)

    row = pl.BlockSpec((bm, D), lambda i: (i, 0))
    return pl.pallas_call(
        body, grid=(T // bm,), in_specs=[row] * 3, out_specs=[row] * 2,
        out_shape=[jax.ShapeDtypeStruct((T, D), BF)] * 2, name="glu_bwd_gates",
        compiler_params=_cparams(1))(dz, pa, pb)


def _glu_bwd_y(dpa, dpb, wa, wb, y_pre, comm=None):
    T, N = dpa.shape
    D = wa.shape[0]
    bm = min(FFN_ROW_TILE, T)
    bn = min(ROW_TILE, D)
    aspec = pl.BlockSpec((bm, N), lambda i, j, k: (i, 0))
    wspec = pl.BlockSpec((bn, N), lambda i, j, k: (j, 0))
    ospec = pl.BlockSpec((bm, bn), lambda i, j, k: (i, j))

    def epi(accs, ex, outs):
        outs[0][...] = accs[0] * _gelu_grad(ex[0][...])

    return _mm("glu_bwd_y", (T // bm, D // bn, 1), [(dpa, aspec), (wa, wspec), (dpb, aspec), (wb, wspec)],
               [(0, 1, NT, 0), (2, 3, NT, 0)], 1, None, [(y_pre, ospec)],
               [(jax.ShapeDtypeStruct((T, D), F32), ospec)], epi, comm=comm, nrow=max(1, bm // ROW_TILE))[0]


def _block_diag_in(x, nj):
    g = GROUPS_PER_BLOCK
    x = x.reshape(nj, g, 1, S5_GROUP, S5_STATE)
    eye = jnp.eye(g, dtype=bool)[None, :, :, None, None]
    full = jnp.where(eye, x, 0.0)
    return full.transpose(0, 1, 3, 2, 4).reshape(nj, g * S5_GROUP, g * S5_STATE)


def _block_diag_out(x, nj):
    return _block_diag_in(x, nj).transpose(0, 2, 1)


def _diag_of_in(m, nj):
    g = GROUPS_PER_BLOCK
    m5 = m.reshape(nj, g, S5_GROUP, g, S5_STATE)
    d = jnp.diagonal(m5, axis1=1, axis2=3)
    return d.transpose(0, 3, 1, 2).reshape(nj * g, S5_GROUP, S5_STATE)


def _mixer_s5_fwd(h, gain, p, dsk, wa, wb, nb, seq, comm_s5=None):
    T, D = h.shape
    nj = D // LANES
    nseg = SUBLANES // nb
    seg_len = seq // nseg
    G = p["s5_lambda_re"].shape[1]
    lam_re = p["s5_lambda_re"].reshape(G, 1, S5_STATE)
    lam_im = p["s5_lambda_im"].reshape(G, 1, S5_STATE)
    log_dt = p["s5_log_dt"].reshape(G, 1, 1)
    bt_re = p["s5_b_re"][0].transpose(0, 2, 1)
    bt_im = p["s5_b_im"][0].transpose(0, 2, 1)
    ar, ai, bbr, bbi = _s5_discretize(lam_re, lam_im, log_dt, bt_re, bt_im)
    mats = (_block_diag_in(bbr, nj).astype(BF), _block_diag_in(bbi, nj).astype(BF),
            _block_diag_out(p["s5_c_re"][0], nj).astype(BF),
            _block_diag_out(-p["s5_c_im"][0], nj).astype(BF),
            ar.reshape(nj, 1, STATE_COLS), ai.reshape(nj, 1, STATE_COLS), dsk)
    h_seg = _to_seg(h, seg_len)
    u = _rms_fwd(h_seg, gain, F32)
    y_pre, yg, init_re, init_im = _s5_fwd(u, mats, seg_len, nseg, comm=comm_s5)
    h_out, pa, pb = _glu_fwd(yg, wa, wb, h_seg)
    disc_in = (lam_re, lam_im, log_dt, bt_re, bt_im)
    return _to_tok(h_out, seg_len), (h_seg, u, mats, y_pre, yg, init_re, init_im, pa, pb, disc_in, seg_len, nseg)


def _mixer_s5_bwd(dh, saved, gain, wa, wb, reduce_start, carry):
    h_seg, u, mats, y_pre, yg, init_re, init_im, pa, pb, disc_in, seg_len, nseg = saved
    T, D = h_seg.shape
    nj = D // LANES
    G = nj * GROUPS_PER_BLOCK
    dh_seg = _to_seg(dh, seg_len)
    dpa, dpb = _glu_bwd_gates(dh_seg, pa, pb)
    dy = _glu_bwd_y(dpa, dpb, wa, wb, y_pre)
    dwa, dwb = _mm_tn("glu_dw", yg, [dpa, dpb])
    comm = _merge_comms(reduce_start(["s5_glu_wa", "s5_glu_wb"],
                                     [dwa.reshape(N_DEV, -1, D), dwb.reshape(N_DEV, -1, D)]) + [carry])
    du, dbdr, dbdi, dcdr, dcdi, dar, dai, dd = _s5_bwd(u, dy, mats, init_re, init_im, seg_len, nseg, comm=comm)
    d_bbr = _diag_of_in(dbdr, nj)
    d_bbi = _diag_of_in(dbdi, nj)
    d_c_re = _diag_of_in(dcdr.transpose(0, 2, 1), nj)
    d_c_im = -_diag_of_in(dcdi.transpose(0, 2, 1), nj)
    dlr, dli, dld, dbr, dbi = _s5_discretize_bwd(
        *disc_in, dar.reshape(G, 1, S5_STATE), dai.reshape(G, 1, S5_STATE), d_bbr, d_bbi)
    small = {"s5_lambda_re": dlr.reshape(1, G, S5_STATE), "s5_lambda_im": dli.reshape(1, G, S5_STATE),
             "s5_log_dt": dld.reshape(1, G),
             "s5_b_re": dbr.transpose(0, 2, 1)[None], "s5_b_im": dbi.transpose(0, 2, 1)[None],
             "s5_c_re": d_c_re[None], "s5_c_im": d_c_im[None], "s5_d": dd}
    dh_in, _, dgain = _rms_bwd(du, h_seg, gain, dh_seg)
    dh_in = _to_tok(dh_in, seg_len)
    return dh_in, dh_in.astype(BF), dgain, small


def _mesh_pos():
    return lax.axis_index("x"), lax.axis_index("y"), lax.axis_index("c")


class _Gather:
    def __init__(self, srcs, slots, send_sems, recv_sems):
        self.srcs, self.slots, self.send_sems, self.recv_sems = srcs, slots, send_sems, recv_sems
        x, y, c = _mesh_pos()
        self.c = c
        self.me, self.sib = (x, y, c), (x, y, 1 - c)
        self.chips = [(1 - x, y), (x, 1 - y), (1 - x, 1 - y)]

    def copy(self, a, k, block, to, own=False):
        dst = self.slots[a].at[4 * block[0] + 2 * block[1] + block[2]]
        return pltpu.make_async_remote_copy(
            src_ref=self.srcs[a] if own else dst, dst_ref=dst, send_sem=self.send_sems.at[7 * a + k],
            recv_sem=self.recv_sems.at[7 * a + k], device_id=to, device_id_type=MESH)

    def own_copies(self, a):
        cps = [self.copy(a, 0, self.me, self.sib, own=True)]
        return cps + [self.copy(a, 1 + j, self.me, (*chip, self.c), own=True) for j, chip in enumerate(self.chips)]

    def start(self):
        for a in range(len(self.srcs)):
            for cp in self.own_copies(a):
                cp.start()

    def finish(self):
        n = len(self.srcs)
        for a in range(n):
            for j, chip in enumerate(self.chips):
                self.copy(a, 1 + j, (*chip, self.c), self.me).wait_recv()
                self.copy(a, 4 + j, (*chip, self.c), self.sib).start()
        for a in range(n):
            self.copy(a, 0, self.sib, self.me).wait_recv()
            for j, chip in enumerate(self.chips):
                self.copy(a, 4 + j, (*chip, 1 - self.c), self.me).wait_recv()
        for a in range(n):
            for cp in self.own_copies(a):
                cp.wait_send()
            for j, chip in enumerate(self.chips):
                self.copy(a, 4 + j, (*chip, self.c), self.sib).wait_send()


def _gather_comm(arrs):
    n = len(arrs)

    def local(xs, outs, sems, a):
        x, y, c = _mesh_pos()
        return pltpu.make_async_copy(xs[a], outs[a].at[4 * x + 2 * y + c], sems[2].at[a])

    def start(xs, outs, sems):
        for a in range(n):
            local(xs, outs, sems, a).start()
        _Gather(xs, outs, sems[0], sems[1]).start()

    def finish(xs, outs, sems):
        _Gather(xs, outs, sems[0], sems[1]).finish()
        for a in range(n):
            local(xs, outs, sems, a).wait()

    return _Comm(list(arrs), [jax.ShapeDtypeStruct((N_DEV,) + a.shape, a.dtype) for a in arrs],
                 [pltpu.SemaphoreType.DMA((7 * n,)), pltpu.SemaphoreType.DMA((7 * n,)),
                  pltpu.SemaphoreType.DMA((n,))], start, finish)


def _exchange_comm(parts):
    n = len(parts)

    def copies(ps, outs, sems):
        x, y, c = _mesh_pos()
        cps = []
        for a in range(n):
            for j in range(1, 4):
                to = (jnp.bitwise_xor(x, j // 2), jnp.bitwise_xor(y, j % 2), c)
                cps.append(pltpu.make_async_remote_copy(
                    src_ref=ps[a].at[j], dst_ref=outs[a].at[j - 1], send_sem=sems[0].at[3 * a + j - 1],
                    recv_sem=sems[1].at[3 * a + j - 1], device_id=to, device_id_type=MESH))
        return cps

    def start(ps, outs, sems):
        for cp in copies(ps, outs, sems):
            cp.start()

    def finish(ps, outs, sems):
        for cp in copies(ps, outs, sems):
            cp.wait()

    return _Comm(list(parts), [jax.ShapeDtypeStruct((3,) + p.shape[1:], p.dtype) for p in parts],
                 [pltpu.SemaphoreType.DMA((3 * n,)), pltpu.SemaphoreType.DMA((3 * n,))], start, finish)


def _run_comm(comm, name):
    ci, co = len(comm.ins), len(comm.outs)

    def body(*refs):
        comm.start(refs[:ci], refs[ci:ci + co], refs[ci + co:])
        comm.finish(refs[:ci], refs[ci:ci + co], refs[ci + co:])

    any_spec = pl.BlockSpec(memory_space=pl.ANY)
    comm.set_results(pl.pallas_call(
        body, in_specs=[any_spec] * ci, out_specs=[any_spec] * co, out_shape=list(comm.outs),
        scratch_shapes=list(comm.sems), name=name, compiler_params=_cparams(0))(*comm.ins))


def _pair_exchange(grads, name):
    n = len(grads)

    def body(*refs):
        gs, outs = refs[:n], refs[n:2 * n]
        send_sems, recv_sems = refs[2 * n:]
        x, y, c = _mesh_pos()
        copies = []
        for a in range(n):
            for k in range(4):
                copies.append(pltpu.make_async_remote_copy(
                    src_ref=gs[a].at[2 * k + 1 - c], dst_ref=outs[a].at[k], send_sem=send_sems.at[4 * a + k],
                    recv_sem=recv_sems.at[4 * a + k], device_id=(x, y, 1 - c), device_id_type=MESH))
        for cp in copies:
            cp.start()
        for cp in copies:
            cp.wait()

    any_spec = pl.BlockSpec(memory_space=pl.ANY)
    return pl.pallas_call(
        body, in_specs=[any_spec] * n, out_specs=[any_spec] * n,
        out_shape=[jax.ShapeDtypeStruct((4,) + g.shape[1:], g.dtype) for g in grads],
        scratch_shapes=[pltpu.SemaphoreType.DMA((4 * n,)), pltpu.SemaphoreType.DMA((4 * n,))],
        name=name, compiler_params=_cparams(0))(*grads)


def _pair_sum(grad, recv, pos):
    _, R, C = grad.shape
    br = _row_block(R, C, PAIR_SUM_ELEMS)

    def body(pos_ref, g_ref, r_ref, o_ref):
        o_ref[...] = (g_ref[...].astype(F32) + r_ref[...].astype(F32)).astype(BF)

    def chip(j, p):
        return jnp.bitwise_xor(p[1], j)

    return pl.pallas_call(
        body, grid_spec=pltpu.PrefetchScalarGridSpec(
            num_scalar_prefetch=1, grid=(4, R // br),
            in_specs=[pl.BlockSpec((None, br, C), lambda j, i, p: (2 * chip(j, p) + p[0], i, 0)),
                      pl.BlockSpec((None, br, C), lambda j, i, p: (chip(j, p), i, 0))],
            out_specs=pl.BlockSpec((None, br, C), lambda j, i, p: (j, i, 0))),
        out_shape=jax.ShapeDtypeStruct((4, R, C), BF), name="pair_sum", compiler_params=_cparams(2))(pos, grad, recv)


def _adamw(w, g, m, v):
    m = ADAM_B1 * m + (1.0 - ADAM_B1) * g
    v = ADAM_B2 * v + (1.0 - ADAM_B2) * (g * g)
    m_hat = m / (1.0 - ADAM_B1 ** ADAM_STEP)
    v_hat = v / (1.0 - ADAM_B2 ** ADAM_STEP)
    return -ADAM_LR * (m_hat / (jnp.sqrt(v_hat) + ADAM_EPS) + ADAM_WD * w), m, v


def _adamw_piece(w, m, v, piece, part, recv, bufs):
    _, R, C = w.shape
    br = _row_block(R, C)

    def body(w_ref, m_ref, v_ref, p_ref, r_ref, b0, b1, b2, b3, g_ref, d_ref, nm_ref, nv_ref):
        g = p_ref[...].astype(F32)
        for j in range(3):
            g = g + r_ref[j].astype(F32)
        d, nm, nv = _adamw(w_ref[...], g, m_ref[...], v_ref[...])
        g_ref[...] = g
        d_ref[...] = d
        nm_ref[...] = nm
        nv_ref[...] = nv

    row = pl.BlockSpec((None, br, C), lambda i: (piece, i, 0))
    any_spec = pl.BlockSpec(memory_space=pl.ANY)
    return pl.pallas_call(
        body, grid=(R // br,),
        in_specs=[row, row, row, pl.BlockSpec((None, br, C), lambda i: (0, i, 0)),
                  pl.BlockSpec((3, br, C), lambda i: (0, i, 0))] + [any_spec] * 4,
        out_specs=[row] * 4, out_shape=[jax.ShapeDtypeStruct(w.shape, F32)] * 4,
        input_output_aliases={5: 0, 6: 1, 7: 2, 8: 3}, name="adamw_piece",
        compiler_params=_cparams(1))(w, m, v, part, recv, *bufs)


def _all_reduce_small(x):
    rows = x.shape[0]

    def body(x_ref, o_ref, buf, send_sems, recv_sems):
        xp, yp, cp = _mesh_pos()
        buf[4 * xp + 2 * yp + cp] = x_ref[...]
        gather = _Gather([x_ref], [buf], send_sems, recv_sems)
        gather.start()
        gather.finish()
        acc = buf[0]
        for d in range(1, N_DEV):
            acc = acc + buf[d]
        o_ref[...] = acc

    vm = pl.BlockSpec(memory_space=pltpu.VMEM)
    return pl.pallas_call(
        body, in_specs=[vm], out_specs=vm, out_shape=jax.ShapeDtypeStruct(x.shape, F32),
        scratch_shapes=[pltpu.VMEM((N_DEV, rows, LANES), F32), pltpu.SemaphoreType.DMA((7,)),
                        pltpu.SemaphoreType.DMA((7,))],
        name="all_reduce_small", compiler_params=_cparams(0))(x)


def _sum_slots(x):
    def body(x_ref, o_ref):
        acc = x_ref[0]
        for d in range(1, N_DEV):
            acc = acc + x_ref[d]
        o_ref[...] = acc

    return pl.pallas_call(body, out_shape=jax.ShapeDtypeStruct(x.shape[1:], F32), name="sum_slots",
                          compiler_params=_cparams(0))(x)


def _adamw_small(w, g, m, v):
    def body(w_ref, g_ref, m_ref, v_ref, d_ref, nm_ref, nv_ref):
        d, nm, nv = _adamw(w_ref[...], g_ref[...], m_ref[...], v_ref[...])
        d_ref[...] = d
        nm_ref[...] = nm
        nv_ref[...] = nv

    sh = jax.ShapeDtypeStruct(w.shape, F32)
    return pl.pallas_call(body, out_shape=[sh] * 3, name="adamw_small", compiler_params=_cparams(0))(w, g, m, v)


def _pack(arrs):
    flat = jnp.concatenate([a.reshape(-1).astype(F32) for a in arrs])
    rows = -(-flat.shape[0] // (SUBLANES * LANES)) * SUBLANES
    return jnp.pad(flat, (0, rows * LANES - flat.shape[0])).reshape(rows, LANES)


def _unpack(buf, shapes):
    flat = buf.reshape(-1)
    out, off = [], 0
    for s in shapes:
        n = 1
        for d in s:
            n *= d
        out.append(flat[off:off + n].reshape(s))
        off += n
    return out


BIG = ("ffn_w1", "ffn_w3", "ffn_w2", "ab_w_in", "ab_w_out", "s5_glu_wa", "s5_glu_wb")
NAMES = ("ln_ffn_pre", "ln_mix", "ln_ffn_post", "ln_final", "ffn_w1", "ffn_w3", "ffn_w2", "ab_w_in",
         "ab_conv_w", "ab_w_out", "s5_lambda_re", "s5_lambda_im", "s5_log_dt", "s5_b_re", "s5_b_im",
         "s5_c_re", "s5_c_im", "s5_d", "s5_glu_wa", "s5_glu_wb")


def kernel(x, ln_ffn_pre, ln_mix, ln_ffn_post, ln_final, ffn_w1, ffn_w3, ffn_w2, ab_w_in, ab_conv_w, ab_w_out, s5_lambda_re, s5_lambda_im, s5_log_dt, s5_b_re, s5_b_im, s5_c_re, s5_c_im, s5_d, s5_glu_wa, s5_glu_wb, loss_target, m_ln_ffn_pre, m_ln_mix, m_ln_ffn_post, m_ln_final, m_ffn_w1, m_ffn_w3, m_ffn_w2, m_ab_w_in, m_ab_conv_w, m_ab_w_out, m_s5_lambda_re, m_s5_lambda_im, m_s5_log_dt, m_s5_b_re, m_s5_b_im, m_s5_c_re, m_s5_c_im, m_s5_d, m_s5_glu_wa, m_s5_glu_wb, v_ln_ffn_pre, v_ln_mix, v_ln_ffn_post, v_ln_final, v_ffn_w1, v_ffn_w3, v_ffn_w2, v_ab_w_in, v_ab_conv_w, v_ab_w_out, v_s5_lambda_re, v_s5_lambda_im, v_s5_log_dt, v_s5_b_re, v_s5_b_im, v_s5_c_re, v_s5_c_im, v_s5_d, v_s5_glu_wa, v_s5_glu_wb):
    w = dict(zip(NAMES, (ln_ffn_pre, ln_mix, ln_ffn_post, ln_final, ffn_w1, ffn_w3, ffn_w2, ab_w_in, ab_conv_w,
                         ab_w_out, s5_lambda_re, s5_lambda_im, s5_log_dt, s5_b_re, s5_b_im, s5_c_re, s5_c_im,
                         s5_d, s5_glu_wa, s5_glu_wb)))
    mom = dict(zip(NAMES, (m_ln_ffn_pre, m_ln_mix, m_ln_ffn_post, m_ln_final, m_ffn_w1, m_ffn_w3, m_ffn_w2,
                           m_ab_w_in, m_ab_conv_w, m_ab_w_out, m_s5_lambda_re, m_s5_lambda_im, m_s5_log_dt,
                           m_s5_b_re, m_s5_b_im, m_s5_c_re, m_s5_c_im, m_s5_d, m_s5_glu_wa, m_s5_glu_wb)))
    var = dict(zip(NAMES, (v_ln_ffn_pre, v_ln_mix, v_ln_ffn_post, v_ln_final, v_ffn_w1, v_ffn_w3, v_ffn_w2,
                           v_ab_w_in, v_ab_conv_w, v_ab_w_out, v_s5_lambda_re, v_s5_lambda_im, v_s5_log_dt,
                           v_s5_b_re, v_s5_b_im, v_s5_c_re, v_s5_c_im, v_s5_d, v_s5_glu_wa, v_s5_glu_wb)))
    nb, seq, D = x.shape
    T = nb * seq
    assert ln_mix.shape[0] == 2 and ab_w_in.shape[0] == 1 and s5_glu_wa.shape[0] == 1
    xc, yc, cc = _mesh_pos()
    dev = 4 * xc + 2 * yc + cc
    pos = jnp.stack([cc, 2 * xc + yc]).astype(jnp.int32)
    bq = min(ATTN_TILE, seq)
    tabs =_rope_tables(seq) + (_branch_bias(seq // bq, bq),)

    def ffn_piece(k, li, fj):
        return w[k][li, fj].astype(BF)

    g0 = _gather_comm([ffn_piece("ffn_w1", 0, 0), ffn_piece("ffn_w3", 0, 0), ab_conv_w[0], s5_d])
    _run_comm(g0, "gather_first")
    w1, w3 = {(0, 0): g0.results[0]}, {(0, 0): g0.results[1]}
    w2 = {}
    conv_w = g0.results[2].transpose(1, 0, 2).reshape(3, -1)
    dsk = g0.results[3].reshape(1, D)
    gains = {k: [w[k][i:i + 1] for i in range(2)] for k in ("ln_ffn_pre", "ln_mix", "ln_ffn_post")}

    h = x.reshape(T, D)
    saved = {}

    def ffn_fwd(h, gain, key, tag, comm_up, comm_down, after_up):
        n = _rms_fwd(h, gain, BF)
        t1, t3, g = _ffn_up(n, w1[key], w3[key], comm=comm_up)
        after_up()
        saved[tag] = (h, n, t1, t3, g)
        return _ffn_down(g, w2[key], h, comm=comm_down)

    c_up = _gather_comm([ffn_piece("ffn_w2", 0, 0), ab_w_out[0].astype(BF)])
    c_dn = _gather_comm([ab_w_in[0].astype(BF)])
    h = ffn_fwd(h, gains["ln_ffn_pre"][0], (0, 0), "pre0", c_up, c_dn,
                lambda: w2.update({(0, 0): c_up.results[0]}))
    wout = c_up.results[1].reshape(-1, D)
    wing = c_dn.results[0]
    c_proj = _gather_comm([ffn_piece("ffn_w1", 0, 1)])
    c_attn = _gather_comm([ffn_piece("ffn_w3", 0, 1), s5_glu_wa[0].astype(BF)])
    c_out = _gather_comm([s5_glu_wb[0].astype(BF)])
    h, saved["mix0"] = _mixer_ab_fwd(h, gains["ln_mix"][0], wing, conv_w, wout, tabs, nb, seq, c_proj, c_attn, c_out)
    w1[(0, 1)] = c_proj.results[0]
    w3[(0, 1)] = c_attn.results[0]
    wa = c_attn.results[1].reshape(-1, D)
    wb = c_out.results[0].reshape(-1, D)
    c_up2 = _gather_comm([ffn_piece("ffn_w2", 0, 1), ffn_piece("ffn_w1", 1, 0)])
    c_dn = _gather_comm([ffn_piece("ffn_w3", 1, 0)])
    h = ffn_fwd(h, gains["ln_ffn_post"][0], (0, 1), "post0", c_up2, c_dn,
                lambda: w2.update({(0, 1): c_up2.results[0]}))
    w1[(1, 0)] = c_up2.results[1]
    w3[(1, 0)] = c_dn.results[0]
    c_up3 = _gather_comm([ffn_piece("ffn_w2", 1, 0), ffn_piece("ffn_w1", 1, 1)])
    c_dn = _gather_comm([ffn_piece("ffn_w3", 1, 1)])
    h = ffn_fwd(h, gains["ln_ffn_pre"][1], (1, 0), "pre1", c_up3, c_dn,
                lambda: w2.update({(1, 0): c_up3.results[0]}))
    w1[(1, 1)] = c_up3.results[1]
    w3[(1, 1)] = c_dn.results[0]
    c_s5 = _gather_comm([ffn_piece("ffn_w2", 1, 1)])
    h, saved["mix1"] = _mixer_s5_fwd(h, gains["ln_mix"][1], w, dsk, wa, wb, nb, seq, c_s5)
    w2[(1, 1)] = c_s5.results[0]
    h = ffn_fwd(h, gains["ln_ffn_post"][1], (1, 1), "post1", None, None, lambda: None)
    dh, dhb, d_ln_final, loss_part = _loss_head(h, ln_final.reshape(1, D), loss_target.reshape(T, D))
    loss = lax.psum(loss_part[0, 0], ("x", "y", "c"))

    reduced = {}

    def reduce_start(names, grads):
        recv = _pair_exchange(grads, "pair_exchange")
        comms = []
        for nm, g, r in zip(names, grads, recv):
            part = _pair_sum(g, r, pos)
            comms.append(_exchange_comm([part]))
            reduced[nm] = (part, comms[-1])
        return comms

    def ffn_bwd(dh, dhb, key, tag, gain, carry, is_last=False, comm_dw2=None):
        h_in, n, t1, t3, g = saved[tag]
        da1, da3 = _ffn_bwd_hidden(dhb, w2[key], t1, t3, comm=carry)
        c2, = reduce_start([("ffn_w2",) + key], [_ffn_dw2(g, dhb, comm=comm_dw2)])
        dw1, dw3 = _ffn_dw13(n, da1, da3, comm=c2)
        c1, c3 = reduce_start([("ffn_w1",) + key, ("ffn_w3",) + key], [dw1, dw3])
        res = _ffn_dn_rms(da1, da3, w1[key], w3[key], h_in, gain, dh,
                          comm=_merge_comms([c1, c3]) if is_last else c1)
        return list(res) + [None if is_last else c3]

    g_small = {"ln_final": d_ln_final.reshape(D)}
    g_ln = {k: [None, None] for k in gains}
    dh, dhb, g_ln["ln_ffn_post"][1], carry = ffn_bwd(dh, dhb, (1, 1), "post1", gains["ln_ffn_post"][1], None)
    dh, dhb, g_ln["ln_mix"][1], s5_small = _mixer_s5_bwd(
        dh, saved["mix1"], gains["ln_mix"][1], wa, wb, reduce_start, carry)
    s5_names = list(s5_small)
    c_s5_grads = _gather_comm([_pack([s5_small[k] for k in s5_names])])
    dh, dhb, g_ln["ln_ffn_pre"][1], carry = ffn_bwd(dh, dhb, (1, 0), "pre1", gains["ln_ffn_pre"][1], None,
                                                    comm_dw2=c_s5_grads)
    g_red = dict(zip(s5_names, _unpack(_sum_slots(c_s5_grads.results[0]), [s5_small[k].shape for k in s5_names])))
    dh, dhb, g_ln["ln_ffn_post"][0], carry = ffn_bwd(dh, dhb, (0, 1), "post0", gains["ln_ffn_post"][0], carry)
    dh, dhb, g_ln["ln_mix"][0], g_small["ab_conv_w"], carry = _mixer_ab_bwd(
        dh, dhb, saved["mix0"], gains["ln_mix"][0], wing, conv_w, wout, tabs, nb, seq, reduce_start, carry)
    dh, dhb, g_ln["ln_ffn_pre"][0], _ = ffn_bwd(dh, dhb, (0, 0), "pre0", gains["ln_ffn_pre"][0], carry, is_last=True)
    grad_x = dh.reshape(nb, seq, D)
    for k in g_ln:
        g_small[k] = jnp.concatenate(g_ln[k], axis=0)

    out = {}
    for k in BIG:
        transposed = k in ("ffn_w1", "ffn_w3")
        pieces = [(li, fj) for li in range(2) for fj in range(2)] if w[k].ndim == 4 else [None]

        def view(a):
            a = a.swapaxes(-1, -2) if transposed else a
            return a.reshape(len(pieces), -1, a.shape[-1])

        w3d, m3d, v3d = view(w[k]), view(mom[k]), view(var[k])
        bufs = [lax.empty(w3d.shape, F32) for _ in range(4)]
        for q, key in enumerate(pieces):
            part, comm = reduced[k if key is None else (k,) + key]
            bufs = _adamw_piece(w3d, m3d, v3d, q, part, comm.results[0], bufs)
        if transposed:
            out[k] = [t.reshape(w[k].shape[:2] + w3d.shape[1:]).swapaxes(-1, -2) for t in bufs]
        else:
            out[k] = [t.reshape(w[k].shape) for t in bufs]

    small_names = [k for k in NAMES if k not in BIG]
    late_names = [k for k in small_names if k not in g_red]
    g_red.update(zip(late_names, _unpack(_all_reduce_small(_pack([g_small[k] for k in late_names])),
                                         [g_small[k].shape for k in late_names])))
    cw = w["ab_conv_w"].shape[-1]
    g_red["ab_conv_w"] = lax.dynamic_slice_in_dim(g_red["ab_conv_w"], dev * cw, cw, axis=1)[None]
    dsz = w["s5_d"].shape[-1]
    g_red["s5_d"] = lax.dynamic_slice_in_dim(g_red["s5_d"].reshape(1, -1), dev * dsz, dsz, axis=1)
    shapes = [w[k].shape for k in small_names]
    g_red = {k: g_red[k].reshape(w[k].shape) for k in small_names}
    d_s, m_s, v_s = _adamw_small(_pack([w[k] for k in small_names]), _pack([g_red[k] for k in small_names]),
                                 _pack([mom[k] for k in small_names]), _pack([var[k] for k in small_names]))
    for k, d, nm, nv in zip(small_names, _unpack(d_s, shapes), _unpack(m_s, shapes), _unpack(v_s, shapes)):
        out[k] = [g_red[k], d, nm, nv]

    return (loss, grad_x, *[out[k][0] for k in NAMES], *[out[k][1] for k in NAMES],
            *[out[k][2] for k in NAMES], *[out[k][3] for k in NAMES])
```

```python
import jax
import jax.numpy as jnp
from jax import lax
from jax.experimental import pallas as pl
from jax.experimental.pallas import tpu as pltpu

F32, BF = jnp.float32, jnp.bfloat16
N_DEV = 8
MESH = pl.DeviceIdType.MESH
LANES = 128
SUBLANES = 8
VMEM_LIMIT = 56 * 2 ** 20
ROW_TILE = 512
FFN_ROW_TILE = 1024
COL_TILE = 512
ATTN_TILE = 512
SCAN_UNROLL = 8
ELEMS_PER_BLOCK = 512 * 1024
PAIR_SUM_ELEMS = 2048 * 1024
RMS_EPS = 1e-6
ROPE_THETA = 10000.0
NEG_INF = -1e30
S5_STATE = 64
S5_GROUP = 16
GROUPS_PER_BLOCK = LANES // S5_GROUP
STATE_COLS = GROUPS_PER_BLOCK * S5_STATE
DILATED_PATTERN = ((128, 1), (512, 4), (2048, 16))
ADAM_LR, ADAM_B1, ADAM_B2, ADAM_EPS, ADAM_WD, ADAM_STEP = 0.001, 0.9, 0.999, 1e-08, 0.01, 10
GELU_C = 0.7978845608028654
GELU_A = 0.044715


def _cparams(n_grid, vmem=VMEM_LIMIT):
    sem = ("arbitrary",) * n_grid if n_grid else None
    return pltpu.CompilerParams(dimension_semantics=sem, vmem_limit_bytes=vmem)


def _sig(x):
    return 1.0 / (1.0 + jnp.exp(-x))


def _gelu(x):
    return 0.5 * x * (1.0 + jnp.tanh(GELU_C * (x + GELU_A * x * x * x)))


def _gelu_grad(x):
    t = jnp.tanh(GELU_C * (x + GELU_A * x * x * x))
    return 0.5 * (1.0 + t) + 0.5 * x * (1.0 - t * t) * GELU_C * (1.0 + 3.0 * GELU_A * x * x)


def _dot(a, b, dims):
    a = a if a.dtype == BF else a.astype(BF)
    b = b if b.dtype == BF else b.astype(BF)
    return lax.dot_general(a, b, (dims, ((), ())), preferred_element_type=F32)


NN = ((1,), (0,))
NT = ((1,), (1,))
TN = ((0,), (0,))


def _row_block(rows, cols, elems=ELEMS_PER_BLOCK, mult=16):
    cap = max(mult, elems // cols)
    best = None
    for b in range(mult, min(rows, cap) + 1, mult):
        if rows % b == 0:
            best = b
    return rows if best is None else best


class _Comm:
    def __init__(self, ins, outs, sems, start, finish, members=()):
        self.ins, self.outs, self.sems, self.start, self.finish = ins, outs, sems, start, finish
        self.members = members
        self.results = None

    def set_results(self, res):
        self.results = list(res)
        off = 0
        for m in self.members:
            m.set_results(res[off:off + len(m.outs)])
            off += len(m.outs)


def _merge_comms(comms):
    comms = [c for c in comms if c is not None]
    if len(comms) < 2:
        return comms[0] if comms else None

    def each(fn_name, ins, outs, sems):
        i = o = s = 0
        for c in comms:
            ni, no, ns = len(c.ins), len(c.outs), len(c.sems)
            getattr(c, fn_name)(ins[i:i + ni], outs[o:o + no], sems[s:s + ns])
            i, o, s = i + ni, o + no, s + ns

    return _Comm([a for c in comms for a in c.ins], [a for c in comms for a in c.outs],
                 [a for c in comms for a in c.sems],
                 lambda ins, outs, sems: each("start", ins, outs, sems),
                 lambda ins, outs, sems: each("finish", ins, outs, sems), members=tuple(comms))


def _call(body, name, grid, in_specs, out_specs, out_shape, args, scratch=(), comm=None):
    in_specs, out_specs, out_shape, scratch = list(in_specs), list(out_specs), list(out_shape), list(scratch)
    if comm is None:
        return pl.pallas_call(body, grid=grid, in_specs=in_specs, out_specs=out_specs, out_shape=out_shape,
                              scratch_shapes=scratch, name=name, compiler_params=_cparams(len(grid)))(*args)
    n_in, n_out, n_sc = len(in_specs), len(out_specs), len(scratch)
    ci, co = len(comm.ins), len(comm.outs)

    def hosted(*refs):
        ins, refs = refs[:n_in], refs[n_in:]
        cins, refs = refs[:ci], refs[ci:]
        outs, refs = refs[:n_out], refs[n_out:]
        couts, refs = refs[:co], refs[co:]
        sc, csems = refs[:n_sc], refs[n_sc:]
        first = last = None
        for d, n in enumerate(grid):
            p = pl.program_id(d)
            first = (p == 0) if first is None else first & (p == 0)
            last = (p == n - 1) if last is None else last & (p == n - 1)

        @pl.when(first)
        def _():
            comm.start(cins, couts, csems)

        body(*ins, *outs, *sc)

        @pl.when(last)
        def _():
            comm.finish(cins, couts, csems)

    any_spec = pl.BlockSpec(memory_space=pl.ANY)
    res = pl.pallas_call(
        hosted, grid=grid, in_specs=in_specs + [any_spec] * ci, out_specs=out_specs + [any_spec] * co,
        out_shape=out_shape + list(comm.outs), scratch_shapes=scratch + list(comm.sems), name=name,
        compiler_params=_cparams(len(grid)))(*args, *comm.ins)
    comm.set_results(res[n_out:])
    return list(res[:n_out])


def _mm(name, grid, operands, pairs, n_acc, acc_shape, extras, outs, epilogue, comm=None, nrow=1, ncol=1,
        whole_tile_epilogue=False):
    nk = grid[2]
    n_op, n_ex, n_out = len(operands), len(extras), len(outs)

    def part_of(ref, dim, t, n):
        if n == 1:
            return ref
        size = ref.shape[dim] // n
        idx = [slice(None)] * len(ref.shape)
        idx[dim] = pl.ds(t * size, size)
        return ref.at[tuple(idx)]

    def tile_of(ref, r, c):
        return part_of(part_of(ref, 0, r, nrow), 1, c, ncol)

    def products(op, r, c):
        parts = [None] * n_acc
        for ai, bi, dims, ci in pairs:
            a = part_of(op[ai], 1 - dims[0][0], r, nrow)
            b = part_of(op[bi], 1 - dims[1][0], c, ncol)
            d = _dot(a[...], b[...], dims)
            parts[ci] = d if parts[ci] is None else parts[ci] + d
        return parts

    def body(*refs):
        op = refs[:n_op]
        ex = refs[n_op:n_op + n_ex]
        out = refs[n_op + n_ex:n_op + n_ex + n_out]
        acc = refs[n_op + n_ex + n_out:]
        tiles = [(r, c) for r in range(nrow) for c in range(ncol)]

        def views(refs_, t):
            return [tile_of(q, *t) for q in refs_]

        if nk == 1:
            parts = products(op, *tiles[0])
            for q, t in enumerate(tiles):
                nxt = products(op, *tiles[q + 1]) if q + 1 < len(tiles) else None
                epilogue(parts, views(ex, t), views(out, t))
                parts = nxt
            return
        k = pl.program_id(2)

        @pl.when(k == 0)
        def _():
            for q in acc:
                q[...] = jnp.zeros_like(q)

        for t in tiles:
            parts = products(op, *t)
            for q, p in zip(views(acc, t), parts):
                q[...] += p

        @pl.when(k == nk - 1)
        def _():
            if whole_tile_epilogue:
                epilogue(acc, ex, out)
                return
            for t in tiles:
                epilogue([q[...] for q in views(acc, t)], views(ex, t), views(out, t))

    return _call(body, name, grid, [s for _, s in operands] + [s for _, s in extras], [s for _, s in outs],
                 [sh for sh, _ in outs], [a for a, _ in operands] + [a for a, _ in extras],
                 scratch=[pltpu.VMEM(acc_shape, F32) for _ in range(n_acc if nk > 1 else 0)], comm=comm)


def _to_seg(a, seg_len):
    T, D = a.shape
    return a.reshape(SUBLANES, seg_len, D).transpose(1, 0, 2).reshape(T, D)


def _to_tok(a, seg_len):
    T, D = a.shape
    return a.reshape(seg_len, SUBLANES, D).transpose(1, 0, 2).reshape(T, D)


def _rms_fwd(h, gain, out_dtype):
    T, D = h.shape
    bm = min(ROW_TILE, T)

    def body(h_ref, g_ref, o_ref):
        x = h_ref[...]
        r = lax.rsqrt(jnp.mean(x * x, axis=-1, keepdims=True) + RMS_EPS)
        o_ref[...] = (x * r * g_ref[...]).astype(out_dtype)

    row = pl.BlockSpec((bm, D), lambda i: (i, 0))
    return pl.pallas_call(
        body, grid=(T // bm,), in_specs=[row, pl.BlockSpec((1, D), lambda i: (0, 0))],
        out_specs=row, out_shape=jax.ShapeDtypeStruct((T, D), out_dtype), name="rms_fwd",
        compiler_params=_cparams(1))(h, gain)


def _rms_bwd_rows(dn, x, g):
    r = lax.rsqrt(jnp.mean(x * x, axis=-1, keepdims=True) + RMS_EPS)
    xh = x * r
    dng = dn * g
    dx = r * (dng - xh * jnp.mean(dng * xh, axis=-1, keepdims=True))
    return dx, jnp.sum(dn * xh, axis=0, keepdims=True)


def _rms_bwd(dn, h, gain, dh_up):
    T, D = h.shape
    bm = min(ROW_TILE, T)

    def body(dn_ref, h_ref, g_ref, up_ref, dh_ref, dhb_ref, dg_ref):
        dx, dg = _rms_bwd_rows(dn_ref[...], h_ref[...], g_ref[...])
        dh = up_ref[...] + dx
        dh_ref[...] = dh
        dhb_ref[...] = dh.astype(BF)

        @pl.when(pl.program_id(0) == 0)
        def _():
            dg_ref[...] = jnp.zeros_like(dg_ref)

        dg_ref[...] += dg

    row = pl.BlockSpec((bm, D), lambda i: (i, 0))
    vec = pl.BlockSpec((1, D), lambda i: (0, 0))
    return pl.pallas_call(
        body, grid=(T // bm,), in_specs=[row, row, vec, row], out_specs=[row, row, vec],
        out_shape=[jax.ShapeDtypeStruct((T, D), F32), jax.ShapeDtypeStruct((T, D), BF),
                   jax.ShapeDtypeStruct((1, D), F32)],
        name="rms_bwd", compiler_params=_cparams(1))(dn, h, gain, dh_up)


def _loss_head(h, gain, target):
    T, D = h.shape
    bm = min(ROW_TILE, T)

    def body(h_ref, g_ref, t_ref, dh_ref, dhb_ref, dg_ref, loss_ref):
        x = h_ref[...]
        g = g_ref[...]
        r = lax.rsqrt(jnp.mean(x * x, axis=-1, keepdims=True) + RMS_EPS)
        err = x * r * g - t_ref[...]
        part = 0.5 * jnp.sum(jnp.sum(err * err, axis=-1, keepdims=True), axis=0, keepdims=True) / D
        dx, dg = _rms_bwd_rows(err / D, x, g)
        dh_ref[...] = dx
        dhb_ref[...] = dx.astype(BF)

        @pl.when(pl.program_id(0) == 0)
        def _():
            dg_ref[...] = jnp.zeros_like(dg_ref)
            loss_ref[...] = jnp.zeros_like(loss_ref)

        dg_ref[...] += dg
        loss_ref[...] += jnp.broadcast_to(part, loss_ref.shape)

    row = pl.BlockSpec((bm, D), lambda i: (i, 0))
    vec = pl.BlockSpec((1, D), lambda i: (0, 0))
    return pl.pallas_call(
        body, grid=(T // bm,), in_specs=[row, vec, row],
        out_specs=[row, row, vec, pl.BlockSpec((SUBLANES, LANES), lambda i: (0, 0))],
        out_shape=[jax.ShapeDtypeStruct((T, D), F32), jax.ShapeDtypeStruct((T, D), BF),
                   jax.ShapeDtypeStruct((1, D), F32), jax.ShapeDtypeStruct((SUBLANES, LANES), F32)],
        name="loss_head", compiler_params=_cparams(1))(h, gain, target)


def _ffn_up(n, w1g, w3g, comm=None):
    T, D = n.shape
    fs = w1g.shape[-1]
    bm = min(FFN_ROW_TILE, T)
    wspec = pl.BlockSpec((None, D, fs), lambda s, i, k: (s, 0, 0))
    ospec = pl.BlockSpec((None, bm, fs), lambda s, i, k: (s, i, 0))

    def epi(accs, ex, outs):
        a1, a3 = accs
        sg = _sig(a1)
        silu = a1 * sg
        outs[0][...] = (a3 * sg * (1.0 + a1 * (1.0 - sg))).astype(BF)
        outs[1][...] = silu.astype(BF)
        outs[2][...] = (silu * a3).astype(BF)

    sh = jax.ShapeDtypeStruct((N_DEV, T, fs), BF)
    return _mm("ffn_up", (N_DEV, T // bm, 1),
               [(n, pl.BlockSpec((bm, D), lambda s, i, k: (i, 0))), (w1g, wspec), (w3g, wspec)],
               [(0, 1, NN, 0), (0, 2, NN, 1)], 2, None, [], [(sh, ospec)] * 3, epi, comm=comm,
               nrow=max(1, bm // ROW_TILE))


def _ffn_down(g, w2g, h, comm=None):
    _, T, fs = g.shape
    D = h.shape[1]
    bm = min(FFN_ROW_TILE, T)
    row = pl.BlockSpec((bm, D), lambda i, j, s: (i, 0))

    def epi(accs, ex, outs):
        outs[0][...] = ex[0][...] + 0.5 * accs[0]

    return _mm("ffn_down", (T // bm, 1, N_DEV),
               [(g, pl.BlockSpec((None, bm, fs), lambda i, j, s: (s, i, 0))),
                (w2g, pl.BlockSpec((None, fs, D), lambda i, j, s: (s, 0, 0)))],
               [(0, 1, NN, 0)], 1, (bm, D), [(h, row)],
               [(jax.ShapeDtypeStruct((T, D), F32), row)], epi, comm=comm,
               nrow=max(1, bm // ROW_TILE), ncol=max(1, D // COL_TILE))[0]


def _ffn_bwd_hidden(dhb, w2g, t1, t3, comm=None):
    T, D = dhb.shape
    fs = t1.shape[-1]
    bm = min(FFN_ROW_TILE, T)
    aspec = pl.BlockSpec((None, bm, fs), lambda s, i, k: (s, i, 0))

    def epi(accs, ex, outs):
        dg = 0.5 * accs[0]
        outs[0][...] = (dg * ex[0][...].astype(F32)).astype(BF)
        outs[1][...] = (dg * ex[1][...].astype(F32)).astype(BF)

    sh = jax.ShapeDtypeStruct((N_DEV, T, fs), BF)
    return _mm("ffn_bwd_hidden", (N_DEV, T // bm, 1),
               [(dhb, pl.BlockSpec((bm, D), lambda s, i, k: (i, 0))),
                (w2g, pl.BlockSpec((None, fs, D), lambda s, i, k: (s, 0, 0)))],
               [(0, 1, NT, 0)], 1, None, [(t1, aspec), (t3, aspec)], [(sh, aspec)] * 2, epi, comm=comm,
               nrow=max(1, bm // ROW_TILE))


def _ffn_dw2(g, dhb, comm=None):
    _, T, fs = g.shape
    D = dhb.shape[1]
    bn = min(COL_TILE, D)

    def epi(accs, ex, outs):
        outs[0][...] = (0.5 * accs[0]).astype(BF)

    return _mm("ffn_dw2", (N_DEV, D // bn, 1),
               [(g, pl.BlockSpec((None, T, fs), lambda s, j, k: (s, 0, 0))),
                (dhb, pl.BlockSpec((T, bn), lambda s, j, k: (0, j)))],
               [(0, 1, TN, 0)], 1, None, [],
               [(jax.ShapeDtypeStruct((N_DEV, fs, D), BF), pl.BlockSpec((None, fs, bn), lambda s, j, k: (s, 0, j)))],
               epi, comm=comm)[0]


def _ffn_dw13(n, da1, da3, comm=None):
    T, D = n.shape
    fs = da1.shape[-1]
    bn = min(COL_TILE, D)
    dspec = pl.BlockSpec((None, T, fs), lambda s, j, k: (s, 0, 0))
    ospec = pl.BlockSpec((None, fs, bn), lambda s, j, k: (s, 0, j))

    def epi(accs, ex, outs):
        outs[0][...] = accs[0].astype(BF)
        outs[1][...] = accs[1].astype(BF)

    sh = jax.ShapeDtypeStruct((N_DEV, fs, D), BF)
    return _mm("ffn_dw13", (N_DEV, D // bn, 1),
               [(da1, dspec), (da3, dspec), (n, pl.BlockSpec((T, bn), lambda s, j, k: (0, j)))],
               [(0, 2, TN, 0), (1, 2, TN, 1)], 2, None, [], [(sh, ospec)] * 2, epi, comm=comm)


def _ffn_dn_rms(da1, da3, w1g, w3g, h, gain, dh_up, comm=None):
    _, T, fs = da1.shape
    D = w1g.shape[-2]
    bm = min(ROW_TILE, T)
    rows_per_pass = min(64, bm)
    dspec = pl.BlockSpec((None, bm, fs), lambda i, j, s: (s, i, 0))
    wspec = pl.BlockSpec((None, D, fs), lambda i, j, s: (s, 0, 0))
    row = pl.BlockSpec((bm, D), lambda i, j, s: (i, 0))
    vec = pl.BlockSpec((1, D), lambda i, j, s: (0, 0))

    def epi(acc, ex, outs):
        h_ref, g_ref, up_ref = ex
        dh_ref, dhb_ref, dg_ref = outs

        @pl.when(pl.program_id(0) == 0)
        def _():
            dg_ref[...] = jnp.zeros_like(dg_ref)

        g = g_ref[...]
        dg = jnp.zeros((1, D), F32)
        for r in range(bm // rows_per_pass):
            rows = pl.ds(r * rows_per_pass, rows_per_pass)
            dx, dg_r = _rms_bwd_rows(acc[0][rows, :], h_ref[rows, :], g)
            dh = up_ref[rows, :] + dx
            dh_ref[rows, :] = dh
            dhb_ref[rows, :] = dh.astype(BF)
            dg = dg + dg_r
        dg_ref[...] += dg

    return _mm("ffn_dn_rms", (T // bm, 1, N_DEV),
               [(da1, dspec), (w1g, wspec), (da3, dspec), (w3g, wspec)],
               [(0, 1, NT, 0), (2, 3, NT, 0)], 1, (bm, D), [(h, row), (gain, vec), (dh_up, row)],
               [(jax.ShapeDtypeStruct((T, D), F32), row), (jax.ShapeDtypeStruct((T, D), BF), row),
                (jax.ShapeDtypeStruct((1, D), F32), vec)],
               epi, comm=comm, ncol=max(1, D // COL_TILE), whole_tile_epilogue=True)


def _rope_tables(seq):
    half = LANES // 2
    inv = ROPE_THETA ** (-jnp.arange(0, half, dtype=F32) * 2.0 / LANES)
    ang = jnp.arange(seq, dtype=F32)[:, None] * inv[None, :]
    cos, sin = jnp.cos(ang), jnp.sin(ang)
    return jnp.concatenate([cos, cos], axis=1), jnp.concatenate([-sin, sin], axis=1)


def _branch_bias(nq, bq):
    d = (jnp.arange(nq)[:, None, None] * bq + jnp.arange(bq)[None, :, None]
         - jnp.arange(bq)[None, None, :])
    mult = jnp.zeros(d.shape, F32)
    for window, dil in DILATED_PATTERN:
        mult = mult + ((d >= 0) & (d % dil == 0) & (d <= window)).astype(F32)
    return jnp.where(mult > 0, jnp.log(jnp.maximum(mult, 1.0)), NEG_INF)


def _proj_fwd(u, wing, comm=None):
    T, D = u.shape
    ws = wing.shape[-1]
    bm = min(FFN_ROW_TILE, T)

    def epi(accs, ex, outs):
        outs[0][...] = accs[0]

    return _mm("proj_fwd", (N_DEV, T // bm, 1),
               [(u, pl.BlockSpec((bm, D), lambda s, i, k: (i, 0))),
                (wing, pl.BlockSpec((None, D, ws), lambda s, i, k: (s, 0, 0)))],
               [(0, 1, NN, 0)], 1, None, [],
               [(jax.ShapeDtypeStruct((T, N_DEV * ws), F32),
                 pl.BlockSpec((bm, ws), lambda s, i, k: (i, s)))], epi, comm=comm,
               nrow=max(1, bm // ROW_TILE))[0]


def _rope_fwd(proj, cosf, sinf, seq, nh):
    T = proj.shape[0]
    bs = min(ROW_TILE, seq)
    nst = seq // bs
    scale = LANES ** -0.5

    def body(x_ref, c_ref, s_ref, o_ref):
        j = pl.program_id(1)
        c = c_ref[...]
        s = s_ref[...]
        mul = jnp.where(j == 0, scale, 1.0)
        for h in range(nh):
            cols = slice(h * LANES, (h + 1) * LANES)
            t = x_ref[:, cols]
            rot = (t * c + pltpu.roll(t, LANES // 2, 1) * s) * mul
            o_ref[:, cols] = jnp.where(j < 2, rot, t).astype(BF)

    blk = pl.BlockSpec((bs, nh * LANES), lambda r, j: (r, j))
    tab = pl.BlockSpec((bs, LANES), lambda r, j: (r % nst, 0))
    return pl.pallas_call(
        body, grid=(T // bs, 3), in_specs=[blk, tab, tab], out_specs=blk,
        out_shape=jax.ShapeDtypeStruct((T, 3 * nh * LANES), BF), name="rope_fwd",
        compiler_params=_cparams(2))(proj, cosf, sinf)


def _attn_fwd(qkv, bias, nb, seq, nh, comm=None):
    T = nb * seq
    bq = bias.shape[1]
    nq = seq // bq

    def body(q_ref, k_ref, v_ref, b_ref, o_ref, lse_ref):
        qi = pl.program_id(2)
        q = q_ref[...]

        def step(kj, carry):
            m, l, acc = carry
            rows = pl.ds(pl.multiple_of(kj * bq, bq), bq)
            s = _dot(q, k_ref[rows, :], NT) + b_ref[qi - kj]
            m_new = jnp.maximum(m, jnp.max(s, axis=1, keepdims=True))
            p = jnp.exp(s - m_new)
            alpha = jnp.exp(m - m_new)
            l = alpha * l + jnp.sum(p, axis=1, keepdims=True)
            acc = alpha * acc + _dot(p, v_ref[rows, :], NN)
            return m_new, l, acc

        init = (jnp.full((bq, 1), NEG_INF, F32), jnp.zeros((bq, 1), F32), jnp.zeros((bq, LANES), F32))
        m, l, acc = lax.fori_loop(0, qi + 1, step, init)
        o_ref[...] = (acc / l).astype(BF)
        lse_ref[...] = m + jnp.log(l)

    return _call(
        body, "attn_fwd", (nb, nh, nq),
        [pl.BlockSpec((bq, LANES), lambda b, h, i: (b * nq + i, h)),
         pl.BlockSpec((seq, LANES), lambda b, h, i: (b, nh + h)),
         pl.BlockSpec((seq, LANES), lambda b, h, i: (b, 2 * nh + h)),
         pl.BlockSpec((nq, bq, bq), lambda b, h, i: (0, 0, 0))],
        [pl.BlockSpec((bq, LANES), lambda b, h, i: (b * nq + i, h)),
         pl.BlockSpec((None, bq, 1), lambda b, h, i: (h, b * nq + i, 0))],
        [jax.ShapeDtypeStruct((T, 2 * nh * LANES), BF), jax.ShapeDtypeStruct((nh, T, 1), F32)],
        (qkv, qkv, qkv, bias), comm=comm)


def _attn_bwd(qkv, cat, dcat, lse, bias, nb, seq, nh, comm=None):
    T = nb * seq
    bq = bias.shape[1]
    nq = seq // bq

    def body(k_ref, v_ref, q_ref, o_ref, do_ref, lse_ref, b_ref, dq_ref, dk_ref, dv_ref):
        kj = pl.program_id(2)
        k = k_ref[...]
        v = v_ref[...]

        @pl.when(kj == 0)
        def _():
            dq_ref[...] = jnp.zeros_like(dq_ref)

        def step(qi, carry):
            dk, dv = carry
            rows = pl.ds(pl.multiple_of(qi * bq, bq), bq)
            q = q_ref[rows, :]
            do = do_ref[rows, :]
            dob = do.astype(BF)
            delta = jnp.sum(do * o_ref[rows, :].astype(F32), axis=1, keepdims=True)
            p = jnp.exp(_dot(q, k, NT) + b_ref[qi - kj] - lse_ref[rows, :])
            dv = dv + _dot(p, dob, TN)
            ds = p * (_dot(dob, v, NT) - delta)
            dq_ref[rows, :] += _dot(ds, k, NN)
            return dk + _dot(ds, q, TN), dv

        z = jnp.zeros((bq, LANES), F32)
        dk, dv = lax.fori_loop(kj, nq, step, (z, z))
        dk_ref[...] = dk
        dv_ref[...] = dv

    whole = pl.BlockSpec((seq, LANES), lambda b, h, i: (b, h))
    tile = pl.BlockSpec((bq, LANES), lambda b, h, i: (b * nq + i, h))
    sh = jax.ShapeDtypeStruct((T, nh * LANES), F32)
    return _call(
        body, "attn_bwd", (nb, nh, nq),
        [pl.BlockSpec((bq, LANES), lambda b, h, i: (b * nq + i, nh + h)),
         pl.BlockSpec((bq, LANES), lambda b, h, i: (b * nq + i, 2 * nh + h)),
         whole, whole, whole, pl.BlockSpec((None, seq, 1), lambda b, h, i: (h, b, 0)),
         pl.BlockSpec((nq, bq, bq), lambda b, h, i: (0, 0, 0))],
        [whole, tile, tile], [sh, sh, sh],
        (qkv, qkv, qkv, cat, dcat, lse, bias), comm=comm)


def _conv_parts(gc, xin, w_ref):
    w = [w_ref[k:k + 1, :] for k in range(3)]
    u = gc * xin
    row = lax.broadcasted_iota(jnp.int32, u.shape, 0)
    u1 = jnp.where(row >= 1, pltpu.roll(u, 1, 0), 0.0)
    u2 = jnp.where(row >= 2, pltpu.roll(u, 2, 0), 0.0)
    return u, u1, u2, w[0] * u2 + w[1] * u1 + w[2] * u, w, row


def _conv_fwd(proj, conv_w, cat, nb, seq, width):
    cw = min(2 * LANES, width)
    nc = width // cw

    def body(gb_ref, gc_ref, x_ref, w_ref, cat_ref, o_ref):
        _, _, _, conv, _, _ = _conv_parts(gc_ref[...], x_ref[...], w_ref)
        o_ref[...] = (gb_ref[...] * conv).astype(BF)

    def sec(k):
        return pl.BlockSpec((seq, cw), lambda b, c: (b, k * nc + c))

    return pl.pallas_call(
        body, grid=(nb, nc),
        in_specs=[sec(3), sec(4), sec(5), pl.BlockSpec((3, cw), lambda b, c: (0, c)),
                  pl.BlockSpec(memory_space=pl.ANY)],
        out_specs=pl.BlockSpec((seq, cw), lambda b, c: (b, nc + c)),
        out_shape=jax.ShapeDtypeStruct(cat.shape, BF), input_output_aliases={4: 0},
        name="conv_fwd", compiler_params=_cparams(2))(proj, proj, proj, conv_w, cat)


def _conv_bwd(proj, conv_w, dcat, nb, seq, width):
    cw = min(2 * LANES, width)
    nc = width // cw
    T = nb * seq

    def body(gb_ref, gc_ref, x_ref, w_ref, d_ref, dgb_ref, dgc_ref, dx_ref, dw_ref):
        gc = gc_ref[...]
        xin = x_ref[...]
        u, u1, u2, conv, w, row = _conv_parts(gc, xin, w_ref)
        dsc = d_ref[...]
        dgb_ref[...] = dsc * conv
        dconv = dsc * gb_ref[...]
        d1 = jnp.where(row < seq - 1, pltpu.roll(dconv, seq - 1, 0), 0.0)
        d2 = jnp.where(row < seq - 2, pltpu.roll(dconv, seq - 2, 0), 0.0)
        du = w[2] * dconv + w[1] * d1 + w[0] * d2
        dgc_ref[...] = du * xin
        dx_ref[...] = du * gc

        @pl.when(pl.program_id(1) == 0)
        def _():
            dw_ref[...] = jnp.zeros_like(dw_ref)

        dw_ref[0:1, :] += jnp.sum(dconv * u2, axis=0, keepdims=True)
        dw_ref[1:2, :] += jnp.sum(dconv * u1, axis=0, keepdims=True)
        dw_ref[2:3, :] += jnp.sum(dconv * u, axis=0, keepdims=True)

    def sec(k):
        return pl.BlockSpec((seq, cw), lambda c, b: (b, k * nc + c))

    out = pl.BlockSpec((seq, cw), lambda c, b: (b, c))
    wsp = pl.BlockSpec((3, cw), lambda c, b: (0, c))
    sh = jax.ShapeDtypeStruct((T, width), F32)
    return pl.pallas_call(
        body, grid=(nc, nb), in_specs=[sec(3), sec(4), sec(5), wsp, sec(1)],
        out_specs=[out, out, out, wsp], out_shape=[sh, sh, sh, jax.ShapeDtypeStruct((3, width), F32)],
        name="conv_bwd", compiler_params=_cparams(2))(proj, proj, proj, conv_w, dcat)


def _assemble_dproj(dq, dk, dv, dgb, dgc, dxin, cosf, sinf, seq):
    T, width = dq.shape
    nh = width // LANES
    bs = min(ROW_TILE, seq)
    nst = seq // bs
    scale = LANES ** -0.5

    def body(dq_ref, dk_ref, dv_ref, dgb_ref, dgc_ref, dx_ref, c_ref, s_ref, o_ref):
        sec = pl.program_id(1)
        c = c_ref[...]
        s = s_ref[...]

        def unrope(ref, mul):
            for h in range(nh):
                cols = slice(h * LANES, (h + 1) * LANES)
                t = ref[:, cols]
                o_ref[:, cols] = ((t * c + pltpu.roll(t * s, LANES // 2, 1)) * mul).astype(BF)

        @pl.when(sec == 0)
        def _():
            unrope(dq_ref, scale)

        @pl.when(sec == 1)
        def _():
            unrope(dk_ref, 1.0)

        for k, ref in ((2, dv_ref), (3, dgb_ref), (4, dgc_ref), (5, dx_ref)):
            @pl.when(sec == k)
            def _(ref=ref):
                o_ref[...] = ref[...].astype(BF)

    blk = pl.BlockSpec((bs, width), lambda r, k: (r, 0))
    tab = pl.BlockSpec((bs, LANES), lambda r, k: (r % nst, 0))
    return pl.pallas_call(
        body, grid=(T // bs, 6), in_specs=[blk] * 6 + [tab, tab],
        out_specs=pl.BlockSpec((bs, width), lambda r, k: (r, k)),
        out_shape=jax.ShapeDtypeStruct((T, 6 * width), BF), name="assemble_dproj",
        compiler_params=_cparams(2))(dq, dk, dv, dgb, dgc, dxin, cosf, sinf)


def _res_mm(name, a, w, h, comm=None):
    T, K = a.shape
    N = w.shape[1]
    bm = min(FFN_ROW_TILE, T)
    bk = min(ROW_TILE, K)
    row = pl.BlockSpec((bm, N), lambda i, j, k: (i, 0))

    def epi(accs, ex, outs):
        outs[0][...] = ex[0][...] + accs[0]

    return _mm(name, (T // bm, 1, K // bk),
               [(a, pl.BlockSpec((bm, bk), lambda i, j, k: (i, k))),
                (w, pl.BlockSpec((bk, N), lambda i, j, k: (k, 0)))],
               [(0, 1, NN, 0)], 1, (bm, N), [(h, row)],
               [(jax.ShapeDtypeStruct((T, N), F32), row)], epi, comm=comm,
               nrow=max(1, bm // ROW_TILE), ncol=max(1, N // COL_TILE))[0]


def _mm_nt(name, a, w, out_dtype):
    T, K = a.shape
    N = w.shape[0]
    bm = min(FFN_ROW_TILE, T)
    bn = min(ROW_TILE, N)

    def epi(accs, ex, outs):
        outs[0][...] = accs[0].astype(out_dtype)

    return _mm(name, (T // bm, N // bn, 1),
               [(a, pl.BlockSpec((bm, K), lambda i, j, k: (i, 0))),
                (w, pl.BlockSpec((bn, K), lambda i, j, k: (j, 0)))],
               [(0, 1, NT, 0)], 1, None, [],
               [(jax.ShapeDtypeStruct((T, N), out_dtype), pl.BlockSpec((bm, bn), lambda i, j, k: (i, j)))],
               epi, nrow=max(1, bm // ROW_TILE))[0]


def _mm_tn(name, a, bs_list):
    T, M = a.shape
    N = bs_list[0].shape[1]
    bmr = min(COL_TILE, M)
    bn = min(COL_TILE, N)
    n = len(bs_list)

    def epi(accs, ex, outs):
        for q in range(n):
            outs[q][...] = accs[q].astype(BF)

    ops = [(a, pl.BlockSpec((T, bmr), lambda r, j, k: (0, r)))]
    ops += [(b, pl.BlockSpec((T, bn), lambda r, j, k: (0, j))) for b in bs_list]
    return _mm(name, (M // bmr, N // bn, 1), ops, [(0, 1 + q, TN, q) for q in range(n)], n, None, [],
               [(jax.ShapeDtypeStruct((M, N), BF), pl.BlockSpec((bmr, bn), lambda r, j, k: (r, j)))] * n, epi)


def _proj_bwd_x(dproj, wing):
    T = dproj.shape[0]
    _, D, ws = wing.shape
    bm = min(FFN_ROW_TILE, T)
    row = pl.BlockSpec((bm, D), lambda i, j, s: (i, 0))

    def epi(accs, ex, outs):
        outs[0][...] = accs[0]

    return _mm("proj_bwd_x", (T // bm, 1, N_DEV),
               [(dproj, pl.BlockSpec((bm, ws), lambda i, j, s: (i, s))),
                (wing, pl.BlockSpec((None, D, ws), lambda i, j, s: (s, 0, 0)))],
               [(0, 1, NT, 0)], 1, (bm, D), [], [(jax.ShapeDtypeStruct((T, D), F32), row)], epi,
               nrow=max(1, bm // ROW_TILE), ncol=max(1, D // COL_TILE))[0]


def _proj_dw(u, dproj, ws):
    T, D = u.shape
    bmr = min(COL_TILE, D)

    def epi(accs, ex, outs):
        outs[0][...] = accs[0].astype(BF)

    return _mm("proj_dw", (N_DEV, D // bmr, 1),
               [(u, pl.BlockSpec((T, bmr), lambda s, r, k: (0, r))),
                (dproj, pl.BlockSpec((T, ws), lambda s, r, k: (0, s)))],
               [(0, 1, TN, 0)], 1, None, [],
               [(jax.ShapeDtypeStruct((N_DEV, D, ws), BF),
                 pl.BlockSpec((None, bmr, ws), lambda s, r, k: (s, r, 0)))], epi)[0]


def _mixer_ab_fwd(h, gain, wing, conv_w, wout, tabs, nb, seq, comm_proj=None, comm_attn=None, comm_out=None):
    cosf, sinf, bias = tabs
    width = wing.shape[-1] * N_DEV // 6
    nh = width // LANES
    u = _rms_fwd(h, gain, BF)
    proj = _proj_fwd(u, wing, comm=comm_proj)
    qkv = _rope_fwd(proj, cosf, sinf, seq, nh)
    cat, lse = _attn_fwd(qkv, bias, nb, seq, nh, comm=comm_attn)
    cat = _conv_fwd(proj, conv_w, cat, nb, seq, width)
    return _res_mm("outproj_fwd", cat, wout, h, comm=comm_out), (h, u, proj, qkv, cat, lse)


def _mixer_ab_bwd(dh, dhb, saved, gain, wing, conv_w, wout, tabs, nb, seq, reduce_start, carry):
    cosf, sinf, bias = tabs
    h, u, proj, qkv, cat, lse = saved
    D = h.shape[1]
    ws = wing.shape[-1]
    width = ws * N_DEV // 6
    nh = width // LANES
    dcat = _mm_nt("outproj_bwd_x", dhb, wout, F32)
    dwout = _mm_tn("outproj_dw", cat, [dhb])[0]
    comm = _merge_comms(reduce_start(["ab_w_out"], [dwout.reshape(N_DEV, -1, D)]) + [carry])
    dq, dk, dv = _attn_bwd(qkv, cat, dcat, lse, bias, nb, seq, nh, comm=comm)
    dgb, dgc, dxin, dconvw = _conv_bwd(proj, conv_w, dcat, nb, seq, width)
    dproj = _assemble_dproj(dq, dk, dv, dgb, dgc, dxin, cosf, sinf, seq)
    du = _proj_bwd_x(dproj, wing)
    comm, = reduce_start(["ab_w_in"], [_proj_dw(u, dproj, ws)])
    dh_in, dhb_in, dgain = _rms_bwd(du, h, gain, dh)
    return dh_in, dhb_in, dgain, dconvw, comm


def _s5_zoh(lr, li, log_dt):
    dt = jnp.exp(log_dt)
    mag = jnp.exp(lr * dt)
    ar = mag * jnp.cos(li * dt)
    ai = mag * jnp.sin(li * dt)
    den = lr * lr + li * li
    return dt, ar, ai, den, ((ar - 1.0) * lr + ai * li) / den, (ai * lr - (ar - 1.0) * li) / den


def _s5_discretize(lam_re, lam_im, log_dt, bt_re, bt_im):
    def body(lr_ref, li_ref, ld_ref, br_ref, bi_ref, ar_ref, ai_ref, bbr_ref, bbi_ref):
        _, ar, ai, _, fr, fi = _s5_zoh(lr_ref[...], li_ref[...], ld_ref[...])
        ar_ref[...] = ar
        ai_ref[...] = ai
        bbr_ref[...] = fr * br_ref[...] - fi * bi_ref[...]
        bbi_ref[...] = fr * bi_ref[...] + fi * br_ref[...]

    small = jax.ShapeDtypeStruct(lam_re.shape, F32)
    big = jax.ShapeDtypeStruct(bt_re.shape, F32)
    return pl.pallas_call(body, out_shape=[small, small, big, big], name="s5_discretize",
                          compiler_params=_cparams(0))(lam_re, lam_im, log_dt, bt_re, bt_im)


def _s5_discretize_bwd(lam_re, lam_im, log_dt, bt_re, bt_im, d_ar, d_ai, d_bbr, d_bbi):

    def body(lr_ref, li_ref, ld_ref, br_ref, bi_ref, dar_ref, dai_ref, dbbr_ref, dbbi_ref,
             dlr_ref, dli_ref, dld_ref, dbr_ref, dbi_ref):
        lr, li = lr_ref[...], li_ref[...]
        dt, ar, ai, den, fr, fi = _s5_zoh(lr, li, ld_ref[...])
        br, bi = br_ref[...], bi_ref[...]
        dbbr, dbbi = dbbr_ref[...], dbbi_ref[...]
        dbr_ref[...] = dbbr * fr + dbbi * fi
        dbi_ref[...] = dbbi * fr - dbbr * fi
        dfr = jnp.sum(dbbr * br + dbbi * bi, axis=1, keepdims=True)
        dfi = jnp.sum(dbbi * br - dbbr * bi, axis=1, keepdims=True)
        dnr = dfr / den
        dni = dfi / den
        dden = -(dfr * fr + dfi * fi) / den
        dar = dar_ref[...] + dnr * lr - dni * li
        dai = dai_ref[...] + dnr * li + dni * lr
        dlr_ref[...] = dnr * (ar - 1.0) + dni * ai + 2.0 * dden * lr + dt * (dar * ar + dai * ai)
        dli_ref[...] = dnr * ai - dni * (ar - 1.0) + 2.0 * dden * li + dt * (dai * ar - dar * ai)
        ddt = jnp.sum(dar * (lr * ar - li * ai) + dai * (lr * ai + li * ar), axis=2, keepdims=True)
        dld_ref[...] = ddt * dt

    small = jax.ShapeDtypeStruct(lam_re.shape, F32)
    big = jax.ShapeDtypeStruct(bt_re.shape, F32)
    return pl.pallas_call(
        body, out_shape=[small, small, jax.ShapeDtypeStruct(log_dt.shape, F32), big, big],
        name="s5_discretize_bwd", compiler_params=_cparams(0))(
            lam_re, lam_im, log_dt, bt_re, bt_im, d_ar, d_ai, d_bbr, d_bbi)


def _rows8(t):
    if isinstance(t, int):
        return pl.ds(t * SUBLANES, SUBLANES)
    return pl.ds(pl.multiple_of(t * SUBLANES, SUBLANES), SUBLANES)


def _cmul_add(ar, ai, sr, si, br, bi):
    return ar * sr - ai * si + br, ar * si + ai * sr + bi


def _steps(n, step, carry):
    head = n % SCAN_UNROLL
    for i in range(head):
        carry = step(i, carry)

    def trip(j, c):
        for q in range(SCAN_UNROLL):
            c = step(head + j * SCAN_UNROLL + q, c)
        return c

    return lax.fori_loop(0, n // SCAN_UNROLL, trip, carry)


def _scan(a, read, write, init, n):
    def step(t, c):
        s = _cmul_add(*a, *c, *read(t))
        if write is not None:
            write(t, s)
        return s

    return _steps(n, step, init)


def _cpow(ar, ai, n):
    rr = ri = None
    while n:
        if n & 1:
            rr, ri = (ar, ai) if rr is None else (rr * ar - ri * ai, rr * ai + ri * ar)
        ar, ai = ar * ar - ai * ai, 2.0 * ar * ai
        n >>= 1
    return rr, ri


def _s5_specs(R, nj):
    sh = STATE_COLS
    return dict(
        rows=pl.BlockSpec((R, LANES), lambda j: (0, j)),
        bd=pl.BlockSpec((None, LANES, sh), lambda j: (j, 0, 0)),
        cd=pl.BlockSpec((None, sh, LANES), lambda j: (j, 0, 0)),
        a=pl.BlockSpec((None, 1, sh), lambda j: (j, 0, 0)),
        vec=pl.BlockSpec((1, LANES), lambda j: (0, j)),
        init=pl.BlockSpec((None, SUBLANES, sh), lambda j: (j, 0, 0)))


def _s5_fwd(u, mats, seg_len, nseg, comm=None):
    bdr, bdi, cdr, cdi, are, aim, dsk = mats
    R, D = u.shape
    nj = D // LANES
    sh = STATE_COLS
    rc = min(R, 512)
    sp = _s5_specs(R, nj)

    def body(u_ref, bdr_ref, bdi_ref, cdr_ref, cdi_ref, ar_ref, ai_ref, d_ref,
             y_ref, yg_ref, ir_ref, ii_ref, sre, sim):
        ar = jnp.broadcast_to(ar_ref[...], (SUBLANES, sh))
        ai = jnp.broadcast_to(ai_ref[...], (SUBLANES, sh))

        def bu_chunk(c, _):
            rows = pl.ds(pl.multiple_of(c * rc, rc), rc)
            ub = u_ref[rows, :].astype(BF)
            sre[rows, :] = _dot(ub, bdr_ref[...], NN)
            sim[rows, :] = _dot(ub, bdi_ref[...], NN)
            return 0

        lax.fori_loop(0, R // rc, bu_chunk, 0)
        z = jnp.zeros((SUBLANES, sh), F32)

        def read(t):
            return sre[_rows8(t), :], sim[_rows8(t), :]

        def write(t, s):
            sre[_rows8(t), :] = s[0]
            sim[_rows8(t), :] = s[1]

        er, ei = _scan((ar, ai), read, None, (z, z), seg_len)
        pr, pi = _cpow(ar, ai, seg_len)
        first = (lax.broadcasted_iota(jnp.int32, (SUBLANES, sh), 0) & (nseg - 1)) == 0

        def prev(x):
            return jnp.where(first, 0.0, pltpu.roll(x, 1, 0))

        xr, xi = er, ei
        for _ in range(nseg - 1):
            xr, xi = _cmul_add(pr, pi, prev(xr), prev(xi), er, ei)
        i_r, i_i = prev(xr), prev(xi)
        ir_ref[...] = i_r
        ii_ref[...] = i_i
        _scan((ar, ai), read, write, (i_r, i_i), seg_len)

        def y_chunk(c, _):
            rows = pl.ds(pl.multiple_of(c * rc, rc), rc)
            y = _dot(sre[rows, :], cdr_ref[...], NN) + _dot(sim[rows, :], cdi_ref[...], NN)
            y = y + d_ref[...] * u_ref[rows, :]
            y_ref[rows, :] = y
            yg_ref[rows, :] = _gelu(y).astype(BF)
            return 0

        lax.fori_loop(0, R // rc, y_chunk, 0)

    init_sh = jax.ShapeDtypeStruct((nj, SUBLANES, STATE_COLS), F32)
    return _call(
        body, "s5_fwd", (nj,),
        [sp["rows"], sp["bd"], sp["bd"], sp["cd"], sp["cd"], sp["a"], sp["a"], sp["vec"]],
        [sp["rows"], sp["rows"], sp["init"], sp["init"]],
        [jax.ShapeDtypeStruct((R, D), F32), jax.ShapeDtypeStruct((R, D), BF), init_sh, init_sh],
        (u, bdr, bdi, cdr, cdi, are, aim, dsk),
        scratch=[pltpu.VMEM((R, sh), F32) for _ in range(2)], comm=comm)


def _s5_bwd(u, dy, mats, init_re, init_im, seg_len, nseg, comm=None):
    bdr, bdi, cdr, cdi, are, aim, dsk = mats
    R, D = u.shape
    nj = D // LANES
    sh = STATE_COLS
    rc = min(R, 512)
    sp = _s5_specs(R, nj)

    def body(u_ref, dy_ref, bdr_ref, bdi_ref, cdr_ref, cdi_ref, ar_ref, ai_ref, d_ref, ir_ref, ii_ref,
             du_ref, dbdr_ref, dbdi_ref, dcdr_ref, dcdi_ref, dar_ref, dai_ref, dd_ref,
             sre, sim, gre, gim):
        ar = jnp.broadcast_to(ar_ref[...], (SUBLANES, sh))
        ai = jnp.broadcast_to(ai_ref[...], (SUBLANES, sh))
        i_r, i_i = ir_ref[...], ii_ref[...]

        def chunk(c):
            return pl.ds(pl.multiple_of(c * rc, rc), rc)

        def bu_chunk(c, _):
            ub = u_ref[chunk(c), :].astype(BF)
            dyb = dy_ref[chunk(c), :].astype(BF)
            sre[chunk(c), :] = _dot(ub, bdr_ref[...], NN)
            sim[chunk(c), :] = _dot(ub, bdi_ref[...], NN)
            gre[chunk(c), :] = _dot(dyb, cdr_ref[...], NT)
            gim[chunk(c), :] = _dot(dyb, cdi_ref[...], NT)
            return 0

        lax.fori_loop(0, R // rc, bu_chunk, 0)

        def read_s(t):
            return sre[_rows8(t), :], sim[_rows8(t), :]

        def read_g(t):
            return gre[_rows8(t), :], gim[_rows8(t), :]

        def both(i, c):
            s = _cmul_add(ar, ai, c[0], c[1], *read_s(i))
            sre[_rows8(i), :], sim[_rows8(i), :] = s
            return (*s, *_cmul_add(ar, -ai, c[2], c[3], *read_g(seg_len - 1 - i)))

        z = jnp.zeros((SUBLANES, sh), F32)
        _, _, fr, fi = _steps(seg_len, both, (i_r, i_i, z, z))

        def c_chunk(c, carry):
            dyb = dy_ref[chunk(c), :].astype(BF)
            return (carry[0] + _dot(sre[chunk(c), :], dyb, TN), carry[1] + _dot(sim[chunk(c), :], dyb, TN))

        zc = jnp.zeros((sh, LANES), F32)
        dcr, dci = lax.fori_loop(0, R // rc, c_chunk, (zc, zc))
        dcdr_ref[...] = dcr
        dcdi_ref[...] = dci
        pr, pi = _cpow(ar, ai, seg_len)
        last =(lax.broadcasted_iota(jnp.int32, (SUBLANES, sh), 0) & (nseg - 1)) == nseg - 1

        def nxt(x):
            return jnp.where(last, 0.0, pltpu.roll(x, SUBLANES - 1, 0))

        xr, xi = fr, fi
        for _ in range(nseg - 1):
            xr, xi = _cmul_add(pr, -pi, nxt(xr), nxt(xi), fr, fi)
        g0r, g0i = nxt(xr), nxt(xi)

        def adj_step(t, c, s_before):
            gr, gi = _cmul_add(ar, -ai, c[0], c[1], *read_g(t))
            gre[_rows8(t), :], gim[_rows8(t), :] = gr, gi
            spr, spi = s_before
            return gr, gi, c[2] + spr * gr + spi * gi, c[3] + spr * gi - spi * gr

        carry = _steps(seg_len - 1, lambda i, c: adj_step(seg_len - 1 - i, c, read_s(seg_len - 2 - i)),
                       (g0r, g0i, z, z))
        carry = adj_step(0, carry, (i_r, i_i))
        dar_ref[...] = jnp.sum(carry[2], axis=0, keepdims=True)
        dai_ref[...] = jnp.sum(carry[3], axis=0, keepdims=True)

        def d_chunk(c, carry):
            ub = u_ref[chunk(c), :].astype(BF)
            grb = gre[chunk(c), :].astype(BF)
            gib = gim[chunk(c), :].astype(BF)
            du = _dot(grb, bdr_ref[...], NT) + _dot(gib, bdi_ref[...], NT)
            du_ref[chunk(c), :] = du + d_ref[...] * dy_ref[chunk(c), :]
            dd = carry[2] + jnp.sum(dy_ref[chunk(c), :] * u_ref[chunk(c), :], axis=0, keepdims=True)
            return carry[0] + _dot(ub, grb, TN), carry[1] + _dot(ub, gib, TN), dd

        zb = jnp.zeros((LANES, sh), F32)
        dbr, dbi, dd = lax.fori_loop(0, R // rc, d_chunk, (zb, zb, jnp.zeros((1, LANES), F32)))
        dbdr_ref[...] = dbr
        dbdi_ref[...] = dbi
        dd_ref[...] = dd

    bd_sh = jax.ShapeDtypeStruct((nj, LANES, STATE_COLS), F32)
    cd_sh = jax.ShapeDtypeStruct((nj, STATE_COLS, LANES), F32)
    a_sh = jax.ShapeDtypeStruct((nj, 1, STATE_COLS), F32)
    return _call(
        body, "s5_bwd", (nj,),
        [sp["rows"], sp["rows"], sp["bd"], sp["bd"], sp["cd"], sp["cd"], sp["a"], sp["a"],
         sp["vec"], sp["init"], sp["init"]],
        [sp["rows"], sp["bd"], sp["bd"], sp["cd"], sp["cd"], sp["a"], sp["a"], sp["vec"]],
        [jax.ShapeDtypeStruct((R, D), F32), bd_sh, bd_sh, cd_sh, cd_sh, a_sh, a_sh,
         jax.ShapeDtypeStruct((1, D), F32)],
        (u, dy, bdr, bdi, cdr, cdi, are, aim, dsk, init_re, init_im),
        scratch=[pltpu.VMEM((R, sh), F32) for _ in range(4)], comm=comm)


def _glu_fwd(yg, wa, wb, h):
    T, D = yg.shape
    N = wa.shape[1]
    bm = min(FFN_ROW_TILE, T)
    bn = min(ROW_TILE, N)
    wspec = pl.BlockSpec((D, bn), lambda i, j, k: (0, j))
    ospec = pl.BlockSpec((bm, bn), lambda i, j, k: (i, j))

    def epi(accs, ex, outs):
        pa, pb = accs
        outs[0][...] = ex[0][...] + pa * _sig(pb)
        outs[1][...] = pa.astype(BF)
        outs[2][...] = pb.astype(BF)

    return _mm("glu_fwd", (T // bm, N // bn, 1),
               [(yg, pl.BlockSpec((bm, D), lambda i, j, k: (i, 0))), (wa, wspec), (wb, wspec)],
               [(0, 1, NN, 0), (0, 2, NN, 1)], 2, None, [(h, ospec)],
               [(jax.ShapeDtypeStruct((T, N), F32), ospec), (jax.ShapeDtypeStruct((T, N), BF), ospec),
                (jax.ShapeDtypeStruct((T, N), BF), ospec)], epi, nrow=max(1, bm // ROW_TILE))


def _glu_bwd_gates(dz, pa, pb):
    T, D = dz.shape
    bm = min(ROW_TILE, T)

    def body(dz_ref, pa_ref, pb_ref, dpa_ref, dpb_ref):
        dz = dz_ref[...]
        sg = _sig(pb_ref[...].astype(F32))
        dpa_ref[...] = (dz * sg).astype(BF)
        dpb_ref[...] = (dz * pa_ref[...].astype(F32) * sg * (1.0 - sg)).astype(BF)

    row = pl.BlockSpec((bm, D), lambda i: (i, 0))
    return pl.pallas_call(
        body, grid=(T // bm,), in_specs=[row] * 3, out_specs=[row] * 2,
        out_shape=[jax.ShapeDtypeStruct((T, D), BF)] * 2, name="glu_bwd_gates",
        compiler_params=_cparams(1))(dz, pa, pb)


def _glu_bwd_y(dpa, dpb, wa, wb, y_pre, comm=None):
    T, N = dpa.shape
    D = wa.shape[0]
    bm = min(FFN_ROW_TILE, T)
    bn = min(ROW_TILE, D)
    aspec = pl.BlockSpec((bm, N), lambda i, j, k: (i, 0))
    wspec = pl.BlockSpec((bn, N), lambda i, j, k: (j, 0))
    ospec = pl.BlockSpec((bm, bn), lambda i, j, k: (i, j))

    def epi(accs, ex, outs):
        outs[0][...] = accs[0] * _gelu_grad(ex[0][...])

    return _mm("glu_bwd_y", (T // bm, D // bn, 1), [(dpa, aspec), (wa, wspec), (dpb, aspec), (wb, wspec)],
               [(0, 1, NT, 0), (2, 3, NT, 0)], 1, None, [(y_pre, ospec)],
               [(jax.ShapeDtypeStruct((T, D), F32), ospec)], epi, comm=comm, nrow=max(1, bm // ROW_TILE))[0]


def _block_diag_in(x, nj):
    g = GROUPS_PER_BLOCK
    x = x.reshape(nj, g, 1, S5_GROUP, S5_STATE)
    eye = jnp.eye(g, dtype=bool)[None, :, :, None, None]
    full = jnp.where(eye, x, 0.0)
    return full.transpose(0, 1, 3, 2, 4).reshape(nj, g * S5_GROUP, g * S5_STATE)


def _block_diag_out(x, nj):
    return _block_diag_in(x, nj).transpose(0, 2, 1)


def _diag_of_in(m, nj):
    g = GROUPS_PER_BLOCK
    m5 = m.reshape(nj, g, S5_GROUP, g, S5_STATE)
    d = jnp.diagonal(m5, axis1=1, axis2=3)
    return d.transpose(0, 3, 1, 2).reshape(nj * g, S5_GROUP, S5_STATE)


def _mixer_s5_fwd(h, gain, p, dsk, wa, wb, nb, seq, comm_s5=None):
    T, D = h.shape
    nj = D // LANES
    nseg = SUBLANES // nb
    seg_len = seq // nseg
    G = p["s5_lambda_re"].shape[1]
    lam_re = p["s5_lambda_re"].reshape(G, 1, S5_STATE)
    lam_im = p["s5_lambda_im"].reshape(G, 1, S5_STATE)
    log_dt = p["s5_log_dt"].reshape(G, 1, 1)
    bt_re = p["s5_b_re"][0].transpose(0, 2, 1)
    bt_im = p["s5_b_im"][0].transpose(0, 2, 1)
    ar, ai, bbr, bbi = _s5_discretize(lam_re, lam_im, log_dt, bt_re, bt_im)
    mats = (_block_diag_in(bbr, nj).astype(BF), _block_diag_in(bbi, nj).astype(BF),
            _block_diag_out(p["s5_c_re"][0], nj).astype(BF),
            _block_diag_out(-p["s5_c_im"][0], nj).astype(BF),
            ar.reshape(nj, 1, STATE_COLS), ai.reshape(nj, 1, STATE_COLS), dsk)
    h_seg = _to_seg(h, seg_len)
    u = _rms_fwd(h_seg, gain, F32)
    y_pre, yg, init_re, init_im = _s5_fwd(u, mats, seg_len, nseg, comm=comm_s5)
    h_out, pa, pb = _glu_fwd(yg, wa, wb, h_seg)
    disc_in = (lam_re, lam_im, log_dt, bt_re, bt_im)
    return _to_tok(h_out, seg_len), (h_seg, u, mats, y_pre, yg, init_re, init_im, pa, pb, disc_in, seg_len, nseg)


def _mixer_s5_bwd(dh, saved, gain, wa, wb, reduce_start, carry):
    h_seg, u, mats, y_pre, yg, init_re, init_im, pa, pb, disc_in, seg_len, nseg = saved
    T, D = h_seg.shape
    nj = D // LANES
    G = nj * GROUPS_PER_BLOCK
    dh_seg = _to_seg(dh, seg_len)
    dpa, dpb = _glu_bwd_gates(dh_seg, pa, pb)
    dy = _glu_bwd_y(dpa, dpb, wa, wb, y_pre)
    dwa, dwb = _mm_tn("glu_dw", yg, [dpa, dpb])
    comm = _merge_comms(reduce_start(["s5_glu_wa", "s5_glu_wb"],
                                     [dwa.reshape(N_DEV, -1, D), dwb.reshape(N_DEV, -1, D)]) + [carry])
    du, dbdr, dbdi, dcdr, dcdi, dar, dai, dd = _s5_bwd(u, dy, mats, init_re, init_im, seg_len, nseg, comm=comm)
    d_bbr = _diag_of_in(dbdr, nj)
    d_bbi = _diag_of_in(dbdi, nj)
    d_c_re = _diag_of_in(dcdr.transpose(0, 2, 1), nj)
    d_c_im = -_diag_of_in(dcdi.transpose(0, 2, 1), nj)
    dlr, dli, dld, dbr, dbi = _s5_discretize_bwd(
        *disc_in, dar.reshape(G, 1, S5_STATE), dai.reshape(G, 1, S5_STATE), d_bbr, d_bbi)
    small = {"s5_lambda_re": dlr.reshape(1, G, S5_STATE), "s5_lambda_im": dli.reshape(1, G, S5_STATE),
             "s5_log_dt": dld.reshape(1, G),
             "s5_b_re": dbr.transpose(0, 2, 1)[None], "s5_b_im": dbi.transpose(0, 2, 1)[None],
             "s5_c_re": d_c_re[None], "s5_c_im": d_c_im[None], "s5_d": dd}
    dh_in, _, dgain = _rms_bwd(du, h_seg, gain, dh_seg)
    dh_in = _to_tok(dh_in, seg_len)
    return dh_in, dh_in.astype(BF), dgain, small


def _mesh_pos():
    return lax.axis_index("x"), lax.axis_index("y"), lax.axis_index("c")


class _Gather:
    def __init__(self, srcs, slots, send_sems, recv_sems):
        self.srcs, self.slots, self.send_sems, self.recv_sems = srcs, slots, send_sems, recv_sems
        x, y, c = _mesh_pos()
        self.c = c
        self.me, self.sib = (x, y, c), (x, y, 1 - c)
        self.chips = [(1 - x, y), (x, 1 - y), (1 - x, 1 - y)]

    def copy(self, a, k, block, to, own=False):
        dst = self.slots[a].at[4 * block[0] + 2 * block[1] + block[2]]
        return pltpu.make_async_remote_copy(
            src_ref=self.srcs[a] if own else dst, dst_ref=dst, send_sem=self.send_sems.at[7 * a + k],
            recv_sem=self.recv_sems.at[7 * a + k], device_id=to, device_id_type=MESH)

    def own_copies(self, a):
        cps = [self.copy(a, 0, self.me, self.sib, own=True)]
        return cps + [self.copy(a, 1 + j, self.me, (*chip, self.c), own=True) for j, chip in enumerate(self.chips)]

    def start(self):
        for a in range(len(self.srcs)):
            for cp in self.own_copies(a):
                cp.start()

    def finish(self):
        n = len(self.srcs)
        for a in range(n):
            for j, chip in enumerate(self.chips):
                self.copy(a, 1 + j, (*chip, self.c), self.me).wait_recv()
                self.copy(a, 4 + j, (*chip, self.c), self.sib).start()
        for a in range(n):
            self.copy(a, 0, self.sib, self.me).wait_recv()
            for j, chip in enumerate(self.chips):
                self.copy(a, 4 + j, (*chip, 1 - self.c), self.me).wait_recv()
        for a in range(n):
            for cp in self.own_copies(a):
                cp.wait_send()
            for j, chip in enumerate(self.chips):
                self.copy(a, 4 + j, (*chip, self.c), self.sib).wait_send()


def _gather_comm(arrs):
    n = len(arrs)

    def local(xs, outs, sems, a):
        x, y, c = _mesh_pos()
        return pltpu.make_async_copy(xs[a], outs[a].at[4 * x + 2 * y + c], sems[2].at[a])

    def start(xs, outs, sems):
        for a in range(n):
            local(xs, outs, sems, a).start()
        _Gather(xs, outs, sems[0], sems[1]).start()

    def finish(xs, outs, sems):
        _Gather(xs, outs, sems[0], sems[1]).finish()
        for a in range(n):
            local(xs, outs, sems, a).wait()

    return _Comm(list(arrs), [jax.ShapeDtypeStruct((N_DEV,) + a.shape, a.dtype) for a in arrs],
                 [pltpu.SemaphoreType.DMA((7 * n,)), pltpu.SemaphoreType.DMA((7 * n,)),
                  pltpu.SemaphoreType.DMA((n,))], start, finish)


def _exchange_comm(parts):
    n = len(parts)

    def copies(ps, outs, sems):
        x, y, c = _mesh_pos()
        cps = []
        for a in range(n):
            for j in range(1, 4):
                to = (jnp.bitwise_xor(x, j // 2), jnp.bitwise_xor(y, j % 2), c)
                cps.append(pltpu.make_async_remote_copy(
                    src_ref=ps[a].at[j], dst_ref=outs[a].at[j - 1], send_sem=sems[0].at[3 * a + j - 1],
                    recv_sem=sems[1].at[3 * a + j - 1], device_id=to, device_id_type=MESH))
        return cps

    def start(ps, outs, sems):
        for cp in copies(ps, outs, sems):
            cp.start()

    def finish(ps, outs, sems):
        for cp in copies(ps, outs, sems):
            cp.wait()

    return _Comm(list(parts), [jax.ShapeDtypeStruct((3,) + p.shape[1:], p.dtype) for p in parts],
                 [pltpu.SemaphoreType.DMA((3 * n,)), pltpu.SemaphoreType.DMA((3 * n,))], start, finish)


def _run_comm(comm, name):
    ci, co = len(comm.ins), len(comm.outs)

    def body(*refs):
        comm.start(refs[:ci], refs[ci:ci + co], refs[ci + co:])
        comm.finish(refs[:ci], refs[ci:ci + co], refs[ci + co:])

    any_spec = pl.BlockSpec(memory_space=pl.ANY)
    comm.set_results(pl.pallas_call(
        body, in_specs=[any_spec] * ci, out_specs=[any_spec] * co, out_shape=list(comm.outs),
        scratch_shapes=list(comm.sems), name=name, compiler_params=_cparams(0))(*comm.ins))


def _pair_exchange(grads, name):
    n = len(grads)

    def body(*refs):
        gs, outs = refs[:n], refs[n:2 * n]
        send_sems, recv_sems = refs[2 * n:]
        x, y, c = _mesh_pos()
        copies = []
        for a in range(n):
            for k in range(4):
                copies.append(pltpu.make_async_remote_copy(
                    src_ref=gs[a].at[2 * k + 1 - c], dst_ref=outs[a].at[k], send_sem=send_sems.at[4 * a + k],
                    recv_sem=recv_sems.at[4 * a + k], device_id=(x, y, 1 - c), device_id_type=MESH))
        for cp in copies:
            cp.start()
        for cp in copies:
            cp.wait()

    any_spec = pl.BlockSpec(memory_space=pl.ANY)
    return pl.pallas_call(
        body, in_specs=[any_spec] * n, out_specs=[any_spec] * n,
        out_shape=[jax.ShapeDtypeStruct((4,) + g.shape[1:], g.dtype) for g in grads],
        scratch_shapes=[pltpu.SemaphoreType.DMA((4 * n,)), pltpu.SemaphoreType.DMA((4 * n,))],
        name=name, compiler_params=_cparams(0))(*grads)


def _pair_sum(grad, recv, pos):
    _, R, C = grad.shape
    br = _row_block(R, C, PAIR_SUM_ELEMS)

    def body(pos_ref, g_ref, r_ref, o_ref):
        o_ref[...] = (g_ref[...].astype(F32) + r_ref[...].astype(F32)).astype(BF)

    def chip(j, p):
        return jnp.bitwise_xor(p[1], j)

    return pl.pallas_call(
        body, grid_spec=pltpu.PrefetchScalarGridSpec(
            num_scalar_prefetch=1, grid=(4, R // br),
            in_specs=[pl.BlockSpec((None, br, C), lambda j, i, p: (2 * chip(j, p) + p[0], i, 0)),
                      pl.BlockSpec((None, br, C), lambda j, i, p: (chip(j, p), i, 0))],
            out_specs=pl.BlockSpec((None, br, C), lambda j, i, p: (j, i, 0))),
        out_shape=jax.ShapeDtypeStruct((4, R, C), BF), name="pair_sum", compiler_params=_cparams(2))(pos, grad, recv)


def _adamw(w, g, m, v):
    m = ADAM_B1 * m + (1.0 - ADAM_B1) * g
    v = ADAM_B2 * v + (1.0 - ADAM_B2) * (g * g)
    m_hat = m / (1.0 - ADAM_B1 ** ADAM_STEP)
    v_hat = v / (1.0 - ADAM_B2 ** ADAM_STEP)
    return -ADAM_LR * (m_hat / (jnp.sqrt(v_hat) + ADAM_EPS) + ADAM_WD * w), m, v


def _adamw_piece(w, m, v, piece, part, recv, bufs):
    _, R, C = w.shape
    br = _row_block(R, C)

    def body(w_ref, m_ref, v_ref, p_ref, r_ref, b0, b1, b2, b3, g_ref, d_ref, nm_ref, nv_ref):
        g = p_ref[...].astype(F32)
        for j in range(3):
            g = g + r_ref[j].astype(F32)
        d, nm, nv = _adamw(w_ref[...], g, m_ref[...], v_ref[...])
        g_ref[...] = g
        d_ref[...] = d
        nm_ref[...] = nm
        nv_ref[...] = nv

    row = pl.BlockSpec((None, br, C), lambda i: (piece, i, 0))
    any_spec = pl.BlockSpec(memory_space=pl.ANY)
    return pl.pallas_call(
        body, grid=(R // br,),
        in_specs=[row, row, row, pl.BlockSpec((None, br, C), lambda i: (0, i, 0)),
                  pl.BlockSpec((3, br, C), lambda i: (0, i, 0))] + [any_spec] * 4,
        out_specs=[row] * 4, out_shape=[jax.ShapeDtypeStruct(w.shape, F32)] * 4,
        input_output_aliases={5: 0, 6: 1, 7: 2, 8: 3}, name="adamw_piece",
        compiler_params=_cparams(1))(w, m, v, part, recv, *bufs)


def _all_reduce_small(x):
    rows = x.shape[0]

    def body(x_ref, o_ref, buf, send_sems, recv_sems):
        xp, yp, cp = _mesh_pos()
        buf[4 * xp + 2 * yp + cp] = x_ref[...]
        gather = _Gather([x_ref], [buf], send_sems, recv_sems)
        gather.start()
        gather.finish()
        acc = buf[0]
        for d in range(1, N_DEV):
            acc = acc + buf[d]
        o_ref[...] = acc

    vm = pl.BlockSpec(memory_space=pltpu.VMEM)
    return pl.pallas_call(
        body, in_specs=[vm], out_specs=vm, out_shape=jax.ShapeDtypeStruct(x.shape, F32),
        scratch_shapes=[pltpu.VMEM((N_DEV, rows, LANES), F32), pltpu.SemaphoreType.DMA((7,)),
                        pltpu.SemaphoreType.DMA((7,))],
        name="all_reduce_small", compiler_params=_cparams(0))(x)


def _sum_slots(x):
    def body(x_ref, o_ref):
        acc = x_ref[0]
        for d in range(1, N_DEV):
            acc = acc + x_ref[d]
        o_ref[...] = acc

    return pl.pallas_call(body, out_shape=jax.ShapeDtypeStruct(x.shape[1:], F32), name="sum_slots",
                          compiler_params=_cparams(0))(x)


def _adamw_small(w, g, m, v):
    def body(w_ref, g_ref, m_ref, v_ref, d_ref, nm_ref, nv_ref):
        d, nm, nv = _adamw(w_ref[...], g_ref[...], m_ref[...], v_ref[...])
        d_ref[...] = d
        nm_ref[...] = nm
        nv_ref[...] = nv

    sh = jax.ShapeDtypeStruct(w.shape, F32)
    return pl.pallas_call(body, out_shape=[sh] * 3, name="adamw_small", compiler_params=_cparams(0))(w, g, m, v)


def _pack(arrs):
    flat = jnp.concatenate([a.reshape(-1).astype(F32) for a in arrs])
    rows = -(-flat.shape[0] // (SUBLANES * LANES)) * SUBLANES
    return jnp.pad(flat, (0, rows * LANES - flat.shape[0])).reshape(rows, LANES)


def _unpack(buf, shapes):
    flat = buf.reshape(-1)
    out, off = [], 0
    for s in shapes:
        n = 1
        for d in s:
            n *= d
        out.append(flat[off:off + n].reshape(s))
        off += n
    return out


BIG = ("ffn_w1", "ffn_w3", "ffn_w2", "ab_w_in", "ab_w_out", "s5_glu_wa", "s5_glu_wb")
NAMES = ("ln_ffn_pre", "ln_mix", "ln_ffn_post", "ln_final", "ffn_w1", "ffn_w3", "ffn_w2", "ab_w_in",
         "ab_conv_w", "ab_w_out", "s5_lambda_re", "s5_lambda_im", "s5_log_dt", "s5_b_re", "s5_b_im",
         "s5_c_re", "s5_c_im", "s5_d", "s5_glu_wa", "s5_glu_wb")


def kernel(x, ln_ffn_pre, ln_mix, ln_ffn_post, ln_final, ffn_w1, ffn_w3, ffn_w2, ab_w_in, ab_conv_w, ab_w_out, s5_lambda_re, s5_lambda_im, s5_log_dt, s5_b_re, s5_b_im, s5_c_re, s5_c_im, s5_d, s5_glu_wa, s5_glu_wb, loss_target, m_ln_ffn_pre, m_ln_mix, m_ln_ffn_post, m_ln_final, m_ffn_w1, m_ffn_w3, m_ffn_w2, m_ab_w_in, m_ab_conv_w, m_ab_w_out, m_s5_lambda_re, m_s5_lambda_im, m_s5_log_dt, m_s5_b_re, m_s5_b_im, m_s5_c_re, m_s5_c_im, m_s5_d, m_s5_glu_wa, m_s5_glu_wb, v_ln_ffn_pre, v_ln_mix, v_ln_ffn_post, v_ln_final, v_ffn_w1, v_ffn_w3, v_ffn_w2, v_ab_w_in, v_ab_conv_w, v_ab_w_out, v_s5_lambda_re, v_s5_lambda_im, v_s5_log_dt, v_s5_b_re, v_s5_b_im, v_s5_c_re, v_s5_c_im, v_s5_d, v_s5_glu_wa, v_s5_glu_wb):
    w = dict(zip(NAMES, (ln_ffn_pre, ln_mix, ln_ffn_post, ln_final, ffn_w1, ffn_w3, ffn_w2, ab_w_in, ab_conv_w,
                         ab_w_out, s5_lambda_re, s5_lambda_im, s5_log_dt, s5_b_re, s5_b_im, s5_c_re, s5_c_im,
                         s5_d, s5_glu_wa, s5_glu_wb)))
    mom = dict(zip(NAMES, (m_ln_ffn_pre, m_ln_mix, m_ln_ffn_post, m_ln_final, m_ffn_w1, m_ffn_w3, m_ffn_w2,
                           m_ab_w_in, m_ab_conv_w, m_ab_w_out, m_s5_lambda_re, m_s5_lambda_im, m_s5_log_dt,
                           m_s5_b_re, m_s5_b_im, m_s5_c_re, m_s5_c_im, m_s5_d, m_s5_glu_wa, m_s5_glu_wb)))
    var = dict(zip(NAMES, (v_ln_ffn_pre, v_ln_mix, v_ln_ffn_post, v_ln_final, v_ffn_w1, v_ffn_w3, v_ffn_w2,
                           v_ab_w_in, v_ab_conv_w, v_ab_w_out, v_s5_lambda_re, v_s5_lambda_im, v_s5_log_dt,
                           v_s5_b_re, v_s5_b_im, v_s5_c_re, v_s5_c_im, v_s5_d, v_s5_glu_wa, v_s5_glu_wb)))
    nb, seq, D = x.shape
    T = nb * seq
    assert ln_mix.shape[0] == 2 and ab_w_in.shape[0] == 1 and s5_glu_wa.shape[0] == 1
    xc, yc, cc = _mesh_pos()
    dev = 4 * xc + 2 * yc + cc
    pos = jnp.stack([cc, 2 * xc + yc]).astype(jnp.int32)
    bq = min(ATTN_TILE, seq)
    tabs =_rope_tables(seq) + (_branch_bias(seq // bq, bq),)

    def ffn_piece(k, li, fj):
        return w[k][li, fj].astype(BF)

    g0 = _gather_comm([ffn_piece("ffn_w1", 0, 0), ffn_piece("ffn_w3", 0, 0), ab_conv_w[0], s5_d])
    _run_comm(g0, "gather_first")
    w1, w3 = {(0, 0): g0.results[0]}, {(0, 0): g0.results[1]}
    w2 = {}
    conv_w = g0.results[2].transpose(1, 0, 2).reshape(3, -1)
    dsk = g0.results[3].reshape(1, D)
    gains = {k: [w[k][i:i + 1] for i in range(2)] for k in ("ln_ffn_pre", "ln_mix", "ln_ffn_post")}

    h = x.reshape(T, D)
    saved = {}

    def ffn_fwd(h, gain, key, tag, comm_up, comm_down, after_up):
        n = _rms_fwd(h, gain, BF)
        t1, t3, g = _ffn_up(n, w1[key], w3[key], comm=comm_up)
        after_up()
        saved[tag] = (h, n, t1, t3, g)
        return _ffn_down(g, w2[key], h, comm=comm_down)

    c_up = _gather_comm([ffn_piece("ffn_w2", 0, 0), ab_w_out[0].astype(BF)])
    c_dn = _gather_comm([ab_w_in[0].astype(BF)])
    h = ffn_fwd(h, gains["ln_ffn_pre"][0], (0, 0), "pre0", c_up, c_dn,
                lambda: w2.update({(0, 0): c_up.results[0]}))
    wout = c_up.results[1].reshape(-1, D)
    wing = c_dn.results[0]
    c_proj = _gather_comm([ffn_piece("ffn_w1", 0, 1)])
    c_attn = _gather_comm([ffn_piece("ffn_w3", 0, 1), s5_glu_wa[0].astype(BF)])
    c_out = _gather_comm([s5_glu_wb[0].astype(BF)])
    h, saved["mix0"] = _mixer_ab_fwd(h, gains["ln_mix"][0], wing, conv_w, wout, tabs, nb, seq, c_proj, c_attn, c_out)
    w1[(0, 1)] = c_proj.results[0]
    w3[(0, 1)] = c_attn.results[0]
    wa = c_attn.results[1].reshape(-1, D)
    wb = c_out.results[0].reshape(-1, D)
    c_up2 = _gather_comm([ffn_piece("ffn_w2", 0, 1), ffn_piece("ffn_w1", 1, 0)])
    c_dn = _gather_comm([ffn_piece("ffn_w3", 1, 0)])
    h = ffn_fwd(h, gains["ln_ffn_post"][0], (0, 1), "post0", c_up2, c_dn,
                lambda: w2.update({(0, 1): c_up2.results[0]}))
    w1[(1, 0)] = c_up2.results[1]
    w3[(1, 0)] = c_dn.results[0]
    c_up3 = _gather_comm([ffn_piece("ffn_w2", 1, 0), ffn_piece("ffn_w1", 1, 1)])
    c_dn = _gather_comm([ffn_piece("ffn_w3", 1, 1)])
    h = ffn_fwd(h, gains["ln_ffn_pre"][1], (1, 0), "pre1", c_up3, c_dn,
                lambda: w2.update({(1, 0): c_up3.results[0]}))
    w1[(1, 1)] = c_up3.results[1]
    w3[(1, 1)] = c_dn.results[0]
    c_s5 = _gather_comm([ffn_piece("ffn_w2", 1, 1)])
    h, saved["mix1"] = _mixer_s5_fwd(h, gains["ln_mix"][1], w, dsk, wa, wb, nb, seq, c_s5)
    w2[(1, 1)] = c_s5.results[0]
    h = ffn_fwd(h, gains["ln_ffn_post"][1], (1, 1), "post1", None, None, lambda: None)
    dh, dhb, d_ln_final, loss_part = _loss_head(h, ln_final.reshape(1, D), loss_target.reshape(T, D))
    loss = lax.psum(loss_part[0, 0], ("x", "y", "c"))

    reduced = {}

    def reduce_start(names, grads):
        recv = _pair_exchange(grads, "pair_exchange")
        comms = []
        for nm, g, r in zip(names, grads, recv):
            part = _pair_sum(g, r, pos)
            comms.append(_exchange_comm([part]))
            reduced[nm] = (part, comms[-1])
        return comms

    def ffn_bwd(dh, dhb, key, tag, gain, carry, is_last=False, comm_dw2=None):
        h_in, n, t1, t3, g = saved[tag]
        da1, da3 = _ffn_bwd_hidden(dhb, w2[key], t1, t3, comm=carry)
        c2, = reduce_start([("ffn_w2",) + key], [_ffn_dw2(g, dhb, comm=comm_dw2)])
        dw1, dw3 = _ffn_dw13(n, da1, da3, comm=c2)
        c1, c3 = reduce_start([("ffn_w1",) + key, ("ffn_w3",) + key], [dw1, dw3])
        res = _ffn_dn_rms(da1, da3, w1[key], w3[key], h_in, gain, dh,
                          comm=_merge_comms([c1, c3]) if is_last else c1)
        return list(res) + [None if is_last else c3]

    g_small = {"ln_final": d_ln_final.reshape(D)}
    g_ln = {k: [None, None] for k in gains}
    dh, dhb, g_ln["ln_ffn_post"][1], carry = ffn_bwd(dh, dhb, (1, 1), "post1", gains["ln_ffn_post"][1], None)
    dh, dhb, g_ln["ln_mix"][1], s5_small = _mixer_s5_bwd(
        dh, saved["mix1"], gains["ln_mix"][1], wa, wb, reduce_start, carry)
    s5_names = list(s5_small)
    c_s5_grads = _gather_comm([_pack([s5_small[k] for k in s5_names])])
    dh, dhb, g_ln["ln_ffn_pre"][1], carry = ffn_bwd(dh, dhb, (1, 0), "pre1", gains["ln_ffn_pre"][1], None,
                                                    comm_dw2=c_s5_grads)
    g_red = dict(zip(s5_names, _unpack(_sum_slots(c_s5_grads.results[0]), [s5_small[k].shape for k in s5_names])))
    dh, dhb, g_ln["ln_ffn_post"][0], carry = ffn_bwd(dh, dhb, (0, 1), "post0", gains["ln_ffn_post"][0], carry)
    dh, dhb, g_ln["ln_mix"][0], g_small["ab_conv_w"], carry = _mixer_ab_bwd(
        dh, dhb, saved["mix0"], gains["ln_mix"][0], wing, conv_w, wout, tabs, nb, seq, reduce_start, carry)
    dh, dhb, g_ln["ln_ffn_pre"][0], _ = ffn_bwd(dh, dhb, (0, 0), "pre0", gains["ln_ffn_pre"][0], carry, is_last=True)
    grad_x = dh.reshape(nb, seq, D)
    for k in g_ln:
        g_small[k] = jnp.concatenate(g_ln[k], axis=0)

    out = {}
    for k in BIG:
        transposed = k in ("ffn_w1", "ffn_w3")
        pieces = [(li, fj) for li in range(2) for fj in range(2)] if w[k].ndim == 4 else [None]

        def view(a):
            a = a.swapaxes(-1, -2) if transposed else a
            return a.reshape(len(pieces), -1, a.shape[-1])

        w3d, m3d, v3d = view(w[k]), view(mom[k]), view(var[k])
        bufs = [lax.empty(w3d.shape, F32) for _ in range(4)]
        for q, key in enumerate(pieces):
            part, comm = reduced[k if key is None else (k,) + key]
            bufs = _adamw_piece(w3d, m3d, v3d, q, part, comm.results[0], bufs)
        if transposed:
            out[k] = [t.reshape(w[k].shape[:2] + w3d.shape[1:]).swapaxes(-1, -2) for t in bufs]
        else:
            out[k] = [t.reshape(w[k].shape) for t in bufs]

    small_names = [k for k in NAMES if k not in BIG]
    late_names = [k for k in small_names if k not in g_red]
    g_red.update(zip(late_names, _unpack(_all_reduce_small(_pack([g_small[k] for k in late_names])),
                                         [g_small[k].shape for k in late_names])))
    cw = w["ab_conv_w"].shape[-1]
    g_red["ab_conv_w"] = lax.dynamic_slice_in_dim(g_red["ab_conv_w"], dev * cw, cw, axis=1)[None]
    dsz = w["s5_d"].shape[-1]
    g_red["s5_d"] = lax.dynamic_slice_in_dim(g_red["s5_d"].reshape(1, -1), dev * dsz, dsz, axis=1)
    shapes = [w[k].shape for k in small_names]
    g_red = {k: g_red[k].reshape(w[k].shape) for k in small_names}
    d_s, m_s, v_s = _adamw_small(_pack([w[k] for k in small_names]), _pack([g_red[k] for k in small_names]),
                                 _pack([mom[k] for k in small_names]), _pack([var[k] for k in small_names]))
    for k, d, nm, nv in zip(small_names, _unpack(d_s, shapes), _unpack(m_s, shapes), _unpack(v_s, shapes)):
        out[k] = [g_red[k], d, nm, nv]

    return (loss, grad_x, *[out[k][0] for k in NAMES], *[out[k][1] for k in NAMES],
            *[out[k][2] for k in NAMES], *[out[k][3] for k in NAMES])
```

```python
import jax
import jax.numpy as jnp
from jax import lax
from jax.experimental import pallas as pl
from jax.experimental.pallas import tpu as pltpu

F32, BF = jnp.float32, jnp.bfloat16
N_DEV = 8
MESH = pl.DeviceIdType.MESH
LANES = 128
SUBLANES = 8
VMEM_LIMIT = 56 * 2 ** 20
ROW_TILE = 512
FFN_ROW_TILE = 1024
COL_TILE = 512
ATTN_TILE = 512
SCAN_UNROLL = 8
ELEMS_PER_BLOCK = 512 * 1024
PAIR_SUM_ELEMS = 2048 * 1024
RMS_EPS = 1e-6
ROPE_THETA = 10000.0
NEG_INF = -1e30
S5_STATE = 64
S5_GROUP = 16
GROUPS_PER_BLOCK = LANES // S5_GROUP
STATE_COLS = GROUPS_PER_BLOCK * S5_STATE
DILATED_PATTERN = ((128, 1), (512, 4), (2048, 16))
ADAM_LR, ADAM_B1, ADAM_B2, ADAM_EPS, ADAM_WD, ADAM_STEP = 0.001, 0.9, 0.999, 1e-08, 0.01, 10
GELU_C = 0.7978845608028654
GELU_A = 0.044715


def _cparams(n_grid, vmem=VMEM_LIMIT):
    sem = ("arbitrary",) * n_grid if n_grid else None
    return pltpu.CompilerParams(dimension_semantics=sem, vmem_limit_bytes=vmem)


def _sig(x):
    return 1.0 / (1.0 + jnp.exp(-x))


def _gelu(x):
    return 0.5 * x * (1.0 + jnp.tanh(GELU_C * (x + GELU_A * x * x * x)))


def _gelu_grad(x):
    t = jnp.tanh(GELU_C * (x + GELU_A * x * x * x))
    return 0.5 * (1.0 + t) + 0.5 * x * (1.0 - t * t) * GELU_C * (1.0 + 3.0 * GELU_A * x * x)


def _dot(a, b, dims):
    a = a if a.dtype == BF else a.astype(BF)
    b = b if b.dtype == BF else b.astype(BF)
    return lax.dot_general(a, b, (dims, ((), ())), preferred_element_type=F32)


NN = ((1,), (0,))
NT = ((1,), (1,))
TN = ((0,), (0,))


def _row_block(rows, cols, elems=ELEMS_PER_BLOCK, mult=16):
    cap = max(mult, elems // cols)
    best = None
    for b in range(mult, min(rows, cap) + 1, mult):
        if rows % b == 0:
            best = b
    return rows if best is None else best


class _Comm:
    def __init__(self, ins, outs, sems, start, finish, members=()):
        self.ins, self.outs, self.sems, self.start, self.finish = ins, outs, sems, start, finish
        self.members = members
        self.results = None

    def set_results(self, res):
        self.results = list(res)
        off = 0
        for m in self.members:
            m.set_results(res[off:off + len(m.outs)])
            off += len(m.outs)


def _merge_comms(comms):
    comms = [c for c in comms if c is not None]
    if len(comms) < 2:
        return comms[0] if comms else None

    def each(fn_name, ins, outs, sems):
        i = o = s = 0
        for c in comms:
            ni, no, ns = len(c.ins), len(c.outs), len(c.sems)
            getattr(c, fn_name)(ins[i:i + ni], outs[o:o + no], sems[s:s + ns])
            i, o, s = i + ni, o + no, s + ns

    return _Comm([a for c in comms for a in c.ins], [a for c in comms for a in c.outs],
                 [a for c in comms for a in c.sems],
                 lambda ins, outs, sems: each("start", ins, outs, sems),
                 lambda ins, outs, sems: each("finish", ins, outs, sems), members=tuple(comms))


def _call(body, name, grid, in_specs, out_specs, out_shape, args, scratch=(), comm=None):
    in_specs, out_specs, out_shape, scratch = list(in_specs), list(out_specs), list(out_shape), list(scratch)
    if comm is None:
        return pl.pallas_call(body, grid=grid, in_specs=in_specs, out_specs=out_specs, out_shape=out_shape,
                              scratch_shapes=scratch, name=name, compiler_params=_cparams(len(grid)))(*args)
    n_in, n_out, n_sc = len(in_specs), len(out_specs), len(scratch)
    ci, co = len(comm.ins), len(comm.outs)

    def hosted(*refs):
        ins, refs = refs[:n_in], refs[n_in:]
        cins, refs = refs[:ci], refs[ci:]
        outs, refs = refs[:n_out], refs[n_out:]
        couts, refs = refs[:co], refs[co:]
        sc, csems = refs[:n_sc], refs[n_sc:]
        first = last = None
        for d, n in enumerate(grid):
            p = pl.program_id(d)
            first = (p == 0) if first is None else first & (p == 0)
            last = (p == n - 1) if last is None else last & (p == n - 1)

        @pl.when(first)
        def _():
            comm.start(cins, couts, csems)

        body(*ins, *outs, *sc)

        @pl.when(last)
        def _():
            comm.finish(cins, couts, csems)

    any_spec = pl.BlockSpec(memory_space=pl.ANY)
    res = pl.pallas_call(
        hosted, grid=grid, in_specs=in_specs + [any_spec] * ci, out_specs=out_specs + [any_spec] * co,
        out_shape=out_shape + list(comm.outs), scratch_shapes=scratch + list(comm.sems), name=name,
        compiler_params=_cparams(len(grid)))(*args, *comm.ins)
    comm.set_results(res[n_out:])
    return list(res[:n_out])


def _mm(name, grid, operands, pairs, n_acc, acc_shape, extras, outs, epilogue, comm=None, nrow=1, ncol=1,
        whole_tile_epilogue=False):
    nk = grid[2]
    n_op, n_ex, n_out = len(operands), len(extras), len(outs)

    def part_of(ref, dim, t, n):
        if n == 1:
            return ref
        size = ref.shape[dim] // n
        idx = [slice(None)] * len(ref.shape)
        idx[dim] = pl.ds(t * size, size)
        return ref.at[tuple(idx)]

    def tile_of(ref, r, c):
        return part_of(part_of(ref, 0, r, nrow), 1, c, ncol)

    def products(op, r, c):
        parts = [None] * n_acc
        for ai, bi, dims, ci in pairs:
            a = part_of(op[ai], 1 - dims[0][0], r, nrow)
            b = part_of(op[bi], 1 - dims[1][0], c, ncol)
            d = _dot(a[...], b[...], dims)
            parts[ci] = d if parts[ci] is None else parts[ci] + d
        return parts

    def body(*refs):
        op = refs[:n_op]
        ex = refs[n_op:n_op + n_ex]
        out = refs[n_op + n_ex:n_op + n_ex + n_out]
        acc = refs[n_op + n_ex + n_out:]
        tiles = [(r, c) for r in range(nrow) for c in range(ncol)]

        def views(refs_, t):
            return [tile_of(q, *t) for q in refs_]

        if nk == 1:
            parts = products(op, *tiles[0])
            for q, t in enumerate(tiles):
                nxt = products(op, *tiles[q + 1]) if q + 1 < len(tiles) else None
                epilogue(parts, views(ex, t), views(out, t))
                parts = nxt
            return
        k = pl.program_id(2)

        @pl.when(k == 0)
        def _():
            for q in acc:
                q[...] = jnp.zeros_like(q)

        for t in tiles:
            parts = products(op, *t)
            for q, p in zip(views(acc, t), parts):
                q[...] += p

        @pl.when(k == nk - 1)
        def _():
            if whole_tile_epilogue:
                epilogue(acc, ex, out)
                return
            for t in tiles:
                epilogue([q[...] for q in views(acc, t)], views(ex, t), views(out, t))

    return _call(body, name, grid, [s for _, s in operands] + [s for _, s in extras], [s for _, s in outs],
                 [sh for sh, _ in outs], [a for a, _ in operands] + [a for a, _ in extras],
                 scratch=[pltpu.VMEM(acc_shape, F32) for _ in range(n_acc if nk > 1 else 0)], comm=comm)


def _to_seg(a, seg_len):
    T, D = a.shape
    return a.reshape(SUBLANES, seg_len, D).transpose(1, 0, 2).reshape(T, D)


def _to_tok(a, seg_len):
    T, D = a.shape
    return a.reshape(seg_len, SUBLANES, D).transpose(1, 0, 2).reshape(T, D)


def _rms_fwd(h, gain, out_dtype):
    T, D = h.shape
    bm = min(ROW_TILE, T)

    def body(h_ref, g_ref, o_ref):
        x = h_ref[...]
        r = lax.rsqrt(jnp.mean(x * x, axis=-1, keepdims=True) + RMS_EPS)
        o_ref[...] = (x * r * g_ref[...]).astype(out_dtype)

    row = pl.BlockSpec((bm, D), lambda i: (i, 0))
    return pl.pallas_call(
        body, grid=(T // bm,), in_specs=[row, pl.BlockSpec((1, D), lambda i: (0, 0))],
        out_specs=row, out_shape=jax.ShapeDtypeStruct((T, D), out_dtype), name="rms_fwd",
        compiler_params=_cparams(1))(h, gain)


def _rms_bwd_rows(dn, x, g):
    r = lax.rsqrt(jnp.mean(x * x, axis=-1, keepdims=True) + RMS_EPS)
    xh = x * r
    dng = dn * g
    dx = r * (dng - xh * jnp.mean(dng * xh, axis=-1, keepdims=True))
    return dx, jnp.sum(dn * xh, axis=0, keepdims=True)


def _rms_bwd(dn, h, gain, dh_up):
    T, D = h.shape
    bm = min(ROW_TILE, T)

    def body(dn_ref, h_ref, g_ref, up_ref, dh_ref, dhb_ref, dg_ref):
        dx, dg = _rms_bwd_rows(dn_ref[...], h_ref[...], g_ref[...])
        dh = up_ref[...] + dx
        dh_ref[...] = dh
        dhb_ref[...] = dh.astype(BF)

        @pl.when(pl.program_id(0) == 0)
        def _():
            dg_ref[...] = jnp.zeros_like(dg_ref)

        dg_ref[...] += dg

    row = pl.BlockSpec((bm, D), lambda i: (i, 0))
    vec = pl.BlockSpec((1, D), lambda i: (0, 0))
    return pl.pallas_call(
        body, grid=(T // bm,), in_specs=[row, row, vec, row], out_specs=[row, row, vec],
        out_shape=[jax.ShapeDtypeStruct((T, D), F32), jax.ShapeDtypeStruct((T, D), BF),
                   jax.ShapeDtypeStruct((1, D), F32)],
        name="rms_bwd", compiler_params=_cparams(1))(dn, h, gain, dh_up)


def _loss_head(h, gain, target):
    T, D = h.shape
    bm = min(ROW_TILE, T)

    def body(h_ref, g_ref, t_ref, dh_ref, dhb_ref, dg_ref, loss_ref):
        x = h_ref[...]
        g = g_ref[...]
        r = lax.rsqrt(jnp.mean(x * x, axis=-1, keepdims=True) + RMS_EPS)
        err = x * r * g - t_ref[...]
        part = 0.5 * jnp.sum(jnp.sum(err * err, axis=-1, keepdims=True), axis=0, keepdims=True) / D
        dx, dg = _rms_bwd_rows(err / D, x, g)
        dh_ref[...] = dx
        dhb_ref[...] = dx.astype(BF)

        @pl.when(pl.program_id(0) == 0)
        def _():
            dg_ref[...] = jnp.zeros_like(dg_ref)
            loss_ref[...] = jnp.zeros_like(loss_ref)

        dg_ref[...] += dg
        loss_ref[...] += jnp.broadcast_to(part, loss_ref.shape)

    row = pl.BlockSpec((bm, D), lambda i: (i, 0))
    vec = pl.BlockSpec((1, D), lambda i: (0, 0))
    return pl.pallas_call(
        body, grid=(T // bm,), in_specs=[row, vec, row],
        out_specs=[row, row, vec, pl.BlockSpec((SUBLANES, LANES), lambda i: (0, 0))],
        out_shape=[jax.ShapeDtypeStruct((T, D), F32), jax.ShapeDtypeStruct((T, D), BF),
                   jax.ShapeDtypeStruct((1, D), F32), jax.ShapeDtypeStruct((SUBLANES, LANES), F32)],
        name="loss_head", compiler_params=_cparams(1))(h, gain, target)


def _ffn_up(n, w1g, w3g, comm=None):
    T, D = n.shape
    fs = w1g.shape[-1]
    bm = min(FFN_ROW_TILE, T)
    wspec = pl.BlockSpec((None, D, fs), lambda s, i, k: (s, 0, 0))
    ospec = pl.BlockSpec((None, bm, fs), lambda s, i, k: (s, i, 0))

    def epi(accs, ex, outs):
        a1, a3 = accs
        sg = _sig(a1)
        silu = a1 * sg
        outs[0][...] = (a3 * sg * (1.0 + a1 * (1.0 - sg))).astype(BF)
        outs[1][...] = silu.astype(BF)
        outs[2][...] = (silu * a3).astype(BF)

    sh = jax.ShapeDtypeStruct((N_DEV, T, fs), BF)
    return _mm("ffn_up", (N_DEV, T // bm, 1),
               [(n, pl.BlockSpec((bm, D), lambda s, i, k: (i, 0))), (w1g, wspec), (w3g, wspec)],
               [(0, 1, NN, 0), (0, 2, NN, 1)], 2, None, [], [(sh, ospec)] * 3, epi, comm=comm,
               nrow=max(1, bm // ROW_TILE))


def _ffn_down(g, w2g, h, comm=None):
    _, T, fs = g.shape
    D = h.shape[1]
    bm = min(FFN_ROW_TILE, T)
    row = pl.BlockSpec((bm, D), lambda i, j, s: (i, 0))

    def epi(accs, ex, outs):
        outs[0][...] = ex[0][...] + 0.5 * accs[0]

    return _mm("ffn_down", (T // bm, 1, N_DEV),
               [(g, pl.BlockSpec((None, bm, fs), lambda i, j, s: (s, i, 0))),
                (w2g, pl.BlockSpec((None, fs, D), lambda i, j, s: (s, 0, 0)))],
               [(0, 1, NN, 0)], 1, (bm, D), [(h, row)],
               [(jax.ShapeDtypeStruct((T, D), F32), row)], epi, comm=comm,
               nrow=max(1, bm // ROW_TILE), ncol=max(1, D // COL_TILE))[0]


def _ffn_bwd_hidden(dhb, w2g, t1, t3, comm=None):
    T, D = dhb.shape
    fs = t1.shape[-1]
    bm = min(FFN_ROW_TILE, T)
    aspec = pl.BlockSpec((None, bm, fs), lambda s, i, k: (s, i, 0))

    def epi(accs, ex, outs):
        dg = 0.5 * accs[0]
        outs[0][...] = (dg * ex[0][...].astype(F32)).astype(BF)
        outs[1][...] = (dg * ex[1][...].astype(F32)).astype(BF)

    sh = jax.ShapeDtypeStruct((N_DEV, T, fs), BF)
    return _mm("ffn_bwd_hidden", (N_DEV, T // bm, 1),
               [(dhb, pl.BlockSpec((bm, D), lambda s, i, k: (i, 0))),
                (w2g, pl.BlockSpec((None, fs, D), lambda s, i, k: (s, 0, 0)))],
               [(0, 1, NT, 0)], 1, None, [(t1, aspec), (t3, aspec)], [(sh, aspec)] * 2, epi, comm=comm,
               nrow=max(1, bm // ROW_TILE))


def _ffn_dw2(g, dhb, comm=None):
    _, T, fs = g.shape
    D = dhb.shape[1]
    bn = min(COL_TILE, D)

    def epi(accs, ex, outs):
        outs[0][...] = (0.5 * accs[0]).astype(BF)

    return _mm("ffn_dw2", (N_DEV, D // bn, 1),
               [(g, pl.BlockSpec((None, T, fs), lambda s, j, k: (s, 0, 0))),
                (dhb, pl.BlockSpec((T, bn), lambda s, j, k: (0, j)))],
               [(0, 1, TN, 0)], 1, None, [],
               [(jax.ShapeDtypeStruct((N_DEV, fs, D), BF), pl.BlockSpec((None, fs, bn), lambda s, j, k: (s, 0, j)))],
               epi, comm=comm)[0]


def _ffn_dw13(n, da1, da3, comm=None):
    T, D = n.shape
    fs = da1.shape[-1]
    bn = min(COL_TILE, D)
    dspec = pl.BlockSpec((None, T, fs), lambda s, j, k: (s, 0, 0))
    ospec = pl.BlockSpec((None, fs, bn), lambda s, j, k: (s, 0, j))

    def epi(accs, ex, outs):
        outs[0][...] = accs[0].astype(BF)
        outs[1][...] = accs[1].astype(BF)

    sh = jax.ShapeDtypeStruct((N_DEV, fs, D), BF)
    return _mm("ffn_dw13", (N_DEV, D // bn, 1),
               [(da1, dspec), (da3, dspec), (n, pl.BlockSpec((T, bn), lambda s, j, k: (0, j)))],
               [(0, 2, TN, 0), (1, 2, TN, 1)], 2, None, [], [(sh, ospec)] * 2, epi, comm=comm)


def _ffn_dn_rms(da1, da3, w1g, w3g, h, gain, dh_up, comm=None):
    _, T, fs = da1.shape
    D = w1g.shape[-2]
    bm = min(ROW_TILE, T)
    rows_per_pass = min(64, bm)
    dspec = pl.BlockSpec((None, bm, fs), lambda i, j, s: (s, i, 0))
    wspec = pl.BlockSpec((None, D, fs), lambda i, j, s: (s, 0, 0))
    row = pl.BlockSpec((bm, D), lambda i, j, s: (i, 0))
    vec = pl.BlockSpec((1, D), lambda i, j, s: (0, 0))

    def epi(acc, ex, outs):
        h_ref, g_ref, up_ref = ex
        dh_ref, dhb_ref, dg_ref = outs

        @pl.when(pl.program_id(0) == 0)
        def _():
            dg_ref[...] = jnp.zeros_like(dg_ref)

        g = g_ref[...]
        dg = jnp.zeros((1, D), F32)
        for r in range(bm // rows_per_pass):
            rows = pl.ds(r * rows_per_pass, rows_per_pass)
            dx, dg_r = _rms_bwd_rows(acc[0][rows, :], h_ref[rows, :], g)
            dh = up_ref[rows, :] + dx
            dh_ref[rows, :] = dh
            dhb_ref[rows, :] = dh.astype(BF)
            dg = dg + dg_r
        dg_ref[...] += dg

    return _mm("ffn_dn_rms", (T // bm, 1, N_DEV),
               [(da1, dspec), (w1g, wspec), (da3, dspec), (w3g, wspec)],
               [(0, 1, NT, 0), (2, 3, NT, 0)], 1, (bm, D), [(h, row), (gain, vec), (dh_up, row)],
               [(jax.ShapeDtypeStruct((T, D), F32), row), (jax.ShapeDtypeStruct((T, D), BF), row),
                (jax.ShapeDtypeStruct((1, D), F32), vec)],
               epi, comm=comm, ncol=max(1, D // COL_TILE), whole_tile_epilogue=True)


def _rope_tables(seq):
    half = LANES // 2
    inv = ROPE_THETA ** (-jnp.arange(0, half, dtype=F32) * 2.0 / LANES)
    ang = jnp.arange(seq, dtype=F32)[:, None] * inv[None, :]
    cos, sin = jnp.cos(ang), jnp.sin(ang)
    return jnp.concatenate([cos, cos], axis=1), jnp.concatenate([-sin, sin], axis=1)


def _branch_bias(nq, bq):
    d = (jnp.arange(nq)[:, None, None] * bq + jnp.arange(bq)[None, :, None]
         - jnp.arange(bq)[None, None, :])
    mult = jnp.zeros(d.shape, F32)
    for window, dil in DILATED_PATTERN:
        mult = mult + ((d >= 0) & (d % dil == 0) & (d <= window)).astype(F32)
    return jnp.where(mult > 0, jnp.log(jnp.maximum(mult, 1.0)), NEG_INF)


def _proj_fwd(u, wing, comm=None):
    T, D = u.shape
    ws = wing.shape[-1]
    bm = min(FFN_ROW_TILE, T)

    def epi(accs, ex, outs):
        outs[0][...] = accs[0]

    return _mm("proj_fwd", (N_DEV, T // bm, 1),
               [(u, pl.BlockSpec((bm, D), lambda s, i, k: (i, 0))),
                (wing, pl.BlockSpec((None, D, ws), lambda s, i, k: (s, 0, 0)))],
               [(0, 1, NN, 0)], 1, None, [],
               [(jax.ShapeDtypeStruct((T, N_DEV * ws), F32),
                 pl.BlockSpec((bm, ws), lambda s, i, k: (i, s)))], epi, comm=comm,
               nrow=max(1, bm // ROW_TILE))[0]


def _rope_fwd(proj, cosf, sinf, seq, nh):
    T = proj.shape[0]
    bs = min(ROW_TILE, seq)
    nst = seq // bs
    scale = LANES ** -0.5

    def body(x_ref, c_ref, s_ref, o_ref):
        j = pl.program_id(1)
        c = c_ref[...]
        s = s_ref[...]
        mul = jnp.where(j == 0, scale, 1.0)
        for h in range(nh):
            cols = slice(h * LANES, (h + 1) * LANES)
            t = x_ref[:, cols]
            rot = (t * c + pltpu.roll(t, LANES // 2, 1) * s) * mul
            o_ref[:, cols] = jnp.where(j < 2, rot, t).astype(BF)

    blk = pl.BlockSpec((bs, nh * LANES), lambda r, j: (r, j))
    tab = pl.BlockSpec((bs, LANES), lambda r, j: (r % nst, 0))
    return pl.pallas_call(
        body, grid=(T // bs, 3), in_specs=[blk, tab, tab], out_specs=blk,
        out_shape=jax.ShapeDtypeStruct((T, 3 * nh * LANES), BF), name="rope_fwd",
        compiler_params=_cparams(2))(proj, cosf, sinf)


def _attn_fwd(qkv, bias, nb, seq, nh, comm=None):
    T = nb * seq
    bq = bias.shape[1]
    nq = seq // bq

    def body(q_ref, k_ref, v_ref, b_ref, o_ref, lse_ref):
        qi = pl.program_id(2)
        q = q_ref[...]

        def step(kj, carry):
            m, l, acc = carry
            rows = pl.ds(pl.multiple_of(kj * bq, bq), bq)
            s = _dot(q, k_ref[rows, :], NT) + b_ref[qi - kj]
            m_new = jnp.maximum(m, jnp.max(s, axis=1, keepdims=True))
            p = jnp.exp(s - m_new)
            alpha = jnp.exp(m - m_new)
            l = alpha * l + jnp.sum(p, axis=1, keepdims=True)
            acc = alpha * acc + _dot(p, v_ref[rows, :], NN)
            return m_new, l, acc

        init = (jnp.full((bq, 1), NEG_INF, F32), jnp.zeros((bq, 1), F32), jnp.zeros((bq, LANES), F32))
        m, l, acc = lax.fori_loop(0, qi + 1, step, init)
        o_ref[...] = (acc / l).astype(BF)
        lse_ref[...] = m + jnp.log(l)

    return _call(
        body, "attn_fwd", (nb, nh, nq),
        [pl.BlockSpec((bq, LANES), lambda b, h, i: (b * nq + i, h)),
         pl.BlockSpec((seq, LANES), lambda b, h, i: (b, nh + h)),
         pl.BlockSpec((seq, LANES), lambda b, h, i: (b, 2 * nh + h)),
         pl.BlockSpec((nq, bq, bq), lambda b, h, i: (0, 0, 0))],
        [pl.BlockSpec((bq, LANES), lambda b, h, i: (b * nq + i, h)),
         pl.BlockSpec((None, bq, 1), lambda b, h, i: (h, b * nq + i, 0))],
        [jax.ShapeDtypeStruct((T, 2 * nh * LANES), BF), jax.ShapeDtypeStruct((nh, T, 1), F32)],
        (qkv, qkv, qkv, bias), comm=comm)


def _attn_bwd(qkv, cat, dcat, lse, bias, nb, seq, nh, comm=None):
    T = nb * seq
    bq = bias.shape[1]
    nq = seq // bq

    def body(k_ref, v_ref, q_ref, o_ref, do_ref, lse_ref, b_ref, dq_ref, dk_ref, dv_ref):
        kj = pl.program_id(2)
        k = k_ref[...]
        v = v_ref[...]

        @pl.when(kj == 0)
        def _():
            dq_ref[...] = jnp.zeros_like(dq_ref)

        def step(qi, carry):
            dk, dv = carry
            rows = pl.ds(pl.multiple_of(qi * bq, bq), bq)
            q = q_ref[rows, :]
            do = do_ref[rows, :]
            dob = do.astype(BF)
            delta = jnp.sum(do * o_ref[rows, :].astype(F32), axis=1, keepdims=True)
            p = jnp.exp(_dot(q, k, NT) + b_ref[qi - kj] - lse_ref[rows, :])
            dv = dv + _dot(p, dob, TN)
            ds = p * (_dot(dob, v, NT) - delta)
            dq_ref[rows, :] += _dot(ds, k, NN)
            return dk + _dot(ds, q, TN), dv

        z = jnp.zeros((bq, LANES), F32)
        dk, dv = lax.fori_loop(kj, nq, step, (z, z))
        dk_ref[...] = dk
        dv_ref[...] = dv

    whole = pl.BlockSpec((seq, LANES), lambda b, h, i: (b, h))
    tile = pl.BlockSpec((bq, LANES), lambda b, h, i: (b * nq + i, h))
    sh = jax.ShapeDtypeStruct((T, nh * LANES), F32)
    return _call(
        body, "attn_bwd", (nb, nh, nq),
        [pl.BlockSpec((bq, LANES), lambda b, h, i: (b * nq + i, nh + h)),
         pl.BlockSpec((bq, LANES), lambda b, h, i: (b * nq + i, 2 * nh + h)),
         whole, whole, whole, pl.BlockSpec((None, seq, 1), lambda b, h, i: (h, b, 0)),
         pl.BlockSpec((nq, bq, bq), lambda b, h, i: (0, 0, 0))],
        [whole, tile, tile], [sh, sh, sh],
        (qkv, qkv, qkv, cat, dcat, lse, bias), comm=comm)


def _conv_parts(gc, xin, w_ref):
    w = [w_ref[k:k + 1, :] for k in range(3)]
    u = gc * xin
    row = lax.broadcasted_iota(jnp.int32, u.shape, 0)
    u1 = jnp.where(row >= 1, pltpu.roll(u, 1, 0), 0.0)
    u2 = jnp.where(row >= 2, pltpu.roll(u, 2, 0), 0.0)
    return u, u1, u2, w[0] * u2 + w[1] * u1 + w[2] * u, w, row


def _conv_fwd(proj, conv_w, cat, nb, seq, width):
    cw = min(2 * LANES, width)
    nc = width // cw

    def body(gb_ref, gc_ref, x_ref, w_ref, cat_ref, o_ref):
        _, _, _, conv, _, _ = _conv_parts(gc_ref[...], x_ref[...], w_ref)
        o_ref[...] = (gb_ref[...] * conv).astype(BF)

    def sec(k):
        return pl.BlockSpec((seq, cw), lambda b, c: (b, k * nc + c))

    return pl.pallas_call(
        body, grid=(nb, nc),
        in_specs=[sec(3), sec(4), sec(5), pl.BlockSpec((3, cw), lambda b, c: (0, c)),
                  pl.BlockSpec(memory_space=pl.ANY)],
        out_specs=pl.BlockSpec((seq, cw), lambda b, c: (b, nc + c)),
        out_shape=jax.ShapeDtypeStruct(cat.shape, BF), input_output_aliases={4: 0},
        name="conv_fwd", compiler_params=_cparams(2))(proj, proj, proj, conv_w, cat)


def _conv_bwd(proj, conv_w, dcat, nb, seq, width):
    cw = min(2 * LANES, width)
    nc = width // cw
    T = nb * seq

    def body(gb_ref, gc_ref, x_ref, w_ref, d_ref, dgb_ref, dgc_ref, dx_ref, dw_ref):
        gc = gc_ref[...]
        xin = x_ref[...]
        u, u1, u2, conv, w, row = _conv_parts(gc, xin, w_ref)
        dsc = d_ref[...]
        dgb_ref[...] = dsc * conv
        dconv = dsc * gb_ref[...]
        d1 = jnp.where(row < seq - 1, pltpu.roll(dconv, seq - 1, 0), 0.0)
        d2 = jnp.where(row < seq - 2, pltpu.roll(dconv, seq - 2, 0), 0.0)
        du = w[2] * dconv + w[1] * d1 + w[0] * d2
        dgc_ref[...] = du * xin
        dx_ref[...] = du * gc

        @pl.when(pl.program_id(1) == 0)
        def _():
            dw_ref[...] = jnp.zeros_like(dw_ref)

        dw_ref[0:1, :] += jnp.sum(dconv * u2, axis=0, keepdims=True)
        dw_ref[1:2, :] += jnp.sum(dconv * u1, axis=0, keepdims=True)
        dw_ref[2:3, :] += jnp.sum(dconv * u, axis=0, keepdims=True)

    def sec(k):
        return pl.BlockSpec((seq, cw), lambda c, b: (b, k * nc + c))

    out = pl.BlockSpec((seq, cw), lambda c, b: (b, c))
    wsp = pl.BlockSpec((3, cw), lambda c, b: (0, c))
    sh = jax.ShapeDtypeStruct((T, width), F32)
    return pl.pallas_call(
        body, grid=(nc, nb), in_specs=[sec(3), sec(4), sec(5), wsp, sec(1)],
        out_specs=[out, out, out, wsp], out_shape=[sh, sh, sh, jax.ShapeDtypeStruct((3, width), F32)],
        name="conv_bwd", compiler_params=_cparams(2))(proj, proj, proj, conv_w, dcat)


def _assemble_dproj(dq, dk, dv, dgb, dgc, dxin, cosf, sinf, seq):
    T, width = dq.shape
    nh = width // LANES
    bs = min(ROW_TILE, seq)
    nst = seq // bs
    scale = LANES ** -0.5

    def body(dq_ref, dk_ref, dv_ref, dgb_ref, dgc_ref, dx_ref, c_ref, s_ref, o_ref):
        sec = pl.program_id(1)
        c = c_ref[...]
        s = s_ref[...]

        def unrope(ref, mul):
            for h in range(nh):
                cols = slice(h * LANES, (h + 1) * LANES)
                t = ref[:, cols]
                o_ref[:, cols] = ((t * c + pltpu.roll(t * s, LANES // 2, 1)) * mul).astype(BF)

        @pl.when(sec == 0)
        def _():
            unrope(dq_ref, scale)

        @pl.when(sec == 1)
        def _():
            unrope(dk_ref, 1.0)

        for k, ref in ((2, dv_ref), (3, dgb_ref), (4, dgc_ref), (5, dx_ref)):
            @pl.when(sec == k)
            def _(ref=ref):
                o_ref[...] = ref[...].astype(BF)

    blk = pl.BlockSpec((bs, width), lambda r, k: (r, 0))
    tab = pl.BlockSpec((bs, LANES), lambda r, k: (r % nst, 0))
    return pl.pallas_call(
        body, grid=(T // bs, 6), in_specs=[blk] * 6 + [tab, tab],
        out_specs=pl.BlockSpec((bs, width), lambda r, k: (r, k)),
        out_shape=jax.ShapeDtypeStruct((T, 6 * width), BF), name="assemble_dproj",
        compiler_params=_cparams(2))(dq, dk, dv, dgb, dgc, dxin, cosf, sinf)


def _res_mm(name, a, w, h, comm=None):
    T, K = a.shape
    N = w.shape[1]
    bm = min(FFN_ROW_TILE, T)
    bk = min(ROW_TILE, K)
    row = pl.BlockSpec((bm, N), lambda i, j, k: (i, 0))

    def epi(accs, ex, outs):
        outs[0][...] = ex[0][...] + accs[0]

    return _mm(name, (T // bm, 1, K // bk),
               [(a, pl.BlockSpec((bm, bk), lambda i, j, k: (i, k))),
                (w, pl.BlockSpec((bk, N), lambda i, j, k: (k, 0)))],
               [(0, 1, NN, 0)], 1, (bm, N), [(h, row)],
               [(jax.ShapeDtypeStruct((T, N), F32), row)], epi, comm=comm,
               nrow=max(1, bm // ROW_TILE), ncol=max(1, N // COL_TILE))[0]


def _mm_nt(name, a, w, out_dtype):
    T, K = a.shape
    N = w.shape[0]
    bm = min(FFN_ROW_TILE, T)
    bn = min(ROW_TILE, N)

    def epi(accs, ex, outs):
        outs[0][...] = accs[0].astype(out_dtype)

    return _mm(name, (T // bm, N // bn, 1),
               [(a, pl.BlockSpec((bm, K), lambda i, j, k: (i, 0))),
                (w, pl.BlockSpec((bn, K), lambda i, j, k: (j, 0)))],
               [(0, 1, NT, 0)], 1, None, [],
               [(jax.ShapeDtypeStruct((T, N), out_dtype), pl.BlockSpec((bm, bn), lambda i, j, k: (i, j)))],
               epi, nrow=max(1, bm // ROW_TILE))[0]


def _mm_tn(name, a, bs_list):
    T, M = a.shape
    N = bs_list[0].shape[1]
    bmr = min(COL_TILE, M)
    bn = min(COL_TILE, N)
    n = len(bs_list)

    def epi(accs, ex, outs):
        for q in range(n):
            outs[q][...] = accs[q].astype(BF)

    ops = [(a, pl.BlockSpec((T, bmr), lambda r, j, k: (0, r)))]
    ops += [(b, pl.BlockSpec((T, bn), lambda r, j, k: (0, j))) for b in bs_list]
    return _mm(name, (M // bmr, N // bn, 1), ops, [(0, 1 + q, TN, q) for q in range(n)], n, None, [],
               [(jax.ShapeDtypeStruct((M, N), BF), pl.BlockSpec((bmr, bn), lambda r, j, k: (r, j)))] * n, epi)


def _proj_bwd_x(dproj, wing):
    T = dproj.shape[0]
    _, D, ws = wing.shape
    bm = min(FFN_ROW_TILE, T)
    row = pl.BlockSpec((bm, D), lambda i, j, s: (i, 0))

    def epi(accs, ex, outs):
        outs[0][...] = accs[0]

    return _mm("proj_bwd_x", (T // bm, 1, N_DEV),
               [(dproj, pl.BlockSpec((bm, ws), lambda i, j, s: (i, s))),
                (wing, pl.BlockSpec((None, D, ws), lambda i, j, s: (s, 0, 0)))],
               [(0, 1, NT, 0)], 1, (bm, D), [], [(jax.ShapeDtypeStruct((T, D), F32), row)], epi,
               nrow=max(1, bm // ROW_TILE), ncol=max(1, D // COL_TILE))[0]


def _proj_dw(u, dproj, ws):
    T, D = u.shape
    bmr = min(COL_TILE, D)

    def epi(accs, ex, outs):
        outs[0][...] = accs[0].astype(BF)

    return _mm("proj_dw", (N_DEV, D // bmr, 1),
               [(u, pl.BlockSpec((T, bmr), lambda s, r, k: (0, r))),
                (dproj, pl.BlockSpec((T, ws), lambda s, r, k: (0, s)))],
               [(0, 1, TN, 0)], 1, None, [],
               [(jax.ShapeDtypeStruct((N_DEV, D, ws), BF),
                 pl.BlockSpec((None, bmr, ws), lambda s, r, k: (s, r, 0)))], epi)[0]


def _mixer_ab_fwd(h, gain, wing, conv_w, wout, tabs, nb, seq, comm_proj=None, comm_attn=None, comm_out=None):
    cosf, sinf, bias = tabs
    width = wing.shape[-1] * N_DEV // 6
    nh = width // LANES
    u = _rms_fwd(h, gain, BF)
    proj = _proj_fwd(u, wing, comm=comm_proj)
    qkv = _rope_fwd(proj, cosf, sinf, seq, nh)
    cat, lse = _attn_fwd(qkv, bias, nb, seq, nh, comm=comm_attn)
    cat = _conv_fwd(proj, conv_w, cat, nb, seq, width)
    return _res_mm("outproj_fwd", cat, wout, h, comm=comm_out), (h, u, proj, qkv, cat, lse)


def _mixer_ab_bwd(dh, dhb, saved, gain, wing, conv_w, wout, tabs, nb, seq, reduce_start, carry):
    cosf, sinf, bias = tabs
    h, u, proj, qkv, cat, lse = saved
    D = h.shape[1]
    ws = wing.shape[-1]
    width = ws * N_DEV // 6
    nh = width // LANES
    dcat = _mm_nt("outproj_bwd_x", dhb, wout, F32)
    dwout = _mm_tn("outproj_dw", cat, [dhb])[0]
    comm = _merge_comms(reduce_start(["ab_w_out"], [dwout.reshape(N_DEV, -1, D)]) + [carry])
    dq, dk, dv = _attn_bwd(qkv, cat, dcat, lse, bias, nb, seq, nh, comm=comm)
    dgb, dgc, dxin, dconvw = _conv_bwd(proj, conv_w, dcat, nb, seq, width)
    dproj = _assemble_dproj(dq, dk, dv, dgb, dgc, dxin, cosf, sinf, seq)
    du = _proj_bwd_x(dproj, wing)
    comm, = reduce_start(["ab_w_in"], [_proj_dw(u, dproj, ws)])
    dh_in, dhb_in, dgain = _rms_bwd(du, h, gain, dh)
    return dh_in, dhb_in, dgain, dconvw, comm


def _s5_zoh(lr, li, log_dt):
    dt = jnp.exp(log_dt)
    mag = jnp.exp(lr * dt)
    ar = mag * jnp.cos(li * dt)
    ai = mag * jnp.sin(li * dt)
    den = lr * lr + li * li
    return dt, ar, ai, den, ((ar - 1.0) * lr + ai * li) / den, (ai * lr - (ar - 1.0) * li) / den


def _s5_discretize(lam_re, lam_im, log_dt, bt_re, bt_im):
    def body(lr_ref, li_ref, ld_ref, br_ref, bi_ref, ar_ref, ai_ref, bbr_ref, bbi_ref):
        _, ar, ai, _, fr, fi = _s5_zoh(lr_ref[...], li_ref[...], ld_ref[...])
        ar_ref[...] = ar
        ai_ref[...] = ai
        bbr_ref[...] = fr * br_ref[...] - fi * bi_ref[...]
        bbi_ref[...] = fr * bi_ref[...] + fi * br_ref[...]

    small = jax.ShapeDtypeStruct(lam_re.shape, F32)
    big = jax.ShapeDtypeStruct(bt_re.shape, F32)
    return pl.pallas_call(body, out_shape=[small, small, big, big], name="s5_discretize",
                          compiler_params=_cparams(0))(lam_re, lam_im, log_dt, bt_re, bt_im)


def _s5_discretize_bwd(lam_re, lam_im, log_dt, bt_re, bt_im, d_ar, d_ai, d_bbr, d_bbi):

    def body(lr_ref, li_ref, ld_ref, br_ref, bi_ref, dar_ref, dai_ref, dbbr_ref, dbbi_ref,
             dlr_ref, dli_ref, dld_ref, dbr_ref, dbi_ref):
        lr, li = lr_ref[...], li_ref[...]
        dt, ar, ai, den, fr, fi = _s5_zoh(lr, li, ld_ref[...])
        br, bi = br_ref[...], bi_ref[...]
        dbbr, dbbi = dbbr_ref[...], dbbi_ref[...]
        dbr_ref[...] = dbbr * fr + dbbi * fi
        dbi_ref[...] = dbbi * fr - dbbr * fi
        dfr = jnp.sum(dbbr * br + dbbi * bi, axis=1, keepdims=True)
        dfi = jnp.sum(dbbi * br - dbbr * bi, axis=1, keepdims=True)
        dnr = dfr / den
        dni = dfi / den
        dden = -(dfr * fr + dfi * fi) / den
        dar = dar_ref[...] + dnr * lr - dni * li
        dai = dai_ref[...] + dnr * li + dni * lr
        dlr_ref[...] = dnr * (ar - 1.0) + dni * ai + 2.0 * dden * lr + dt * (dar * ar + dai * ai)
        dli_ref[...] = dnr * ai - dni * (ar - 1.0) + 2.0 * dden * li + dt * (dai * ar - dar * ai)
        ddt = jnp.sum(dar * (lr * ar - li * ai) + dai * (lr * ai + li * ar), axis=2, keepdims=True)
        dld_ref[...] = ddt * dt

    small = jax.ShapeDtypeStruct(lam_re.shape, F32)
    big = jax.ShapeDtypeStruct(bt_re.shape, F32)
    return pl.pallas_call(
        body, out_shape=[small, small, jax.ShapeDtypeStruct(log_dt.shape, F32), big, big],
        name="s5_discretize_bwd", compiler_params=_cparams(0))(
            lam_re, lam_im, log_dt, bt_re, bt_im, d_ar, d_ai, d_bbr, d_bbi)


def _rows8(t):
    if isinstance(t, int):
        return pl.ds(t * SUBLANES, SUBLANES)
    return pl.ds(pl.multiple_of(t * SUBLANES, SUBLANES), SUBLANES)


def _cmul_add(ar, ai, sr, si, br, bi):
    return ar * sr - ai * si + br, ar * si + ai * sr + bi


def _steps(n, step, carry):
    head = n % SCAN_UNROLL
    for i in range(head):
        carry = step(i, carry)

    def trip(j, c):
        for q in range(SCAN_UNROLL):
            c = step(head + j * SCAN_UNROLL + q, c)
        return c

    return lax.fori_loop(0, n // SCAN_UNROLL, trip, carry)


def _scan(a, read, write, init, n):
    def step(t, c):
        s = _cmul_add(*a, *c, *read(t))
        if write is not None:
            write(t, s)
        return s

    return _steps(n, step, init)


def _cpow(ar, ai, n):
    rr = ri = None
    while n:
        if n & 1:
            rr, ri = (ar, ai) if rr is None else (rr * ar - ri * ai, rr * ai + ri * ar)
        ar, ai = ar * ar - ai * ai, 2.0 * ar * ai
        n >>= 1
    return rr, ri


def _s5_specs(R, nj):
    sh = STATE_COLS
    return dict(
        rows=pl.BlockSpec((R, LANES), lambda j: (0, j)),
        bd=pl.BlockSpec((None, LANES, sh), lambda j: (j, 0, 0)),
        cd=pl.BlockSpec((None, sh, LANES), lambda j: (j, 0, 0)),
        a=pl.BlockSpec((None, 1, sh), lambda j: (j, 0, 0)),
        vec=pl.BlockSpec((1, LANES), lambda j: (0, j)),
        init=pl.BlockSpec((None, SUBLANES, sh), lambda j: (j, 0, 0)))


def _s5_fwd(u, mats, seg_len, nseg, comm=None):
    bdr, bdi, cdr, cdi, are, aim, dsk = mats
    R, D = u.shape
    nj = D // LANES
    sh = STATE_COLS
    rc = min(R, 512)
    sp = _s5_specs(R, nj)

    def body(u_ref, bdr_ref, bdi_ref, cdr_ref, cdi_ref, ar_ref, ai_ref, d_ref,
             y_ref, yg_ref, ir_ref, ii_ref, sre, sim):
        ar = jnp.broadcast_to(ar_ref[...], (SUBLANES, sh))
        ai = jnp.broadcast_to(ai_ref[...], (SUBLANES, sh))

        def bu_chunk(c, _):
            rows = pl.ds(pl.multiple_of(c * rc, rc), rc)
            ub = u_ref[rows, :].astype(BF)
            sre[rows, :] = _dot(ub, bdr_ref[...], NN)
            sim[rows, :] = _dot(ub, bdi_ref[...], NN)
            return 0

        lax.fori_loop(0, R // rc, bu_chunk, 0)
        z = jnp.zeros((SUBLANES, sh), F32)

        def read(t):
            return sre[_rows8(t), :], sim[_rows8(t), :]

        def write(t, s):
            sre[_rows8(t), :] = s[0]
            sim[_rows8(t), :] = s[1]

        er, ei = _scan((ar, ai), read, None, (z, z), seg_len)
        pr, pi = _cpow(ar, ai, seg_len)
        first = (lax.broadcasted_iota(jnp.int32, (SUBLANES, sh), 0) & (nseg - 1)) == 0

        def prev(x):
            return jnp.where(first, 0.0, pltpu.roll(x, 1, 0))

        xr, xi = er, ei
        for _ in range(nseg - 1):
            xr, xi = _cmul_add(pr, pi, prev(xr), prev(xi), er, ei)
        i_r, i_i = prev(xr), prev(xi)
        ir_ref[...] = i_r
        ii_ref[...] = i_i
        _scan((ar, ai), read, write, (i_r, i_i), seg_len)

        def y_chunk(c, _):
            rows = pl.ds(pl.multiple_of(c * rc, rc), rc)
            y = _dot(sre[rows, :], cdr_ref[...], NN) + _dot(sim[rows, :], cdi_ref[...], NN)
            y = y + d_ref[...] * u_ref[rows, :]
            y_ref[rows, :] = y
            yg_ref[rows, :] = _gelu(y).astype(BF)
            return 0

        lax.fori_loop(0, R // rc, y_chunk, 0)

    init_sh = jax.ShapeDtypeStruct((nj, SUBLANES, STATE_COLS), F32)
    return _call(
        body, "s5_fwd", (nj,),
        [sp["rows"], sp["bd"], sp["bd"], sp["cd"], sp["cd"], sp["a"], sp["a"], sp["vec"]],
        [sp["rows"], sp["rows"], sp["init"], sp["init"]],
        [jax.ShapeDtypeStruct((R, D), F32), jax.ShapeDtypeStruct((R, D), BF), init_sh, init_sh],
        (u, bdr, bdi, cdr, cdi, are, aim, dsk),
        scratch=[pltpu.VMEM((R, sh), F32) for _ in range(2)], comm=comm)


def _s5_bwd(u, dy, mats, init_re, init_im, seg_len, nseg, comm=None):
    bdr, bdi, cdr, cdi, are, aim, dsk = mats
    R, D = u.shape
    nj = D // LANES
    sh = STATE_COLS
    rc = min(R, 512)
    sp = _s5_specs(R, nj)

    def body(u_ref, dy_ref, bdr_ref, bdi_ref, cdr_ref, cdi_ref, ar_ref, ai_ref, d_ref, ir_ref, ii_ref,
             du_ref, dbdr_ref, dbdi_ref, dcdr_ref, dcdi_ref, dar_ref, dai_ref, dd_ref,
             sre, sim, gre, gim):
        ar = jnp.broadcast_to(ar_ref[...], (SUBLANES, sh))
        ai = jnp.broadcast_to(ai_ref[...], (SUBLANES, sh))
        i_r, i_i = ir_ref[...], ii_ref[...]

        def chunk(c):
            return pl.ds(pl.multiple_of(c * rc, rc), rc)

        def bu_chunk(c, _):
            ub = u_ref[chunk(c), :].astype(BF)
            dyb = dy_ref[chunk(c), :].astype(BF)
            sre[chunk(c), :] = _dot(ub, bdr_ref[...], NN)
            sim[chunk(c), :] = _dot(ub, bdi_ref[...], NN)
            gre[chunk(c), :] = _dot(dyb, cdr_ref[...], NT)
            gim[chunk(c), :] = _dot(dyb, cdi_ref[...], NT)
            return 0

        lax.fori_loop(0, R // rc, bu_chunk, 0)

        def read_s(t):
            return sre[_rows8(t), :], sim[_rows8(t), :]

        def read_g(t):
            return gre[_rows8(t), :], gim[_rows8(t), :]

        def both(i, c):
            s = _cmul_add(ar, ai, c[0], c[1], *read_s(i))
            sre[_rows8(i), :], sim[_rows8(i), :] = s
            return (*s, *_cmul_add(ar, -ai, c[2], c[3], *read_g(seg_len - 1 - i)))

        z = jnp.zeros((SUBLANES, sh), F32)
        _, _, fr, fi = _steps(seg_len, both, (i_r, i_i, z, z))

        def c_chunk(c, carry):
            dyb = dy_ref[chunk(c), :].astype(BF)
            return (carry[0] + _dot(sre[chunk(c), :], dyb, TN), carry[1] + _dot(sim[chunk(c), :], dyb, TN))

        zc = jnp.zeros((sh, LANES), F32)
        dcr, dci = lax.fori_loop(0, R // rc, c_chunk, (zc, zc))
        dcdr_ref[...] = dcr
        dcdi_ref[...] = dci
        pr, pi = _cpow(ar, ai, seg_len)
        last =(lax.broadcasted_iota(jnp.int32, (SUBLANES, sh), 0) & (nseg - 1)) == nseg - 1

        def nxt(x):
            return jnp.where(last, 0.0, pltpu.roll(x, SUBLANES - 1, 0))

        xr, xi = fr, fi
        for _ in range(nseg - 1):
            xr, xi = _cmul_add(pr, -pi, nxt(xr), nxt(xi), fr, fi)
        g0r, g0i = nxt(xr), nxt(xi)

        def adj_step(t, c, s_before):
            gr, gi = _cmul_add(ar, -ai, c[0], c[1], *read_g(t))
            gre[_rows8(t), :], gim[_rows8(t), :] = gr, gi
            spr, spi = s_before
            return gr, gi, c[2] + spr * gr + spi * gi, c[3] + spr * gi - spi * gr

        carry = _steps(seg_len - 1, lambda i, c: adj_step(seg_len - 1 - i, c, read_s(seg_len - 2 - i)),
                       (g0r, g0i, z, z))
        carry = adj_step(0, carry, (i_r, i_i))
        dar_ref[...] = jnp.sum(carry[2], axis=0, keepdims=True)
        dai_ref[...] = jnp.sum(carry[3], axis=0, keepdims=True)

        def d_chunk(c, carry):
            ub = u_ref[chunk(c), :].astype(BF)
            grb = gre[chunk(c), :].astype(BF)
            gib = gim[chunk(c), :].astype(BF)
            du = _dot(grb, bdr_ref[...], NT) + _dot(gib, bdi_ref[...], NT)
            du_ref[chunk(c), :] = du + d_ref[...] * dy_ref[chunk(c), :]
            dd = carry[2] + jnp.sum(dy_ref[chunk(c), :] * u_ref[chunk(c), :], axis=0, keepdims=True)
            return carry[0] + _dot(ub, grb, TN), carry[1] + _dot(ub, gib, TN), dd

        zb = jnp.zeros((LANES, sh), F32)
        dbr, dbi, dd = lax.fori_loop(0, R // rc, d_chunk, (zb, zb, jnp.zeros((1, LANES), F32)))
        dbdr_ref[...] = dbr
        dbdi_ref[...] = dbi
        dd_ref[...] = dd

    bd_sh = jax.ShapeDtypeStruct((nj, LANES, STATE_COLS), F32)
    cd_sh = jax.ShapeDtypeStruct((nj, STATE_COLS, LANES), F32)
    a_sh = jax.ShapeDtypeStruct((nj, 1, STATE_COLS), F32)
    return _call(
        body, "s5_bwd", (nj,),
        [sp["rows"], sp["rows"], sp["bd"], sp["bd"], sp["cd"], sp["cd"], sp["a"], sp["a"],
         sp["vec"], sp["init"], sp["init"]],
        [sp["rows"], sp["bd"], sp["bd"], sp["cd"], sp["cd"], sp["a"], sp["a"], sp["vec"]],
        [jax.ShapeDtypeStruct((R, D), F32), bd_sh, bd_sh, cd_sh, cd_sh, a_sh, a_sh,
         jax.ShapeDtypeStruct((1, D), F32)],
        (u, dy, bdr, bdi, cdr, cdi, are, aim, dsk, init_re, init_im),
        scratch=[pltpu.VMEM((R, sh), F32) for _ in range(4)], comm=comm)


def _glu_fwd(yg, wa, wb, h):
    T, D = yg.shape
    N = wa.shape[1]
    bm = min(FFN_ROW_TILE, T)
    bn = min(ROW_TILE, N)
    wspec = pl.BlockSpec((D, bn), lambda i, j, k: (0, j))
    ospec = pl.BlockSpec((bm, bn), lambda i, j, k: (i, j))

    def epi(accs, ex, outs):
        pa, pb = accs
        outs[0][...] = ex[0][...] + pa * _sig(pb)
        outs[1][...] = pa.astype(BF)
        outs[2][...] = pb.astype(BF)

    return _mm("glu_fwd", (T // bm, N // bn, 1),
               [(yg, pl.BlockSpec((bm, D), lambda i, j, k: (i, 0))), (wa, wspec), (wb, wspec)],
               [(0, 1, NN, 0), (0, 2, NN, 1)], 2, None, [(h, ospec)],
               [(jax.ShapeDtypeStruct((T, N), F32), ospec), (jax.ShapeDtypeStruct((T, N), BF), ospec),
                (jax.ShapeDtypeStruct((T, N), BF), ospec)], epi, nrow=max(1, bm // ROW_TILE))


def _glu_bwd_gates(dz, pa, pb):
    T, D = dz.shape
    bm = min(ROW_TILE, T)

    def body(dz_ref, pa_ref, pb_ref, dpa_ref, dpb_ref):
        dz = dz_ref[...]
        sg = _sig(pb_ref[...].astype(F32))
        dpa_ref[...] = (dz * sg).astype(BF)
        dpb_ref[...] = (dz * pa_ref[...].astype(F32) * sg * (1.0 - sg)).astype(BF)

    row = pl.BlockSpec((bm, D), lambda i: (i, 0))
    return pl.pallas_call(
        body, grid=(T // bm,), in_specs=[row] * 3, out_specs=[row] * 2,
        out_shape=[jax.ShapeDtypeStruct((T, D), BF)] * 2, name="glu_bwd_gates",
        compiler_params=_cparams(1))(dz, pa, pb)


def _glu_bwd_y(dpa, dpb, wa, wb, y_pre, comm=None):
    T, N = dpa.shape
    D = wa.shape[0]
    bm = min(FFN_ROW_TILE, T)
    bn = min(ROW_TILE, D)
    aspec = pl.BlockSpec((bm, N), lambda i, j, k: (i, 0))
    wspec = pl.BlockSpec((bn, N), lambda i, j, k: (j, 0))
    ospec = pl.BlockSpec((bm, bn), lambda i, j, k: (i, j))

    def epi(accs, ex, outs):
        outs[0][...] = accs[0] * _gelu_grad(ex[0][...])

    return _mm("glu_bwd_y", (T // bm, D // bn, 1), [(dpa, aspec), (wa, wspec), (dpb, aspec), (wb, wspec)],
               [(0, 1, NT, 0), (2, 3, NT, 0)], 1, None, [(y_pre, ospec)],
               [(jax.ShapeDtypeStruct((T, D), F32), ospec)], epi, comm=comm, nrow=max(1, bm // ROW_TILE))[0]


def _block_diag_in(x, nj):
    g = GROUPS_PER_BLOCK
    x = x.reshape(nj, g, 1, S5_GROUP, S5_STATE)
    eye = jnp.eye(g, dtype=bool)[None, :, :, None, None]
    full = jnp.where(eye, x, 0.0)
    return full.transpose(0, 1, 3, 2, 4).reshape(nj, g * S5_GROUP, g * S5_STATE)


def _block_diag_out(x, nj):
    return _block_diag_in(x, nj).transpose(0, 2, 1)


def _diag_of_in(m, nj):
    g = GROUPS_PER_BLOCK
    m5 = m.reshape(nj, g, S5_GROUP, g, S5_STATE)
    d = jnp.diagonal(m5, axis1=1, axis2=3)
    return d.transpose(0, 3, 1, 2).reshape(nj * g, S5_GROUP, S5_STATE)


def _mixer_s5_fwd(h, gain, p, dsk, wa, wb, nb, seq, comm_s5=None):
    T, D = h.shape
    nj = D // LANES
    nseg = SUBLANES // nb
    seg_len = seq // nseg
    G = p["s5_lambda_re"].shape[1]
    lam_re = p["s5_lambda_re"].reshape(G, 1, S5_STATE)
    lam_im = p["s5_lambda_im"].reshape(G, 1, S5_STATE)
    log_dt = p["s5_log_dt"].reshape(G, 1, 1)
    bt_re = p["s5_b_re"][0].transpose(0, 2, 1)
    bt_im = p["s5_b_im"][0].transpose(0, 2, 1)
    ar, ai, bbr, bbi = _s5_discretize(lam_re, lam_im, log_dt, bt_re, bt_im)
    mats = (_block_diag_in(bbr, nj).astype(BF), _block_diag_in(bbi, nj).astype(BF),
            _block_diag_out(p["s5_c_re"][0], nj).astype(BF),
            _block_diag_out(-p["s5_c_im"][0], nj).astype(BF),
            ar.reshape(nj, 1, STATE_COLS), ai.reshape(nj, 1, STATE_COLS), dsk)
    h_seg = _to_seg(h, seg_len)
    u = _rms_fwd(h_seg, gain, F32)
    y_pre, yg, init_re, init_im = _s5_fwd(u, mats, seg_len, nseg, comm=comm_s5)
    h_out, pa, pb = _glu_fwd(yg, wa, wb, h_seg)
    disc_in = (lam_re, lam_im, log_dt, bt_re, bt_im)
    return _to_tok(h_out, seg_len), (h_seg, u, mats, y_pre, yg, init_re, init_im, pa, pb, disc_in, seg_len, nseg)


def _mixer_s5_bwd(dh, saved, gain, wa, wb, reduce_start, carry):
    h_seg, u, mats, y_pre, yg, init_re, init_im, pa, pb, disc_in, seg_len, nseg = saved
    T, D = h_seg.shape
    nj = D // LANES
    G = nj * GROUPS_PER_BLOCK
    dh_seg = _to_seg(dh, seg_len)
    dpa, dpb = _glu_bwd_gates(dh_seg, pa, pb)
    dy = _glu_bwd_y(dpa, dpb, wa, wb, y_pre)
    dwa, dwb = _mm_tn("glu_dw", yg, [dpa, dpb])
    comm = _merge_comms(reduce_start(["s5_glu_wa", "s5_glu_wb"],
                                     [dwa.reshape(N_DEV, -1, D), dwb.reshape(N_DEV, -1, D)]) + [carry])
    du, dbdr, dbdi, dcdr, dcdi, dar, dai, dd = _s5_bwd(u, dy, mats, init_re, init_im, seg_len, nseg, comm=comm)
    d_bbr = _diag_of_in(dbdr, nj)
    d_bbi = _diag_of_in(dbdi, nj)
    d_c_re = _diag_of_in(dcdr.transpose(0, 2, 1), nj)
    d_c_im = -_diag_of_in(dcdi.transpose(0, 2, 1), nj)
    dlr, dli, dld, dbr, dbi = _s5_discretize_bwd(
        *disc_in, dar.reshape(G, 1, S5_STATE), dai.reshape(G, 1, S5_STATE), d_bbr, d_bbi)
    small = {"s5_lambda_re": dlr.reshape(1, G, S5_STATE), "s5_lambda_im": dli.reshape(1, G, S5_STATE),
             "s5_log_dt": dld.reshape(1, G),
             "s5_b_re": dbr.transpose(0, 2, 1)[None], "s5_b_im": dbi.transpose(0, 2, 1)[None],
             "s5_c_re": d_c_re[None], "s5_c_im": d_c_im[None], "s5_d": dd}
    dh_in, _, dgain = _rms_bwd(du, h_seg, gain, dh_seg)
    dh_in = _to_tok(dh_in, seg_len)
    return dh_in, dh_in.astype(BF), dgain, small


def _mesh_pos():
    return lax.axis_index("x"), lax.axis_index("y"), lax.axis_index("c")


class _Gather:
    def __init__(self, srcs, slots, send_sems, recv_sems):
        self.srcs, self.slots, self.send_sems, self.recv_sems = srcs, slots, send_sems, recv_sems
        x, y, c = _mesh_pos()
        self.c = c
        self.me, self.sib = (x, y, c), (x, y, 1 - c)
        self.chips = [(1 - x, y), (x, 1 - y), (1 - x, 1 - y)]

    def copy(self, a, k, block, to, own=False):
        dst = self.slots[a].at[4 * block[0] + 2 * block[1] + block[2]]
        return pltpu.make_async_remote_copy(
            src_ref=self.srcs[a] if own else dst, dst_ref=dst, send_sem=self.send_sems.at[7 * a + k],
            recv_sem=self.recv_sems.at[7 * a + k], device_id=to, device_id_type=MESH)

    def own_copies(self, a):
        cps = [self.copy(a, 0, self.me, self.sib, own=True)]
        return cps + [self.copy(a, 1 + j, self.me, (*chip, self.c), own=True) for j, chip in enumerate(self.chips)]

    def start(self):
        for a in range(len(self.srcs)):
            for cp in self.own_copies(a):
                cp.start()

    def finish(self):
        n = len(self.srcs)
        for a in range(n):
            for j, chip in enumerate(self.chips):
                self.copy(a, 1 + j, (*chip, self.c), self.me).wait_recv()
                self.copy(a, 4 + j, (*chip, self.c), self.sib).start()
        for a in range(n):
            self.copy(a, 0, self.sib, self.me).wait_recv()
            for j, chip in enumerate(self.chips):
                self.copy(a, 4 + j, (*chip, 1 - self.c), self.me).wait_recv()
        for a in range(n):
            for cp in self.own_copies(a):
                cp.wait_send()
            for j, chip in enumerate(self.chips):
                self.copy(a, 4 + j, (*chip, self.c), self.sib).wait_send()


def _gather_comm(arrs):
    n = len(arrs)

    def local(xs, outs, sems, a):
        x, y, c = _mesh_pos()
        return pltpu.make_async_copy(xs[a], outs[a].at[4 * x + 2 * y + c], sems[2].at[a])

    def start(xs, outs, sems):
        for a in range(n):
            local(xs, outs, sems, a).start()
        _Gather(xs, outs, sems[0], sems[1]).start()

    def finish(xs, outs, sems):
        _Gather(xs, outs, sems[0], sems[1]).finish()
        for a in range(n):
            local(xs, outs, sems, a).wait()

    return _Comm(list(arrs), [jax.ShapeDtypeStruct((N_DEV,) + a.shape, a.dtype) for a in arrs],
                 [pltpu.SemaphoreType.DMA((7 * n,)), pltpu.SemaphoreType.DMA((7 * n,)),
                  pltpu.SemaphoreType.DMA((n,))], start, finish)


def _exchange_comm(parts):
    n = len(parts)

    def copies(ps, outs, sems):
        x, y, c = _mesh_pos()
        cps = []
        for a in range(n):
            for j in range(1, 4):
                to = (jnp.bitwise_xor(x, j // 2), jnp.bitwise_xor(y, j % 2), c)
                cps.append(pltpu.make_async_remote_copy(
                    src_ref=ps[a].at[j], dst_ref=outs[a].at[j - 1], send_sem=sems[0].at[3 * a + j - 1],
                    recv_sem=sems[1].at[3 * a + j - 1], device_id=to, device_id_type=MESH))
        return cps

    def start(ps, outs, sems):
        for cp in copies(ps, outs, sems):
            cp.start()

    def finish(ps, outs, sems):
        for cp in copies(ps, outs, sems):
            cp.wait()

    return _Comm(list(parts), [jax.ShapeDtypeStruct((3,) + p.shape[1:], p.dtype) for p in parts],
                 [pltpu.SemaphoreType.DMA((3 * n,)), pltpu.SemaphoreType.DMA((3 * n,))], start, finish)


def _run_comm(comm, name):
    ci, co = len(comm.ins), len(comm.outs)

    def body(*refs):
        comm.start(refs[:ci], refs[ci:ci + co], refs[ci + co:])
        comm.finish(refs[:ci], refs[ci:ci + co], refs[ci + co:])

    any_spec = pl.BlockSpec(memory_space=pl.ANY)
    comm.set_results(pl.pallas_call(
        body, in_specs=[any_spec] * ci, out_specs=[any_spec] * co, out_shape=list(comm.outs),
        scratch_shapes=list(comm.sems), name=name, compiler_params=_cparams(0))(*comm.ins))


def _pair_comm(grads):
    n = len(grads)

    def copies(gs, outs, sems):
        x, y, c = _mesh_pos()
        cps = []
        for a in range(n):
            for k in range(4):
                cps.append(pltpu.make_async_remote_copy(
                    src_ref=gs[a].at[2 * k + 1 - c], dst_ref=outs[a].at[k], send_sem=sems[0].at[4 * a + k],
                    recv_sem=sems[1].at[4 * a + k], device_id=(x, y, 1 - c), device_id_type=MESH))
        return cps

    def start(gs, outs, sems):
        for cp in copies(gs, outs, sems):
            cp.start()

    def finish(gs, outs, sems):
        for cp in copies(gs, outs, sems):
            cp.wait()

    return _Comm(list(grads), [jax.ShapeDtypeStruct((4,) + g.shape[1:], g.dtype) for g in grads],
                 [pltpu.SemaphoreType.DMA((4 * n,)), pltpu.SemaphoreType.DMA((4 * n,))], start, finish)


def _pair_exchange(grads, name):
    comm = _pair_comm(grads)
    _run_comm(comm, name)
    return comm.results


def _pair_sum(grad, recv, pos):
    _, R, C = grad.shape
    br = _row_block(R, C, PAIR_SUM_ELEMS)

    def body(pos_ref, g_ref, r_ref, o_ref):
        o_ref[...] = (g_ref[...].astype(F32) + r_ref[...].astype(F32)).astype(BF)

    def chip(j, p):
        return jnp.bitwise_xor(p[1], j)

    return pl.pallas_call(
        body, grid_spec=pltpu.PrefetchScalarGridSpec(
            num_scalar_prefetch=1, grid=(4, R // br),
            in_specs=[pl.BlockSpec((None, br, C), lambda j, i, p: (2 * chip(j, p) + p[0], i, 0)),
                      pl.BlockSpec((None, br, C), lambda j, i, p: (chip(j, p), i, 0))],
            out_specs=pl.BlockSpec((None, br, C), lambda j, i, p: (j, i, 0))),
        out_shape=jax.ShapeDtypeStruct((4, R, C), BF), name="pair_sum", compiler_params=_cparams(2))(pos, grad, recv)


def _adamw(w, g, m, v):
    m = ADAM_B1 * m + (1.0 - ADAM_B1) * g
    v = ADAM_B2 * v + (1.0 - ADAM_B2) * (g * g)
    m_hat = m / (1.0 - ADAM_B1 ** ADAM_STEP)
    v_hat = v / (1.0 - ADAM_B2 ** ADAM_STEP)
    return -ADAM_LR * (m_hat / (jnp.sqrt(v_hat) + ADAM_EPS) + ADAM_WD * w), m, v


def _adamw_piece(w, m, v, piece, part, recv, bufs):
    _, R, C = w.shape
    br = _row_block(R, C)

    def body(w_ref, m_ref, v_ref, p_ref, r_ref, b0, b1, b2, b3, g_ref, d_ref, nm_ref, nv_ref):
        g = p_ref[...].astype(F32)
        for j in range(3):
            g = g + r_ref[j].astype(F32)
        d, nm, nv = _adamw(w_ref[...], g, m_ref[...], v_ref[...])
        g_ref[...] = g
        d_ref[...] = d
        nm_ref[...] = nm
        nv_ref[...] = nv

    row = pl.BlockSpec((None, br, C), lambda i: (piece, i, 0))
    any_spec = pl.BlockSpec(memory_space=pl.ANY)
    return pl.pallas_call(
        body, grid=(R // br,),
        in_specs=[row, row, row, pl.BlockSpec((None, br, C), lambda i: (0, i, 0)),
                  pl.BlockSpec((3, br, C), lambda i: (0, i, 0))] + [any_spec] * 4,
        out_specs=[row] * 4, out_shape=[jax.ShapeDtypeStruct(w.shape, F32)] * 4,
        input_output_aliases={5: 0, 6: 1, 7: 2, 8: 3}, name="adamw_piece",
        compiler_params=_cparams(1))(w, m, v, part, recv, *bufs)


def _all_reduce_small(x):
    rows = x.shape[0]

    def body(x_ref, o_ref, buf, send_sems, recv_sems):
        xp, yp, cp = _mesh_pos()
        buf[4 * xp + 2 * yp + cp] = x_ref[...]
        gather = _Gather([x_ref], [buf], send_sems, recv_sems)
        gather.start()
        gather.finish()
        acc = buf[0]
        for d in range(1, N_DEV):
            acc = acc + buf[d]
        o_ref[...] = acc

    vm = pl.BlockSpec(memory_space=pltpu.VMEM)
    return pl.pallas_call(
        body, in_specs=[vm], out_specs=vm, out_shape=jax.ShapeDtypeStruct(x.shape, F32),
        scratch_shapes=[pltpu.VMEM((N_DEV, rows, LANES), F32), pltpu.SemaphoreType.DMA((7,)),
                        pltpu.SemaphoreType.DMA((7,))],
        name="all_reduce_small", compiler_params=_cparams(0))(x)


def _sum_slots(x):
    def body(x_ref, o_ref):
        acc = x_ref[0]
        for d in range(1, N_DEV):
            acc = acc + x_ref[d]
        o_ref[...] = acc

    return pl.pallas_call(body, out_shape=jax.ShapeDtypeStruct(x.shape[1:], F32), name="sum_slots",
                          compiler_params=_cparams(0))(x)


def _adamw_small(w, g, m, v):
    def body(w_ref, g_ref, m_ref, v_ref, d_ref, nm_ref, nv_ref):
        d, nm, nv = _adamw(w_ref[...], g_ref[...], m_ref[...], v_ref[...])
        d_ref[...] = d
        nm_ref[...] = nm
        nv_ref[...] = nv

    sh = jax.ShapeDtypeStruct(w.shape, F32)
    return pl.pallas_call(body, out_shape=[sh] * 3, name="adamw_small", compiler_params=_cparams(0))(w, g, m, v)


def _pack(arrs):
    flat = jnp.concatenate([a.reshape(-1).astype(F32) for a in arrs])
    rows = -(-flat.shape[0] // (SUBLANES * LANES)) * SUBLANES
    return jnp.pad(flat, (0, rows * LANES - flat.shape[0])).reshape(rows, LANES)


def _unpack(buf, shapes):
    flat = buf.reshape(-1)
    out, off = [], 0
    for s in shapes:
        n = 1
        for d in s:
            n *= d
        out.append(flat[off:off + n].reshape(s))
        off += n
    return out


BIG = ("ffn_w1", "ffn_w3", "ffn_w2", "ab_w_in", "ab_w_out", "s5_glu_wa", "s5_glu_wb")
NAMES = ("ln_ffn_pre", "ln_mix", "ln_ffn_post", "ln_final", "ffn_w1", "ffn_w3", "ffn_w2", "ab_w_in",
         "ab_conv_w", "ab_w_out", "s5_lambda_re", "s5_lambda_im", "s5_log_dt", "s5_b_re", "s5_b_im",
         "s5_c_re", "s5_c_im", "s5_d", "s5_glu_wa", "s5_glu_wb")


def kernel(x, ln_ffn_pre, ln_mix, ln_ffn_post, ln_final, ffn_w1, ffn_w3, ffn_w2, ab_w_in, ab_conv_w, ab_w_out, s5_lambda_re, s5_lambda_im, s5_log_dt, s5_b_re, s5_b_im, s5_c_re, s5_c_im, s5_d, s5_glu_wa, s5_glu_wb, loss_target, m_ln_ffn_pre, m_ln_mix, m_ln_ffn_post, m_ln_final, m_ffn_w1, m_ffn_w3, m_ffn_w2, m_ab_w_in, m_ab_conv_w, m_ab_w_out, m_s5_lambda_re, m_s5_lambda_im, m_s5_log_dt, m_s5_b_re, m_s5_b_im, m_s5_c_re, m_s5_c_im, m_s5_d, m_s5_glu_wa, m_s5_glu_wb, v_ln_ffn_pre, v_ln_mix, v_ln_ffn_post, v_ln_final, v_ffn_w1, v_ffn_w3, v_ffn_w2, v_ab_w_in, v_ab_conv_w, v_ab_w_out, v_s5_lambda_re, v_s5_lambda_im, v_s5_log_dt, v_s5_b_re, v_s5_b_im, v_s5_c_re, v_s5_c_im, v_s5_d, v_s5_glu_wa, v_s5_glu_wb):
    w = dict(zip(NAMES, (ln_ffn_pre, ln_mix, ln_ffn_post, ln_final, ffn_w1, ffn_w3, ffn_w2, ab_w_in, ab_conv_w,
                         ab_w_out, s5_lambda_re, s5_lambda_im, s5_log_dt, s5_b_re, s5_b_im, s5_c_re, s5_c_im,
                         s5_d, s5_glu_wa, s5_glu_wb)))
    mom = dict(zip(NAMES, (m_ln_ffn_pre, m_ln_mix, m_ln_ffn_post, m_ln_final, m_ffn_w1, m_ffn_w3, m_ffn_w2,
                           m_ab_w_in, m_ab_conv_w, m_ab_w_out, m_s5_lambda_re, m_s5_lambda_im, m_s5_log_dt,
                           m_s5_b_re, m_s5_b_im, m_s5_c_re, m_s5_c_im, m_s5_d, m_s5_glu_wa, m_s5_glu_wb)))
    var = dict(zip(NAMES, (v_ln_ffn_pre, v_ln_mix, v_ln_ffn_post, v_ln_final, v_ffn_w1, v_ffn_w3, v_ffn_w2,
                           v_ab_w_in, v_ab_conv_w, v_ab_w_out, v_s5_lambda_re, v_s5_lambda_im, v_s5_log_dt,
                           v_s5_b_re, v_s5_b_im, v_s5_c_re, v_s5_c_im, v_s5_d, v_s5_glu_wa, v_s5_glu_wb)))
    nb, seq, D = x.shape
    T = nb * seq
    assert ln_mix.shape[0] == 2 and ab_w_in.shape[0] == 1 and s5_glu_wa.shape[0] == 1
    xc, yc, cc = _mesh_pos()
    dev = 4 * xc + 2 * yc + cc
    pos = jnp.stack([cc, 2 * xc + yc]).astype(jnp.int32)
    bq = min(ATTN_TILE, seq)
    tabs =_rope_tables(seq) + (_branch_bias(seq // bq, bq),)

    def ffn_piece(k, li, fj):
        return w[k][li, fj].astype(BF)

    g0 = _gather_comm([ffn_piece("ffn_w1", 0, 0), ffn_piece("ffn_w3", 0, 0), ab_conv_w[0], s5_d])
    _run_comm(g0, "gather_first")
    w1, w3 = {(0, 0): g0.results[0]}, {(0, 0): g0.results[1]}
    w2 = {}
    conv_w = g0.results[2].transpose(1, 0, 2).reshape(3, -1)
    dsk = g0.results[3].reshape(1, D)
    gains = {k: [w[k][i:i + 1] for i in range(2)] for k in ("ln_ffn_pre", "ln_mix", "ln_ffn_post")}

    h = x.reshape(T, D)
    saved = {}

    def ffn_fwd(h, gain, key, tag, comm_up, comm_down, after_up):
        n = _rms_fwd(h, gain, BF)
        t1, t3, g = _ffn_up(n, w1[key], w3[key], comm=comm_up)
        after_up()
        saved[tag] = (h, n, t1, t3, g)
        return _ffn_down(g, w2[key], h, comm=comm_down)

    c_up = _gather_comm([ffn_piece("ffn_w2", 0, 0), ab_w_out[0].astype(BF)])
    c_dn = _gather_comm([ab_w_in[0].astype(BF)])
    h = ffn_fwd(h, gains["ln_ffn_pre"][0], (0, 0), "pre0", c_up, c_dn,
                lambda: w2.update({(0, 0): c_up.results[0]}))
    wout = c_up.results[1].reshape(-1, D)
    wing = c_dn.results[0]
    c_proj = _gather_comm([ffn_piece("ffn_w1", 0, 1)])
    c_attn = _gather_comm([ffn_piece("ffn_w3", 0, 1), s5_glu_wa[0].astype(BF)])
    c_out = _gather_comm([s5_glu_wb[0].astype(BF)])
    h, saved["mix0"] = _mixer_ab_fwd(h, gains["ln_mix"][0], wing, conv_w, wout, tabs, nb, seq, c_proj, c_attn, c_out)
    w1[(0, 1)] = c_proj.results[0]
    w3[(0, 1)] = c_attn.results[0]
    wa = c_attn.results[1].reshape(-1, D)
    wb = c_out.results[0].reshape(-1, D)
    c_up2 = _gather_comm([ffn_piece("ffn_w2", 0, 1), ffn_piece("ffn_w1", 1, 0)])
    c_dn = _gather_comm([ffn_piece("ffn_w3", 1, 0)])
    h = ffn_fwd(h, gains["ln_ffn_post"][0], (0, 1), "post0", c_up2, c_dn,
                lambda: w2.update({(0, 1): c_up2.results[0]}))
    w1[(1, 0)] = c_up2.results[1]
    w3[(1, 0)] = c_dn.results[0]
    c_up3 = _gather_comm([ffn_piece("ffn_w2", 1, 0), ffn_piece("ffn_w1", 1, 1)])
    c_dn = _gather_comm([ffn_piece("ffn_w3", 1, 1)])
    h = ffn_fwd(h, gains["ln_ffn_pre"][1], (1, 0), "pre1", c_up3, c_dn,
                lambda: w2.update({(1, 0): c_up3.results[0]}))
    w1[(1, 1)] = c_up3.results[1]
    w3[(1, 1)] = c_dn.results[0]
    c_s5 = _gather_comm([ffn_piece("ffn_w2", 1, 1)])
    h, saved["mix1"] = _mixer_s5_fwd(h, gains["ln_mix"][1], w, dsk, wa, wb, nb, seq, c_s5)
    w2[(1, 1)] = c_s5.results[0]
    h = ffn_fwd(h, gains["ln_ffn_post"][1], (1, 1), "post1", None, None, lambda: None)
    dh, dhb, d_ln_final, loss_part = _loss_head(h, ln_final.reshape(1, D), loss_target.reshape(T, D))
    loss = lax.psum(loss_part[0, 0], ("x", "y", "c"))

    reduced = {}

    def reduce_start(names, grads, recv=None):
        recv = _pair_exchange(grads, "pair_exchange") if recv is None else recv
        comms = []
        for nm, g, r in zip(names, grads, recv):
            part = _pair_sum(g, r, pos)
            comms.append(_exchange_comm([part]))
            reduced[nm] = (part, comms[-1])
        return comms

    def ffn_bwd(dh, dhb, key, tag, gain, carry, is_last=False, comm_dw2=None):
        h_in, n, t1, t3, g = saved[tag]
        da1, da3 = _ffn_bwd_hidden(dhb, w2[key], t1, t3, comm=carry)
        dw2 = _ffn_dw2(g, dhb, comm=comm_dw2)
        if is_last:
            c2, = reduce_start([("ffn_w2",) + key], [dw2])
            dw1, dw3 = _ffn_dw13(n, da1, da3, comm=c2)
            c1, c3 = reduce_start([("ffn_w1",) + key, ("ffn_w3",) + key], [dw1, dw3])
            hosted = [c1, c3]
        else:
            pair2 = _pair_comm([dw2])
            dw1, dw3 = _ffn_dw13(n, da1, da3, comm=pair2)
            c2, = reduce_start([("ffn_w2",) + key], [dw2], recv=pair2.results)
            c1, c3 = reduce_start([("ffn_w1",) + key, ("ffn_w3",) + key], [dw1, dw3])
            hosted = [c2, c1]
        res = _ffn_dn_rms(da1, da3, w1[key], w3[key], h_in, gain, dh, comm=_merge_comms(hosted))
        return list(res) + [None if is_last else c3]

    g_small = {"ln_final": d_ln_final.reshape(D)}
    g_ln = {k: [None, None] for k in gains}
    dh, dhb, g_ln["ln_ffn_post"][1], carry = ffn_bwd(dh, dhb, (1, 1), "post1", gains["ln_ffn_post"][1], None)
    dh, dhb, g_ln["ln_mix"][1], s5_small = _mixer_s5_bwd(
        dh, saved["mix1"], gains["ln_mix"][1], wa, wb, reduce_start, carry)
    s5_names = list(s5_small)
    c_s5_grads = _gather_comm([_pack([s5_small[k] for k in s5_names])])
    dh, dhb, g_ln["ln_ffn_pre"][1], carry = ffn_bwd(dh, dhb, (1, 0), "pre1", gains["ln_ffn_pre"][1], None,
                                                    comm_dw2=c_s5_grads)
    g_red = dict(zip(s5_names, _unpack(_sum_slots(c_s5_grads.results[0]), [s5_small[k].shape for k in s5_names])))
    dh, dhb, g_ln["ln_ffn_post"][0], carry = ffn_bwd(dh, dhb, (0, 1), "post0", gains["ln_ffn_post"][0], carry)
    dh, dhb, g_ln["ln_mix"][0], g_small["ab_conv_w"], carry = _mixer_ab_bwd(
        dh, dhb, saved["mix0"], gains["ln_mix"][0], wing, conv_w, wout, tabs, nb, seq, reduce_start, carry)
    dh, dhb, g_ln["ln_ffn_pre"][0], _ = ffn_bwd(dh, dhb, (0, 0), "pre0", gains["ln_ffn_pre"][0], carry, is_last=True)
    grad_x = dh.reshape(nb, seq, D)
    for k in g_ln:
        g_small[k] = jnp.concatenate(g_ln[k], axis=0)

    out = {}
    for k in BIG:
        transposed = k in ("ffn_w1", "ffn_w3")
        pieces = [(li, fj) for li in range(2) for fj in range(2)] if w[k].ndim == 4 else [None]

        def view(a):
            a = a.swapaxes(-1, -2) if transposed else a
            return a.reshape(len(pieces), -1, a.shape[-1])

        w3d, m3d, v3d = view(w[k]), view(mom[k]), view(var[k])
        bufs = [lax.empty(w3d.shape, F32) for _ in range(4)]
        for q, key in enumerate(pieces):
            part, comm = reduced[k if key is None else (k,) + key]
            bufs = _adamw_piece(w3d, m3d, v3d, q, part, comm.results[0], bufs)
        if transposed:
            out[k] = [t.reshape(w[k].shape[:2] + w3d.shape[1:]).swapaxes(-1, -2) for t in bufs]
        else:
            out[k] = [t.reshape(w[k].shape) for t in bufs]

    small_names = [k for k in NAMES if k not in BIG]
    late_names = [k for k in small_names if k not in g_red]
    g_red.update(zip(late_names, _unpack(_all_reduce_small(_pack([g_small[k] for k in late_names])),
                                         [g_small[k].shape for k in late_names])))
    cw = w["ab_conv_w"].shape[-1]
    g_red["ab_conv_w"] = lax.dynamic_slice_in_dim(g_red["ab_conv_w"], dev * cw, cw, axis=1)[None]
    dsz = w["s5_d"].shape[-1]
    g_red["s5_d"] = lax.dynamic_slice_in_dim(g_red["s5_d"].reshape(1, -1), dev * dsz, dsz, axis=1)
    shapes = [w[k].shape for k in small_names]
    g_red = {k: g_red[k].reshape(w[k].shape) for k in small_names}
    d_s, m_s, v_s = _adamw_small(_pack([w[k] for k in small_names]), _pack([g_red[k] for k in small_names]),
                                 _pack([mom[k] for k in small_names]), _pack([var[k] for k in small_names]))
    for k, d, nm, nv in zip(small_names, _unpack(d_s, shapes), _unpack(m_s, shapes), _unpack(v_s, shapes)):
        out[k] = [g_red[k], d, nm, nv]

    return (loss, grad_x, *[out[k][0] for k in NAMES], *[out[k][1] for k in NAMES],
            *[out[k][2] for k in NAMES], *[out[k][3] for k in NAMES])
```
